```python
import math
import jax, jax.numpy as jnp
from jax import lax
import numpy as np

D_MODEL = 2048
BATCH = 8
SEQ = 4096
DEPTH = 1

MIX_WIDTH = D_MODEL
SSM_WIDTH = MIX_WIDTH // 2
ATTN_WIDTH = MIX_WIDTH - SSM_WIDTH
SSM_GROUP = 16
SSM_GROUPS = SSM_WIDTH // SSM_GROUP
SSM_STATE = 64
HEAD_DIM = 64
N_HEADS = ATTN_WIDTH // HEAD_DIM
Q_BLOCK = 128
D_FF = ((8 * D_MODEL // 3 + 255) // 256) * 256
N_MOD = 6
EPS = 1e-6
DT_MIN = 1e-3
DT_MAX = 1e-1
IN_COLS = SSM_WIDTH + 3 * ATTN_WIDTH

kernel_name = "hybrid_s5_stickbreaking_adaln_block"


def rmsnorm(x, gain):
    xf = x.astype(jnp.float32)
    xf = xf * lax.rsqrt(jnp.mean(xf * xf, axis=-1, keepdims=True) + EPS)
    return (xf * gain.astype(jnp.float32)).astype(x.dtype)


def s5_mixer(u, a_re, a_im, log_dt, b_re, b_im, c_re, c_im, d_skip, w_glu):
    bsz, seqlen, _ = u.shape
    f32 = jnp.float32
    uf = u.astype(f32).reshape(bsz, seqlen, SSM_GROUPS, SSM_GROUP)
    lam = lax.complex(a_re.astype(f32), a_im.astype(f32))
    dt = jnp.exp(log_dt.astype(f32))[:, None]
    lam_bar = jnp.exp(lam * dt)
    b = lax.complex(b_re.astype(f32), b_im.astype(f32))
    b_bar = ((lam_bar - 1.0) / lam)[..., None] * b
    bu = jnp.einsum('blgh,gph->blgp', uf.astype(jnp.complex64), b_bar)
    a = jnp.broadcast_to(lam_bar, bu.shape)

    def combine(left, right):
        a_l, b_l = left
        a_r, b_r = right
        return a_r * a_l, a_r * b_l + b_r

    _, states = lax.associative_scan(combine, (a, bu), axis=1)
    c = lax.complex(c_re.astype(f32), c_im.astype(f32))
    y = jnp.einsum('ghp,blgp->blgh', c, states).real + d_skip.astype(f32) * uf
    y = jax.nn.gelu(y.reshape(bsz, seqlen, SSM_WIDTH))
    y = y * jax.nn.sigmoid(y @ w_glu.astype(f32))
    return y.astype(u.dtype)


def stick_breaking_attention(q, k, v):
    bsz, seqlen, nh, dh = q.shape
    nblk = seqlen // Q_BLOCK
    f32 = jnp.float32
    qf = q.astype(f32) * (1.0 / math.sqrt(dh))
    kf = k.astype(f32)
    vf = v.astype(f32)
    qb = qf.reshape(bsz, nblk, Q_BLOCK, nh, dh).transpose(1, 0, 3, 2, 4)
    key_pos = jnp.arange(seqlen)

    def one_block(args):
        q_blk, blk_idx = args
        q_pos = blk_idx * Q_BLOCK + jnp.arange(Q_BLOCK)
        z = jnp.einsum('bhqd,bkhd->bhqk', q_blk, kf)
        past = key_pos[None, :] < q_pos[:, None]
        log_keep = jnp.where(past, jax.nn.log_sigmoid(-z), 0.0)
        after = lax.cumsum(log_keep, axis=3, reverse=True) - log_keep
        log_w = jax.nn.log_sigmoid(z) + after
        w = jnp.where(past, jnp.exp(log_w), 0.0)
        return jnp.einsum('bhqk,bkhd->bhqd', w, vf)

    out = lax.map(one_block, (qb, jnp.arange(nblk)))
    out = out.transpose(1, 0, 3, 2, 4).reshape(bsz, seqlen, nh * dh)
    return out.astype(q.dtype)


def _fwd_setup_inputs(seed: int = 0) -> dict:
    key = jax.random.key(seed)
    ks = jax.random.split(key, 24)
    f32 = jnp.float32
    nrm = lambda k, shape, s: (jax.random.normal(k, shape, f32) * s)
    x = jax.random.normal(ks[0], (BATCH, SEQ, D_MODEL), f32)
    c = jax.random.normal(ks[1], (BATCH, D_MODEL), f32)
    w_ada = nrm(ks[2], (DEPTH, D_MODEL, N_MOD * D_MODEL), D_MODEL ** -0.5)
    b_ada = nrm(ks[3], (DEPTH, N_MOD * D_MODEL), 0.01)
    g_mix = 1.0 + nrm(ks[4], (DEPTH, D_MODEL), 0.02)
    w_in = nrm(ks[5], (DEPTH, D_MODEL, IN_COLS), D_MODEL ** -0.5)
    a_re = -0.5 + nrm(ks[6], (DEPTH, SSM_GROUPS, SSM_STATE), 0.01)
    a_im = (math.pi * jnp.arange(SSM_STATE, dtype=f32))[None, None, :] + nrm(ks[7], (DEPTH, SSM_GROUPS, SSM_STATE), 0.01)
    log_dt = jax.random.uniform(ks[8], (DEPTH, SSM_GROUPS), f32, math.log(DT_MIN), math.log(DT_MAX))
    b_re = nrm(ks[9], (DEPTH, SSM_GROUPS, SSM_STATE, SSM_GROUP), (2 * SSM_GROUP) ** -0.5)
    b_im = nrm(ks[10], (DEPTH, SSM_GROUPS, SSM_STATE, SSM_GROUP), (2 * SSM_GROUP) ** -0.5)
    c_re = nrm(ks[11], (DEPTH, SSM_GROUPS, SSM_GROUP, SSM_STATE), SSM_STATE ** -0.5)
    c_im = nrm(ks[12], (DEPTH, SSM_GROUPS, SSM_GROUP, SSM_STATE), SSM_STATE ** -0.5)
    d_skip = nrm(ks[13], (DEPTH, SSM_GROUPS, SSM_GROUP), 1.0)
    w_glu = nrm(ks[14], (DEPTH, SSM_WIDTH, SSM_WIDTH), SSM_WIDTH ** -0.5)
    q_gain = 1.0 + nrm(ks[15], (DEPTH, HEAD_DIM), 0.02)
    k_gain = 1.0 + nrm(ks[16], (DEPTH, HEAD_DIM), 0.02)
    g_ssm_out = 1.0 + nrm(ks[17], (DEPTH, SSM_WIDTH), 0.02)
    g_attn_out = 1.0 + nrm(ks[18], (DEPTH, ATTN_WIDTH), 0.02)
    w_out = nrm(ks[19], (DEPTH, MIX_WIDTH, D_MODEL), MIX_WIDTH ** -0.5)
    g_ffn = 1.0 + nrm(ks[20], (DEPTH, D_MODEL), 0.02)
    w_gate = nrm(ks[21], (DEPTH, D_MODEL, D_FF), D_MODEL ** -0.5)
    w_up = nrm(ks[22], (DEPTH, D_MODEL, D_FF), D_MODEL ** -0.5)
    w_down = nrm(ks[23], (DEPTH, D_FF, D_MODEL), D_FF ** -0.5)
    return {"x": x, "c": c, "w_ada": w_ada, "b_ada": b_ada, "g_mix": g_mix, "w_in": w_in,
            "a_re": a_re, "a_im": a_im, "log_dt": log_dt, "b_re": b_re, "b_im": b_im,
            "c_re": c_re, "c_im": c_im, "d_skip": d_skip, "w_glu": w_glu,
            "q_gain": q_gain, "k_gain": k_gain, "g_ssm_out": g_ssm_out, "g_attn_out": g_attn_out,
            "w_out": w_out, "g_ffn": g_ffn, "w_gate": w_gate, "w_up": w_up, "w_down": w_down}


def _fwd_reference(x, c, w_ada, b_ada, g_mix, w_in, a_re, a_im, log_dt, b_re, b_im, c_re, c_im,
              d_skip, w_glu, q_gain, k_gain, g_ssm_out, g_attn_out, w_out, g_ffn, w_gate,
              w_up, w_down):
    bsz, seqlen, _ = x.shape
    h = x
    cond = jax.nn.silu(c.astype(jnp.float32))
    for layer in range(DEPTH):
        mod = (cond @ w_ada[layer].astype(jnp.float32) + b_ada[layer].astype(jnp.float32)).astype(x.dtype)
        shift_m, scale_m, gate_m, shift_f, scale_f, gate_f = jnp.split(mod[:, None, :], N_MOD, axis=-1)

        xm = rmsnorm(h, g_mix[layer]) * (1.0 + scale_m) + shift_m
        proj = xm @ w_in[layer]
        u, q, k, v = jnp.split(proj, [SSM_WIDTH, SSM_WIDTH + ATTN_WIDTH, SSM_WIDTH + 2 * ATTN_WIDTH], axis=-1)
        y_ssm = s5_mixer(u, a_re[layer], a_im[layer], log_dt[layer], b_re[layer], b_im[layer],
                         c_re[layer], c_im[layer], d_skip[layer], w_glu[layer])
        q = rmsnorm(q.reshape(bsz, seqlen, N_HEADS, HEAD_DIM), q_gain[layer])
        k = rmsnorm(k.reshape(bsz, seqlen, N_HEADS, HEAD_DIM), k_gain[layer])
        v = v.reshape(bsz, seqlen, N_HEADS, HEAD_DIM)
        y_attn = stick_breaking_attention(q, k, v)
        mixed = jnp.concatenate([rmsnorm(y_ssm, g_ssm_out[layer]),
                                 rmsnorm(y_attn, g_attn_out[layer])], axis=-1)
        h = h + gate_m * (mixed @ w_out[layer])

        xf = rmsnorm(h, g_ffn[layer]) * (1.0 + scale_f) + shift_f
        ffn = (jax.nn.silu(xf @ w_gate[layer]) * (xf @ w_up[layer])) @ w_down[layer]
        h = h + gate_f * ffn
    return h


import jax as _jax
import jax.numpy as _jnp

TWIN_FORMAT = 'train_step'
FWD_PARAMS = ['x', 'c', 'w_ada', 'b_ada', 'g_mix', 'w_in', 'a_re', 'a_im', 'log_dt', 'b_re', 'b_im', 'c_re', 'c_im', 'd_skip', 'w_glu', 'q_gain', 'k_gain', 'g_ssm_out', 'g_attn_out', 'w_out', 'g_ffn', 'w_gate', 'w_up', 'w_down']
TWIN_WEIGHTS = ['w_ada', 'b_ada', 'g_mix', 'w_in', 'a_re', 'a_im', 'log_dt', 'b_re', 'b_im', 'c_re', 'c_im', 'd_skip', 'w_glu', 'q_gain', 'k_gain', 'g_ssm_out', 'g_attn_out', 'w_out', 'g_ffn', 'w_gate', 'w_up', 'w_down']
TWIN_DIFF_INPUT = 'x'
TWIN_INPUTS = ['x', 'c', 'w_ada', 'b_ada', 'g_mix', 'w_in', 'a_re', 'a_im', 'log_dt', 'b_re', 'b_im', 'c_re', 'c_im', 'd_skip', 'w_glu', 'q_gain', 'k_gain', 'g_ssm_out', 'g_attn_out', 'w_out', 'g_ffn', 'w_gate', 'w_up', 'w_down', 'loss_target', 'm_w_ada', 'm_b_ada', 'm_g_mix', 'm_w_in', 'm_a_re', 'm_a_im', 'm_log_dt', 'm_b_re', 'm_b_im', 'm_c_re', 'm_c_im', 'm_d_skip', 'm_w_glu', 'm_q_gain', 'm_k_gain', 'm_g_ssm_out', 'm_g_attn_out', 'm_w_out', 'm_g_ffn', 'm_w_gate', 'm_w_up', 'm_w_down', 'v_w_ada', 'v_b_ada', 'v_g_mix', 'v_w_in', 'v_a_re', 'v_a_im', 'v_log_dt', 'v_b_re', 'v_b_im', 'v_c_re', 'v_c_im', 'v_d_skip', 'v_w_glu', 'v_q_gain', 'v_k_gain', 'v_g_ssm_out', 'v_g_attn_out', 'v_w_out', 'v_g_ffn', 'v_w_gate', 'v_w_up', 'v_w_down']
TWIN_OUTPUTS = ['loss', 'grad_x', 'grad_w_ada', 'grad_b_ada', 'grad_g_mix', 'grad_w_in', 'grad_a_re', 'grad_a_im', 'grad_log_dt', 'grad_b_re', 'grad_b_im', 'grad_c_re', 'grad_c_im', 'grad_d_skip', 'grad_w_glu', 'grad_q_gain', 'grad_k_gain', 'grad_g_ssm_out', 'grad_g_attn_out', 'grad_w_out', 'grad_g_ffn', 'grad_w_gate', 'grad_w_up', 'grad_w_down', 'delta_w_ada', 'delta_b_ada', 'delta_g_mix', 'delta_w_in', 'delta_a_re', 'delta_a_im', 'delta_log_dt', 'delta_b_re', 'delta_b_im', 'delta_c_re', 'delta_c_im', 'delta_d_skip', 'delta_w_glu', 'delta_q_gain', 'delta_k_gain', 'delta_g_ssm_out', 'delta_g_attn_out', 'delta_w_out', 'delta_g_ffn', 'delta_w_gate', 'delta_w_up', 'delta_w_down', 'new_m_w_ada', 'new_m_b_ada', 'new_m_g_mix', 'new_m_w_in', 'new_m_a_re', 'new_m_a_im', 'new_m_log_dt', 'new_m_b_re', 'new_m_b_im', 'new_m_c_re', 'new_m_c_im', 'new_m_d_skip', 'new_m_w_glu', 'new_m_q_gain', 'new_m_k_gain', 'new_m_g_ssm_out', 'new_m_g_attn_out', 'new_m_w_out', 'new_m_g_ffn', 'new_m_w_gate', 'new_m_w_up', 'new_m_w_down', 'new_v_w_ada', 'new_v_b_ada', 'new_v_g_mix', 'new_v_w_in', 'new_v_a_re', 'new_v_a_im', 'new_v_log_dt', 'new_v_b_re', 'new_v_b_im', 'new_v_c_re', 'new_v_c_im', 'new_v_d_skip', 'new_v_w_glu', 'new_v_q_gain', 'new_v_k_gain', 'new_v_g_ssm_out', 'new_v_g_attn_out', 'new_v_w_out', 'new_v_g_ffn', 'new_v_w_gate', 'new_v_w_up', 'new_v_w_down']
TWIN_LEAF_KINDS = {'loss': 'loss', 'grad_x': 'grad_x', 'grad_w_ada': 'grad_w', 'grad_b_ada': 'grad_w', 'grad_g_mix': 'grad_w', 'grad_w_in': 'grad_w', 'grad_a_re': 'grad_w', 'grad_a_im': 'grad_w', 'grad_log_dt': 'grad_w', 'grad_b_re': 'grad_w', 'grad_b_im': 'grad_w', 'grad_c_re': 'grad_w', 'grad_c_im': 'grad_w', 'grad_d_skip': 'grad_w', 'grad_w_glu': 'grad_w', 'grad_q_gain': 'grad_w', 'grad_k_gain': 'grad_w', 'grad_g_ssm_out': 'grad_w', 'grad_g_attn_out': 'grad_w', 'grad_w_out': 'grad_w', 'grad_g_ffn': 'grad_w', 'grad_w_gate': 'grad_w', 'grad_w_up': 'grad_w', 'grad_w_down': 'grad_w', 'delta_w_ada': 'delta_w', 'delta_b_ada': 'delta_w', 'delta_g_mix': 'delta_w', 'delta_w_in': 'delta_w', 'delta_a_re': 'delta_w', 'delta_a_im': 'delta_w', 'delta_log_dt': 'delta_w', 'delta_b_re': 'delta_w', 'delta_b_im': 'delta_w', 'delta_c_re': 'delta_w', 'delta_c_im': 'delta_w', 'delta_d_skip': 'delta_w', 'delta_w_glu': 'delta_w', 'delta_q_gain': 'delta_w', 'delta_k_gain': 'delta_w', 'delta_g_ssm_out': 'delta_w', 'delta_g_attn_out': 'delta_w', 'delta_w_out': 'delta_w', 'delta_g_ffn': 'delta_w', 'delta_w_gate': 'delta_w', 'delta_w_up': 'delta_w', 'delta_w_down': 'delta_w', 'new_m_w_ada': 'new_m', 'new_m_b_ada': 'new_m', 'new_m_g_mix': 'new_m', 'new_m_w_in': 'new_m', 'new_m_a_re': 'new_m', 'new_m_a_im': 'new_m', 'new_m_log_dt': 'new_m', 'new_m_b_re': 'new_m', 'new_m_b_im': 'new_m', 'new_m_c_re': 'new_m', 'new_m_c_im': 'new_m', 'new_m_d_skip': 'new_m', 'new_m_w_glu': 'new_m', 'new_m_q_gain': 'new_m', 'new_m_k_gain': 'new_m', 'new_m_g_ssm_out': 'new_m', 'new_m_g_attn_out': 'new_m', 'new_m_w_out': 'new_m', 'new_m_g_ffn': 'new_m', 'new_m_w_gate': 'new_m', 'new_m_w_up': 'new_m', 'new_m_w_down': 'new_m', 'new_v_w_ada': 'new_v', 'new_v_b_ada': 'new_v', 'new_v_g_mix': 'new_v', 'new_v_w_in': 'new_v', 'new_v_a_re': 'new_v', 'new_v_a_im': 'new_v', 'new_v_log_dt': 'new_v', 'new_v_b_re': 'new_v', 'new_v_b_im': 'new_v', 'new_v_c_re': 'new_v', 'new_v_c_im': 'new_v', 'new_v_d_skip': 'new_v', 'new_v_w_glu': 'new_v', 'new_v_q_gain': 'new_v', 'new_v_k_gain': 'new_v', 'new_v_g_ssm_out': 'new_v', 'new_v_g_attn_out': 'new_v', 'new_v_w_out': 'new_v', 'new_v_g_ffn': 'new_v', 'new_v_w_gate': 'new_v', 'new_v_w_up': 'new_v', 'new_v_w_down': 'new_v'}


def _forward(args):
    return _fwd_reference(*[args[k] for k in FWD_PARAMS])


def _output_shape():
    def fwd():
        inp = _fwd_setup_inputs(0)
        return _fwd_reference(*[inp[k] for k in FWD_PARAMS])
    out = _jax.eval_shape(fwd)
    return out.shape, out.dtype

N_MICROBATCH = 1
ADAM_LR = 0.001
ADAM_B1 = 0.9
ADAM_B2 = 0.999
ADAM_EPS = 1e-08
ADAM_WD = 0.01
ADAM_STEP = 10
PER_EXAMPLE_BATCH_AXIS = {'x': 0, 'c': 0, 'loss_target': 0}
SHARED_INPUTS = []
_WEIGHT_DTYPES = {'w_ada': _jnp.float32, 'b_ada': _jnp.float32, 'g_mix': _jnp.float32, 'w_in': _jnp.float32, 'a_re': _jnp.float32, 'a_im': _jnp.float32, 'log_dt': _jnp.float32, 'b_re': _jnp.float32, 'b_im': _jnp.float32, 'c_re': _jnp.float32, 'c_im': _jnp.float32, 'd_skip': _jnp.float32, 'w_glu': _jnp.float32, 'q_gain': _jnp.float32, 'k_gain': _jnp.float32, 'g_ssm_out': _jnp.float32, 'g_attn_out': _jnp.float32, 'w_out': _jnp.float32, 'g_ffn': _jnp.float32, 'w_gate': _jnp.float32, 'w_up': _jnp.float32, 'w_down': _jnp.float32}
MOMENT_SCALE = {'w_ada': 2.276398e+00, 'b_ada': 5.531730e+00, 'g_mix': 1.545572e-01, 'w_in': 1.130582e+00, 'a_re': 1.037141e-01, 'a_im': 1.236927e-01, 'log_dt': 5.022024e+00, 'b_re': 9.141079e-02, 'b_im': 8.995756e-02, 'c_re': 1.131279e-01, 'c_im': 1.071822e-01, 'd_skip': 2.134491e+00, 'w_glu': 5.102014e-01, 'q_gain': 2.136550e-01, 'k_gain': 2.146561e-01, 'g_ssm_out': 1.163420e+01, 'g_attn_out': 6.306636e+00, 'w_out': 2.054296e+00, 'g_ffn': 1.333432e+01, 'w_gate': 7.101422e-01, 'w_up': 5.395486e-01, 'w_down': 6.408208e-01}


def _to_microbatches(a, axis):
    t = _jnp.moveaxis(a, axis, 0)
    t = t.reshape((N_MICROBATCH, t.shape[0] // N_MICROBATCH) + t.shape[1:])
    return _jnp.moveaxis(t, 1, axis + 1)


def setup_inputs(seed: int = 0) -> dict:
    inp = _fwd_setup_inputs(seed)
    key = _jax.random.fold_in(_jax.random.key(seed), 7919)
    shape, _ = _output_shape()
    out = dict(inp)
    out["loss_target"] = _jax.random.normal(_jax.random.fold_in(key, 0), shape, _jnp.float32)
    for i, name in enumerate(TWIN_WEIGHTS):
        w = inp[name].astype(_jnp.float32)
        if MOMENT_SCALE is None:
            s = _jnp.sqrt(_jnp.mean(_jnp.square(w)) + 1e-30)
        else:
            s = MOMENT_SCALE[name]
        km, kv = _jax.random.split(_jax.random.fold_in(key, i + 1))
        out[name] = w
        out["m_" + name] = s * _jax.random.normal(km, w.shape, _jnp.float32)
        out["v_" + name] = (s * s) * _jax.random.uniform(kv, w.shape, _jnp.float32, 0.5, 1.5)
    if N_MICROBATCH > 1:
        for name, axis in PER_EXAMPLE_BATCH_AXIS.items():
            out[name] = _to_microbatches(out[name], axis)
    return {'x': out['x'], 'c': out['c'], 'w_ada': out['w_ada'], 'b_ada': out['b_ada'], 'g_mix': out['g_mix'], 'w_in': out['w_in'], 'a_re': out['a_re'], 'a_im': out['a_im'], 'log_dt': out['log_dt'], 'b_re': out['b_re'], 'b_im': out['b_im'], 'c_re': out['c_re'], 'c_im': out['c_im'], 'd_skip': out['d_skip'], 'w_glu': out['w_glu'], 'q_gain': out['q_gain'], 'k_gain': out['k_gain'], 'g_ssm_out': out['g_ssm_out'], 'g_attn_out': out['g_attn_out'], 'w_out': out['w_out'], 'g_ffn': out['g_ffn'], 'w_gate': out['w_gate'], 'w_up': out['w_up'], 'w_down': out['w_down'], 'loss_target': out['loss_target'], 'm_w_ada': out['m_w_ada'], 'm_b_ada': out['m_b_ada'], 'm_g_mix': out['m_g_mix'], 'm_w_in': out['m_w_in'], 'm_a_re': out['m_a_re'], 'm_a_im': out['m_a_im'], 'm_log_dt': out['m_log_dt'], 'm_b_re': out['m_b_re'], 'm_b_im': out['m_b_im'], 'm_c_re': out['m_c_re'], 'm_c_im': out['m_c_im'], 'm_d_skip': out['m_d_skip'], 'm_w_glu': out['m_w_glu'], 'm_q_gain': out['m_q_gain'], 'm_k_gain': out['m_k_gain'], 'm_g_ssm_out': out['m_g_ssm_out'], 'm_g_attn_out': out['m_g_attn_out'], 'm_w_out': out['m_w_out'], 'm_g_ffn': out['m_g_ffn'], 'm_w_gate': out['m_w_gate'], 'm_w_up': out['m_w_up'], 'm_w_down': out['m_w_down'], 'v_w_ada': out['v_w_ada'], 'v_b_ada': out['v_b_ada'], 'v_g_mix': out['v_g_mix'], 'v_w_in': out['v_w_in'], 'v_a_re': out['v_a_re'], 'v_a_im': out['v_a_im'], 'v_log_dt': out['v_log_dt'], 'v_b_re': out['v_b_re'], 'v_b_im': out['v_b_im'], 'v_c_re': out['v_c_re'], 'v_c_im': out['v_c_im'], 'v_d_skip': out['v_d_skip'], 'v_w_glu': out['v_w_glu'], 'v_q_gain': out['v_q_gain'], 'v_k_gain': out['v_k_gain'], 'v_g_ssm_out': out['v_g_ssm_out'], 'v_g_attn_out': out['v_g_attn_out'], 'v_w_out': out['v_w_out'], 'v_g_ffn': out['v_g_ffn'], 'v_w_gate': out['v_w_gate'], 'v_w_up': out['v_w_up'], 'v_w_down': out['v_w_down']}


def _loss(weights, diff, rest, loss_target):
    with _jax.named_scope("forward"):
        args = {**rest, TWIN_DIFF_INPUT: diff, **{k: w.astype(_WEIGHT_DTYPES[k]) for k, w in weights.items()}}
        y = _forward(args)
    with _jax.named_scope("loss_head"):
        err = _jnp.square(y.astype(_jnp.float32) - loss_target)
        return 0.5 * _jnp.sum(_jnp.mean(err, axis=-1)) if err.ndim else 0.5 * err


def _adamw(w, g, m, v):
    m = ADAM_B1 * m + (1.0 - ADAM_B1) * g
    v = ADAM_B2 * v + (1.0 - ADAM_B2) * _jnp.square(g)
    m_hat = m / (1.0 - ADAM_B1 ** ADAM_STEP)
    v_hat = v / (1.0 - ADAM_B2 ** ADAM_STEP)
    delta = -ADAM_LR * (m_hat / (_jnp.sqrt(v_hat) + ADAM_EPS) + ADAM_WD * w)
    return delta, m, v


def reference(x, c, w_ada, b_ada, g_mix, w_in, a_re, a_im, log_dt, b_re, b_im, c_re, c_im, d_skip, w_glu, q_gain, k_gain, g_ssm_out, g_attn_out, w_out, g_ffn, w_gate, w_up, w_down, loss_target, m_w_ada, m_b_ada, m_g_mix, m_w_in, m_a_re, m_a_im, m_log_dt, m_b_re, m_b_im, m_c_re, m_c_im, m_d_skip, m_w_glu, m_q_gain, m_k_gain, m_g_ssm_out, m_g_attn_out, m_w_out, m_g_ffn, m_w_gate, m_w_up, m_w_down, v_w_ada, v_b_ada, v_g_mix, v_w_in, v_a_re, v_a_im, v_log_dt, v_b_re, v_b_im, v_c_re, v_c_im, v_d_skip, v_w_glu, v_q_gain, v_k_gain, v_g_ssm_out, v_g_attn_out, v_w_out, v_g_ffn, v_w_gate, v_w_up, v_w_down):
    given = dict(x=x, c=c, w_ada=w_ada, b_ada=b_ada, g_mix=g_mix, w_in=w_in, a_re=a_re, a_im=a_im, log_dt=log_dt, b_re=b_re, b_im=b_im, c_re=c_re, c_im=c_im, d_skip=d_skip, w_glu=w_glu, q_gain=q_gain, k_gain=k_gain, g_ssm_out=g_ssm_out, g_attn_out=g_attn_out, w_out=w_out, g_ffn=g_ffn, w_gate=w_gate, w_up=w_up, w_down=w_down, loss_target=loss_target, m_w_ada=m_w_ada, m_b_ada=m_b_ada, m_g_mix=m_g_mix, m_w_in=m_w_in, m_a_re=m_a_re, m_a_im=m_a_im, m_log_dt=m_log_dt, m_b_re=m_b_re, m_b_im=m_b_im, m_c_re=m_c_re, m_c_im=m_c_im, m_d_skip=m_d_skip, m_w_glu=m_w_glu, m_q_gain=m_q_gain, m_k_gain=m_k_gain, m_g_ssm_out=m_g_ssm_out, m_g_attn_out=m_g_attn_out, m_w_out=m_w_out, m_g_ffn=m_g_ffn, m_w_gate=m_w_gate, m_w_up=m_w_up, m_w_down=m_w_down, v_w_ada=v_w_ada, v_b_ada=v_b_ada, v_g_mix=v_g_mix, v_w_in=v_w_in, v_a_re=v_a_re, v_a_im=v_a_im, v_log_dt=v_log_dt, v_b_re=v_b_re, v_b_im=v_b_im, v_c_re=v_c_re, v_c_im=v_c_im, v_d_skip=v_d_skip, v_w_glu=v_w_glu, v_q_gain=v_q_gain, v_k_gain=v_k_gain, v_g_ssm_out=v_g_ssm_out, v_g_attn_out=v_g_attn_out, v_w_out=v_w_out, v_g_ffn=v_g_ffn, v_w_gate=v_w_gate, v_w_up=v_w_up, v_w_down=v_w_down)
    weights = {n: given[n] for n in TWIN_WEIGHTS}
    shared = {n: given[n] for n in SHARED_INPUTS}
    per_example = {n: given[n] for n in ['x', 'c']}
    grad_fn = _jax.value_and_grad(_loss, argnums=(0, 1))

    def one_microbatch(ex, loss_target):
        ex = dict(ex)
        diff = ex.pop(TWIN_DIFF_INPUT)
        return grad_fn(weights, diff, {**shared, **ex}, loss_target)

    if N_MICROBATCH == 1:
        loss, (grad_w, grad_x) = one_microbatch(per_example, given["loss_target"])
    else:
        def body(carry, xs):
            loss_sum, grad_sum = carry
            l_k, (gw_k, gx_k) = one_microbatch(xs[0], xs[1])
            with _jax.named_scope("update"):
                return (loss_sum + l_k, _jax.tree.map(_jnp.add, grad_sum, gw_k)), gx_k

        init = (_jnp.zeros((), _jnp.float32), _jax.tree.map(_jnp.zeros_like, weights))
        (loss, grad_w), grad_x = _jax.lax.scan(body, init, (per_example, given["loss_target"]))
    with _jax.named_scope("update"):
        delta_w, new_m, new_v = {}, {}, {}
        for n in TWIN_WEIGHTS:
            delta_w[n], new_m[n], new_v[n] = _adamw(weights[n], grad_w[n], given["m_" + n], given["v_" + n])
    return (loss, grad_x, *[grad_w[n] for n in TWIN_WEIGHTS], *[delta_w[n] for n in TWIN_WEIGHTS],
            *[new_m[n] for n in TWIN_WEIGHTS], *[new_v[n] for n in TWIN_WEIGHTS])
```

```python
import functools
import math

import jax
import jax.numpy as jnp
from jax import lax
from jax.experimental import pallas as pl
from jax.experimental.pallas import tpu as pltpu

F32 = jnp.float32
BF16 = jnp.bfloat16
NDEV = 8
MESH_AXES = ("x", "y", "c")
MESH_ID = pl.DeviceIdType.MESH
EPS = 1e-6
LANES = 128
SUBLANES = 8
HEAD_DIM = 64
SSM_GROUP = 16
ADAM_LR, ADAM_B1, ADAM_B2, ADAM_EPS, ADAM_WD, ADAM_STEP = 0.001, 0.9, 0.999, 1e-08, 0.01, 10

NN = (((1,), (0,)), ((), ()))
NT = (((1,), (1,)), ((), ()))
TN = (((0,), (0,)), ((), ()))


def _dot(a, b, dn=NN):
    return lax.dot_general(a, b, dn, preferred_element_type=F32)


def _tile(dim, pref):
    t = min(dim, pref)
    while dim % t:
        t //= 2
    return t


def _params(n):
    return pltpu.CompilerParams(dimension_semantics=("arbitrary",) * n)


def _me():
    mx, my, mc = lax.axis_index("x"), lax.axis_index("y"), lax.axis_index("c")
    return mx, my, mc


def _peer(mx, my, mc, k):
    px = 1 - mx if (k >> 2) & 1 else mx
    py = 1 - my if (k >> 1) & 1 else my
    pc = 1 - mc if k & 1 else mc
    return (px, py, pc), 4 * px + 2 * py + pc


def _all_gather(x, name):
    def body(x_ref, o_ref, send_sems, recv_sems, local_sem):
        mx, my, mc = _me()
        me = 4 * mx + 2 * my + mc
        local = pltpu.make_async_copy(x_ref, o_ref.at[me], local_sem)
        local.start()
        sends = []
        for k in range(1, NDEV):
            peer, _ = _peer(mx, my, mc, k)
            cp = pltpu.make_async_remote_copy(src_ref=x_ref, dst_ref=o_ref.at[me], send_sem=send_sems.at[k - 1],
                                              recv_sem=recv_sems.at[k - 1], device_id=peer, device_id_type=MESH_ID)
            cp.start()
            sends.append(cp)
        for k in range(1, NDEV):
            peer, pidx = _peer(mx, my, mc, k)
            pltpu.make_async_remote_copy(src_ref=x_ref, dst_ref=o_ref.at[pidx], send_sem=send_sems.at[k - 1],
                                         recv_sem=recv_sems.at[k - 1], device_id=peer, device_id_type=MESH_ID).wait_recv()
        for cp in sends:
            cp.wait_send()
        local.wait()

    return pl.pallas_call(
        body, name=name, out_shape=jax.ShapeDtypeStruct((NDEV,) + x.shape, x.dtype),
        in_specs=[pl.BlockSpec(memory_space=pl.ANY)], out_specs=pl.BlockSpec(memory_space=pl.ANY),
        scratch_shapes=[pltpu.SemaphoreType.DMA((NDEV - 1,)), pltpu.SemaphoreType.DMA((NDEV - 1,)), pltpu.SemaphoreType.DMA],
    )(x)


def _all_to_all(x, name):
    def body(x_ref, o_ref, send_sems, recv_sems, local_sem):
        mx, my, mc = _me()
        me = 4 * mx + 2 * my + mc
        local = pltpu.make_async_copy(x_ref.at[me], o_ref.at[me], local_sem)
        local.start()
        sends = []
        for k in range(1, NDEV):
            peer, pidx = _peer(mx, my, mc, k)
            cp = pltpu.make_async_remote_copy(src_ref=x_ref.at[pidx], dst_ref=o_ref.at[me], send_sem=send_sems.at[k - 1],
                                              recv_sem=recv_sems.at[k - 1], device_id=peer, device_id_type=MESH_ID)
            cp.start()
            sends.append(cp)
        for k in range(1, NDEV):
            peer, pidx = _peer(mx, my, mc, k)
            pltpu.make_async_remote_copy(src_ref=x_ref.at[pidx], dst_ref=o_ref.at[pidx], send_sem=send_sems.at[k - 1],
                                         recv_sem=recv_sems.at[k - 1], device_id=peer, device_id_type=MESH_ID).wait_recv()
        for cp in sends:
            cp.wait_send()
        local.wait()

    return pl.pallas_call(
        body, name=name, out_shape=jax.ShapeDtypeStruct(x.shape, x.dtype),
        in_specs=[pl.BlockSpec(memory_space=pl.ANY)], out_specs=pl.BlockSpec(memory_space=pl.ANY),
        scratch_shapes=[pltpu.SemaphoreType.DMA((NDEV - 1,)), pltpu.SemaphoreType.DMA((NDEV - 1,)), pltpu.SemaphoreType.DMA],
    )(x)


def _mm(name, a, b, dn, grid, a_spec, b_spec, o_spec, out_shape, out_dtype, acc_shape):
    nk = grid[2]

    def body(a_ref, b_ref, o_ref, acc_ref):
        k = pl.program_id(2)

        @pl.when(k == 0)
        def _():
            acc_ref[...] = jnp.zeros_like(acc_ref)

        acc_ref[...] += _dot(a_ref[...].astype(BF16), b_ref[...].astype(BF16), dn)

        @pl.when(k == nk - 1)
        def _():
            o_ref[...] = acc_ref[...].astype(o_ref.dtype)

    return pl.pallas_call(
        body, name=name, grid=grid, in_specs=[a_spec, b_spec], out_specs=o_spec,
        out_shape=jax.ShapeDtypeStruct(out_shape, out_dtype), scratch_shapes=[pltpu.VMEM(acc_shape, F32)],
        compiler_params=_params(3),
    )(a, b)


BM, BN, BK = 1024, 1024, 512


def _mm_plain(name, a, b, dn, out_dtype):
    if dn == NN:
        (m, kk), n = a.shape, b.shape[1]
    elif dn == NT:
        (m, kk), n = a.shape, b.shape[0]
    else:
        (kk, m), n = a.shape, b.shape[1]
    bm, bn, bk = _tile(m, BM), _tile(n, BN), _tile(kk, BK)
    a_spec = pl.BlockSpec((bk, bm), lambda i, j, k: (k, i)) if dn == TN else pl.BlockSpec((bm, bk), lambda i, j, k: (i, k))
    b_spec = pl.BlockSpec((bn, bk), lambda i, j, k: (j, k)) if dn == NT else pl.BlockSpec((bk, bn), lambda i, j, k: (k, j))
    return _mm(name, a, b, dn, (m // bm, n // bn, kk // bk), a_spec, b_spec,
               pl.BlockSpec((bm, bn), lambda i, j, k: (i, j)), (m, n), out_dtype, (bm, bn))


def _row_spec(tile, width, col=0):
    return pl.BlockSpec((tile, width), lambda i: (i, col))


def _vec_spec(width, col=0):
    return pl.BlockSpec((1, width), lambda i: (0, col))


def _rowwise_fwd(name, fn, rows, row_specs, vecs, vec_specs, out_shapes, out_specs, n_tiles):
    nr, nv = len(rows), len(vecs)

    def body(*refs):
        ins = [r[...].astype(F32) for r in refs[:nr + nv]]
        outs = fn(*ins)
        for o_ref, o in zip(refs[nr + nv:], outs):
            o_ref[...] = o.astype(o_ref.dtype)

    return pl.pallas_call(body, name=name, grid=(n_tiles,), in_specs=list(row_specs) + list(vec_specs),
                          out_specs=list(out_specs), out_shape=list(out_shapes), compiler_params=_params(1))(*rows, *vecs)


def _rowwise_bwd(name, fn, rows, row_specs, vecs, vec_specs, cts, ct_specs, ct_groups,
                 drow_idx, drow_shapes, drow_specs, dvec_idx, dvec_shapes, dvec_specs, n_tiles):
    nr, nv, nc = len(rows), len(vecs), len(cts)

    def body(*refs):
        ins = [r[...].astype(F32) for r in refs[:nr + nv]]
        ct_vals = [r[...].astype(F32) for r in refs[nr + nv:nr + nv + nc]]
        out_refs = refs[nr + nv + nc:]
        _, vjp = jax.vjp(fn, *ins)
        grads = vjp(tuple(functools.reduce(lambda p, q: p + q, [ct_vals[j] for j in grp]) for grp in ct_groups))
        for o_ref, idx in zip(out_refs[:len(drow_idx)], drow_idx):
            o_ref[...] = grads[idx].astype(o_ref.dtype)
        step = pl.program_id(0)
        for o_ref, idx in zip(out_refs[len(drow_idx):], dvec_idx):
            @pl.when(step == 0)
            def _(o_ref=o_ref):
                o_ref[...] = jnp.zeros_like(o_ref)
            o_ref[...] += grads[nr + idx]

    return pl.pallas_call(body, name=name, grid=(n_tiles,),
                          in_specs=list(row_specs) + list(vec_specs) + list(ct_specs),
                          out_specs=list(drow_specs) + list(dvec_specs),
                          out_shape=list(drow_shapes) + list(dvec_shapes), compiler_params=_params(1))(*rows, *vecs, *cts)


def _rms(x):
    return x * lax.rsqrt(jnp.mean(x * x, axis=-1, keepdims=True) + EPS)


def _seg_in(x, shift, scale, gain):
    return _rms(x) * gain * (1.0 + scale) + shift, x


def _seg_qk(q, k, qg, kg):
    def norm(t, g, mult):
        blocks = []
        lane = lax.broadcasted_iota(jnp.int32, (1, LANES), 1)
        for p in range(t.shape[1] // LANES):
            tb = t[:, p * LANES:(p + 1) * LANES]
            sq = tb * tb
            lo = jnp.sum(jnp.where(lane < HEAD_DIM, sq, 0.0), axis=-1, keepdims=True)
            hi = jnp.sum(jnp.where(lane < HEAD_DIM, 0.0, sq), axis=-1, keepdims=True)
            ms = jnp.where(lane < HEAD_DIM, lo, hi) * (1.0 / HEAD_DIM)
            blocks.append(tb * lax.rsqrt(ms + EPS) * (g[:, p * LANES:(p + 1) * LANES] * mult))
        return jnp.concatenate(blocks, axis=-1) if len(blocks) > 1 else blocks[0]
    return norm(q, qg, 1.0 / math.sqrt(HEAD_DIM)), norm(k, kg, 1.0)


def _seg_gelu(ypre):
    return (jax.nn.gelu(ypre),)


def _seg_mix(y1, z, yattn, g_ssm, g_attn):
    ys = y1 * jax.nn.sigmoid(z)
    return (jnp.concatenate([_rms(ys) * g_ssm, _rms(yattn) * g_attn], axis=-1),)


def _seg_mid(x, o, gate_m, g_ffn, scale_f, shift_f):
    h1 = x + gate_m * o
    return h1, _rms(h1) * g_ffn * (1.0 + scale_f) + shift_f


def _seg_act(gate, up):
    return (jax.nn.silu(gate) * up,)


def _s5_lam(a_re, a_im, log_dt):
    dt = jnp.exp(log_dt)
    mag = jnp.exp(a_re * dt)
    lr, li = mag * jnp.cos(a_im * dt), mag * jnp.sin(a_im * dt)
    den = a_re * a_re + a_im * a_im
    nr, ni = lr - 1.0, li
    return lr, li, (nr * a_re + ni * a_im) / den, (ni * a_re - nr * a_im) / den


def _s5_bbar(coef_re, coef_im, b_re, b_im):
    return coef_re * b_re - coef_im * b_im, coef_re * b_im + coef_im * b_re


def _whole(name, fn, ins, out_shapes):
    n = len(ins)

    def body(*refs):
        outs = fn(*[r[...] for r in refs[:n]])
        for o_ref, o in zip(refs[n:], outs):
            o_ref[...] = o

    return pl.pallas_call(body, name=name, out_shape=[jax.ShapeDtypeStruct(s, F32) for s in out_shapes])(*ins)


def _whole_vjp(name, fn, ins, cts, out_shapes):
    n, nc = len(ins), len(cts)

    def body(*refs):
        _, vjp = jax.vjp(fn, *[r[...] for r in refs[:n]])
        grads = vjp(tuple(r[...] for r in refs[n:n + nc]))
        for o_ref, g in zip(refs[n + nc:], grads):
            o_ref[...] = g

    return pl.pallas_call(body, name=name, out_shape=[jax.ShapeDtypeStruct(s, F32) for s in out_shapes])(*ins, *cts)


SCAN_SHIFTS = (1, 2, 4)


def _cmul(ar, ai, br, bi):
    return ar * br - ai * bi, ar * bi + ai * br


def _scan_coefs(lr, li, reverse):
    s = lr.shape[1]
    row = lax.broadcasted_iota(jnp.int32, (SUBLANES, s), 0)
    p1 = (lr, li)
    p2 = _cmul(*p1, *p1)
    p4 = _cmul(*p2, *p2)
    p8 = _cmul(*p4, *p4)
    p3, p5, p6 = _cmul(*p1, *p2), _cmul(*p4, *p1), _cmul(*p4, *p2)
    p7 = _cmul(*p6, *p1)
    pows = (p1, p2, p3, p4, p5, p6, p7, p8)
    bc = lambda t: jnp.broadcast_to(t, (SUBLANES, s))
    steps = []
    for sh, pw in zip(SCAN_SHIFTS, (p1, p2, p4)):
        keep = (row + sh <= SUBLANES - 1) if reverse else (row >= sh)
        steps.append((jnp.where(keep, bc(pw[0]), 0.0), jnp.where(keep, bc(pw[1]), 0.0)))
    cr, ci = jnp.zeros((SUBLANES, s), F32), jnp.zeros((SUBLANES, s), F32)
    for r in range(SUBLANES):
        pw = pows[SUBLANES - 1 - r] if reverse else pows[r]
        cr = jnp.where(row == r, bc(pw[0]), cr)
        ci = jnp.where(row == r, bc(pw[1]), ci)
    return steps, (cr, ci)


def _scan_tile(xr, xi, steps, carry_pow, cr, ci, reverse):
    for sh, (ar, ai) in zip(SCAN_SHIFTS, steps):
        rs = SUBLANES - sh if reverse else sh
        sr, si = pltpu.roll(xr, rs, 0), pltpu.roll(xi, rs, 0)
        xr, xi = xr + ar * sr - ai * si, xi + ar * si + ai * sr
    pr, pi = carry_pow
    return xr + pr * cr - pi * ci, xi + pr * ci + pi * cr


def _s5_forward(proj, b_blk_re, b_blk_im, c_blk_re, c_blk_im, lam_re, lam_im, d_skip, n_blk, t_chunk):
    seq = proj.shape[0]
    n_chunks = seq // t_chunk
    n_tiles = t_chunk // SUBLANES
    s = b_blk_re.shape[2]

    def body(u_ref, bre_ref, bim_ref, cre_ref, cim_ref, lr_ref, li_ref, d_ref, y_ref, xr_ref, xi_ref, wr, wi, carry):
        t = pl.program_id(1)

        @pl.when(t == 0)
        def _():
            carry[...] = jnp.zeros_like(carry)

        u = u_ref[...]
        ub = u.astype(BF16)
        wr[...] = _dot(ub, bre_ref[...])
        wi[...] = _dot(ub, bim_ref[...])
        steps, cpow = _scan_coefs(lr_ref[...], li_ref[...], False)

        def tile(i, c):
            r0 = pl.multiple_of(i * SUBLANES, SUBLANES)
            xr, xi = _scan_tile(wr[pl.ds(r0, SUBLANES), :], wi[pl.ds(r0, SUBLANES), :], steps, cpow, c[0], c[1], False)
            xr_ref[pl.ds(r0, SUBLANES), :] = xr
            xi_ref[pl.ds(r0, SUBLANES), :] = xi
            last = SUBLANES - 1
            return (jnp.broadcast_to(xr[last:, :], xr.shape), jnp.broadcast_to(xi[last:, :], xi.shape))

        cr, ci = lax.fori_loop(0, n_tiles, tile, (carry[0], carry[1]))
        carry[0] = cr
        carry[1] = ci
        y = _dot(xr_ref[...].astype(BF16), cre_ref[...]) - _dot(xi_ref[...].astype(BF16), cim_ref[...])
        y_ref[...] = y + d_ref[...] * u

    blk = lambda shape: pl.BlockSpec((None,) + shape, lambda j, t: (j, 0, 0))
    return pl.pallas_call(
        body, name="s5_fwd", grid=(n_blk, n_chunks),
        in_specs=[pl.BlockSpec((t_chunk, LANES), lambda j, t: (t, j)), blk((LANES, s)), blk((LANES, s)),
                  blk((s, LANES)), blk((s, LANES)), blk((1, s)), blk((1, s)), pl.BlockSpec((1, LANES), lambda j, t: (0, j))],
        out_specs=[pl.BlockSpec((t_chunk, LANES), lambda j, t: (t, j)), pl.BlockSpec((t_chunk, s), lambda j, t: (t, j)),
                   pl.BlockSpec((t_chunk, s), lambda j, t: (t, j))],
        out_shape=[jax.ShapeDtypeStruct((seq, n_blk * LANES), F32), jax.ShapeDtypeStruct((seq, n_blk * s), F32),
                   jax.ShapeDtypeStruct((seq, n_blk * s), F32)],
        scratch_shapes=[pltpu.VMEM((t_chunk, s), F32), pltpu.VMEM((t_chunk, s), F32), pltpu.VMEM((2, SUBLANES, s), F32)],
        compiler_params=_params(2),
    )(proj, b_blk_re, b_blk_im, c_blk_re, c_blk_im, lam_re, lam_im, d_skip)


def _s5_backward(dypre, proj, x_re, x_im, b_blk_re, b_blk_im, c_blk_re, c_blk_im, lam_re, lam_im, d_skip, n_blk, t_chunk):
    seq = proj.shape[0]
    n_chunks = seq // t_chunk
    n_tiles = t_chunk // SUBLANES
    s = b_blk_re.shape[2]

    def body(dy_ref, u_ref, xr_ref, xi_ref, pr_ref, pi_ref, bre_ref, bim_ref, cre_ref, cim_ref, lr_ref, li_ref, d_ref,
             du_ref, dbre_ref, dbim_ref, dcre_ref, dcim_ref, dlr_ref, dli_ref, dd_ref, gr, gi, carry):
        t = pl.program_id(1)

        @pl.when(t == 0)
        def _():
            carry[...] = jnp.zeros_like(carry)
            for r in (dbre_ref, dbim_ref, dcre_ref, dcim_ref, dlr_ref, dli_ref, dd_ref):
                r[...] = jnp.zeros_like(r)

        dy = dy_ref[...]
        dyb = dy.astype(BF16)
        u = u_ref[...]
        gr[...] = _dot(dyb, cre_ref[...], NT)
        gi[...] = -_dot(dyb, cim_ref[...], NT)
        steps, cpow = _scan_coefs(lr_ref[...], -li_ref[...], True)
        row = lax.broadcasted_iota(jnp.int32, (SUBLANES, s), 0)
        last = SUBLANES - 1
        first_chunk = t == n_chunks - 1

        def tile_at(r0, prev_r, prev_i, c):
            cr, ci, ar, ai = c
            lr_, li_ = _scan_tile(gr[pl.ds(r0, SUBLANES), :], gi[pl.ds(r0, SUBLANES), :], steps, cpow, cr, ci, True)
            gr[pl.ds(r0, SUBLANES), :] = lr_
            gi[pl.ds(r0, SUBLANES), :] = li_
            xr, xi = xr_ref[pl.ds(r0, SUBLANES), :], xi_ref[pl.ds(r0, SUBLANES), :]
            xpr = jnp.where(row == 0, jnp.broadcast_to(prev_r[last:, :], xr.shape), pltpu.roll(xr, 1, 0))
            xpi = jnp.where(row == 0, jnp.broadcast_to(prev_i[last:, :], xi.shape), pltpu.roll(xi, 1, 0))
            ar = ar + lr_ * xpr + li_ * xpi
            ai = ai + li_ * xpr - lr_ * xpi
            return (jnp.broadcast_to(lr_[:1, :], lr_.shape), jnp.broadcast_to(li_[:1, :], li_.shape), ar, ai)

        def tile(ii, c):
            i = n_tiles - 1 - ii
            r0 = pl.multiple_of(i * SUBLANES, SUBLANES)
            rp = pl.multiple_of(r0 - SUBLANES, SUBLANES)
            return tile_at(r0, xr_ref[pl.ds(rp, SUBLANES), :], xi_ref[pl.ds(rp, SUBLANES), :], c)

        zero = jnp.zeros((SUBLANES, s), F32)
        c = lax.fori_loop(0, n_tiles - 1, tile, (carry[0], carry[1], zero, zero))
        keep = jnp.where(first_chunk, 0.0, 1.0)
        c = tile_at(0, pr_ref[...] * keep, pi_ref[...] * keep, c)
        carry[0] = c[0]
        carry[1] = c[1]
        dlr_ref[...] += jnp.sum(c[2], axis=0, keepdims=True)
        dli_ref[...] += jnp.sum(c[3], axis=0, keepdims=True)

        lam_r, lam_i = gr[...].astype(BF16), gi[...].astype(BF16)
        du_ref[...] = _dot(lam_r, bre_ref[...], NT) + _dot(lam_i, bim_ref[...], NT) + d_ref[...] * dy
        ub = u.astype(BF16)
        dbre_ref[...] += _dot(ub, lam_r, TN)
        dbim_ref[...] += _dot(ub, lam_i, TN)
        dcre_ref[...] += _dot(xr_ref[...].astype(BF16), dyb, TN)
        dcim_ref[...] -= _dot(xi_ref[...].astype(BF16), dyb, TN)
        dd_ref[...] += jnp.sum(dy * u, axis=0, keepdims=True)

    rev = lambda t: n_chunks - 1 - t
    blk = lambda shape: pl.BlockSpec((None,) + shape, lambda j, t: (j, 0, 0))
    tpc = t_chunk // SUBLANES
    prev_spec = pl.BlockSpec((SUBLANES, s), lambda j, t: (jnp.maximum(rev(t) * tpc - 1, 0), j))
    chunk = lambda w: pl.BlockSpec((t_chunk, w), lambda j, t: (rev(t), j))
    return pl.pallas_call(
        body, name="s5_bwd", grid=(n_blk, n_chunks),
        in_specs=[chunk(LANES), chunk(LANES), chunk(s), chunk(s), prev_spec, prev_spec, blk((LANES, s)), blk((LANES, s)),
                  blk((s, LANES)), blk((s, LANES)), blk((1, s)), blk((1, s)), pl.BlockSpec((1, LANES), lambda j, t: (0, j))],
        out_specs=[chunk(LANES), blk((LANES, s)), blk((LANES, s)), blk((s, LANES)), blk((s, LANES)), blk((1, s)), blk((1, s)),
                   pl.BlockSpec((1, LANES), lambda j, t: (0, j))],
        out_shape=[jax.ShapeDtypeStruct((seq, n_blk * LANES), F32),
                   jax.ShapeDtypeStruct((n_blk, LANES, s), F32), jax.ShapeDtypeStruct((n_blk, LANES, s), F32),
                   jax.ShapeDtypeStruct((n_blk, s, LANES), F32), jax.ShapeDtypeStruct((n_blk, s, LANES), F32),
                   jax.ShapeDtypeStruct((n_blk, 1, s), F32), jax.ShapeDtypeStruct((n_blk, 1, s), F32),
                   jax.ShapeDtypeStruct((1, n_blk * LANES), F32)],
        scratch_shapes=[pltpu.VMEM((t_chunk, s), F32), pltpu.VMEM((t_chunk, s), F32), pltpu.VMEM((2, SUBLANES, s), F32)],
        compiler_params=_params(2),
    )(dypre, proj, x_re, x_im, x_re, x_im, b_blk_re, b_blk_im, c_blk_re, c_blk_im, lam_re, lam_im, d_skip)


TQ, TK = 256, 128


def _split_bf16(x):
    hi = x.astype(BF16)
    return hi, (x - hi.astype(F32)).astype(BF16)


def _sb_block(q_h, kb, past, carry, tri):
    z = _dot(q_h, kb, NT)
    sp = jnp.log(1.0 + jnp.exp(-jnp.abs(z)))
    ls = jnp.minimum(z, 0.0) - sp
    lk = jnp.where(past, -jnp.maximum(z, 0.0) - sp, 0.0)
    hi, lo = _split_bf16(lk)
    after = _dot(hi, tri) + _dot(lo, tri)
    w = jnp.where(past, jnp.exp(ls + after + carry), 0.0)
    return ls, lk, w


def _attention_forward(qh, kh, proj, v_col, n_pair, tq, tk):
    seq = qh.shape[0]

    def body(q_ref, k_ref, v_ref, o_ref):
        i = pl.program_id(1)
        lane = lax.broadcasted_iota(jnp.int32, (1, LANES), 1)
        q2 = q_ref[...]
        heads = (jnp.where(lane < HEAD_DIM, q2, 0.0).astype(BF16), jnp.where(lane < HEAD_DIM, 0.0, q2).astype(BF16))
        tri = (lax.broadcasted_iota(jnp.int32, (tk, tk), 0) > lax.broadcasted_iota(jnp.int32, (tk, tk), 1)).astype(BF16)
        qpos = i * tq + lax.broadcasted_iota(jnp.int32, (tq, tk), 0)
        kidx = lax.broadcasted_iota(jnp.int32, (tq, tk), 1)
        n_kb = (i + 1) * (tq // tk)

        def step(jj, c):
            j = n_kb - 1 - jj
            c0 = pl.multiple_of(j * tk, tk)
            kb = k_ref[pl.ds(c0, tk), :].astype(BF16)
            vb = v_ref[pl.ds(c0, tk), :].astype(BF16)
            past = (kidx + c0) < qpos
            new = []
            for h in range(2):
                carry, acc = c[2 * h], c[2 * h + 1]
                _, lk, w = _sb_block(heads[h], kb, past, carry, tri)
                new += [carry + jnp.sum(lk, axis=-1, keepdims=True), acc + _dot(w.astype(BF16), vb)]
            return tuple(new)

        zc, za = jnp.zeros((tq, 1), F32), jnp.zeros((tq, LANES), F32)
        c = lax.fori_loop(0, n_kb, step, (zc, za, zc, za))
        o_ref[...] = jnp.where(lane < HEAD_DIM, c[1], c[3])

    return pl.pallas_call(
        body, name="attn_fwd", grid=(n_pair, seq // tq),
        in_specs=[pl.BlockSpec((tq, LANES), lambda p, i: (i, p)), pl.BlockSpec((seq, LANES), lambda p, i: (0, p)),
                  pl.BlockSpec((seq, LANES), lambda p, i: (0, v_col + p))],
        out_specs=pl.BlockSpec((tq, LANES), lambda p, i: (i, p)),
        out_shape=jax.ShapeDtypeStruct(qh.shape, F32), compiler_params=_params(2),
    )(qh, kh, proj)


def _attention_backward(qh, kh, proj, v_col, y, dy, n_pair, tq, tk):
    seq = qh.shape[0]

    def body(q_ref, k_ref, v_ref, y_ref, dy_ref, dq_ref, dk_ref, dv_ref):
        i = pl.program_id(1)

        @pl.when(i == 0)
        def _():
            dk_ref[...] = jnp.zeros_like(dk_ref)
            dv_ref[...] = jnp.zeros_like(dv_ref)

        lane = lax.broadcasted_iota(jnp.int32, (1, LANES), 1)
        sel = (lane < HEAD_DIM, lane >= HEAD_DIM)
        q2, do2 = q_ref[...], dy_ref[...].astype(BF16)
        dot_oy = do2.astype(F32) * y_ref[...]
        heads = tuple(jnp.where(m, q2, 0.0).astype(BF16) for m in sel)
        douts = tuple(jnp.where(m, do2, jnp.zeros_like(do2)) for m in sel)
        totals = tuple(jnp.sum(jnp.where(m, dot_oy, 0.0), axis=-1, keepdims=True) for m in sel)
        r_i, c_i = lax.broadcasted_iota(jnp.int32, (tk, tk), 0), lax.broadcasted_iota(jnp.int32, (tk, tk), 1)
        tri = (r_i > c_i).astype(BF16)
        tri_ge = (r_i >= c_i).astype(BF16)
        qpos = i * tq + lax.broadcasted_iota(jnp.int32, (tq, tk), 0)
        kidx = lax.broadcasted_iota(jnp.int32, (tq, tk), 1)
        n_kb = (i + 1) * (tq // tk)

        def step(jj, c):
            j = n_kb - 1 - jj
            c0 = pl.multiple_of(j * tk, tk)
            kb = k_ref[pl.ds(c0, tk), :].astype(BF16)
            vb = v_ref[pl.ds(c0, tk), :].astype(BF16)
            past = (kidx + c0) < qpos
            new = []
            dk_add, dv_add = jnp.zeros((tk, LANES), F32), jnp.zeros((tk, LANES), F32)
            for h in range(2):
                carry, carry2, dq = c[3 * h], c[3 * h + 1], c[3 * h + 2]
                ls, lk, w = _sb_block(heads[h], kb, past, carry, tri)
                wb = w.astype(BF16)
                dlw = _dot(douts[h], vb, NT) * wb.astype(F32)
                hi, lo = _split_bf16(dlw)
                suffix = _dot(hi, tri_ge) + _dot(lo, tri_ge)
                dlk = jnp.where(past, totals[h] - carry2 - suffix, 0.0)
                sig = jnp.exp(ls)
                dz = (dlw * (1.0 - sig) - dlk * sig).astype(BF16)
                new += [carry + jnp.sum(lk, axis=-1, keepdims=True), carry2 + jnp.sum(dlw, axis=-1, keepdims=True),
                        dq + _dot(dz, kb)]
                dk_add = dk_add + _dot(dz, heads[h], TN)
                dv_add = dv_add + _dot(wb, douts[h], TN)
            dk_ref[pl.ds(c0, tk), :] += dk_add
            dv_ref[pl.ds(c0, tk), :] += dv_add
            return tuple(new)

        zc, za = jnp.zeros((tq, 1), F32), jnp.zeros((tq, LANES), F32)
        c = lax.fori_loop(0, n_kb, step, (zc, zc, za, zc, zc, za))
        dq_ref[...] = jnp.where(sel[0], c[2], c[5])

    blk = pl.BlockSpec((tq, LANES), lambda p, i: (i, p))
    full = pl.BlockSpec((seq, LANES), lambda p, i: (0, p))
    shape = jax.ShapeDtypeStruct(qh.shape, F32)
    return pl.pallas_call(
        body, name="attn_bwd", grid=(n_pair, seq // tq),
        in_specs=[blk, full, pl.BlockSpec((seq, LANES), lambda p, i: (0, v_col + p)), blk, blk],
        out_specs=[blk, full, full], out_shape=[shape, shape, shape], compiler_params=_params(2),
    )(qh, kh, proj, y, dy)


def _loss_head(h1, ffn, target, gate_f, tile):
    seq, d = h1.shape

    def body(h_ref, f_ref, t_ref, g_ref, dy_ref, df_ref, dg_ref, loss_ref):
        @pl.when(pl.program_id(0) == 0)
        def _():
            dg_ref[...] = jnp.zeros_like(dg_ref)
            loss_ref[...] = jnp.zeros_like(loss_ref)

        f, g = f_ref[...], g_ref[...]
        err = h_ref[...] + g * f - t_ref[...]
        dy = err * (1.0 / d)
        dy_ref[...] = dy
        df_ref[...] = (dy * g).astype(df_ref.dtype)
        dg_ref[...] += jnp.sum(dy * f, axis=0, keepdims=True)
        loss_ref[...] += jnp.sum(jnp.sum(err * err, axis=-1, keepdims=True), axis=0, keepdims=True) * (0.5 / d)

    row = _row_spec(tile, d)
    return pl.pallas_call(
        body, name="loss_head", grid=(seq // tile,), in_specs=[row, row, row, _vec_spec(d)],
        out_specs=[row, row, _vec_spec(d), pl.BlockSpec((1, 1), lambda i: (0, 0))],
        out_shape=[jax.ShapeDtypeStruct((seq, d), F32), jax.ShapeDtypeStruct((seq, d), BF16),
                   jax.ShapeDtypeStruct((1, d), F32), jax.ShapeDtypeStruct((1, 1), F32)],
        compiler_params=_params(1),
    )(h1, ffn, target, gate_f)


def _dot3(a, b, dn):
    ah, al = _split_bf16(a)
    bh, bl = _split_bf16(b)
    return _dot(ah, bh, dn) + (_dot(ah, bl, dn) + _dot(al, bh, dn))


def _ada_forward(c_all, w_shard, b_cols):
    d, n = w_shard.shape
    bk = _tile(d, 512)

    def body(c_ref, w_ref, b_ref, o_ref):
        @pl.when(pl.program_id(0) == 0)
        def _():
            o_ref[...] = jnp.broadcast_to(b_ref[...], o_ref.shape)

        o_ref[...] += _dot3(jax.nn.silu(c_ref[...]), w_ref[...], NN)

    return pl.pallas_call(
        body, name="ada_fwd", grid=(d // bk,),
        in_specs=[pl.BlockSpec((NDEV, bk), lambda k: (0, k)), pl.BlockSpec((bk, n), lambda k: (k, 0)), _vec_spec(n)],
        out_specs=pl.BlockSpec((NDEV, n), lambda k: (0, 0)), out_shape=jax.ShapeDtypeStruct((NDEV, n), F32),
        compiler_params=_params(1),
    )(c_all, w_shard, b_cols)


def _adam(w, g, m, v):
    m = ADAM_B1 * m + (1.0 - ADAM_B1) * g
    v = ADAM_B2 * v + (1.0 - ADAM_B2) * (g * g)
    m_hat = m / (1.0 - ADAM_B1 ** ADAM_STEP)
    v_hat = v / (1.0 - ADAM_B2 ** ADAM_STEP)
    return -ADAM_LR * (m_hat / (jnp.sqrt(v_hat) + ADAM_EPS) + ADAM_WD * w), m, v


def _adam_ada(c_all, dmod_cols, w, m, v):
    d, n = w.shape
    tr = _tile(d, 256)

    def body(c_ref, dm_ref, w_ref, m_ref, v_ref, g_ref, dl_ref, nm_ref, nv_ref):
        g = _dot3(jax.nn.silu(c_ref[...]), dm_ref[...], TN)
        delta, nm, nv = _adam(w_ref[...], g, m_ref[...], v_ref[...])
        g_ref[...] = g
        dl_ref[...] = delta
        nm_ref[...] = nm
        nv_ref[...] = nv

    row = _row_spec(tr, n)
    return pl.pallas_call(
        body, name="adam_ada", grid=(d // tr,),
        in_specs=[pl.BlockSpec((NDEV, tr), lambda i: (0, i)), pl.BlockSpec((NDEV, n), lambda i: (0, 0)), row, row, row],
        out_specs=[row] * 4, out_shape=[jax.ShapeDtypeStruct((d, n), F32)] * 4, compiler_params=_params(1),
    )(c_all, dmod_cols, w, m, v)


def _adam_sum(name, parts, part_spec, w, m, v, tr):
    r, c = w.shape

    def body(p_ref, w_ref, m_ref, v_ref, g_ref, dl_ref, nm_ref, nv_ref):
        g = p_ref[0].astype(F32)
        for k in range(1, NDEV):
            g = g + p_ref[k].astype(F32)
        delta, nm, nv = _adam(w_ref[...], g, m_ref[...], v_ref[...])
        g_ref[...] = g
        dl_ref[...] = delta
        nm_ref[...] = nm
        nv_ref[...] = nv

    row = _row_spec(tr, c)
    return pl.pallas_call(
        body, name=name, grid=(r // tr,), in_specs=[part_spec, row, row, row],
        out_specs=[row] * 4, out_shape=[jax.ShapeDtypeStruct((r, c), F32)] * 4, compiler_params=_params(1),
    )(parts, w, m, v)


GROUPS_PER_BLOCK = LANES // SSM_GROUP


def _to_b_blocks(bb, n_blk, p):
    t = bb.reshape(n_blk, GROUPS_PER_BLOCK, p, SSM_GROUP)
    eye = jnp.eye(GROUPS_PER_BLOCK, dtype=bb.dtype)
    return jnp.einsum("jgph,gk->jghkp", t, eye).reshape(n_blk, LANES, GROUPS_PER_BLOCK * p)


def _from_b_blocks(blk, n_blk, p):
    t = blk.reshape(n_blk, GROUPS_PER_BLOCK, SSM_GROUP, GROUPS_PER_BLOCK, p)
    eye = jnp.eye(GROUPS_PER_BLOCK, dtype=blk.dtype)
    return jnp.einsum("jghkp,gk->jgph", t, eye).reshape(n_blk * GROUPS_PER_BLOCK, p, SSM_GROUP)


def _to_c_blocks(cc, n_blk, p):
    t = cc.reshape(n_blk, GROUPS_PER_BLOCK, SSM_GROUP, p)
    eye = jnp.eye(GROUPS_PER_BLOCK, dtype=cc.dtype)
    return jnp.einsum("jghp,gk->jgpkh", t, eye).reshape(n_blk, GROUPS_PER_BLOCK * p, LANES)


def _from_c_blocks(blk, n_blk, p):
    t = blk.reshape(n_blk, GROUPS_PER_BLOCK, p, GROUPS_PER_BLOCK, SSM_GROUP)
    eye = jnp.eye(GROUPS_PER_BLOCK, dtype=blk.dtype)
    return jnp.einsum("jgpkh,gk->jghp", t, eye).reshape(n_blk * GROUPS_PER_BLOCK, SSM_GROUP, p)


SMALL = ("b_ada", "g_mix", "a_re", "a_im", "log_dt", "b_re", "b_im", "c_re", "c_im", "d_skip",
         "q_gain", "k_gain", "g_ssm_out", "g_attn_out", "g_ffn")
PACK_COLS = 1024


def _pack(arrs):
    flat = jnp.concatenate([a.reshape(-1) for a in arrs])
    n = flat.shape[0]
    quantum = SUBLANES * PACK_COLS
    padded = -(-n // quantum) * quantum
    return jnp.pad(flat, (0, padded - n)).reshape(padded // PACK_COLS, PACK_COLS)


def _unpack(packed, like):
    flat, out, off = packed.reshape(-1), [], 0
    for a in like:
        out.append(flat[off:off + a.size].reshape(a.shape))
        off += a.size
    return out


def kernel(x, c, w_ada, b_ada, g_mix, w_in, a_re, a_im, log_dt, b_re, b_im, c_re, c_im, d_skip, w_glu, q_gain, k_gain, g_ssm_out, g_attn_out, w_out, g_ffn, w_gate, w_up, w_down, loss_target, m_w_ada, m_b_ada, m_g_mix, m_w_in, m_a_re, m_a_im, m_log_dt, m_b_re, m_b_im, m_c_re, m_c_im, m_d_skip, m_w_glu, m_q_gain, m_k_gain, m_g_ssm_out, m_g_attn_out, m_w_out, m_g_ffn, m_w_gate, m_w_up, m_w_down, v_w_ada, v_b_ada, v_g_mix, v_w_in, v_a_re, v_a_im, v_log_dt, v_b_re, v_b_im, v_c_re, v_c_im, v_d_skip, v_w_glu, v_q_gain, v_k_gain, v_g_ssm_out, v_g_attn_out, v_w_out, v_g_ffn, v_w_gate, v_w_up, v_w_down):
    given = dict(locals())
    seq, d = x.shape[1], x.shape[2]
    xs, tgt = x[0], loss_target[0]
    n_groups, p_state = a_re.shape[1], a_re.shape[2]
    w_ssm = n_groups * SSM_GROUP
    w_attn = w_in.shape[2] * NDEV - w_ssm
    w_attn //= 3
    n_blk, n_pair = w_ssm // LANES, w_attn // LANES
    n_heads = w_attn // HEAD_DIM
    ns_in, ns_ff = w_in.shape[2], w_gate.shape[2]
    d_mix = w_ssm + w_attn
    mx, my, mc = _me()
    me = 4 * mx + 2 * my + mc
    rt = _tile(seq, 256)
    n_rt = seq // rt
    sds = jax.ShapeDtypeStruct

    w_in_g = _all_gather(w_in[0].astype(BF16), "comm_ag_w_in")
    w_glu_g = _all_gather(w_glu[0].astype(BF16), "comm_ag_w_glu").reshape(w_ssm, w_ssm)
    w_out_g = _all_gather(w_out[0].astype(BF16), "comm_ag_w_out").reshape(d_mix, d)
    w_gu_g = _all_gather(jnp.stack([w_gate[0], w_up[0]]).astype(BF16), "comm_ag_w_gu").reshape(2 * NDEV, d, ns_ff)
    w_down_g = _all_gather(w_down[0].astype(BF16), "comm_ag_w_down")

    c_all = _all_gather(c, "comm_ag_c").reshape(NDEV, d)
    n_ada = w_ada.shape[2]
    b_cols = lax.dynamic_slice(b_ada, (0, me * n_ada), (1, n_ada))
    mod_cols = _ada_forward(c_all, w_ada[0], b_cols)
    mod_all = _all_gather(mod_cols, "comm_ag_mod")
    mod = lax.dynamic_slice(mod_all, (0, me, 0), (NDEV, 1, n_ada)).reshape(1, NDEV * n_ada)
    shift_m, scale_m, gate_m, shift_f, scale_f, gate_f = [mod[:, i * d:(i + 1) * d] for i in range(6)]

    gp = n_groups * p_state
    a_re2, a_im2, ldt2 = a_re[0], a_im[0], log_dt[0].reshape(n_groups, 1)
    b_re2, b_im2 = b_re[0].reshape(gp, SSM_GROUP), b_im[0].reshape(gp, SSM_GROUP)
    lam_r, lam_i, coef_r, coef_i = _whole("s5_lam", _s5_lam, [a_re2, a_im2, ldt2], [(n_groups, p_state)] * 4)
    coef_r2, coef_i2 = coef_r.reshape(gp, 1), coef_i.reshape(gp, 1)
    bb_r, bb_i = _whole("s5_bbar", _s5_bbar, [coef_r2, coef_i2, b_re2, b_im2], [(gp, SSM_GROUP)] * 2)
    s_blk = GROUPS_PER_BLOCK * p_state
    b_blk_r = _to_b_blocks(bb_r.reshape(n_groups, p_state, SSM_GROUP), n_blk, p_state).astype(BF16)
    b_blk_i = _to_b_blocks(bb_i.reshape(n_groups, p_state, SSM_GROUP), n_blk, p_state).astype(BF16)
    c_blk_r = _to_c_blocks(c_re[0], n_blk, p_state).astype(BF16)
    c_blk_i = _to_c_blocks(c_im[0], n_blk, p_state).astype(BF16)
    lam_r3, lam_i3 = lam_r.reshape(n_blk, 1, s_blk), lam_i.reshape(n_blk, 1, s_blk)
    d_skip2 = d_skip[0].reshape(1, w_ssm)

    row_d, vec_d = _row_spec(rt, d), _vec_spec(d)
    xm, = _rowwise_fwd("seg_in", lambda *a: _seg_in(*a)[:1], [xs], [row_d], [shift_m, scale_m, g_mix], [vec_d] * 3,
                       [sds((seq, d), BF16)], [row_d], n_rt)
    bn_in = _tile(ns_in, 512)
    per = ns_in // bn_in
    bm, bk = _tile(seq, BM), _tile(d, BK)
    proj = _mm("mm_in", xm, w_in_g, NN, (seq // bm, NDEV * per, d // bk),
               pl.BlockSpec((bm, bk), lambda i, j, k: (i, k)),
               pl.BlockSpec((None, bk, bn_in), lambda i, j, k: (j // per, k, j % per)),
               pl.BlockSpec((bm, bn_in), lambda i, j, k: (i, j)), (seq, NDEV * ns_in), F32, (bm, bn_in))
    q_col, k_col, v_col = w_ssm // w_attn, w_ssm // w_attn + 1, (w_ssm + 2 * w_attn) // LANES
    qg_t, kg_t = jnp.tile(q_gain, (1, n_heads)), jnp.tile(k_gain, (1, n_heads))
    row_a, vec_a = _row_spec(rt, w_attn), _vec_spec(w_attn)
    qk_rows, qk_specs = [proj, proj], [_row_spec(rt, w_attn, q_col), _row_spec(rt, w_attn, k_col)]
    qh, kh = _rowwise_fwd("seg_qk", _seg_qk, qk_rows, qk_specs, [qg_t, kg_t], [vec_a] * 2,
                          [sds((seq, w_attn), F32)] * 2, [row_a] * 2, n_rt)
    t_chunk = _tile(seq, 256)
    ypre, x_re, x_im = _s5_forward(proj, b_blk_r, b_blk_i, c_blk_r, c_blk_i, lam_r3, lam_i3, d_skip2, n_blk, t_chunk)
    tq, tk = _tile(seq, TQ), _tile(seq, TK)
    y_attn = _attention_forward(qh, kh, proj, v_col, n_pair, tq, tk)
    row_s, vec_s = _row_spec(rt, w_ssm), _vec_spec(w_ssm)
    y1, = _rowwise_fwd("seg_gelu", _seg_gelu, [ypre], [row_s], [], [], [sds((seq, w_ssm), F32)], [row_s], n_rt)
    z = _mm_plain("mm_glu", y1, w_glu_g, NN, F32)
    row_m = _row_spec(rt, d_mix)
    mixed, = _rowwise_fwd("seg_mix", _seg_mix, [y1, z, y_attn], [row_s, row_s, row_a], [g_ssm_out, g_attn_out], [vec_s, vec_a],
                          [sds((seq, d_mix), BF16)], [row_m], n_rt)
    o = _mm_plain("mm_out", mixed, w_out_g, NN, F32)
    h1, xf = _rowwise_fwd("seg_mid", _seg_mid, [xs, o], [row_d] * 2, [gate_m, g_ffn, scale_f, shift_f], [vec_d] * 4,
                          [sds((seq, d), F32), sds((seq, d), BF16)], [row_d] * 2, n_rt)
    gu = _mm("mm_gu", xf, w_gu_g, NN, (seq // bm, 2 * NDEV, d // bk),
             pl.BlockSpec((bm, bk), lambda i, j, k: (i, k)), pl.BlockSpec((None, bk, ns_ff), lambda i, j, k: (j, k, 0)),
             pl.BlockSpec((None, bm, ns_ff), lambda i, j, k: (j, i, 0)), (2 * NDEV, seq, ns_ff), F32, (bm, ns_ff))
    gu4 = gu.reshape(NDEV, 2, seq, ns_ff)
    ft = _tile(seq, 512)
    pair_spec = pl.BlockSpec((None, 2, ft, ns_ff), lambda s, i: (s, 0, i, 0))
    one_spec = pl.BlockSpec((None, ft, ns_ff), lambda s, i: (s, i, 0))

    def act_body(gu_ref, a_ref):
        a_ref[...] = _seg_act(gu_ref[0], gu_ref[1])[0].astype(a_ref.dtype)

    act = pl.pallas_call(act_body, name="seg_act", grid=(NDEV, seq // ft), in_specs=[pair_spec], out_specs=one_spec,
                         out_shape=sds((NDEV, seq, ns_ff), BF16), compiler_params=_params(2))(gu4)
    bn_d = _tile(d, BN)
    ffn = _mm("mm_down", act, w_down_g, NN, (seq // bm, d // bn_d, NDEV),
              pl.BlockSpec((None, bm, ns_ff), lambda i, j, k: (k, i, 0)), pl.BlockSpec((None, ns_ff, bn_d), lambda i, j, k: (k, 0, j)),
              pl.BlockSpec((bm, bn_d), lambda i, j, k: (i, j)), (seq, d), F32, (bm, bn_d))
    dy, dffn, d_gate_f, loss_part = _loss_head(h1, ffn, tgt, gate_f, rt)
    loss = lax.psum(loss_part[0, 0], MESH_AXES)

    dact = _mm("mm_dact", dffn, w_down_g, NT, (seq // bm, NDEV, d // bk),
               pl.BlockSpec((bm, bk), lambda i, j, k: (i, k)), pl.BlockSpec((None, ns_ff, bk), lambda i, j, k: (j, 0, k)),
               pl.BlockSpec((None, bm, ns_ff), lambda i, j, k: (j, i, 0)), (NDEV, seq, ns_ff), F32, (bm, ns_ff))
    bl = _tile(seq, BK)
    gw_down = _mm("mm_dw_down", act, dffn, TN, (NDEV, d // bn_d, seq // bl),
                  pl.BlockSpec((None, bl, ns_ff), lambda i, j, k: (i, k, 0)), pl.BlockSpec((bl, bn_d), lambda i, j, k: (k, j)),
                  pl.BlockSpec((None, ns_ff, bn_d), lambda i, j, k: (i, 0, j)), (NDEV, ns_ff, d), BF16, (ns_ff, bn_d))

    def dact_body(gu_ref, da_ref, dgu_ref):
        _, vjp = jax.vjp(_seg_act, gu_ref[0], gu_ref[1])
        dg, du_ = vjp((da_ref[...],))
        dgu_ref[0] = dg.astype(dgu_ref.dtype)
        dgu_ref[1] = du_.astype(dgu_ref.dtype)

    dgu4 = pl.pallas_call(dact_body, name="seg_act_bwd", grid=(NDEV, seq // ft), in_specs=[pair_spec, one_spec],
                          out_specs=pair_spec, out_shape=sds((NDEV, 2, seq, ns_ff), BF16), compiler_params=_params(2))(gu4, dact)
    dgu = dgu4.reshape(2 * NDEV, seq, ns_ff)
    dxf = _mm("mm_dxf", dgu, w_gu_g, NT, (seq // bm, d // bn_d, 2 * NDEV),
              pl.BlockSpec((None, bm, ns_ff), lambda i, j, k: (k, i, 0)), pl.BlockSpec((None, bn_d, ns_ff), lambda i, j, k: (k, j, 0)),
              pl.BlockSpec((bm, bn_d), lambda i, j, k: (i, j)), (seq, d), F32, (bm, bn_d))
    bmd = _tile(d, BM)
    gw_gu = _mm("mm_dw_gu", xf, dgu, TN, (d // bmd, 2 * NDEV, seq // bl),
                pl.BlockSpec((bl, bmd), lambda i, j, k: (k, i)), pl.BlockSpec((None, bl, ns_ff), lambda i, j, k: (j, k, 0)),
                pl.BlockSpec((None, bmd, ns_ff), lambda i, j, k: (j, i, 0)), (2 * NDEV, d, ns_ff), BF16, (bmd, ns_ff))
    (do, dx_a, d_gate_m, d_g_ffn, d_scale_f, d_shift_f) = _rowwise_bwd(
        "seg_mid_bwd", _seg_mid, [xs, o], [row_d] * 2, [gate_m, g_ffn, scale_f, shift_f], [vec_d] * 4,
        [dy, dxf], [row_d] * 2, [[0], [1]], [1, 0], [sds((seq, d), BF16), sds((seq, d), F32)], [row_d] * 2,
        [0, 1, 2, 3], [sds((1, d), F32)] * 4, [vec_d] * 4, n_rt)

    dmixed = _mm_plain("mm_dmixed", do, w_out_g, NT, F32)
    gw_out = _mm_plain("mm_dw_out", mixed, do, TN, BF16)
    (dz, dy1_a, dy_attn, d_g_ssm, d_g_attn) = _rowwise_bwd(
        "seg_mix_bwd", _seg_mix, [y1, z, y_attn], [row_s, row_s, row_a], [g_ssm_out, g_attn_out], [vec_s, vec_a],
        [dmixed], [row_m], [[0]], [1, 0, 2], [sds((seq, w_ssm), BF16), sds((seq, w_ssm), F32), sds((seq, w_attn), F32)],
        [row_s, row_s, row_a], [0, 1], [sds((1, w_ssm), F32), sds((1, w_attn), F32)], [vec_s, vec_a], n_rt)
    dy1_b = _mm_plain("mm_dy1", dz, w_glu_g, NT, F32)
    gw_glu = _mm_plain("mm_dw_glu", y1, dz, TN, BF16)
    (dypre,) = _rowwise_bwd("seg_gelu_bwd", _seg_gelu, [ypre], [row_s], [], [], [dy1_a, dy1_b], [row_s] * 2, [[0, 1]],
                            [0], [sds((seq, w_ssm), F32)], [row_s], [], [], [], n_rt)
    (du, db_blk_r, db_blk_i, dc_blk_r, dc_blk_i, dlam_r3, dlam_i3, dd_skip2) = _s5_backward(
        dypre, proj, x_re, x_im, b_blk_r, b_blk_i, c_blk_r, c_blk_i, lam_r3, lam_i3, d_skip2, n_blk, t_chunk)
    dqh, dkh, dv = _attention_backward(qh, kh, proj, v_col, y_attn, dy_attn, n_pair, tq, tk)
    (dq, dk, dqg_t, dkg_t) = _rowwise_bwd(
        "seg_qk_bwd", _seg_qk, qk_rows, qk_specs, [qg_t, kg_t], [vec_a] * 2, [dqh, dkh], [row_a] * 2, [[0], [1]],
        [0, 1], [sds((seq, w_attn), BF16)] * 2, [row_a] * 2, [0, 1], [sds((1, w_attn), F32)] * 2, [vec_a] * 2, n_rt)
    dproj = jnp.concatenate([du.astype(BF16), dq, dk, dv.astype(BF16)], axis=-1)
    bk_in = _tile(ns_in, BK)
    per_k = ns_in // bk_in
    dxm = _mm("mm_dxm", dproj, w_in_g, NT, (seq // bm, d // bn_d, NDEV * per_k),
              pl.BlockSpec((bm, bk_in), lambda i, j, k: (i, k)),
              pl.BlockSpec((None, bn_d, bk_in), lambda i, j, k: (k // per_k, j, k % per_k)),
              pl.BlockSpec((bm, bn_d), lambda i, j, k: (i, j)), (seq, d), F32, (bm, bn_d))
    gw_in = _mm("mm_dw_in", xm, dproj, TN, (d // bmd, NDEV * per, seq // bl),
                pl.BlockSpec((bl, bmd), lambda i, j, k: (k, i)), pl.BlockSpec((bl, bn_in), lambda i, j, k: (k, j)),
                pl.BlockSpec((None, bmd, bn_in), lambda i, j, k: (j // per, i, j % per)), (NDEV, d, ns_in), BF16, (bmd, bn_in))
    (grad_x, d_shift_m, d_scale_m, d_g_mix) = _rowwise_bwd(
        "seg_in_bwd", _seg_in, [xs], [row_d], [shift_m, scale_m, g_mix], [vec_d] * 3, [dxm, dx_a], [row_d] * 2, [[0], [1]],
        [0], [sds((seq, d), F32)], [row_d], [0, 1, 2], [sds((1, d), F32)] * 3, [vec_d] * 3, n_rt)

    dbb_r = _from_b_blocks(db_blk_r, n_blk, p_state).reshape(gp, SSM_GROUP)
    dbb_i = _from_b_blocks(db_blk_i, n_blk, p_state).reshape(gp, SSM_GROUP)
    dcoef_r2, dcoef_i2, db_re2, db_im2 = _whole_vjp("s5_bbar_bwd", _s5_bbar, [coef_r2, coef_i2, b_re2, b_im2], [dbb_r, dbb_i],
                                                    [(gp, 1), (gp, 1), (gp, SSM_GROUP), (gp, SSM_GROUP)])
    lam_cts = [dlam_r3.reshape(n_groups, p_state), dlam_i3.reshape(n_groups, p_state),
               dcoef_r2.reshape(n_groups, p_state), dcoef_i2.reshape(n_groups, p_state)]
    da_re2, da_im2, dldt2 = _whole_vjp("s5_lam_bwd", _s5_lam, [a_re2, a_im2, ldt2], lam_cts,
                                       [(n_groups, p_state), (n_groups, p_state), (n_groups, 1)])
    dc_re2, dc_im2 = _from_c_blocks(dc_blk_r, n_blk, p_state), _from_c_blocks(dc_blk_i, n_blk, p_state)

    dmod = jnp.concatenate([d_shift_m, d_scale_m, d_gate_m, d_shift_f, d_scale_f, d_gate_f], axis=-1)
    small_part = {
        "b_ada": dmod, "g_mix": d_g_mix, "a_re": da_re2, "a_im": da_im2, "log_dt": dldt2, "b_re": db_re2, "b_im": db_im2,
        "c_re": dc_re2, "c_im": dc_im2, "d_skip": dd_skip2,
        "q_gain": dqg_t.reshape(n_heads, HEAD_DIM).sum(0), "k_gain": dkg_t.reshape(n_heads, HEAD_DIM).sum(0),
        "g_ssm_out": d_g_ssm, "g_attn_out": d_g_attn, "g_ffn": d_g_ffn,
    }
    packed_parts = _all_gather(_pack([small_part[n] for n in SMALL]), "comm_ag_small")
    rows_p = packed_parts.shape[1]
    tr_p = _tile(rows_p, 64)
    small_spec = pl.BlockSpec((NDEV, tr_p, PACK_COLS), lambda i: (0, i, 0))
    sm = _adam_sum("adam_small", packed_parts, small_spec, _pack([given[n] for n in SMALL]),
                   _pack([given["m_" + n] for n in SMALL]), _pack([given["v_" + n] for n in SMALL]), tr_p)
    like = [given[n] for n in SMALL]
    small_out = [dict(zip(SMALL, _unpack(t, like))) for t in sm]

    dmod_all = packed_parts[:, :(6 * d) // PACK_COLS, :].reshape(NDEV, 6 * d) if (6 * d) % PACK_COLS == 0 else None
    assert dmod_all is not None
    dmod_cols = lax.dynamic_slice(dmod_all, (0, me * n_ada), (NDEV, n_ada))
    big = {"w_ada": _adam_ada(c_all, dmod_cols, w_ada[0], m_w_ada[0], v_w_ada[0])}

    def sharded(name, partial, part_spec_fn, tr_pref):
        got = _all_to_all(partial, "comm_a2a_" + name)
        w = given[name][0]
        tr = _tile(w.shape[0], tr_pref)
        return got, tr

    got, tr = sharded("w_in", gw_in, None, 256)
    big["w_in"] = _adam_sum("adam_w_in", got, pl.BlockSpec((NDEV, tr, ns_in), lambda i: (0, i, 0)), w_in[0], m_w_in[0], v_w_in[0], tr)
    r_glu = w_glu.shape[1]
    got, tr = sharded("w_glu", gw_glu.reshape(NDEV, r_glu, w_ssm), None, 128)
    big["w_glu"] = _adam_sum("adam_w_glu", got, pl.BlockSpec((NDEV, tr, w_ssm), lambda i: (0, i, 0)), w_glu[0], m_w_glu[0], v_w_glu[0], tr)
    r_out = w_out.shape[1]
    got, tr = sharded("w_out", gw_out.reshape(NDEV, r_out, d), None, 128)
    big["w_out"] = _adam_sum("adam_w_out", got, pl.BlockSpec((NDEV, tr, d), lambda i: (0, i, 0)), w_out[0], m_w_out[0], v_w_out[0], tr)
    got_gu = _all_to_all(gw_gu.reshape(NDEV, 2, d, ns_ff), "comm_a2a_w_gu")
    tr = _tile(d, 256)
    for which, nm in enumerate(("w_gate", "w_up")):
        spec = pl.BlockSpec((NDEV, None, tr, ns_ff), lambda i, which=which: (0, which, i, 0))
        big[nm] = _adam_sum("adam_" + nm, got_gu, spec, given[nm][0], given["m_" + nm][0], given["v_" + nm][0], tr)
    got, tr = sharded("w_down", gw_down, None, 64)
    big["w_down"] = _adam_sum("adam_w_down", got, pl.BlockSpec((NDEV, tr, d), lambda i: (0, i, 0)), w_down[0], m_w_down[0], v_w_down[0], tr)

    order = ("w_ada", "b_ada", "g_mix", "w_in", "a_re", "a_im", "log_dt", "b_re", "b_im", "c_re", "c_im", "d_skip", "w_glu",
             "q_gain", "k_gain", "g_ssm_out", "g_attn_out", "w_out", "g_ffn", "w_gate", "w_up", "w_down")
    outs = [loss, grad_x[None]]
    for kind in range(4):
        for n in order:
            outs.append(big[n][kind][None] if n in big else small_out[kind][n])
    return tuple(outs)
```

```python
import functools
import math

import jax
import jax.numpy as jnp
from jax import lax
from jax.experimental import pallas as pl
from jax.experimental.pallas import tpu as pltpu

F32 = jnp.float32
BF16 = jnp.bfloat16
NDEV = 8
MESH_AXES = ("x", "y", "c")
MESH_ID = pl.DeviceIdType.MESH
EPS = 1e-6
LANES = 128
SUBLANES = 8
HEAD_DIM = 64
SSM_GROUP = 16
ADAM_LR, ADAM_B1, ADAM_B2, ADAM_EPS, ADAM_WD, ADAM_STEP = 0.001, 0.9, 0.999, 1e-08, 0.01, 10

NN = (((1,), (0,)), ((), ()))
NT = (((1,), (1,)), ((), ()))
TN = (((0,), (0,)), ((), ()))


def _dot(a, b, dn=NN):
    return lax.dot_general(a, b, dn, preferred_element_type=F32)


def _tile(dim, pref):
    t = min(dim, pref)
    while dim % t:
        t //= 2
    return t


def _params(n):
    return pltpu.CompilerParams(dimension_semantics=("arbitrary",) * n)


def _me():
    mx, my, mc = lax.axis_index("x"), lax.axis_index("y"), lax.axis_index("c")
    return mx, my, mc


def _peer(mx, my, mc, k):
    px = 1 - mx if (k >> 2) & 1 else mx
    py = 1 - my if (k >> 1) & 1 else my
    pc = 1 - mc if k & 1 else mc
    return (px, py, pc), 4 * px + 2 * py + pc


def _all_gather(x, name):
    def body(x_ref, o_ref, send_sems, recv_sems, local_sem):
        mx, my, mc = _me()
        me = 4 * mx + 2 * my + mc
        local = pltpu.make_async_copy(x_ref, o_ref.at[me], local_sem)
        local.start()
        sends = []
        for k in range(1, NDEV):
            peer, _ = _peer(mx, my, mc, k)
            cp = pltpu.make_async_remote_copy(src_ref=x_ref, dst_ref=o_ref.at[me], send_sem=send_sems.at[k - 1],
                                              recv_sem=recv_sems.at[k - 1], device_id=peer, device_id_type=MESH_ID)
            cp.start()
            sends.append(cp)
        for k in range(1, NDEV):
            peer, pidx = _peer(mx, my, mc, k)
            pltpu.make_async_remote_copy(src_ref=x_ref, dst_ref=o_ref.at[pidx], send_sem=send_sems.at[k - 1],
                                         recv_sem=recv_sems.at[k - 1], device_id=peer, device_id_type=MESH_ID).wait_recv()
        for cp in sends:
            cp.wait_send()
        local.wait()

    return pl.pallas_call(
        body, name=name, out_shape=jax.ShapeDtypeStruct((NDEV,) + x.shape, x.dtype),
        in_specs=[pl.BlockSpec(memory_space=pl.ANY)], out_specs=pl.BlockSpec(memory_space=pl.ANY),
        scratch_shapes=[pltpu.SemaphoreType.DMA((NDEV - 1,)), pltpu.SemaphoreType.DMA((NDEV - 1,)), pltpu.SemaphoreType.DMA],
    )(x)


def _all_to_all(x, name):
    def body(x_ref, o_ref, send_sems, recv_sems, local_sem):
        mx, my, mc = _me()
        me = 4 * mx + 2 * my + mc
        local = pltpu.make_async_copy(x_ref.at[me], o_ref.at[me], local_sem)
        local.start()
        sends = []
        for k in range(1, NDEV):
            peer, pidx = _peer(mx, my, mc, k)
            cp = pltpu.make_async_remote_copy(src_ref=x_ref.at[pidx], dst_ref=o_ref.at[me], send_sem=send_sems.at[k - 1],
                                              recv_sem=recv_sems.at[k - 1], device_id=peer, device_id_type=MESH_ID)
            cp.start()
            sends.append(cp)
        for k in range(1, NDEV):
            peer, pidx = _peer(mx, my, mc, k)
            pltpu.make_async_remote_copy(src_ref=x_ref.at[pidx], dst_ref=o_ref.at[pidx], send_sem=send_sems.at[k - 1],
                                         recv_sem=recv_sems.at[k - 1], device_id=peer, device_id_type=MESH_ID).wait_recv()
        for cp in sends:
            cp.wait_send()
        local.wait()

    return pl.pallas_call(
        body, name=name, out_shape=jax.ShapeDtypeStruct(x.shape, x.dtype),
        in_specs=[pl.BlockSpec(memory_space=pl.ANY)], out_specs=pl.BlockSpec(memory_space=pl.ANY),
        scratch_shapes=[pltpu.SemaphoreType.DMA((NDEV - 1,)), pltpu.SemaphoreType.DMA((NDEV - 1,)), pltpu.SemaphoreType.DMA],
    )(x)


def _mm(name, a, b, dn, grid, a_spec, b_spec, o_spec, out_shape, out_dtype, acc_shape):
    nk = grid[2]

    def body(a_ref, b_ref, o_ref, acc_ref):
        k = pl.program_id(2)

        @pl.when(k == 0)
        def _():
            acc_ref[...] = jnp.zeros_like(acc_ref)

        acc_ref[...] += _dot(a_ref[...].astype(BF16), b_ref[...].astype(BF16), dn)

        @pl.when(k == nk - 1)
        def _():
            o_ref[...] = acc_ref[...].astype(o_ref.dtype)

    return pl.pallas_call(
        body, name=name, grid=grid, in_specs=[a_spec, b_spec], out_specs=o_spec,
        out_shape=jax.ShapeDtypeStruct(out_shape, out_dtype), scratch_shapes=[pltpu.VMEM(acc_shape, F32)],
        compiler_params=_params(3),
    )(a, b)


BM, BN, BK = 1024, 1024, 512


def _mm_plain(name, a, b, dn, out_dtype):
    if dn == NN:
        (m, kk), n = a.shape, b.shape[1]
    elif dn == NT:
        (m, kk), n = a.shape, b.shape[0]
    else:
        (kk, m), n = a.shape, b.shape[1]
    bm, bn, bk = _tile(m, BM), _tile(n, BN), _tile(kk, BK)
    a_spec = pl.BlockSpec((bk, bm), lambda i, j, k: (k, i)) if dn == TN else pl.BlockSpec((bm, bk), lambda i, j, k: (i, k))
    b_spec = pl.BlockSpec((bn, bk), lambda i, j, k: (j, k)) if dn == NT else pl.BlockSpec((bk, bn), lambda i, j, k: (k, j))
    return _mm(name, a, b, dn, (m // bm, n // bn, kk // bk), a_spec, b_spec,
               pl.BlockSpec((bm, bn), lambda i, j, k: (i, j)), (m, n), out_dtype, (bm, bn))


def _row_spec(tile, width, col=0):
    return pl.BlockSpec((tile, width), lambda i: (i, col))


def _vec_spec(width, col=0):
    return pl.BlockSpec((1, width), lambda i: (0, col))


def _rowwise_fwd(name, fn, rows, row_specs, vecs, vec_specs, out_shapes, out_specs, n_tiles):
    nr, nv = len(rows), len(vecs)

    def body(*refs):
        ins = [r[...].astype(F32) for r in refs[:nr + nv]]
        outs = fn(*ins)
        for o_ref, o in zip(refs[nr + nv:], outs):
            o_ref[...] = o.astype(o_ref.dtype)

    return pl.pallas_call(body, name=name, grid=(n_tiles,), in_specs=list(row_specs) + list(vec_specs),
                          out_specs=list(out_specs), out_shape=list(out_shapes), compiler_params=_params(1))(*rows, *vecs)


def _rowwise_bwd(name, fn, rows, row_specs, vecs, vec_specs, cts, ct_specs, ct_groups,
                 drow_idx, drow_shapes, drow_specs, dvec_idx, dvec_shapes, dvec_specs, n_tiles):
    nr, nv, nc = len(rows), len(vecs), len(cts)

    def body(*refs):
        ins = [r[...].astype(F32) for r in refs[:nr + nv]]
        ct_vals = [r[...].astype(F32) for r in refs[nr + nv:nr + nv + nc]]
        out_refs = refs[nr + nv + nc:]
        _, vjp = jax.vjp(fn, *ins)
        grads = vjp(tuple(functools.reduce(lambda p, q: p + q, [ct_vals[j] for j in grp]) for grp in ct_groups))
        for o_ref, idx in zip(out_refs[:len(drow_idx)], drow_idx):
            o_ref[...] = grads[idx].astype(o_ref.dtype)
        step = pl.program_id(0)
        for o_ref, idx in zip(out_refs[len(drow_idx):], dvec_idx):
            @pl.when(step == 0)
            def _(o_ref=o_ref):
                o_ref[...] = jnp.zeros_like(o_ref)
            o_ref[...] += grads[nr + idx]

    return pl.pallas_call(body, name=name, grid=(n_tiles,),
                          in_specs=list(row_specs) + list(vec_specs) + list(ct_specs),
                          out_specs=list(drow_specs) + list(dvec_specs),
                          out_shape=list(drow_shapes) + list(dvec_shapes), compiler_params=_params(1))(*rows, *vecs, *cts)


def _rms(x):
    return x * lax.rsqrt(jnp.mean(x * x, axis=-1, keepdims=True) + EPS)


def _seg_in(x, shift, scale, gain):
    return _rms(x) * gain * (1.0 + scale) + shift, x


def _seg_qk(q, k, qg, kg):
    def norm(t, g, mult):
        blocks = []
        lane = lax.broadcasted_iota(jnp.int32, (1, LANES), 1)
        for p in range(t.shape[1] // LANES):
            tb = t[:, p * LANES:(p + 1) * LANES]
            sq = tb * tb
            lo = jnp.sum(jnp.where(lane < HEAD_DIM, sq, 0.0), axis=-1, keepdims=True)
            hi = jnp.sum(jnp.where(lane < HEAD_DIM, 0.0, sq), axis=-1, keepdims=True)
            ms = jnp.where(lane < HEAD_DIM, lo, hi) * (1.0 / HEAD_DIM)
            blocks.append(tb * lax.rsqrt(ms + EPS) * (g[:, p * LANES:(p + 1) * LANES] * mult))
        return jnp.concatenate(blocks, axis=-1) if len(blocks) > 1 else blocks[0]
    return norm(q, qg, 1.0 / math.sqrt(HEAD_DIM)), norm(k, kg, 1.0)


def _seg_gelu(ypre):
    return (jax.nn.gelu(ypre),)


def _seg_mix(y1, z, yattn, g_ssm, g_attn):
    ys = y1 * jax.nn.sigmoid(z)
    return (jnp.concatenate([_rms(ys) * g_ssm, _rms(yattn) * g_attn], axis=-1),)


def _seg_mid(x, o, gate_m, g_ffn, scale_f, shift_f):
    h1 = x + gate_m * o
    return h1, _rms(h1) * g_ffn * (1.0 + scale_f) + shift_f


def _seg_act(gate, up):
    return (jax.nn.silu(gate) * up,)


def _s5_lam(a_re, a_im, log_dt):
    dt = jnp.exp(log_dt)
    mag = jnp.exp(a_re * dt)
    lr, li = mag * jnp.cos(a_im * dt), mag * jnp.sin(a_im * dt)
    den = a_re * a_re + a_im * a_im
    nr, ni = lr - 1.0, li
    return lr, li, (nr * a_re + ni * a_im) / den, (ni * a_re - nr * a_im) / den


def _s5_bbar(coef_re, coef_im, b_re, b_im):
    return coef_re * b_re - coef_im * b_im, coef_re * b_im + coef_im * b_re


def _whole(name, fn, ins, out_shapes):
    n = len(ins)

    def body(*refs):
        outs = fn(*[r[...] for r in refs[:n]])
        for o_ref, o in zip(refs[n:], outs):
            o_ref[...] = o

    return pl.pallas_call(body, name=name, out_shape=[jax.ShapeDtypeStruct(s, F32) for s in out_shapes])(*ins)


def _whole_vjp(name, fn, ins, cts, out_shapes):
    n, nc = len(ins), len(cts)

    def body(*refs):
        _, vjp = jax.vjp(fn, *[r[...] for r in refs[:n]])
        grads = vjp(tuple(r[...] for r in refs[n:n + nc]))
        for o_ref, g in zip(refs[n + nc:], grads):
            o_ref[...] = g

    return pl.pallas_call(body, name=name, out_shape=[jax.ShapeDtypeStruct(s, F32) for s in out_shapes])(*ins, *cts)


SCAN_SHIFTS = (1, 2, 4)


def _cmul(ar, ai, br, bi):
    return ar * br - ai * bi, ar * bi + ai * br


def _scan_coefs(lr, li, reverse):
    s = lr.shape[1]
    row = lax.broadcasted_iota(jnp.int32, (SUBLANES, s), 0)
    p1 = (lr, li)
    p2 = _cmul(*p1, *p1)
    p4 = _cmul(*p2, *p2)
    p8 = _cmul(*p4, *p4)
    p3, p5, p6 = _cmul(*p1, *p2), _cmul(*p4, *p1), _cmul(*p4, *p2)
    p7 = _cmul(*p6, *p1)
    pows = (p1, p2, p3, p4, p5, p6, p7, p8)
    bc = lambda t: jnp.broadcast_to(t, (SUBLANES, s))
    steps = []
    for sh, pw in zip(SCAN_SHIFTS, (p1, p2, p4)):
        keep = (row + sh <= SUBLANES - 1) if reverse else (row >= sh)
        steps.append((jnp.where(keep, bc(pw[0]), 0.0), jnp.where(keep, bc(pw[1]), 0.0)))
    cr, ci = jnp.zeros((SUBLANES, s), F32), jnp.zeros((SUBLANES, s), F32)
    for r in range(SUBLANES):
        pw = pows[SUBLANES - 1 - r] if reverse else pows[r]
        cr = jnp.where(row == r, bc(pw[0]), cr)
        ci = jnp.where(row == r, bc(pw[1]), ci)
    return steps, (cr, ci)


def _scan_tile(xr, xi, steps, carry_pow, cr, ci, reverse):
    for sh, (ar, ai) in zip(SCAN_SHIFTS, steps):
        rs = SUBLANES - sh if reverse else sh
        sr, si = pltpu.roll(xr, rs, 0), pltpu.roll(xi, rs, 0)
        xr, xi = xr + ar * sr - ai * si, xi + ar * si + ai * sr
    pr, pi = carry_pow
    return xr + pr * cr - pi * ci, xi + pr * ci + pi * cr


def _s5_forward(proj, b_blk_re, b_blk_im, c_blk_re, c_blk_im, lam_re, lam_im, d_skip, n_blk, t_chunk):
    seq = proj.shape[0]
    n_chunks = seq // t_chunk
    n_tiles = t_chunk // SUBLANES
    s = b_blk_re.shape[2]

    def body(u_ref, bre_ref, bim_ref, cre_ref, cim_ref, lr_ref, li_ref, d_ref, y_ref, xr_ref, xi_ref, wr, wi, carry):
        t = pl.program_id(1)

        @pl.when(t == 0)
        def _():
            carry[...] = jnp.zeros_like(carry)

        u = u_ref[...]
        ub = u.astype(BF16)
        wr[...] = _dot(ub, bre_ref[...])
        wi[...] = _dot(ub, bim_ref[...])
        steps, cpow = _scan_coefs(lr_ref[...], li_ref[...], False)

        def tile(i, c):
            r0 = pl.multiple_of(i * SUBLANES, SUBLANES)
            xr, xi = _scan_tile(wr[pl.ds(r0, SUBLANES), :], wi[pl.ds(r0, SUBLANES), :], steps, cpow, c[0], c[1], False)
            xr_ref[pl.ds(r0, SUBLANES), :] = xr
            xi_ref[pl.ds(r0, SUBLANES), :] = xi
            last = SUBLANES - 1
            return (jnp.broadcast_to(xr[last:, :], xr.shape), jnp.broadcast_to(xi[last:, :], xi.shape))

        cr, ci = lax.fori_loop(0, n_tiles, tile, (carry[0], carry[1]))
        carry[0] = cr
        carry[1] = ci
        y = _dot(xr_ref[...].astype(BF16), cre_ref[...]) - _dot(xi_ref[...].astype(BF16), cim_ref[...])
        y_ref[...] = y + d_ref[...] * u

    blk = lambda shape: pl.BlockSpec((None,) + shape, lambda j, t: (j, 0, 0))
    return pl.pallas_call(
        body, name="s5_fwd", grid=(n_blk, n_chunks),
        in_specs=[pl.BlockSpec((t_chunk, LANES), lambda j, t: (t, j)), blk((LANES, s)), blk((LANES, s)),
                  blk((s, LANES)), blk((s, LANES)), blk((1, s)), blk((1, s)), pl.BlockSpec((1, LANES), lambda j, t: (0, j))],
        out_specs=[pl.BlockSpec((t_chunk, LANES), lambda j, t: (t, j)), pl.BlockSpec((t_chunk, s), lambda j, t: (t, j)),
                   pl.BlockSpec((t_chunk, s), lambda j, t: (t, j))],
        out_shape=[jax.ShapeDtypeStruct((seq, n_blk * LANES), F32), jax.ShapeDtypeStruct((seq, n_blk * s), F32),
                   jax.ShapeDtypeStruct((seq, n_blk * s), F32)],
        scratch_shapes=[pltpu.VMEM((t_chunk, s), F32), pltpu.VMEM((t_chunk, s), F32), pltpu.VMEM((2, SUBLANES, s), F32)],
        compiler_params=_params(2),
    )(proj, b_blk_re, b_blk_im, c_blk_re, c_blk_im, lam_re, lam_im, d_skip)


def _s5_backward(dypre, proj, x_re, x_im, b_blk_re, b_blk_im, c_blk_re, c_blk_im, lam_re, lam_im, d_skip, n_blk, t_chunk):
    seq = proj.shape[0]
    n_chunks = seq // t_chunk
    n_tiles = t_chunk // SUBLANES
    s = b_blk_re.shape[2]

    def body(dy_ref, u_ref, xr_ref, xi_ref, pr_ref, pi_ref, bre_ref, bim_ref, cre_ref, cim_ref, lr_ref, li_ref, d_ref,
             du_ref, dbre_ref, dbim_ref, dcre_ref, dcim_ref, dlr_ref, dli_ref, dd_ref, gr, gi, carry):
        t = pl.program_id(1)

        @pl.when(t == 0)
        def _():
            carry[...] = jnp.zeros_like(carry)
            for r in (dbre_ref, dbim_ref, dcre_ref, dcim_ref, dlr_ref, dli_ref, dd_ref):
                r[...] = jnp.zeros_like(r)

        dy = dy_ref[...]
        dyb = dy.astype(BF16)
        u = u_ref[...]
        gr[...] = _dot(dyb, cre_ref[...], NT)
        gi[...] = -_dot(dyb, cim_ref[...], NT)
        steps, cpow = _scan_coefs(lr_ref[...], -li_ref[...], True)
        row = lax.broadcasted_iota(jnp.int32, (SUBLANES, s), 0)
        last = SUBLANES - 1
        first_chunk = t == n_chunks - 1

        def tile_at(r0, prev_r, prev_i, c):
            cr, ci, ar, ai = c
            lr_, li_ = _scan_tile(gr[pl.ds(r0, SUBLANES), :], gi[pl.ds(r0, SUBLANES), :], steps, cpow, cr, ci, True)
            gr[pl.ds(r0, SUBLANES), :] = lr_
            gi[pl.ds(r0, SUBLANES), :] = li_
            xr, xi = xr_ref[pl.ds(r0, SUBLANES), :], xi_ref[pl.ds(r0, SUBLANES), :]
            xpr = jnp.where(row == 0, jnp.broadcast_to(prev_r[last:, :], xr.shape), pltpu.roll(xr, 1, 0))
            xpi = jnp.where(row == 0, jnp.broadcast_to(prev_i[last:, :], xi.shape), pltpu.roll(xi, 1, 0))
            ar = ar + lr_ * xpr + li_ * xpi
            ai = ai + li_ * xpr - lr_ * xpi
            return (jnp.broadcast_to(lr_[:1, :], lr_.shape), jnp.broadcast_to(li_[:1, :], li_.shape), ar, ai)

        def tile(ii, c):
            i = n_tiles - 1 - ii
            r0 = pl.multiple_of(i * SUBLANES, SUBLANES)
            rp = pl.multiple_of(r0 - SUBLANES, SUBLANES)
            return tile_at(r0, xr_ref[pl.ds(rp, SUBLANES), :], xi_ref[pl.ds(rp, SUBLANES), :], c)

        zero = jnp.zeros((SUBLANES, s), F32)
        c = lax.fori_loop(0, n_tiles - 1, tile, (carry[0], carry[1], zero, zero))
        keep = jnp.where(first_chunk, 0.0, 1.0)
        c = tile_at(0, pr_ref[...] * keep, pi_ref[...] * keep, c)
        carry[0] = c[0]
        carry[1] = c[1]
        dlr_ref[...] += jnp.sum(c[2], axis=0, keepdims=True)
        dli_ref[...] += jnp.sum(c[3], axis=0, keepdims=True)

        lam_r, lam_i = gr[...].astype(BF16), gi[...].astype(BF16)
        du_ref[...] = _dot(lam_r, bre_ref[...], NT) + _dot(lam_i, bim_ref[...], NT) + d_ref[...] * dy
        ub = u.astype(BF16)
        dbre_ref[...] += _dot(ub, lam_r, TN)
        dbim_ref[...] += _dot(ub, lam_i, TN)
        dcre_ref[...] += _dot(xr_ref[...].astype(BF16), dyb, TN)
        dcim_ref[...] -= _dot(xi_ref[...].astype(BF16), dyb, TN)
        dd_ref[...] += jnp.sum(dy * u, axis=0, keepdims=True)

    rev = lambda t: n_chunks - 1 - t
    blk = lambda shape: pl.BlockSpec((None,) + shape, lambda j, t: (j, 0, 0))
    tpc = t_chunk // SUBLANES
    prev_spec = pl.BlockSpec((SUBLANES, s), lambda j, t: (jnp.maximum(rev(t) * tpc - 1, 0), j))
    chunk = lambda w: pl.BlockSpec((t_chunk, w), lambda j, t: (rev(t), j))
    return pl.pallas_call(
        body, name="s5_bwd", grid=(n_blk, n_chunks),
        in_specs=[chunk(LANES), chunk(LANES), chunk(s), chunk(s), prev_spec, prev_spec, blk((LANES, s)), blk((LANES, s)),
                  blk((s, LANES)), blk((s, LANES)), blk((1, s)), blk((1, s)), pl.BlockSpec((1, LANES), lambda j, t: (0, j))],
        out_specs=[chunk(LANES), blk((LANES, s)), blk((LANES, s)), blk((s, LANES)), blk((s, LANES)), blk((1, s)), blk((1, s)),
                   pl.BlockSpec((1, LANES), lambda j, t: (0, j))],
        out_shape=[jax.ShapeDtypeStruct((seq, n_blk * LANES), F32),
                   jax.ShapeDtypeStruct((n_blk, LANES, s), F32), jax.ShapeDtypeStruct((n_blk, LANES, s), F32),
                   jax.ShapeDtypeStruct((n_blk, s, LANES), F32), jax.ShapeDtypeStruct((n_blk, s, LANES), F32),
                   jax.ShapeDtypeStruct((n_blk, 1, s), F32), jax.ShapeDtypeStruct((n_blk, 1, s), F32),
                   jax.ShapeDtypeStruct((1, n_blk * LANES), F32)],
        scratch_shapes=[pltpu.VMEM((t_chunk, s), F32), pltpu.VMEM((t_chunk, s), F32), pltpu.VMEM((2, SUBLANES, s), F32)],
        compiler_params=_params(2),
    )(dypre, proj, x_re, x_im, x_re, x_im, b_blk_re, b_blk_im, c_blk_re, c_blk_im, lam_re, lam_im, d_skip)


TQ, TK = 256, 128


def _split_bf16(x):
    hi = x.astype(BF16)
    return hi, (x - hi.astype(F32)).astype(BF16)


def _sb_weights(z, past, carry, tri):
    ls = jnp.minimum(z, 0.0) - jnp.log(1.0 + jnp.exp(-jnp.abs(z)))
    lk = ls - z
    if past is not None:
        lk = jnp.where(past, lk, 0.0)
    hi, lo = _split_bf16(lk)
    w = jnp.exp(ls + (_dot(hi, tri) + _dot(lo, tri)) + carry)
    if past is not None:
        w = jnp.where(past, w, 0.0)
    return ls, lk, w


def _walk_key_blocks(i, ratio, prologue, block, epilogue):
    n_kb = (i + 1) * ratio
    prologue(n_kb - 1)
    for n in range(ratio):
        block(n_kb - 1 - n, n % 2, True)

    def pair(t, carry):
        j = n_kb - 1 - ratio - 2 * t
        block(j, ratio % 2, False)
        block(j - 1, (ratio + 1) % 2, False)
        return carry

    lax.fori_loop(0, (i * ratio) // 2, pair, 0)
    epilogue()


def _attention_forward(qh, kh, proj, v_col, n_pair, tq, tk):
    seq = qh.shape[0]
    ratio = tq // tk
    assert ratio % 2 == 0

    def body(q_ref, k_ref, v_ref, o_ref, q_scr, z_scr, w_scr, acc_scr, c_scr):
        i = pl.program_id(1)
        lane = lax.broadcasted_iota(jnp.int32, (1, LANES), 1)
        q2 = q_ref[...]
        q_scr[0] = jnp.where(lane < HEAD_DIM, q2, 0.0).astype(BF16)
        q_scr[1] = jnp.where(lane < HEAD_DIM, 0.0, q2).astype(BF16)
        tri = (lax.broadcasted_iota(jnp.int32, (tk, tk), 0) > lax.broadcasted_iota(jnp.int32, (tk, tk), 1)).astype(BF16)
        qpos = i * tq + lax.broadcasted_iota(jnp.int32, (tq, tk), 0)
        kidx = lax.broadcasted_iota(jnp.int32, (tq, tk), 1)

        def rows(ref, j):
            j = jnp.clip(j, 0, seq // tk - 1)
            return ref[pl.ds(pl.multiple_of(j * tk, tk), tk), :].astype(BF16)

        def scores(j, slot):
            kb = rows(k_ref, j)
            for h in range(2):
                z_scr[slot, h] = _dot(q_scr[h], kb, NT)

        def finish(j):
            vb = rows(v_ref, j)
            for h in range(2):
                acc_scr[h] += _dot(w_scr[h], vb)

        def prologue(j):
            w_scr[...] = jnp.zeros_like(w_scr)
            acc_scr[...] = jnp.zeros_like(acc_scr)
            c_scr[...] = jnp.zeros_like(c_scr)
            scores(j, 0)

        def block(j, slot, masked):
            scores(j - 1, 1 - slot)
            finish(j + 1)
            past = ((kidx + j * tk) < qpos) if masked else None
            for h in range(2):
                _, lk, w = _sb_weights(z_scr[slot, h], past, c_scr[h], tri)
                w_scr[h] = w.astype(BF16)
                c_scr[h] += jnp.sum(lk, axis=-1, keepdims=True)

        _walk_key_blocks(i, ratio, prologue, block, lambda: finish(0))
        o_ref[...] = jnp.where(lane < HEAD_DIM, acc_scr[0], acc_scr[1])

    return pl.pallas_call(
        body, name="attn_fwd", grid=(n_pair, seq // tq),
        in_specs=[pl.BlockSpec((tq, LANES), lambda p, i: (i, p)), pl.BlockSpec((seq, LANES), lambda p, i: (0, p)),
                  pl.BlockSpec((seq, LANES), lambda p, i: (0, v_col + p))],
        out_specs=pl.BlockSpec((tq, LANES), lambda p, i: (i, p)),
        out_shape=jax.ShapeDtypeStruct(qh.shape, F32),
        scratch_shapes=[pltpu.VMEM((2, tq, LANES), BF16), pltpu.VMEM((2, 2, tq, tk), F32), pltpu.VMEM((2, tq, tk), BF16),
                        pltpu.VMEM((2, tq, LANES), F32), pltpu.VMEM((2, tq, 1), F32)],
        compiler_params=_params(2),
    )(qh, kh, proj)


def _attention_backward(qh, kh, proj, v_col, y, dy, n_pair, tq, tk):
    seq = qh.shape[0]

    ratio = tq // tk
    assert ratio % 2 == 0

    def body(q_ref, k_ref, v_ref, y_ref, dy_ref, dq_ref, dk_ref, dv_ref,
             q_scr, do_scr, z_scr, dw_scr, w_scr, dz_scr, dq_scr, c_scr, c2_scr, tot_scr):
        i = pl.program_id(1)

        @pl.when(i == 0)
        def _():
            dk_ref[...] = jnp.zeros_like(dk_ref)
            dv_ref[...] = jnp.zeros_like(dv_ref)

        lane = lax.broadcasted_iota(jnp.int32, (1, LANES), 1)
        sel = (lane < HEAD_DIM, lane >= HEAD_DIM)
        q2, do2 = q_ref[...], dy_ref[...].astype(BF16)
        dot_oy = do2.astype(F32) * y_ref[...]
        for h in range(2):
            q_scr[h] = jnp.where(sel[h], q2, 0.0).astype(BF16)
            do_scr[h] = jnp.where(sel[h], do2, jnp.zeros_like(do2))
            tot_scr[h] = jnp.sum(jnp.where(sel[h], dot_oy, 0.0), axis=-1, keepdims=True)
        r_i, c_i = lax.broadcasted_iota(jnp.int32, (tk, tk), 0), lax.broadcasted_iota(jnp.int32, (tk, tk), 1)
        tri = (r_i > c_i).astype(BF16)
        tri_ge = (r_i >= c_i).astype(BF16)
        qpos = i * tq + lax.broadcasted_iota(jnp.int32, (tq, tk), 0)
        kidx = lax.broadcasted_iota(jnp.int32, (tq, tk), 1)

        def start(j):
            return pl.multiple_of(jnp.clip(j, 0, seq // tk - 1) * tk, tk)

        def scores(j, slot):
            c0 = start(j)
            kb, vb = k_ref[pl.ds(c0, tk), :].astype(BF16), v_ref[pl.ds(c0, tk), :].astype(BF16)
            for h in range(2):
                z_scr[slot, h] = _dot(q_scr[h], kb, NT)
                dw_scr[slot, h] = _dot(do_scr[h], vb, NT)

        def finish(j):
            c0 = start(j)
            kb = k_ref[pl.ds(c0, tk), :].astype(BF16)
            dk_add, dv_add = jnp.zeros((tk, LANES), F32), jnp.zeros((tk, LANES), F32)
            for h in range(2):
                dz = dz_scr[h]
                dq_scr[h] += _dot(dz, kb)
                dk_add = dk_add + _dot(dz, q_scr[h], TN)
                dv_add = dv_add + _dot(w_scr[h], do_scr[h], TN)
            dk_ref[pl.ds(c0, tk), :] += dk_add
            dv_ref[pl.ds(c0, tk), :] += dv_add

        def prologue(j):
            for r in (w_scr, dz_scr, dq_scr, c_scr, c2_scr):
                r[...] = jnp.zeros_like(r)
            scores(j, 0)

        def block(j, slot, masked):
            scores(j - 1, 1 - slot)
            finish(j + 1)
            past = ((kidx + j * tk) < qpos) if masked else None
            for h in range(2):
                ls, lk, w = _sb_weights(z_scr[slot, h], past, c_scr[h], tri)
                wb = w.astype(BF16)
                dlw = dw_scr[slot, h] * wb.astype(F32)
                hi, lo = _split_bf16(dlw)
                dlk = tot_scr[h] - c2_scr[h] - (_dot(hi, tri_ge) + _dot(lo, tri_ge))
                if masked:
                    dlk = jnp.where(past, dlk, 0.0)
                sig = jnp.exp(ls)
                w_scr[h] = wb
                dz_scr[h] = (dlw * (1.0 - sig) - dlk * sig).astype(BF16)
                c_scr[h] += jnp.sum(lk, axis=-1, keepdims=True)
                c2_scr[h] += jnp.sum(dlw, axis=-1, keepdims=True)

        _walk_key_blocks(i, ratio, prologue, block, lambda: finish(0))
        dq_ref[...] = jnp.where(sel[0], dq_scr[0], dq_scr[1])

    blk = pl.BlockSpec((tq, LANES), lambda p, i: (i, p))
    full = pl.BlockSpec((seq, LANES), lambda p, i: (0, p))
    shape = jax.ShapeDtypeStruct(qh.shape, F32)
    return pl.pallas_call(
        body, name="attn_bwd", grid=(n_pair, seq // tq),
        in_specs=[blk, full, pl.BlockSpec((seq, LANES), lambda p, i: (0, v_col + p)), blk, blk],
        out_specs=[blk, full, full], out_shape=[shape, shape, shape],
        scratch_shapes=[pltpu.VMEM((2, tq, LANES), BF16), pltpu.VMEM((2, tq, LANES), BF16),
                        pltpu.VMEM((2, 2, tq, tk), F32), pltpu.VMEM((2, 2, tq, tk), F32),
                        pltpu.VMEM((2, tq, tk), BF16), pltpu.VMEM((2, tq, tk), BF16), pltpu.VMEM((2, tq, LANES), F32),
                        pltpu.VMEM((2, tq, 1), F32), pltpu.VMEM((2, tq, 1), F32), pltpu.VMEM((2, tq, 1), F32)],
        compiler_params=_params(2),
    )(qh, kh, proj, y, dy)


def _loss_head(h1, ffn, target, gate_f, tile):
    seq, d = h1.shape

    def body(h_ref, f_ref, t_ref, g_ref, dy_ref, df_ref, dg_ref, loss_ref):
        @pl.when(pl.program_id(0) == 0)
        def _():
            dg_ref[...] = jnp.zeros_like(dg_ref)
            loss_ref[...] = jnp.zeros_like(loss_ref)

        f, g = f_ref[...], g_ref[...]
        err = h_ref[...] + g * f - t_ref[...]
        dy = err * (1.0 / d)
        dy_ref[...] = dy
        df_ref[...] = (dy * g).astype(df_ref.dtype)
        dg_ref[...] += jnp.sum(dy * f, axis=0, keepdims=True)
        loss_ref[...] += jnp.sum(jnp.sum(err * err, axis=-1, keepdims=True), axis=0, keepdims=True) * (0.5 / d)

    row = _row_spec(tile, d)
    return pl.pallas_call(
        body, name="loss_head", grid=(seq // tile,), in_specs=[row, row, row, _vec_spec(d)],
        out_specs=[row, row, _vec_spec(d), pl.BlockSpec((1, 1), lambda i: (0, 0))],
        out_shape=[jax.ShapeDtypeStruct((seq, d), F32), jax.ShapeDtypeStruct((seq, d), BF16),
                   jax.ShapeDtypeStruct((1, d), F32), jax.ShapeDtypeStruct((1, 1), F32)],
        compiler_params=_params(1),
    )(h1, ffn, target, gate_f)


def _dot3(a, b, dn):
    ah, al = _split_bf16(a)
    bh, bl = _split_bf16(b)
    return _dot(ah, bh, dn) + (_dot(ah, bl, dn) + _dot(al, bh, dn))


def _ada_forward(c_all, w_shard, b_cols):
    d, n = w_shard.shape
    bk = _tile(d, 512)

    def body(c_ref, w_ref, b_ref, o_ref):
        @pl.when(pl.program_id(0) == 0)
        def _():
            o_ref[...] = jnp.broadcast_to(b_ref[...], o_ref.shape)

        o_ref[...] += _dot3(jax.nn.silu(c_ref[...]), w_ref[...], NN)

    return pl.pallas_call(
        body, name="ada_fwd", grid=(d // bk,),
        in_specs=[pl.BlockSpec((NDEV, bk), lambda k: (0, k)), pl.BlockSpec((bk, n), lambda k: (k, 0)), _vec_spec(n)],
        out_specs=pl.BlockSpec((NDEV, n), lambda k: (0, 0)), out_shape=jax.ShapeDtypeStruct((NDEV, n), F32),
        compiler_params=_params(1),
    )(c_all, w_shard, b_cols)


def _adam(w, g, m, v):
    m = ADAM_B1 * m + (1.0 - ADAM_B1) * g
    v = ADAM_B2 * v + (1.0 - ADAM_B2) * (g * g)
    m_hat = m / (1.0 - ADAM_B1 ** ADAM_STEP)
    v_hat = v / (1.0 - ADAM_B2 ** ADAM_STEP)
    return -ADAM_LR * (m_hat / (jnp.sqrt(v_hat) + ADAM_EPS) + ADAM_WD * w), m, v


def _adam_ada(c_all, dmod_cols, w, m, v):
    d, n = w.shape
    tr = _tile(d, 256)

    def body(c_ref, dm_ref, w_ref, m_ref, v_ref, g_ref, dl_ref, nm_ref, nv_ref):
        g = _dot3(jax.nn.silu(c_ref[...]), dm_ref[...], TN)
        delta, nm, nv = _adam(w_ref[...], g, m_ref[...], v_ref[...])
        g_ref[...] = g
        dl_ref[...] = delta
        nm_ref[...] = nm
        nv_ref[...] = nv

    row = _row_spec(tr, n)
    return pl.pallas_call(
        body, name="adam_ada", grid=(d // tr,),
        in_specs=[pl.BlockSpec((NDEV, tr), lambda i: (0, i)), pl.BlockSpec((NDEV, n), lambda i: (0, 0)), row, row, row],
        out_specs=[row] * 4, out_shape=[jax.ShapeDtypeStruct((d, n), F32)] * 4, compiler_params=_params(1),
    )(c_all, dmod_cols, w, m, v)


def _adam_sum(name, parts, part_spec, w, m, v, tr):
    r, c = w.shape

    def body(p_ref, w_ref, m_ref, v_ref, g_ref, dl_ref, nm_ref, nv_ref):
        g = p_ref[0].astype(F32)
        for k in range(1, NDEV):
            g = g + p_ref[k].astype(F32)
        delta, nm, nv = _adam(w_ref[...], g, m_ref[...], v_ref[...])
        g_ref[...] = g
        dl_ref[...] = delta
        nm_ref[...] = nm
        nv_ref[...] = nv

    row = _row_spec(tr, c)
    return pl.pallas_call(
        body, name=name, grid=(r // tr,), in_specs=[part_spec, row, row, row],
        out_specs=[row] * 4, out_shape=[jax.ShapeDtypeStruct((r, c), F32)] * 4, compiler_params=_params(1),
    )(parts, w, m, v)


GROUPS_PER_BLOCK = LANES // SSM_GROUP


def _to_b_blocks(bb, n_blk, p):
    t = bb.reshape(n_blk, GROUPS_PER_BLOCK, p, SSM_GROUP)
    eye = jnp.eye(GROUPS_PER_BLOCK, dtype=bb.dtype)
    return jnp.einsum("jgph,gk->jghkp", t, eye).reshape(n_blk, LANES, GROUPS_PER_BLOCK * p)


def _from_b_blocks(blk, n_blk, p):
    t = blk.reshape(n_blk, GROUPS_PER_BLOCK, SSM_GROUP, GROUPS_PER_BLOCK, p)
    eye = jnp.eye(GROUPS_PER_BLOCK, dtype=blk.dtype)
    return jnp.einsum("jghkp,gk->jgph", t, eye).reshape(n_blk * GROUPS_PER_BLOCK, p, SSM_GROUP)


def _to_c_blocks(cc, n_blk, p):
    t = cc.reshape(n_blk, GROUPS_PER_BLOCK, SSM_GROUP, p)
    eye = jnp.eye(GROUPS_PER_BLOCK, dtype=cc.dtype)
    return jnp.einsum("jghp,gk->jgpkh", t, eye).reshape(n_blk, GROUPS_PER_BLOCK * p, LANES)


def _from_c_blocks(blk, n_blk, p):
    t = blk.reshape(n_blk, GROUPS_PER_BLOCK, p, GROUPS_PER_BLOCK, SSM_GROUP)
    eye = jnp.eye(GROUPS_PER_BLOCK, dtype=blk.dtype)
    return jnp.einsum("jgpkh,gk->jghp", t, eye).reshape(n_blk * GROUPS_PER_BLOCK, SSM_GROUP, p)


SMALL = ("b_ada", "g_mix", "a_re", "a_im", "log_dt", "b_re", "b_im", "c_re", "c_im", "d_skip",
         "q_gain", "k_gain", "g_ssm_out", "g_attn_out", "g_ffn")
PACK_COLS = 1024


def _pack(arrs):
    flat = jnp.concatenate([a.reshape(-1) for a in arrs])
    n = flat.shape[0]
    quantum = SUBLANES * PACK_COLS
    padded = -(-n // quantum) * quantum
    return jnp.pad(flat, (0, padded - n)).reshape(padded // PACK_COLS, PACK_COLS)


def _unpack(packed, like):
    flat, out, off = packed.reshape(-1), [], 0
    for a in like:
        out.append(flat[off:off + a.size].reshape(a.shape))
        off += a.size
    return out


def kernel(x, c, w_ada, b_ada, g_mix, w_in, a_re, a_im, log_dt, b_re, b_im, c_re, c_im, d_skip, w_glu, q_gain, k_gain, g_ssm_out, g_attn_out, w_out, g_ffn, w_gate, w_up, w_down, loss_target, m_w_ada, m_b_ada, m_g_mix, m_w_in, m_a_re, m_a_im, m_log_dt, m_b_re, m_b_im, m_c_re, m_c_im, m_d_skip, m_w_glu, m_q_gain, m_k_gain, m_g_ssm_out, m_g_attn_out, m_w_out, m_g_ffn, m_w_gate, m_w_up, m_w_down, v_w_ada, v_b_ada, v_g_mix, v_w_in, v_a_re, v_a_im, v_log_dt, v_b_re, v_b_im, v_c_re, v_c_im, v_d_skip, v_w_glu, v_q_gain, v_k_gain, v_g_ssm_out, v_g_attn_out, v_w_out, v_g_ffn, v_w_gate, v_w_up, v_w_down):
    given = dict(locals())
    seq, d = x.shape[1], x.shape[2]
    xs, tgt = x[0], loss_target[0]
    n_groups, p_state = a_re.shape[1], a_re.shape[2]
    w_ssm = n_groups * SSM_GROUP
    w_attn = w_in.shape[2] * NDEV - w_ssm
    w_attn //= 3
    n_blk, n_pair = w_ssm // LANES, w_attn // LANES
    n_heads = w_attn // HEAD_DIM
    ns_in, ns_ff = w_in.shape[2], w_gate.shape[2]
    d_mix = w_ssm + w_attn
    mx, my, mc = _me()
    me = 4 * mx + 2 * my + mc
    rt = _tile(seq, 256)
    n_rt = seq // rt
    sds = jax.ShapeDtypeStruct

    w_in_g = _all_gather(w_in[0].astype(BF16), "comm_ag_w_in")
    w_glu_g = _all_gather(w_glu[0].astype(BF16), "comm_ag_w_glu").reshape(w_ssm, w_ssm)
    w_out_g = _all_gather(w_out[0].astype(BF16), "comm_ag_w_out").reshape(d_mix, d)
    w_gu_g = _all_gather(jnp.stack([w_gate[0], w_up[0]]).astype(BF16), "comm_ag_w_gu").reshape(2 * NDEV, d, ns_ff)
    w_down_g = _all_gather(w_down[0].astype(BF16), "comm_ag_w_down")

    c_all = _all_gather(c, "comm_ag_c").reshape(NDEV, d)
    n_ada = w_ada.shape[2]
    b_cols = lax.dynamic_slice(b_ada, (0, me * n_ada), (1, n_ada))
    mod_cols = _ada_forward(c_all, w_ada[0], b_cols)
    mod_all = _all_gather(mod_cols, "comm_ag_mod")
    mod = lax.dynamic_slice(mod_all, (0, me, 0), (NDEV, 1, n_ada)).reshape(1, NDEV * n_ada)
    shift_m, scale_m, gate_m, shift_f, scale_f, gate_f = [mod[:, i * d:(i + 1) * d] for i in range(6)]

    gp = n_groups * p_state
    a_re2, a_im2, ldt2 = a_re[0], a_im[0], log_dt[0].reshape(n_groups, 1)
    b_re2, b_im2 = b_re[0].reshape(gp, SSM_GROUP), b_im[0].reshape(gp, SSM_GROUP)
    lam_r, lam_i, coef_r, coef_i = _whole("s5_lam", _s5_lam, [a_re2, a_im2, ldt2], [(n_groups, p_state)] * 4)
    coef_r2, coef_i2 = coef_r.reshape(gp, 1), coef_i.reshape(gp, 1)
    bb_r, bb_i = _whole("s5_bbar", _s5_bbar, [coef_r2, coef_i2, b_re2, b_im2], [(gp, SSM_GROUP)] * 2)
    s_blk = GROUPS_PER_BLOCK * p_state
    b_blk_r = _to_b_blocks(bb_r.reshape(n_groups, p_state, SSM_GROUP), n_blk, p_state).astype(BF16)
    b_blk_i = _to_b_blocks(bb_i.reshape(n_groups, p_state, SSM_GROUP), n_blk, p_state).astype(BF16)
    c_blk_r = _to_c_blocks(c_re[0], n_blk, p_state).astype(BF16)
    c_blk_i = _to_c_blocks(c_im[0], n_blk, p_state).astype(BF16)
    lam_r3, lam_i3 = lam_r.reshape(n_blk, 1, s_blk), lam_i.reshape(n_blk, 1, s_blk)
    d_skip2 = d_skip[0].reshape(1, w_ssm)

    row_d, vec_d = _row_spec(rt, d), _vec_spec(d)
    xm, = _rowwise_fwd("seg_in", lambda *a: _seg_in(*a)[:1], [xs], [row_d], [shift_m, scale_m, g_mix], [vec_d] * 3,
                       [sds((seq, d), BF16)], [row_d], n_rt)
    bn_in = _tile(ns_in, 512)
    per = ns_in // bn_in
    bm, bk = _tile(seq, BM), _tile(d, BK)
    proj = _mm("mm_in", xm, w_in_g, NN, (seq // bm, NDEV * per, d // bk),
               pl.BlockSpec((bm, bk), lambda i, j, k: (i, k)),
               pl.BlockSpec((None, bk, bn_in), lambda i, j, k: (j // per, k, j % per)),
               pl.BlockSpec((bm, bn_in), lambda i, j, k: (i, j)), (seq, NDEV * ns_in), F32, (bm, bn_in))
    q_col, k_col, v_col = w_ssm // w_attn, w_ssm // w_attn + 1, (w_ssm + 2 * w_attn) // LANES
    qg_t, kg_t = jnp.tile(q_gain, (1, n_heads)), jnp.tile(k_gain, (1, n_heads))
    row_a, vec_a = _row_spec(rt, w_attn), _vec_spec(w_attn)
    qk_rows, qk_specs = [proj, proj], [_row_spec(rt, w_attn, q_col), _row_spec(rt, w_attn, k_col)]
    qh, kh = _rowwise_fwd("seg_qk", _seg_qk, qk_rows, qk_specs, [qg_t, kg_t], [vec_a] * 2,
                          [sds((seq, w_attn), F32)] * 2, [row_a] * 2, n_rt)
    t_chunk = _tile(seq, 256)
    ypre, x_re, x_im = _s5_forward(proj, b_blk_r, b_blk_i, c_blk_r, c_blk_i, lam_r3, lam_i3, d_skip2, n_blk, t_chunk)
    tq, tk = _tile(seq, TQ), _tile(seq, TK)
    y_attn = _attention_forward(qh, kh, proj, v_col, n_pair, tq, tk)
    row_s, vec_s = _row_spec(rt, w_ssm), _vec_spec(w_ssm)
    y1, = _rowwise_fwd("seg_gelu", _seg_gelu, [ypre], [row_s], [], [], [sds((seq, w_ssm), F32)], [row_s], n_rt)
    z = _mm_plain("mm_glu", y1, w_glu_g, NN, F32)
    row_m = _row_spec(rt, d_mix)
    mixed, = _rowwise_fwd("seg_mix", _seg_mix, [y1, z, y_attn], [row_s, row_s, row_a], [g_ssm_out, g_attn_out], [vec_s, vec_a],
                          [sds((seq, d_mix), BF16)], [row_m], n_rt)
    o = _mm_plain("mm_out", mixed, w_out_g, NN, F32)
    h1, xf = _rowwise_fwd("seg_mid", _seg_mid, [xs, o], [row_d] * 2, [gate_m, g_ffn, scale_f, shift_f], [vec_d] * 4,
                          [sds((seq, d), F32), sds((seq, d), BF16)], [row_d] * 2, n_rt)
    gu = _mm("mm_gu", xf, w_gu_g, NN, (seq // bm, 2 * NDEV, d // bk),
             pl.BlockSpec((bm, bk), lambda i, j, k: (i, k)), pl.BlockSpec((None, bk, ns_ff), lambda i, j, k: (j, k, 0)),
             pl.BlockSpec((None, bm, ns_ff), lambda i, j, k: (j, i, 0)), (2 * NDEV, seq, ns_ff), F32, (bm, ns_ff))
    gu4 = gu.reshape(NDEV, 2, seq, ns_ff)
    ft = _tile(seq, 512)
    pair_spec = pl.BlockSpec((None, 2, ft, ns_ff), lambda s, i: (s, 0, i, 0))
    one_spec = pl.BlockSpec((None, ft, ns_ff), lambda s, i: (s, i, 0))

    def act_body(gu_ref, a_ref):
        a_ref[...] = _seg_act(gu_ref[0], gu_ref[1])[0].astype(a_ref.dtype)

    act = pl.pallas_call(act_body, name="seg_act", grid=(NDEV, seq // ft), in_specs=[pair_spec], out_specs=one_spec,
                         out_shape=sds((NDEV, seq, ns_ff), BF16), compiler_params=_params(2))(gu4)
    bn_d = _tile(d, BN)
    ffn = _mm("mm_down", act, w_down_g, NN, (seq // bm, d // bn_d, NDEV),
              pl.BlockSpec((None, bm, ns_ff), lambda i, j, k: (k, i, 0)), pl.BlockSpec((None, ns_ff, bn_d), lambda i, j, k: (k, 0, j)),
              pl.BlockSpec((bm, bn_d), lambda i, j, k: (i, j)), (seq, d), F32, (bm, bn_d))
    dy, dffn, d_gate_f, loss_part = _loss_head(h1, ffn, tgt, gate_f, rt)
    loss = lax.psum(loss_part[0, 0], MESH_AXES)

    dact = _mm("mm_dact", dffn, w_down_g, NT, (seq // bm, NDEV, d // bk),
               pl.BlockSpec((bm, bk), lambda i, j, k: (i, k)), pl.BlockSpec((None, ns_ff, bk), lambda i, j, k: (j, 0, k)),
               pl.BlockSpec((None, bm, ns_ff), lambda i, j, k: (j, i, 0)), (NDEV, seq, ns_ff), F32, (bm, ns_ff))
    bl = _tile(seq, BK)
    gw_down = _mm("mm_dw_down", act, dffn, TN, (NDEV, d // bn_d, seq // bl),
                  pl.BlockSpec((None, bl, ns_ff), lambda i, j, k: (i, k, 0)), pl.BlockSpec((bl, bn_d), lambda i, j, k: (k, j)),
                  pl.BlockSpec((None, ns_ff, bn_d), lambda i, j, k: (i, 0, j)), (NDEV, ns_ff, d), BF16, (ns_ff, bn_d))

    def dact_body(gu_ref, da_ref, dgu_ref):
        _, vjp = jax.vjp(_seg_act, gu_ref[0], gu_ref[1])
        dg, du_ = vjp((da_ref[...],))
        dgu_ref[0] = dg.astype(dgu_ref.dtype)
        dgu_ref[1] = du_.astype(dgu_ref.dtype)

    dgu4 = pl.pallas_call(dact_body, name="seg_act_bwd", grid=(NDEV, seq // ft), in_specs=[pair_spec, one_spec],
                          out_specs=pair_spec, out_shape=sds((NDEV, 2, seq, ns_ff), BF16), compiler_params=_params(2))(gu4, dact)
    dgu = dgu4.reshape(2 * NDEV, seq, ns_ff)
    dxf = _mm("mm_dxf", dgu, w_gu_g, NT, (seq // bm, d // bn_d, 2 * NDEV),
              pl.BlockSpec((None, bm, ns_ff), lambda i, j, k: (k, i, 0)), pl.BlockSpec((None, bn_d, ns_ff), lambda i, j, k: (k, j, 0)),
              pl.BlockSpec((bm, bn_d), lambda i, j, k: (i, j)), (seq, d), F32, (bm, bn_d))
    bmd = _tile(d, BM)
    gw_gu = _mm("mm_dw_gu", xf, dgu, TN, (d // bmd, 2 * NDEV, seq // bl),
                pl.BlockSpec((bl, bmd), lambda i, j, k: (k, i)), pl.BlockSpec((None, bl, ns_ff), lambda i, j, k: (j, k, 0)),
                pl.BlockSpec((None, bmd, ns_ff), lambda i, j, k: (j, i, 0)), (2 * NDEV, d, ns_ff), BF16, (bmd, ns_ff))
    (do, dx_a, d_gate_m, d_g_ffn, d_scale_f, d_shift_f) = _rowwise_bwd(
        "seg_mid_bwd", _seg_mid, [xs, o], [row_d] * 2, [gate_m, g_ffn, scale_f, shift_f], [vec_d] * 4,
        [dy, dxf], [row_d] * 2, [[0], [1]], [1, 0], [sds((seq, d), BF16), sds((seq, d), F32)], [row_d] * 2,
        [0, 1, 2, 3], [sds((1, d), F32)] * 4, [vec_d] * 4, n_rt)

    dmixed = _mm_plain("mm_dmixed", do, w_out_g, NT, F32)
    gw_out = _mm_plain("mm_dw_out", mixed, do, TN, BF16)
    (dz, dy1_a, dy_attn, d_g_ssm, d_g_attn) = _rowwise_bwd(
        "seg_mix_bwd", _seg_mix, [y1, z, y_attn], [row_s, row_s, row_a], [g_ssm_out, g_attn_out], [vec_s, vec_a],
        [dmixed], [row_m], [[0]], [1, 0, 2], [sds((seq, w_ssm), BF16), sds((seq, w_ssm), F32), sds((seq, w_attn), F32)],
        [row_s, row_s, row_a], [0, 1], [sds((1, w_ssm), F32), sds((1, w_attn), F32)], [vec_s, vec_a], n_rt)
    dy1_b = _mm_plain("mm_dy1", dz, w_glu_g, NT, F32)
    gw_glu = _mm_plain("mm_dw_glu", y1, dz, TN, BF16)
    (dypre,) = _rowwise_bwd("seg_gelu_bwd", _seg_gelu, [ypre], [row_s], [], [], [dy1_a, dy1_b], [row_s] * 2, [[0, 1]],
                            [0], [sds((seq, w_ssm), F32)], [row_s], [], [], [], n_rt)
    (du, db_blk_r, db_blk_i, dc_blk_r, dc_blk_i, dlam_r3, dlam_i3, dd_skip2) = _s5_backward(
        dypre, proj, x_re, x_im, b_blk_r, b_blk_i, c_blk_r, c_blk_i, lam_r3, lam_i3, d_skip2, n_blk, t_chunk)
    dqh, dkh, dv = _attention_backward(qh, kh, proj, v_col, y_attn, dy_attn, n_pair, tq, tk)
    (dq, dk, dqg_t, dkg_t) = _rowwise_bwd(
        "seg_qk_bwd", _seg_qk, qk_rows, qk_specs, [qg_t, kg_t], [vec_a] * 2, [dqh, dkh], [row_a] * 2, [[0], [1]],
        [0, 1], [sds((seq, w_attn), BF16)] * 2, [row_a] * 2, [0, 1], [sds((1, w_attn), F32)] * 2, [vec_a] * 2, n_rt)
    dproj = jnp.concatenate([du.astype(BF16), dq, dk, dv.astype(BF16)], axis=-1)
    bk_in = _tile(ns_in, BK)
    per_k = ns_in // bk_in
    dxm = _mm("mm_dxm", dproj, w_in_g, NT, (seq // bm, d // bn_d, NDEV * per_k),
              pl.BlockSpec((bm, bk_in), lambda i, j, k: (i, k)),
              pl.BlockSpec((None, bn_d, bk_in), lambda i, j, k: (k // per_k, j, k % per_k)),
              pl.BlockSpec((bm, bn_d), lambda i, j, k: (i, j)), (seq, d), F32, (bm, bn_d))
    gw_in = _mm("mm_dw_in", xm, dproj, TN, (d // bmd, NDEV * per, seq // bl),
                pl.BlockSpec((bl, bmd), lambda i, j, k: (k, i)), pl.BlockSpec((bl, bn_in), lambda i, j, k: (k, j)),
                pl.BlockSpec((None, bmd, bn_in), lambda i, j, k: (j // per, i, j % per)), (NDEV, d, ns_in), BF16, (bmd, bn_in))
    (grad_x, d_shift_m, d_scale_m, d_g_mix) = _rowwise_bwd(
        "seg_in_bwd", _seg_in, [xs], [row_d], [shift_m, scale_m, g_mix], [vec_d] * 3, [dxm, dx_a], [row_d] * 2, [[0], [1]],
        [0], [sds((seq, d), F32)], [row_d], [0, 1, 2], [sds((1, d), F32)] * 3, [vec_d] * 3, n_rt)

    dbb_r = _from_b_blocks(db_blk_r, n_blk, p_state).reshape(gp, SSM_GROUP)
    dbb_i = _from_b_blocks(db_blk_i, n_blk, p_state).reshape(gp, SSM_GROUP)
    dcoef_r2, dcoef_i2, db_re2, db_im2 = _whole_vjp("s5_bbar_bwd", _s5_bbar, [coef_r2, coef_i2, b_re2, b_im2], [dbb_r, dbb_i],
                                                    [(gp, 1), (gp, 1), (gp, SSM_GROUP), (gp, SSM_GROUP)])
    lam_cts = [dlam_r3.reshape(n_groups, p_state), dlam_i3.reshape(n_groups, p_state),
               dcoef_r2.reshape(n_groups, p_state), dcoef_i2.reshape(n_groups, p_state)]
    da_re2, da_im2, dldt2 = _whole_vjp("s5_lam_bwd", _s5_lam, [a_re2, a_im2, ldt2], lam_cts,
                                       [(n_groups, p_state), (n_groups, p_state), (n_groups, 1)])
    dc_re2, dc_im2 = _from_c_blocks(dc_blk_r, n_blk, p_state), _from_c_blocks(dc_blk_i, n_blk, p_state)

    dmod = jnp.concatenate([d_shift_m, d_scale_m, d_gate_m, d_shift_f, d_scale_f, d_gate_f], axis=-1)
    small_part = {
        "b_ada": dmod, "g_mix": d_g_mix, "a_re": da_re2, "a_im": da_im2, "log_dt": dldt2, "b_re": db_re2, "b_im": db_im2,
        "c_re": dc_re2, "c_im": dc_im2, "d_skip": dd_skip2,
        "q_gain": dqg_t.reshape(n_heads, HEAD_DIM).sum(0), "k_gain": dkg_t.reshape(n_heads, HEAD_DIM).sum(0),
        "g_ssm_out": d_g_ssm, "g_attn_out": d_g_attn, "g_ffn": d_g_ffn,
    }
    packed_parts = _all_gather(_pack([small_part[n] for n in SMALL]), "comm_ag_small")
    rows_p = packed_parts.shape[1]
    tr_p = _tile(rows_p, 64)
    small_spec = pl.BlockSpec((NDEV, tr_p, PACK_COLS), lambda i: (0, i, 0))
    sm = _adam_sum("adam_small", packed_parts, small_spec, _pack([given[n] for n in SMALL]),
                   _pack([given["m_" + n] for n in SMALL]), _pack([given["v_" + n] for n in SMALL]), tr_p)
    like = [given[n] for n in SMALL]
    small_out = [dict(zip(SMALL, _unpack(t, like))) for t in sm]

    dmod_all = packed_parts[:, :(6 * d) // PACK_COLS, :].reshape(NDEV, 6 * d) if (6 * d) % PACK_COLS == 0 else None
    assert dmod_all is not None
    dmod_cols = lax.dynamic_slice(dmod_all, (0, me * n_ada), (NDEV, n_ada))
    big = {"w_ada": _adam_ada(c_all, dmod_cols, w_ada[0], m_w_ada[0], v_w_ada[0])}

    def sharded(name, partial, part_spec_fn, tr_pref):
        got = _all_to_all(partial, "comm_a2a_" + name)
        w = given[name][0]
        tr = _tile(w.shape[0], tr_pref)
        return got, tr

    got, tr = sharded("w_in", gw_in, None, 256)
    big["w_in"] = _adam_sum("adam_w_in", got, pl.BlockSpec((NDEV, tr, ns_in), lambda i: (0, i, 0)), w_in[0], m_w_in[0], v_w_in[0], tr)
    r_glu = w_glu.shape[1]
    got, tr = sharded("w_glu", gw_glu.reshape(NDEV, r_glu, w_ssm), None, 128)
    big["w_glu"] = _adam_sum("adam_w_glu", got, pl.BlockSpec((NDEV, tr, w_ssm), lambda i: (0, i, 0)), w_glu[0], m_w_glu[0], v_w_glu[0], tr)
    r_out = w_out.shape[1]
    got, tr = sharded("w_out", gw_out.reshape(NDEV, r_out, d), None, 128)
    big["w_out"] = _adam_sum("adam_w_out", got, pl.BlockSpec((NDEV, tr, d), lambda i: (0, i, 0)), w_out[0], m_w_out[0], v_w_out[0], tr)
    got_gu = _all_to_all(gw_gu.reshape(NDEV, 2, d, ns_ff), "comm_a2a_w_gu")
    tr = _tile(d, 256)
    for which, nm in enumerate(("w_gate", "w_up")):
        spec = pl.BlockSpec((NDEV, None, tr, ns_ff), lambda i, which=which: (0, which, i, 0))
        big[nm] = _adam_sum("adam_" + nm, got_gu, spec, given[nm][0], given["m_" + nm][0], given["v_" + nm][0], tr)
    got, tr = sharded("w_down", gw_down, None, 64)
    big["w_down"] = _adam_sum("adam_w_down", got, pl.BlockSpec((NDEV, tr, d), lambda i: (0, i, 0)), w_down[0], m_w_down[0], v_w_down[0], tr)

    order = ("w_ada", "b_ada", "g_mix", "w_in", "a_re", "a_im", "log_dt", "b_re", "b_im", "c_re", "c_im", "d_skip", "w_glu",
             "q_gain", "k_gain", "g_ssm_out", "g_attn_out", "w_out", "g_ffn", "w_gate", "w_up", "w_down")
    outs = [loss, grad_x[None]]
    for kind in range(4):
        for n in order:
            outs.append(big[n][kind][None] if n in big else small_out[kind][n])
    return tuple(outs)
```

```python
import functools
import math

import jax
import jax.numpy as jnp
from jax import lax
from jax.experimental import pallas as pl
from jax.experimental.pallas import tpu as pltpu

F32 = jnp.float32
BF16 = jnp.bfloat16
NDEV = 8
MESH_AXES = ("x", "y", "c")
MESH_ID = pl.DeviceIdType.MESH
EPS = 1e-6
LANES = 128
SUBLANES = 8
HEAD_DIM = 64
SSM_GROUP = 16
ADAM_LR, ADAM_B1, ADAM_B2, ADAM_EPS, ADAM_WD, ADAM_STEP = 0.001, 0.9, 0.999, 1e-08, 0.01, 10

NN = (((1,), (0,)), ((), ()))
NT = (((1,), (1,)), ((), ()))
TN = (((0,), (0,)), ((), ()))


def _dot(a, b, dn=NN):
    return lax.dot_general(a, b, dn, preferred_element_type=F32)


def _tile(dim, pref):
    t = min(dim, pref)
    while dim % t:
        t //= 2
    return t


def _params(n):
    return pltpu.CompilerParams(dimension_semantics=("arbitrary",) * n)


def _me():
    mx, my, mc = lax.axis_index("x"), lax.axis_index("y"), lax.axis_index("c")
    return mx, my, mc


def _peer(mx, my, mc, k):
    px = 1 - mx if (k >> 2) & 1 else mx
    py = 1 - my if (k >> 1) & 1 else my
    pc = 1 - mc if k & 1 else mc
    return (px, py, pc), 4 * px + 2 * py + pc


def _exchange_copies(x_ref, land_ref, send_sems, recv_sems, gather):
    mx, my, mc = _me()
    me = 4 * mx + 2 * my + mc
    pairs = []
    for k in range(1, NDEV):
        peer, pidx = _peer(mx, my, mc, k)
        src = x_ref if gather else x_ref.at[pidx]
        mk = lambda dst, src=src, k=k, peer=peer: pltpu.make_async_remote_copy(
            src_ref=src, dst_ref=dst, send_sem=send_sems.at[k - 1], recv_sem=recv_sems.at[k - 1],
            device_id=peer, device_id_type=MESH_ID)
        pairs.append((mk(land_ref.at[me]), mk(land_ref.at[pidx])))
    return me, pairs


def _exchange(x, gather, name):
    def body(x_ref, o_ref, send_sems, recv_sems, local_sem):
        me, pairs = _exchange_copies(x_ref, o_ref, send_sems, recv_sems, gather)
        local = pltpu.make_async_copy(x_ref if gather else x_ref.at[me], o_ref.at[me], local_sem)
        local.start()
        for send, _ in pairs:
            send.start()
        for _, arrival in pairs:
            arrival.wait_recv()
        for send, _ in pairs:
            send.wait_send()
        local.wait()

    return pl.pallas_call(
        body, name=name, out_shape=jax.ShapeDtypeStruct(((NDEV,) + x.shape) if gather else x.shape, x.dtype),
        in_specs=[pl.BlockSpec(memory_space=pl.ANY)], out_specs=pl.BlockSpec(memory_space=pl.ANY),
        scratch_shapes=[pltpu.SemaphoreType.DMA((NDEV - 1,)), pltpu.SemaphoreType.DMA((NDEV - 1,)), pltpu.SemaphoreType.DMA],
    )(x)


HBM_SPEC = pl.BlockSpec(memory_space=pltpu.HBM)
SEM_SPEC = pl.BlockSpec(memory_space=pltpu.SEMAPHORE)
SIDE_EFFECT = pltpu.SideEffectType.DATAFLOW_SIDE_EFFECTING


def _exchange_start(x, gather, name):
    land_shape = ((NDEV,) + x.shape) if gather else x.shape

    def body(x_ref, land_ref, send_sems, recv_sems, x_thru, land_thru, token, local_sem):
        me, pairs = _exchange_copies(x_ref, land_ref, send_sems, recv_sems, gather)
        local = pltpu.make_async_copy(x_ref if gather else x_ref.at[me], land_ref.at[me], local_sem)
        local.start()
        for send, _ in pairs:
            send.start()
        local.wait()
        token[...] = jnp.zeros_like(token)

    sems = pltpu.SemaphoreType.DMA((NDEV - 1,))
    send_sems, recv_sems, x_thru, land_thru, token = pl.pallas_call(
        body, name=name + "_start",
        out_shape=(sems, sems, pltpu.HBM(x.shape, x.dtype), pltpu.HBM(land_shape, x.dtype),
                   jax.ShapeDtypeStruct((SUBLANES, LANES), F32)),
        in_specs=(HBM_SPEC, HBM_SPEC), out_specs=(SEM_SPEC, SEM_SPEC, HBM_SPEC, HBM_SPEC, pl.BlockSpec(memory_space=pltpu.VMEM)),
        input_output_aliases={0: 2, 1: 3}, scratch_shapes=[pltpu.SemaphoreType.DMA],
        compiler_params=pltpu.CompilerParams(has_side_effects=SIDE_EFFECT),
    )(pltpu.with_memory_space_constraint(x, pltpu.HBM),
      pltpu.with_memory_space_constraint(lax.empty(land_shape, x.dtype), pltpu.HBM))
    return (send_sems, recv_sems, x_thru, land_thru, gather, name), token


def _exchange_finish(handle, after):
    send_sems, recv_sems, x_thru, land_thru, gather, name = handle

    def body(x_ref, land_ref, send_sems, recv_sems, after_ref, x_dead, got_ref):
        _, pairs = _exchange_copies(x_ref, land_ref, send_sems, recv_sems, gather)
        for send, arrival in pairs:
            send.wait_send()
            arrival.wait_recv()

    return pl.pallas_call(
        body, name=name + "_wait",
        out_shape=(pltpu.HBM(x_thru.shape, x_thru.dtype), pltpu.HBM(land_thru.shape, land_thru.dtype)),
        in_specs=(HBM_SPEC, HBM_SPEC, SEM_SPEC, SEM_SPEC, HBM_SPEC), out_specs=(HBM_SPEC, HBM_SPEC),
        input_output_aliases={0: 0, 1: 1}, compiler_params=pltpu.CompilerParams(has_side_effects=SIDE_EFFECT),
    )(x_thru, land_thru, send_sems, recv_sems, pltpu.with_memory_space_constraint(after, pltpu.HBM))[1]


def _mm(name, a, b, dn, grid, a_spec, b_spec, o_spec, out_shape, out_dtype, acc_shape):
    nk = grid[2]

    def body(a_ref, b_ref, o_ref, acc_ref):
        k = pl.program_id(2)

        @pl.when(k == 0)
        def _():
            acc_ref[...] = jnp.zeros_like(acc_ref)

        acc_ref[...] += _dot(a_ref[...].astype(BF16), b_ref[...].astype(BF16), dn)

        @pl.when(k == nk - 1)
        def _():
            o_ref[...] = acc_ref[...].astype(o_ref.dtype)

    return pl.pallas_call(
        body, name=name, grid=grid, in_specs=[a_spec, b_spec], out_specs=o_spec,
        out_shape=jax.ShapeDtypeStruct(out_shape, out_dtype), scratch_shapes=[pltpu.VMEM(acc_shape, F32)],
        compiler_params=_params(3),
    )(a, b)


BM, BN, BK = 1024, 1024, 512


def _mm_plain(name, a, b, dn, out_dtype):
    if dn == NN:
        (m, kk), n = a.shape, b.shape[1]
    elif dn == NT:
        (m, kk), n = a.shape, b.shape[0]
    else:
        (kk, m), n = a.shape, b.shape[1]
    bm, bn, bk = _tile(m, BM), _tile(n, BN), _tile(kk, BK)
    a_spec = pl.BlockSpec((bk, bm), lambda i, j, k: (k, i)) if dn == TN else pl.BlockSpec((bm, bk), lambda i, j, k: (i, k))
    b_spec = pl.BlockSpec((bn, bk), lambda i, j, k: (j, k)) if dn == NT else pl.BlockSpec((bk, bn), lambda i, j, k: (k, j))
    return _mm(name, a, b, dn, (m // bm, n // bn, kk // bk), a_spec, b_spec,
               pl.BlockSpec((bm, bn), lambda i, j, k: (i, j)), (m, n), out_dtype, (bm, bn))


def _row_spec(tile, width, col=0):
    return pl.BlockSpec((tile, width), lambda i: (i, col))


def _vec_spec(width, col=0):
    return pl.BlockSpec((1, width), lambda i: (0, col))


def _rowwise_fwd(name, fn, rows, row_specs, vecs, vec_specs, out_shapes, out_specs, n_tiles):
    nr, nv = len(rows), len(vecs)

    def body(*refs):
        ins = [r[...].astype(F32) for r in refs[:nr + nv]]
        outs = fn(*ins)
        for o_ref, o in zip(refs[nr + nv:], outs):
            o_ref[...] = o.astype(o_ref.dtype)

    return pl.pallas_call(body, name=name, grid=(n_tiles,), in_specs=list(row_specs) + list(vec_specs),
                          out_specs=list(out_specs), out_shape=list(out_shapes), compiler_params=_params(1))(*rows, *vecs)


def _rowwise_bwd(name, fn, rows, row_specs, vecs, vec_specs, cts, ct_specs, ct_groups,
                 drow_idx, drow_shapes, drow_specs, dvec_idx, dvec_shapes, dvec_specs, n_tiles):
    nr, nv, nc = len(rows), len(vecs), len(cts)

    def body(*refs):
        ins = [r[...].astype(F32) for r in refs[:nr + nv]]
        ct_vals = [r[...].astype(F32) for r in refs[nr + nv:nr + nv + nc]]
        out_refs = refs[nr + nv + nc:]
        _, vjp = jax.vjp(fn, *ins)
        grads = vjp(tuple(functools.reduce(lambda p, q: p + q, [ct_vals[j] for j in grp]) for grp in ct_groups))
        for o_ref, idx in zip(out_refs[:len(drow_idx)], drow_idx):
            o_ref[...] = grads[idx].astype(o_ref.dtype)
        step = pl.program_id(0)
        for o_ref, idx in zip(out_refs[len(drow_idx):], dvec_idx):
            @pl.when(step == 0)
            def _(o_ref=o_ref):
                o_ref[...] = jnp.zeros_like(o_ref)
            o_ref[...] += grads[nr + idx]

    return pl.pallas_call(body, name=name, grid=(n_tiles,),
                          in_specs=list(row_specs) + list(vec_specs) + list(ct_specs),
                          out_specs=list(drow_specs) + list(dvec_specs),
                          out_shape=list(drow_shapes) + list(dvec_shapes), compiler_params=_params(1))(*rows, *vecs, *cts)


def _rms(x):
    return x * lax.rsqrt(jnp.mean(x * x, axis=-1, keepdims=True) + EPS)


def _seg_in(x, shift, scale, gain):
    return _rms(x) * gain * (1.0 + scale) + shift, x


def _seg_qk(q, k, qg, kg):
    def norm(t, g, mult):
        blocks = []
        lane = lax.broadcasted_iota(jnp.int32, (1, LANES), 1)
        for p in range(t.shape[1] // LANES):
            tb = t[:, p * LANES:(p + 1) * LANES]
            sq = tb * tb
            lo = jnp.sum(jnp.where(lane < HEAD_DIM, sq, 0.0), axis=-1, keepdims=True)
            hi = jnp.sum(jnp.where(lane < HEAD_DIM, 0.0, sq), axis=-1, keepdims=True)
            ms = jnp.where(lane < HEAD_DIM, lo, hi) * (1.0 / HEAD_DIM)
            blocks.append(tb * lax.rsqrt(ms + EPS) * (g[:, p * LANES:(p + 1) * LANES] * mult))
        return jnp.concatenate(blocks, axis=-1) if len(blocks) > 1 else blocks[0]
    return norm(q, qg, 1.0 / math.sqrt(HEAD_DIM)), norm(k, kg, 1.0)


def _seg_gelu(ypre):
    return (jax.nn.gelu(ypre),)


def _seg_mix(y1, z, yattn, g_ssm, g_attn):
    ys = y1 * jax.nn.sigmoid(z)
    return (jnp.concatenate([_rms(ys) * g_ssm, _rms(yattn) * g_attn], axis=-1),)


def _seg_mid(x, o, gate_m, g_ffn, scale_f, shift_f):
    h1 = x + gate_m * o
    return h1, _rms(h1) * g_ffn * (1.0 + scale_f) + shift_f


def _seg_act(gate, up):
    return (jax.nn.silu(gate) * up,)


def _s5_lam(a_re, a_im, log_dt):
    dt = jnp.exp(log_dt)
    mag = jnp.exp(a_re * dt)
    lr, li = mag * jnp.cos(a_im * dt), mag * jnp.sin(a_im * dt)
    den = a_re * a_re + a_im * a_im
    nr, ni = lr - 1.0, li
    return lr, li, (nr * a_re + ni * a_im) / den, (ni * a_re - nr * a_im) / den


def _s5_bbar(coef_re, coef_im, b_re, b_im):
    return coef_re * b_re - coef_im * b_im, coef_re * b_im + coef_im * b_re


def _whole(name, fn, ins, out_shapes):
    n = len(ins)

    def body(*refs):
        outs = fn(*[r[...] for r in refs[:n]])
        for o_ref, o in zip(refs[n:], outs):
            o_ref[...] = o

    return pl.pallas_call(body, name=name, out_shape=[jax.ShapeDtypeStruct(s, F32) for s in out_shapes])(*ins)


def _whole_vjp(name, fn, ins, cts, out_shapes):
    n, nc = len(ins), len(cts)

    def body(*refs):
        _, vjp = jax.vjp(fn, *[r[...] for r in refs[:n]])
        grads = vjp(tuple(r[...] for r in refs[n:n + nc]))
        for o_ref, g in zip(refs[n + nc:], grads):
            o_ref[...] = g

    return pl.pallas_call(body, name=name, out_shape=[jax.ShapeDtypeStruct(s, F32) for s in out_shapes])(*ins, *cts)


SCAN_SHIFTS = (1, 2, 4)


def _cmul(ar, ai, br, bi):
    return ar * br - ai * bi, ar * bi + ai * br


def _scan_coefs(lr, li, reverse):
    s = lr.shape[1]
    row = lax.broadcasted_iota(jnp.int32, (SUBLANES, s), 0)
    p1 = (lr, li)
    p2 = _cmul(*p1, *p1)
    p4 = _cmul(*p2, *p2)
    p8 = _cmul(*p4, *p4)
    p3, p5, p6 = _cmul(*p1, *p2), _cmul(*p4, *p1), _cmul(*p4, *p2)
    p7 = _cmul(*p6, *p1)
    pows = (p1, p2, p3, p4, p5, p6, p7, p8)
    bc = lambda t: jnp.broadcast_to(t, (SUBLANES, s))
    steps = []
    for sh, pw in zip(SCAN_SHIFTS, (p1, p2, p4)):
        keep = (row + sh <= SUBLANES - 1) if reverse else (row >= sh)
        steps.append((jnp.where(keep, bc(pw[0]), 0.0), jnp.where(keep, bc(pw[1]), 0.0)))
    cr, ci = jnp.zeros((SUBLANES, s), F32), jnp.zeros((SUBLANES, s), F32)
    for r in range(SUBLANES):
        pw = pows[SUBLANES - 1 - r] if reverse else pows[r]
        cr = jnp.where(row == r, bc(pw[0]), cr)
        ci = jnp.where(row == r, bc(pw[1]), ci)
    return steps, (cr, ci)


def _scan_tile(xr, xi, steps, carry_pow, cr, ci, reverse):
    for sh, (ar, ai) in zip(SCAN_SHIFTS, steps):
        rs = SUBLANES - sh if reverse else sh
        sr, si = pltpu.roll(xr, rs, 0), pltpu.roll(xi, rs, 0)
        xr, xi = xr + ar * sr - ai * si, xi + ar * si + ai * sr
    pr, pi = carry_pow
    return xr + pr * cr - pi * ci, xi + pr * ci + pi * cr


def _s5_forward(proj, b_blk_re, b_blk_im, c_blk_re, c_blk_im, lam_re, lam_im, d_skip, n_blk, t_chunk):
    seq = proj.shape[0]
    n_chunks = seq // t_chunk
    n_tiles = t_chunk // SUBLANES
    s = b_blk_re.shape[2]

    def body(u_ref, bre_ref, bim_ref, cre_ref, cim_ref, lr_ref, li_ref, d_ref, y_ref, xr_ref, xi_ref, wr, wi, carry):
        t = pl.program_id(1)

        @pl.when(t == 0)
        def _():
            carry[...] = jnp.zeros_like(carry)

        u = u_ref[...]
        ub = u.astype(BF16)
        wr[...] = _dot(ub, bre_ref[...])
        wi[...] = _dot(ub, bim_ref[...])
        steps, cpow = _scan_coefs(lr_ref[...], li_ref[...], False)

        def tile(i, c):
            r0 = pl.multiple_of(i * SUBLANES, SUBLANES)
            xr, xi = _scan_tile(wr[pl.ds(r0, SUBLANES), :], wi[pl.ds(r0, SUBLANES), :], steps, cpow, c[0], c[1], False)
            xr_ref[pl.ds(r0, SUBLANES), :] = xr
            xi_ref[pl.ds(r0, SUBLANES), :] = xi
            last = SUBLANES - 1
            return (jnp.broadcast_to(xr[last:, :], xr.shape), jnp.broadcast_to(xi[last:, :], xi.shape))

        cr, ci = lax.fori_loop(0, n_tiles, tile, (carry[0], carry[1]))
        carry[0] = cr
        carry[1] = ci
        y = _dot(xr_ref[...].astype(BF16), cre_ref[...]) - _dot(xi_ref[...].astype(BF16), cim_ref[...])
        y_ref[...] = y + d_ref[...] * u

    blk = lambda shape: pl.BlockSpec((None,) + shape, lambda j, t: (j, 0, 0))
    return pl.pallas_call(
        body, name="s5_fwd", grid=(n_blk, n_chunks),
        in_specs=[pl.BlockSpec((t_chunk, LANES), lambda j, t: (t, j)), blk((LANES, s)), blk((LANES, s)),
                  blk((s, LANES)), blk((s, LANES)), blk((1, s)), blk((1, s)), pl.BlockSpec((1, LANES), lambda j, t: (0, j))],
        out_specs=[pl.BlockSpec((t_chunk, LANES), lambda j, t: (t, j)), pl.BlockSpec((t_chunk, s), lambda j, t: (t, j)),
                   pl.BlockSpec((t_chunk, s), lambda j, t: (t, j))],
        out_shape=[jax.ShapeDtypeStruct((seq, n_blk * LANES), F32), jax.ShapeDtypeStruct((seq, n_blk * s), F32),
                   jax.ShapeDtypeStruct((seq, n_blk * s), F32)],
        scratch_shapes=[pltpu.VMEM((t_chunk, s), F32), pltpu.VMEM((t_chunk, s), F32), pltpu.VMEM((2, SUBLANES, s), F32)],
        compiler_params=_params(2),
    )(proj, b_blk_re, b_blk_im, c_blk_re, c_blk_im, lam_re, lam_im, d_skip)


def _s5_backward(dypre, proj, x_re, x_im, b_blk_re, b_blk_im, c_blk_re, c_blk_im, lam_re, lam_im, d_skip, n_blk, t_chunk):
    seq = proj.shape[0]
    n_chunks = seq // t_chunk
    n_tiles = t_chunk // SUBLANES
    s = b_blk_re.shape[2]

    def body(dy_ref, u_ref, xr_ref, xi_ref, pr_ref, pi_ref, bre_ref, bim_ref, cre_ref, cim_ref, lr_ref, li_ref, d_ref,
             du_ref, dbre_ref, dbim_ref, dcre_ref, dcim_ref, dlr_ref, dli_ref, dd_ref, gr, gi, carry):
        t = pl.program_id(1)

        @pl.when(t == 0)
        def _():
            carry[...] = jnp.zeros_like(carry)
            for r in (dbre_ref, dbim_ref, dcre_ref, dcim_ref, dlr_ref, dli_ref, dd_ref):
                r[...] = jnp.zeros_like(r)

        dy = dy_ref[...]
        dyb = dy.astype(BF16)
        u = u_ref[...]
        gr[...] = _dot(dyb, cre_ref[...], NT)
        gi[...] = -_dot(dyb, cim_ref[...], NT)
        steps, cpow = _scan_coefs(lr_ref[...], -li_ref[...], True)
        row = lax.broadcasted_iota(jnp.int32, (SUBLANES, s), 0)
        last = SUBLANES - 1
        first_chunk = t == n_chunks - 1

        def tile_at(r0, prev_r, prev_i, c):
            cr, ci, ar, ai = c
            lr_, li_ = _scan_tile(gr[pl.ds(r0, SUBLANES), :], gi[pl.ds(r0, SUBLANES), :], steps, cpow, cr, ci, True)
            gr[pl.ds(r0, SUBLANES), :] = lr_
            gi[pl.ds(r0, SUBLANES), :] = li_
            xr, xi = xr_ref[pl.ds(r0, SUBLANES), :], xi_ref[pl.ds(r0, SUBLANES), :]
            xpr = jnp.where(row == 0, jnp.broadcast_to(prev_r[last:, :], xr.shape), pltpu.roll(xr, 1, 0))
            xpi = jnp.where(row == 0, jnp.broadcast_to(prev_i[last:, :], xi.shape), pltpu.roll(xi, 1, 0))
            ar = ar + lr_ * xpr + li_ * xpi
            ai = ai + li_ * xpr - lr_ * xpi
            return (jnp.broadcast_to(lr_[:1, :], lr_.shape), jnp.broadcast_to(li_[:1, :], li_.shape), ar, ai)

        def tile(ii, c):
            i = n_tiles - 1 - ii
            r0 = pl.multiple_of(i * SUBLANES, SUBLANES)
            rp = pl.multiple_of(r0 - SUBLANES, SUBLANES)
            return tile_at(r0, xr_ref[pl.ds(rp, SUBLANES), :], xi_ref[pl.ds(rp, SUBLANES), :], c)

        zero = jnp.zeros((SUBLANES, s), F32)
        c = lax.fori_loop(0, n_tiles - 1, tile, (carry[0], carry[1], zero, zero))
        keep = jnp.where(first_chunk, 0.0, 1.0)
        c = tile_at(0, pr_ref[...] * keep, pi_ref[...] * keep, c)
        carry[0] = c[0]
        carry[1] = c[1]
        dlr_ref[...] += jnp.sum(c[2], axis=0, keepdims=True)
        dli_ref[...] += jnp.sum(c[3], axis=0, keepdims=True)

        lam_r, lam_i = gr[...].astype(BF16), gi[...].astype(BF16)
        du_ref[...] = _dot(lam_r, bre_ref[...], NT) + _dot(lam_i, bim_ref[...], NT) + d_ref[...] * dy
        ub = u.astype(BF16)
        dbre_ref[...] += _dot(ub, lam_r, TN)
        dbim_ref[...] += _dot(ub, lam_i, TN)
        dcre_ref[...] += _dot(xr_ref[...].astype(BF16), dyb, TN)
        dcim_ref[...] -= _dot(xi_ref[...].astype(BF16), dyb, TN)
        dd_ref[...] += jnp.sum(dy * u, axis=0, keepdims=True)

    rev = lambda t: n_chunks - 1 - t
    blk = lambda shape: pl.BlockSpec((None,) + shape, lambda j, t: (j, 0, 0))
    tpc = t_chunk // SUBLANES
    prev_spec = pl.BlockSpec((SUBLANES, s), lambda j, t: (jnp.maximum(rev(t) * tpc - 1, 0), j))
    chunk = lambda w: pl.BlockSpec((t_chunk, w), lambda j, t: (rev(t), j))
    return pl.pallas_call(
        body, name="s5_bwd", grid=(n_blk, n_chunks),
        in_specs=[chunk(LANES), chunk(LANES), chunk(s), chunk(s), prev_spec, prev_spec, blk((LANES, s)), blk((LANES, s)),
                  blk((s, LANES)), blk((s, LANES)), blk((1, s)), blk((1, s)), pl.BlockSpec((1, LANES), lambda j, t: (0, j))],
        out_specs=[chunk(LANES), blk((LANES, s)), blk((LANES, s)), blk((s, LANES)), blk((s, LANES)), blk((1, s)), blk((1, s)),
                   pl.BlockSpec((1, LANES), lambda j, t: (0, j))],
        out_shape=[jax.ShapeDtypeStruct((seq, n_blk * LANES), F32),
                   jax.ShapeDtypeStruct((n_blk, LANES, s), F32), jax.ShapeDtypeStruct((n_blk, LANES, s), F32),
                   jax.ShapeDtypeStruct((n_blk, s, LANES), F32), jax.ShapeDtypeStruct((n_blk, s, LANES), F32),
                   jax.ShapeDtypeStruct((n_blk, 1, s), F32), jax.ShapeDtypeStruct((n_blk, 1, s), F32),
                   jax.ShapeDtypeStruct((1, n_blk * LANES), F32)],
        scratch_shapes=[pltpu.VMEM((t_chunk, s), F32), pltpu.VMEM((t_chunk, s), F32), pltpu.VMEM((2, SUBLANES, s), F32)],
        compiler_params=_params(2),
    )(dypre, proj, x_re, x_im, x_re, x_im, b_blk_re, b_blk_im, c_blk_re, c_blk_im, lam_re, lam_im, d_skip)


TQ, TK = 256, 128


def _split_bf16(x):
    hi = x.astype(BF16)
    return hi, (x - hi.astype(F32)).astype(BF16)


def _sb_weights(z, past, carry, tri):
    ls = jnp.minimum(z, 0.0) - jnp.log(1.0 + jnp.exp(-jnp.abs(z)))
    lk = ls - z
    if past is not None:
        lk = jnp.where(past, lk, 0.0)
    hi, lo = _split_bf16(lk)
    w = jnp.exp(ls + (_dot(hi, tri) + _dot(lo, tri)) + carry)
    if past is not None:
        w = jnp.where(past, w, 0.0)
    return ls, lk, w


def _walk_key_blocks(i, ratio, prologue, block, epilogue):
    n_kb = (i + 1) * ratio
    prologue(n_kb - 1)
    for n in range(ratio):
        block(n_kb - 1 - n, n % 2, True)

    def pair(t, carry):
        j = n_kb - 1 - ratio - 2 * t
        block(j, ratio % 2, False)
        block(j - 1, (ratio + 1) % 2, False)
        return carry

    lax.fori_loop(0, (i * ratio) // 2, pair, 0)
    epilogue()


def _attention_forward(qh, kh, proj, v_col, n_pair, tq, tk):
    seq = qh.shape[0]
    ratio = tq // tk
    assert ratio % 2 == 0

    def body(q_ref, k_ref, v_ref, o_ref, q_scr, z_scr, w_scr, acc_scr, c_scr):
        i = pl.program_id(1)
        lane = lax.broadcasted_iota(jnp.int32, (1, LANES), 1)
        q2 = q_ref[...]
        q_scr[0] = jnp.where(lane < HEAD_DIM, q2, 0.0).astype(BF16)
        q_scr[1] = jnp.where(lane < HEAD_DIM, 0.0, q2).astype(BF16)
        tri = (lax.broadcasted_iota(jnp.int32, (tk, tk), 0) > lax.broadcasted_iota(jnp.int32, (tk, tk), 1)).astype(BF16)
        qpos = i * tq + lax.broadcasted_iota(jnp.int32, (tq, tk), 0)
        kidx = lax.broadcasted_iota(jnp.int32, (tq, tk), 1)

        def rows(ref, j):
            j = jnp.clip(j, 0, seq // tk - 1)
            return ref[pl.ds(pl.multiple_of(j * tk, tk), tk), :].astype(BF16)

        def scores(j, slot):
            kb = rows(k_ref, j)
            for h in range(2):
                z_scr[slot, h] = _dot(q_scr[h], kb, NT)

        def finish(j):
            vb = rows(v_ref, j)
            for h in range(2):
                acc_scr[h] += _dot(w_scr[h], vb)

        def prologue(j):
            w_scr[...] = jnp.zeros_like(w_scr)
            acc_scr[...] = jnp.zeros_like(acc_scr)
            c_scr[...] = jnp.zeros_like(c_scr)
            scores(j, 0)

        def block(j, slot, masked):
            scores(j - 1, 1 - slot)
            finish(j + 1)
            past = ((kidx + j * tk) < qpos) if masked else None
            for h in range(2):
                _, lk, w = _sb_weights(z_scr[slot, h], past, c_scr[h], tri)
                w_scr[h] = w.astype(BF16)
                c_scr[h] += jnp.sum(lk, axis=-1, keepdims=True)

        _walk_key_blocks(i, ratio, prologue, block, lambda: finish(0))
        o_ref[...] = jnp.where(lane < HEAD_DIM, acc_scr[0], acc_scr[1])

    return pl.pallas_call(
        body, name="attn_fwd", grid=(n_pair, seq // tq),
        in_specs=[pl.BlockSpec((tq, LANES), lambda p, i: (i, p)), pl.BlockSpec((seq, LANES), lambda p, i: (0, p)),
                  pl.BlockSpec((seq, LANES), lambda p, i: (0, v_col + p))],
        out_specs=pl.BlockSpec((tq, LANES), lambda p, i: (i, p)),
        out_shape=jax.ShapeDtypeStruct(qh.shape, F32),
        scratch_shapes=[pltpu.VMEM((2, tq, LANES), BF16), pltpu.VMEM((2, 2, tq, tk), F32), pltpu.VMEM((2, tq, tk), BF16),
                        pltpu.VMEM((2, tq, LANES), F32), pltpu.VMEM((2, tq, 1), F32)],
        compiler_params=_params(2),
    )(qh, kh, proj)


def _attention_backward(qh, kh, proj, v_col, y, dy, n_pair, tq, tk):
    seq = qh.shape[0]

    ratio = tq // tk
    assert ratio % 2 == 0

    def body(q_ref, k_ref, v_ref, y_ref, dy_ref, dq_ref, dk_ref, dv_ref,
             q_scr, do_scr, z_scr, dw_scr, w_scr, dz_scr, dq_scr, c_scr, c2_scr, tot_scr):
        i = pl.program_id(1)

        @pl.when(i == 0)
        def _():
            dk_ref[...] = jnp.zeros_like(dk_ref)
            dv_ref[...] = jnp.zeros_like(dv_ref)

        lane = lax.broadcasted_iota(jnp.int32, (1, LANES), 1)
        sel = (lane < HEAD_DIM, lane >= HEAD_DIM)
        q2, do2 = q_ref[...], dy_ref[...].astype(BF16)
        dot_oy = do2.astype(F32) * y_ref[...]
        for h in range(2):
            q_scr[h] = jnp.where(sel[h], q2, 0.0).astype(BF16)
            do_scr[h] = jnp.where(sel[h], do2, jnp.zeros_like(do2))
            tot_scr[h] = jnp.sum(jnp.where(sel[h], dot_oy, 0.0), axis=-1, keepdims=True)
        r_i, c_i = lax.broadcasted_iota(jnp.int32, (tk, tk), 0), lax.broadcasted_iota(jnp.int32, (tk, tk), 1)
        tri = (r_i > c_i).astype(BF16)
        tri_ge = (r_i >= c_i).astype(BF16)
        qpos = i * tq + lax.broadcasted_iota(jnp.int32, (tq, tk), 0)
        kidx = lax.broadcasted_iota(jnp.int32, (tq, tk), 1)

        def start(j):
            return pl.multiple_of(jnp.clip(j, 0, seq // tk - 1) * tk, tk)

        def scores(j, slot):
            c0 = start(j)
            kb, vb = k_ref[pl.ds(c0, tk), :].astype(BF16), v_ref[pl.ds(c0, tk), :].astype(BF16)
            for h in range(2):
                z_scr[slot, h] = _dot(q_scr[h], kb, NT)
                dw_scr[slot, h] = _dot(do_scr[h], vb, NT)

        def finish(j):
            c0 = start(j)
            kb = k_ref[pl.ds(c0, tk), :].astype(BF16)
            dk_add, dv_add = jnp.zeros((tk, LANES), F32), jnp.zeros((tk, LANES), F32)
            for h in range(2):
                dz = dz_scr[h]
                dq_scr[h] += _dot(dz, kb)
                dk_add = dk_add + _dot(dz, q_scr[h], TN)
                dv_add = dv_add + _dot(w_scr[h], do_scr[h], TN)
            dk_ref[pl.ds(c0, tk), :] += dk_add
            dv_ref[pl.ds(c0, tk), :] += dv_add

        def prologue(j):
            for r in (w_scr, dz_scr, dq_scr, c_scr, c2_scr):
                r[...] = jnp.zeros_like(r)
            scores(j, 0)

        def block(j, slot, masked):
            scores(j - 1, 1 - slot)
            finish(j + 1)
            past = ((kidx + j * tk) < qpos) if masked else None
            for h in range(2):
                ls, lk, w = _sb_weights(z_scr[slot, h], past, c_scr[h], tri)
                wb = w.astype(BF16)
                dlw = dw_scr[slot, h] * wb.astype(F32)
                hi, lo = _split_bf16(dlw)
                dlk = tot_scr[h] - c2_scr[h] - (_dot(hi, tri_ge) + _dot(lo, tri_ge))
                if masked:
                    dlk = jnp.where(past, dlk, 0.0)
                sig = jnp.exp(ls)
                w_scr[h] = wb
                dz_scr[h] = (dlw * (1.0 - sig) - dlk * sig).astype(BF16)
                c_scr[h] += jnp.sum(lk, axis=-1, keepdims=True)
                c2_scr[h] += jnp.sum(dlw, axis=-1, keepdims=True)

        _walk_key_blocks(i, ratio, prologue, block, lambda: finish(0))
        dq_ref[...] = jnp.where(sel[0], dq_scr[0], dq_scr[1])

    blk = pl.BlockSpec((tq, LANES), lambda p, i: (i, p))
    full = pl.BlockSpec((seq, LANES), lambda p, i: (0, p))
    shape = jax.ShapeDtypeStruct(qh.shape, F32)
    return pl.pallas_call(
        body, name="attn_bwd", grid=(n_pair, seq // tq),
        in_specs=[blk, full, pl.BlockSpec((seq, LANES), lambda p, i: (0, v_col + p)), blk, blk],
        out_specs=[blk, full, full], out_shape=[shape, shape, shape],
        scratch_shapes=[pltpu.VMEM((2, tq, LANES), BF16), pltpu.VMEM((2, tq, LANES), BF16),
                        pltpu.VMEM((2, 2, tq, tk), F32), pltpu.VMEM((2, 2, tq, tk), F32),
                        pltpu.VMEM((2, tq, tk), BF16), pltpu.VMEM((2, tq, tk), BF16), pltpu.VMEM((2, tq, LANES), F32),
                        pltpu.VMEM((2, tq, 1), F32), pltpu.VMEM((2, tq, 1), F32), pltpu.VMEM((2, tq, 1), F32)],
        compiler_params=_params(2),
    )(qh, kh, proj, y, dy)


def _loss_head(h1, ffn, target, gate_f, tile):
    seq, d = h1.shape

    def body(h_ref, f_ref, t_ref, g_ref, dy_ref, df_ref, dg_ref, loss_ref):
        @pl.when(pl.program_id(0) == 0)
        def _():
            dg_ref[...] = jnp.zeros_like(dg_ref)
            loss_ref[...] = jnp.zeros_like(loss_ref)

        f, g = f_ref[...], g_ref[...]
        err = h_ref[...] + g * f - t_ref[...]
        dy = err * (1.0 / d)
        dy_ref[...] = dy
        df_ref[...] = (dy * g).astype(df_ref.dtype)
        dg_ref[...] += jnp.sum(dy * f, axis=0, keepdims=True)
        loss_ref[...] += jnp.sum(jnp.sum(err * err, axis=-1, keepdims=True), axis=0, keepdims=True) * (0.5 / d)

    row = _row_spec(tile, d)
    return pl.pallas_call(
        body, name="loss_head", grid=(seq // tile,), in_specs=[row, row, row, _vec_spec(d)],
        out_specs=[row, row, _vec_spec(d), pl.BlockSpec((1, 1), lambda i: (0, 0))],
        out_shape=[jax.ShapeDtypeStruct((seq, d), F32), jax.ShapeDtypeStruct((seq, d), BF16),
                   jax.ShapeDtypeStruct((1, d), F32), jax.ShapeDtypeStruct((1, 1), F32)],
        compiler_params=_params(1),
    )(h1, ffn, target, gate_f)


def _dot3(a, b, dn):
    ah, al = _split_bf16(a)
    bh, bl = _split_bf16(b)
    return _dot(ah, bh, dn) + (_dot(ah, bl, dn) + _dot(al, bh, dn))


def _ada_forward(c_all, w_shard, b_cols):
    d, n = w_shard.shape
    bk = _tile(d, 512)

    def body(c_ref, w_ref, b_ref, o_ref):
        @pl.when(pl.program_id(0) == 0)
        def _():
            o_ref[...] = jnp.broadcast_to(b_ref[...], o_ref.shape)

        o_ref[...] += _dot3(jax.nn.silu(c_ref[...]), w_ref[...], NN)

    return pl.pallas_call(
        body, name="ada_fwd", grid=(d // bk,),
        in_specs=[pl.BlockSpec((NDEV, bk), lambda k: (0, k)), pl.BlockSpec((bk, n), lambda k: (k, 0)), _vec_spec(n)],
        out_specs=pl.BlockSpec((NDEV, n), lambda k: (0, 0)), out_shape=jax.ShapeDtypeStruct((NDEV, n), F32),
        compiler_params=_params(1),
    )(c_all, w_shard, b_cols)


def _adam(w, g, m, v):
    m = ADAM_B1 * m + (1.0 - ADAM_B1) * g
    v = ADAM_B2 * v + (1.0 - ADAM_B2) * (g * g)
    m_hat = m / (1.0 - ADAM_B1 ** ADAM_STEP)
    v_hat = v / (1.0 - ADAM_B2 ** ADAM_STEP)
    return -ADAM_LR * (m_hat / (jnp.sqrt(v_hat) + ADAM_EPS) + ADAM_WD * w), m, v


def _adam_ada(c_all, dmod_cols, w, m, v):
    d, n = w.shape
    tr = _tile(d, 256)

    def body(c_ref, dm_ref, w_ref, m_ref, v_ref, g_ref, dl_ref, nm_ref, nv_ref):
        g = _dot3(jax.nn.silu(c_ref[...]), dm_ref[...], TN)
        delta, nm, nv = _adam(w_ref[...], g, m_ref[...], v_ref[...])
        g_ref[...] = g
        dl_ref[...] = delta
        nm_ref[...] = nm
        nv_ref[...] = nv

    row = _row_spec(tr, n)
    return pl.pallas_call(
        body, name="adam_ada", grid=(d // tr,),
        in_specs=[pl.BlockSpec((NDEV, tr), lambda i: (0, i)), pl.BlockSpec((NDEV, n), lambda i: (0, 0)), row, row, row],
        out_specs=[row] * 4, out_shape=[jax.ShapeDtypeStruct((d, n), F32)] * 4, compiler_params=_params(1),
    )(c_all, dmod_cols, w, m, v)


def _adam_sum(name, parts, part_spec, w, m, v, tr):
    r, c = w.shape

    def body(p_ref, w_ref, m_ref, v_ref, g_ref, dl_ref, nm_ref, nv_ref):
        g = p_ref[0].astype(F32)
        for k in range(1, NDEV):
            g = g + p_ref[k].astype(F32)
        delta, nm, nv = _adam(w_ref[...], g, m_ref[...], v_ref[...])
        g_ref[...] = g
        dl_ref[...] = delta
        nm_ref[...] = nm
        nv_ref[...] = nv

    row = _row_spec(tr, c)
    return pl.pallas_call(
        body, name=name, grid=(r // tr,), in_specs=[part_spec, row, row, row],
        out_specs=[row] * 4, out_shape=[jax.ShapeDtypeStruct((r, c), F32)] * 4, compiler_params=_params(1),
    )(parts, w, m, v)


GROUPS_PER_BLOCK = LANES // SSM_GROUP


def _to_b_blocks(bb, n_blk, p):
    t = bb.reshape(n_blk, GROUPS_PER_BLOCK, p, SSM_GROUP)
    eye = jnp.eye(GROUPS_PER_BLOCK, dtype=bb.dtype)
    return jnp.einsum("jgph,gk->jghkp", t, eye).reshape(n_blk, LANES, GROUPS_PER_BLOCK * p)


def _from_b_blocks(blk, n_blk, p):
    t = blk.reshape(n_blk, GROUPS_PER_BLOCK, SSM_GROUP, GROUPS_PER_BLOCK, p)
    eye = jnp.eye(GROUPS_PER_BLOCK, dtype=blk.dtype)
    return jnp.einsum("jghkp,gk->jgph", t, eye).reshape(n_blk * GROUPS_PER_BLOCK, p, SSM_GROUP)


def _to_c_blocks(cc, n_blk, p):
    t = cc.reshape(n_blk, GROUPS_PER_BLOCK, SSM_GROUP, p)
    eye = jnp.eye(GROUPS_PER_BLOCK, dtype=cc.dtype)
    return jnp.einsum("jghp,gk->jgpkh", t, eye).reshape(n_blk, GROUPS_PER_BLOCK * p, LANES)


def _from_c_blocks(blk, n_blk, p):
    t = blk.reshape(n_blk, GROUPS_PER_BLOCK, p, GROUPS_PER_BLOCK, SSM_GROUP)
    eye = jnp.eye(GROUPS_PER_BLOCK, dtype=blk.dtype)
    return jnp.einsum("jgpkh,gk->jghp", t, eye).reshape(n_blk * GROUPS_PER_BLOCK, SSM_GROUP, p)


SMALL = ("b_ada", "g_mix", "a_re", "a_im", "log_dt", "b_re", "b_im", "c_re", "c_im", "d_skip",
         "q_gain", "k_gain", "g_ssm_out", "g_attn_out", "g_ffn")
PACK_COLS = 1024


def _pack(arrs):
    flat = jnp.concatenate([a.reshape(-1) for a in arrs])
    n = flat.shape[0]
    quantum = SUBLANES * PACK_COLS
    padded = -(-n // quantum) * quantum
    return jnp.pad(flat, (0, padded - n)).reshape(padded // PACK_COLS, PACK_COLS)


def _unpack(packed, like):
    flat, out, off = packed.reshape(-1), [], 0
    for a in like:
        out.append(flat[off:off + a.size].reshape(a.shape))
        off += a.size
    return out


def kernel(x, c, w_ada, b_ada, g_mix, w_in, a_re, a_im, log_dt, b_re, b_im, c_re, c_im, d_skip, w_glu, q_gain, k_gain, g_ssm_out, g_attn_out, w_out, g_ffn, w_gate, w_up, w_down, loss_target, m_w_ada, m_b_ada, m_g_mix, m_w_in, m_a_re, m_a_im, m_log_dt, m_b_re, m_b_im, m_c_re, m_c_im, m_d_skip, m_w_glu, m_q_gain, m_k_gain, m_g_ssm_out, m_g_attn_out, m_w_out, m_g_ffn, m_w_gate, m_w_up, m_w_down, v_w_ada, v_b_ada, v_g_mix, v_w_in, v_a_re, v_a_im, v_log_dt, v_b_re, v_b_im, v_c_re, v_c_im, v_d_skip, v_w_glu, v_q_gain, v_k_gain, v_g_ssm_out, v_g_attn_out, v_w_out, v_g_ffn, v_w_gate, v_w_up, v_w_down):
    given = dict(locals())
    seq, d = x.shape[1], x.shape[2]
    xs, tgt = x[0], loss_target[0]
    n_groups, p_state = a_re.shape[1], a_re.shape[2]
    w_ssm = n_groups * SSM_GROUP
    w_attn = w_in.shape[2] * NDEV - w_ssm
    w_attn //= 3
    n_blk, n_pair = w_ssm // LANES, w_attn // LANES
    n_heads = w_attn // HEAD_DIM
    ns_in, ns_ff = w_in.shape[2], w_gate.shape[2]
    d_mix = w_ssm + w_attn
    mx, my, mc = _me()
    me = 4 * mx + 2 * my + mc
    rt = _tile(seq, 256)
    n_rt = seq // rt
    sds = jax.ShapeDtypeStruct

    order_tok = [jnp.zeros((), F32)]

    def start(x, gather, name):
        handle, token = _exchange_start(x, gather, name)
        order_tok[0] = token[0, 0]
        return handle

    def behind(v):
        return v + order_tok[0].astype(v.dtype)

    h_w_in = start(behind(w_in[0]).astype(BF16), True, "comm_ag_w_in")
    h_w_glu = start(behind(w_glu[0]).astype(BF16), True, "comm_ag_w_glu")
    h_w_out = start(behind(w_out[0]).astype(BF16), True, "comm_ag_w_out")
    h_w_gu = start(behind(jnp.stack([w_gate[0], w_up[0]])).astype(BF16), True, "comm_ag_w_gu")
    h_w_down = start(behind(w_down[0]).astype(BF16), True, "comm_ag_w_down")

    c_all = _exchange(behind(c), True, "comm_ag_c").reshape(NDEV, d)
    n_ada = w_ada.shape[2]
    b_cols = lax.dynamic_slice(b_ada, (0, me * n_ada), (1, n_ada))
    mod_cols = _ada_forward(c_all, w_ada[0], b_cols)
    mod_all = _exchange(mod_cols, True, "comm_ag_mod")
    mod = lax.dynamic_slice(mod_all, (0, me, 0), (NDEV, 1, n_ada)).reshape(1, NDEV * n_ada)
    shift_m, scale_m, gate_m, shift_f, scale_f, gate_f = [mod[:, i * d:(i + 1) * d] for i in range(6)]

    gp = n_groups * p_state
    a_re2, a_im2, ldt2 = a_re[0], a_im[0], log_dt[0].reshape(n_groups, 1)
    b_re2, b_im2 = b_re[0].reshape(gp, SSM_GROUP), b_im[0].reshape(gp, SSM_GROUP)
    lam_r, lam_i, coef_r, coef_i = _whole("s5_lam", _s5_lam, [a_re2, a_im2, ldt2], [(n_groups, p_state)] * 4)
    coef_r2, coef_i2 = coef_r.reshape(gp, 1), coef_i.reshape(gp, 1)
    bb_r, bb_i = _whole("s5_bbar", _s5_bbar, [coef_r2, coef_i2, b_re2, b_im2], [(gp, SSM_GROUP)] * 2)
    s_blk = GROUPS_PER_BLOCK * p_state
    b_blk_r = _to_b_blocks(bb_r.reshape(n_groups, p_state, SSM_GROUP), n_blk, p_state).astype(BF16)
    b_blk_i = _to_b_blocks(bb_i.reshape(n_groups, p_state, SSM_GROUP), n_blk, p_state).astype(BF16)
    c_blk_r = _to_c_blocks(c_re[0], n_blk, p_state).astype(BF16)
    c_blk_i = _to_c_blocks(c_im[0], n_blk, p_state).astype(BF16)
    lam_r3, lam_i3 = lam_r.reshape(n_blk, 1, s_blk), lam_i.reshape(n_blk, 1, s_blk)
    d_skip2 = d_skip[0].reshape(1, w_ssm)

    row_d, vec_d = _row_spec(rt, d), _vec_spec(d)
    xm, = _rowwise_fwd("seg_in", lambda *a: _seg_in(*a)[:1], [xs], [row_d], [shift_m, scale_m, g_mix], [vec_d] * 3,
                       [sds((seq, d), BF16)], [row_d], n_rt)
    bn_in = _tile(ns_in, 512)
    per = ns_in // bn_in
    bm, bk = _tile(seq, BM), _tile(d, BK)
    w_in_g = _exchange_finish(h_w_in, xm)
    proj = _mm("mm_in", xm, w_in_g, NN, (seq // bm, NDEV * per, d // bk),
               pl.BlockSpec((bm, bk), lambda i, j, k: (i, k)),
               pl.BlockSpec((None, bk, bn_in), lambda i, j, k: (j // per, k, j % per)),
               pl.BlockSpec((bm, bn_in), lambda i, j, k: (i, j)), (seq, NDEV * ns_in), F32, (bm, bn_in))
    q_col, k_col, v_col = w_ssm // w_attn, w_ssm // w_attn + 1, (w_ssm + 2 * w_attn) // LANES
    qg_t, kg_t = jnp.tile(q_gain, (1, n_heads)), jnp.tile(k_gain, (1, n_heads))
    row_a, vec_a = _row_spec(rt, w_attn), _vec_spec(w_attn)
    qk_rows, qk_specs = [proj, proj], [_row_spec(rt, w_attn, q_col), _row_spec(rt, w_attn, k_col)]
    qh, kh = _rowwise_fwd("seg_qk", _seg_qk, qk_rows, qk_specs, [qg_t, kg_t], [vec_a] * 2,
                          [sds((seq, w_attn), F32)] * 2, [row_a] * 2, n_rt)
    t_chunk = _tile(seq, 256)
    ypre, x_re, x_im = _s5_forward(proj, b_blk_r, b_blk_i, c_blk_r, c_blk_i, lam_r3, lam_i3, d_skip2, n_blk, t_chunk)
    tq, tk = _tile(seq, TQ), _tile(seq, TK)
    y_attn = _attention_forward(qh, kh, proj, v_col, n_pair, tq, tk)
    row_s, vec_s = _row_spec(rt, w_ssm), _vec_spec(w_ssm)
    y1, = _rowwise_fwd("seg_gelu", _seg_gelu, [ypre], [row_s], [], [], [sds((seq, w_ssm), F32)], [row_s], n_rt)
    w_glu_g = _exchange_finish(h_w_glu, y1).reshape(w_ssm, w_ssm)
    z = _mm_plain("mm_glu", y1, w_glu_g, NN, F32)
    row_m = _row_spec(rt, d_mix)
    mixed, = _rowwise_fwd("seg_mix", _seg_mix, [y1, z, y_attn], [row_s, row_s, row_a], [g_ssm_out, g_attn_out], [vec_s, vec_a],
                          [sds((seq, d_mix), BF16)], [row_m], n_rt)
    w_out_g = _exchange_finish(h_w_out, mixed).reshape(d_mix, d)
    o = _mm_plain("mm_out", mixed, w_out_g, NN, F32)
    h1, xf = _rowwise_fwd("seg_mid", _seg_mid, [xs, o], [row_d] * 2, [gate_m, g_ffn, scale_f, shift_f], [vec_d] * 4,
                          [sds((seq, d), F32), sds((seq, d), BF16)], [row_d] * 2, n_rt)
    w_gu_g = _exchange_finish(h_w_gu, xf).reshape(2 * NDEV, d, ns_ff)
    gu = _mm("mm_gu", xf, w_gu_g, NN, (seq // bm, 2 * NDEV, d // bk),
             pl.BlockSpec((bm, bk), lambda i, j, k: (i, k)), pl.BlockSpec((None, bk, ns_ff), lambda i, j, k: (j, k, 0)),
             pl.BlockSpec((None, bm, ns_ff), lambda i, j, k: (j, i, 0)), (2 * NDEV, seq, ns_ff), F32, (bm, ns_ff))
    gu4 = gu.reshape(NDEV, 2, seq, ns_ff)
    ft = _tile(seq, 512)
    pair_spec = pl.BlockSpec((None, 2, ft, ns_ff), lambda s, i: (s, 0, i, 0))
    one_spec = pl.BlockSpec((None, ft, ns_ff), lambda s, i: (s, i, 0))

    def act_body(gu_ref, a_ref):
        a_ref[...] = _seg_act(gu_ref[0], gu_ref[1])[0].astype(a_ref.dtype)

    act = pl.pallas_call(act_body, name="seg_act", grid=(NDEV, seq // ft), in_specs=[pair_spec], out_specs=one_spec,
                         out_shape=sds((NDEV, seq, ns_ff), BF16), compiler_params=_params(2))(gu4)
    bn_d = _tile(d, BN)
    w_down_g = _exchange_finish(h_w_down, act)
    ffn = _mm("mm_down", act, w_down_g, NN, (seq // bm, d // bn_d, NDEV),
              pl.BlockSpec((None, bm, ns_ff), lambda i, j, k: (k, i, 0)), pl.BlockSpec((None, ns_ff, bn_d), lambda i, j, k: (k, 0, j)),
              pl.BlockSpec((bm, bn_d), lambda i, j, k: (i, j)), (seq, d), F32, (bm, bn_d))
    dy, dffn, d_gate_f, loss_part = _loss_head(h1, ffn, tgt, gate_f, rt)
    loss = lax.psum(loss_part[0, 0], MESH_AXES)

    dact = _mm("mm_dact", dffn, w_down_g, NT, (seq // bm, NDEV, d // bk),
               pl.BlockSpec((bm, bk), lambda i, j, k: (i, k)), pl.BlockSpec((None, ns_ff, bk), lambda i, j, k: (j, 0, k)),
               pl.BlockSpec((None, bm, ns_ff), lambda i, j, k: (j, i, 0)), (NDEV, seq, ns_ff), F32, (bm, ns_ff))
    bl = _tile(seq, BK)
    gw_down = _mm("mm_dw_down", act, dffn, TN, (NDEV, d // bn_d, seq // bl),
                  pl.BlockSpec((None, bl, ns_ff), lambda i, j, k: (i, k, 0)), pl.BlockSpec((bl, bn_d), lambda i, j, k: (k, j)),
                  pl.BlockSpec((None, ns_ff, bn_d), lambda i, j, k: (i, 0, j)), (NDEV, ns_ff, d), BF16, (ns_ff, bn_d))
    h_g_down = start(gw_down, False, "comm_a2a_w_down")

    def dact_body(gu_ref, da_ref, dgu_ref):
        _, vjp = jax.vjp(_seg_act, gu_ref[0], gu_ref[1])
        dg, du_ = vjp((da_ref[...],))
        dgu_ref[0] = dg.astype(dgu_ref.dtype)
        dgu_ref[1] = du_.astype(dgu_ref.dtype)

    dgu4 = pl.pallas_call(dact_body, name="seg_act_bwd", grid=(NDEV, seq // ft), in_specs=[pair_spec, one_spec],
                          out_specs=pair_spec, out_shape=sds((NDEV, 2, seq, ns_ff), BF16), compiler_params=_params(2))(gu4, dact)
    dgu = dgu4.reshape(2 * NDEV, seq, ns_ff)
    dxf = _mm("mm_dxf", dgu, w_gu_g, NT, (seq // bm, d // bn_d, 2 * NDEV),
              pl.BlockSpec((None, bm, ns_ff), lambda i, j, k: (k, i, 0)), pl.BlockSpec((None, bn_d, ns_ff), lambda i, j, k: (k, j, 0)),
              pl.BlockSpec((bm, bn_d), lambda i, j, k: (i, j)), (seq, d), F32, (bm, bn_d))
    bmd = _tile(d, BM)
    gw_gu = _mm("mm_dw_gu", xf, dgu, TN, (d // bmd, 2 * NDEV, seq // bl),
                pl.BlockSpec((bl, bmd), lambda i, j, k: (k, i)), pl.BlockSpec((None, bl, ns_ff), lambda i, j, k: (j, k, 0)),
                pl.BlockSpec((None, bmd, ns_ff), lambda i, j, k: (j, i, 0)), (2 * NDEV, d, ns_ff), BF16, (bmd, ns_ff))
    h_g_gu = start(gw_gu.reshape(NDEV, 2, d, ns_ff), False, "comm_a2a_w_gu")
    (do, dx_a, d_gate_m, d_g_ffn, d_scale_f, d_shift_f) = _rowwise_bwd(
        "seg_mid_bwd", _seg_mid, [xs, o], [row_d] * 2, [behind(gate_m), g_ffn, scale_f, shift_f], [vec_d] * 4,
        [dy, dxf], [row_d] * 2, [[0], [1]], [1, 0], [sds((seq, d), BF16), sds((seq, d), F32)], [row_d] * 2,
        [0, 1, 2, 3], [sds((1, d), F32)] * 4, [vec_d] * 4, n_rt)

    dmixed = _mm_plain("mm_dmixed", do, w_out_g, NT, F32)
    gw_out = _mm_plain("mm_dw_out", mixed, do, TN, BF16)
    h_g_out = start(gw_out.reshape(NDEV, w_out.shape[1], d), False, "comm_a2a_w_out")
    (dz, dy1_a, dy_attn, d_g_ssm, d_g_attn) = _rowwise_bwd(
        "seg_mix_bwd", _seg_mix, [y1, z, y_attn], [row_s, row_s, row_a], [behind(g_ssm_out), g_attn_out], [vec_s, vec_a],
        [dmixed], [row_m], [[0]], [1, 0, 2], [sds((seq, w_ssm), BF16), sds((seq, w_ssm), F32), sds((seq, w_attn), F32)],
        [row_s, row_s, row_a], [0, 1], [sds((1, w_ssm), F32), sds((1, w_attn), F32)], [vec_s, vec_a], n_rt)
    dy1_b = _mm_plain("mm_dy1", dz, w_glu_g, NT, F32)
    gw_glu = _mm_plain("mm_dw_glu", y1, dz, TN, BF16)
    h_g_glu = start(gw_glu.reshape(NDEV, w_glu.shape[1], w_ssm), False, "comm_a2a_w_glu")
    (dypre,) = _rowwise_bwd("seg_gelu_bwd", _seg_gelu, [ypre], [row_s], [], [], [dy1_a, dy1_b], [row_s] * 2, [[0, 1]],
                            [0], [sds((seq, w_ssm), F32)], [row_s], [], [], [], n_rt)
    (du, db_blk_r, db_blk_i, dc_blk_r, dc_blk_i, dlam_r3, dlam_i3, dd_skip2) = _s5_backward(
        dypre, proj, x_re, x_im, b_blk_r, b_blk_i, c_blk_r, c_blk_i, lam_r3, lam_i3, behind(d_skip2), n_blk, t_chunk)
    dqh, dkh, dv = _attention_backward(qh, kh, proj, v_col, y_attn, dy_attn, n_pair, tq, tk)
    (dq, dk, dqg_t, dkg_t) = _rowwise_bwd(
        "seg_qk_bwd", _seg_qk, qk_rows, qk_specs, [qg_t, kg_t], [vec_a] * 2, [dqh, dkh], [row_a] * 2, [[0], [1]],
        [0, 1], [sds((seq, w_attn), BF16)] * 2, [row_a] * 2, [0, 1], [sds((1, w_attn), F32)] * 2, [vec_a] * 2, n_rt)
    dproj = jnp.concatenate([du.astype(BF16), dq, dk, dv.astype(BF16)], axis=-1)
    bk_in = _tile(ns_in, BK)
    per_k = ns_in // bk_in
    dxm = _mm("mm_dxm", dproj, w_in_g, NT, (seq // bm, d // bn_d, NDEV * per_k),
              pl.BlockSpec((bm, bk_in), lambda i, j, k: (i, k)),
              pl.BlockSpec((None, bn_d, bk_in), lambda i, j, k: (k // per_k, j, k % per_k)),
              pl.BlockSpec((bm, bn_d), lambda i, j, k: (i, j)), (seq, d), F32, (bm, bn_d))
    gw_in = _mm("mm_dw_in", xm, dproj, TN, (d // bmd, NDEV * per, seq // bl),
                pl.BlockSpec((bl, bmd), lambda i, j, k: (k, i)), pl.BlockSpec((bl, bn_in), lambda i, j, k: (k, j)),
                pl.BlockSpec((None, bmd, bn_in), lambda i, j, k: (j // per, i, j % per)), (NDEV, d, ns_in), BF16, (bmd, bn_in))
    h_g_in = start(gw_in, False, "comm_a2a_w_in")
    (grad_x, d_shift_m, d_scale_m, d_g_mix) = _rowwise_bwd(
        "seg_in_bwd", _seg_in, [xs], [row_d], [shift_m, scale_m, behind(g_mix)], [vec_d] * 3, [dxm, dx_a], [row_d] * 2, [[0], [1]],
        [0], [sds((seq, d), F32)], [row_d], [0, 1, 2], [sds((1, d), F32)] * 3, [vec_d] * 3, n_rt)

    dbb_r = _from_b_blocks(db_blk_r, n_blk, p_state).reshape(gp, SSM_GROUP)
    dbb_i = _from_b_blocks(db_blk_i, n_blk, p_state).reshape(gp, SSM_GROUP)
    dcoef_r2, dcoef_i2, db_re2, db_im2 = _whole_vjp("s5_bbar_bwd", _s5_bbar, [coef_r2, coef_i2, b_re2, b_im2], [dbb_r, dbb_i],
                                                    [(gp, 1), (gp, 1), (gp, SSM_GROUP), (gp, SSM_GROUP)])
    lam_cts = [dlam_r3.reshape(n_groups, p_state), dlam_i3.reshape(n_groups, p_state),
               dcoef_r2.reshape(n_groups, p_state), dcoef_i2.reshape(n_groups, p_state)]
    da_re2, da_im2, dldt2 = _whole_vjp("s5_lam_bwd", _s5_lam, [a_re2, a_im2, ldt2], lam_cts,
                                       [(n_groups, p_state), (n_groups, p_state), (n_groups, 1)])
    dc_re2, dc_im2 = _from_c_blocks(dc_blk_r, n_blk, p_state), _from_c_blocks(dc_blk_i, n_blk, p_state)

    dmod = jnp.concatenate([d_shift_m, d_scale_m, d_gate_m, d_shift_f, d_scale_f, d_gate_f], axis=-1)
    small_part = {
        "b_ada": dmod, "g_mix": d_g_mix, "a_re": da_re2, "a_im": da_im2, "log_dt": dldt2, "b_re": db_re2, "b_im": db_im2,
        "c_re": dc_re2, "c_im": dc_im2, "d_skip": dd_skip2,
        "q_gain": dqg_t.reshape(n_heads, HEAD_DIM).sum(0), "k_gain": dkg_t.reshape(n_heads, HEAD_DIM).sum(0),
        "g_ssm_out": d_g_ssm, "g_attn_out": d_g_attn, "g_ffn": d_g_ffn,
    }
    h_small = start(_pack([small_part[n] for n in SMALL]), True, "comm_ag_small")

    big = {}
    after = jnp.broadcast_to(order_tok[0], (SUBLANES, LANES))

    def sharded(nm, handle, part_block, index_map, tr, after):
        got = _exchange_finish(handle, after)
        big[nm] = _adam_sum("adam_" + nm, got, pl.BlockSpec(part_block, index_map), given[nm][0], given["m_" + nm][0],
                            given["v_" + nm][0], tr)
        return got

    tr = _tile(w_down.shape[1], 64)
    sharded("w_down", h_g_down, (NDEV, tr, d), lambda i: (0, i, 0), tr, after)
    tr = _tile(d, 256)
    got_gu = sharded("w_gate", h_g_gu, (NDEV, None, tr, ns_ff), lambda i: (0, 0, i, 0), tr, big["w_down"][1])
    big["w_up"] = _adam_sum("adam_w_up", got_gu, pl.BlockSpec((NDEV, None, tr, ns_ff), lambda i: (0, 1, i, 0)),
                            w_up[0], m_w_up[0], v_w_up[0], tr)
    tr = _tile(w_out.shape[1], 128)
    sharded("w_out", h_g_out, (NDEV, tr, d), lambda i: (0, i, 0), tr, big["w_up"][1])
    tr = _tile(w_glu.shape[1], 128)
    sharded("w_glu", h_g_glu, (NDEV, tr, w_ssm), lambda i: (0, i, 0), tr, big["w_out"][1])
    tr = _tile(d, 256)
    sharded("w_in", h_g_in, (NDEV, tr, ns_in), lambda i: (0, i, 0), tr, big["w_glu"][1])

    packed_parts = _exchange_finish(h_small, big["w_in"][1])
    rows_p = packed_parts.shape[1]
    tr_p = _tile(rows_p, 64)
    small_spec = pl.BlockSpec((NDEV, tr_p, PACK_COLS), lambda i: (0, i, 0))
    sm = _adam_sum("adam_small", packed_parts, small_spec, _pack([given[n] for n in SMALL]),
                   _pack([given["m_" + n] for n in SMALL]), _pack([given["v_" + n] for n in SMALL]), tr_p)
    like = [given[n] for n in SMALL]
    small_out = [dict(zip(SMALL, _unpack(t, like))) for t in sm]

    dmod_all = packed_parts[:, :(6 * d) // PACK_COLS, :].reshape(NDEV, 6 * d) if (6 * d) % PACK_COLS == 0 else None
    assert dmod_all is not None
    dmod_cols = lax.dynamic_slice(dmod_all, (0, me * n_ada), (NDEV, n_ada))
    big["w_ada"] = _adam_ada(c_all, dmod_cols, w_ada[0], m_w_ada[0], v_w_ada[0])

    order = ("w_ada", "b_ada", "g_mix", "w_in", "a_re", "a_im", "log_dt", "b_re", "b_im", "c_re", "c_im", "d_skip", "w_glu",
             "q_gain", "k_gain", "g_ssm_out", "g_attn_out", "w_out", "g_ffn", "w_gate", "w_up", "w_down")
    outs = [loss, grad_x[None]]
    for kind in range(4):
        for n in order:
            outs.append(big[n][kind][None] if n in big else small_out[kind][n])
    return tuple(outs)
```

```python
import functools
import math

import jax
import jax.numpy as jnp
from jax import lax
from jax.experimental import pallas as pl
from jax.experimental.pallas import tpu as pltpu

F32 = jnp.float32
BF16 = jnp.bfloat16
NDEV = 8
MESH_AXES = ("x", "y", "c")
MESH_ID = pl.DeviceIdType.MESH
EPS = 1e-6
LANES = 128
SUBLANES = 8
HEAD_DIM = 64
SSM_GROUP = 16
ADAM_LR, ADAM_B1, ADAM_B2, ADAM_EPS, ADAM_WD, ADAM_STEP = 0.001, 0.9, 0.999, 1e-08, 0.01, 10

NN = (((1,), (0,)), ((), ()))
NT = (((1,), (1,)), ((), ()))
TN = (((0,), (0,)), ((), ()))


def _dot(a, b, dn=NN):
    return lax.dot_general(a, b, dn, preferred_element_type=F32)


def _tile(dim, pref):
    t = min(dim, pref)
    while dim % t:
        t //= 2
    return t


def _params(n):
    return pltpu.CompilerParams(dimension_semantics=("arbitrary",) * n)


def _me():
    mx, my, mc = lax.axis_index("x"), lax.axis_index("y"), lax.axis_index("c")
    return mx, my, mc


def _peer(mx, my, mc, k):
    px = 1 - mx if (k >> 2) & 1 else mx
    py = 1 - my if (k >> 1) & 1 else my
    pc = 1 - mc if k & 1 else mc
    return (px, py, pc), 4 * px + 2 * py + pc


def _exchange_copies(x_ref, land_ref, send_sems, recv_sems, gather):
    mx, my, mc = _me()
    me = 4 * mx + 2 * my + mc
    pairs = []
    for k in range(1, NDEV):
        peer, pidx = _peer(mx, my, mc, k)
        src = x_ref if gather else x_ref.at[pidx]
        mk = lambda dst, src=src, k=k, peer=peer: pltpu.make_async_remote_copy(
            src_ref=src, dst_ref=dst, send_sem=send_sems.at[k - 1], recv_sem=recv_sems.at[k - 1],
            device_id=peer, device_id_type=MESH_ID)
        pairs.append((mk(land_ref.at[me]), mk(land_ref.at[pidx])))
    return me, pairs


def _exchange(x, gather, name):
    def body(x_ref, o_ref, token, send_sems, recv_sems, local_sem):
        me, pairs = _exchange_copies(x_ref, o_ref, send_sems, recv_sems, gather)
        local = pltpu.make_async_copy(x_ref if gather else x_ref.at[me], o_ref.at[me], local_sem)
        local.start()
        for send, _ in pairs:
            send.start()
        for _, arrival in pairs:
            arrival.wait_recv()
        for send, _ in pairs:
            send.wait_send()
        local.wait()
        token[...] = jnp.zeros_like(token)

    return pl.pallas_call(
        body, name=name,
        out_shape=(jax.ShapeDtypeStruct(((NDEV,) + x.shape) if gather else x.shape, x.dtype),
                   jax.ShapeDtypeStruct((SUBLANES, LANES), F32)),
        in_specs=[pl.BlockSpec(memory_space=pl.ANY)],
        out_specs=(pl.BlockSpec(memory_space=pl.ANY), pl.BlockSpec(memory_space=pltpu.VMEM)),
        scratch_shapes=[pltpu.SemaphoreType.DMA((NDEV - 1,)), pltpu.SemaphoreType.DMA((NDEV - 1,)), pltpu.SemaphoreType.DMA],
    )(x)


HBM_SPEC = pl.BlockSpec(memory_space=pltpu.HBM)
SEM_SPEC = pl.BlockSpec(memory_space=pltpu.SEMAPHORE)
SIDE_EFFECT = pltpu.SideEffectType.DATAFLOW_SIDE_EFFECTING


def _exchange_start(x, gather, name):
    land_shape = ((NDEV,) + x.shape) if gather else x.shape

    def body(x_ref, land_ref, send_sems, recv_sems, local_sem, x_thru, land_thru, token):
        me, pairs = _exchange_copies(x_ref, land_ref, send_sems, recv_sems, gather)
        for send, _ in pairs:
            send.start()
        pltpu.make_async_copy(x_ref if gather else x_ref.at[me], land_ref.at[me], local_sem).start()
        token[...] = jnp.zeros_like(token)

    sems = pltpu.SemaphoreType.DMA((NDEV - 1,))
    send_sems, recv_sems, local_sem, x_thru, land_thru, token = pl.pallas_call(
        body, name=name + "_start",
        out_shape=(sems, sems, pltpu.SemaphoreType.DMA(()), pltpu.HBM(x.shape, x.dtype), pltpu.HBM(land_shape, x.dtype),
                   jax.ShapeDtypeStruct((SUBLANES, LANES), F32)),
        in_specs=(HBM_SPEC, HBM_SPEC),
        out_specs=(SEM_SPEC, SEM_SPEC, SEM_SPEC, HBM_SPEC, HBM_SPEC, pl.BlockSpec(memory_space=pltpu.VMEM)),
        input_output_aliases={0: 3, 1: 4}, compiler_params=pltpu.CompilerParams(has_side_effects=SIDE_EFFECT),
    )(pltpu.with_memory_space_constraint(x, pltpu.HBM),
      pltpu.with_memory_space_constraint(lax.empty(land_shape, x.dtype), pltpu.HBM))
    return (send_sems, recv_sems, local_sem, x_thru, land_thru, gather, name), token


def _exchange_finish(handle, after):
    send_sems, recv_sems, local_sem, x_thru, land_thru, gather, name = handle

    def body(x_ref, land_ref, send_sems, recv_sems, local_sem, after_ref, x_dead, got_ref):
        me, pairs = _exchange_copies(x_ref, land_ref, send_sems, recv_sems, gather)
        for send, arrival in pairs:
            send.wait_send()
            arrival.wait_recv()
        pltpu.make_async_copy(x_ref if gather else x_ref.at[me], land_ref.at[me], local_sem).wait()

    return pl.pallas_call(
        body, name=name + "_wait",
        out_shape=(pltpu.HBM(x_thru.shape, x_thru.dtype), pltpu.HBM(land_thru.shape, land_thru.dtype)),
        in_specs=(HBM_SPEC, HBM_SPEC, SEM_SPEC, SEM_SPEC, SEM_SPEC, HBM_SPEC), out_specs=(HBM_SPEC, HBM_SPEC),
        input_output_aliases={0: 0, 1: 1}, compiler_params=pltpu.CompilerParams(has_side_effects=SIDE_EFFECT),
    )(x_thru, land_thru, send_sems, recv_sems, local_sem, pltpu.with_memory_space_constraint(after, pltpu.HBM))[1]


def _mm(name, a, b, dn, grid, a_spec, b_spec, o_spec, out_shape, out_dtype, acc_shape, behind=None):
    nk = grid[2]
    extra = [] if behind is None else [pltpu.with_memory_space_constraint(behind, pltpu.HBM)]

    def body(a_ref, b_ref, *rest):
        o_ref, acc_ref = rest[len(extra):]
        k = pl.program_id(2)

        @pl.when(k == 0)
        def _():
            acc_ref[...] = jnp.zeros_like(acc_ref)

        acc_ref[...] += _dot(a_ref[...].astype(BF16), b_ref[...].astype(BF16), dn)

        @pl.when(k == nk - 1)
        def _():
            o_ref[...] = acc_ref[...].astype(o_ref.dtype)

    return pl.pallas_call(
        body, name=name, grid=grid, in_specs=[a_spec, b_spec] + [HBM_SPEC] * len(extra), out_specs=o_spec,
        out_shape=jax.ShapeDtypeStruct(out_shape, out_dtype), scratch_shapes=[pltpu.VMEM(acc_shape, F32)],
        compiler_params=_params(3),
    )(a, b, *extra)


BM, BN, BK = 1024, 1024, 512


def _mm_plain(name, a, b, dn, out_dtype):
    if dn == NN:
        (m, kk), n = a.shape, b.shape[1]
    elif dn == NT:
        (m, kk), n = a.shape, b.shape[0]
    else:
        (kk, m), n = a.shape, b.shape[1]
    bm, bn, bk = _tile(m, BM), _tile(n, BN), _tile(kk, BK)
    a_spec = pl.BlockSpec((bk, bm), lambda i, j, k: (k, i)) if dn == TN else pl.BlockSpec((bm, bk), lambda i, j, k: (i, k))
    b_spec = pl.BlockSpec((bn, bk), lambda i, j, k: (j, k)) if dn == NT else pl.BlockSpec((bk, bn), lambda i, j, k: (k, j))
    return _mm(name, a, b, dn, (m // bm, n // bn, kk // bk), a_spec, b_spec,
               pl.BlockSpec((bm, bn), lambda i, j, k: (i, j)), (m, n), out_dtype, (bm, bn))


def _row_spec(tile, width, col=0):
    return pl.BlockSpec((tile, width), lambda i: (i, col))


def _vec_spec(width, col=0):
    return pl.BlockSpec((1, width), lambda i: (0, col))


def _rowwise_fwd(name, fn, rows, row_specs, vecs, vec_specs, out_shapes, out_specs, n_tiles):
    nr, nv = len(rows), len(vecs)

    def body(*refs):
        ins = [r[...].astype(F32) for r in refs[:nr + nv]]
        outs = fn(*ins)
        for o_ref, o in zip(refs[nr + nv:], outs):
            o_ref[...] = o.astype(o_ref.dtype)

    return pl.pallas_call(body, name=name, grid=(n_tiles,), in_specs=list(row_specs) + list(vec_specs),
                          out_specs=list(out_specs), out_shape=list(out_shapes), compiler_params=_params(1))(*rows, *vecs)


def _rowwise_bwd(name, fn, rows, row_specs, vecs, vec_specs, cts, ct_specs, ct_groups,
                 drow_idx, drow_shapes, drow_specs, dvec_idx, dvec_shapes, dvec_specs, n_tiles):
    nr, nv, nc = len(rows), len(vecs), len(cts)

    def body(*refs):
        ins = [r[...].astype(F32) for r in refs[:nr + nv]]
        ct_vals = [r[...].astype(F32) for r in refs[nr + nv:nr + nv + nc]]
        out_refs = refs[nr + nv + nc:]
        _, vjp = jax.vjp(fn, *ins)
        grads = vjp(tuple(functools.reduce(lambda p, q: p + q, [ct_vals[j] for j in grp]) for grp in ct_groups))
        for o_ref, idx in zip(out_refs[:len(drow_idx)], drow_idx):
            o_ref[...] = grads[idx].astype(o_ref.dtype)
        step = pl.program_id(0)
        for o_ref, idx in zip(out_refs[len(drow_idx):], dvec_idx):
            @pl.when(step == 0)
            def _(o_ref=o_ref):
                o_ref[...] = jnp.zeros_like(o_ref)
            o_ref[...] += grads[nr + idx]

    return pl.pallas_call(body, name=name, grid=(n_tiles,),
                          in_specs=list(row_specs) + list(vec_specs) + list(ct_specs),
                          out_specs=list(drow_specs) + list(dvec_specs),
                          out_shape=list(drow_shapes) + list(dvec_shapes), compiler_params=_params(1))(*rows, *vecs, *cts)


def _rms(x):
    return x * lax.rsqrt(jnp.mean(x * x, axis=-1, keepdims=True) + EPS)


def _seg_in(x, shift, scale, gain):
    return _rms(x) * gain * (1.0 + scale) + shift, x


def _seg_qk(q, k, qg, kg):
    def norm(t, g, mult):
        blocks = []
        lane = lax.broadcasted_iota(jnp.int32, (1, LANES), 1)
        for p in range(t.shape[1] // LANES):
            tb = t[:, p * LANES:(p + 1) * LANES]
            sq = tb * tb
            lo = jnp.sum(jnp.where(lane < HEAD_DIM, sq, 0.0), axis=-1, keepdims=True)
            hi = jnp.sum(jnp.where(lane < HEAD_DIM, 0.0, sq), axis=-1, keepdims=True)
            ms = jnp.where(lane < HEAD_DIM, lo, hi) * (1.0 / HEAD_DIM)
            blocks.append(tb * lax.rsqrt(ms + EPS) * (g[:, p * LANES:(p + 1) * LANES] * mult))
        return jnp.concatenate(blocks, axis=-1) if len(blocks) > 1 else blocks[0]
    return norm(q, qg, 1.0 / math.sqrt(HEAD_DIM)), norm(k, kg, 1.0)


def _seg_gelu(ypre):
    return (jax.nn.gelu(ypre),)


def _seg_mix(y1, z, yattn, g_ssm, g_attn):
    ys = y1 * jax.nn.sigmoid(z)
    return (jnp.concatenate([_rms(ys) * g_ssm, _rms(yattn) * g_attn], axis=-1),)


def _seg_mid(x, o, gate_m, g_ffn, scale_f, shift_f):
    h1 = x + gate_m * o
    return h1, _rms(h1) * g_ffn * (1.0 + scale_f) + shift_f


def _seg_act(gate, up):
    return (jax.nn.silu(gate) * up,)


def _s5_lam(a_re, a_im, log_dt):
    dt = jnp.exp(log_dt)
    mag = jnp.exp(a_re * dt)
    lr, li = mag * jnp.cos(a_im * dt), mag * jnp.sin(a_im * dt)
    den = a_re * a_re + a_im * a_im
    nr, ni = lr - 1.0, li
    return lr, li, (nr * a_re + ni * a_im) / den, (ni * a_re - nr * a_im) / den


def _s5_bbar(coef_re, coef_im, b_re, b_im):
    return coef_re * b_re - coef_im * b_im, coef_re * b_im + coef_im * b_re


def _whole(name, fn, ins, out_shapes):
    n = len(ins)

    def body(*refs):
        outs = fn(*[r[...] for r in refs[:n]])
        for o_ref, o in zip(refs[n:], outs):
            o_ref[...] = o

    return pl.pallas_call(body, name=name, out_shape=[jax.ShapeDtypeStruct(s, F32) for s in out_shapes])(*ins)


def _whole_vjp(name, fn, ins, cts, out_shapes):
    n, nc = len(ins), len(cts)

    def body(*refs):
        _, vjp = jax.vjp(fn, *[r[...] for r in refs[:n]])
        grads = vjp(tuple(r[...] for r in refs[n:n + nc]))
        for o_ref, g in zip(refs[n + nc:], grads):
            o_ref[...] = g

    return pl.pallas_call(body, name=name, out_shape=[jax.ShapeDtypeStruct(s, F32) for s in out_shapes])(*ins, *cts)


SCAN_SHIFTS = (1, 2, 4)


def _cmul(ar, ai, br, bi):
    return ar * br - ai * bi, ar * bi + ai * br


def _scan_coefs(lr, li, reverse):
    s = lr.shape[1]
    row = lax.broadcasted_iota(jnp.int32, (SUBLANES, s), 0)
    p1 = (lr, li)
    p2 = _cmul(*p1, *p1)
    p4 = _cmul(*p2, *p2)
    p8 = _cmul(*p4, *p4)
    p3, p5, p6 = _cmul(*p1, *p2), _cmul(*p4, *p1), _cmul(*p4, *p2)
    p7 = _cmul(*p6, *p1)
    pows = (p1, p2, p3, p4, p5, p6, p7, p8)
    bc = lambda t: jnp.broadcast_to(t, (SUBLANES, s))
    steps = []
    for sh, pw in zip(SCAN_SHIFTS, (p1, p2, p4)):
        keep = (row + sh <= SUBLANES - 1) if reverse else (row >= sh)
        steps.append((jnp.where(keep, bc(pw[0]), 0.0), jnp.where(keep, bc(pw[1]), 0.0)))
    cr, ci = jnp.zeros((SUBLANES, s), F32), jnp.zeros((SUBLANES, s), F32)
    for r in range(SUBLANES):
        pw = pows[SUBLANES - 1 - r] if reverse else pows[r]
        cr = jnp.where(row == r, bc(pw[0]), cr)
        ci = jnp.where(row == r, bc(pw[1]), ci)
    return steps, (cr, ci)


def _scan_tile(xr, xi, steps, carry_pow, cr, ci, reverse):
    for sh, (ar, ai) in zip(SCAN_SHIFTS, steps):
        rs = SUBLANES - sh if reverse else sh
        sr, si = pltpu.roll(xr, rs, 0), pltpu.roll(xi, rs, 0)
        xr, xi = xr + ar * sr - ai * si, xi + ar * si + ai * sr
    pr, pi = carry_pow
    return xr + pr * cr - pi * ci, xi + pr * ci + pi * cr


def _s5_forward(proj, b_blk_re, b_blk_im, c_blk_re, c_blk_im, lam_re, lam_im, d_skip, n_blk, t_chunk):
    seq = proj.shape[0]
    n_chunks = seq // t_chunk
    n_tiles = t_chunk // SUBLANES
    s = b_blk_re.shape[2]

    def body(u_ref, bre_ref, bim_ref, cre_ref, cim_ref, lr_ref, li_ref, d_ref, y_ref, xr_ref, xi_ref, wr, wi, carry):
        t = pl.program_id(1)

        @pl.when(t == 0)
        def _():
            carry[...] = jnp.zeros_like(carry)

        u = u_ref[...]
        ub = u.astype(BF16)
        wr[...] = _dot(ub, bre_ref[...])
        wi[...] = _dot(ub, bim_ref[...])
        steps, cpow = _scan_coefs(lr_ref[...], li_ref[...], False)

        def tile(i, c):
            r0 = pl.multiple_of(i * SUBLANES, SUBLANES)
            xr, xi = _scan_tile(wr[pl.ds(r0, SUBLANES), :], wi[pl.ds(r0, SUBLANES), :], steps, cpow, c[0], c[1], False)
            xr_ref[pl.ds(r0, SUBLANES), :] = xr
            xi_ref[pl.ds(r0, SUBLANES), :] = xi
            last = SUBLANES - 1
            return (jnp.broadcast_to(xr[last:, :], xr.shape), jnp.broadcast_to(xi[last:, :], xi.shape))

        cr, ci = lax.fori_loop(0, n_tiles, tile, (carry[0], carry[1]))
        carry[0] = cr
        carry[1] = ci
        y = _dot(xr_ref[...].astype(BF16), cre_ref[...]) - _dot(xi_ref[...].astype(BF16), cim_ref[...])
        y_ref[...] = y + d_ref[...] * u

    blk = lambda shape: pl.BlockSpec((None,) + shape, lambda j, t: (j, 0, 0))
    return pl.pallas_call(
        body, name="s5_fwd", grid=(n_blk, n_chunks),
        in_specs=[pl.BlockSpec((t_chunk, LANES), lambda j, t: (t, j)), blk((LANES, s)), blk((LANES, s)),
                  blk((s, LANES)), blk((s, LANES)), blk((1, s)), blk((1, s)), pl.BlockSpec((1, LANES), lambda j, t: (0, j))],
        out_specs=[pl.BlockSpec((t_chunk, LANES), lambda j, t: (t, j)), pl.BlockSpec((t_chunk, s), lambda j, t: (t, j)),
                   pl.BlockSpec((t_chunk, s), lambda j, t: (t, j))],
        out_shape=[jax.ShapeDtypeStruct((seq, n_blk * LANES), F32), jax.ShapeDtypeStruct((seq, n_blk * s), F32),
                   jax.ShapeDtypeStruct((seq, n_blk * s), F32)],
        scratch_shapes=[pltpu.VMEM((t_chunk, s), F32), pltpu.VMEM((t_chunk, s), F32), pltpu.VMEM((2, SUBLANES, s), F32)],
        compiler_params=_params(2),
    )(proj, b_blk_re, b_blk_im, c_blk_re, c_blk_im, lam_re, lam_im, d_skip)


def _s5_backward(dypre, proj, x_re, x_im, b_blk_re, b_blk_im, c_blk_re, c_blk_im, lam_re, lam_im, d_skip, n_blk, t_chunk):
    seq = proj.shape[0]
    n_chunks = seq // t_chunk
    n_tiles = t_chunk // SUBLANES
    s = b_blk_re.shape[2]

    def body(dy_ref, u_ref, xr_ref, xi_ref, pr_ref, pi_ref, bre_ref, bim_ref, cre_ref, cim_ref, lr_ref, li_ref, d_ref,
             du_ref, dbre_ref, dbim_ref, dcre_ref, dcim_ref, dlr_ref, dli_ref, dd_ref, gr, gi, carry):
        t = pl.program_id(1)

        @pl.when(t == 0)
        def _():
            carry[...] = jnp.zeros_like(carry)
            for r in (dbre_ref, dbim_ref, dcre_ref, dcim_ref, dlr_ref, dli_ref, dd_ref):
                r[...] = jnp.zeros_like(r)

        dy = dy_ref[...]
        dyb = dy.astype(BF16)
        u = u_ref[...]
        gr[...] = _dot(dyb, cre_ref[...], NT)
        gi[...] = -_dot(dyb, cim_ref[...], NT)
        steps, cpow = _scan_coefs(lr_ref[...], -li_ref[...], True)
        row = lax.broadcasted_iota(jnp.int32, (SUBLANES, s), 0)
        last = SUBLANES - 1
        first_chunk = t == n_chunks - 1

        def tile_at(r0, prev_r, prev_i, c):
            cr, ci, ar, ai = c
            lr_, li_ = _scan_tile(gr[pl.ds(r0, SUBLANES), :], gi[pl.ds(r0, SUBLANES), :], steps, cpow, cr, ci, True)
            gr[pl.ds(r0, SUBLANES), :] = lr_
            gi[pl.ds(r0, SUBLANES), :] = li_
            xr, xi = xr_ref[pl.ds(r0, SUBLANES), :], xi_ref[pl.ds(r0, SUBLANES), :]
            xpr = jnp.where(row == 0, jnp.broadcast_to(prev_r[last:, :], xr.shape), pltpu.roll(xr, 1, 0))
            xpi = jnp.where(row == 0, jnp.broadcast_to(prev_i[last:, :], xi.shape), pltpu.roll(xi, 1, 0))
            ar = ar + lr_ * xpr + li_ * xpi
            ai = ai + li_ * xpr - lr_ * xpi
            return (jnp.broadcast_to(lr_[:1, :], lr_.shape), jnp.broadcast_to(li_[:1, :], li_.shape), ar, ai)

        def tile(ii, c):
            i = n_tiles - 1 - ii
            r0 = pl.multiple_of(i * SUBLANES, SUBLANES)
            rp = pl.multiple_of(r0 - SUBLANES, SUBLANES)
            return tile_at(r0, xr_ref[pl.ds(rp, SUBLANES), :], xi_ref[pl.ds(rp, SUBLANES), :], c)

        zero = jnp.zeros((SUBLANES, s), F32)
        c = lax.fori_loop(0, n_tiles - 1, tile, (carry[0], carry[1], zero, zero))
        keep = jnp.where(first_chunk, 0.0, 1.0)
        c = tile_at(0, pr_ref[...] * keep, pi_ref[...] * keep, c)
        carry[0] = c[0]
        carry[1] = c[1]
        dlr_ref[...] += jnp.sum(c[2], axis=0, keepdims=True)
        dli_ref[...] += jnp.sum(c[3], axis=0, keepdims=True)

        lam_r, lam_i = gr[...].astype(BF16), gi[...].astype(BF16)
        du_ref[...] = _dot(lam_r, bre_ref[...], NT) + _dot(lam_i, bim_ref[...], NT) + d_ref[...] * dy
        ub = u.astype(BF16)
        dbre_ref[...] += _dot(ub, lam_r, TN)
        dbim_ref[...] += _dot(ub, lam_i, TN)
        dcre_ref[...] += _dot(xr_ref[...].astype(BF16), dyb, TN)
        dcim_ref[...] -= _dot(xi_ref[...].astype(BF16), dyb, TN)
        dd_ref[...] += jnp.sum(dy * u, axis=0, keepdims=True)

    rev = lambda t: n_chunks - 1 - t
    blk = lambda shape: pl.BlockSpec((None,) + shape, lambda j, t: (j, 0, 0))
    tpc = t_chunk // SUBLANES
    prev_spec = pl.BlockSpec((SUBLANES, s), lambda j, t: (jnp.maximum(rev(t) * tpc - 1, 0), j))
    chunk = lambda w: pl.BlockSpec((t_chunk, w), lambda j, t: (rev(t), j))
    return pl.pallas_call(
        body, name="s5_bwd", grid=(n_blk, n_chunks),
        in_specs=[chunk(LANES), chunk(LANES), chunk(s), chunk(s), prev_spec, prev_spec, blk((LANES, s)), blk((LANES, s)),
                  blk((s, LANES)), blk((s, LANES)), blk((1, s)), blk((1, s)), pl.BlockSpec((1, LANES), lambda j, t: (0, j))],
        out_specs=[chunk(LANES), blk((LANES, s)), blk((LANES, s)), blk((s, LANES)), blk((s, LANES)), blk((1, s)), blk((1, s)),
                   pl.BlockSpec((1, LANES), lambda j, t: (0, j))],
        out_shape=[jax.ShapeDtypeStruct((seq, n_blk * LANES), F32),
                   jax.ShapeDtypeStruct((n_blk, LANES, s), F32), jax.ShapeDtypeStruct((n_blk, LANES, s), F32),
                   jax.ShapeDtypeStruct((n_blk, s, LANES), F32), jax.ShapeDtypeStruct((n_blk, s, LANES), F32),
                   jax.ShapeDtypeStruct((n_blk, 1, s), F32), jax.ShapeDtypeStruct((n_blk, 1, s), F32),
                   jax.ShapeDtypeStruct((1, n_blk * LANES), F32)],
        scratch_shapes=[pltpu.VMEM((t_chunk, s), F32), pltpu.VMEM((t_chunk, s), F32), pltpu.VMEM((2, SUBLANES, s), F32)],
        compiler_params=_params(2),
    )(dypre, proj, x_re, x_im, x_re, x_im, b_blk_re, b_blk_im, c_blk_re, c_blk_im, lam_re, lam_im, d_skip)


TQ, TK = 256, 128


def _split_bf16(x):
    hi = x.astype(BF16)
    return hi, (x - hi.astype(F32)).astype(BF16)


def _sb_weights(z, past, carry, tri):
    ls = jnp.minimum(z, 0.0) - jnp.log(1.0 + jnp.exp(-jnp.abs(z)))
    lk = ls - z
    if past is not None:
        lk = jnp.where(past, lk, 0.0)
    hi, lo = _split_bf16(lk)
    w = jnp.exp(ls + (_dot(hi, tri) + _dot(lo, tri)) + carry)
    if past is not None:
        w = jnp.where(past, w, 0.0)
    return ls, lk, w


def _walk_key_blocks(i, ratio, prologue, block, epilogue):
    n_kb = (i + 1) * ratio
    prologue(n_kb - 1)
    for n in range(ratio):
        block(n_kb - 1 - n, n % 2, True)

    def pair(t, carry):
        j = n_kb - 1 - ratio - 2 * t
        block(j, ratio % 2, False)
        block(j - 1, (ratio + 1) % 2, False)
        return carry

    lax.fori_loop(0, (i * ratio) // 2, pair, 0)
    epilogue()


def _attention_forward(qh, kh, proj, v_col, n_pair, tq, tk):
    seq = qh.shape[0]
    ratio = tq // tk
    assert ratio % 2 == 0

    def body(q_ref, k_ref, v_ref, o_ref, q_scr, z_scr, w_scr, acc_scr, c_scr):
        i = pl.program_id(1)
        lane = lax.broadcasted_iota(jnp.int32, (1, LANES), 1)
        q2 = q_ref[...]
        q_scr[0] = jnp.where(lane < HEAD_DIM, q2, 0.0).astype(BF16)
        q_scr[1] = jnp.where(lane < HEAD_DIM, 0.0, q2).astype(BF16)
        tri = (lax.broadcasted_iota(jnp.int32, (tk, tk), 0) > lax.broadcasted_iota(jnp.int32, (tk, tk), 1)).astype(BF16)
        qpos = i * tq + lax.broadcasted_iota(jnp.int32, (tq, tk), 0)
        kidx = lax.broadcasted_iota(jnp.int32, (tq, tk), 1)

        def rows(ref, j):
            j = jnp.clip(j, 0, seq // tk - 1)
            return ref[pl.ds(pl.multiple_of(j * tk, tk), tk), :].astype(BF16)

        def scores(j, slot):
            kb = rows(k_ref, j)
            for h in range(2):
                z_scr[slot, h] = _dot(q_scr[h], kb, NT)

        def finish(j):
            vb = rows(v_ref, j)
            for h in range(2):
                acc_scr[h] += _dot(w_scr[h], vb)

        def prologue(j):
            w_scr[...] = jnp.zeros_like(w_scr)
            acc_scr[...] = jnp.zeros_like(acc_scr)
            c_scr[...] = jnp.zeros_like(c_scr)
            scores(j, 0)

        def block(j, slot, masked):
            scores(j - 1, 1 - slot)
            finish(j + 1)
            past = ((kidx + j * tk) < qpos) if masked else None
            for h in range(2):
                _, lk, w = _sb_weights(z_scr[slot, h], past, c_scr[h], tri)
                w_scr[h] = w.astype(BF16)
                c_scr[h] += jnp.sum(lk, axis=-1, keepdims=True)

        _walk_key_blocks(i, ratio, prologue, block, lambda: finish(0))
        o_ref[...] = jnp.where(lane < HEAD_DIM, acc_scr[0], acc_scr[1])

    return pl.pallas_call(
        body, name="attn_fwd", grid=(n_pair, seq // tq),
        in_specs=[pl.BlockSpec((tq, LANES), lambda p, i: (i, p)), pl.BlockSpec((seq, LANES), lambda p, i: (0, p)),
                  pl.BlockSpec((seq, LANES), lambda p, i: (0, v_col + p))],
        out_specs=pl.BlockSpec((tq, LANES), lambda p, i: (i, p)),
        out_shape=jax.ShapeDtypeStruct(qh.shape, F32),
        scratch_shapes=[pltpu.VMEM((2, tq, LANES), BF16), pltpu.VMEM((2, 2, tq, tk), F32), pltpu.VMEM((2, tq, tk), BF16),
                        pltpu.VMEM((2, tq, LANES), F32), pltpu.VMEM((2, tq, 1), F32)],
        compiler_params=_params(2),
    )(qh, kh, proj)


def _attention_backward(qh, kh, proj, v_col, y, dy, n_pair, tq, tk):
    seq = qh.shape[0]

    ratio = tq // tk
    assert ratio % 2 == 0

    def body(q_ref, k_ref, v_ref, y_ref, dy_ref, dq_ref, dk_ref, dv_ref,
             q_scr, do_scr, z_scr, dw_scr, w_scr, dz_scr, dq_scr, c_scr, c2_scr, tot_scr):
        i = pl.program_id(1)

        @pl.when(i == 0)
        def _():
            dk_ref[...] = jnp.zeros_like(dk_ref)
            dv_ref[...] = jnp.zeros_like(dv_ref)

        lane = lax.broadcasted_iota(jnp.int32, (1, LANES), 1)
        sel = (lane < HEAD_DIM, lane >= HEAD_DIM)
        q2, do2 = q_ref[...], dy_ref[...].astype(BF16)
        dot_oy = do2.astype(F32) * y_ref[...]
        for h in range(2):
            q_scr[h] = jnp.where(sel[h], q2, 0.0).astype(BF16)
            do_scr[h] = jnp.where(sel[h], do2, jnp.zeros_like(do2))
            tot_scr[h] = jnp.sum(jnp.where(sel[h], dot_oy, 0.0), axis=-1, keepdims=True)
        r_i, c_i = lax.broadcasted_iota(jnp.int32, (tk, tk), 0), lax.broadcasted_iota(jnp.int32, (tk, tk), 1)
        tri = (r_i > c_i).astype(BF16)
        tri_ge = (r_i >= c_i).astype(BF16)
        qpos = i * tq + lax.broadcasted_iota(jnp.int32, (tq, tk), 0)
        kidx = lax.broadcasted_iota(jnp.int32, (tq, tk), 1)

        def start(j):
            return pl.multiple_of(jnp.clip(j, 0, seq // tk - 1) * tk, tk)

        def scores(j, slot):
            c0 = start(j)
            kb, vb = k_ref[pl.ds(c0, tk), :].astype(BF16), v_ref[pl.ds(c0, tk), :].astype(BF16)
            for h in range(2):
                z_scr[slot, h] = _dot(q_scr[h], kb, NT)
                dw_scr[slot, h] = _dot(do_scr[h], vb, NT)

        def finish(j):
            c0 = start(j)
            kb = k_ref[pl.ds(c0, tk), :].astype(BF16)
            dk_add, dv_add = jnp.zeros((tk, LANES), F32), jnp.zeros((tk, LANES), F32)
            for h in range(2):
                dz = dz_scr[h]
                dq_scr[h] += _dot(dz, kb)
                dk_add = dk_add + _dot(dz, q_scr[h], TN)
                dv_add = dv_add + _dot(w_scr[h], do_scr[h], TN)
            dk_ref[pl.ds(c0, tk), :] += dk_add
            dv_ref[pl.ds(c0, tk), :] += dv_add

        def prologue(j):
            for r in (w_scr, dz_scr, dq_scr, c_scr, c2_scr):
                r[...] = jnp.zeros_like(r)
            scores(j, 0)

        def block(j, slot, masked):
            scores(j - 1, 1 - slot)
            finish(j + 1)
            past = ((kidx + j * tk) < qpos) if masked else None
            for h in range(2):
                ls, lk, w = _sb_weights(z_scr[slot, h], past, c_scr[h], tri)
                wb = w.astype(BF16)
                dlw = dw_scr[slot, h] * wb.astype(F32)
                hi, lo = _split_bf16(dlw)
                dlk = tot_scr[h] - c2_scr[h] - (_dot(hi, tri_ge) + _dot(lo, tri_ge))
                if masked:
                    dlk = jnp.where(past, dlk, 0.0)
                sig = jnp.exp(ls)
                w_scr[h] = wb
                dz_scr[h] = (dlw * (1.0 - sig) - dlk * sig).astype(BF16)
                c_scr[h] += jnp.sum(lk, axis=-1, keepdims=True)
                c2_scr[h] += jnp.sum(dlw, axis=-1, keepdims=True)

        _walk_key_blocks(i, ratio, prologue, block, lambda: finish(0))
        dq_ref[...] = jnp.where(sel[0], dq_scr[0], dq_scr[1])

    blk = pl.BlockSpec((tq, LANES), lambda p, i: (i, p))
    full = pl.BlockSpec((seq, LANES), lambda p, i: (0, p))
    shape = jax.ShapeDtypeStruct(qh.shape, F32)
    return pl.pallas_call(
        body, name="attn_bwd", grid=(n_pair, seq // tq),
        in_specs=[blk, full, pl.BlockSpec((seq, LANES), lambda p, i: (0, v_col + p)), blk, blk],
        out_specs=[blk, full, full], out_shape=[shape, shape, shape],
        scratch_shapes=[pltpu.VMEM((2, tq, LANES), BF16), pltpu.VMEM((2, tq, LANES), BF16),
                        pltpu.VMEM((2, 2, tq, tk), F32), pltpu.VMEM((2, 2, tq, tk), F32),
                        pltpu.VMEM((2, tq, tk), BF16), pltpu.VMEM((2, tq, tk), BF16), pltpu.VMEM((2, tq, LANES), F32),
                        pltpu.VMEM((2, tq, 1), F32), pltpu.VMEM((2, tq, 1), F32), pltpu.VMEM((2, tq, 1), F32)],
        compiler_params=_params(2),
    )(qh, kh, proj, y, dy)


def _loss_head(h1, ffn, target, gate_f, tile):
    seq, d = h1.shape

    def body(h_ref, f_ref, t_ref, g_ref, dy_ref, df_ref, dg_ref, loss_ref):
        @pl.when(pl.program_id(0) == 0)
        def _():
            dg_ref[...] = jnp.zeros_like(dg_ref)
            loss_ref[...] = jnp.zeros_like(loss_ref)

        f, g = f_ref[...], g_ref[...]
        err = h_ref[...] + g * f - t_ref[...]
        dy = err * (1.0 / d)
        dy_ref[...] = dy
        df_ref[...] = (dy * g).astype(df_ref.dtype)
        dg_ref[...] += jnp.sum(dy * f, axis=0, keepdims=True)
        loss_ref[...] += jnp.sum(jnp.sum(err * err, axis=-1, keepdims=True), axis=0, keepdims=True) * (0.5 / d)

    row = _row_spec(tile, d)
    return pl.pallas_call(
        body, name="loss_head", grid=(seq // tile,), in_specs=[row, row, row, _vec_spec(d)],
        out_specs=[row, row, _vec_spec(d), pl.BlockSpec((1, 1), lambda i: (0, 0))],
        out_shape=[jax.ShapeDtypeStruct((seq, d), F32), jax.ShapeDtypeStruct((seq, d), BF16),
                   jax.ShapeDtypeStruct((1, d), F32), jax.ShapeDtypeStruct((1, 1), F32)],
        compiler_params=_params(1),
    )(h1, ffn, target, gate_f)


def _dot3(a, b, dn):
    ah, al = _split_bf16(a)
    bh, bl = _split_bf16(b)
    return _dot(ah, bh, dn) + (_dot(ah, bl, dn) + _dot(al, bh, dn))


def _ada_forward(c_all, w_shard, b_cols):
    d, n = w_shard.shape
    bk = _tile(d, 512)

    def body(c_ref, w_ref, b_ref, o_ref):
        @pl.when(pl.program_id(0) == 0)
        def _():
            o_ref[...] = jnp.broadcast_to(b_ref[...], o_ref.shape)

        o_ref[...] += _dot3(jax.nn.silu(c_ref[...]), w_ref[...], NN)

    return pl.pallas_call(
        body, name="ada_fwd", grid=(d // bk,),
        in_specs=[pl.BlockSpec((NDEV, bk), lambda k: (0, k)), pl.BlockSpec((bk, n), lambda k: (k, 0)), _vec_spec(n)],
        out_specs=pl.BlockSpec((NDEV, n), lambda k: (0, 0)), out_shape=jax.ShapeDtypeStruct((NDEV, n), F32),
        compiler_params=_params(1),
    )(c_all, w_shard, b_cols)


def _adam(w, g, m, v):
    m = ADAM_B1 * m + (1.0 - ADAM_B1) * g
    v = ADAM_B2 * v + (1.0 - ADAM_B2) * (g * g)
    m_hat = m / (1.0 - ADAM_B1 ** ADAM_STEP)
    v_hat = v / (1.0 - ADAM_B2 ** ADAM_STEP)
    return -ADAM_LR * (m_hat / (jnp.sqrt(v_hat) + ADAM_EPS) + ADAM_WD * w), m, v


def _adam_ada(c_all, dmod_cols, w, m, v):
    d, n = w.shape
    tr = _tile(d, 256)

    def body(c_ref, dm_ref, w_ref, m_ref, v_ref, g_ref, dl_ref, nm_ref, nv_ref):
        g = _dot3(jax.nn.silu(c_ref[...]), dm_ref[...], TN)
        delta, nm, nv = _adam(w_ref[...], g, m_ref[...], v_ref[...])
        g_ref[...] = g
        dl_ref[...] = delta
        nm_ref[...] = nm
        nv_ref[...] = nv

    row = _row_spec(tr, n)
    return pl.pallas_call(
        body, name="adam_ada", grid=(d // tr,),
        in_specs=[pl.BlockSpec((NDEV, tr), lambda i: (0, i)), pl.BlockSpec((NDEV, n), lambda i: (0, 0)), row, row, row],
        out_specs=[row] * 4, out_shape=[jax.ShapeDtypeStruct((d, n), F32)] * 4, compiler_params=_params(1),
    )(c_all, dmod_cols, w, m, v)


def _adam_sum(name, parts, part_spec, w, m, v, tr):
    r, c = w.shape

    def body(p_ref, w_ref, m_ref, v_ref, g_ref, dl_ref, nm_ref, nv_ref):
        g = p_ref[0].astype(F32)
        for k in range(1, NDEV):
            g = g + p_ref[k].astype(F32)
        delta, nm, nv = _adam(w_ref[...], g, m_ref[...], v_ref[...])
        g_ref[...] = g
        dl_ref[...] = delta
        nm_ref[...] = nm
        nv_ref[...] = nv

    row = _row_spec(tr, c)
    return pl.pallas_call(
        body, name=name, grid=(r // tr,), in_specs=[part_spec, row, row, row],
        out_specs=[row] * 4, out_shape=[jax.ShapeDtypeStruct((r, c), F32)] * 4, compiler_params=_params(1),
    )(parts, w, m, v)


GROUPS_PER_BLOCK = LANES // SSM_GROUP


def _to_b_blocks(bb, n_blk, p):
    t = bb.reshape(n_blk, GROUPS_PER_BLOCK, p, SSM_GROUP)
    eye = jnp.eye(GROUPS_PER_BLOCK, dtype=bb.dtype)
    return jnp.einsum("jgph,gk->jghkp", t, eye).reshape(n_blk, LANES, GROUPS_PER_BLOCK * p)


def _from_b_blocks(blk, n_blk, p):
    t = blk.reshape(n_blk, GROUPS_PER_BLOCK, SSM_GROUP, GROUPS_PER_BLOCK, p)
    eye = jnp.eye(GROUPS_PER_BLOCK, dtype=blk.dtype)
    return jnp.einsum("jghkp,gk->jgph", t, eye).reshape(n_blk * GROUPS_PER_BLOCK, p, SSM_GROUP)


def _to_c_blocks(cc, n_blk, p):
    t = cc.reshape(n_blk, GROUPS_PER_BLOCK, SSM_GROUP, p)
    eye = jnp.eye(GROUPS_PER_BLOCK, dtype=cc.dtype)
    return jnp.einsum("jghp,gk->jgpkh", t, eye).reshape(n_blk, GROUPS_PER_BLOCK * p, LANES)


def _from_c_blocks(blk, n_blk, p):
    t = blk.reshape(n_blk, GROUPS_PER_BLOCK, p, GROUPS_PER_BLOCK, SSM_GROUP)
    eye = jnp.eye(GROUPS_PER_BLOCK, dtype=blk.dtype)
    return jnp.einsum("jgpkh,gk->jghp", t, eye).reshape(n_blk * GROUPS_PER_BLOCK, SSM_GROUP, p)


SMALL = ("b_ada", "g_mix", "a_re", "a_im", "log_dt", "b_re", "b_im", "c_re", "c_im", "d_skip",
         "q_gain", "k_gain", "g_ssm_out", "g_attn_out", "g_ffn")
PACK_COLS = 1024


def _pack(arrs):
    flat = jnp.concatenate([a.reshape(-1) for a in arrs])
    n = flat.shape[0]
    quantum = SUBLANES * PACK_COLS
    padded = -(-n // quantum) * quantum
    return jnp.pad(flat, (0, padded - n)).reshape(padded // PACK_COLS, PACK_COLS)


def _unpack(packed, like):
    flat, out, off = packed.reshape(-1), [], 0
    for a in like:
        out.append(flat[off:off + a.size].reshape(a.shape))
        off += a.size
    return out


def kernel(x, c, w_ada, b_ada, g_mix, w_in, a_re, a_im, log_dt, b_re, b_im, c_re, c_im, d_skip, w_glu, q_gain, k_gain, g_ssm_out, g_attn_out, w_out, g_ffn, w_gate, w_up, w_down, loss_target, m_w_ada, m_b_ada, m_g_mix, m_w_in, m_a_re, m_a_im, m_log_dt, m_b_re, m_b_im, m_c_re, m_c_im, m_d_skip, m_w_glu, m_q_gain, m_k_gain, m_g_ssm_out, m_g_attn_out, m_w_out, m_g_ffn, m_w_gate, m_w_up, m_w_down, v_w_ada, v_b_ada, v_g_mix, v_w_in, v_a_re, v_a_im, v_log_dt, v_b_re, v_b_im, v_c_re, v_c_im, v_d_skip, v_w_glu, v_q_gain, v_k_gain, v_g_ssm_out, v_g_attn_out, v_w_out, v_g_ffn, v_w_gate, v_w_up, v_w_down):
    given = dict(locals())
    seq, d = x.shape[1], x.shape[2]
    xs, tgt = x[0], loss_target[0]
    n_groups, p_state = a_re.shape[1], a_re.shape[2]
    w_ssm = n_groups * SSM_GROUP
    w_attn = w_in.shape[2] * NDEV - w_ssm
    w_attn //= 3
    n_blk, n_pair = w_ssm // LANES, w_attn // LANES
    n_heads = w_attn // HEAD_DIM
    ns_in, ns_ff = w_in.shape[2], w_gate.shape[2]
    d_mix = w_ssm + w_attn
    mx, my, mc = _me()
    me = 4 * mx + 2 * my + mc
    rt = _tile(seq, 256)
    n_rt = seq // rt
    sds = jax.ShapeDtypeStruct

    order_tok = [jnp.zeros((), F32)]

    def start(x, gather, name):
        handle, token = _exchange_start(x, gather, name)
        order_tok[0] = order_tok[0] + token[0, 0]
        return handle

    def behind(v):
        return v + order_tok[0].astype(v.dtype)

    c_all = _exchange(c, True, "comm_ag_c")[0].reshape(NDEV, d)
    n_ada = w_ada.shape[2]
    b_cols = lax.dynamic_slice(b_ada, (0, me * n_ada), (1, n_ada))
    mod_cols = _ada_forward(c_all, w_ada[0], b_cols)
    mod_all, mod_token = _exchange(mod_cols, True, "comm_ag_mod")
    mod = lax.dynamic_slice(mod_all, (0, me, 0), (NDEV, 1, n_ada)).reshape(1, NDEV * n_ada)
    order_tok[0] = mod_token[0, 0]
    h_w_in = start(behind(w_in[0]).astype(BF16), True, "comm_ag_w_in")
    h_w_glu = start(behind(w_glu[0]).astype(BF16), True, "comm_ag_w_glu")
    h_w_out = start(behind(w_out[0]).astype(BF16), True, "comm_ag_w_out")
    h_w_gu = start(behind(jnp.stack([w_gate[0], w_up[0]])).astype(BF16), True, "comm_ag_w_gu")
    h_w_down = start(behind(w_down[0]).astype(BF16), True, "comm_ag_w_down")
    mod = behind(mod)
    shift_m, scale_m, gate_m, shift_f, scale_f, gate_f = [mod[:, i * d:(i + 1) * d] for i in range(6)]

    gp = n_groups * p_state
    a_re2, a_im2, ldt2 = a_re[0], a_im[0], log_dt[0].reshape(n_groups, 1)
    b_re2, b_im2 = b_re[0].reshape(gp, SSM_GROUP), b_im[0].reshape(gp, SSM_GROUP)
    lam_r, lam_i, coef_r, coef_i = _whole("s5_lam", _s5_lam, [a_re2, a_im2, ldt2], [(n_groups, p_state)] * 4)
    coef_r2, coef_i2 = coef_r.reshape(gp, 1), coef_i.reshape(gp, 1)
    bb_r, bb_i = _whole("s5_bbar", _s5_bbar, [coef_r2, coef_i2, b_re2, b_im2], [(gp, SSM_GROUP)] * 2)
    s_blk = GROUPS_PER_BLOCK * p_state
    b_blk_r = _to_b_blocks(bb_r.reshape(n_groups, p_state, SSM_GROUP), n_blk, p_state).astype(BF16)
    b_blk_i = _to_b_blocks(bb_i.reshape(n_groups, p_state, SSM_GROUP), n_blk, p_state).astype(BF16)
    c_blk_r = _to_c_blocks(c_re[0], n_blk, p_state).astype(BF16)
    c_blk_i = _to_c_blocks(c_im[0], n_blk, p_state).astype(BF16)
    lam_r3, lam_i3 = lam_r.reshape(n_blk, 1, s_blk), lam_i.reshape(n_blk, 1, s_blk)
    d_skip2 = d_skip[0].reshape(1, w_ssm)

    row_d, vec_d = _row_spec(rt, d), _vec_spec(d)
    xm, = _rowwise_fwd("seg_in", lambda *a: _seg_in(*a)[:1], [xs], [row_d], [shift_m, scale_m, g_mix], [vec_d] * 3,
                       [sds((seq, d), BF16)], [row_d], n_rt)
    bn_in = _tile(ns_in, 512)
    per = ns_in // bn_in
    bm, bk = _tile(seq, BM), _tile(d, BK)
    w_in_g = _exchange_finish(h_w_in, xm)
    proj = _mm("mm_in", xm, w_in_g, NN, (seq // bm, NDEV * per, d // bk),
               pl.BlockSpec((bm, bk), lambda i, j, k: (i, k)),
               pl.BlockSpec((None, bk, bn_in), lambda i, j, k: (j // per, k, j % per)),
               pl.BlockSpec((bm, bn_in), lambda i, j, k: (i, j)), (seq, NDEV * ns_in), F32, (bm, bn_in))
    q_col, k_col, v_col = w_ssm // w_attn, w_ssm // w_attn + 1, (w_ssm + 2 * w_attn) // LANES
    qg_t, kg_t = jnp.tile(q_gain, (1, n_heads)), jnp.tile(k_gain, (1, n_heads))
    row_a, vec_a = _row_spec(rt, w_attn), _vec_spec(w_attn)
    qk_rows, qk_specs = [proj, proj], [_row_spec(rt, w_attn, q_col), _row_spec(rt, w_attn, k_col)]
    qh, kh = _rowwise_fwd("seg_qk", _seg_qk, qk_rows, qk_specs, [qg_t, kg_t], [vec_a] * 2,
                          [sds((seq, w_attn), F32)] * 2, [row_a] * 2, n_rt)
    t_chunk = _tile(seq, 256)
    ypre, x_re, x_im = _s5_forward(proj, b_blk_r, b_blk_i, c_blk_r, c_blk_i, lam_r3, lam_i3, d_skip2, n_blk, t_chunk)
    tq, tk = _tile(seq, TQ), _tile(seq, TK)
    y_attn = _attention_forward(qh, kh, proj, v_col, n_pair, tq, tk)
    row_s, vec_s = _row_spec(rt, w_ssm), _vec_spec(w_ssm)
    y1, = _rowwise_fwd("seg_gelu", _seg_gelu, [ypre], [row_s], [], [], [sds((seq, w_ssm), F32)], [row_s], n_rt)
    w_glu_g = _exchange_finish(h_w_glu, y1).reshape(w_ssm, w_ssm)
    z = _mm_plain("mm_glu", y1, w_glu_g, NN, F32)
    row_m = _row_spec(rt, d_mix)
    mixed, = _rowwise_fwd("seg_mix", _seg_mix, [y1, z, y_attn], [row_s, row_s, row_a], [g_ssm_out, g_attn_out], [vec_s, vec_a],
                          [sds((seq, d_mix), BF16)], [row_m], n_rt)
    w_out_g = _exchange_finish(h_w_out, mixed).reshape(d_mix, d)
    o = _mm_plain("mm_out", mixed, w_out_g, NN, F32)
    h1, xf = _rowwise_fwd("seg_mid", _seg_mid, [xs, o], [row_d] * 2, [gate_m, g_ffn, scale_f, shift_f], [vec_d] * 4,
                          [sds((seq, d), F32), sds((seq, d), BF16)], [row_d] * 2, n_rt)
    w_gu_g = _exchange_finish(h_w_gu, xf).reshape(2 * NDEV, d, ns_ff)
    gu = _mm("mm_gu", xf, w_gu_g, NN, (seq // bm, 2 * NDEV, d // bk),
             pl.BlockSpec((bm, bk), lambda i, j, k: (i, k)), pl.BlockSpec((None, bk, ns_ff), lambda i, j, k: (j, k, 0)),
             pl.BlockSpec((None, bm, ns_ff), lambda i, j, k: (j, i, 0)), (2 * NDEV, seq, ns_ff), F32, (bm, ns_ff))
    gu4 = gu.reshape(NDEV, 2, seq, ns_ff)
    ft = _tile(seq, 512)
    pair_spec = pl.BlockSpec((None, 2, ft, ns_ff), lambda s, i: (s, 0, i, 0))
    one_spec = pl.BlockSpec((None, ft, ns_ff), lambda s, i: (s, i, 0))

    def act_body(gu_ref, a_ref):
        a_ref[...] = _seg_act(gu_ref[0], gu_ref[1])[0].astype(a_ref.dtype)

    act = pl.pallas_call(act_body, name="seg_act", grid=(NDEV, seq // ft), in_specs=[pair_spec], out_specs=one_spec,
                         out_shape=sds((NDEV, seq, ns_ff), BF16), compiler_params=_params(2))(gu4)
    bn_d = _tile(d, BN)
    w_down_g = _exchange_finish(h_w_down, act)
    ffn = _mm("mm_down", act, w_down_g, NN, (seq // bm, d // bn_d, NDEV),
              pl.BlockSpec((None, bm, ns_ff), lambda i, j, k: (k, i, 0)), pl.BlockSpec((None, ns_ff, bn_d), lambda i, j, k: (k, 0, j)),
              pl.BlockSpec((bm, bn_d), lambda i, j, k: (i, j)), (seq, d), F32, (bm, bn_d))
    dy, dffn, d_gate_f, loss_part = _loss_head(h1, ffn, tgt, gate_f, rt)
    loss = lax.psum(loss_part[0, 0], MESH_AXES)

    tok_tile = lambda: jnp.broadcast_to(order_tok[0], (SUBLANES, LANES))
    bl = _tile(seq, BK)
    gw_down = _mm("mm_dw_down", act, dffn, TN, (NDEV, d // bn_d, seq // bl),
                  pl.BlockSpec((None, bl, ns_ff), lambda i, j, k: (i, k, 0)), pl.BlockSpec((bl, bn_d), lambda i, j, k: (k, j)),
                  pl.BlockSpec((None, ns_ff, bn_d), lambda i, j, k: (i, 0, j)), (NDEV, ns_ff, d), BF16, (ns_ff, bn_d))
    h_g_down = start(gw_down, False, "comm_a2a_w_down")
    dact = _mm("mm_dact", dffn, w_down_g, NT, (seq // bm, NDEV, d // bk),
               pl.BlockSpec((bm, bk), lambda i, j, k: (i, k)), pl.BlockSpec((None, ns_ff, bk), lambda i, j, k: (j, 0, k)),
               pl.BlockSpec((None, bm, ns_ff), lambda i, j, k: (j, i, 0)), (NDEV, seq, ns_ff), F32, (bm, ns_ff),
               behind=tok_tile())

    def dact_body(gu_ref, da_ref, dgu_ref):
        _, vjp = jax.vjp(_seg_act, gu_ref[0], gu_ref[1])
        dg, du_ = vjp((da_ref[...],))
        dgu_ref[0] = dg.astype(dgu_ref.dtype)
        dgu_ref[1] = du_.astype(dgu_ref.dtype)

    dgu4 = pl.pallas_call(dact_body, name="seg_act_bwd", grid=(NDEV, seq // ft), in_specs=[pair_spec, one_spec],
                          out_specs=pair_spec, out_shape=sds((NDEV, 2, seq, ns_ff), BF16), compiler_params=_params(2))(gu4, dact)
    dgu = dgu4.reshape(2 * NDEV, seq, ns_ff)
    bmd = _tile(d, BM)
    gw_gu = _mm("mm_dw_gu", xf, dgu, TN, (d // bmd, 2 * NDEV, seq // bl),
                pl.BlockSpec((bl, bmd), lambda i, j, k: (k, i)), pl.BlockSpec((None, bl, ns_ff), lambda i, j, k: (j, k, 0)),
                pl.BlockSpec((None, bmd, ns_ff), lambda i, j, k: (j, i, 0)), (2 * NDEV, d, ns_ff), BF16, (bmd, ns_ff))
    h_g_gu = start(gw_gu.reshape(NDEV, 2, d, ns_ff), False, "comm_a2a_w_gu")
    dxf = _mm("mm_dxf", dgu, w_gu_g, NT, (seq // bm, d // bn_d, 2 * NDEV),
              pl.BlockSpec((None, bm, ns_ff), lambda i, j, k: (k, i, 0)), pl.BlockSpec((None, bn_d, ns_ff), lambda i, j, k: (k, j, 0)),
              pl.BlockSpec((bm, bn_d), lambda i, j, k: (i, j)), (seq, d), F32, (bm, bn_d), behind=tok_tile())
    (do, dx_a, d_gate_m, d_g_ffn, d_scale_f, d_shift_f) = _rowwise_bwd(
        "seg_mid_bwd", _seg_mid, [xs, o], [row_d] * 2, [gate_m, g_ffn, scale_f, shift_f], [vec_d] * 4,
        [dy, dxf], [row_d] * 2, [[0], [1]], [1, 0], [sds((seq, d), BF16), sds((seq, d), F32)], [row_d] * 2,
        [0, 1, 2, 3], [sds((1, d), F32)] * 4, [vec_d] * 4, n_rt)

    dmixed = _mm_plain("mm_dmixed", do, w_out_g, NT, F32)
    gw_out = _mm_plain("mm_dw_out", mixed, do, TN, BF16)
    h_g_out = start(gw_out.reshape(NDEV, w_out.shape[1], d), False, "comm_a2a_w_out")
    (dz, dy1_a, dy_attn, d_g_ssm, d_g_attn) = _rowwise_bwd(
        "seg_mix_bwd", _seg_mix, [y1, z, y_attn], [row_s, row_s, row_a], [behind(g_ssm_out), g_attn_out], [vec_s, vec_a],
        [dmixed], [row_m], [[0]], [1, 0, 2], [sds((seq, w_ssm), BF16), sds((seq, w_ssm), F32), sds((seq, w_attn), F32)],
        [row_s, row_s, row_a], [0, 1], [sds((1, w_ssm), F32), sds((1, w_attn), F32)], [vec_s, vec_a], n_rt)
    dy1_b = _mm_plain("mm_dy1", dz, w_glu_g, NT, F32)
    gw_glu = _mm_plain("mm_dw_glu", y1, dz, TN, BF16)
    h_g_glu = start(gw_glu.reshape(NDEV, w_glu.shape[1], w_ssm), False, "comm_a2a_w_glu")
    (dypre,) = _rowwise_bwd("seg_gelu_bwd", _seg_gelu, [ypre], [row_s], [], [], [dy1_a, dy1_b], [row_s] * 2, [[0, 1]],
                            [0], [sds((seq, w_ssm), F32)], [row_s], [], [], [], n_rt)
    (du, db_blk_r, db_blk_i, dc_blk_r, dc_blk_i, dlam_r3, dlam_i3, dd_skip2) = _s5_backward(
        dypre, proj, x_re, x_im, b_blk_r, b_blk_i, c_blk_r, c_blk_i, lam_r3, lam_i3, behind(d_skip2), n_blk, t_chunk)
    dqh, dkh, dv = _attention_backward(qh, kh, proj, v_col, y_attn, dy_attn, n_pair, tq, tk)
    (dq, dk, dqg_t, dkg_t) = _rowwise_bwd(
        "seg_qk_bwd", _seg_qk, qk_rows, qk_specs, [qg_t, kg_t], [vec_a] * 2, [dqh, dkh], [row_a] * 2, [[0], [1]],
        [0, 1], [sds((seq, w_attn), BF16)] * 2, [row_a] * 2, [0, 1], [sds((1, w_attn), F32)] * 2, [vec_a] * 2, n_rt)
    dproj = jnp.concatenate([du.astype(BF16), dq, dk, dv.astype(BF16)], axis=-1)
    bk_in = _tile(ns_in, BK)
    per_k = ns_in // bk_in
    gw_in = _mm("mm_dw_in", xm, dproj, TN, (d // bmd, NDEV * per, seq // bl),
                pl.BlockSpec((bl, bmd), lambda i, j, k: (k, i)), pl.BlockSpec((bl, bn_in), lambda i, j, k: (k, j)),
                pl.BlockSpec((None, bmd, bn_in), lambda i, j, k: (j // per, i, j % per)), (NDEV, d, ns_in), BF16, (bmd, bn_in))
    h_g_in = start(gw_in, False, "comm_a2a_w_in")
    dxm = _mm("mm_dxm", dproj, w_in_g, NT, (seq // bm, d // bn_d, NDEV * per_k),
              pl.BlockSpec((bm, bk_in), lambda i, j, k: (i, k)),
              pl.BlockSpec((None, bn_d, bk_in), lambda i, j, k: (k // per_k, j, k % per_k)),
              pl.BlockSpec((bm, bn_d), lambda i, j, k: (i, j)), (seq, d), F32, (bm, bn_d), behind=tok_tile())
    (grad_x, d_shift_m, d_scale_m, d_g_mix) = _rowwise_bwd(
        "seg_in_bwd", _seg_in, [xs], [row_d], [shift_m, scale_m, behind(g_mix)], [vec_d] * 3, [dxm, dx_a], [row_d] * 2, [[0], [1]],
        [0], [sds((seq, d), F32)], [row_d], [0, 1, 2], [sds((1, d), F32)] * 3, [vec_d] * 3, n_rt)

    dbb_r = _from_b_blocks(db_blk_r, n_blk, p_state).reshape(gp, SSM_GROUP)
    dbb_i = _from_b_blocks(db_blk_i, n_blk, p_state).reshape(gp, SSM_GROUP)
    dcoef_r2, dcoef_i2, db_re2, db_im2 = _whole_vjp("s5_bbar_bwd", _s5_bbar, [coef_r2, coef_i2, b_re2, b_im2], [dbb_r, dbb_i],
                                                    [(gp, 1), (gp, 1), (gp, SSM_GROUP), (gp, SSM_GROUP)])
    lam_cts = [dlam_r3.reshape(n_groups, p_state), dlam_i3.reshape(n_groups, p_state),
               dcoef_r2.reshape(n_groups, p_state), dcoef_i2.reshape(n_groups, p_state)]
    da_re2, da_im2, dldt2 = _whole_vjp("s5_lam_bwd", _s5_lam, [a_re2, a_im2, ldt2], lam_cts,
                                       [(n_groups, p_state), (n_groups, p_state), (n_groups, 1)])
    dc_re2, dc_im2 = _from_c_blocks(dc_blk_r, n_blk, p_state), _from_c_blocks(dc_blk_i, n_blk, p_state)

    dmod = jnp.concatenate([d_shift_m, d_scale_m, d_gate_m, d_shift_f, d_scale_f, d_gate_f], axis=-1)
    small_part = {
        "b_ada": dmod, "g_mix": d_g_mix, "a_re": da_re2, "a_im": da_im2, "log_dt": dldt2, "b_re": db_re2, "b_im": db_im2,
        "c_re": dc_re2, "c_im": dc_im2, "d_skip": dd_skip2,
        "q_gain": dqg_t.reshape(n_heads, HEAD_DIM).sum(0), "k_gain": dkg_t.reshape(n_heads, HEAD_DIM).sum(0),
        "g_ssm_out": d_g_ssm, "g_attn_out": d_g_attn, "g_ffn": d_g_ffn,
    }
    h_small = start(_pack([small_part[n] for n in SMALL]), True, "comm_ag_small")

    big = {}
    after = jnp.broadcast_to(order_tok[0], (SUBLANES, LANES))

    def sharded(nm, handle, part_block, index_map, tr, after):
        got = _exchange_finish(handle, after)
        big[nm] = _adam_sum("adam_" + nm, got, pl.BlockSpec(part_block, index_map), given[nm][0], given["m_" + nm][0],
                            given["v_" + nm][0], tr)
        return got

    tr = _tile(w_down.shape[1], 64)
    sharded("w_down", h_g_down, (NDEV, tr, d), lambda i: (0, i, 0), tr, after)
    tr = _tile(d, 256)
    got_gu = sharded("w_gate", h_g_gu, (NDEV, None, tr, ns_ff), lambda i: (0, 0, i, 0), tr, big["w_down"][1])
    big["w_up"] = _adam_sum("adam_w_up", got_gu, pl.BlockSpec((NDEV, None, tr, ns_ff), lambda i: (0, 1, i, 0)),
                            w_up[0], m_w_up[0], v_w_up[0], tr)
    tr = _tile(w_out.shape[1], 128)
    sharded("w_out", h_g_out, (NDEV, tr, d), lambda i: (0, i, 0), tr, big["w_up"][1])
    tr = _tile(w_glu.shape[1], 128)
    sharded("w_glu", h_g_glu, (NDEV, tr, w_ssm), lambda i: (0, i, 0), tr, big["w_out"][1])
    tr = _tile(d, 256)
    sharded("w_in", h_g_in, (NDEV, tr, ns_in), lambda i: (0, i, 0), tr, big["w_glu"][1])

    packed_parts = _exchange_finish(h_small, big["w_in"][1])
    rows_p = packed_parts.shape[1]
    tr_p = _tile(rows_p, 64)
    small_spec = pl.BlockSpec((NDEV, tr_p, PACK_COLS), lambda i: (0, i, 0))
    sm = _adam_sum("adam_small", packed_parts, small_spec, _pack([given[n] for n in SMALL]),
                   _pack([given["m_" + n] for n in SMALL]), _pack([given["v_" + n] for n in SMALL]), tr_p)
    like = [given[n] for n in SMALL]
    small_out = [dict(zip(SMALL, _unpack(t, like))) for t in sm]

    dmod_all = packed_parts[:, :(6 * d) // PACK_COLS, :].reshape(NDEV, 6 * d) if (6 * d) % PACK_COLS == 0 else None
    assert dmod_all is not None
    dmod_cols = lax.dynamic_slice(dmod_all, (0, me * n_ada), (NDEV, n_ada))
    big["w_ada"] = _adam_ada(c_all, dmod_cols, w_ada[0], m_w_ada[0], v_w_ada[0])

    order = ("w_ada", "b_ada", "g_mix", "w_in", "a_re", "a_im", "log_dt", "b_re", "b_im", "c_re", "c_im", "d_skip", "w_glu",
             "q_gain", "k_gain", "g_ssm_out", "g_attn_out", "w_out", "g_ffn", "w_gate", "w_up", "w_down")
    outs = [loss, grad_x[None]]
    for kind in range(4):
        for n in order:
            outs.append(big[n][kind][None] if n in big else small_out[kind][n])
    return tuple(outs)
```

```python
import functools
import math

import jax
import jax.numpy as jnp
from jax import lax
from jax.experimental import pallas as pl
from jax.experimental.pallas import tpu as pltpu

F32 = jnp.float32
BF16 = jnp.bfloat16
NDEV = 8
MESH_AXES = ("x", "y", "c")
MESH_ID = pl.DeviceIdType.MESH
EPS = 1e-6
LANES = 128
SUBLANES = 8
HEAD_DIM = 64
SSM_GROUP = 16
ADAM_LR, ADAM_B1, ADAM_B2, ADAM_EPS, ADAM_WD, ADAM_STEP = 0.001, 0.9, 0.999, 1e-08, 0.01, 10

NN = (((1,), (0,)), ((), ()))
NT = (((1,), (1,)), ((), ()))
TN = (((0,), (0,)), ((), ()))


def _dot(a, b, dn=NN):
    return lax.dot_general(a, b, dn, preferred_element_type=F32)


def _tile(dim, pref):
    t = min(dim, pref)
    while dim % t:
        t //= 2
    return t


def _params(n):
    return pltpu.CompilerParams(dimension_semantics=("arbitrary",) * n)


def _me():
    mx, my, mc = lax.axis_index("x"), lax.axis_index("y"), lax.axis_index("c")
    return mx, my, mc


def _peer(mx, my, mc, k):
    px = 1 - mx if (k >> 2) & 1 else mx
    py = 1 - my if (k >> 1) & 1 else my
    pc = 1 - mc if k & 1 else mc
    return (px, py, pc), 4 * px + 2 * py + pc


def _exchange_copies(x_ref, land_ref, send_sems, recv_sems, gather):
    mx, my, mc = _me()
    me = 4 * mx + 2 * my + mc
    pairs = []
    for k in range(1, NDEV):
        peer, pidx = _peer(mx, my, mc, k)
        src = x_ref if gather else x_ref.at[pidx]
        mk = lambda dst, src=src, k=k, peer=peer: pltpu.make_async_remote_copy(
            src_ref=src, dst_ref=dst, send_sem=send_sems.at[k - 1], recv_sem=recv_sems.at[k - 1],
            device_id=peer, device_id_type=MESH_ID)
        pairs.append((mk(land_ref.at[me]), mk(land_ref.at[pidx])))
    return me, pairs


def _exchange(x, gather, name):
    def body(x_ref, o_ref, token, send_sems, recv_sems, local_sem):
        me, pairs = _exchange_copies(x_ref, o_ref, send_sems, recv_sems, gather)
        local = pltpu.make_async_copy(x_ref if gather else x_ref.at[me], o_ref.at[me], local_sem)
        local.start()
        for send, _ in pairs:
            send.start()
        for _, arrival in pairs:
            arrival.wait_recv()
        for send, _ in pairs:
            send.wait_send()
        local.wait()
        token[...] = jnp.zeros_like(token)

    return pl.pallas_call(
        body, name=name,
        out_shape=(jax.ShapeDtypeStruct(((NDEV,) + x.shape) if gather else x.shape, x.dtype),
                   jax.ShapeDtypeStruct((SUBLANES, LANES), F32)),
        in_specs=[pl.BlockSpec(memory_space=pl.ANY)],
        out_specs=(pl.BlockSpec(memory_space=pl.ANY), pl.BlockSpec(memory_space=pltpu.VMEM)),
        scratch_shapes=[pltpu.SemaphoreType.DMA((NDEV - 1,)), pltpu.SemaphoreType.DMA((NDEV - 1,)), pltpu.SemaphoreType.DMA],
    )(x)


HBM_SPEC = pl.BlockSpec(memory_space=pltpu.HBM)
SEM_SPEC = pl.BlockSpec(memory_space=pltpu.SEMAPHORE)
SIDE_EFFECT = pltpu.SideEffectType.DATAFLOW_SIDE_EFFECTING


def _exchange_start(x, gather, name):
    land_shape = ((NDEV,) + x.shape) if gather else x.shape

    def body(x_ref, land_ref, send_sems, recv_sems, local_sem, x_thru, land_thru, token):
        me, pairs = _exchange_copies(x_ref, land_ref, send_sems, recv_sems, gather)
        for send, _ in pairs:
            send.start()
        pltpu.make_async_copy(x_ref if gather else x_ref.at[me], land_ref.at[me], local_sem).start()
        token[...] = jnp.zeros_like(token)

    sems = pltpu.SemaphoreType.DMA((NDEV - 1,))
    send_sems, recv_sems, local_sem, x_thru, land_thru, token = pl.pallas_call(
        body, name=name + "_start",
        out_shape=(sems, sems, pltpu.SemaphoreType.DMA(()), pltpu.HBM(x.shape, x.dtype), pltpu.HBM(land_shape, x.dtype),
                   jax.ShapeDtypeStruct((SUBLANES, LANES), F32)),
        in_specs=(HBM_SPEC, HBM_SPEC),
        out_specs=(SEM_SPEC, SEM_SPEC, SEM_SPEC, HBM_SPEC, HBM_SPEC, pl.BlockSpec(memory_space=pltpu.VMEM)),
        input_output_aliases={0: 3, 1: 4}, compiler_params=pltpu.CompilerParams(has_side_effects=SIDE_EFFECT),
    )(pltpu.with_memory_space_constraint(x, pltpu.HBM),
      pltpu.with_memory_space_constraint(lax.empty(land_shape, x.dtype), pltpu.HBM))
    return (send_sems, recv_sems, local_sem, x_thru, land_thru, gather, name), token


def _exchange_finish(handle, after):
    send_sems, recv_sems, local_sem, x_thru, land_thru, gather, name = handle

    def body(x_ref, land_ref, send_sems, recv_sems, local_sem, after_ref, x_dead, got_ref):
        me, pairs = _exchange_copies(x_ref, land_ref, send_sems, recv_sems, gather)
        for send, arrival in pairs:
            send.wait_send()
            arrival.wait_recv()
        pltpu.make_async_copy(x_ref if gather else x_ref.at[me], land_ref.at[me], local_sem).wait()

    return pl.pallas_call(
        body, name=name + "_wait",
        out_shape=(pltpu.HBM(x_thru.shape, x_thru.dtype), pltpu.HBM(land_thru.shape, land_thru.dtype)),
        in_specs=(HBM_SPEC, HBM_SPEC, SEM_SPEC, SEM_SPEC, SEM_SPEC, HBM_SPEC), out_specs=(HBM_SPEC, HBM_SPEC),
        input_output_aliases={0: 0, 1: 1}, compiler_params=pltpu.CompilerParams(has_side_effects=SIDE_EFFECT),
    )(x_thru, land_thru, send_sems, recv_sems, local_sem, pltpu.with_memory_space_constraint(after, pltpu.HBM))[1]


def _mm(name, a, b, dn, grid, a_spec, b_spec, o_spec, out_shape, out_dtype, acc_shape, behind=None):
    nk = grid[2]
    extra = [] if behind is None else [pltpu.with_memory_space_constraint(behind, pltpu.HBM)]

    def body(a_ref, b_ref, *rest):
        rest = rest[len(extra):]
        o_ref = rest[0]
        part = _dot(a_ref[...].astype(BF16), b_ref[...].astype(BF16), dn)
        if nk == 1:
            o_ref[...] = part.astype(o_ref.dtype)
            return
        acc_ref = rest[1]
        k = pl.program_id(2)

        @pl.when(k == 0)
        def _():
            acc_ref[...] = part

        @pl.when(k > 0)
        def _():
            acc_ref[...] += part

        @pl.when(k == nk - 1)
        def _():
            o_ref[...] = acc_ref[...].astype(o_ref.dtype)

    return pl.pallas_call(
        body, name=name, grid=grid, in_specs=[a_spec, b_spec] + [HBM_SPEC] * len(extra), out_specs=o_spec,
        out_shape=jax.ShapeDtypeStruct(out_shape, out_dtype),
        scratch_shapes=[] if nk == 1 else [pltpu.VMEM(acc_shape, F32)], compiler_params=_params(3),
    )(a, b, *extra)


BM, BN, BK = 1024, 1024, 4096


def _mm_plain(name, a, b, dn, out_dtype):
    if dn == NN:
        (m, kk), n = a.shape, b.shape[1]
    elif dn == NT:
        (m, kk), n = a.shape, b.shape[0]
    else:
        (kk, m), n = a.shape, b.shape[1]
    half = 2 if dn == TN else 1
    bm, bn, bk = _tile(m, BM // half), _tile(n, BN // half), _tile(kk, BK)
    a_spec = pl.BlockSpec((bk, bm), lambda i, j, k: (k, i)) if dn == TN else pl.BlockSpec((bm, bk), lambda i, j, k: (i, k))
    b_spec = pl.BlockSpec((bn, bk), lambda i, j, k: (j, k)) if dn == NT else pl.BlockSpec((bk, bn), lambda i, j, k: (k, j))
    return _mm(name, a, b, dn, (m // bm, n // bn, kk // bk), a_spec, b_spec,
               pl.BlockSpec((bm, bn), lambda i, j, k: (i, j)), (m, n), out_dtype, (bm, bn))


def _row_spec(tile, width, col=0):
    return pl.BlockSpec((tile, width), lambda i: (i, col))


def _vec_spec(width, col=0):
    return pl.BlockSpec((1, width), lambda i: (0, col))


def _rowwise_fwd(name, fn, rows, row_specs, vecs, vec_specs, out_shapes, out_specs, n_tiles):
    nr, nv = len(rows), len(vecs)

    def body(*refs):
        ins = [r[...].astype(F32) for r in refs[:nr + nv]]
        outs = fn(*ins)
        for o_ref, o in zip(refs[nr + nv:], outs):
            o_ref[...] = o.astype(o_ref.dtype)

    return pl.pallas_call(body, name=name, grid=(n_tiles,), in_specs=list(row_specs) + list(vec_specs),
                          out_specs=list(out_specs), out_shape=list(out_shapes), compiler_params=_params(1))(*rows, *vecs)


def _rowwise_bwd(name, fn, rows, row_specs, vecs, vec_specs, cts, ct_specs, ct_groups,
                 drow_idx, drow_shapes, drow_specs, dvec_idx, dvec_shapes, dvec_specs, n_tiles):
    nr, nv, nc = len(rows), len(vecs), len(cts)

    def body(*refs):
        ins = [r[...].astype(F32) for r in refs[:nr + nv]]
        ct_vals = [r[...].astype(F32) for r in refs[nr + nv:nr + nv + nc]]
        out_refs = refs[nr + nv + nc:]
        _, vjp = jax.vjp(fn, *ins)
        grads = vjp(tuple(functools.reduce(lambda p, q: p + q, [ct_vals[j] for j in grp]) for grp in ct_groups))
        for o_ref, idx in zip(out_refs[:len(drow_idx)], drow_idx):
            o_ref[...] = grads[idx].astype(o_ref.dtype)
        step = pl.program_id(0)
        for o_ref, idx in zip(out_refs[len(drow_idx):], dvec_idx):
            @pl.when(step == 0)
            def _(o_ref=o_ref):
                o_ref[...] = jnp.zeros_like(o_ref)
            o_ref[...] += grads[nr + idx]

    return pl.pallas_call(body, name=name, grid=(n_tiles,),
                          in_specs=list(row_specs) + list(vec_specs) + list(ct_specs),
                          out_specs=list(drow_specs) + list(dvec_specs),
                          out_shape=list(drow_shapes) + list(dvec_shapes), compiler_params=_params(1))(*rows, *vecs, *cts)


def _rms(x):
    return x * lax.rsqrt(jnp.mean(x * x, axis=-1, keepdims=True) + EPS)


def _seg_in(x, shift, scale, gain):
    return _rms(x) * gain * (1.0 + scale) + shift, x


def _seg_qk(q, k, qg, kg):
    def norm(t, g, mult):
        blocks = []
        lane = lax.broadcasted_iota(jnp.int32, (1, LANES), 1)
        for p in range(t.shape[1] // LANES):
            tb = t[:, p * LANES:(p + 1) * LANES]
            sq = tb * tb
            lo = jnp.sum(jnp.where(lane < HEAD_DIM, sq, 0.0), axis=-1, keepdims=True)
            hi = jnp.sum(jnp.where(lane < HEAD_DIM, 0.0, sq), axis=-1, keepdims=True)
            ms = jnp.where(lane < HEAD_DIM, lo, hi) * (1.0 / HEAD_DIM)
            blocks.append(tb * lax.rsqrt(ms + EPS) * (g[:, p * LANES:(p + 1) * LANES] * mult))
        return jnp.concatenate(blocks, axis=-1) if len(blocks) > 1 else blocks[0]
    return norm(q, qg, 1.0 / math.sqrt(HEAD_DIM)), norm(k, kg, 1.0)


def _seg_gelu(ypre):
    return (jax.nn.gelu(ypre),)


def _seg_mix(y1, z, yattn, g_ssm, g_attn):
    ys = y1 * jax.nn.sigmoid(z)
    return (jnp.concatenate([_rms(ys) * g_ssm, _rms(yattn) * g_attn], axis=-1),)


def _seg_mid(x, o, gate_m, g_ffn, scale_f, shift_f):
    h1 = x + gate_m * o
    return h1, _rms(h1) * g_ffn * (1.0 + scale_f) + shift_f


def _seg_act(gate, up):
    return (jax.nn.silu(gate) * up,)


def _s5_lam(a_re, a_im, log_dt):
    dt = jnp.exp(log_dt)
    mag = jnp.exp(a_re * dt)
    lr, li = mag * jnp.cos(a_im * dt), mag * jnp.sin(a_im * dt)
    den = a_re * a_re + a_im * a_im
    nr, ni = lr - 1.0, li
    return lr, li, (nr * a_re + ni * a_im) / den, (ni * a_re - nr * a_im) / den


def _s5_bbar(coef_re, coef_im, b_re, b_im):
    return coef_re * b_re - coef_im * b_im, coef_re * b_im + coef_im * b_re


def _whole(name, fn, ins, out_shapes):
    n = len(ins)

    def body(*refs):
        outs = fn(*[r[...] for r in refs[:n]])
        for o_ref, o in zip(refs[n:], outs):
            o_ref[...] = o

    return pl.pallas_call(body, name=name, out_shape=[jax.ShapeDtypeStruct(s, F32) for s in out_shapes])(*ins)


def _whole_vjp(name, fn, ins, cts, out_shapes):
    n, nc = len(ins), len(cts)

    def body(*refs):
        _, vjp = jax.vjp(fn, *[r[...] for r in refs[:n]])
        grads = vjp(tuple(r[...] for r in refs[n:n + nc]))
        for o_ref, g in zip(refs[n + nc:], grads):
            o_ref[...] = g

    return pl.pallas_call(body, name=name, out_shape=[jax.ShapeDtypeStruct(s, F32) for s in out_shapes])(*ins, *cts)


SCAN_SHIFTS = (1, 2, 4)


def _cmul(ar, ai, br, bi):
    return ar * br - ai * bi, ar * bi + ai * br


def _scan_coefs(lr, li, reverse):
    s = lr.shape[1]
    row = lax.broadcasted_iota(jnp.int32, (SUBLANES, s), 0)
    p1 = (lr, li)
    p2 = _cmul(*p1, *p1)
    p4 = _cmul(*p2, *p2)
    p8 = _cmul(*p4, *p4)
    p3, p5, p6 = _cmul(*p1, *p2), _cmul(*p4, *p1), _cmul(*p4, *p2)
    p7 = _cmul(*p6, *p1)
    pows = (p1, p2, p3, p4, p5, p6, p7, p8)
    bc = lambda t: jnp.broadcast_to(t, (SUBLANES, s))
    steps = []
    for sh, pw in zip(SCAN_SHIFTS, (p1, p2, p4)):
        keep = (row + sh <= SUBLANES - 1) if reverse else (row >= sh)
        steps.append((jnp.where(keep, bc(pw[0]), 0.0), jnp.where(keep, bc(pw[1]), 0.0)))
    cr, ci = jnp.zeros((SUBLANES, s), F32), jnp.zeros((SUBLANES, s), F32)
    for r in range(SUBLANES):
        pw = pows[SUBLANES - 1 - r] if reverse else pows[r]
        cr = jnp.where(row == r, bc(pw[0]), cr)
        ci = jnp.where(row == r, bc(pw[1]), ci)
    return steps, (cr, ci)


def _scan_tile(xr, xi, steps, carry_pow, cr, ci, reverse):
    for sh, (ar, ai) in zip(SCAN_SHIFTS, steps):
        rs = SUBLANES - sh if reverse else sh
        sr, si = pltpu.roll(xr, rs, 0), pltpu.roll(xi, rs, 0)
        xr, xi = xr + ar * sr - ai * si, xi + ar * si + ai * sr
    pr, pi = carry_pow
    return xr + pr * cr - pi * ci, xi + pr * ci + pi * cr


def _s5_forward(proj, b_blk_re, b_blk_im, c_blk_re, c_blk_im, lam_re, lam_im, d_skip, n_blk, t_chunk):
    seq = proj.shape[0]
    n_chunks = seq // t_chunk
    n_tiles = t_chunk // SUBLANES
    s = b_blk_re.shape[2]

    def body(u_ref, bre_ref, bim_ref, cre_ref, cim_ref, lr_ref, li_ref, d_ref, y_ref, xr_ref, xi_ref, wr, wi, carry):
        t = pl.program_id(1)

        @pl.when(t == 0)
        def _():
            carry[...] = jnp.zeros_like(carry)

        u = u_ref[...]
        ub = u.astype(BF16)
        wr[...] = _dot(ub, bre_ref[...])
        wi[...] = _dot(ub, bim_ref[...])
        steps, cpow = _scan_coefs(lr_ref[...], li_ref[...], False)

        def tile(i, c):
            r0 = pl.multiple_of(i * SUBLANES, SUBLANES)
            xr, xi = _scan_tile(wr[pl.ds(r0, SUBLANES), :], wi[pl.ds(r0, SUBLANES), :], steps, cpow, c[0], c[1], False)
            xr_ref[pl.ds(r0, SUBLANES), :] = xr
            xi_ref[pl.ds(r0, SUBLANES), :] = xi
            last = SUBLANES - 1
            return (jnp.broadcast_to(xr[last:, :], xr.shape), jnp.broadcast_to(xi[last:, :], xi.shape))

        cr, ci = lax.fori_loop(0, n_tiles, tile, (carry[0], carry[1]))
        carry[0] = cr
        carry[1] = ci
        y = _dot(xr_ref[...].astype(BF16), cre_ref[...]) - _dot(xi_ref[...].astype(BF16), cim_ref[...])
        y_ref[...] = y + d_ref[...] * u

    blk = lambda shape: pl.BlockSpec((None,) + shape, lambda j, t: (j, 0, 0))
    return pl.pallas_call(
        body, name="s5_fwd", grid=(n_blk, n_chunks),
        in_specs=[pl.BlockSpec((t_chunk, LANES), lambda j, t: (t, j)), blk((LANES, s)), blk((LANES, s)),
                  blk((s, LANES)), blk((s, LANES)), blk((1, s)), blk((1, s)), pl.BlockSpec((1, LANES), lambda j, t: (0, j))],
        out_specs=[pl.BlockSpec((t_chunk, LANES), lambda j, t: (t, j)), pl.BlockSpec((t_chunk, s), lambda j, t: (t, j)),
                   pl.BlockSpec((t_chunk, s), lambda j, t: (t, j))],
        out_shape=[jax.ShapeDtypeStruct((seq, n_blk * LANES), F32), jax.ShapeDtypeStruct((seq, n_blk * s), F32),
                   jax.ShapeDtypeStruct((seq, n_blk * s), F32)],
        scratch_shapes=[pltpu.VMEM((t_chunk, s), F32), pltpu.VMEM((t_chunk, s), F32), pltpu.VMEM((2, SUBLANES, s), F32)],
        compiler_params=_params(2),
    )(proj, b_blk_re, b_blk_im, c_blk_re, c_blk_im, lam_re, lam_im, d_skip)


def _s5_backward(dypre, proj, x_re, x_im, b_blk_re, b_blk_im, c_blk_re, c_blk_im, lam_re, lam_im, d_skip, n_blk, t_chunk):
    seq = proj.shape[0]
    n_chunks = seq // t_chunk
    n_tiles = t_chunk // SUBLANES
    s = b_blk_re.shape[2]

    def body(dy_ref, u_ref, xr_ref, xi_ref, pr_ref, pi_ref, bre_ref, bim_ref, cre_ref, cim_ref, lr_ref, li_ref, d_ref,
             du_ref, dbre_ref, dbim_ref, dcre_ref, dcim_ref, dlr_ref, dli_ref, dd_ref, gr, gi, carry):
        t = pl.program_id(1)

        @pl.when(t == 0)
        def _():
            carry[...] = jnp.zeros_like(carry)
            for r in (dbre_ref, dbim_ref, dcre_ref, dcim_ref, dlr_ref, dli_ref, dd_ref):
                r[...] = jnp.zeros_like(r)

        dy = dy_ref[...]
        dyb = dy.astype(BF16)
        u = u_ref[...]
        gr[...] = _dot(dyb, cre_ref[...], NT)
        gi[...] = -_dot(dyb, cim_ref[...], NT)
        steps, cpow = _scan_coefs(lr_ref[...], -li_ref[...], True)
        row = lax.broadcasted_iota(jnp.int32, (SUBLANES, s), 0)
        last = SUBLANES - 1
        first_chunk = t == n_chunks - 1

        def tile_at(r0, prev_r, prev_i, c):
            cr, ci, ar, ai = c
            lr_, li_ = _scan_tile(gr[pl.ds(r0, SUBLANES), :], gi[pl.ds(r0, SUBLANES), :], steps, cpow, cr, ci, True)
            gr[pl.ds(r0, SUBLANES), :] = lr_
            gi[pl.ds(r0, SUBLANES), :] = li_
            xr, xi = xr_ref[pl.ds(r0, SUBLANES), :], xi_ref[pl.ds(r0, SUBLANES), :]
            xpr = jnp.where(row == 0, jnp.broadcast_to(prev_r[last:, :], xr.shape), pltpu.roll(xr, 1, 0))
            xpi = jnp.where(row == 0, jnp.broadcast_to(prev_i[last:, :], xi.shape), pltpu.roll(xi, 1, 0))
            ar = ar + lr_ * xpr + li_ * xpi
            ai = ai + li_ * xpr - lr_ * xpi
            return (jnp.broadcast_to(lr_[:1, :], lr_.shape), jnp.broadcast_to(li_[:1, :], li_.shape), ar, ai)

        def tile(ii, c):
            i = n_tiles - 1 - ii
            r0 = pl.multiple_of(i * SUBLANES, SUBLANES)
            rp = pl.multiple_of(r0 - SUBLANES, SUBLANES)
            return tile_at(r0, xr_ref[pl.ds(rp, SUBLANES), :], xi_ref[pl.ds(rp, SUBLANES), :], c)

        zero = jnp.zeros((SUBLANES, s), F32)
        c = lax.fori_loop(0, n_tiles - 1, tile, (carry[0], carry[1], zero, zero))
        keep = jnp.where(first_chunk, 0.0, 1.0)
        c = tile_at(0, pr_ref[...] * keep, pi_ref[...] * keep, c)
        carry[0] = c[0]
        carry[1] = c[1]
        dlr_ref[...] += jnp.sum(c[2], axis=0, keepdims=True)
        dli_ref[...] += jnp.sum(c[3], axis=0, keepdims=True)

        lam_r, lam_i = gr[...].astype(BF16), gi[...].astype(BF16)
        du_ref[...] = _dot(lam_r, bre_ref[...], NT) + _dot(lam_i, bim_ref[...], NT) + d_ref[...] * dy
        ub = u.astype(BF16)
        dbre_ref[...] += _dot(ub, lam_r, TN)
        dbim_ref[...] += _dot(ub, lam_i, TN)
        dcre_ref[...] += _dot(xr_ref[...].astype(BF16), dyb, TN)
        dcim_ref[...] -= _dot(xi_ref[...].astype(BF16), dyb, TN)
        dd_ref[...] += jnp.sum(dy * u, axis=0, keepdims=True)

    rev = lambda t: n_chunks - 1 - t
    blk = lambda shape: pl.BlockSpec((None,) + shape, lambda j, t: (j, 0, 0))
    tpc = t_chunk // SUBLANES
    prev_spec = pl.BlockSpec((SUBLANES, s), lambda j, t: (jnp.maximum(rev(t) * tpc - 1, 0), j))
    chunk = lambda w: pl.BlockSpec((t_chunk, w), lambda j, t: (rev(t), j))
    return pl.pallas_call(
        body, name="s5_bwd", grid=(n_blk, n_chunks),
        in_specs=[chunk(LANES), chunk(LANES), chunk(s), chunk(s), prev_spec, prev_spec, blk((LANES, s)), blk((LANES, s)),
                  blk((s, LANES)), blk((s, LANES)), blk((1, s)), blk((1, s)), pl.BlockSpec((1, LANES), lambda j, t: (0, j))],
        out_specs=[chunk(LANES), blk((LANES, s)), blk((LANES, s)), blk((s, LANES)), blk((s, LANES)), blk((1, s)), blk((1, s)),
                   pl.BlockSpec((1, LANES), lambda j, t: (0, j))],
        out_shape=[jax.ShapeDtypeStruct((seq, n_blk * LANES), F32),
                   jax.ShapeDtypeStruct((n_blk, LANES, s), F32), jax.ShapeDtypeStruct((n_blk, LANES, s), F32),
                   jax.ShapeDtypeStruct((n_blk, s, LANES), F32), jax.ShapeDtypeStruct((n_blk, s, LANES), F32),
                   jax.ShapeDtypeStruct((n_blk, 1, s), F32), jax.ShapeDtypeStruct((n_blk, 1, s), F32),
                   jax.ShapeDtypeStruct((1, n_blk * LANES), F32)],
        scratch_shapes=[pltpu.VMEM((t_chunk, s), F32), pltpu.VMEM((t_chunk, s), F32), pltpu.VMEM((2, SUBLANES, s), F32)],
        compiler_params=_params(2),
    )(dypre, proj, x_re, x_im, x_re, x_im, b_blk_re, b_blk_im, c_blk_re, c_blk_im, lam_re, lam_im, d_skip)


TQ, TK = 256, 128


def _split_bf16(x):
    hi = x.astype(BF16)
    return hi, (x - hi.astype(F32)).astype(BF16)


def _sb_weights(z, past, carry, tri):
    ls = jnp.minimum(z, 0.0) - jnp.log(1.0 + jnp.exp(-jnp.abs(z)))
    lk = ls - z
    if past is not None:
        lk = jnp.where(past, lk, 0.0)
    hi, lo = _split_bf16(lk)
    w = jnp.exp(ls + (_dot(hi, tri) + _dot(lo, tri)) + carry)
    if past is not None:
        w = jnp.where(past, w, 0.0)
    return ls, lk, w


def _walk_key_blocks(i, ratio, prologue, block, epilogue):
    n_kb = (i + 1) * ratio
    prologue(n_kb - 1)
    for n in range(ratio):
        block(n_kb - 1 - n, n % 2, True)

    def pair(t, carry):
        j = n_kb - 1 - ratio - 2 * t
        block(j, ratio % 2, False)
        block(j - 1, (ratio + 1) % 2, False)
        return carry

    lax.fori_loop(0, (i * ratio) // 2, pair, 0)
    epilogue()


def _attention_forward(qh, kh, proj, v_col, n_pair, tq, tk):
    seq = qh.shape[0]
    ratio = tq // tk
    assert ratio % 2 == 0

    def body(q_ref, k_ref, v_ref, o_ref, q_scr, z_scr, w_scr, acc_scr, c_scr):
        i = pl.program_id(1)
        lane = lax.broadcasted_iota(jnp.int32, (1, LANES), 1)
        q2 = q_ref[...]
        q_scr[0] = jnp.where(lane < HEAD_DIM, q2, 0.0).astype(BF16)
        q_scr[1] = jnp.where(lane < HEAD_DIM, 0.0, q2).astype(BF16)
        tri = (lax.broadcasted_iota(jnp.int32, (tk, tk), 0) > lax.broadcasted_iota(jnp.int32, (tk, tk), 1)).astype(BF16)
        qpos = i * tq + lax.broadcasted_iota(jnp.int32, (tq, tk), 0)
        kidx = lax.broadcasted_iota(jnp.int32, (tq, tk), 1)

        def rows(ref, j):
            j = jnp.clip(j, 0, seq // tk - 1)
            return ref[pl.ds(pl.multiple_of(j * tk, tk), tk), :].astype(BF16)

        def scores(j, slot):
            kb = rows(k_ref, j)
            for h in range(2):
                z_scr[slot, h] = _dot(q_scr[h], kb, NT)

        def finish(j):
            vb = rows(v_ref, j)
            for h in range(2):
                acc_scr[h] += _dot(w_scr[h], vb)

        def prologue(j):
            w_scr[...] = jnp.zeros_like(w_scr)
            acc_scr[...] = jnp.zeros_like(acc_scr)
            c_scr[...] = jnp.zeros_like(c_scr)
            scores(j, 0)

        def block(j, slot, masked):
            scores(j - 1, 1 - slot)
            finish(j + 1)
            past = ((kidx + j * tk) < qpos) if masked else None
            for h in range(2):
                _, lk, w = _sb_weights(z_scr[slot, h], past, c_scr[h], tri)
                w_scr[h] = w.astype(BF16)
                c_scr[h] += jnp.sum(lk, axis=-1, keepdims=True)

        _walk_key_blocks(i, ratio, prologue, block, lambda: finish(0))
        o_ref[...] = jnp.where(lane < HEAD_DIM, acc_scr[0], acc_scr[1])

    return pl.pallas_call(
        body, name="attn_fwd", grid=(n_pair, seq // tq),
        in_specs=[pl.BlockSpec((tq, LANES), lambda p, i: (i, p)), pl.BlockSpec((seq, LANES), lambda p, i: (0, p)),
                  pl.BlockSpec((seq, LANES), lambda p, i: (0, v_col + p))],
        out_specs=pl.BlockSpec((tq, LANES), lambda p, i: (i, p)),
        out_shape=jax.ShapeDtypeStruct(qh.shape, F32),
        scratch_shapes=[pltpu.VMEM((2, tq, LANES), BF16), pltpu.VMEM((2, 2, tq, tk), F32), pltpu.VMEM((2, tq, tk), BF16),
                        pltpu.VMEM((2, tq, LANES), F32), pltpu.VMEM((2, tq, 1), F32)],
        compiler_params=_params(2),
    )(qh, kh, proj)


def _attention_backward(qh, kh, proj, v_col, y, dy, n_pair, tq, tk):
    seq = qh.shape[0]

    ratio = tq // tk
    assert ratio % 2 == 0

    def body(q_ref, k_ref, v_ref, y_ref, dy_ref, dq_ref, dk_ref, dv_ref,
             q_scr, do_scr, z_scr, dw_scr, w_scr, dz_scr, dq_scr, c_scr, c2_scr, tot_scr):
        i = pl.program_id(1)

        @pl.when(i == 0)
        def _():
            dk_ref[...] = jnp.zeros_like(dk_ref)
            dv_ref[...] = jnp.zeros_like(dv_ref)

        lane = lax.broadcasted_iota(jnp.int32, (1, LANES), 1)
        sel = (lane < HEAD_DIM, lane >= HEAD_DIM)
        q2, do2 = q_ref[...], dy_ref[...].astype(BF16)
        dot_oy = do2.astype(F32) * y_ref[...]
        for h in range(2):
            q_scr[h] = jnp.where(sel[h], q2, 0.0).astype(BF16)
            do_scr[h] = jnp.where(sel[h], do2, jnp.zeros_like(do2))
            tot_scr[h] = jnp.sum(jnp.where(sel[h], dot_oy, 0.0), axis=-1, keepdims=True)
        r_i, c_i = lax.broadcasted_iota(jnp.int32, (tk, tk), 0), lax.broadcasted_iota(jnp.int32, (tk, tk), 1)
        tri = (r_i > c_i).astype(BF16)
        tri_ge = (r_i >= c_i).astype(BF16)
        qpos = i * tq + lax.broadcasted_iota(jnp.int32, (tq, tk), 0)
        kidx = lax.broadcasted_iota(jnp.int32, (tq, tk), 1)

        def start(j):
            return pl.multiple_of(jnp.clip(j, 0, seq // tk - 1) * tk, tk)

        def scores(j, slot):
            c0 = start(j)
            kb, vb = k_ref[pl.ds(c0, tk), :].astype(BF16), v_ref[pl.ds(c0, tk), :].astype(BF16)
            for h in range(2):
                z_scr[slot, h] = _dot(q_scr[h], kb, NT)
                dw_scr[slot, h] = _dot(do_scr[h], vb, NT)

        def finish(j):
            c0 = start(j)
            kb = k_ref[pl.ds(c0, tk), :].astype(BF16)
            dk_add, dv_add = jnp.zeros((tk, LANES), F32), jnp.zeros((tk, LANES), F32)
            for h in range(2):
                dz = dz_scr[h]
                dq_scr[h] += _dot(dz, kb)
                dk_add = dk_add + _dot(dz, q_scr[h], TN)
                dv_add = dv_add + _dot(w_scr[h], do_scr[h], TN)
            dk_ref[pl.ds(c0, tk), :] += dk_add
            dv_ref[pl.ds(c0, tk), :] += dv_add

        def prologue(j):
            for r in (w_scr, dz_scr, dq_scr, c_scr, c2_scr):
                r[...] = jnp.zeros_like(r)
            scores(j, 0)

        def block(j, slot, masked):
            scores(j - 1, 1 - slot)
            finish(j + 1)
            past = ((kidx + j * tk) < qpos) if masked else None
            for h in range(2):
                ls, lk, w = _sb_weights(z_scr[slot, h], past, c_scr[h], tri)
                wb = w.astype(BF16)
                dlw = dw_scr[slot, h] * wb.astype(F32)
                hi, lo = _split_bf16(dlw)
                dlk = tot_scr[h] - c2_scr[h] - (_dot(hi, tri_ge) + _dot(lo, tri_ge))
                if masked:
                    dlk = jnp.where(past, dlk, 0.0)
                sig = jnp.exp(ls)
                w_scr[h] = wb
                dz_scr[h] = (dlw * (1.0 - sig) - dlk * sig).astype(BF16)
                c_scr[h] += jnp.sum(lk, axis=-1, keepdims=True)
                c2_scr[h] += jnp.sum(dlw, axis=-1, keepdims=True)

        _walk_key_blocks(i, ratio, prologue, block, lambda: finish(0))
        dq_ref[...] = jnp.where(sel[0], dq_scr[0], dq_scr[1])

    blk = pl.BlockSpec((tq, LANES), lambda p, i: (i, p))
    full = pl.BlockSpec((seq, LANES), lambda p, i: (0, p))
    shape = jax.ShapeDtypeStruct(qh.shape, F32)
    return pl.pallas_call(
        body, name="attn_bwd", grid=(n_pair, seq // tq),
        in_specs=[blk, full, pl.BlockSpec((seq, LANES), lambda p, i: (0, v_col + p)), blk, blk],
        out_specs=[blk, full, full], out_shape=[shape, shape, shape],
        scratch_shapes=[pltpu.VMEM((2, tq, LANES), BF16), pltpu.VMEM((2, tq, LANES), BF16),
                        pltpu.VMEM((2, 2, tq, tk), F32), pltpu.VMEM((2, 2, tq, tk), F32),
                        pltpu.VMEM((2, tq, tk), BF16), pltpu.VMEM((2, tq, tk), BF16), pltpu.VMEM((2, tq, LANES), F32),
                        pltpu.VMEM((2, tq, 1), F32), pltpu.VMEM((2, tq, 1), F32), pltpu.VMEM((2, tq, 1), F32)],
        compiler_params=_params(2),
    )(qh, kh, proj, y, dy)


def _loss_head(h1, ffn, target, gate_f, tile):
    seq, d = h1.shape

    def body(h_ref, f_ref, t_ref, g_ref, dy_ref, df_ref, dg_ref, loss_ref):
        @pl.when(pl.program_id(0) == 0)
        def _():
            dg_ref[...] = jnp.zeros_like(dg_ref)
            loss_ref[...] = jnp.zeros_like(loss_ref)

        f, g = f_ref[...], g_ref[...]
        err = h_ref[...] + g * f - t_ref[...]
        dy = err * (1.0 / d)
        dy_ref[...] = dy
        df_ref[...] = (dy * g).astype(df_ref.dtype)
        dg_ref[...] += jnp.sum(dy * f, axis=0, keepdims=True)
        loss_ref[...] += jnp.sum(jnp.sum(err * err, axis=-1, keepdims=True), axis=0, keepdims=True) * (0.5 / d)

    row = _row_spec(tile, d)
    return pl.pallas_call(
        body, name="loss_head", grid=(seq // tile,), in_specs=[row, row, row, _vec_spec(d)],
        out_specs=[row, row, _vec_spec(d), pl.BlockSpec((1, 1), lambda i: (0, 0))],
        out_shape=[jax.ShapeDtypeStruct((seq, d), F32), jax.ShapeDtypeStruct((seq, d), BF16),
                   jax.ShapeDtypeStruct((1, d), F32), jax.ShapeDtypeStruct((1, 1), F32)],
        compiler_params=_params(1),
    )(h1, ffn, target, gate_f)


def _dot3(a, b, dn):
    ah, al = _split_bf16(a)
    bh, bl = _split_bf16(b)
    return _dot(ah, bh, dn) + (_dot(ah, bl, dn) + _dot(al, bh, dn))


def _ada_forward(c_all, w_shard, b_cols):
    d, n = w_shard.shape
    bk = _tile(d, 512)

    def body(c_ref, w_ref, b_ref, o_ref):
        @pl.when(pl.program_id(0) == 0)
        def _():
            o_ref[...] = jnp.broadcast_to(b_ref[...], o_ref.shape)

        o_ref[...] += _dot3(jax.nn.silu(c_ref[...]), w_ref[...], NN)

    return pl.pallas_call(
        body, name="ada_fwd", grid=(d // bk,),
        in_specs=[pl.BlockSpec((NDEV, bk), lambda k: (0, k)), pl.BlockSpec((bk, n), lambda k: (k, 0)), _vec_spec(n)],
        out_specs=pl.BlockSpec((NDEV, n), lambda k: (0, 0)), out_shape=jax.ShapeDtypeStruct((NDEV, n), F32),
        compiler_params=_params(1),
    )(c_all, w_shard, b_cols)


def _adam(w, g, m, v):
    m = ADAM_B1 * m + (1.0 - ADAM_B1) * g
    v = ADAM_B2 * v + (1.0 - ADAM_B2) * (g * g)
    m_hat = m / (1.0 - ADAM_B1 ** ADAM_STEP)
    v_hat = v / (1.0 - ADAM_B2 ** ADAM_STEP)
    return -ADAM_LR * (m_hat / (jnp.sqrt(v_hat) + ADAM_EPS) + ADAM_WD * w), m, v


def _adam_ada(c_all, dmod_cols, w, m, v):
    d, n = w.shape
    tr = _tile(d, 256)

    def body(c_ref, dm_ref, w_ref, m_ref, v_ref, g_ref, dl_ref, nm_ref, nv_ref):
        g = _dot3(jax.nn.silu(c_ref[...]), dm_ref[...], TN)
        delta, nm, nv = _adam(w_ref[...], g, m_ref[...], v_ref[...])
        g_ref[...] = g
        dl_ref[...] = delta
        nm_ref[...] = nm
        nv_ref[...] = nv

    row = _row_spec(tr, n)
    return pl.pallas_call(
        body, name="adam_ada", grid=(d // tr,),
        in_specs=[pl.BlockSpec((NDEV, tr), lambda i: (0, i)), pl.BlockSpec((NDEV, n), lambda i: (0, 0)), row, row, row],
        out_specs=[row] * 4, out_shape=[jax.ShapeDtypeStruct((d, n), F32)] * 4, compiler_params=_params(1),
    )(c_all, dmod_cols, w, m, v)


def _adam_sum(name, parts, part_spec, w, m, v, tr):
    r, c = w.shape

    def body(p_ref, w_ref, m_ref, v_ref, g_ref, dl_ref, nm_ref, nv_ref):
        g = p_ref[0].astype(F32)
        for k in range(1, NDEV):
            g = g + p_ref[k].astype(F32)
        delta, nm, nv = _adam(w_ref[...], g, m_ref[...], v_ref[...])
        g_ref[...] = g
        dl_ref[...] = delta
        nm_ref[...] = nm
        nv_ref[...] = nv

    row = _row_spec(tr, c)
    return pl.pallas_call(
        body, name=name, grid=(r // tr,), in_specs=[part_spec, row, row, row],
        out_specs=[row] * 4, out_shape=[jax.ShapeDtypeStruct((r, c), F32)] * 4, compiler_params=_params(1),
    )(parts, w, m, v)


GROUPS_PER_BLOCK = LANES // SSM_GROUP


def _to_b_blocks(bb, n_blk, p):
    t = bb.reshape(n_blk, GROUPS_PER_BLOCK, p, SSM_GROUP)
    eye = jnp.eye(GROUPS_PER_BLOCK, dtype=bb.dtype)
    return jnp.einsum("jgph,gk->jghkp", t, eye).reshape(n_blk, LANES, GROUPS_PER_BLOCK * p)


def _from_b_blocks(blk, n_blk, p):
    t = blk.reshape(n_blk, GROUPS_PER_BLOCK, SSM_GROUP, GROUPS_PER_BLOCK, p)
    eye = jnp.eye(GROUPS_PER_BLOCK, dtype=blk.dtype)
    return jnp.einsum("jghkp,gk->jgph", t, eye).reshape(n_blk * GROUPS_PER_BLOCK, p, SSM_GROUP)


def _to_c_blocks(cc, n_blk, p):
    t = cc.reshape(n_blk, GROUPS_PER_BLOCK, SSM_GROUP, p)
    eye = jnp.eye(GROUPS_PER_BLOCK, dtype=cc.dtype)
    return jnp.einsum("jghp,gk->jgpkh", t, eye).reshape(n_blk, GROUPS_PER_BLOCK * p, LANES)


def _from_c_blocks(blk, n_blk, p):
    t = blk.reshape(n_blk, GROUPS_PER_BLOCK, p, GROUPS_PER_BLOCK, SSM_GROUP)
    eye = jnp.eye(GROUPS_PER_BLOCK, dtype=blk.dtype)
    return jnp.einsum("jgpkh,gk->jghp", t, eye).reshape(n_blk * GROUPS_PER_BLOCK, SSM_GROUP, p)


SMALL = ("b_ada", "g_mix", "a_re", "a_im", "log_dt", "b_re", "b_im", "c_re", "c_im", "d_skip",
         "q_gain", "k_gain", "g_ssm_out", "g_attn_out", "g_ffn")
PACK_COLS = 1024


def _pack(arrs):
    flat = jnp.concatenate([a.reshape(-1) for a in arrs])
    n = flat.shape[0]
    quantum = SUBLANES * PACK_COLS
    padded = -(-n // quantum) * quantum
    return jnp.pad(flat, (0, padded - n)).reshape(padded // PACK_COLS, PACK_COLS)


def _unpack(packed, like):
    flat, out, off = packed.reshape(-1), [], 0
    for a in like:
        out.append(flat[off:off + a.size].reshape(a.shape))
        off += a.size
    return out


def kernel(x, c, w_ada, b_ada, g_mix, w_in, a_re, a_im, log_dt, b_re, b_im, c_re, c_im, d_skip, w_glu, q_gain, k_gain, g_ssm_out, g_attn_out, w_out, g_ffn, w_gate, w_up, w_down, loss_target, m_w_ada, m_b_ada, m_g_mix, m_w_in, m_a_re, m_a_im, m_log_dt, m_b_re, m_b_im, m_c_re, m_c_im, m_d_skip, m_w_glu, m_q_gain, m_k_gain, m_g_ssm_out, m_g_attn_out, m_w_out, m_g_ffn, m_w_gate, m_w_up, m_w_down, v_w_ada, v_b_ada, v_g_mix, v_w_in, v_a_re, v_a_im, v_log_dt, v_b_re, v_b_im, v_c_re, v_c_im, v_d_skip, v_w_glu, v_q_gain, v_k_gain, v_g_ssm_out, v_g_attn_out, v_w_out, v_g_ffn, v_w_gate, v_w_up, v_w_down):
    given = dict(locals())
    seq, d = x.shape[1], x.shape[2]
    xs, tgt = x[0], loss_target[0]
    n_groups, p_state = a_re.shape[1], a_re.shape[2]
    w_ssm = n_groups * SSM_GROUP
    w_attn = w_in.shape[2] * NDEV - w_ssm
    w_attn //= 3
    n_blk, n_pair = w_ssm // LANES, w_attn // LANES
    n_heads = w_attn // HEAD_DIM
    ns_in, ns_ff = w_in.shape[2], w_gate.shape[2]
    d_mix = w_ssm + w_attn
    mx, my, mc = _me()
    me = 4 * mx + 2 * my + mc
    rt = _tile(seq, 256)
    n_rt = seq // rt
    sds = jax.ShapeDtypeStruct

    order_tok = [jnp.zeros((), F32)]

    def start(x, gather, name):
        handle, token = _exchange_start(x, gather, name)
        order_tok[0] = order_tok[0] + token[0, 0]
        return handle

    def behind(v):
        return v + order_tok[0].astype(v.dtype)

    c_all = _exchange(c, True, "comm_ag_c")[0].reshape(NDEV, d)
    n_ada = w_ada.shape[2]
    b_cols = lax.dynamic_slice(b_ada, (0, me * n_ada), (1, n_ada))
    mod_cols = _ada_forward(c_all, w_ada[0], b_cols)
    mod_all, mod_token = _exchange(mod_cols, True, "comm_ag_mod")
    mod = lax.dynamic_slice(mod_all, (0, me, 0), (NDEV, 1, n_ada)).reshape(1, NDEV * n_ada)
    order_tok[0] = mod_token[0, 0]
    h_w_in = start(behind(w_in[0]).astype(BF16), True, "comm_ag_w_in")
    h_w_glu = start(behind(w_glu[0]).astype(BF16), True, "comm_ag_w_glu")
    h_w_out = start(behind(w_out[0]).astype(BF16), True, "comm_ag_w_out")
    h_w_gu = start(behind(jnp.stack([w_gate[0], w_up[0]])).astype(BF16), True, "comm_ag_w_gu")
    h_w_down = start(behind(w_down[0]).astype(BF16), True, "comm_ag_w_down")
    mod = behind(mod)
    shift_m, scale_m, gate_m, shift_f, scale_f, gate_f = [mod[:, i * d:(i + 1) * d] for i in range(6)]

    gp = n_groups * p_state
    a_re2, a_im2, ldt2 = a_re[0], a_im[0], log_dt[0].reshape(n_groups, 1)
    b_re2, b_im2 = b_re[0].reshape(gp, SSM_GROUP), b_im[0].reshape(gp, SSM_GROUP)
    lam_r, lam_i, coef_r, coef_i = _whole("s5_lam", _s5_lam, [a_re2, a_im2, ldt2], [(n_groups, p_state)] * 4)
    coef_r2, coef_i2 = coef_r.reshape(gp, 1), coef_i.reshape(gp, 1)
    bb_r, bb_i = _whole("s5_bbar", _s5_bbar, [coef_r2, coef_i2, b_re2, b_im2], [(gp, SSM_GROUP)] * 2)
    s_blk = GROUPS_PER_BLOCK * p_state
    b_blk_r = _to_b_blocks(bb_r.reshape(n_groups, p_state, SSM_GROUP), n_blk, p_state).astype(BF16)
    b_blk_i = _to_b_blocks(bb_i.reshape(n_groups, p_state, SSM_GROUP), n_blk, p_state).astype(BF16)
    c_blk_r = _to_c_blocks(c_re[0], n_blk, p_state).astype(BF16)
    c_blk_i = _to_c_blocks(c_im[0], n_blk, p_state).astype(BF16)
    lam_r3, lam_i3 = lam_r.reshape(n_blk, 1, s_blk), lam_i.reshape(n_blk, 1, s_blk)
    d_skip2 = d_skip[0].reshape(1, w_ssm)

    row_d, vec_d = _row_spec(rt, d), _vec_spec(d)
    xm, = _rowwise_fwd("seg_in", lambda *a: _seg_in(*a)[:1], [xs], [row_d], [shift_m, scale_m, g_mix], [vec_d] * 3,
                       [sds((seq, d), BF16)], [row_d], n_rt)
    bn_in = _tile(ns_in, 512)
    per = ns_in // bn_in
    bm, bk = _tile(seq, BM), _tile(d, BK)
    w_in_g = _exchange_finish(h_w_in, xm)
    proj = _mm("mm_in", xm, w_in_g, NN, (seq // bm, NDEV * per, d // bk),
               pl.BlockSpec((bm, bk), lambda i, j, k: (i, k)),
               pl.BlockSpec((None, bk, bn_in), lambda i, j, k: (j // per, k, j % per)),
               pl.BlockSpec((bm, bn_in), lambda i, j, k: (i, j)), (seq, NDEV * ns_in), F32, (bm, bn_in))
    q_col, k_col, v_col = w_ssm // w_attn, w_ssm // w_attn + 1, (w_ssm + 2 * w_attn) // LANES
    qg_t, kg_t = jnp.tile(q_gain, (1, n_heads)), jnp.tile(k_gain, (1, n_heads))
    row_a, vec_a = _row_spec(rt, w_attn), _vec_spec(w_attn)
    qk_rows, qk_specs = [proj, proj], [_row_spec(rt, w_attn, q_col), _row_spec(rt, w_attn, k_col)]
    qh, kh = _rowwise_fwd("seg_qk", _seg_qk, qk_rows, qk_specs, [qg_t, kg_t], [vec_a] * 2,
                          [sds((seq, w_attn), F32)] * 2, [row_a] * 2, n_rt)
    t_chunk = _tile(seq, 256)
    ypre, x_re, x_im = _s5_forward(proj, b_blk_r, b_blk_i, c_blk_r, c_blk_i, lam_r3, lam_i3, d_skip2, n_blk, t_chunk)
    tq, tk = _tile(seq, TQ), _tile(seq, TK)
    y_attn = _attention_forward(qh, kh, proj, v_col, n_pair, tq, tk)
    row_s, vec_s = _row_spec(rt, w_ssm), _vec_spec(w_ssm)
    y1, = _rowwise_fwd("seg_gelu", _seg_gelu, [ypre], [row_s], [], [], [sds((seq, w_ssm), F32)], [row_s], n_rt)
    w_glu_g = _exchange_finish(h_w_glu, y1).reshape(w_ssm, w_ssm)
    z = _mm_plain("mm_glu", y1, w_glu_g, NN, F32)
    row_m = _row_spec(rt, d_mix)
    mixed, = _rowwise_fwd("seg_mix", _seg_mix, [y1, z, y_attn], [row_s, row_s, row_a], [g_ssm_out, g_attn_out], [vec_s, vec_a],
                          [sds((seq, d_mix), BF16)], [row_m], n_rt)
    w_out_g = _exchange_finish(h_w_out, mixed).reshape(d_mix, d)
    o = _mm_plain("mm_out", mixed, w_out_g, NN, F32)
    h1, xf = _rowwise_fwd("seg_mid", _seg_mid, [xs, o], [row_d] * 2, [gate_m, g_ffn, scale_f, shift_f], [vec_d] * 4,
                          [sds((seq, d), F32), sds((seq, d), BF16)], [row_d] * 2, n_rt)
    w_gu_g = _exchange_finish(h_w_gu, xf).reshape(2 * NDEV, d, ns_ff)
    gu = _mm("mm_gu", xf, w_gu_g, NN, (seq // bm, 2 * NDEV, d // bk),
             pl.BlockSpec((bm, bk), lambda i, j, k: (i, k)), pl.BlockSpec((None, bk, ns_ff), lambda i, j, k: (j, k, 0)),
             pl.BlockSpec((None, bm, ns_ff), lambda i, j, k: (j, i, 0)), (2 * NDEV, seq, ns_ff), F32, (bm, ns_ff))
    gu4 = gu.reshape(NDEV, 2, seq, ns_ff)
    ft = _tile(seq, 512)
    pair_spec = pl.BlockSpec((None, 2, ft, ns_ff), lambda s, i: (s, 0, i, 0))
    one_spec = pl.BlockSpec((None, ft, ns_ff), lambda s, i: (s, i, 0))

    def act_body(gu_ref, a_ref):
        a_ref[...] = _seg_act(gu_ref[0], gu_ref[1])[0].astype(a_ref.dtype)

    act = pl.pallas_call(act_body, name="seg_act", grid=(NDEV, seq // ft), in_specs=[pair_spec], out_specs=one_spec,
                         out_shape=sds((NDEV, seq, ns_ff), BF16), compiler_params=_params(2))(gu4)
    bn_d = _tile(d, BN)
    w_down_g = _exchange_finish(h_w_down, act)
    ffn = _mm("mm_down", act, w_down_g, NN, (seq // bm, d // bn_d, NDEV),
              pl.BlockSpec((None, bm, ns_ff), lambda i, j, k: (k, i, 0)), pl.BlockSpec((None, ns_ff, bn_d), lambda i, j, k: (k, 0, j)),
              pl.BlockSpec((bm, bn_d), lambda i, j, k: (i, j)), (seq, d), F32, (bm, bn_d))
    dy, dffn, d_gate_f, loss_part = _loss_head(h1, ffn, tgt, gate_f, rt)
    loss = lax.psum(loss_part[0, 0], MESH_AXES)

    tok_tile = lambda: jnp.broadcast_to(order_tok[0], (SUBLANES, LANES))
    bl = _tile(seq, BK)
    gw_down = _mm("mm_dw_down", act, dffn, TN, (NDEV, d // bn_d, seq // bl),
                  pl.BlockSpec((None, bl, ns_ff), lambda i, j, k: (i, k, 0)), pl.BlockSpec((bl, bn_d), lambda i, j, k: (k, j)),
                  pl.BlockSpec((None, ns_ff, bn_d), lambda i, j, k: (i, 0, j)), (NDEV, ns_ff, d), BF16, (ns_ff, bn_d))
    h_g_down = start(gw_down, False, "comm_a2a_w_down")
    dact = _mm("mm_dact", dffn, w_down_g, NT, (seq // bm, NDEV, d // bk),
               pl.BlockSpec((bm, bk), lambda i, j, k: (i, k)), pl.BlockSpec((None, ns_ff, bk), lambda i, j, k: (j, 0, k)),
               pl.BlockSpec((None, bm, ns_ff), lambda i, j, k: (j, i, 0)), (NDEV, seq, ns_ff), F32, (bm, ns_ff),
               behind=tok_tile())

    def dact_body(gu_ref, da_ref, dgu_ref):
        _, vjp = jax.vjp(_seg_act, gu_ref[0], gu_ref[1])
        dg, du_ = vjp((da_ref[...],))
        dgu_ref[0] = dg.astype(dgu_ref.dtype)
        dgu_ref[1] = du_.astype(dgu_ref.dtype)

    dgu4 = pl.pallas_call(dact_body, name="seg_act_bwd", grid=(NDEV, seq // ft), in_specs=[pair_spec, one_spec],
                          out_specs=pair_spec, out_shape=sds((NDEV, 2, seq, ns_ff), BF16), compiler_params=_params(2))(gu4, dact)
    dgu = dgu4.reshape(2 * NDEV, seq, ns_ff)
    bmd = _tile(d, BM)
    gw_gu = _mm("mm_dw_gu", xf, dgu, TN, (d // bmd, 2 * NDEV, seq // bl),
                pl.BlockSpec((bl, bmd), lambda i, j, k: (k, i)), pl.BlockSpec((None, bl, ns_ff), lambda i, j, k: (j, k, 0)),
                pl.BlockSpec((None, bmd, ns_ff), lambda i, j, k: (j, i, 0)), (2 * NDEV, d, ns_ff), BF16, (bmd, ns_ff))
    h_g_gu = start(gw_gu.reshape(NDEV, 2, d, ns_ff), False, "comm_a2a_w_gu")
    dxf = _mm("mm_dxf", dgu, w_gu_g, NT, (seq // bm, d // bn_d, 2 * NDEV),
              pl.BlockSpec((None, bm, ns_ff), lambda i, j, k: (k, i, 0)), pl.BlockSpec((None, bn_d, ns_ff), lambda i, j, k: (k, j, 0)),
              pl.BlockSpec((bm, bn_d), lambda i, j, k: (i, j)), (seq, d), F32, (bm, bn_d), behind=tok_tile())
    (do, dx_a, d_gate_m, d_g_ffn, d_scale_f, d_shift_f) = _rowwise_bwd(
        "seg_mid_bwd", _seg_mid, [xs, o], [row_d] * 2, [gate_m, g_ffn, scale_f, shift_f], [vec_d] * 4,
        [dy, dxf], [row_d] * 2, [[0], [1]], [1, 0], [sds((seq, d), BF16), sds((seq, d), F32)], [row_d] * 2,
        [0, 1, 2, 3], [sds((1, d), F32)] * 4, [vec_d] * 4, n_rt)

    dmixed = _mm_plain("mm_dmixed", do, w_out_g, NT, F32)
    gw_out = _mm_plain("mm_dw_out", mixed, do, TN, BF16)
    h_g_out = start(gw_out.reshape(NDEV, w_out.shape[1], d), False, "comm_a2a_w_out")
    (dz, dy1_a, dy_attn, d_g_ssm, d_g_attn) = _rowwise_bwd(
        "seg_mix_bwd", _seg_mix, [y1, z, y_attn], [row_s, row_s, row_a], [behind(g_ssm_out), g_attn_out], [vec_s, vec_a],
        [dmixed], [row_m], [[0]], [1, 0, 2], [sds((seq, w_ssm), BF16), sds((seq, w_ssm), F32), sds((seq, w_attn), F32)],
        [row_s, row_s, row_a], [0, 1], [sds((1, w_ssm), F32), sds((1, w_attn), F32)], [vec_s, vec_a], n_rt)
    dy1_b = _mm_plain("mm_dy1", dz, w_glu_g, NT, F32)
    gw_glu = _mm_plain("mm_dw_glu", y1, dz, TN, BF16)
    h_g_glu = start(gw_glu.reshape(NDEV, w_glu.shape[1], w_ssm), False, "comm_a2a_w_glu")
    (dypre,) = _rowwise_bwd("seg_gelu_bwd", _seg_gelu, [ypre], [row_s], [], [], [dy1_a, dy1_b], [row_s] * 2, [[0, 1]],
                            [0], [sds((seq, w_ssm), F32)], [row_s], [], [], [], n_rt)
    (du, db_blk_r, db_blk_i, dc_blk_r, dc_blk_i, dlam_r3, dlam_i3, dd_skip2) = _s5_backward(
        dypre, proj, x_re, x_im, b_blk_r, b_blk_i, c_blk_r, c_blk_i, lam_r3, lam_i3, behind(d_skip2), n_blk, t_chunk)
    dqh, dkh, dv = _attention_backward(qh, kh, proj, v_col, y_attn, dy_attn, n_pair, tq, tk)
    (dq, dk, dqg_t, dkg_t) = _rowwise_bwd(
        "seg_qk_bwd", _seg_qk, qk_rows, qk_specs, [qg_t, kg_t], [vec_a] * 2, [dqh, dkh], [row_a] * 2, [[0], [1]],
        [0, 1], [sds((seq, w_attn), BF16)] * 2, [row_a] * 2, [0, 1], [sds((1, w_attn), F32)] * 2, [vec_a] * 2, n_rt)
    dproj = jnp.concatenate([du.astype(BF16), dq, dk, dv.astype(BF16)], axis=-1)
    bk_in = _tile(ns_in, BK)
    per_k = ns_in // bk_in
    gw_in = _mm("mm_dw_in", xm, dproj, TN, (d // bmd, NDEV * per, seq // bl),
                pl.BlockSpec((bl, bmd), lambda i, j, k: (k, i)), pl.BlockSpec((bl, bn_in), lambda i, j, k: (k, j)),
                pl.BlockSpec((None, bmd, bn_in), lambda i, j, k: (j // per, i, j % per)), (NDEV, d, ns_in), BF16, (bmd, bn_in))
    h_g_in = start(gw_in, False, "comm_a2a_w_in")
    dxm = _mm("mm_dxm", dproj, w_in_g, NT, (seq // bm, d // bn_d, NDEV * per_k),
              pl.BlockSpec((bm, bk_in), lambda i, j, k: (i, k)),
              pl.BlockSpec((None, bn_d, bk_in), lambda i, j, k: (k // per_k, j, k % per_k)),
              pl.BlockSpec((bm, bn_d), lambda i, j, k: (i, j)), (seq, d), F32, (bm, bn_d), behind=tok_tile())
    (grad_x, d_shift_m, d_scale_m, d_g_mix) = _rowwise_bwd(
        "seg_in_bwd", _seg_in, [xs], [row_d], [shift_m, scale_m, behind(g_mix)], [vec_d] * 3, [dxm, dx_a], [row_d] * 2, [[0], [1]],
        [0], [sds((seq, d), F32)], [row_d], [0, 1, 2], [sds((1, d), F32)] * 3, [vec_d] * 3, n_rt)

    dbb_r = _from_b_blocks(db_blk_r, n_blk, p_state).reshape(gp, SSM_GROUP)
    dbb_i = _from_b_blocks(db_blk_i, n_blk, p_state).reshape(gp, SSM_GROUP)
    dcoef_r2, dcoef_i2, db_re2, db_im2 = _whole_vjp("s5_bbar_bwd", _s5_bbar, [coef_r2, coef_i2, b_re2, b_im2], [dbb_r, dbb_i],
                                                    [(gp, 1), (gp, 1), (gp, SSM_GROUP), (gp, SSM_GROUP)])
    lam_cts = [dlam_r3.reshape(n_groups, p_state), dlam_i3.reshape(n_groups, p_state),
               dcoef_r2.reshape(n_groups, p_state), dcoef_i2.reshape(n_groups, p_state)]
    da_re2, da_im2, dldt2 = _whole_vjp("s5_lam_bwd", _s5_lam, [a_re2, a_im2, ldt2], lam_cts,
                                       [(n_groups, p_state), (n_groups, p_state), (n_groups, 1)])
    dc_re2, dc_im2 = _from_c_blocks(dc_blk_r, n_blk, p_state), _from_c_blocks(dc_blk_i, n_blk, p_state)

    dmod = jnp.concatenate([d_shift_m, d_scale_m, d_gate_m, d_shift_f, d_scale_f, d_gate_f], axis=-1)
    small_part = {
        "b_ada": dmod, "g_mix": d_g_mix, "a_re": da_re2, "a_im": da_im2, "log_dt": dldt2, "b_re": db_re2, "b_im": db_im2,
        "c_re": dc_re2, "c_im": dc_im2, "d_skip": dd_skip2,
        "q_gain": dqg_t.reshape(n_heads, HEAD_DIM).sum(0), "k_gain": dkg_t.reshape(n_heads, HEAD_DIM).sum(0),
        "g_ssm_out": d_g_ssm, "g_attn_out": d_g_attn, "g_ffn": d_g_ffn,
    }
    h_small = start(_pack([small_part[n] for n in SMALL]), True, "comm_ag_small")

    big = {}
    after = jnp.broadcast_to(order_tok[0], (SUBLANES, LANES))

    def sharded(nm, handle, part_block, index_map, tr, after):
        got = _exchange_finish(handle, after)
        big[nm] = _adam_sum("adam_" + nm, got, pl.BlockSpec(part_block, index_map), given[nm][0], given["m_" + nm][0],
                            given["v_" + nm][0], tr)
        return got

    tr = _tile(w_down.shape[1], 64)
    sharded("w_down", h_g_down, (NDEV, tr, d), lambda i: (0, i, 0), tr, after)
    tr = _tile(d, 256)
    got_gu = sharded("w_gate", h_g_gu, (NDEV, None, tr, ns_ff), lambda i: (0, 0, i, 0), tr, big["w_down"][1])
    big["w_up"] = _adam_sum("adam_w_up", got_gu, pl.BlockSpec((NDEV, None, tr, ns_ff), lambda i: (0, 1, i, 0)),
                            w_up[0], m_w_up[0], v_w_up[0], tr)
    tr = _tile(w_out.shape[1], 128)
    sharded("w_out", h_g_out, (NDEV, tr, d), lambda i: (0, i, 0), tr, big["w_up"][1])
    tr = _tile(w_glu.shape[1], 128)
    sharded("w_glu", h_g_glu, (NDEV, tr, w_ssm), lambda i: (0, i, 0), tr, big["w_out"][1])
    tr = _tile(d, 256)
    sharded("w_in", h_g_in, (NDEV, tr, ns_in), lambda i: (0, i, 0), tr, big["w_glu"][1])

    packed_parts = _exchange_finish(h_small, big["w_in"][1])
    rows_p = packed_parts.shape[1]
    tr_p = _tile(rows_p, 64)
    small_spec = pl.BlockSpec((NDEV, tr_p, PACK_COLS), lambda i: (0, i, 0))
    sm = _adam_sum("adam_small", packed_parts, small_spec, _pack([given[n] for n in SMALL]),
                   _pack([given["m_" + n] for n in SMALL]), _pack([given["v_" + n] for n in SMALL]), tr_p)
    like = [given[n] for n in SMALL]
    small_out = [dict(zip(SMALL, _unpack(t, like))) for t in sm]

    dmod_all = packed_parts[:, :(6 * d) // PACK_COLS, :].reshape(NDEV, 6 * d) if (6 * d) % PACK_COLS == 0 else None
    assert dmod_all is not None
    dmod_cols = lax.dynamic_slice(dmod_all, (0, me * n_ada), (NDEV, n_ada))
    big["w_ada"] = _adam_ada(c_all, dmod_cols, w_ada[0], m_w_ada[0], v_w_ada[0])

    order = ("w_ada", "b_ada", "g_mix", "w_in", "a_re", "a_im", "log_dt", "b_re", "b_im", "c_re", "c_im", "d_skip", "w_glu",
             "q_gain", "k_gain", "g_ssm_out", "g_attn_out", "w_out", "g_ffn", "w_gate", "w_up", "w_down")
    outs = [loss, grad_x[None]]
    for kind in range(4):
        for n in order:
            outs.append(big[n][kind][None] if n in big else small_out[kind][n])
    return tuple(outs)
```

```python
import functools
import math

import jax
import jax.numpy as jnp
from jax import lax
from jax.experimental import pallas as pl
from jax.experimental.pallas import tpu as pltpu

F32 = jnp.float32
BF16 = jnp.bfloat16
NDEV = 8
MESH_AXES = ("x", "y", "c")
MESH_ID = pl.DeviceIdType.MESH
EPS = 1e-6
LANES = 128
SUBLANES = 8
HEAD_DIM = 64
SSM_GROUP = 16
ADAM_LR, ADAM_B1, ADAM_B2, ADAM_EPS, ADAM_WD, ADAM_STEP = 0.001, 0.9, 0.999, 1e-08, 0.01, 10

NN = (((1,), (0,)), ((), ()))
NT = (((1,), (1,)), ((), ()))
TN = (((0,), (0,)), ((), ()))


def _dot(a, b, dn=NN):
    return lax.dot_general(a, b, dn, preferred_element_type=F32)


def _tile(dim, pref):
    t = min(dim, pref)
    while dim % t:
        t //= 2
    return t


def _params(n):
    return pltpu.CompilerParams(dimension_semantics=("arbitrary",) * n)


def _me():
    mx, my, mc = lax.axis_index("x"), lax.axis_index("y"), lax.axis_index("c")
    return mx, my, mc


def _peer(mx, my, mc, k):
    px = 1 - mx if (k >> 2) & 1 else mx
    py = 1 - my if (k >> 1) & 1 else my
    pc = 1 - mc if k & 1 else mc
    return (px, py, pc), 4 * px + 2 * py + pc


def _exchange_copies(x_ref, land_ref, send_sems, recv_sems, gather):
    mx, my, mc = _me()
    me = 4 * mx + 2 * my + mc
    pairs = []
    for k in range(1, NDEV):
        peer, pidx = _peer(mx, my, mc, k)
        src = x_ref if gather else x_ref.at[pidx]
        mk = lambda dst, src=src, k=k, peer=peer: pltpu.make_async_remote_copy(
            src_ref=src, dst_ref=dst, send_sem=send_sems.at[k - 1], recv_sem=recv_sems.at[k - 1],
            device_id=peer, device_id_type=MESH_ID)
        pairs.append((mk(land_ref.at[me]), mk(land_ref.at[pidx])))
    return me, pairs


def _exchange(x, gather, name):
    def body(x_ref, o_ref, token, send_sems, recv_sems, local_sem):
        me, pairs = _exchange_copies(x_ref, o_ref, send_sems, recv_sems, gather)
        local = pltpu.make_async_copy(x_ref if gather else x_ref.at[me], o_ref.at[me], local_sem)
        local.start()
        for send, _ in pairs:
            send.start()
        for _, arrival in pairs:
            arrival.wait_recv()
        for send, _ in pairs:
            send.wait_send()
        local.wait()
        token[...] = jnp.zeros_like(token)

    return pl.pallas_call(
        body, name=name,
        out_shape=(jax.ShapeDtypeStruct(((NDEV,) + x.shape) if gather else x.shape, x.dtype),
                   jax.ShapeDtypeStruct((SUBLANES, LANES), F32)),
        in_specs=[pl.BlockSpec(memory_space=pl.ANY)],
        out_specs=(pl.BlockSpec(memory_space=pl.ANY), pl.BlockSpec(memory_space=pltpu.VMEM)),
        scratch_shapes=[pltpu.SemaphoreType.DMA((NDEV - 1,)), pltpu.SemaphoreType.DMA((NDEV - 1,)), pltpu.SemaphoreType.DMA],
    )(x)


HBM_SPEC = pl.BlockSpec(memory_space=pltpu.HBM)
def _exchange_start(x, gather, name):
    got, token = _exchange(x, gather, name)
    return (got,), token


def _exchange_finish(handle, after):
    return handle[0]


def _mm(name, a, b, dn, grid, a_spec, b_spec, o_spec, out_shape, out_dtype, acc_shape, behind=None):
    nk = grid[2]
    extra = [] if behind is None else [pltpu.with_memory_space_constraint(behind, pltpu.HBM)]

    def body(a_ref, b_ref, *rest):
        rest = rest[len(extra):]
        o_ref = rest[0]
        part = _dot(a_ref[...].astype(BF16), b_ref[...].astype(BF16), dn)
        if nk == 1:
            o_ref[...] = part.astype(o_ref.dtype)
            return
        acc_ref = rest[1]
        k = pl.program_id(2)

        @pl.when(k == 0)
        def _():
            acc_ref[...] = part

        @pl.when(k > 0)
        def _():
            acc_ref[...] += part

        @pl.when(k == nk - 1)
        def _():
            o_ref[...] = acc_ref[...].astype(o_ref.dtype)

    return pl.pallas_call(
        body, name=name, grid=grid, in_specs=[a_spec, b_spec] + [HBM_SPEC] * len(extra), out_specs=o_spec,
        out_shape=jax.ShapeDtypeStruct(out_shape, out_dtype),
        scratch_shapes=[] if nk == 1 else [pltpu.VMEM(acc_shape, F32)], compiler_params=_params(3),
    )(a, b, *extra)


BM, BN, BK = 1024, 1024, 4096


def _mm_plain(name, a, b, dn, out_dtype):
    if dn == NN:
        (m, kk), n = a.shape, b.shape[1]
    elif dn == NT:
        (m, kk), n = a.shape, b.shape[0]
    else:
        (kk, m), n = a.shape, b.shape[1]
    half = 2 if dn == TN else 1
    bm, bn, bk = _tile(m, BM // half), _tile(n, BN // half), _tile(kk, BK)
    a_spec = pl.BlockSpec((bk, bm), lambda i, j, k: (k, i)) if dn == TN else pl.BlockSpec((bm, bk), lambda i, j, k: (i, k))
    b_spec = pl.BlockSpec((bn, bk), lambda i, j, k: (j, k)) if dn == NT else pl.BlockSpec((bk, bn), lambda i, j, k: (k, j))
    return _mm(name, a, b, dn, (m // bm, n // bn, kk // bk), a_spec, b_spec,
               pl.BlockSpec((bm, bn), lambda i, j, k: (i, j)), (m, n), out_dtype, (bm, bn))


def _row_spec(tile, width, col=0):
    return pl.BlockSpec((tile, width), lambda i: (i, col))


def _vec_spec(width, col=0):
    return pl.BlockSpec((1, width), lambda i: (0, col))


def _rowwise_fwd(name, fn, rows, row_specs, vecs, vec_specs, out_shapes, out_specs, n_tiles):
    nr, nv = len(rows), len(vecs)

    def body(*refs):
        ins = [r[...].astype(F32) for r in refs[:nr + nv]]
        outs = fn(*ins)
        for o_ref, o in zip(refs[nr + nv:], outs):
            o_ref[...] = o.astype(o_ref.dtype)

    return pl.pallas_call(body, name=name, grid=(n_tiles,), in_specs=list(row_specs) + list(vec_specs),
                          out_specs=list(out_specs), out_shape=list(out_shapes), compiler_params=_params(1))(*rows, *vecs)


def _rowwise_bwd(name, fn, rows, row_specs, vecs, vec_specs, cts, ct_specs, ct_groups,
                 drow_idx, drow_shapes, drow_specs, dvec_idx, dvec_shapes, dvec_specs, n_tiles):
    nr, nv, nc = len(rows), len(vecs), len(cts)

    def body(*refs):
        ins = [r[...].astype(F32) for r in refs[:nr + nv]]
        ct_vals = [r[...].astype(F32) for r in refs[nr + nv:nr + nv + nc]]
        out_refs = refs[nr + nv + nc:]
        _, vjp = jax.vjp(fn, *ins)
        grads = vjp(tuple(functools.reduce(lambda p, q: p + q, [ct_vals[j] for j in grp]) for grp in ct_groups))
        for o_ref, idx in zip(out_refs[:len(drow_idx)], drow_idx):
            o_ref[...] = grads[idx].astype(o_ref.dtype)
        step = pl.program_id(0)
        for o_ref, idx in zip(out_refs[len(drow_idx):], dvec_idx):
            @pl.when(step == 0)
            def _(o_ref=o_ref):
                o_ref[...] = jnp.zeros_like(o_ref)
            o_ref[...] += grads[nr + idx]

    return pl.pallas_call(body, name=name, grid=(n_tiles,),
                          in_specs=list(row_specs) + list(vec_specs) + list(ct_specs),
                          out_specs=list(drow_specs) + list(dvec_specs),
                          out_shape=list(drow_shapes) + list(dvec_shapes), compiler_params=_params(1))(*rows, *vecs, *cts)


def _rms(x):
    return x * lax.rsqrt(jnp.mean(x * x, axis=-1, keepdims=True) + EPS)


def _seg_in(x, shift, scale, gain):
    return _rms(x) * gain * (1.0 + scale) + shift, x


def _seg_qk(q, k, qg, kg):
    def norm(t, g, mult):
        blocks = []
        lane = lax.broadcasted_iota(jnp.int32, (1, LANES), 1)
        for p in range(t.shape[1] // LANES):
            tb = t[:, p * LANES:(p + 1) * LANES]
            sq = tb * tb
            lo = jnp.sum(jnp.where(lane < HEAD_DIM, sq, 0.0), axis=-1, keepdims=True)
            hi = jnp.sum(jnp.where(lane < HEAD_DIM, 0.0, sq), axis=-1, keepdims=True)
            ms = jnp.where(lane < HEAD_DIM, lo, hi) * (1.0 / HEAD_DIM)
            blocks.append(tb * lax.rsqrt(ms + EPS) * (g[:, p * LANES:(p + 1) * LANES] * mult))
        return jnp.concatenate(blocks, axis=-1) if len(blocks) > 1 else blocks[0]
    return norm(q, qg, 1.0 / math.sqrt(HEAD_DIM)), norm(k, kg, 1.0)


def _seg_gelu(ypre):
    return (jax.nn.gelu(ypre),)


def _seg_mix(y1, z, yattn, g_ssm, g_attn):
    ys = y1 * jax.nn.sigmoid(z)
    return (jnp.concatenate([_rms(ys) * g_ssm, _rms(yattn) * g_attn], axis=-1),)


def _seg_mid(x, o, gate_m, g_ffn, scale_f, shift_f):
    h1 = x + gate_m * o
    return h1, _rms(h1) * g_ffn * (1.0 + scale_f) + shift_f


def _seg_act(gate, up):
    return (jax.nn.silu(gate) * up,)


def _s5_lam(a_re, a_im, log_dt):
    dt = jnp.exp(log_dt)
    mag = jnp.exp(a_re * dt)
    lr, li = mag * jnp.cos(a_im * dt), mag * jnp.sin(a_im * dt)
    den = a_re * a_re + a_im * a_im
    nr, ni = lr - 1.0, li
    return lr, li, (nr * a_re + ni * a_im) / den, (ni * a_re - nr * a_im) / den


def _s5_bbar(coef_re, coef_im, b_re, b_im):
    return coef_re * b_re - coef_im * b_im, coef_re * b_im + coef_im * b_re


def _whole(name, fn, ins, out_shapes):
    n = len(ins)

    def body(*refs):
        outs = fn(*[r[...] for r in refs[:n]])
        for o_ref, o in zip(refs[n:], outs):
            o_ref[...] = o

    return pl.pallas_call(body, name=name, out_shape=[jax.ShapeDtypeStruct(s, F32) for s in out_shapes])(*ins)


def _whole_vjp(name, fn, ins, cts, out_shapes):
    n, nc = len(ins), len(cts)

    def body(*refs):
        _, vjp = jax.vjp(fn, *[r[...] for r in refs[:n]])
        grads = vjp(tuple(r[...] for r in refs[n:n + nc]))
        for o_ref, g in zip(refs[n + nc:], grads):
            o_ref[...] = g

    return pl.pallas_call(body, name=name, out_shape=[jax.ShapeDtypeStruct(s, F32) for s in out_shapes])(*ins, *cts)


SCAN_SHIFTS = (1, 2, 4)


def _cmul(ar, ai, br, bi):
    return ar * br - ai * bi, ar * bi + ai * br


def _scan_coefs(lr, li, reverse):
    s = lr.shape[1]
    row = lax.broadcasted_iota(jnp.int32, (SUBLANES, s), 0)
    p1 = (lr, li)
    p2 = _cmul(*p1, *p1)
    p4 = _cmul(*p2, *p2)
    p8 = _cmul(*p4, *p4)
    p3, p5, p6 = _cmul(*p1, *p2), _cmul(*p4, *p1), _cmul(*p4, *p2)
    p7 = _cmul(*p6, *p1)
    pows = (p1, p2, p3, p4, p5, p6, p7, p8)
    bc = lambda t: jnp.broadcast_to(t, (SUBLANES, s))
    steps = []
    for sh, pw in zip(SCAN_SHIFTS, (p1, p2, p4)):
        keep = (row + sh <= SUBLANES - 1) if reverse else (row >= sh)
        steps.append((jnp.where(keep, bc(pw[0]), 0.0), jnp.where(keep, bc(pw[1]), 0.0)))
    cr, ci = jnp.zeros((SUBLANES, s), F32), jnp.zeros((SUBLANES, s), F32)
    for r in range(SUBLANES):
        pw = pows[SUBLANES - 1 - r] if reverse else pows[r]
        cr = jnp.where(row == r, bc(pw[0]), cr)
        ci = jnp.where(row == r, bc(pw[1]), ci)
    return steps, (cr, ci)


def _scan_tile(xr, xi, steps, carry_pow, cr, ci, reverse):
    for sh, (ar, ai) in zip(SCAN_SHIFTS, steps):
        rs = SUBLANES - sh if reverse else sh
        sr, si = pltpu.roll(xr, rs, 0), pltpu.roll(xi, rs, 0)
        xr, xi = xr + ar * sr - ai * si, xi + ar * si + ai * sr
    pr, pi = carry_pow
    return xr + pr * cr - pi * ci, xi + pr * ci + pi * cr


def _s5_forward(proj, b_blk_re, b_blk_im, c_blk_re, c_blk_im, lam_re, lam_im, d_skip, n_blk, t_chunk):
    seq = proj.shape[0]
    n_chunks = seq // t_chunk
    n_tiles = t_chunk // SUBLANES
    s = b_blk_re.shape[2]

    def body(u_ref, bre_ref, bim_ref, cre_ref, cim_ref, lr_ref, li_ref, d_ref, y_ref, xr_ref, xi_ref, wr, wi, carry):
        t = pl.program_id(1)

        @pl.when(t == 0)
        def _():
            carry[...] = jnp.zeros_like(carry)

        u = u_ref[...]
        ub = u.astype(BF16)
        wr[...] = _dot(ub, bre_ref[...])
        wi[...] = _dot(ub, bim_ref[...])
        steps, cpow = _scan_coefs(lr_ref[...], li_ref[...], False)

        def tile(i, c):
            r0 = pl.multiple_of(i * SUBLANES, SUBLANES)
            xr, xi = _scan_tile(wr[pl.ds(r0, SUBLANES), :], wi[pl.ds(r0, SUBLANES), :], steps, cpow, c[0], c[1], False)
            xr_ref[pl.ds(r0, SUBLANES), :] = xr
            xi_ref[pl.ds(r0, SUBLANES), :] = xi
            last = SUBLANES - 1
            return (jnp.broadcast_to(xr[last:, :], xr.shape), jnp.broadcast_to(xi[last:, :], xi.shape))

        cr, ci = lax.fori_loop(0, n_tiles, tile, (carry[0], carry[1]))
        carry[0] = cr
        carry[1] = ci
        y = _dot(xr_ref[...].astype(BF16), cre_ref[...]) - _dot(xi_ref[...].astype(BF16), cim_ref[...])
        y_ref[...] = y + d_ref[...] * u

    blk = lambda shape: pl.BlockSpec((None,) + shape, lambda j, t: (j, 0, 0))
    return pl.pallas_call(
        body, name="s5_fwd", grid=(n_blk, n_chunks),
        in_specs=[pl.BlockSpec((t_chunk, LANES), lambda j, t: (t, j)), blk((LANES, s)), blk((LANES, s)),
                  blk((s, LANES)), blk((s, LANES)), blk((1, s)), blk((1, s)), pl.BlockSpec((1, LANES), lambda j, t: (0, j))],
        out_specs=[pl.BlockSpec((t_chunk, LANES), lambda j, t: (t, j)), pl.BlockSpec((t_chunk, s), lambda j, t: (t, j)),
                   pl.BlockSpec((t_chunk, s), lambda j, t: (t, j))],
        out_shape=[jax.ShapeDtypeStruct((seq, n_blk * LANES), F32), jax.ShapeDtypeStruct((seq, n_blk * s), F32),
                   jax.ShapeDtypeStruct((seq, n_blk * s), F32)],
        scratch_shapes=[pltpu.VMEM((t_chunk, s), F32), pltpu.VMEM((t_chunk, s), F32), pltpu.VMEM((2, SUBLANES, s), F32)],
        compiler_params=_params(2),
    )(proj, b_blk_re, b_blk_im, c_blk_re, c_blk_im, lam_re, lam_im, d_skip)


def _s5_backward(dypre, proj, x_re, x_im, b_blk_re, b_blk_im, c_blk_re, c_blk_im, lam_re, lam_im, d_skip, n_blk, t_chunk):
    seq = proj.shape[0]
    n_chunks = seq // t_chunk
    n_tiles = t_chunk // SUBLANES
    s = b_blk_re.shape[2]

    def body(dy_ref, u_ref, xr_ref, xi_ref, pr_ref, pi_ref, bre_ref, bim_ref, cre_ref, cim_ref, lr_ref, li_ref, d_ref,
             du_ref, dbre_ref, dbim_ref, dcre_ref, dcim_ref, dlr_ref, dli_ref, dd_ref, gr, gi, carry):
        t = pl.program_id(1)

        @pl.when(t == 0)
        def _():
            carry[...] = jnp.zeros_like(carry)
            for r in (dbre_ref, dbim_ref, dcre_ref, dcim_ref, dlr_ref, dli_ref, dd_ref):
                r[...] = jnp.zeros_like(r)

        dy = dy_ref[...]
        dyb = dy.astype(BF16)
        u = u_ref[...]
        gr[...] = _dot(dyb, cre_ref[...], NT)
        gi[...] = -_dot(dyb, cim_ref[...], NT)
        steps, cpow = _scan_coefs(lr_ref[...], -li_ref[...], True)
        row = lax.broadcasted_iota(jnp.int32, (SUBLANES, s), 0)
        last = SUBLANES - 1
        first_chunk = t == n_chunks - 1

        def tile_at(r0, prev_r, prev_i, c):
            cr, ci, ar, ai = c
            lr_, li_ = _scan_tile(gr[pl.ds(r0, SUBLANES), :], gi[pl.ds(r0, SUBLANES), :], steps, cpow, cr, ci, True)
            gr[pl.ds(r0, SUBLANES), :] = lr_
            gi[pl.ds(r0, SUBLANES), :] = li_
            xr, xi = xr_ref[pl.ds(r0, SUBLANES), :], xi_ref[pl.ds(r0, SUBLANES), :]
            xpr = jnp.where(row == 0, jnp.broadcast_to(prev_r[last:, :], xr.shape), pltpu.roll(xr, 1, 0))
            xpi = jnp.where(row == 0, jnp.broadcast_to(prev_i[last:, :], xi.shape), pltpu.roll(xi, 1, 0))
            ar = ar + lr_ * xpr + li_ * xpi
            ai = ai + li_ * xpr - lr_ * xpi
            return (jnp.broadcast_to(lr_[:1, :], lr_.shape), jnp.broadcast_to(li_[:1, :], li_.shape), ar, ai)

        def tile(ii, c):
            i = n_tiles - 1 - ii
            r0 = pl.multiple_of(i * SUBLANES, SUBLANES)
            rp = pl.multiple_of(r0 - SUBLANES, SUBLANES)
            return tile_at(r0, xr_ref[pl.ds(rp, SUBLANES), :], xi_ref[pl.ds(rp, SUBLANES), :], c)

        zero = jnp.zeros((SUBLANES, s), F32)
        c = lax.fori_loop(0, n_tiles - 1, tile, (carry[0], carry[1], zero, zero))
        keep = jnp.where(first_chunk, 0.0, 1.0)
        c = tile_at(0, pr_ref[...] * keep, pi_ref[...] * keep, c)
        carry[0] = c[0]
        carry[1] = c[1]
        dlr_ref[...] += jnp.sum(c[2], axis=0, keepdims=True)
        dli_ref[...] += jnp.sum(c[3], axis=0, keepdims=True)

        lam_r, lam_i = gr[...].astype(BF16), gi[...].astype(BF16)
        du_ref[...] = _dot(lam_r, bre_ref[...], NT) + _dot(lam_i, bim_ref[...], NT) + d_ref[...] * dy
        ub = u.astype(BF16)
        dbre_ref[...] += _dot(ub, lam_r, TN)
        dbim_ref[...] += _dot(ub, lam_i, TN)
        dcre_ref[...] += _dot(xr_ref[...].astype(BF16), dyb, TN)
        dcim_ref[...] -= _dot(xi_ref[...].astype(BF16), dyb, TN)
        dd_ref[...] += jnp.sum(dy * u, axis=0, keepdims=True)

    rev = lambda t: n_chunks - 1 - t
    blk = lambda shape: pl.BlockSpec((None,) + shape, lambda j, t: (j, 0, 0))
    tpc = t_chunk // SUBLANES
    prev_spec = pl.BlockSpec((SUBLANES, s), lambda j, t: (jnp.maximum(rev(t) * tpc - 1, 0), j))
    chunk = lambda w: pl.BlockSpec((t_chunk, w), lambda j, t: (rev(t), j))
    return pl.pallas_call(
        body, name="s5_bwd", grid=(n_blk, n_chunks),
        in_specs=[chunk(LANES), chunk(LANES), chunk(s), chunk(s), prev_spec, prev_spec, blk((LANES, s)), blk((LANES, s)),
                  blk((s, LANES)), blk((s, LANES)), blk((1, s)), blk((1, s)), pl.BlockSpec((1, LANES), lambda j, t: (0, j))],
        out_specs=[chunk(LANES), blk((LANES, s)), blk((LANES, s)), blk((s, LANES)), blk((s, LANES)), blk((1, s)), blk((1, s)),
                   pl.BlockSpec((1, LANES), lambda j, t: (0, j))],
        out_shape=[jax.ShapeDtypeStruct((seq, n_blk * LANES), F32),
                   jax.ShapeDtypeStruct((n_blk, LANES, s), F32), jax.ShapeDtypeStruct((n_blk, LANES, s), F32),
                   jax.ShapeDtypeStruct((n_blk, s, LANES), F32), jax.ShapeDtypeStruct((n_blk, s, LANES), F32),
                   jax.ShapeDtypeStruct((n_blk, 1, s), F32), jax.ShapeDtypeStruct((n_blk, 1, s), F32),
                   jax.ShapeDtypeStruct((1, n_blk * LANES), F32)],
        scratch_shapes=[pltpu.VMEM((t_chunk, s), F32), pltpu.VMEM((t_chunk, s), F32), pltpu.VMEM((2, SUBLANES, s), F32)],
        compiler_params=_params(2),
    )(dypre, proj, x_re, x_im, x_re, x_im, b_blk_re, b_blk_im, c_blk_re, c_blk_im, lam_re, lam_im, d_skip)


TQ, TK = 256, 128


def _split_bf16(x):
    hi = x.astype(BF16)
    return hi, (x - hi.astype(F32)).astype(BF16)


def _sb_weights(z, past, carry, tri):
    ls = jnp.minimum(z, 0.0) - jnp.log(1.0 + jnp.exp(-jnp.abs(z)))
    lk = ls - z
    if past is not None:
        lk = jnp.where(past, lk, 0.0)
    hi, lo = _split_bf16(lk)
    w = jnp.exp(ls + (_dot(hi, tri) + _dot(lo, tri)) + carry)
    if past is not None:
        w = jnp.where(past, w, 0.0)
    return ls, lk, w


LOG_KEEP_DEAD = -104.0


def _walk_key_blocks(i, ratio, prologue, block, epilogue, log_keep):
    n_kb = (i + 1) * ratio
    prologue(n_kb - 1)
    for n in range(ratio):
        block(n_kb - 1 - n, n % 2, True)
    n_pairs = (i * ratio) // 2

    def more(state):
        t, alive = state
        return jnp.logical_and(t < n_pairs, alive)

    def pair(state):
        t, _ = state
        j = n_kb - 1 - ratio - 2 * t
        block(j, ratio % 2, False)
        block(j - 1, (ratio + 1) % 2, False)
        return t + 1, log_keep() >= LOG_KEEP_DEAD

    done, _ = lax.while_loop(more, pair, (jnp.int32(0), log_keep() >= LOG_KEEP_DEAD))
    epilogue(n_kb - ratio - 2 * done)


def _attention_forward(qh, kh, proj, v_col, n_pair, tq, tk):
    seq = qh.shape[0]
    ratio = tq // tk
    assert ratio % 2 == 0

    def body(q_ref, k_ref, v_ref, o_ref, q_scr, z_scr, w_scr, acc_scr, c_scr):
        i = pl.program_id(1)
        lane = lax.broadcasted_iota(jnp.int32, (1, LANES), 1)
        q2 = q_ref[...]
        q_scr[0] = jnp.where(lane < HEAD_DIM, q2, 0.0).astype(BF16)
        q_scr[1] = jnp.where(lane < HEAD_DIM, 0.0, q2).astype(BF16)
        tri = (lax.broadcasted_iota(jnp.int32, (tk, tk), 0) > lax.broadcasted_iota(jnp.int32, (tk, tk), 1)).astype(BF16)
        qpos = i * tq + lax.broadcasted_iota(jnp.int32, (tq, tk), 0)
        kidx = lax.broadcasted_iota(jnp.int32, (tq, tk), 1)

        def rows(ref, j):
            j = jnp.clip(j, 0, seq // tk - 1)
            return ref[pl.ds(pl.multiple_of(j * tk, tk), tk), :].astype(BF16)

        def scores(j, slot):
            kb = rows(k_ref, j)
            for h in range(2):
                z_scr[slot, h] = _dot(q_scr[h], kb, NT)

        def finish(j):
            vb = rows(v_ref, j)
            for h in range(2):
                acc_scr[h] += _dot(w_scr[h], vb)

        def prologue(j):
            w_scr[...] = jnp.zeros_like(w_scr)
            acc_scr[...] = jnp.zeros_like(acc_scr)
            c_scr[...] = jnp.zeros_like(c_scr)
            scores(j, 0)

        def block(j, slot, masked):
            scores(j - 1, 1 - slot)
            finish(j + 1)
            past = ((kidx + j * tk) < qpos) if masked else None
            for h in range(2):
                _, lk, w = _sb_weights(z_scr[slot, h], past, c_scr[h], tri)
                w_scr[h] = w.astype(BF16)
                c_scr[h] += jnp.sum(lk, axis=-1, keepdims=True)

        _walk_key_blocks(i, ratio, prologue, block, finish, lambda: jnp.max(c_scr[...]))
        o_ref[...] = jnp.where(lane < HEAD_DIM, acc_scr[0], acc_scr[1])

    return pl.pallas_call(
        body, name="attn_fwd", grid=(n_pair, seq // tq),
        in_specs=[pl.BlockSpec((tq, LANES), lambda p, i: (i, p)), pl.BlockSpec((seq, LANES), lambda p, i: (0, p)),
                  pl.BlockSpec((seq, LANES), lambda p, i: (0, v_col + p))],
        out_specs=pl.BlockSpec((tq, LANES), lambda p, i: (i, p)),
        out_shape=jax.ShapeDtypeStruct(qh.shape, F32),
        scratch_shapes=[pltpu.VMEM((2, tq, LANES), BF16), pltpu.VMEM((2, 2, tq, tk), F32), pltpu.VMEM((2, tq, tk), BF16),
                        pltpu.VMEM((2, tq, LANES), F32), pltpu.VMEM((2, tq, 1), F32)],
        compiler_params=_params(2),
    )(qh, kh, proj)


def _attention_backward(qh, kh, proj, v_col, y, dy, n_pair, tq, tk):
    seq = qh.shape[0]

    ratio = tq // tk
    assert ratio % 2 == 0

    def body(q_ref, k_ref, v_ref, y_ref, dy_ref, dq_ref, dk_ref, dv_ref,
             q_scr, do_scr, z_scr, dw_scr, w_scr, dz_scr, dq_scr, c_scr, c2_scr, tot_scr):
        i = pl.program_id(1)

        @pl.when(i == 0)
        def _():
            dk_ref[...] = jnp.zeros_like(dk_ref)
            dv_ref[...] = jnp.zeros_like(dv_ref)

        lane = lax.broadcasted_iota(jnp.int32, (1, LANES), 1)
        sel = (lane < HEAD_DIM, lane >= HEAD_DIM)
        q2, do2 = q_ref[...], dy_ref[...].astype(BF16)
        dot_oy = do2.astype(F32) * y_ref[...]
        for h in range(2):
            q_scr[h] = jnp.where(sel[h], q2, 0.0).astype(BF16)
            do_scr[h] = jnp.where(sel[h], do2, jnp.zeros_like(do2))
            tot_scr[h] = jnp.sum(jnp.where(sel[h], dot_oy, 0.0), axis=-1, keepdims=True)
        r_i, c_i = lax.broadcasted_iota(jnp.int32, (tk, tk), 0), lax.broadcasted_iota(jnp.int32, (tk, tk), 1)
        tri = (r_i > c_i).astype(BF16)
        tri_ge = (r_i >= c_i).astype(BF16)
        qpos = i * tq + lax.broadcasted_iota(jnp.int32, (tq, tk), 0)
        kidx = lax.broadcasted_iota(jnp.int32, (tq, tk), 1)

        def start(j):
            return pl.multiple_of(jnp.clip(j, 0, seq // tk - 1) * tk, tk)

        def scores(j, slot):
            c0 = start(j)
            kb, vb = k_ref[pl.ds(c0, tk), :].astype(BF16), v_ref[pl.ds(c0, tk), :].astype(BF16)
            for h in range(2):
                z_scr[slot, h] = _dot(q_scr[h], kb, NT)
                dw_scr[slot, h] = _dot(do_scr[h], vb, NT)

        def finish(j):
            c0 = start(j)
            kb = k_ref[pl.ds(c0, tk), :].astype(BF16)
            dk_add, dv_add = jnp.zeros((tk, LANES), F32), jnp.zeros((tk, LANES), F32)
            for h in range(2):
                dz = dz_scr[h]
                dq_scr[h] += _dot(dz, kb)
                dk_add = dk_add + _dot(dz, q_scr[h], TN)
                dv_add = dv_add + _dot(w_scr[h], do_scr[h], TN)
            dk_ref[pl.ds(c0, tk), :] += dk_add
            dv_ref[pl.ds(c0, tk), :] += dv_add

        def prologue(j):
            for r in (w_scr, dz_scr, dq_scr, c_scr, c2_scr):
                r[...] = jnp.zeros_like(r)
            scores(j, 0)

        def block(j, slot, masked):
            scores(j - 1, 1 - slot)
            finish(j + 1)
            past = ((kidx + j * tk) < qpos) if masked else None
            for h in range(2):
                ls, lk, w = _sb_weights(z_scr[slot, h], past, c_scr[h], tri)
                wb = w.astype(BF16)
                dlw = dw_scr[slot, h] * wb.astype(F32)
                hi, lo = _split_bf16(dlw)
                dlk = tot_scr[h] - c2_scr[h] - (_dot(hi, tri_ge) + _dot(lo, tri_ge))
                if masked:
                    dlk = jnp.where(past, dlk, 0.0)
                sig = jnp.exp(ls)
                w_scr[h] = wb
                dz_scr[h] = (dlw * (1.0 - sig) - dlk * sig).astype(BF16)
                c_scr[h] += jnp.sum(lk, axis=-1, keepdims=True)
                c2_scr[h] += jnp.sum(dlw, axis=-1, keepdims=True)

        _walk_key_blocks(i, ratio, prologue, block, finish, lambda: jnp.max(c_scr[...]))
        dq_ref[...] = jnp.where(sel[0], dq_scr[0], dq_scr[1])

    blk = pl.BlockSpec((tq, LANES), lambda p, i: (i, p))
    full = pl.BlockSpec((seq, LANES), lambda p, i: (0, p))
    shape = jax.ShapeDtypeStruct(qh.shape, F32)
    return pl.pallas_call(
        body, name="attn_bwd", grid=(n_pair, seq // tq),
        in_specs=[blk, full, pl.BlockSpec((seq, LANES), lambda p, i: (0, v_col + p)), blk, blk],
        out_specs=[blk, full, full], out_shape=[shape, shape, shape],
        scratch_shapes=[pltpu.VMEM((2, tq, LANES), BF16), pltpu.VMEM((2, tq, LANES), BF16),
                        pltpu.VMEM((2, 2, tq, tk), F32), pltpu.VMEM((2, 2, tq, tk), F32),
                        pltpu.VMEM((2, tq, tk), BF16), pltpu.VMEM((2, tq, tk), BF16), pltpu.VMEM((2, tq, LANES), F32),
                        pltpu.VMEM((2, tq, 1), F32), pltpu.VMEM((2, tq, 1), F32), pltpu.VMEM((2, tq, 1), F32)],
        compiler_params=_params(2),
    )(qh, kh, proj, y, dy)


def _loss_head(h1, ffn, target, gate_f, tile):
    seq, d = h1.shape

    def body(h_ref, f_ref, t_ref, g_ref, dy_ref, df_ref, dg_ref, loss_ref):
        @pl.when(pl.program_id(0) == 0)
        def _():
            dg_ref[...] = jnp.zeros_like(dg_ref)
            loss_ref[...] = jnp.zeros_like(loss_ref)

        f, g = f_ref[...], g_ref[...]
        err = h_ref[...] + g * f - t_ref[...]
        dy = err * (1.0 / d)
        dy_ref[...] = dy
        df_ref[...] = (dy * g).astype(df_ref.dtype)
        dg_ref[...] += jnp.sum(dy * f, axis=0, keepdims=True)
        loss_ref[...] += jnp.sum(jnp.sum(err * err, axis=-1, keepdims=True), axis=0, keepdims=True) * (0.5 / d)

    row = _row_spec(tile, d)
    return pl.pallas_call(
        body, name="loss_head", grid=(seq // tile,), in_specs=[row, row, row, _vec_spec(d)],
        out_specs=[row, row, _vec_spec(d), pl.BlockSpec((1, 1), lambda i: (0, 0))],
        out_shape=[jax.ShapeDtypeStruct((seq, d), F32), jax.ShapeDtypeStruct((seq, d), BF16),
                   jax.ShapeDtypeStruct((1, d), F32), jax.ShapeDtypeStruct((1, 1), F32)],
        compiler_params=_params(1),
    )(h1, ffn, target, gate_f)


def _dot3(a, b, dn):
    ah, al = _split_bf16(a)
    bh, bl = _split_bf16(b)
    return _dot(ah, bh, dn) + (_dot(ah, bl, dn) + _dot(al, bh, dn))


def _ada_forward(c_all, w_shard, b_cols):
    d, n = w_shard.shape
    bk = _tile(d, 512)

    def body(c_ref, w_ref, b_ref, o_ref):
        @pl.when(pl.program_id(0) == 0)
        def _():
            o_ref[...] = jnp.broadcast_to(b_ref[...], o_ref.shape)

        o_ref[...] += _dot3(jax.nn.silu(c_ref[...]), w_ref[...], NN)

    return pl.pallas_call(
        body, name="ada_fwd", grid=(d // bk,),
        in_specs=[pl.BlockSpec((NDEV, bk), lambda k: (0, k)), pl.BlockSpec((bk, n), lambda k: (k, 0)), _vec_spec(n)],
        out_specs=pl.BlockSpec((NDEV, n), lambda k: (0, 0)), out_shape=jax.ShapeDtypeStruct((NDEV, n), F32),
        compiler_params=_params(1),
    )(c_all, w_shard, b_cols)


def _adam(w, g, m, v):
    m = ADAM_B1 * m + (1.0 - ADAM_B1) * g
    v = ADAM_B2 * v + (1.0 - ADAM_B2) * (g * g)
    m_hat = m / (1.0 - ADAM_B1 ** ADAM_STEP)
    v_hat = v / (1.0 - ADAM_B2 ** ADAM_STEP)
    return -ADAM_LR * (m_hat / (jnp.sqrt(v_hat) + ADAM_EPS) + ADAM_WD * w), m, v


def _adam_ada(c_all, dmod_cols, w, m, v):
    d, n = w.shape
    tr = _tile(d, 256)

    def body(c_ref, dm_ref, w_ref, m_ref, v_ref, g_ref, dl_ref, nm_ref, nv_ref):
        g = _dot3(jax.nn.silu(c_ref[...]), dm_ref[...], TN)
        delta, nm, nv = _adam(w_ref[...], g, m_ref[...], v_ref[...])
        g_ref[...] = g
        dl_ref[...] = delta
        nm_ref[...] = nm
        nv_ref[...] = nv

    row = _row_spec(tr, n)
    return pl.pallas_call(
        body, name="adam_ada", grid=(d // tr,),
        in_specs=[pl.BlockSpec((NDEV, tr), lambda i: (0, i)), pl.BlockSpec((NDEV, n), lambda i: (0, 0)), row, row, row],
        out_specs=[row] * 4, out_shape=[jax.ShapeDtypeStruct((d, n), F32)] * 4, compiler_params=_params(1),
    )(c_all, dmod_cols, w, m, v)


def _adam_sum(name, parts, part_spec, w, m, v, tr):
    r, c = w.shape

    def body(p_ref, w_ref, m_ref, v_ref, g_ref, dl_ref, nm_ref, nv_ref):
        g = p_ref[0].astype(F32)
        for k in range(1, NDEV):
            g = g + p_ref[k].astype(F32)
        delta, nm, nv = _adam(w_ref[...], g, m_ref[...], v_ref[...])
        g_ref[...] = g
        dl_ref[...] = delta
        nm_ref[...] = nm
        nv_ref[...] = nv

    row = _row_spec(tr, c)
    return pl.pallas_call(
        body, name=name, grid=(r // tr,), in_specs=[part_spec, row, row, row],
        out_specs=[row] * 4, out_shape=[jax.ShapeDtypeStruct((r, c), F32)] * 4, compiler_params=_params(1),
    )(parts, w, m, v)


GROUPS_PER_BLOCK = LANES // SSM_GROUP


def _to_b_blocks(bb, n_blk, p):
    t = bb.reshape(n_blk, GROUPS_PER_BLOCK, p, SSM_GROUP)
    eye = jnp.eye(GROUPS_PER_BLOCK, dtype=bb.dtype)
    return jnp.einsum("jgph,gk->jghkp", t, eye).reshape(n_blk, LANES, GROUPS_PER_BLOCK * p)


def _from_b_blocks(blk, n_blk, p):
    t = blk.reshape(n_blk, GROUPS_PER_BLOCK, SSM_GROUP, GROUPS_PER_BLOCK, p)
    eye = jnp.eye(GROUPS_PER_BLOCK, dtype=blk.dtype)
    return jnp.einsum("jghkp,gk->jgph", t, eye).reshape(n_blk * GROUPS_PER_BLOCK, p, SSM_GROUP)


def _to_c_blocks(cc, n_blk, p):
    t = cc.reshape(n_blk, GROUPS_PER_BLOCK, SSM_GROUP, p)
    eye = jnp.eye(GROUPS_PER_BLOCK, dtype=cc.dtype)
    return jnp.einsum("jghp,gk->jgpkh", t, eye).reshape(n_blk, GROUPS_PER_BLOCK * p, LANES)


def _from_c_blocks(blk, n_blk, p):
    t = blk.reshape(n_blk, GROUPS_PER_BLOCK, p, GROUPS_PER_BLOCK, SSM_GROUP)
    eye = jnp.eye(GROUPS_PER_BLOCK, dtype=blk.dtype)
    return jnp.einsum("jgpkh,gk->jghp", t, eye).reshape(n_blk * GROUPS_PER_BLOCK, SSM_GROUP, p)


SMALL = ("b_ada", "g_mix", "a_re", "a_im", "log_dt", "b_re", "b_im", "c_re", "c_im", "d_skip",
         "q_gain", "k_gain", "g_ssm_out", "g_attn_out", "g_ffn")
PACK_COLS = 1024


def _pack(arrs):
    flat = jnp.concatenate([a.reshape(-1) for a in arrs])
    n = flat.shape[0]
    quantum = SUBLANES * PACK_COLS
    padded = -(-n // quantum) * quantum
    return jnp.pad(flat, (0, padded - n)).reshape(padded // PACK_COLS, PACK_COLS)


def _unpack(packed, like):
    flat, out, off = packed.reshape(-1), [], 0
    for a in like:
        out.append(flat[off:off + a.size].reshape(a.shape))
        off += a.size
    return out


def kernel(x, c, w_ada, b_ada, g_mix, w_in, a_re, a_im, log_dt, b_re, b_im, c_re, c_im, d_skip, w_glu, q_gain, k_gain, g_ssm_out, g_attn_out, w_out, g_ffn, w_gate, w_up, w_down, loss_target, m_w_ada, m_b_ada, m_g_mix, m_w_in, m_a_re, m_a_im, m_log_dt, m_b_re, m_b_im, m_c_re, m_c_im, m_d_skip, m_w_glu, m_q_gain, m_k_gain, m_g_ssm_out, m_g_attn_out, m_w_out, m_g_ffn, m_w_gate, m_w_up, m_w_down, v_w_ada, v_b_ada, v_g_mix, v_w_in, v_a_re, v_a_im, v_log_dt, v_b_re, v_b_im, v_c_re, v_c_im, v_d_skip, v_w_glu, v_q_gain, v_k_gain, v_g_ssm_out, v_g_attn_out, v_w_out, v_g_ffn, v_w_gate, v_w_up, v_w_down):
    given = dict(locals())
    seq, d = x.shape[1], x.shape[2]
    xs, tgt = x[0], loss_target[0]
    n_groups, p_state = a_re.shape[1], a_re.shape[2]
    w_ssm = n_groups * SSM_GROUP
    w_attn = w_in.shape[2] * NDEV - w_ssm
    w_attn //= 3
    n_blk, n_pair = w_ssm // LANES, w_attn // LANES
    n_heads = w_attn // HEAD_DIM
    ns_in, ns_ff = w_in.shape[2], w_gate.shape[2]
    d_mix = w_ssm + w_attn
    mx, my, mc = _me()
    me = 4 * mx + 2 * my + mc
    rt = _tile(seq, 256)
    n_rt = seq // rt
    sds = jax.ShapeDtypeStruct

    order_tok = [jnp.zeros((), F32)]

    def start(x, gather, name):
        handle, token = _exchange_start(x, gather, name)
        order_tok[0] = order_tok[0] + token[0, 0]
        return handle

    def behind(v):
        return v + order_tok[0].astype(v.dtype)

    c_all = _exchange(c, True, "comm_ag_c")[0].reshape(NDEV, d)
    n_ada = w_ada.shape[2]
    b_cols = lax.dynamic_slice(b_ada, (0, me * n_ada), (1, n_ada))
    mod_cols = _ada_forward(c_all, w_ada[0], b_cols)
    mod_all, mod_token = _exchange(mod_cols, True, "comm_ag_mod")
    mod = lax.dynamic_slice(mod_all, (0, me, 0), (NDEV, 1, n_ada)).reshape(1, NDEV * n_ada)
    order_tok[0] = mod_token[0, 0]
    h_w_in = start(behind(w_in[0]).astype(BF16), True, "comm_ag_w_in")
    h_w_glu = start(behind(w_glu[0]).astype(BF16), True, "comm_ag_w_glu")
    h_w_out = start(behind(w_out[0]).astype(BF16), True, "comm_ag_w_out")
    h_w_gu = start(behind(jnp.stack([w_gate[0], w_up[0]])).astype(BF16), True, "comm_ag_w_gu")
    h_w_down = start(behind(w_down[0]).astype(BF16), True, "comm_ag_w_down")
    mod = behind(mod)
    shift_m, scale_m, gate_m, shift_f, scale_f, gate_f = [mod[:, i * d:(i + 1) * d] for i in range(6)]

    gp = n_groups * p_state
    a_re2, a_im2, ldt2 = a_re[0], a_im[0], log_dt[0].reshape(n_groups, 1)
    b_re2, b_im2 = b_re[0].reshape(gp, SSM_GROUP), b_im[0].reshape(gp, SSM_GROUP)
    lam_r, lam_i, coef_r, coef_i = _whole("s5_lam", _s5_lam, [a_re2, a_im2, ldt2], [(n_groups, p_state)] * 4)
    coef_r2, coef_i2 = coef_r.reshape(gp, 1), coef_i.reshape(gp, 1)
    bb_r, bb_i = _whole("s5_bbar", _s5_bbar, [coef_r2, coef_i2, b_re2, b_im2], [(gp, SSM_GROUP)] * 2)
    s_blk = GROUPS_PER_BLOCK * p_state
    b_blk_r = _to_b_blocks(bb_r.reshape(n_groups, p_state, SSM_GROUP), n_blk, p_state).astype(BF16)
    b_blk_i = _to_b_blocks(bb_i.reshape(n_groups, p_state, SSM_GROUP), n_blk, p_state).astype(BF16)
    c_blk_r = _to_c_blocks(c_re[0], n_blk, p_state).astype(BF16)
    c_blk_i = _to_c_blocks(c_im[0], n_blk, p_state).astype(BF16)
    lam_r3, lam_i3 = lam_r.reshape(n_blk, 1, s_blk), lam_i.reshape(n_blk, 1, s_blk)
    d_skip2 = d_skip[0].reshape(1, w_ssm)

    row_d, vec_d = _row_spec(rt, d), _vec_spec(d)
    xm, = _rowwise_fwd("seg_in", lambda *a: _seg_in(*a)[:1], [xs], [row_d], [shift_m, scale_m, g_mix], [vec_d] * 3,
                       [sds((seq, d), BF16)], [row_d], n_rt)
    bn_in = _tile(ns_in, 512)
    per = ns_in // bn_in
    bm, bk = _tile(seq, BM), _tile(d, BK)
    w_in_g = _exchange_finish(h_w_in, xm)
    proj = _mm("mm_in", xm, w_in_g, NN, (seq // bm, NDEV * per, d // bk),
               pl.BlockSpec((bm, bk), lambda i, j, k: (i, k)),
               pl.BlockSpec((None, bk, bn_in), lambda i, j, k: (j // per, k, j % per)),
               pl.BlockSpec((bm, bn_in), lambda i, j, k: (i, j)), (seq, NDEV * ns_in), F32, (bm, bn_in))
    q_col, k_col, v_col = w_ssm // w_attn, w_ssm // w_attn + 1, (w_ssm + 2 * w_attn) // LANES
    qg_t, kg_t = jnp.tile(q_gain, (1, n_heads)), jnp.tile(k_gain, (1, n_heads))
    row_a, vec_a = _row_spec(rt, w_attn), _vec_spec(w_attn)
    qk_rows, qk_specs = [proj, proj], [_row_spec(rt, w_attn, q_col), _row_spec(rt, w_attn, k_col)]
    qh, kh = _rowwise_fwd("seg_qk", _seg_qk, qk_rows, qk_specs, [qg_t, kg_t], [vec_a] * 2,
                          [sds((seq, w_attn), F32)] * 2, [row_a] * 2, n_rt)
    t_chunk = _tile(seq, 256)
    ypre, x_re, x_im = _s5_forward(proj, b_blk_r, b_blk_i, c_blk_r, c_blk_i, lam_r3, lam_i3, d_skip2, n_blk, t_chunk)
    tq, tk = _tile(seq, TQ), _tile(seq, TK)
    y_attn = _attention_forward(qh, kh, proj, v_col, n_pair, tq, tk)
    row_s, vec_s = _row_spec(rt, w_ssm), _vec_spec(w_ssm)
    y1, = _rowwise_fwd("seg_gelu", _seg_gelu, [ypre], [row_s], [], [], [sds((seq, w_ssm), F32)], [row_s], n_rt)
    w_glu_g = _exchange_finish(h_w_glu, y1).reshape(w_ssm, w_ssm)
    z = _mm_plain("mm_glu", y1, w_glu_g, NN, F32)
    row_m = _row_spec(rt, d_mix)
    mixed, = _rowwise_fwd("seg_mix", _seg_mix, [y1, z, y_attn], [row_s, row_s, row_a], [g_ssm_out, g_attn_out], [vec_s, vec_a],
                          [sds((seq, d_mix), BF16)], [row_m], n_rt)
    w_out_g = _exchange_finish(h_w_out, mixed).reshape(d_mix, d)
    o = _mm_plain("mm_out", mixed, w_out_g, NN, F32)
    h1, xf = _rowwise_fwd("seg_mid", _seg_mid, [xs, o], [row_d] * 2, [gate_m, g_ffn, scale_f, shift_f], [vec_d] * 4,
                          [sds((seq, d), F32), sds((seq, d), BF16)], [row_d] * 2, n_rt)
    w_gu_g = _exchange_finish(h_w_gu, xf).reshape(2 * NDEV, d, ns_ff)
    gu = _mm("mm_gu", xf, w_gu_g, NN, (seq // bm, 2 * NDEV, d // bk),
             pl.BlockSpec((bm, bk), lambda i, j, k: (i, k)), pl.BlockSpec((None, bk, ns_ff), lambda i, j, k: (j, k, 0)),
             pl.BlockSpec((None, bm, ns_ff), lambda i, j, k: (j, i, 0)), (2 * NDEV, seq, ns_ff), F32, (bm, ns_ff))
    gu4 = gu.reshape(NDEV, 2, seq, ns_ff)
    ft = _tile(seq, 512)
    pair_spec = pl.BlockSpec((None, 2, ft, ns_ff), lambda s, i: (s, 0, i, 0))
    one_spec = pl.BlockSpec((None, ft, ns_ff), lambda s, i: (s, i, 0))

    def act_body(gu_ref, a_ref):
        a_ref[...] = _seg_act(gu_ref[0], gu_ref[1])[0].astype(a_ref.dtype)

    act = pl.pallas_call(act_body, name="seg_act", grid=(NDEV, seq // ft), in_specs=[pair_spec], out_specs=one_spec,
                         out_shape=sds((NDEV, seq, ns_ff), BF16), compiler_params=_params(2))(gu4)
    bn_d = _tile(d, BN)
    w_down_g = _exchange_finish(h_w_down, act)
    ffn = _mm("mm_down", act, w_down_g, NN, (seq // bm, d // bn_d, NDEV),
              pl.BlockSpec((None, bm, ns_ff), lambda i, j, k: (k, i, 0)), pl.BlockSpec((None, ns_ff, bn_d), lambda i, j, k: (k, 0, j)),
              pl.BlockSpec((bm, bn_d), lambda i, j, k: (i, j)), (seq, d), F32, (bm, bn_d))
    dy, dffn, d_gate_f, loss_part = _loss_head(h1, ffn, tgt, gate_f, rt)
    loss = lax.psum(loss_part[0, 0], MESH_AXES)

    tok_tile = lambda: jnp.broadcast_to(order_tok[0], (SUBLANES, LANES))
    bl = _tile(seq, BK)
    gw_down = _mm("mm_dw_down", act, dffn, TN, (NDEV, d // bn_d, seq // bl),
                  pl.BlockSpec((None, bl, ns_ff), lambda i, j, k: (i, k, 0)), pl.BlockSpec((bl, bn_d), lambda i, j, k: (k, j)),
                  pl.BlockSpec((None, ns_ff, bn_d), lambda i, j, k: (i, 0, j)), (NDEV, ns_ff, d), BF16, (ns_ff, bn_d))
    h_g_down = start(gw_down, False, "comm_a2a_w_down")
    dact = _mm("mm_dact", dffn, w_down_g, NT, (seq // bm, NDEV, d // bk),
               pl.BlockSpec((bm, bk), lambda i, j, k: (i, k)), pl.BlockSpec((None, ns_ff, bk), lambda i, j, k: (j, 0, k)),
               pl.BlockSpec((None, bm, ns_ff), lambda i, j, k: (j, i, 0)), (NDEV, seq, ns_ff), F32, (bm, ns_ff),
               behind=tok_tile())

    def dact_body(gu_ref, da_ref, dgu_ref):
        _, vjp = jax.vjp(_seg_act, gu_ref[0], gu_ref[1])
        dg, du_ = vjp((da_ref[...],))
        dgu_ref[0] = dg.astype(dgu_ref.dtype)
        dgu_ref[1] = du_.astype(dgu_ref.dtype)

    dgu4 = pl.pallas_call(dact_body, name="seg_act_bwd", grid=(NDEV, seq // ft), in_specs=[pair_spec, one_spec],
                          out_specs=pair_spec, out_shape=sds((NDEV, 2, seq, ns_ff), BF16), compiler_params=_params(2))(gu4, dact)
    dgu = dgu4.reshape(2 * NDEV, seq, ns_ff)
    bmd = _tile(d, BM)
    gw_gu = _mm("mm_dw_gu", xf, dgu, TN, (d // bmd, 2 * NDEV, seq // bl),
                pl.BlockSpec((bl, bmd), lambda i, j, k: (k, i)), pl.BlockSpec((None, bl, ns_ff), lambda i, j, k: (j, k, 0)),
                pl.BlockSpec((None, bmd, ns_ff), lambda i, j, k: (j, i, 0)), (2 * NDEV, d, ns_ff), BF16, (bmd, ns_ff))
    h_g_gu = start(gw_gu.reshape(NDEV, 2, d, ns_ff), False, "comm_a2a_w_gu")
    dxf = _mm("mm_dxf", dgu, w_gu_g, NT, (seq // bm, d // bn_d, 2 * NDEV),
              pl.BlockSpec((None, bm, ns_ff), lambda i, j, k: (k, i, 0)), pl.BlockSpec((None, bn_d, ns_ff), lambda i, j, k: (k, j, 0)),
              pl.BlockSpec((bm, bn_d), lambda i, j, k: (i, j)), (seq, d), F32, (bm, bn_d), behind=tok_tile())
    (do, dx_a, d_gate_m, d_g_ffn, d_scale_f, d_shift_f) = _rowwise_bwd(
        "seg_mid_bwd", _seg_mid, [xs, o], [row_d] * 2, [gate_m, g_ffn, scale_f, shift_f], [vec_d] * 4,
        [dy, dxf], [row_d] * 2, [[0], [1]], [1, 0], [sds((seq, d), BF16), sds((seq, d), F32)], [row_d] * 2,
        [0, 1, 2, 3], [sds((1, d), F32)] * 4, [vec_d] * 4, n_rt)

    dmixed = _mm_plain("mm_dmixed", do, w_out_g, NT, F32)
    gw_out = _mm_plain("mm_dw_out", mixed, do, TN, BF16)
    h_g_out = start(gw_out.reshape(NDEV, w_out.shape[1], d), False, "comm_a2a_w_out")
    (dz, dy1_a, dy_attn, d_g_ssm, d_g_attn) = _rowwise_bwd(
        "seg_mix_bwd", _seg_mix, [y1, z, y_attn], [row_s, row_s, row_a], [behind(g_ssm_out), g_attn_out], [vec_s, vec_a],
        [dmixed], [row_m], [[0]], [1, 0, 2], [sds((seq, w_ssm), BF16), sds((seq, w_ssm), F32), sds((seq, w_attn), F32)],
        [row_s, row_s, row_a], [0, 1], [sds((1, w_ssm), F32), sds((1, w_attn), F32)], [vec_s, vec_a], n_rt)
    dy1_b = _mm_plain("mm_dy1", dz, w_glu_g, NT, F32)
    gw_glu = _mm_plain("mm_dw_glu", y1, dz, TN, BF16)
    h_g_glu = start(gw_glu.reshape(NDEV, w_glu.shape[1], w_ssm), False, "comm_a2a_w_glu")
    (dypre,) = _rowwise_bwd("seg_gelu_bwd", _seg_gelu, [ypre], [row_s], [], [], [dy1_a, dy1_b], [row_s] * 2, [[0, 1]],
                            [0], [sds((seq, w_ssm), F32)], [row_s], [], [], [], n_rt)
    (du, db_blk_r, db_blk_i, dc_blk_r, dc_blk_i, dlam_r3, dlam_i3, dd_skip2) = _s5_backward(
        dypre, proj, x_re, x_im, b_blk_r, b_blk_i, c_blk_r, c_blk_i, lam_r3, lam_i3, behind(d_skip2), n_blk, t_chunk)
    dqh, dkh, dv = _attention_backward(qh, kh, proj, v_col, y_attn, dy_attn, n_pair, tq, tk)
    (dq, dk, dqg_t, dkg_t) = _rowwise_bwd(
        "seg_qk_bwd", _seg_qk, qk_rows, qk_specs, [qg_t, kg_t], [vec_a] * 2, [dqh, dkh], [row_a] * 2, [[0], [1]],
        [0, 1], [sds((seq, w_attn), BF16)] * 2, [row_a] * 2, [0, 1], [sds((1, w_attn), F32)] * 2, [vec_a] * 2, n_rt)
    dproj = jnp.concatenate([du.astype(BF16), dq, dk, dv.astype(BF16)], axis=-1)
    bk_in = _tile(ns_in, BK)
    per_k = ns_in // bk_in
    gw_in = _mm("mm_dw_in", xm, dproj, TN, (d // bmd, NDEV * per, seq // bl),
                pl.BlockSpec((bl, bmd), lambda i, j, k: (k, i)), pl.BlockSpec((bl, bn_in), lambda i, j, k: (k, j)),
                pl.BlockSpec((None, bmd, bn_in), lambda i, j, k: (j // per, i, j % per)), (NDEV, d, ns_in), BF16, (bmd, bn_in))
    h_g_in = start(gw_in, False, "comm_a2a_w_in")
    dxm = _mm("mm_dxm", dproj, w_in_g, NT, (seq // bm, d // bn_d, NDEV * per_k),
              pl.BlockSpec((bm, bk_in), lambda i, j, k: (i, k)),
              pl.BlockSpec((None, bn_d, bk_in), lambda i, j, k: (k // per_k, j, k % per_k)),
              pl.BlockSpec((bm, bn_d), lambda i, j, k: (i, j)), (seq, d), F32, (bm, bn_d), behind=tok_tile())
    (grad_x, d_shift_m, d_scale_m, d_g_mix) = _rowwise_bwd(
        "seg_in_bwd", _seg_in, [xs], [row_d], [shift_m, scale_m, behind(g_mix)], [vec_d] * 3, [dxm, dx_a], [row_d] * 2, [[0], [1]],
        [0], [sds((seq, d), F32)], [row_d], [0, 1, 2], [sds((1, d), F32)] * 3, [vec_d] * 3, n_rt)

    dbb_r = _from_b_blocks(db_blk_r, n_blk, p_state).reshape(gp, SSM_GROUP)
    dbb_i = _from_b_blocks(db_blk_i, n_blk, p_state).reshape(gp, SSM_GROUP)
    dcoef_r2, dcoef_i2, db_re2, db_im2 = _whole_vjp("s5_bbar_bwd", _s5_bbar, [coef_r2, coef_i2, b_re2, b_im2], [dbb_r, dbb_i],
                                                    [(gp, 1), (gp, 1), (gp, SSM_GROUP), (gp, SSM_GROUP)])
    lam_cts = [dlam_r3.reshape(n_groups, p_state), dlam_i3.reshape(n_groups, p_state),
               dcoef_r2.reshape(n_groups, p_state), dcoef_i2.reshape(n_groups, p_state)]
    da_re2, da_im2, dldt2 = _whole_vjp("s5_lam_bwd", _s5_lam, [a_re2, a_im2, ldt2], lam_cts,
                                       [(n_groups, p_state), (n_groups, p_state), (n_groups, 1)])
    dc_re2, dc_im2 = _from_c_blocks(dc_blk_r, n_blk, p_state), _from_c_blocks(dc_blk_i, n_blk, p_state)

    dmod = jnp.concatenate([d_shift_m, d_scale_m, d_gate_m, d_shift_f, d_scale_f, d_gate_f], axis=-1)
    small_part = {
        "b_ada": dmod, "g_mix": d_g_mix, "a_re": da_re2, "a_im": da_im2, "log_dt": dldt2, "b_re": db_re2, "b_im": db_im2,
        "c_re": dc_re2, "c_im": dc_im2, "d_skip": dd_skip2,
        "q_gain": dqg_t.reshape(n_heads, HEAD_DIM).sum(0), "k_gain": dkg_t.reshape(n_heads, HEAD_DIM).sum(0),
        "g_ssm_out": d_g_ssm, "g_attn_out": d_g_attn, "g_ffn": d_g_ffn,
    }
    h_small = start(_pack([small_part[n] for n in SMALL]), True, "comm_ag_small")

    big = {}
    after = jnp.broadcast_to(order_tok[0], (SUBLANES, LANES))

    def sharded(nm, handle, part_block, index_map, tr, after):
        got = _exchange_finish(handle, after)
        big[nm] = _adam_sum("adam_" + nm, got, pl.BlockSpec(part_block, index_map), given[nm][0], given["m_" + nm][0],
                            given["v_" + nm][0], tr)
        return got

    tr = _tile(w_down.shape[1], 64)
    sharded("w_down", h_g_down, (NDEV, tr, d), lambda i: (0, i, 0), tr, after)
    tr = _tile(d, 256)
    got_gu = sharded("w_gate", h_g_gu, (NDEV, None, tr, ns_ff), lambda i: (0, 0, i, 0), tr, big["w_down"][1])
    big["w_up"] = _adam_sum("adam_w_up", got_gu, pl.BlockSpec((NDEV, None, tr, ns_ff), lambda i: (0, 1, i, 0)),
                            w_up[0], m_w_up[0], v_w_up[0], tr)
    tr = _tile(w_out.shape[1], 128)
    sharded("w_out", h_g_out, (NDEV, tr, d), lambda i: (0, i, 0), tr, big["w_up"][1])
    tr = _tile(w_glu.shape[1], 128)
    sharded("w_glu", h_g_glu, (NDEV, tr, w_ssm), lambda i: (0, i, 0), tr, big["w_out"][1])
    tr = _tile(d, 256)
    sharded("w_in", h_g_in, (NDEV, tr, ns_in), lambda i: (0, i, 0), tr, big["w_glu"][1])

    packed_parts = _exchange_finish(h_small, big["w_in"][1])
    rows_p = packed_parts.shape[1]
    tr_p = _tile(rows_p, 64)
    small_spec = pl.BlockSpec((NDEV, tr_p, PACK_COLS), lambda i: (0, i, 0))
    sm = _adam_sum("adam_small", packed_parts, small_spec, _pack([given[n] for n in SMALL]),
                   _pack([given["m_" + n] for n in SMALL]), _pack([given["v_" + n] for n in SMALL]), tr_p)
    like = [given[n] for n in SMALL]
    small_out = [dict(zip(SMALL, _unpack(t, like))) for t in sm]

    dmod_all = packed_parts[:, :(6 * d) // PACK_COLS, :].reshape(NDEV, 6 * d) if (6 * d) % PACK_COLS == 0 else None
    assert dmod_all is not None
    dmod_cols = lax.dynamic_slice(dmod_all, (0, me * n_ada), (NDEV, n_ada))
    big["w_ada"] = _adam_ada(c_all, dmod_cols, w_ada[0], m_w_ada[0], v_w_ada[0])

    order = ("w_ada", "b_ada", "g_mix", "w_in", "a_re", "a_im", "log_dt", "b_re", "b_im", "c_re", "c_im", "d_skip", "w_glu",
             "q_gain", "k_gain", "g_ssm_out", "g_attn_out", "w_out", "g_ffn", "w_gate", "w_up", "w_down")
    outs = [loss, grad_x[None]]
    for kind in range(4):
        for n in order:
            outs.append(big[n][kind][None] if n in big else small_out[kind][n])
    return tuple(outs)
```

```python
import functools
import math

import jax
import jax.numpy as jnp
from jax import lax
from jax.experimental import pallas as pl
from jax.experimental.pallas import tpu as pltpu

F32 = jnp.float32
BF16 = jnp.bfloat16
NDEV = 8
MESH_AXES = ("x", "y", "c")
MESH_ID = pl.DeviceIdType.MESH
EPS = 1e-6
LANES = 128
SUBLANES = 8
HEAD_DIM = 64
SSM_GROUP = 16
ADAM_LR, ADAM_B1, ADAM_B2, ADAM_EPS, ADAM_WD, ADAM_STEP = 0.001, 0.9, 0.999, 1e-08, 0.01, 10

NN = (((1,), (0,)), ((), ()))
NT = (((1,), (1,)), ((), ()))
TN = (((0,), (0,)), ((), ()))


def _dot(a, b, dn=NN):
    return lax.dot_general(a, b, dn, preferred_element_type=F32)


def _tile(dim, pref):
    t = min(dim, pref)
    while dim % t:
        t //= 2
    return t


def _params(n):
    return pltpu.CompilerParams(dimension_semantics=("arbitrary",) * n)


def _me():
    mx, my, mc = lax.axis_index("x"), lax.axis_index("y"), lax.axis_index("c")
    return mx, my, mc


def _peer(mx, my, mc, k):
    px = 1 - mx if (k >> 2) & 1 else mx
    py = 1 - my if (k >> 1) & 1 else my
    pc = 1 - mc if k & 1 else mc
    return (px, py, pc), 4 * px + 2 * py + pc


def _exchange_copies(x_ref, land_ref, send_sems, recv_sems, gather):
    mx, my, mc = _me()
    me = 4 * mx + 2 * my + mc
    pairs = []
    for k in range(1, NDEV):
        peer, pidx = _peer(mx, my, mc, k)
        src = x_ref if gather else x_ref.at[pidx]
        mk = lambda dst, src=src, k=k, peer=peer: pltpu.make_async_remote_copy(
            src_ref=src, dst_ref=dst, send_sem=send_sems.at[k - 1], recv_sem=recv_sems.at[k - 1],
            device_id=peer, device_id_type=MESH_ID)
        pairs.append((mk(land_ref.at[me]), mk(land_ref.at[pidx])))
    return me, pairs


def _exchange(x, gather, name):
    def body(x_ref, o_ref, send_sems, recv_sems, local_sem):
        me, pairs = _exchange_copies(x_ref, o_ref, send_sems, recv_sems, gather)
        local = pltpu.make_async_copy(x_ref if gather else x_ref.at[me], o_ref.at[me], local_sem)
        local.start()
        for send, _ in pairs:
            send.start()
        for _, arrival in pairs:
            arrival.wait_recv()
        for send, _ in pairs:
            send.wait_send()
        local.wait()

    return pl.pallas_call(
        body, name=name, out_shape=jax.ShapeDtypeStruct(((NDEV,) + x.shape) if gather else x.shape, x.dtype),
        in_specs=[pl.BlockSpec(memory_space=pl.ANY)], out_specs=pl.BlockSpec(memory_space=pl.ANY),
        scratch_shapes=[pltpu.SemaphoreType.DMA((NDEV - 1,)), pltpu.SemaphoreType.DMA((NDEV - 1,)), pltpu.SemaphoreType.DMA],
    )(x)


class _Rider:
    def __init__(self, x, gather, land=None, slot=None, n_slots=None):
        self.x, self.gather, self.land, self.slot = x, gather, land, slot
        own = ((NDEV,) + x.shape) if gather else x.shape
        self.land_shape = land.shape if land is not None else (own if n_slots is None else (n_slots,) + own)

    def zone(self, land_ref):
        return land_ref if self.slot is None else land_ref.at[self.slot]

    def copies(self, x_ref, land_ref, send_sems, recv_sems, local_sem):
        zone = self.zone(land_ref)
        me, pairs = _exchange_copies(x_ref, zone, send_sems, recv_sems, self.gather)
        return pairs, pltpu.make_async_copy(x_ref if self.gather else x_ref.at[me], zone.at[me], local_sem)


def _ride(call_name, grid, riders, inner, in_specs, out_specs, out_shape, scratch_shapes, compiler_params, operands):
    n_in, n_out, n_scr = len(in_specs), len(out_specs), len(scratch_shapes)
    any_spec = pl.BlockSpec(memory_space=pl.ANY)
    extra_in, aliases = [], {}
    for r_idx, r in enumerate(riders):
        extra_in.append(r.x)
        if r.land is not None:
            aliases[n_in + len(extra_in)] = n_out + r_idx
            extra_in.append(r.land)
    sems = []
    for _ in riders:
        sems += [pltpu.SemaphoreType.DMA((NDEV - 1,)), pltpu.SemaphoreType.DMA((NDEV - 1,)), pltpu.SemaphoreType.DMA]

    def body(*refs):
        base_in, rest = refs[:n_in], refs[n_in:]
        rider_in, rest = rest[:len(extra_in)], rest[len(extra_in):]
        base_out, rest = rest[:n_out], rest[n_out:]
        lands, rest = rest[:len(riders)], rest[len(riders):]
        base_scr, rider_sems = rest[:n_scr], rest[n_scr:]
        ids = [pl.program_id(a) for a in range(len(grid))]
        first = functools.reduce(jnp.logical_and, [i == 0 for i in ids])
        last = functools.reduce(jnp.logical_and, [i == g - 1 for i, g in zip(ids, grid)])
        sets, pos = [], 0
        for r_idx, r in enumerate(riders):
            x_ref = rider_in[pos]
            pos += 2 if r.land is not None else 1
            sets.append(r.copies(x_ref, lands[r_idx], *rider_sems[3 * r_idx:3 * r_idx + 3]))

        if sets:
            @pl.when(first)
            def _():
                for pairs, local in sets:
                    for send, _ in pairs:
                        send.start()
                    local.start()

        inner(*base_in, *base_out, *base_scr)

        if sets:
            @pl.when(last)
            def _():
                for pairs, local in sets:
                    for send, arrival in pairs:
                        send.wait_send()
                        arrival.wait_recv()
                    local.wait()

    outs = pl.pallas_call(
        body, name=call_name, grid=grid, in_specs=list(in_specs) + [any_spec] * len(extra_in),
        out_specs=list(out_specs) + [any_spec] * len(riders),
        out_shape=list(out_shape) + [jax.ShapeDtypeStruct(r.land_shape, r.x.dtype) for r in riders],
        scratch_shapes=list(scratch_shapes) + sems, input_output_aliases=aliases, compiler_params=compiler_params,
    )(*operands, *extra_in)
    return outs[:n_out], outs[n_out:]


def _mm(name, a, b, dn, grid, a_spec, b_spec, o_spec, out_shape, out_dtype, acc_shape, riders=()):
    nk = grid[2]

    def body(a_ref, b_ref, o_ref, *scratch):
        part = _dot(a_ref[...].astype(BF16), b_ref[...].astype(BF16), dn)
        if nk == 1:
            o_ref[...] = part.astype(o_ref.dtype)
            return
        acc_ref = scratch[0]
        k = pl.program_id(2)

        @pl.when(k == 0)
        def _():
            acc_ref[...] = part

        @pl.when(k > 0)
        def _():
            acc_ref[...] += part

        @pl.when(k == nk - 1)
        def _():
            o_ref[...] = acc_ref[...].astype(o_ref.dtype)

    (out,), lands = _ride(name, grid, riders, body, [a_spec, b_spec], [o_spec], [jax.ShapeDtypeStruct(out_shape, out_dtype)],
                          [] if nk == 1 else [pltpu.VMEM(acc_shape, F32)], _params(3), [a, b])
    return (out, *lands) if riders else out


BM, BN, BK = 1024, 1024, 4096


def _mm_plain(name, a, b, dn, out_dtype):
    if dn == NN:
        (m, kk), n = a.shape, b.shape[1]
    elif dn == NT:
        (m, kk), n = a.shape, b.shape[0]
    else:
        (kk, m), n = a.shape, b.shape[1]
    half = 2 if dn == TN else 1
    bm, bn, bk = _tile(m, BM // half), _tile(n, BN // half), _tile(kk, BK)
    a_spec = pl.BlockSpec((bk, bm), lambda i, j, k: (k, i)) if dn == TN else pl.BlockSpec((bm, bk), lambda i, j, k: (i, k))
    b_spec = pl.BlockSpec((bn, bk), lambda i, j, k: (j, k)) if dn == NT else pl.BlockSpec((bk, bn), lambda i, j, k: (k, j))
    return _mm(name, a, b, dn, (m // bm, n // bn, kk // bk), a_spec, b_spec,
               pl.BlockSpec((bm, bn), lambda i, j, k: (i, j)), (m, n), out_dtype, (bm, bn))


def _row_spec(tile, width, col=0):
    return pl.BlockSpec((tile, width), lambda i: (i, col))


def _vec_spec(width, col=0):
    return pl.BlockSpec((1, width), lambda i: (0, col))


def _rowwise_fwd(name, fn, rows, row_specs, vecs, vec_specs, out_shapes, out_specs, n_tiles):
    nr, nv = len(rows), len(vecs)

    def body(*refs):
        ins = [r[...].astype(F32) for r in refs[:nr + nv]]
        outs = fn(*ins)
        for o_ref, o in zip(refs[nr + nv:], outs):
            o_ref[...] = o.astype(o_ref.dtype)

    return pl.pallas_call(body, name=name, grid=(n_tiles,), in_specs=list(row_specs) + list(vec_specs),
                          out_specs=list(out_specs), out_shape=list(out_shapes), compiler_params=_params(1))(*rows, *vecs)


def _rowwise_bwd(name, fn, rows, row_specs, vecs, vec_specs, cts, ct_specs, ct_groups,
                 drow_idx, drow_shapes, drow_specs, dvec_idx, dvec_shapes, dvec_specs, n_tiles):
    nr, nv, nc = len(rows), len(vecs), len(cts)

    def body(*refs):
        ins = [r[...].astype(F32) for r in refs[:nr + nv]]
        ct_vals = [r[...].astype(F32) for r in refs[nr + nv:nr + nv + nc]]
        out_refs = refs[nr + nv + nc:]
        _, vjp = jax.vjp(fn, *ins)
        grads = vjp(tuple(functools.reduce(lambda p, q: p + q, [ct_vals[j] for j in grp]) for grp in ct_groups))
        for o_ref, idx in zip(out_refs[:len(drow_idx)], drow_idx):
            o_ref[...] = grads[idx].astype(o_ref.dtype)
        step = pl.program_id(0)
        for o_ref, idx in zip(out_refs[len(drow_idx):], dvec_idx):
            @pl.when(step == 0)
            def _(o_ref=o_ref):
                o_ref[...] = jnp.zeros_like(o_ref)
            o_ref[...] += grads[nr + idx]

    return pl.pallas_call(body, name=name, grid=(n_tiles,),
                          in_specs=list(row_specs) + list(vec_specs) + list(ct_specs),
                          out_specs=list(drow_specs) + list(dvec_specs),
                          out_shape=list(drow_shapes) + list(dvec_shapes), compiler_params=_params(1))(*rows, *vecs, *cts)


def _rms(x):
    return x * lax.rsqrt(jnp.mean(x * x, axis=-1, keepdims=True) + EPS)


def _seg_in(x, shift, scale, gain):
    return _rms(x) * gain * (1.0 + scale) + shift, x


def _seg_qk(q, k, qg, kg):
    def norm(t, g, mult):
        blocks = []
        lane = lax.broadcasted_iota(jnp.int32, (1, LANES), 1)
        for p in range(t.shape[1] // LANES):
            tb = t[:, p * LANES:(p + 1) * LANES]
            sq = tb * tb
            lo = jnp.sum(jnp.where(lane < HEAD_DIM, sq, 0.0), axis=-1, keepdims=True)
            hi = jnp.sum(jnp.where(lane < HEAD_DIM, 0.0, sq), axis=-1, keepdims=True)
            ms = jnp.where(lane < HEAD_DIM, lo, hi) * (1.0 / HEAD_DIM)
            blocks.append(tb * lax.rsqrt(ms + EPS) * (g[:, p * LANES:(p + 1) * LANES] * mult))
        return jnp.concatenate(blocks, axis=-1) if len(blocks) > 1 else blocks[0]
    return norm(q, qg, 1.0 / math.sqrt(HEAD_DIM)), norm(k, kg, 1.0)


def _seg_gelu(ypre):
    return (jax.nn.gelu(ypre),)


def _seg_mix(y1, z, yattn, g_ssm, g_attn):
    ys = y1 * jax.nn.sigmoid(z)
    return (jnp.concatenate([_rms(ys) * g_ssm, _rms(yattn) * g_attn], axis=-1),)


def _seg_mid(x, o, gate_m, g_ffn, scale_f, shift_f):
    h1 = x + gate_m * o
    return h1, _rms(h1) * g_ffn * (1.0 + scale_f) + shift_f


def _seg_act(gate, up):
    return (jax.nn.silu(gate) * up,)


def _s5_lam(a_re, a_im, log_dt):
    dt = jnp.exp(log_dt)
    mag = jnp.exp(a_re * dt)
    lr, li = mag * jnp.cos(a_im * dt), mag * jnp.sin(a_im * dt)
    den = a_re * a_re + a_im * a_im
    nr, ni = lr - 1.0, li
    return lr, li, (nr * a_re + ni * a_im) / den, (ni * a_re - nr * a_im) / den


def _s5_bbar(coef_re, coef_im, b_re, b_im):
    return coef_re * b_re - coef_im * b_im, coef_re * b_im + coef_im * b_re


def _whole(name, fn, ins, out_shapes):
    n = len(ins)

    def body(*refs):
        outs = fn(*[r[...] for r in refs[:n]])
        for o_ref, o in zip(refs[n:], outs):
            o_ref[...] = o

    return pl.pallas_call(body, name=name, out_shape=[jax.ShapeDtypeStruct(s, F32) for s in out_shapes])(*ins)


def _whole_vjp(name, fn, ins, cts, out_shapes):
    n, nc = len(ins), len(cts)

    def body(*refs):
        _, vjp = jax.vjp(fn, *[r[...] for r in refs[:n]])
        grads = vjp(tuple(r[...] for r in refs[n:n + nc]))
        for o_ref, g in zip(refs[n + nc:], grads):
            o_ref[...] = g

    return pl.pallas_call(body, name=name, out_shape=[jax.ShapeDtypeStruct(s, F32) for s in out_shapes])(*ins, *cts)


SCAN_SHIFTS = (1, 2, 4)


def _cmul(ar, ai, br, bi):
    return ar * br - ai * bi, ar * bi + ai * br


def _scan_coefs(lr, li, reverse):
    s = lr.shape[1]
    row = lax.broadcasted_iota(jnp.int32, (SUBLANES, s), 0)
    p1 = (lr, li)
    p2 = _cmul(*p1, *p1)
    p4 = _cmul(*p2, *p2)
    p8 = _cmul(*p4, *p4)
    p3, p5, p6 = _cmul(*p1, *p2), _cmul(*p4, *p1), _cmul(*p4, *p2)
    p7 = _cmul(*p6, *p1)
    pows = (p1, p2, p3, p4, p5, p6, p7, p8)
    bc = lambda t: jnp.broadcast_to(t, (SUBLANES, s))
    steps = []
    for sh, pw in zip(SCAN_SHIFTS, (p1, p2, p4)):
        keep = (row + sh <= SUBLANES - 1) if reverse else (row >= sh)
        steps.append((jnp.where(keep, bc(pw[0]), 0.0), jnp.where(keep, bc(pw[1]), 0.0)))
    cr, ci = jnp.zeros((SUBLANES, s), F32), jnp.zeros((SUBLANES, s), F32)
    for r in range(SUBLANES):
        pw = pows[SUBLANES - 1 - r] if reverse else pows[r]
        cr = jnp.where(row == r, bc(pw[0]), cr)
        ci = jnp.where(row == r, bc(pw[1]), ci)
    return steps, (cr, ci)


def _scan_tile(xr, xi, steps, carry_pow, cr, ci, reverse):
    for sh, (ar, ai) in zip(SCAN_SHIFTS, steps):
        rs = SUBLANES - sh if reverse else sh
        sr, si = pltpu.roll(xr, rs, 0), pltpu.roll(xi, rs, 0)
        xr, xi = xr + ar * sr - ai * si, xi + ar * si + ai * sr
    pr, pi = carry_pow
    return xr + pr * cr - pi * ci, xi + pr * ci + pi * cr


def _s5_forward(proj, b_blk_re, b_blk_im, c_blk_re, c_blk_im, lam_re, lam_im, d_skip, n_blk, t_chunk, riders=()):
    seq = proj.shape[0]
    n_chunks = seq // t_chunk
    n_tiles = t_chunk // SUBLANES
    s = b_blk_re.shape[2]

    def body(u_ref, bre_ref, bim_ref, cre_ref, cim_ref, lr_ref, li_ref, d_ref, y_ref, xr_ref, xi_ref, wr, wi, carry):
        t = pl.program_id(1)

        @pl.when(t == 0)
        def _():
            carry[...] = jnp.zeros_like(carry)

        u = u_ref[...]
        ub = u.astype(BF16)
        wr[...] = _dot(ub, bre_ref[...])
        wi[...] = _dot(ub, bim_ref[...])
        steps, cpow = _scan_coefs(lr_ref[...], li_ref[...], False)

        def tile(i, c):
            r0 = pl.multiple_of(i * SUBLANES, SUBLANES)
            xr, xi = _scan_tile(wr[pl.ds(r0, SUBLANES), :], wi[pl.ds(r0, SUBLANES), :], steps, cpow, c[0], c[1], False)
            xr_ref[pl.ds(r0, SUBLANES), :] = xr
            xi_ref[pl.ds(r0, SUBLANES), :] = xi
            last = SUBLANES - 1
            return (jnp.broadcast_to(xr[last:, :], xr.shape), jnp.broadcast_to(xi[last:, :], xi.shape))

        cr, ci = lax.fori_loop(0, n_tiles, tile, (carry[0], carry[1]))
        carry[0] = cr
        carry[1] = ci
        y = _dot(xr_ref[...].astype(BF16), cre_ref[...]) - _dot(xi_ref[...].astype(BF16), cim_ref[...])
        y_ref[...] = y + d_ref[...] * u

    blk = lambda shape: pl.BlockSpec((None,) + shape, lambda j, t: (j, 0, 0))
    return _ride(
        "s5_fwd", (n_blk, n_chunks), riders, body,
        [pl.BlockSpec((t_chunk, LANES), lambda j, t: (t, j)), blk((LANES, s)), blk((LANES, s)),
         blk((s, LANES)), blk((s, LANES)), blk((1, s)), blk((1, s)), pl.BlockSpec((1, LANES), lambda j, t: (0, j))],
        [pl.BlockSpec((t_chunk, LANES), lambda j, t: (t, j)), pl.BlockSpec((t_chunk, s), lambda j, t: (t, j)),
         pl.BlockSpec((t_chunk, s), lambda j, t: (t, j))],
        [jax.ShapeDtypeStruct((seq, n_blk * LANES), F32), jax.ShapeDtypeStruct((seq, n_blk * s), F32),
         jax.ShapeDtypeStruct((seq, n_blk * s), F32)],
        [pltpu.VMEM((t_chunk, s), F32), pltpu.VMEM((t_chunk, s), F32), pltpu.VMEM((2, SUBLANES, s), F32)],
        _params(2), [proj, b_blk_re, b_blk_im, c_blk_re, c_blk_im, lam_re, lam_im, d_skip])


def _s5_backward(dypre, proj, x_re, x_im, b_blk_re, b_blk_im, c_blk_re, c_blk_im, lam_re, lam_im, d_skip, n_blk, t_chunk,
                 riders=()):
    seq = proj.shape[0]
    n_chunks = seq // t_chunk
    n_tiles = t_chunk // SUBLANES
    s = b_blk_re.shape[2]

    def body(dy_ref, u_ref, xr_ref, xi_ref, pr_ref, pi_ref, bre_ref, bim_ref, cre_ref, cim_ref, lr_ref, li_ref, d_ref,
             du_ref, dbre_ref, dbim_ref, dcre_ref, dcim_ref, dlr_ref, dli_ref, dd_ref, gr, gi, carry):
        t = pl.program_id(1)

        @pl.when(t == 0)
        def _():
            carry[...] = jnp.zeros_like(carry)
            for r in (dbre_ref, dbim_ref, dcre_ref, dcim_ref, dlr_ref, dli_ref, dd_ref):
                r[...] = jnp.zeros_like(r)

        dy = dy_ref[...]
        dyb = dy.astype(BF16)
        u = u_ref[...]
        gr[...] = _dot(dyb, cre_ref[...], NT)
        gi[...] = -_dot(dyb, cim_ref[...], NT)
        steps, cpow = _scan_coefs(lr_ref[...], -li_ref[...], True)
        row = lax.broadcasted_iota(jnp.int32, (SUBLANES, s), 0)
        last = SUBLANES - 1
        first_chunk = t == n_chunks - 1

        def tile_at(r0, prev_r, prev_i, c):
            cr, ci, ar, ai = c
            lr_, li_ = _scan_tile(gr[pl.ds(r0, SUBLANES), :], gi[pl.ds(r0, SUBLANES), :], steps, cpow, cr, ci, True)
            gr[pl.ds(r0, SUBLANES), :] = lr_
            gi[pl.ds(r0, SUBLANES), :] = li_
            xr, xi = xr_ref[pl.ds(r0, SUBLANES), :], xi_ref[pl.ds(r0, SUBLANES), :]
            xpr = jnp.where(row == 0, jnp.broadcast_to(prev_r[last:, :], xr.shape), pltpu.roll(xr, 1, 0))
            xpi = jnp.where(row == 0, jnp.broadcast_to(prev_i[last:, :], xi.shape), pltpu.roll(xi, 1, 0))
            ar = ar + lr_ * xpr + li_ * xpi
            ai = ai + li_ * xpr - lr_ * xpi
            return (jnp.broadcast_to(lr_[:1, :], lr_.shape), jnp.broadcast_to(li_[:1, :], li_.shape), ar, ai)

        def tile(ii, c):
            i = n_tiles - 1 - ii
            r0 = pl.multiple_of(i * SUBLANES, SUBLANES)
            rp = pl.multiple_of(r0 - SUBLANES, SUBLANES)
            return tile_at(r0, xr_ref[pl.ds(rp, SUBLANES), :], xi_ref[pl.ds(rp, SUBLANES), :], c)

        zero = jnp.zeros((SUBLANES, s), F32)
        c = lax.fori_loop(0, n_tiles - 1, tile, (carry[0], carry[1], zero, zero))
        keep = jnp.where(first_chunk, 0.0, 1.0)
        c = tile_at(0, pr_ref[...] * keep, pi_ref[...] * keep, c)
        carry[0] = c[0]
        carry[1] = c[1]
        dlr_ref[...] += jnp.sum(c[2], axis=0, keepdims=True)
        dli_ref[...] += jnp.sum(c[3], axis=0, keepdims=True)

        lam_r, lam_i = gr[...].astype(BF16), gi[...].astype(BF16)
        du_ref[...] = _dot(lam_r, bre_ref[...], NT) + _dot(lam_i, bim_ref[...], NT) + d_ref[...] * dy
        ub = u.astype(BF16)
        dbre_ref[...] += _dot(ub, lam_r, TN)
        dbim_ref[...] += _dot(ub, lam_i, TN)
        dcre_ref[...] += _dot(xr_ref[...].astype(BF16), dyb, TN)
        dcim_ref[...] -= _dot(xi_ref[...].astype(BF16), dyb, TN)
        dd_ref[...] += jnp.sum(dy * u, axis=0, keepdims=True)

    rev = lambda t: n_chunks - 1 - t
    blk = lambda shape: pl.BlockSpec((None,) + shape, lambda j, t: (j, 0, 0))
    tpc = t_chunk // SUBLANES
    prev_spec = pl.BlockSpec((SUBLANES, s), lambda j, t: (jnp.maximum(rev(t) * tpc - 1, 0), j))
    chunk = lambda w: pl.BlockSpec((t_chunk, w), lambda j, t: (rev(t), j))
    return _ride(
        "s5_bwd", (n_blk, n_chunks), riders, body,
        [chunk(LANES), chunk(LANES), chunk(s), chunk(s), prev_spec, prev_spec, blk((LANES, s)), blk((LANES, s)),
         blk((s, LANES)), blk((s, LANES)), blk((1, s)), blk((1, s)), pl.BlockSpec((1, LANES), lambda j, t: (0, j))],
        [chunk(LANES), blk((LANES, s)), blk((LANES, s)), blk((s, LANES)), blk((s, LANES)), blk((1, s)), blk((1, s)),
         pl.BlockSpec((1, LANES), lambda j, t: (0, j))],
        [jax.ShapeDtypeStruct((seq, n_blk * LANES), F32),
         jax.ShapeDtypeStruct((n_blk, LANES, s), F32), jax.ShapeDtypeStruct((n_blk, LANES, s), F32),
         jax.ShapeDtypeStruct((n_blk, s, LANES), F32), jax.ShapeDtypeStruct((n_blk, s, LANES), F32),
         jax.ShapeDtypeStruct((n_blk, 1, s), F32), jax.ShapeDtypeStruct((n_blk, 1, s), F32),
         jax.ShapeDtypeStruct((1, n_blk * LANES), F32)],
        [pltpu.VMEM((t_chunk, s), F32), pltpu.VMEM((t_chunk, s), F32), pltpu.VMEM((2, SUBLANES, s), F32)],
        _params(2), [dypre, proj, x_re, x_im, x_re, x_im, b_blk_re, b_blk_im, c_blk_re, c_blk_im, lam_re, lam_im, d_skip])


TQ, TK = 256, 128


def _split_bf16(x):
    hi = x.astype(BF16)
    return hi, (x - hi.astype(F32)).astype(BF16)


def _sb_weights(z, past, carry, tri):
    ls = jnp.minimum(z, 0.0) - jnp.log(1.0 + jnp.exp(-jnp.abs(z)))
    lk = ls - z
    if past is not None:
        lk = jnp.where(past, lk, 0.0)
    hi, lo = _split_bf16(lk)
    w = jnp.exp(ls + (_dot(hi, tri) + _dot(lo, tri)) + carry)
    if past is not None:
        w = jnp.where(past, w, 0.0)
    return ls, lk, w


LOG_KEEP_DEAD = -104.0


def _walk_key_blocks(i, ratio, prologue, block, epilogue, log_keep):
    n_kb = (i + 1) * ratio
    prologue(n_kb - 1)
    for n in range(ratio):
        block(n_kb - 1 - n, n % 2, True)
    n_pairs = (i * ratio) // 2

    def more(state):
        t, alive = state
        return jnp.logical_and(t < n_pairs, alive)

    def pair(state):
        t, _ = state
        j = n_kb - 1 - ratio - 2 * t
        block(j, ratio % 2, False)
        block(j - 1, (ratio + 1) % 2, False)
        return t + 1, log_keep() >= LOG_KEEP_DEAD

    done, _ = lax.while_loop(more, pair, (jnp.int32(0), log_keep() >= LOG_KEEP_DEAD))
    epilogue(n_kb - ratio - 2 * done)


def _attention_forward(qh, kh, proj, v_col, n_pair, tq, tk, riders=()):
    seq = qh.shape[0]
    ratio = tq // tk
    assert ratio % 2 == 0

    def body(q_ref, k_ref, v_ref, o_ref, q_scr, z_scr, w_scr, acc_scr, c_scr):
        i = pl.program_id(1)
        lane = lax.broadcasted_iota(jnp.int32, (1, LANES), 1)
        q2 = q_ref[...]
        q_scr[0] = jnp.where(lane < HEAD_DIM, q2, 0.0).astype(BF16)
        q_scr[1] = jnp.where(lane < HEAD_DIM, 0.0, q2).astype(BF16)
        tri = (lax.broadcasted_iota(jnp.int32, (tk, tk), 0) > lax.broadcasted_iota(jnp.int32, (tk, tk), 1)).astype(BF16)
        qpos = i * tq + lax.broadcasted_iota(jnp.int32, (tq, tk), 0)
        kidx = lax.broadcasted_iota(jnp.int32, (tq, tk), 1)

        def rows(ref, j):
            j = jnp.clip(j, 0, seq // tk - 1)
            return ref[pl.ds(pl.multiple_of(j * tk, tk), tk), :].astype(BF16)

        def scores(j, slot):
            kb = rows(k_ref, j)
            for h in range(2):
                z_scr[slot, h] = _dot(q_scr[h], kb, NT)

        def finish(j):
            vb = rows(v_ref, j)
            for h in range(2):
                acc_scr[h] += _dot(w_scr[h], vb)

        def prologue(j):
            w_scr[...] = jnp.zeros_like(w_scr)
            acc_scr[...] = jnp.zeros_like(acc_scr)
            c_scr[...] = jnp.zeros_like(c_scr)
            scores(j, 0)

        def block(j, slot, masked):
            scores(j - 1, 1 - slot)
            finish(j + 1)
            past = ((kidx + j * tk) < qpos) if masked else None
            for h in range(2):
                _, lk, w = _sb_weights(z_scr[slot, h], past, c_scr[h], tri)
                w_scr[h] = w.astype(BF16)
                c_scr[h] += jnp.sum(lk, axis=-1, keepdims=True)

        _walk_key_blocks(i, ratio, prologue, block, finish, lambda: jnp.max(c_scr[...]))
        o_ref[...] = jnp.where(lane < HEAD_DIM, acc_scr[0], acc_scr[1])

    (out,), lands = _ride(
        "attn_fwd", (n_pair, seq // tq), riders, body,
        [pl.BlockSpec((tq, LANES), lambda p, i: (i, p)), pl.BlockSpec((seq, LANES), lambda p, i: (0, p)),
         pl.BlockSpec((seq, LANES), lambda p, i: (0, v_col + p))],
        [pl.BlockSpec((tq, LANES), lambda p, i: (i, p))], [jax.ShapeDtypeStruct(qh.shape, F32)],
        [pltpu.VMEM((2, tq, LANES), BF16), pltpu.VMEM((2, 2, tq, tk), F32), pltpu.VMEM((2, tq, tk), BF16),
         pltpu.VMEM((2, tq, LANES), F32), pltpu.VMEM((2, tq, 1), F32)],
        _params(2), [qh, kh, proj])
    return out, lands


def _attention_backward(qh, kh, proj, v_col, y, dy, n_pair, tq, tk):
    seq = qh.shape[0]

    ratio = tq // tk
    assert ratio % 2 == 0

    def body(q_ref, k_ref, v_ref, y_ref, dy_ref, dq_ref, dk_ref, dv_ref,
             q_scr, do_scr, z_scr, dw_scr, w_scr, dz_scr, dq_scr, c_scr, c2_scr, tot_scr):
        i = pl.program_id(1)

        @pl.when(i == 0)
        def _():
            dk_ref[...] = jnp.zeros_like(dk_ref)
            dv_ref[...] = jnp.zeros_like(dv_ref)

        lane = lax.broadcasted_iota(jnp.int32, (1, LANES), 1)
        sel = (lane < HEAD_DIM, lane >= HEAD_DIM)
        q2, do2 = q_ref[...], dy_ref[...].astype(BF16)
        dot_oy = do2.astype(F32) * y_ref[...]
        for h in range(2):
            q_scr[h] = jnp.where(sel[h], q2, 0.0).astype(BF16)
            do_scr[h] = jnp.where(sel[h], do2, jnp.zeros_like(do2))
            tot_scr[h] = jnp.sum(jnp.where(sel[h], dot_oy, 0.0), axis=-1, keepdims=True)
        r_i, c_i = lax.broadcasted_iota(jnp.int32, (tk, tk), 0), lax.broadcasted_iota(jnp.int32, (tk, tk), 1)
        tri = (r_i > c_i).astype(BF16)
        tri_ge = (r_i >= c_i).astype(BF16)
        qpos = i * tq + lax.broadcasted_iota(jnp.int32, (tq, tk), 0)
        kidx = lax.broadcasted_iota(jnp.int32, (tq, tk), 1)

        def start(j):
            return pl.multiple_of(jnp.clip(j, 0, seq // tk - 1) * tk, tk)

        def scores(j, slot):
            c0 = start(j)
            kb, vb = k_ref[pl.ds(c0, tk), :].astype(BF16), v_ref[pl.ds(c0, tk), :].astype(BF16)
            for h in range(2):
                z_scr[slot, h] = _dot(q_scr[h], kb, NT)
                dw_scr[slot, h] = _dot(do_scr[h], vb, NT)

        def finish(j):
            c0 = start(j)
            kb = k_ref[pl.ds(c0, tk), :].astype(BF16)
            dk_add, dv_add = jnp.zeros((tk, LANES), F32), jnp.zeros((tk, LANES), F32)
            for h in range(2):
                dz = dz_scr[h]
                dq_scr[h] += _dot(dz, kb)
                dk_add = dk_add + _dot(dz, q_scr[h], TN)
                dv_add = dv_add + _dot(w_scr[h], do_scr[h], TN)
            dk_ref[pl.ds(c0, tk), :] += dk_add
            dv_ref[pl.ds(c0, tk), :] += dv_add

        def prologue(j):
            for r in (w_scr, dz_scr, dq_scr, c_scr, c2_scr):
                r[...] = jnp.zeros_like(r)
            scores(j, 0)

        def block(j, slot, masked):
            scores(j - 1, 1 - slot)
            finish(j + 1)
            past = ((kidx + j * tk) < qpos) if masked else None
            for h in range(2):
                ls, lk, w = _sb_weights(z_scr[slot, h], past, c_scr[h], tri)
                wb = w.astype(BF16)
                dlw = dw_scr[slot, h] * wb.astype(F32)
                hi, lo = _split_bf16(dlw)
                dlk = tot_scr[h] - c2_scr[h] - (_dot(hi, tri_ge) + _dot(lo, tri_ge))
                if masked:
                    dlk = jnp.where(past, dlk, 0.0)
                sig = jnp.exp(ls)
                w_scr[h] = wb
                dz_scr[h] = (dlw * (1.0 - sig) - dlk * sig).astype(BF16)
                c_scr[h] += jnp.sum(lk, axis=-1, keepdims=True)
                c2_scr[h] += jnp.sum(dlw, axis=-1, keepdims=True)

        _walk_key_blocks(i, ratio, prologue, block, finish, lambda: jnp.max(c_scr[...]))
        dq_ref[...] = jnp.where(sel[0], dq_scr[0], dq_scr[1])

    blk = pl.BlockSpec((tq, LANES), lambda p, i: (i, p))
    full = pl.BlockSpec((seq, LANES), lambda p, i: (0, p))
    shape = jax.ShapeDtypeStruct(qh.shape, F32)
    return pl.pallas_call(
        body, name="attn_bwd", grid=(n_pair, seq // tq),
        in_specs=[blk, full, pl.BlockSpec((seq, LANES), lambda p, i: (0, v_col + p)), blk, blk],
        out_specs=[blk, full, full], out_shape=[shape, shape, shape],
        scratch_shapes=[pltpu.VMEM((2, tq, LANES), BF16), pltpu.VMEM((2, tq, LANES), BF16),
                        pltpu.VMEM((2, 2, tq, tk), F32), pltpu.VMEM((2, 2, tq, tk), F32),
                        pltpu.VMEM((2, tq, tk), BF16), pltpu.VMEM((2, tq, tk), BF16), pltpu.VMEM((2, tq, LANES), F32),
                        pltpu.VMEM((2, tq, 1), F32), pltpu.VMEM((2, tq, 1), F32), pltpu.VMEM((2, tq, 1), F32)],
        compiler_params=_params(2),
    )(qh, kh, proj, y, dy)


def _loss_head(h1, ffn, target, gate_f, tile):
    seq, d = h1.shape

    def body(h_ref, f_ref, t_ref, g_ref, dy_ref, df_ref, dg_ref, loss_ref):
        @pl.when(pl.program_id(0) == 0)
        def _():
            dg_ref[...] = jnp.zeros_like(dg_ref)
            loss_ref[...] = jnp.zeros_like(loss_ref)

        f, g = f_ref[...], g_ref[...]
        err = h_ref[...] + g * f - t_ref[...]
        dy = err * (1.0 / d)
        dy_ref[...] = dy
        df_ref[...] = (dy * g).astype(df_ref.dtype)
        dg_ref[...] += jnp.sum(dy * f, axis=0, keepdims=True)
        loss_ref[...] += jnp.sum(jnp.sum(err * err, axis=-1, keepdims=True), axis=0, keepdims=True) * (0.5 / d)

    row = _row_spec(tile, d)
    return pl.pallas_call(
        body, name="loss_head", grid=(seq // tile,), in_specs=[row, row, row, _vec_spec(d)],
        out_specs=[row, row, _vec_spec(d), pl.BlockSpec((1, 1), lambda i: (0, 0))],
        out_shape=[jax.ShapeDtypeStruct((seq, d), F32), jax.ShapeDtypeStruct((seq, d), BF16),
                   jax.ShapeDtypeStruct((1, d), F32), jax.ShapeDtypeStruct((1, 1), F32)],
        compiler_params=_params(1),
    )(h1, ffn, target, gate_f)


def _dot3(a, b, dn):
    ah, al = _split_bf16(a)
    bh, bl = _split_bf16(b)
    return _dot(ah, bh, dn) + (_dot(ah, bl, dn) + _dot(al, bh, dn))


def _ada_forward(c_all, w_shard, b_cols):
    d, n = w_shard.shape
    bk = _tile(d, 512)

    def body(c_ref, w_ref, b_ref, o_ref):
        @pl.when(pl.program_id(0) == 0)
        def _():
            o_ref[...] = jnp.broadcast_to(b_ref[...], o_ref.shape)

        o_ref[...] += _dot3(jax.nn.silu(c_ref[...]), w_ref[...], NN)

    return pl.pallas_call(
        body, name="ada_fwd", grid=(d // bk,),
        in_specs=[pl.BlockSpec((NDEV, bk), lambda k: (0, k)), pl.BlockSpec((bk, n), lambda k: (k, 0)), _vec_spec(n)],
        out_specs=pl.BlockSpec((NDEV, n), lambda k: (0, 0)), out_shape=jax.ShapeDtypeStruct((NDEV, n), F32),
        compiler_params=_params(1),
    )(c_all, w_shard, b_cols)


def _adam(w, g, m, v):
    m = ADAM_B1 * m + (1.0 - ADAM_B1) * g
    v = ADAM_B2 * v + (1.0 - ADAM_B2) * (g * g)
    m_hat = m / (1.0 - ADAM_B1 ** ADAM_STEP)
    v_hat = v / (1.0 - ADAM_B2 ** ADAM_STEP)
    return -ADAM_LR * (m_hat / (jnp.sqrt(v_hat) + ADAM_EPS) + ADAM_WD * w), m, v


def _adam_ada(c_all, dmod_cols, w, m, v):
    d, n = w.shape
    tr = _tile(d, 256)

    def body(c_ref, dm_ref, w_ref, m_ref, v_ref, g_ref, dl_ref, nm_ref, nv_ref):
        g = _dot3(jax.nn.silu(c_ref[...]), dm_ref[...], TN)
        delta, nm, nv = _adam(w_ref[...], g, m_ref[...], v_ref[...])
        g_ref[...] = g
        dl_ref[...] = delta
        nm_ref[...] = nm
        nv_ref[...] = nv

    row = _row_spec(tr, n)
    return pl.pallas_call(
        body, name="adam_ada", grid=(d // tr,),
        in_specs=[pl.BlockSpec((NDEV, tr), lambda i: (0, i)), pl.BlockSpec((NDEV, n), lambda i: (0, 0)), row, row, row],
        out_specs=[row] * 4, out_shape=[jax.ShapeDtypeStruct((d, n), F32)] * 4, compiler_params=_params(1),
    )(c_all, dmod_cols, w, m, v)


def _adam_sum(name, parts, part_spec, w, m, v, tr):
    r, c = w.shape

    def body(p_ref, w_ref, m_ref, v_ref, g_ref, dl_ref, nm_ref, nv_ref):
        g = p_ref[0].astype(F32)
        for k in range(1, NDEV):
            g = g + p_ref[k].astype(F32)
        delta, nm, nv = _adam(w_ref[...], g, m_ref[...], v_ref[...])
        g_ref[...] = g
        dl_ref[...] = delta
        nm_ref[...] = nm
        nv_ref[...] = nv

    row = _row_spec(tr, c)
    return pl.pallas_call(
        body, name=name, grid=(r // tr,), in_specs=[part_spec, row, row, row],
        out_specs=[row] * 4, out_shape=[jax.ShapeDtypeStruct((r, c), F32)] * 4, compiler_params=_params(1),
    )(parts, w, m, v)


GROUPS_PER_BLOCK = LANES // SSM_GROUP


def _to_b_blocks(bb, n_blk, p):
    t = bb.reshape(n_blk, GROUPS_PER_BLOCK, p, SSM_GROUP)
    eye = jnp.eye(GROUPS_PER_BLOCK, dtype=bb.dtype)
    return jnp.einsum("jgph,gk->jghkp", t, eye).reshape(n_blk, LANES, GROUPS_PER_BLOCK * p)


def _from_b_blocks(blk, n_blk, p):
    t = blk.reshape(n_blk, GROUPS_PER_BLOCK, SSM_GROUP, GROUPS_PER_BLOCK, p)
    eye = jnp.eye(GROUPS_PER_BLOCK, dtype=blk.dtype)
    return jnp.einsum("jghkp,gk->jgph", t, eye).reshape(n_blk * GROUPS_PER_BLOCK, p, SSM_GROUP)


def _to_c_blocks(cc, n_blk, p):
    t = cc.reshape(n_blk, GROUPS_PER_BLOCK, SSM_GROUP, p)
    eye = jnp.eye(GROUPS_PER_BLOCK, dtype=cc.dtype)
    return jnp.einsum("jghp,gk->jgpkh", t, eye).reshape(n_blk, GROUPS_PER_BLOCK * p, LANES)


def _from_c_blocks(blk, n_blk, p):
    t = blk.reshape(n_blk, GROUPS_PER_BLOCK, p, GROUPS_PER_BLOCK, SSM_GROUP)
    eye = jnp.eye(GROUPS_PER_BLOCK, dtype=blk.dtype)
    return jnp.einsum("jgpkh,gk->jghp", t, eye).reshape(n_blk * GROUPS_PER_BLOCK, SSM_GROUP, p)


SMALL_LATE = ("b_ada_a", "g_mix")
SMALL_EARLY = ("b_ada_b", "a_re", "a_im", "log_dt", "b_re", "b_im", "c_re", "c_im", "d_skip",
               "q_gain", "k_gain", "g_ssm_out", "g_attn_out", "g_ffn")
PACK_COLS = 1024


def _pack(arrs):
    flat = jnp.concatenate([a.reshape(-1) for a in arrs])
    n = flat.shape[0]
    quantum = SUBLANES * PACK_COLS
    padded = -(-n // quantum) * quantum
    return jnp.pad(flat, (0, padded - n)).reshape(padded // PACK_COLS, PACK_COLS)


def _unpack(packed, like):
    flat, out, off = packed.reshape(-1), [], 0
    for a in like:
        out.append(flat[off:off + a.size].reshape(a.shape))
        off += a.size
    return out


def kernel(x, c, w_ada, b_ada, g_mix, w_in, a_re, a_im, log_dt, b_re, b_im, c_re, c_im, d_skip, w_glu, q_gain, k_gain, g_ssm_out, g_attn_out, w_out, g_ffn, w_gate, w_up, w_down, loss_target, m_w_ada, m_b_ada, m_g_mix, m_w_in, m_a_re, m_a_im, m_log_dt, m_b_re, m_b_im, m_c_re, m_c_im, m_d_skip, m_w_glu, m_q_gain, m_k_gain, m_g_ssm_out, m_g_attn_out, m_w_out, m_g_ffn, m_w_gate, m_w_up, m_w_down, v_w_ada, v_b_ada, v_g_mix, v_w_in, v_a_re, v_a_im, v_log_dt, v_b_re, v_b_im, v_c_re, v_c_im, v_d_skip, v_w_glu, v_q_gain, v_k_gain, v_g_ssm_out, v_g_attn_out, v_w_out, v_g_ffn, v_w_gate, v_w_up, v_w_down):
    given = dict(locals())
    seq, d = x.shape[1], x.shape[2]
    xs, tgt = x[0], loss_target[0]
    n_groups, p_state = a_re.shape[1], a_re.shape[2]
    w_ssm = n_groups * SSM_GROUP
    w_attn = w_in.shape[2] * NDEV - w_ssm
    w_attn //= 3
    n_blk, n_pair = w_ssm // LANES, w_attn // LANES
    n_heads = w_attn // HEAD_DIM
    ns_in, ns_ff = w_in.shape[2], w_gate.shape[2]
    d_mix = w_ssm + w_attn
    mx, my, mc = _me()
    me = 4 * mx + 2 * my + mc
    rt = _tile(seq, 256)
    n_rt = seq // rt
    sds = jax.ShapeDtypeStruct

    c_all = _exchange(c, True, "comm_ag_c").reshape(NDEV, d)
    n_ada = w_ada.shape[2]
    b_cols = lax.dynamic_slice(b_ada, (0, me * n_ada), (1, n_ada))
    mod_cols = _ada_forward(c_all, w_ada[0], b_cols)
    mod_all = _exchange(mod_cols, True, "comm_ag_mod")
    mod = lax.dynamic_slice(mod_all, (0, me, 0), (NDEV, 1, n_ada)).reshape(1, NDEV * n_ada)
    shift_m, scale_m, gate_m, shift_f, scale_f, gate_f = [mod[:, i * d:(i + 1) * d] for i in range(6)]
    w_in_g = _exchange(w_in[0].astype(BF16), True, "comm_ag_w_in")

    gp = n_groups * p_state
    a_re2, a_im2, ldt2 = a_re[0], a_im[0], log_dt[0].reshape(n_groups, 1)
    b_re2, b_im2 = b_re[0].reshape(gp, SSM_GROUP), b_im[0].reshape(gp, SSM_GROUP)
    lam_r, lam_i, coef_r, coef_i = _whole("s5_lam", _s5_lam, [a_re2, a_im2, ldt2], [(n_groups, p_state)] * 4)
    coef_r2, coef_i2 = coef_r.reshape(gp, 1), coef_i.reshape(gp, 1)
    bb_r, bb_i = _whole("s5_bbar", _s5_bbar, [coef_r2, coef_i2, b_re2, b_im2], [(gp, SSM_GROUP)] * 2)
    s_blk = GROUPS_PER_BLOCK * p_state
    b_blk_r = _to_b_blocks(bb_r.reshape(n_groups, p_state, SSM_GROUP), n_blk, p_state).astype(BF16)
    b_blk_i = _to_b_blocks(bb_i.reshape(n_groups, p_state, SSM_GROUP), n_blk, p_state).astype(BF16)
    c_blk_r = _to_c_blocks(c_re[0], n_blk, p_state).astype(BF16)
    c_blk_i = _to_c_blocks(c_im[0], n_blk, p_state).astype(BF16)
    lam_r3, lam_i3 = lam_r.reshape(n_blk, 1, s_blk), lam_i.reshape(n_blk, 1, s_blk)
    d_skip2 = d_skip[0].reshape(1, w_ssm)

    row_d, vec_d = _row_spec(rt, d), _vec_spec(d)
    xm, = _rowwise_fwd("seg_in", lambda *a: _seg_in(*a)[:1], [xs], [row_d], [shift_m, scale_m, g_mix], [vec_d] * 3,
                       [sds((seq, d), BF16)], [row_d], n_rt)
    bn_in = _tile(ns_in, 512)
    per = ns_in // bn_in
    bm, bk = _tile(seq, BM), _tile(d, BK)
    proj, w_glu_g, w_out_g = _mm(
        "mm_in", xm, w_in_g, NN, (seq // bm, NDEV * per, d // bk),
        pl.BlockSpec((bm, bk), lambda i, j, k: (i, k)),
        pl.BlockSpec((None, bk, bn_in), lambda i, j, k: (j // per, k, j % per)),
        pl.BlockSpec((bm, bn_in), lambda i, j, k: (i, j)), (seq, NDEV * ns_in), F32, (bm, bn_in),
        riders=[_Rider(w_glu[0].astype(BF16), True), _Rider(w_out[0].astype(BF16), True)])
    w_glu_g, w_out_g = w_glu_g.reshape(w_ssm, w_ssm), w_out_g.reshape(d_mix, d)
    q_col, k_col, v_col = w_ssm // w_attn, w_ssm // w_attn + 1, (w_ssm + 2 * w_attn) // LANES
    qg_t, kg_t = jnp.tile(q_gain, (1, n_heads)), jnp.tile(k_gain, (1, n_heads))
    row_a, vec_a = _row_spec(rt, w_attn), _vec_spec(w_attn)
    qk_rows, qk_specs = [proj, proj], [_row_spec(rt, w_attn, q_col), _row_spec(rt, w_attn, k_col)]
    qh, kh = _rowwise_fwd("seg_qk", _seg_qk, qk_rows, qk_specs, [qg_t, kg_t], [vec_a] * 2,
                          [sds((seq, w_attn), F32)] * 2, [row_a] * 2, n_rt)
    t_chunk = _tile(seq, 256)
    (ypre, x_re, x_im), (w_gu_land,) = _s5_forward(
        proj, b_blk_r, b_blk_i, c_blk_r, c_blk_i, lam_r3, lam_i3, d_skip2, n_blk, t_chunk,
        riders=[_Rider(w_up[0].astype(BF16), True, slot=1, n_slots=2)])
    tq, tk = _tile(seq, TQ), _tile(seq, TK)
    y_attn, (w_gu_land,) = _attention_forward(qh, kh, proj, v_col, n_pair, tq, tk,
                                              riders=[_Rider(w_gate[0].astype(BF16), True, land=w_gu_land, slot=0)])
    w_gu_g = w_gu_land.reshape(2 * NDEV, d, ns_ff)
    row_s, vec_s = _row_spec(rt, w_ssm), _vec_spec(w_ssm)
    y1, = _rowwise_fwd("seg_gelu", _seg_gelu, [ypre], [row_s], [], [], [sds((seq, w_ssm), F32)], [row_s], n_rt)
    z = _mm_plain("mm_glu", y1, w_glu_g, NN, F32)
    row_m = _row_spec(rt, d_mix)
    mixed, = _rowwise_fwd("seg_mix", _seg_mix, [y1, z, y_attn], [row_s, row_s, row_a], [g_ssm_out, g_attn_out], [vec_s, vec_a],
                          [sds((seq, d_mix), BF16)], [row_m], n_rt)
    o = _mm_plain("mm_out", mixed, w_out_g, NN, F32)
    h1, xf = _rowwise_fwd("seg_mid", _seg_mid, [xs, o], [row_d] * 2, [gate_m, g_ffn, scale_f, shift_f], [vec_d] * 4,
                          [sds((seq, d), F32), sds((seq, d), BF16)], [row_d] * 2, n_rt)
    gu, w_down_g = _mm(
        "mm_gu", xf, w_gu_g, NN, (seq // bm, 2 * NDEV, d // bk),
        pl.BlockSpec((bm, bk), lambda i, j, k: (i, k)), pl.BlockSpec((None, bk, ns_ff), lambda i, j, k: (j, k, 0)),
        pl.BlockSpec((None, bm, ns_ff), lambda i, j, k: (j, i, 0)), (2 * NDEV, seq, ns_ff), F32, (bm, ns_ff),
        riders=[_Rider(w_down[0].astype(BF16), True)])
    gu4 = gu.reshape(2, NDEV, seq, ns_ff)
    ft = _tile(seq, 512)
    pair_spec = pl.BlockSpec((2, None, ft, ns_ff), lambda s, i: (0, s, i, 0))
    one_spec = pl.BlockSpec((None, ft, ns_ff), lambda s, i: (s, i, 0))

    def act_body(gu_ref, a_ref):
        a_ref[...] = _seg_act(gu_ref[0], gu_ref[1])[0].astype(a_ref.dtype)

    act = pl.pallas_call(act_body, name="seg_act", grid=(NDEV, seq // ft), in_specs=[pair_spec], out_specs=one_spec,
                         out_shape=sds((NDEV, seq, ns_ff), BF16), compiler_params=_params(2))(gu4)
    bn_d = _tile(d, BN)
    ffn = _mm("mm_down", act, w_down_g, NN, (seq // bm, d // bn_d, NDEV),
              pl.BlockSpec((None, bm, ns_ff), lambda i, j, k: (k, i, 0)), pl.BlockSpec((None, ns_ff, bn_d), lambda i, j, k: (k, 0, j)),
              pl.BlockSpec((bm, bn_d), lambda i, j, k: (i, j)), (seq, d), F32, (bm, bn_d))
    dy, dffn, d_gate_f, loss_part = _loss_head(h1, ffn, tgt, gate_f, rt)
    loss = lax.psum(loss_part[0, 0], MESH_AXES)

    bl = _tile(seq, BK)
    gw_down = _mm("mm_dw_down", act, dffn, TN, (NDEV, d // bn_d, seq // bl),
                  pl.BlockSpec((None, bl, ns_ff), lambda i, j, k: (i, k, 0)), pl.BlockSpec((bl, bn_d), lambda i, j, k: (k, j)),
                  pl.BlockSpec((None, ns_ff, bn_d), lambda i, j, k: (i, 0, j)), (NDEV, ns_ff, d), BF16, (ns_ff, bn_d))
    dact, got_down = _mm(
        "mm_dact", dffn, w_down_g, NT, (seq // bm, NDEV, d // bk),
        pl.BlockSpec((bm, bk), lambda i, j, k: (i, k)), pl.BlockSpec((None, ns_ff, bk), lambda i, j, k: (j, 0, k)),
        pl.BlockSpec((None, bm, ns_ff), lambda i, j, k: (j, i, 0)), (NDEV, seq, ns_ff), F32, (bm, ns_ff),
        riders=[_Rider(gw_down, False)])

    def dact_body(gu_ref, da_ref, dgu_ref):
        _, vjp = jax.vjp(_seg_act, gu_ref[0], gu_ref[1])
        dg, du_ = vjp((da_ref[...],))
        dgu_ref[0] = dg.astype(dgu_ref.dtype)
        dgu_ref[1] = du_.astype(dgu_ref.dtype)

    dgu4 = pl.pallas_call(dact_body, name="seg_act_bwd", grid=(NDEV, seq // ft), in_specs=[pair_spec, one_spec],
                          out_specs=pair_spec, out_shape=sds((2, NDEV, seq, ns_ff), BF16), compiler_params=_params(2))(gu4, dact)
    dgu = dgu4.reshape(2 * NDEV, seq, ns_ff)
    bmd = _tile(d, BM)

    def dw_half(name, which, riders):
        return _mm(name, xf, dgu, TN, (d // bmd, NDEV, seq // bl), pl.BlockSpec((bl, bmd), lambda i, j, k: (k, i)),
                   pl.BlockSpec((None, bl, ns_ff), lambda i, j, k: (which * NDEV + j, k, 0)),
                   pl.BlockSpec((None, bmd, ns_ff), lambda i, j, k: (j, i, 0)), (NDEV, d, ns_ff), BF16, (bmd, ns_ff), riders=riders)

    gw_gate = dw_half("mm_dw_gate", 0, ())
    gw_up, got_gate = dw_half("mm_dw_up", 1, [_Rider(gw_gate, False)])
    dxf, got_up = _mm(
        "mm_dxf", dgu, w_gu_g, NT, (seq // bm, d // bn_d, 2 * NDEV),
        pl.BlockSpec((None, bm, ns_ff), lambda i, j, k: (k, i, 0)), pl.BlockSpec((None, bn_d, ns_ff), lambda i, j, k: (k, j, 0)),
        pl.BlockSpec((bm, bn_d), lambda i, j, k: (i, j)), (seq, d), F32, (bm, bn_d), riders=[_Rider(gw_up, False)])
    (do, dx_a, d_gate_m, d_g_ffn, d_scale_f, d_shift_f) = _rowwise_bwd(
        "seg_mid_bwd", _seg_mid, [xs, o], [row_d] * 2, [gate_m, g_ffn, scale_f, shift_f], [vec_d] * 4,
        [dy, dxf], [row_d] * 2, [[0], [1]], [1, 0], [sds((seq, d), BF16), sds((seq, d), F32)], [row_d] * 2,
        [0, 1, 2, 3], [sds((1, d), F32)] * 4, [vec_d] * 4, n_rt)

    dmixed = _mm_plain("mm_dmixed", do, w_out_g, NT, F32)
    gw_out = _mm_plain("mm_dw_out", mixed, do, TN, BF16)
    (dz, dy1_a, dy_attn, d_g_ssm, d_g_attn) = _rowwise_bwd(
        "seg_mix_bwd", _seg_mix, [y1, z, y_attn], [row_s, row_s, row_a], [g_ssm_out, g_attn_out], [vec_s, vec_a],
        [dmixed], [row_m], [[0]], [1, 0, 2], [sds((seq, w_ssm), BF16), sds((seq, w_ssm), F32), sds((seq, w_attn), F32)],
        [row_s, row_s, row_a], [0, 1], [sds((1, w_ssm), F32), sds((1, w_attn), F32)], [vec_s, vec_a], n_rt)
    dy1_b = _mm_plain("mm_dy1", dz, w_glu_g, NT, F32)
    gw_glu = _mm_plain("mm_dw_glu", y1, dz, TN, BF16)
    (dypre,) = _rowwise_bwd("seg_gelu_bwd", _seg_gelu, [ypre], [row_s], [], [], [dy1_a, dy1_b], [row_s] * 2, [[0, 1]],
                            [0], [sds((seq, w_ssm), F32)], [row_s], [], [], [], n_rt)
    (du, db_blk_r, db_blk_i, dc_blk_r, dc_blk_i, dlam_r3, dlam_i3, dd_skip2), (got_out, got_glu) = _s5_backward(
        dypre, proj, x_re, x_im, b_blk_r, b_blk_i, c_blk_r, c_blk_i, lam_r3, lam_i3, d_skip2, n_blk, t_chunk,
        riders=[_Rider(gw_out.reshape(NDEV, w_out.shape[1], d), False),
                _Rider(gw_glu.reshape(NDEV, w_glu.shape[1], w_ssm), False)])
    dqh, dkh, dv = _attention_backward(qh, kh, proj, v_col, y_attn, dy_attn, n_pair, tq, tk)
    (dq, dk, dqg_t, dkg_t) = _rowwise_bwd(
        "seg_qk_bwd", _seg_qk, qk_rows, qk_specs, [qg_t, kg_t], [vec_a] * 2, [dqh, dkh], [row_a] * 2, [[0], [1]],
        [0, 1], [sds((seq, w_attn), BF16)] * 2, [row_a] * 2, [0, 1], [sds((1, w_attn), F32)] * 2, [vec_a] * 2, n_rt)

    dbb_r = _from_b_blocks(db_blk_r, n_blk, p_state).reshape(gp, SSM_GROUP)
    dbb_i = _from_b_blocks(db_blk_i, n_blk, p_state).reshape(gp, SSM_GROUP)
    dcoef_r2, dcoef_i2, db_re2, db_im2 = _whole_vjp("s5_bbar_bwd", _s5_bbar, [coef_r2, coef_i2, b_re2, b_im2], [dbb_r, dbb_i],
                                                    [(gp, 1), (gp, 1), (gp, SSM_GROUP), (gp, SSM_GROUP)])
    lam_cts = [dlam_r3.reshape(n_groups, p_state), dlam_i3.reshape(n_groups, p_state),
               dcoef_r2.reshape(n_groups, p_state), dcoef_i2.reshape(n_groups, p_state)]
    da_re2, da_im2, dldt2 = _whole_vjp("s5_lam_bwd", _s5_lam, [a_re2, a_im2, ldt2], lam_cts,
                                       [(n_groups, p_state), (n_groups, p_state), (n_groups, 1)])
    dc_re2, dc_im2 = _from_c_blocks(dc_blk_r, n_blk, p_state), _from_c_blocks(dc_blk_i, n_blk, p_state)

    small_part = {
        "b_ada_b": jnp.concatenate([d_gate_m, d_shift_f, d_scale_f, d_gate_f], axis=-1),
        "a_re": da_re2, "a_im": da_im2, "log_dt": dldt2, "b_re": db_re2, "b_im": db_im2,
        "c_re": dc_re2, "c_im": dc_im2, "d_skip": dd_skip2,
        "q_gain": dqg_t.reshape(n_heads, HEAD_DIM).sum(0), "k_gain": dkg_t.reshape(n_heads, HEAD_DIM).sum(0),
        "g_ssm_out": d_g_ssm, "g_attn_out": d_g_attn, "g_ffn": d_g_ffn,
    }
    dproj = jnp.concatenate([du.astype(BF16), dq, dk, dv.astype(BF16)], axis=-1)
    bk_in = _tile(ns_in, BK)
    per_k = ns_in // bk_in
    gw_in, early_parts = _mm(
        "mm_dw_in", xm, dproj, TN, (d // bmd, NDEV * per, seq // bl),
        pl.BlockSpec((bl, bmd), lambda i, j, k: (k, i)), pl.BlockSpec((bl, bn_in), lambda i, j, k: (k, j)),
        pl.BlockSpec((None, bmd, bn_in), lambda i, j, k: (j // per, i, j % per)), (NDEV, d, ns_in), BF16, (bmd, bn_in),
        riders=[_Rider(_pack([small_part[n] for n in SMALL_EARLY]), True)])
    dxm, got_in = _mm(
        "mm_dxm", dproj, w_in_g, NT, (seq // bm, d // bn_d, NDEV * per_k),
        pl.BlockSpec((bm, bk_in), lambda i, j, k: (i, k)),
        pl.BlockSpec((None, bn_d, bk_in), lambda i, j, k: (k // per_k, j, k % per_k)),
        pl.BlockSpec((bm, bn_d), lambda i, j, k: (i, j)), (seq, d), F32, (bm, bn_d), riders=[_Rider(gw_in, False)])
    (grad_x, d_shift_m, d_scale_m, d_g_mix) = _rowwise_bwd(
        "seg_in_bwd", _seg_in, [xs], [row_d], [shift_m, scale_m, g_mix], [vec_d] * 3, [dxm, dx_a], [row_d] * 2, [[0], [1]],
        [0], [sds((seq, d), F32)], [row_d], [0, 1, 2], [sds((1, d), F32)] * 3, [vec_d] * 3, n_rt)
    small_part["b_ada_a"] = jnp.concatenate([d_shift_m, d_scale_m], axis=-1)
    small_part["g_mix"] = d_g_mix
    late_parts = _exchange(_pack([small_part[n] for n in SMALL_LATE]), True, "comm_ag_small_late")
    packed_parts = jnp.concatenate([late_parts, early_parts], axis=1)

    big = {}

    def sharded(nm, got, width, tr):
        big[nm] = _adam_sum("adam_" + nm, got, pl.BlockSpec((NDEV, tr, width), lambda i: (0, i, 0)), given[nm][0],
                            given["m_" + nm][0], given["v_" + nm][0], tr)

    sharded("w_down", got_down, d, _tile(w_down.shape[1], 64))
    sharded("w_gate", got_gate, ns_ff, _tile(d, 256))
    sharded("w_up", got_up, ns_ff, _tile(d, 256))
    sharded("w_out", got_out, d, _tile(w_out.shape[1], 128))
    sharded("w_glu", got_glu, w_ssm, _tile(w_glu.shape[1], 128))
    sharded("w_in", got_in, ns_in, _tile(d, 256))

    split = dict(given)
    for pre in ("", "m_", "v_"):
        split[pre + "b_ada_a"], split[pre + "b_ada_b"] = given[pre + "b_ada"][:, :2 * d], given[pre + "b_ada"][:, 2 * d:]
    packs = [jnp.concatenate([_pack([split[pre + n] for n in SMALL_LATE]), _pack([split[pre + n] for n in SMALL_EARLY])])
             for pre in ("", "m_", "v_")]
    rows_p = packed_parts.shape[1]
    tr_p = _tile(rows_p, 64)
    sm = _adam_sum("adam_small", packed_parts, pl.BlockSpec((NDEV, tr_p, PACK_COLS), lambda i: (0, i, 0)), *packs, tr_p)
    rows_late = late_parts.shape[1]
    small_out = []
    for t in sm:
        out = dict(zip(SMALL_LATE, _unpack(t[:rows_late], [split[n] for n in SMALL_LATE])))
        out.update(zip(SMALL_EARLY, _unpack(t[rows_late:], [split[n] for n in SMALL_EARLY])))
        out["b_ada"] = jnp.concatenate([out["b_ada_a"], out["b_ada_b"]], axis=1)
        small_out.append(out)

    rows_a, rows_b = (2 * d) // PACK_COLS, (4 * d) // PACK_COLS
    assert rows_a * PACK_COLS == 2 * d
    dmod_all = jnp.concatenate([late_parts[:, :rows_a].reshape(NDEV, 2 * d), early_parts[:, :rows_b].reshape(NDEV, 4 * d)], axis=1)
    dmod_cols = lax.dynamic_slice(dmod_all, (0, me * n_ada), (NDEV, n_ada))
    big["w_ada"] = _adam_ada(c_all, dmod_cols, w_ada[0], m_w_ada[0], v_w_ada[0])

    order = ("w_ada", "b_ada", "g_mix", "w_in", "a_re", "a_im", "log_dt", "b_re", "b_im", "c_re", "c_im", "d_skip", "w_glu",
             "q_gain", "k_gain", "g_ssm_out", "g_attn_out", "w_out", "g_ffn", "w_gate", "w_up", "w_down")
    outs = [loss, grad_x[None]]
    for kind in range(4):
        for n in order:
            outs.append(big[n][kind][None] if n in big else small_out[kind][n])
    return tuple(outs)
```

```python
import functools
import math

import jax
import jax.numpy as jnp
from jax import lax
from jax.experimental import pallas as pl
from jax.experimental.pallas import tpu as pltpu

F32 = jnp.float32
BF16 = jnp.bfloat16
NDEV = 8
MESH_AXES = ("x", "y", "c")
MESH_ID = pl.DeviceIdType.MESH
EPS = 1e-6
LANES = 128
SUBLANES = 8
HEAD_DIM = 64
SSM_GROUP = 16
ADAM_LR, ADAM_B1, ADAM_B2, ADAM_EPS, ADAM_WD, ADAM_STEP = 0.001, 0.9, 0.999, 1e-08, 0.01, 10

NN = (((1,), (0,)), ((), ()))
NT = (((1,), (1,)), ((), ()))
TN = (((0,), (0,)), ((), ()))


def _dot(a, b, dn=NN):
    return lax.dot_general(a, b, dn, preferred_element_type=F32)


def _tile(dim, pref):
    t = min(dim, pref)
    while dim % t:
        t //= 2
    return t


def _params(n):
    return pltpu.CompilerParams(dimension_semantics=("arbitrary",) * n)


def _me():
    mx, my, mc = lax.axis_index("x"), lax.axis_index("y"), lax.axis_index("c")
    return mx, my, mc


def _peer(mx, my, mc, k):
    px = 1 - mx if (k >> 2) & 1 else mx
    py = 1 - my if (k >> 1) & 1 else my
    pc = 1 - mc if k & 1 else mc
    return (px, py, pc), 4 * px + 2 * py + pc


def _exchange_copies(x_ref, land_ref, send_sems, recv_sems, gather):
    mx, my, mc = _me()
    me = 4 * mx + 2 * my + mc
    pairs = []
    for k in range(1, NDEV):
        peer, pidx = _peer(mx, my, mc, k)
        src = x_ref if gather else x_ref.at[pidx]
        mk = lambda dst, src=src, k=k, peer=peer: pltpu.make_async_remote_copy(
            src_ref=src, dst_ref=dst, send_sem=send_sems.at[k - 1], recv_sem=recv_sems.at[k - 1],
            device_id=peer, device_id_type=MESH_ID)
        pairs.append((mk(land_ref.at[me]), mk(land_ref.at[pidx])))
    return me, pairs


def _exchange(x, gather, name):
    def body(x_ref, o_ref, send_sems, recv_sems, local_sem):
        me, pairs = _exchange_copies(x_ref, o_ref, send_sems, recv_sems, gather)
        local = pltpu.make_async_copy(x_ref if gather else x_ref.at[me], o_ref.at[me], local_sem)
        local.start()
        for send, _ in pairs:
            send.start()
        for _, arrival in pairs:
            arrival.wait_recv()
        for send, _ in pairs:
            send.wait_send()
        local.wait()

    return pl.pallas_call(
        body, name=name, out_shape=jax.ShapeDtypeStruct(((NDEV,) + x.shape) if gather else x.shape, x.dtype),
        in_specs=[pl.BlockSpec(memory_space=pl.ANY)], out_specs=pl.BlockSpec(memory_space=pl.ANY),
        scratch_shapes=[pltpu.SemaphoreType.DMA((NDEV - 1,)), pltpu.SemaphoreType.DMA((NDEV - 1,)), pltpu.SemaphoreType.DMA],
    )(x)


class _Rider:
    def __init__(self, x, gather, land=None, slot=None, n_slots=None):
        self.x, self.gather, self.land, self.slot = x, gather, land, slot
        own = ((NDEV,) + x.shape) if gather else x.shape
        self.land_shape = land.shape if land is not None else (own if n_slots is None else (n_slots,) + own)

    def zone(self, land_ref):
        return land_ref if self.slot is None else land_ref.at[self.slot]

    def copies(self, x_ref, land_ref, send_sems, recv_sems, local_sem):
        zone = self.zone(land_ref)
        me, pairs = _exchange_copies(x_ref, zone, send_sems, recv_sems, self.gather)
        return pairs, pltpu.make_async_copy(x_ref if self.gather else x_ref.at[me], zone.at[me], local_sem)


def _ride(call_name, grid, riders, inner, in_specs, out_specs, out_shape, scratch_shapes, compiler_params, operands):
    n_in, n_out, n_scr = len(in_specs), len(out_specs), len(scratch_shapes)
    any_spec = pl.BlockSpec(memory_space=pl.ANY)
    extra_in, aliases = [], {}
    for r_idx, r in enumerate(riders):
        extra_in.append(r.x)
        if r.land is not None:
            aliases[n_in + len(extra_in)] = n_out + r_idx
            extra_in.append(r.land)
    sems = []
    for _ in riders:
        sems += [pltpu.SemaphoreType.DMA((NDEV - 1,)), pltpu.SemaphoreType.DMA((NDEV - 1,)), pltpu.SemaphoreType.DMA]

    def body(*refs):
        base_in, rest = refs[:n_in], refs[n_in:]
        rider_in, rest = rest[:len(extra_in)], rest[len(extra_in):]
        base_out, rest = rest[:n_out], rest[n_out:]
        lands, rest = rest[:len(riders)], rest[len(riders):]
        base_scr, rider_sems = rest[:n_scr], rest[n_scr:]
        ids = [pl.program_id(a) for a in range(len(grid))]
        first = functools.reduce(jnp.logical_and, [i == 0 for i in ids])
        last = functools.reduce(jnp.logical_and, [i == g - 1 for i, g in zip(ids, grid)])
        sets, pos = [], 0
        for r_idx, r in enumerate(riders):
            x_ref = rider_in[pos]
            pos += 2 if r.land is not None else 1
            sets.append(r.copies(x_ref, lands[r_idx], *rider_sems[3 * r_idx:3 * r_idx + 3]))

        if sets:
            @pl.when(first)
            def _():
                for pairs, local in sets:
                    for send, _ in pairs:
                        send.start()
                    local.start()

        inner(*base_in, *base_out, *base_scr)

        if sets:
            @pl.when(last)
            def _():
                for pairs, local in sets:
                    for send, arrival in pairs:
                        send.wait_send()
                        arrival.wait_recv()
                    local.wait()

    outs = pl.pallas_call(
        body, name=call_name, grid=grid, in_specs=list(in_specs) + [any_spec] * len(extra_in),
        out_specs=list(out_specs) + [any_spec] * len(riders),
        out_shape=list(out_shape) + [jax.ShapeDtypeStruct(r.land_shape, r.x.dtype) for r in riders],
        scratch_shapes=list(scratch_shapes) + sems, input_output_aliases=aliases, compiler_params=compiler_params,
    )(*operands, *extra_in)
    return outs[:n_out], outs[n_out:]


def _mm(name, a, b, dn, grid, a_spec, b_spec, o_spec, out_shape, out_dtype, acc_shape, riders=()):
    nk = grid[2]

    def body(a_ref, b_ref, o_ref, *scratch):
        part = _dot(a_ref[...].astype(BF16), b_ref[...].astype(BF16), dn)
        if nk == 1:
            o_ref[...] = part.astype(o_ref.dtype)
            return
        acc_ref = scratch[0]
        k = pl.program_id(2)

        @pl.when(k == 0)
        def _():
            acc_ref[...] = part

        @pl.when(k > 0)
        def _():
            acc_ref[...] += part

        @pl.when(k == nk - 1)
        def _():
            o_ref[...] = acc_ref[...].astype(o_ref.dtype)

    (out,), lands = _ride(name, grid, riders, body, [a_spec, b_spec], [o_spec], [jax.ShapeDtypeStruct(out_shape, out_dtype)],
                          [] if nk == 1 else [pltpu.VMEM(acc_shape, F32)], _params(3), [a, b])
    return (out, *lands) if riders else out


BM, BN, BK = 1024, 1024, 4096


def _mm_plain(name, a, b, dn, out_dtype):
    if dn == NN:
        (m, kk), n = a.shape, b.shape[1]
    elif dn == NT:
        (m, kk), n = a.shape, b.shape[0]
    else:
        (kk, m), n = a.shape, b.shape[1]
    half = 2 if dn == TN else 1
    bm, bn, bk = _tile(m, BM // half), _tile(n, BN // half), _tile(kk, BK)
    a_spec = pl.BlockSpec((bk, bm), lambda i, j, k: (k, i)) if dn == TN else pl.BlockSpec((bm, bk), lambda i, j, k: (i, k))
    b_spec = pl.BlockSpec((bn, bk), lambda i, j, k: (j, k)) if dn == NT else pl.BlockSpec((bk, bn), lambda i, j, k: (k, j))
    return _mm(name, a, b, dn, (m // bm, n // bn, kk // bk), a_spec, b_spec,
               pl.BlockSpec((bm, bn), lambda i, j, k: (i, j)), (m, n), out_dtype, (bm, bn))


def _row_spec(tile, width, col=0):
    return pl.BlockSpec((tile, width), lambda i: (i, col))


def _vec_spec(width, col=0):
    return pl.BlockSpec((1, width), lambda i: (0, col))


def _rowwise_fwd(name, fn, rows, row_specs, vecs, vec_specs, out_shapes, out_specs, n_tiles):
    nr, nv = len(rows), len(vecs)

    def body(*refs):
        ins = [r[...].astype(F32) for r in refs[:nr + nv]]
        outs = fn(*ins)
        for o_ref, o in zip(refs[nr + nv:], outs):
            o_ref[...] = o.astype(o_ref.dtype)

    return pl.pallas_call(body, name=name, grid=(n_tiles,), in_specs=list(row_specs) + list(vec_specs),
                          out_specs=list(out_specs), out_shape=list(out_shapes), compiler_params=_params(1))(*rows, *vecs)


def _rowwise_bwd(name, fn, rows, row_specs, vecs, vec_specs, cts, ct_specs, ct_groups,
                 drow_idx, drow_shapes, drow_specs, dvec_idx, dvec_shapes, dvec_specs, n_tiles):
    nr, nv, nc = len(rows), len(vecs), len(cts)

    def body(*refs):
        ins = [r[...].astype(F32) for r in refs[:nr + nv]]
        ct_vals = [r[...].astype(F32) for r in refs[nr + nv:nr + nv + nc]]
        out_refs = refs[nr + nv + nc:]
        _, vjp = jax.vjp(fn, *ins)
        grads = vjp(tuple(functools.reduce(lambda p, q: p + q, [ct_vals[j] for j in grp]) for grp in ct_groups))
        for o_ref, idx in zip(out_refs[:len(drow_idx)], drow_idx):
            o_ref[...] = grads[idx].astype(o_ref.dtype)
        step = pl.program_id(0)
        for o_ref, idx in zip(out_refs[len(drow_idx):], dvec_idx):
            @pl.when(step == 0)
            def _(o_ref=o_ref):
                o_ref[...] = jnp.zeros_like(o_ref)
            o_ref[...] += grads[nr + idx]

    return pl.pallas_call(body, name=name, grid=(n_tiles,),
                          in_specs=list(row_specs) + list(vec_specs) + list(ct_specs),
                          out_specs=list(drow_specs) + list(dvec_specs),
                          out_shape=list(drow_shapes) + list(dvec_shapes), compiler_params=_params(1))(*rows, *vecs, *cts)


def _rms(x):
    return x * lax.rsqrt(jnp.mean(x * x, axis=-1, keepdims=True) + EPS)


def _seg_in(x, shift, scale, gain):
    return _rms(x) * gain * (1.0 + scale) + shift, x


def _seg_qk(q, k, qg, kg):
    def norm(t, g, mult):
        blocks = []
        lane = lax.broadcasted_iota(jnp.int32, (1, LANES), 1)
        for p in range(t.shape[1] // LANES):
            tb = t[:, p * LANES:(p + 1) * LANES]
            sq = tb * tb
            lo = jnp.sum(jnp.where(lane < HEAD_DIM, sq, 0.0), axis=-1, keepdims=True)
            hi = jnp.sum(jnp.where(lane < HEAD_DIM, 0.0, sq), axis=-1, keepdims=True)
            ms = jnp.where(lane < HEAD_DIM, lo, hi) * (1.0 / HEAD_DIM)
            blocks.append(tb * lax.rsqrt(ms + EPS) * (g[:, p * LANES:(p + 1) * LANES] * mult))
        return jnp.concatenate(blocks, axis=-1) if len(blocks) > 1 else blocks[0]
    return norm(q, qg, 1.0 / math.sqrt(HEAD_DIM)), norm(k, kg, 1.0)


def _seg_gelu(ypre):
    return (jax.nn.gelu(ypre),)


def _seg_mix(y1, z, yattn, g_ssm, g_attn):
    ys = y1 * jax.nn.sigmoid(z)
    return (jnp.concatenate([_rms(ys) * g_ssm, _rms(yattn) * g_attn], axis=-1),)


def _seg_mid(x, o, gate_m, g_ffn, scale_f, shift_f):
    h1 = x + gate_m * o
    return h1, _rms(h1) * g_ffn * (1.0 + scale_f) + shift_f


def _seg_act(gate, up):
    return (jax.nn.silu(gate) * up,)


def _s5_lam(a_re, a_im, log_dt):
    dt = jnp.exp(log_dt)
    mag = jnp.exp(a_re * dt)
    lr, li = mag * jnp.cos(a_im * dt), mag * jnp.sin(a_im * dt)
    den = a_re * a_re + a_im * a_im
    nr, ni = lr - 1.0, li
    return lr, li, (nr * a_re + ni * a_im) / den, (ni * a_re - nr * a_im) / den


def _s5_bbar(coef_re, coef_im, b_re, b_im):
    return coef_re * b_re - coef_im * b_im, coef_re * b_im + coef_im * b_re


def _whole(name, fn, ins, out_shapes):
    n = len(ins)

    def body(*refs):
        outs = fn(*[r[...] for r in refs[:n]])
        for o_ref, o in zip(refs[n:], outs):
            o_ref[...] = o

    return pl.pallas_call(body, name=name, out_shape=[jax.ShapeDtypeStruct(s, F32) for s in out_shapes])(*ins)


def _whole_vjp(name, fn, ins, cts, out_shapes):
    n, nc = len(ins), len(cts)

    def body(*refs):
        _, vjp = jax.vjp(fn, *[r[...] for r in refs[:n]])
        grads = vjp(tuple(r[...] for r in refs[n:n + nc]))
        for o_ref, g in zip(refs[n + nc:], grads):
            o_ref[...] = g

    return pl.pallas_call(body, name=name, out_shape=[jax.ShapeDtypeStruct(s, F32) for s in out_shapes])(*ins, *cts)


SCAN_SHIFTS = (1, 2, 4)


def _cmul(ar, ai, br, bi):
    return ar * br - ai * bi, ar * bi + ai * br


def _scan_coefs(lr, li, reverse):
    s = lr.shape[1]
    row = lax.broadcasted_iota(jnp.int32, (SUBLANES, s), 0)
    p1 = (lr, li)
    p2 = _cmul(*p1, *p1)
    p4 = _cmul(*p2, *p2)
    p8 = _cmul(*p4, *p4)
    p3, p5, p6 = _cmul(*p1, *p2), _cmul(*p4, *p1), _cmul(*p4, *p2)
    p7 = _cmul(*p6, *p1)
    pows = (p1, p2, p3, p4, p5, p6, p7, p8)
    bc = lambda t: jnp.broadcast_to(t, (SUBLANES, s))
    steps = []
    for sh, pw in zip(SCAN_SHIFTS, (p1, p2, p4)):
        keep = (row + sh <= SUBLANES - 1) if reverse else (row >= sh)
        steps.append((jnp.where(keep, bc(pw[0]), 0.0), jnp.where(keep, bc(pw[1]), 0.0)))
    cr, ci = jnp.zeros((SUBLANES, s), F32), jnp.zeros((SUBLANES, s), F32)
    for r in range(SUBLANES):
        pw = pows[SUBLANES - 1 - r] if reverse else pows[r]
        cr = jnp.where(row == r, bc(pw[0]), cr)
        ci = jnp.where(row == r, bc(pw[1]), ci)
    return steps, (cr, ci)


def _scan_tile(xr, xi, steps, carry_pow, cr, ci, reverse):
    for sh, (ar, ai) in zip(SCAN_SHIFTS, steps):
        rs = SUBLANES - sh if reverse else sh
        sr, si = pltpu.roll(xr, rs, 0), pltpu.roll(xi, rs, 0)
        xr, xi = xr + ar * sr - ai * si, xi + ar * si + ai * sr
    pr, pi = carry_pow
    return xr + pr * cr - pi * ci, xi + pr * ci + pi * cr


def _s5_forward(proj, b_blk_re, b_blk_im, c_blk_re, c_blk_im, lam_re, lam_im, d_skip, n_blk, t_chunk, riders=()):
    seq = proj.shape[0]
    n_chunks = seq // t_chunk
    n_tiles = t_chunk // SUBLANES
    s = b_blk_re.shape[2]

    def body(u_ref, bre_ref, bim_ref, cre_ref, cim_ref, lr_ref, li_ref, d_ref, y_ref, xr_ref, xi_ref, wr, wi, carry):
        t = pl.program_id(1)

        @pl.when(t == 0)
        def _():
            carry[...] = jnp.zeros_like(carry)

        u = u_ref[...]
        ub = u.astype(BF16)
        wr[...] = _dot(ub, bre_ref[...])
        wi[...] = _dot(ub, bim_ref[...])
        steps, cpow = _scan_coefs(lr_ref[...], li_ref[...], False)

        def tile(i, c):
            r0 = pl.multiple_of(i * SUBLANES, SUBLANES)
            xr, xi = _scan_tile(wr[pl.ds(r0, SUBLANES), :], wi[pl.ds(r0, SUBLANES), :], steps, cpow, c[0], c[1], False)
            xr_ref[pl.ds(r0, SUBLANES), :] = xr
            xi_ref[pl.ds(r0, SUBLANES), :] = xi
            last = SUBLANES - 1
            return (jnp.broadcast_to(xr[last:, :], xr.shape), jnp.broadcast_to(xi[last:, :], xi.shape))

        cr, ci = lax.fori_loop(0, n_tiles, tile, (carry[0], carry[1]))
        carry[0] = cr
        carry[1] = ci
        y = _dot(xr_ref[...].astype(BF16), cre_ref[...]) - _dot(xi_ref[...].astype(BF16), cim_ref[...])
        y_ref[...] = y + d_ref[...] * u

    blk = lambda shape: pl.BlockSpec((None,) + shape, lambda j, t: (j, 0, 0))
    return _ride(
        "s5_fwd", (n_blk, n_chunks), riders, body,
        [pl.BlockSpec((t_chunk, LANES), lambda j, t: (t, j)), blk((LANES, s)), blk((LANES, s)),
         blk((s, LANES)), blk((s, LANES)), blk((1, s)), blk((1, s)), pl.BlockSpec((1, LANES), lambda j, t: (0, j))],
        [pl.BlockSpec((t_chunk, LANES), lambda j, t: (t, j)), pl.BlockSpec((t_chunk, s), lambda j, t: (t, j)),
         pl.BlockSpec((t_chunk, s), lambda j, t: (t, j))],
        [jax.ShapeDtypeStruct((seq, n_blk * LANES), F32), jax.ShapeDtypeStruct((seq, n_blk * s), F32),
         jax.ShapeDtypeStruct((seq, n_blk * s), F32)],
        [pltpu.VMEM((t_chunk, s), F32), pltpu.VMEM((t_chunk, s), F32), pltpu.VMEM((2, SUBLANES, s), F32)],
        _params(2), [proj, b_blk_re, b_blk_im, c_blk_re, c_blk_im, lam_re, lam_im, d_skip])


def _s5_backward(dypre, proj, x_re, x_im, b_blk_re, b_blk_im, c_blk_re, c_blk_im, lam_re, lam_im, d_skip, n_blk, t_chunk,
                 riders=()):
    seq = proj.shape[0]
    n_chunks = seq // t_chunk
    n_tiles = t_chunk // SUBLANES
    s = b_blk_re.shape[2]

    def body(dy_ref, u_ref, xr_ref, xi_ref, pr_ref, pi_ref, bre_ref, bim_ref, cre_ref, cim_ref, lr_ref, li_ref, d_ref,
             du_ref, dbre_ref, dbim_ref, dcre_ref, dcim_ref, dlr_ref, dli_ref, dd_ref, gr, gi, carry):
        t = pl.program_id(1)

        @pl.when(t == 0)
        def _():
            carry[...] = jnp.zeros_like(carry)
            for r in (dbre_ref, dbim_ref, dcre_ref, dcim_ref, dlr_ref, dli_ref, dd_ref):
                r[...] = jnp.zeros_like(r)

        dy = dy_ref[...]
        dyb = dy.astype(BF16)
        u = u_ref[...]
        gr[...] = _dot(dyb, cre_ref[...], NT)
        gi[...] = -_dot(dyb, cim_ref[...], NT)
        steps, cpow = _scan_coefs(lr_ref[...], -li_ref[...], True)
        row = lax.broadcasted_iota(jnp.int32, (SUBLANES, s), 0)
        last = SUBLANES - 1
        first_chunk = t == n_chunks - 1

        def tile_at(r0, prev_r, prev_i, c):
            cr, ci, ar, ai = c
            lr_, li_ = _scan_tile(gr[pl.ds(r0, SUBLANES), :], gi[pl.ds(r0, SUBLANES), :], steps, cpow, cr, ci, True)
            gr[pl.ds(r0, SUBLANES), :] = lr_
            gi[pl.ds(r0, SUBLANES), :] = li_
            xr, xi = xr_ref[pl.ds(r0, SUBLANES), :], xi_ref[pl.ds(r0, SUBLANES), :]
            xpr = jnp.where(row == 0, jnp.broadcast_to(prev_r[last:, :], xr.shape), pltpu.roll(xr, 1, 0))
            xpi = jnp.where(row == 0, jnp.broadcast_to(prev_i[last:, :], xi.shape), pltpu.roll(xi, 1, 0))
            ar = ar + lr_ * xpr + li_ * xpi
            ai = ai + li_ * xpr - lr_ * xpi
            return (jnp.broadcast_to(lr_[:1, :], lr_.shape), jnp.broadcast_to(li_[:1, :], li_.shape), ar, ai)

        def tile(ii, c):
            i = n_tiles - 1 - ii
            r0 = pl.multiple_of(i * SUBLANES, SUBLANES)
            rp = pl.multiple_of(r0 - SUBLANES, SUBLANES)
            return tile_at(r0, xr_ref[pl.ds(rp, SUBLANES), :], xi_ref[pl.ds(rp, SUBLANES), :], c)

        zero = jnp.zeros((SUBLANES, s), F32)
        c = lax.fori_loop(0, n_tiles - 1, tile, (carry[0], carry[1], zero, zero))
        keep = jnp.where(first_chunk, 0.0, 1.0)
        c = tile_at(0, pr_ref[...] * keep, pi_ref[...] * keep, c)
        carry[0] = c[0]
        carry[1] = c[1]
        dlr_ref[...] += jnp.sum(c[2], axis=0, keepdims=True)
        dli_ref[...] += jnp.sum(c[3], axis=0, keepdims=True)

        lam_r, lam_i = gr[...].astype(BF16), gi[...].astype(BF16)
        du_ref[...] = _dot(lam_r, bre_ref[...], NT) + _dot(lam_i, bim_ref[...], NT) + d_ref[...] * dy
        ub = u.astype(BF16)
        dbre_ref[...] += _dot(ub, lam_r, TN)
        dbim_ref[...] += _dot(ub, lam_i, TN)
        dcre_ref[...] += _dot(xr_ref[...].astype(BF16), dyb, TN)
        dcim_ref[...] -= _dot(xi_ref[...].astype(BF16), dyb, TN)
        dd_ref[...] += jnp.sum(dy * u, axis=0, keepdims=True)

    rev = lambda t: n_chunks - 1 - t
    blk = lambda shape: pl.BlockSpec((None,) + shape, lambda j, t: (j, 0, 0))
    tpc = t_chunk // SUBLANES
    prev_spec = pl.BlockSpec((SUBLANES, s), lambda j, t: (jnp.maximum(rev(t) * tpc - 1, 0), j))
    chunk = lambda w: pl.BlockSpec((t_chunk, w), lambda j, t: (rev(t), j))
    return _ride(
        "s5_bwd", (n_blk, n_chunks), riders, body,
        [chunk(LANES), chunk(LANES), chunk(s), chunk(s), prev_spec, prev_spec, blk((LANES, s)), blk((LANES, s)),
         blk((s, LANES)), blk((s, LANES)), blk((1, s)), blk((1, s)), pl.BlockSpec((1, LANES), lambda j, t: (0, j))],
        [chunk(LANES), blk((LANES, s)), blk((LANES, s)), blk((s, LANES)), blk((s, LANES)), blk((1, s)), blk((1, s)),
         pl.BlockSpec((1, LANES), lambda j, t: (0, j))],
        [jax.ShapeDtypeStruct((seq, n_blk * LANES), F32),
         jax.ShapeDtypeStruct((n_blk, LANES, s), F32), jax.ShapeDtypeStruct((n_blk, LANES, s), F32),
         jax.ShapeDtypeStruct((n_blk, s, LANES), F32), jax.ShapeDtypeStruct((n_blk, s, LANES), F32),
         jax.ShapeDtypeStruct((n_blk, 1, s), F32), jax.ShapeDtypeStruct((n_blk, 1, s), F32),
         jax.ShapeDtypeStruct((1, n_blk * LANES), F32)],
        [pltpu.VMEM((t_chunk, s), F32), pltpu.VMEM((t_chunk, s), F32), pltpu.VMEM((2, SUBLANES, s), F32)],
        _params(2), [dypre, proj, x_re, x_im, x_re, x_im, b_blk_re, b_blk_im, c_blk_re, c_blk_im, lam_re, lam_im, d_skip])


TQ, TK = 256, 128


def _split_bf16(x):
    hi = x.astype(BF16)
    return hi, (x - hi.astype(F32)).astype(BF16)


def _sb_weights(z, past, carry, tri):
    ls = jnp.minimum(z, 0.0) - jnp.log(1.0 + jnp.exp(-jnp.abs(z)))
    lk = ls - z
    if past is not None:
        lk = jnp.where(past, lk, 0.0)
    hi, lo = _split_bf16(lk)
    w = jnp.exp(ls + (_dot(hi, tri) + _dot(lo, tri)) + carry)
    if past is not None:
        w = jnp.where(past, w, 0.0)
    return ls, lk, w


LOG_KEEP_DEAD = -104.0


def _walk_key_blocks(i, ratio, prologue, block, epilogue, log_keep):
    n_kb = (i + 1) * ratio
    prologue(n_kb - 1)
    for n in range(ratio):
        block(n_kb - 1 - n, n % 2, True)
    n_pairs = (i * ratio) // 2

    def more(state):
        t, alive = state
        return jnp.logical_and(t < n_pairs, alive)

    def pair(state):
        t, _ = state
        j = n_kb - 1 - ratio - 2 * t
        block(j, ratio % 2, False)
        block(j - 1, (ratio + 1) % 2, False)
        return t + 1, log_keep() >= LOG_KEEP_DEAD

    done, _ = lax.while_loop(more, pair, (jnp.int32(0), log_keep() >= LOG_KEEP_DEAD))
    epilogue(n_kb - ratio - 2 * done)


def _attention_forward(qh, kh, proj, v_col, n_pair, tq, tk, riders=()):
    seq = qh.shape[0]
    ratio = tq // tk
    assert ratio % 2 == 0

    def body(q_ref, k_ref, v_ref, o_ref, q_scr, z_scr, w_scr, acc_scr, c_scr):
        i = pl.program_id(1)
        lane = lax.broadcasted_iota(jnp.int32, (1, LANES), 1)
        q2 = q_ref[...]
        q_scr[0] = jnp.where(lane < HEAD_DIM, q2, 0.0).astype(BF16)
        q_scr[1] = jnp.where(lane < HEAD_DIM, 0.0, q2).astype(BF16)
        tri = (lax.broadcasted_iota(jnp.int32, (tk, tk), 0) > lax.broadcasted_iota(jnp.int32, (tk, tk), 1)).astype(BF16)
        qpos = i * tq + lax.broadcasted_iota(jnp.int32, (tq, tk), 0)
        kidx = lax.broadcasted_iota(jnp.int32, (tq, tk), 1)

        def rows(ref, j):
            j = jnp.clip(j, 0, seq // tk - 1)
            return ref[pl.ds(pl.multiple_of(j * tk, tk), tk), :].astype(BF16)

        def scores(j, slot):
            kb = rows(k_ref, j)
            for h in range(2):
                z_scr[slot, h] = _dot(q_scr[h], kb, NT)

        def finish(j):
            vb = rows(v_ref, j)
            for h in range(2):
                acc_scr[h] += _dot(w_scr[h], vb)

        def prologue(j):
            w_scr[...] = jnp.zeros_like(w_scr)
            acc_scr[...] = jnp.zeros_like(acc_scr)
            c_scr[...] = jnp.zeros_like(c_scr)
            scores(j, 0)

        def block(j, slot, masked):
            scores(j - 1, 1 - slot)
            finish(j + 1)
            past = ((kidx + j * tk) < qpos) if masked else None
            for h in range(2):
                _, lk, w = _sb_weights(z_scr[slot, h], past, c_scr[h], tri)
                w_scr[h] = w.astype(BF16)
                c_scr[h] += jnp.sum(lk, axis=-1, keepdims=True)

        _walk_key_blocks(i, ratio, prologue, block, finish, lambda: jnp.max(c_scr[...]))
        o_ref[...] = jnp.where(lane < HEAD_DIM, acc_scr[0], acc_scr[1])

    (out,), lands = _ride(
        "attn_fwd", (n_pair, seq // tq), riders, body,
        [pl.BlockSpec((tq, LANES), lambda p, i: (i, p)), pl.BlockSpec((seq, LANES), lambda p, i: (0, p)),
         pl.BlockSpec((seq, LANES), lambda p, i: (0, v_col + p))],
        [pl.BlockSpec((tq, LANES), lambda p, i: (i, p))], [jax.ShapeDtypeStruct(qh.shape, F32)],
        [pltpu.VMEM((2, tq, LANES), BF16), pltpu.VMEM((2, 2, tq, tk), F32), pltpu.VMEM((2, tq, tk), BF16),
         pltpu.VMEM((2, tq, LANES), F32), pltpu.VMEM((2, tq, 1), F32)],
        _params(2), [qh, kh, proj])
    return out, lands


def _attention_backward(qh, kh, proj, v_col, y, dy, n_pair, tq, tk, riders=()):
    seq = qh.shape[0]

    ratio = tq // tk
    assert ratio % 2 == 0

    def body(q_ref, k_ref, v_ref, y_ref, dy_ref, dq_ref, dk_ref, dv_ref,
             q_scr, do_scr, z_scr, dw_scr, w_scr, dz_scr, dq_scr, c_scr, c2_scr, tot_scr):
        i = pl.program_id(1)

        @pl.when(i == 0)
        def _():
            dk_ref[...] = jnp.zeros_like(dk_ref)
            dv_ref[...] = jnp.zeros_like(dv_ref)

        lane = lax.broadcasted_iota(jnp.int32, (1, LANES), 1)
        sel = (lane < HEAD_DIM, lane >= HEAD_DIM)
        q2, do2 = q_ref[...], dy_ref[...].astype(BF16)
        dot_oy = do2.astype(F32) * y_ref[...]
        for h in range(2):
            q_scr[h] = jnp.where(sel[h], q2, 0.0).astype(BF16)
            do_scr[h] = jnp.where(sel[h], do2, jnp.zeros_like(do2))
            tot_scr[h] = jnp.sum(jnp.where(sel[h], dot_oy, 0.0), axis=-1, keepdims=True)
        r_i, c_i = lax.broadcasted_iota(jnp.int32, (tk, tk), 0), lax.broadcasted_iota(jnp.int32, (tk, tk), 1)
        tri = (r_i > c_i).astype(BF16)
        tri_ge = (r_i >= c_i).astype(BF16)
        qpos = i * tq + lax.broadcasted_iota(jnp.int32, (tq, tk), 0)
        kidx = lax.broadcasted_iota(jnp.int32, (tq, tk), 1)

        def start(j):
            return pl.multiple_of(jnp.clip(j, 0, seq // tk - 1) * tk, tk)

        def scores(j, slot):
            c0 = start(j)
            kb, vb = k_ref[pl.ds(c0, tk), :].astype(BF16), v_ref[pl.ds(c0, tk), :].astype(BF16)
            for h in range(2):
                z_scr[slot, h] = _dot(q_scr[h], kb, NT)
                dw_scr[slot, h] = _dot(do_scr[h], vb, NT)

        def finish(j):
            c0 = start(j)
            kb = k_ref[pl.ds(c0, tk), :].astype(BF16)
            dk_add, dv_add = jnp.zeros((tk, LANES), F32), jnp.zeros((tk, LANES), F32)
            for h in range(2):
                dz = dz_scr[h]
                dq_scr[h] += _dot(dz, kb)
                dk_add = dk_add + _dot(dz, q_scr[h], TN)
                dv_add = dv_add + _dot(w_scr[h], do_scr[h], TN)
            dk_ref[pl.ds(c0, tk), :] += dk_add
            dv_ref[pl.ds(c0, tk), :] += dv_add

        def prologue(j):
            for r in (w_scr, dz_scr, dq_scr, c_scr, c2_scr):
                r[...] = jnp.zeros_like(r)
            scores(j, 0)

        def block(j, slot, masked):
            scores(j - 1, 1 - slot)
            finish(j + 1)
            past = ((kidx + j * tk) < qpos) if masked else None
            for h in range(2):
                ls, lk, w = _sb_weights(z_scr[slot, h], past, c_scr[h], tri)
                wb = w.astype(BF16)
                dlw = dw_scr[slot, h] * wb.astype(F32)
                hi, lo = _split_bf16(dlw)
                dlk = tot_scr[h] - c2_scr[h] - (_dot(hi, tri_ge) + _dot(lo, tri_ge))
                if masked:
                    dlk = jnp.where(past, dlk, 0.0)
                sig = jnp.exp(ls)
                w_scr[h] = wb
                dz_scr[h] = (dlw * (1.0 - sig) - dlk * sig).astype(BF16)
                c_scr[h] += jnp.sum(lk, axis=-1, keepdims=True)
                c2_scr[h] += jnp.sum(dlw, axis=-1, keepdims=True)

        _walk_key_blocks(i, ratio, prologue, block, finish, lambda: jnp.max(c_scr[...]))
        dq_ref[...] = jnp.where(sel[0], dq_scr[0], dq_scr[1])

    blk = pl.BlockSpec((tq, LANES), lambda p, i: (i, p))
    full = pl.BlockSpec((seq, LANES), lambda p, i: (0, p))
    shape = jax.ShapeDtypeStruct(qh.shape, F32)
    return _ride(
        "attn_bwd", (n_pair, seq // tq), riders, body,
        [blk, full, pl.BlockSpec((seq, LANES), lambda p, i: (0, v_col + p)), blk, blk],
        [blk, full, full], [shape, shape, shape],
        [pltpu.VMEM((2, tq, LANES), BF16), pltpu.VMEM((2, tq, LANES), BF16),
         pltpu.VMEM((2, 2, tq, tk), F32), pltpu.VMEM((2, 2, tq, tk), F32),
         pltpu.VMEM((2, tq, tk), BF16), pltpu.VMEM((2, tq, tk), BF16), pltpu.VMEM((2, tq, LANES), F32),
         pltpu.VMEM((2, tq, 1), F32), pltpu.VMEM((2, tq, 1), F32), pltpu.VMEM((2, tq, 1), F32)],
        _params(2), [qh, kh, proj, y, dy])


def _loss_head(h1, ffn, target, gate_f, tile):
    seq, d = h1.shape

    def body(h_ref, f_ref, t_ref, g_ref, dy_ref, df_ref, dg_ref, loss_ref):
        @pl.when(pl.program_id(0) == 0)
        def _():
            dg_ref[...] = jnp.zeros_like(dg_ref)
            loss_ref[...] = jnp.zeros_like(loss_ref)

        f, g = f_ref[...], g_ref[...]
        err = h_ref[...] + g * f - t_ref[...]
        dy = err * (1.0 / d)
        dy_ref[...] = dy
        df_ref[...] = (dy * g).astype(df_ref.dtype)
        dg_ref[...] += jnp.sum(dy * f, axis=0, keepdims=True)
        loss_ref[...] += jnp.sum(jnp.sum(err * err, axis=-1, keepdims=True), axis=0, keepdims=True) * (0.5 / d)

    row = _row_spec(tile, d)
    return pl.pallas_call(
        body, name="loss_head", grid=(seq // tile,), in_specs=[row, row, row, _vec_spec(d)],
        out_specs=[row, row, _vec_spec(d), pl.BlockSpec((1, 1), lambda i: (0, 0))],
        out_shape=[jax.ShapeDtypeStruct((seq, d), F32), jax.ShapeDtypeStruct((seq, d), BF16),
                   jax.ShapeDtypeStruct((1, d), F32), jax.ShapeDtypeStruct((1, 1), F32)],
        compiler_params=_params(1),
    )(h1, ffn, target, gate_f)


def _dot3(a, b, dn):
    ah, al = _split_bf16(a)
    bh, bl = _split_bf16(b)
    return _dot(ah, bh, dn) + (_dot(ah, bl, dn) + _dot(al, bh, dn))


def _ada_forward(c_all, w_shard, b_cols):
    d, n = w_shard.shape
    bk = _tile(d, 512)

    def body(c_ref, w_ref, b_ref, o_ref):
        @pl.when(pl.program_id(0) == 0)
        def _():
            o_ref[...] = jnp.broadcast_to(b_ref[...], o_ref.shape)

        o_ref[...] += _dot3(jax.nn.silu(c_ref[...]), w_ref[...], NN)

    return pl.pallas_call(
        body, name="ada_fwd", grid=(d // bk,),
        in_specs=[pl.BlockSpec((NDEV, bk), lambda k: (0, k)), pl.BlockSpec((bk, n), lambda k: (k, 0)), _vec_spec(n)],
        out_specs=pl.BlockSpec((NDEV, n), lambda k: (0, 0)), out_shape=jax.ShapeDtypeStruct((NDEV, n), F32),
        compiler_params=_params(1),
    )(c_all, w_shard, b_cols)


def _adam(w, g, m, v):
    m = ADAM_B1 * m + (1.0 - ADAM_B1) * g
    v = ADAM_B2 * v + (1.0 - ADAM_B2) * (g * g)
    m_hat = m / (1.0 - ADAM_B1 ** ADAM_STEP)
    v_hat = v / (1.0 - ADAM_B2 ** ADAM_STEP)
    return -ADAM_LR * (m_hat / (jnp.sqrt(v_hat) + ADAM_EPS) + ADAM_WD * w), m, v


def _adam_ada(c_all, dmod_cols, w, m, v):
    d, n = w.shape
    tr = _tile(d, 256)

    def body(c_ref, dm_ref, w_ref, m_ref, v_ref, g_ref, dl_ref, nm_ref, nv_ref):
        g = _dot3(jax.nn.silu(c_ref[...]), dm_ref[...], TN)
        delta, nm, nv = _adam(w_ref[...], g, m_ref[...], v_ref[...])
        g_ref[...] = g
        dl_ref[...] = delta
        nm_ref[...] = nm
        nv_ref[...] = nv

    row = _row_spec(tr, n)
    return pl.pallas_call(
        body, name="adam_ada", grid=(d // tr,),
        in_specs=[pl.BlockSpec((NDEV, tr), lambda i: (0, i)), pl.BlockSpec((NDEV, n), lambda i: (0, 0)), row, row, row],
        out_specs=[row] * 4, out_shape=[jax.ShapeDtypeStruct((d, n), F32)] * 4, compiler_params=_params(1),
    )(c_all, dmod_cols, w, m, v)


def _adam_sum(name, parts, part_spec, w, m, v, tr):
    r, c = w.shape

    def body(p_ref, w_ref, m_ref, v_ref, g_ref, dl_ref, nm_ref, nv_ref):
        g = p_ref[0].astype(F32)
        for k in range(1, NDEV):
            g = g + p_ref[k].astype(F32)
        delta, nm, nv = _adam(w_ref[...], g, m_ref[...], v_ref[...])
        g_ref[...] = g
        dl_ref[...] = delta
        nm_ref[...] = nm
        nv_ref[...] = nv

    row = _row_spec(tr, c)
    return pl.pallas_call(
        body, name=name, grid=(r // tr,), in_specs=[part_spec, row, row, row],
        out_specs=[row] * 4, out_shape=[jax.ShapeDtypeStruct((r, c), F32)] * 4, compiler_params=_params(1),
    )(parts, w, m, v)


GROUPS_PER_BLOCK = LANES // SSM_GROUP


def _to_b_blocks(bb, n_blk, p):
    t = bb.reshape(n_blk, GROUPS_PER_BLOCK, p, SSM_GROUP)
    eye = jnp.eye(GROUPS_PER_BLOCK, dtype=bb.dtype)
    return jnp.einsum("jgph,gk->jghkp", t, eye).reshape(n_blk, LANES, GROUPS_PER_BLOCK * p)


def _from_b_blocks(blk, n_blk, p):
    t = blk.reshape(n_blk, GROUPS_PER_BLOCK, SSM_GROUP, GROUPS_PER_BLOCK, p)
    eye = jnp.eye(GROUPS_PER_BLOCK, dtype=blk.dtype)
    return jnp.einsum("jghkp,gk->jgph", t, eye).reshape(n_blk * GROUPS_PER_BLOCK, p, SSM_GROUP)


def _to_c_blocks(cc, n_blk, p):
    t = cc.reshape(n_blk, GROUPS_PER_BLOCK, SSM_GROUP, p)
    eye = jnp.eye(GROUPS_PER_BLOCK, dtype=cc.dtype)
    return jnp.einsum("jghp,gk->jgpkh", t, eye).reshape(n_blk, GROUPS_PER_BLOCK * p, LANES)


def _from_c_blocks(blk, n_blk, p):
    t = blk.reshape(n_blk, GROUPS_PER_BLOCK, p, GROUPS_PER_BLOCK, SSM_GROUP)
    eye = jnp.eye(GROUPS_PER_BLOCK, dtype=blk.dtype)
    return jnp.einsum("jgpkh,gk->jghp", t, eye).reshape(n_blk * GROUPS_PER_BLOCK, SSM_GROUP, p)


SMALL_LATE = ("b_ada_a", "g_mix")
SMALL_EARLY = ("b_ada_b", "a_re", "a_im", "log_dt", "b_re", "b_im", "c_re", "c_im", "d_skip",
               "q_gain", "k_gain", "g_ssm_out", "g_attn_out", "g_ffn")
PACK_COLS = 1024


def _pack(arrs):
    flat = jnp.concatenate([a.reshape(-1) for a in arrs])
    n = flat.shape[0]
    quantum = SUBLANES * PACK_COLS
    padded = -(-n // quantum) * quantum
    return jnp.pad(flat, (0, padded - n)).reshape(padded // PACK_COLS, PACK_COLS)


def _unpack(packed, like):
    flat, out, off = packed.reshape(-1), [], 0
    for a in like:
        out.append(flat[off:off + a.size].reshape(a.shape))
        off += a.size
    return out


def kernel(x, c, w_ada, b_ada, g_mix, w_in, a_re, a_im, log_dt, b_re, b_im, c_re, c_im, d_skip, w_glu, q_gain, k_gain, g_ssm_out, g_attn_out, w_out, g_ffn, w_gate, w_up, w_down, loss_target, m_w_ada, m_b_ada, m_g_mix, m_w_in, m_a_re, m_a_im, m_log_dt, m_b_re, m_b_im, m_c_re, m_c_im, m_d_skip, m_w_glu, m_q_gain, m_k_gain, m_g_ssm_out, m_g_attn_out, m_w_out, m_g_ffn, m_w_gate, m_w_up, m_w_down, v_w_ada, v_b_ada, v_g_mix, v_w_in, v_a_re, v_a_im, v_log_dt, v_b_re, v_b_im, v_c_re, v_c_im, v_d_skip, v_w_glu, v_q_gain, v_k_gain, v_g_ssm_out, v_g_attn_out, v_w_out, v_g_ffn, v_w_gate, v_w_up, v_w_down):
    given = dict(locals())
    seq, d = x.shape[1], x.shape[2]
    xs, tgt = x[0], loss_target[0]
    n_groups, p_state = a_re.shape[1], a_re.shape[2]
    w_ssm = n_groups * SSM_GROUP
    w_attn = w_in.shape[2] * NDEV - w_ssm
    w_attn //= 3
    n_blk, n_pair = w_ssm // LANES, w_attn // LANES
    n_heads = w_attn // HEAD_DIM
    ns_in, ns_ff = w_in.shape[2], w_gate.shape[2]
    d_mix = w_ssm + w_attn
    mx, my, mc = _me()
    me = 4 * mx + 2 * my + mc
    rt = _tile(seq, 256)
    n_rt = seq // rt
    sds = jax.ShapeDtypeStruct

    c_all = _exchange(c, True, "comm_ag_c").reshape(NDEV, d)
    n_ada = w_ada.shape[2]
    b_cols = lax.dynamic_slice(b_ada, (0, me * n_ada), (1, n_ada))
    mod_cols = _ada_forward(c_all, w_ada[0], b_cols)
    mod_all = _exchange(mod_cols, True, "comm_ag_mod")
    mod = lax.dynamic_slice(mod_all, (0, me, 0), (NDEV, 1, n_ada)).reshape(1, NDEV * n_ada)
    shift_m, scale_m, gate_m, shift_f, scale_f, gate_f = [mod[:, i * d:(i + 1) * d] for i in range(6)]
    w_in_g = _exchange(w_in[0].astype(BF16), True, "comm_ag_w_in")

    gp = n_groups * p_state
    a_re2, a_im2, ldt2 = a_re[0], a_im[0], log_dt[0].reshape(n_groups, 1)
    b_re2, b_im2 = b_re[0].reshape(gp, SSM_GROUP), b_im[0].reshape(gp, SSM_GROUP)
    lam_r, lam_i, coef_r, coef_i = _whole("s5_lam", _s5_lam, [a_re2, a_im2, ldt2], [(n_groups, p_state)] * 4)
    coef_r2, coef_i2 = coef_r.reshape(gp, 1), coef_i.reshape(gp, 1)
    bb_r, bb_i = _whole("s5_bbar", _s5_bbar, [coef_r2, coef_i2, b_re2, b_im2], [(gp, SSM_GROUP)] * 2)
    s_blk = GROUPS_PER_BLOCK * p_state
    b_blk_r = _to_b_blocks(bb_r.reshape(n_groups, p_state, SSM_GROUP), n_blk, p_state).astype(BF16)
    b_blk_i = _to_b_blocks(bb_i.reshape(n_groups, p_state, SSM_GROUP), n_blk, p_state).astype(BF16)
    c_blk_r = _to_c_blocks(c_re[0], n_blk, p_state).astype(BF16)
    c_blk_i = _to_c_blocks(c_im[0], n_blk, p_state).astype(BF16)
    lam_r3, lam_i3 = lam_r.reshape(n_blk, 1, s_blk), lam_i.reshape(n_blk, 1, s_blk)
    d_skip2 = d_skip[0].reshape(1, w_ssm)

    row_d, vec_d = _row_spec(rt, d), _vec_spec(d)
    xm, = _rowwise_fwd("seg_in", lambda *a: _seg_in(*a)[:1], [xs], [row_d], [shift_m, scale_m, g_mix], [vec_d] * 3,
                       [sds((seq, d), BF16)], [row_d], n_rt)
    bn_in = _tile(ns_in, 512)
    per = ns_in // bn_in
    bm, bk = _tile(seq, BM), _tile(d, BK)
    proj, w_glu_g, w_out_g = _mm(
        "mm_in", xm, w_in_g, NN, (seq // bm, NDEV * per, d // bk),
        pl.BlockSpec((bm, bk), lambda i, j, k: (i, k)),
        pl.BlockSpec((None, bk, bn_in), lambda i, j, k: (j // per, k, j % per)),
        pl.BlockSpec((bm, bn_in), lambda i, j, k: (i, j)), (seq, NDEV * ns_in), F32, (bm, bn_in),
        riders=[_Rider(w_glu[0].astype(BF16), True), _Rider(w_out[0].astype(BF16), True)])
    w_glu_g, w_out_g = w_glu_g.reshape(w_ssm, w_ssm), w_out_g.reshape(d_mix, d)
    q_col, k_col, v_col = w_ssm // w_attn, w_ssm // w_attn + 1, (w_ssm + 2 * w_attn) // LANES
    qg_t, kg_t = jnp.tile(q_gain, (1, n_heads)), jnp.tile(k_gain, (1, n_heads))
    row_a, vec_a = _row_spec(rt, w_attn), _vec_spec(w_attn)
    qk_rows, qk_specs = [proj, proj], [_row_spec(rt, w_attn, q_col), _row_spec(rt, w_attn, k_col)]
    qh, kh = _rowwise_fwd("seg_qk", _seg_qk, qk_rows, qk_specs, [qg_t, kg_t], [vec_a] * 2,
                          [sds((seq, w_attn), F32)] * 2, [row_a] * 2, n_rt)
    t_chunk = _tile(seq, 256)
    (ypre, x_re, x_im), (w_gu_land,) = _s5_forward(
        proj, b_blk_r, b_blk_i, c_blk_r, c_blk_i, lam_r3, lam_i3, d_skip2, n_blk, t_chunk,
        riders=[_Rider(w_up[0].astype(BF16), True, slot=1, n_slots=2)])
    tq, tk = _tile(seq, TQ), _tile(seq, TK)
    y_attn, (w_gu_land,) = _attention_forward(qh, kh, proj, v_col, n_pair, tq, tk,
                                              riders=[_Rider(w_gate[0].astype(BF16), True, land=w_gu_land, slot=0)])
    w_gu_g = w_gu_land.reshape(2 * NDEV, d, ns_ff)
    row_s, vec_s = _row_spec(rt, w_ssm), _vec_spec(w_ssm)
    y1, = _rowwise_fwd("seg_gelu", _seg_gelu, [ypre], [row_s], [], [], [sds((seq, w_ssm), F32)], [row_s], n_rt)
    z = _mm_plain("mm_glu", y1, w_glu_g, NN, F32)
    row_m = _row_spec(rt, d_mix)
    mixed, = _rowwise_fwd("seg_mix", _seg_mix, [y1, z, y_attn], [row_s, row_s, row_a], [g_ssm_out, g_attn_out], [vec_s, vec_a],
                          [sds((seq, d_mix), BF16)], [row_m], n_rt)
    o = _mm_plain("mm_out", mixed, w_out_g, NN, F32)
    h1, xf = _rowwise_fwd("seg_mid", _seg_mid, [xs, o], [row_d] * 2, [gate_m, g_ffn, scale_f, shift_f], [vec_d] * 4,
                          [sds((seq, d), F32), sds((seq, d), BF16)], [row_d] * 2, n_rt)
    gu, w_down_g = _mm(
        "mm_gu", xf, w_gu_g, NN, (seq // bm, 2 * NDEV, d // bk),
        pl.BlockSpec((bm, bk), lambda i, j, k: (i, k)), pl.BlockSpec((None, bk, ns_ff), lambda i, j, k: (j, k, 0)),
        pl.BlockSpec((None, bm, ns_ff), lambda i, j, k: (j, i, 0)), (2 * NDEV, seq, ns_ff), F32, (bm, ns_ff),
        riders=[_Rider(w_down[0].astype(BF16), True)])
    gu4 = gu.reshape(2, NDEV, seq, ns_ff)
    ft = _tile(seq, 512)
    pair_spec = pl.BlockSpec((2, None, ft, ns_ff), lambda s, i: (0, s, i, 0))
    one_spec = pl.BlockSpec((None, ft, ns_ff), lambda s, i: (s, i, 0))

    def act_body(gu_ref, a_ref):
        a_ref[...] = _seg_act(gu_ref[0], gu_ref[1])[0].astype(a_ref.dtype)

    act = pl.pallas_call(act_body, name="seg_act", grid=(NDEV, seq // ft), in_specs=[pair_spec], out_specs=one_spec,
                         out_shape=sds((NDEV, seq, ns_ff), BF16), compiler_params=_params(2))(gu4)
    bn_d = _tile(d, BN)
    ffn = _mm("mm_down", act, w_down_g, NN, (seq // bm, d // bn_d, NDEV),
              pl.BlockSpec((None, bm, ns_ff), lambda i, j, k: (k, i, 0)), pl.BlockSpec((None, ns_ff, bn_d), lambda i, j, k: (k, 0, j)),
              pl.BlockSpec((bm, bn_d), lambda i, j, k: (i, j)), (seq, d), F32, (bm, bn_d))
    dy, dffn, d_gate_f, loss_part = _loss_head(h1, ffn, tgt, gate_f, rt)
    loss = lax.psum(loss_part[0, 0], MESH_AXES)

    bl = _tile(seq, BK)
    gw_down = _mm("mm_dw_down", act, dffn, TN, (NDEV, d // bn_d, seq // bl),
                  pl.BlockSpec((None, bl, ns_ff), lambda i, j, k: (i, k, 0)), pl.BlockSpec((bl, bn_d), lambda i, j, k: (k, j)),
                  pl.BlockSpec((None, ns_ff, bn_d), lambda i, j, k: (i, 0, j)), (NDEV, ns_ff, d), BF16, (ns_ff, bn_d))
    dact = _mm(
        "mm_dact", dffn, w_down_g, NT, (seq // bm, NDEV, d // bk),
        pl.BlockSpec((bm, bk), lambda i, j, k: (i, k)), pl.BlockSpec((None, ns_ff, bk), lambda i, j, k: (j, 0, k)),
        pl.BlockSpec((None, bm, ns_ff), lambda i, j, k: (j, i, 0)), (NDEV, seq, ns_ff), F32, (bm, ns_ff))

    def dact_body(gu_ref, da_ref, dgu_ref):
        _, vjp = jax.vjp(_seg_act, gu_ref[0], gu_ref[1])
        dg, du_ = vjp((da_ref[...],))
        dgu_ref[0] = dg.astype(dgu_ref.dtype)
        dgu_ref[1] = du_.astype(dgu_ref.dtype)

    dgu4 = pl.pallas_call(dact_body, name="seg_act_bwd", grid=(NDEV, seq // ft), in_specs=[pair_spec, one_spec],
                          out_specs=pair_spec, out_shape=sds((2, NDEV, seq, ns_ff), BF16), compiler_params=_params(2))(gu4, dact)
    dgu = dgu4.reshape(2 * NDEV, seq, ns_ff)
    bmd = _tile(d, BM)

    def dw_half(name, which, riders):
        return _mm(name, xf, dgu, TN, (d // bmd, NDEV, seq // bl), pl.BlockSpec((bl, bmd), lambda i, j, k: (k, i)),
                   pl.BlockSpec((None, bl, ns_ff), lambda i, j, k: (which * NDEV + j, k, 0)),
                   pl.BlockSpec((None, bmd, ns_ff), lambda i, j, k: (j, i, 0)), (NDEV, d, ns_ff), BF16, (bmd, ns_ff), riders=riders)

    gw_gate = dw_half("mm_dw_gate", 0, ())
    gw_up = dw_half("mm_dw_up", 1, ())
    dxf, got_down = _mm(
        "mm_dxf", dgu, w_gu_g, NT, (seq // bm, d // bn_d, 2 * NDEV),
        pl.BlockSpec((None, bm, ns_ff), lambda i, j, k: (k, i, 0)), pl.BlockSpec((None, bn_d, ns_ff), lambda i, j, k: (k, j, 0)),
        pl.BlockSpec((bm, bn_d), lambda i, j, k: (i, j)), (seq, d), F32, (bm, bn_d), riders=[_Rider(gw_down, False)])
    (do, dx_a, d_gate_m, d_g_ffn, d_scale_f, d_shift_f) = _rowwise_bwd(
        "seg_mid_bwd", _seg_mid, [xs, o], [row_d] * 2, [gate_m, g_ffn, scale_f, shift_f], [vec_d] * 4,
        [dy, dxf], [row_d] * 2, [[0], [1]], [1, 0], [sds((seq, d), BF16), sds((seq, d), F32)], [row_d] * 2,
        [0, 1, 2, 3], [sds((1, d), F32)] * 4, [vec_d] * 4, n_rt)

    dmixed = _mm_plain("mm_dmixed", do, w_out_g, NT, F32)
    gw_out = _mm_plain("mm_dw_out", mixed, do, TN, BF16)
    (dz, dy1_a, dy_attn, d_g_ssm, d_g_attn) = _rowwise_bwd(
        "seg_mix_bwd", _seg_mix, [y1, z, y_attn], [row_s, row_s, row_a], [g_ssm_out, g_attn_out], [vec_s, vec_a],
        [dmixed], [row_m], [[0]], [1, 0, 2], [sds((seq, w_ssm), BF16), sds((seq, w_ssm), F32), sds((seq, w_attn), F32)],
        [row_s, row_s, row_a], [0, 1], [sds((1, w_ssm), F32), sds((1, w_attn), F32)], [vec_s, vec_a], n_rt)
    dy1_b = _mm_plain("mm_dy1", dz, w_glu_g, NT, F32)
    gw_glu = _mm_plain("mm_dw_glu", y1, dz, TN, BF16)
    (dypre,) = _rowwise_bwd("seg_gelu_bwd", _seg_gelu, [ypre], [row_s], [], [], [dy1_a, dy1_b], [row_s] * 2, [[0, 1]],
                            [0], [sds((seq, w_ssm), F32)], [row_s], [], [], [], n_rt)
    (du, db_blk_r, db_blk_i, dc_blk_r, dc_blk_i, dlam_r3, dlam_i3, dd_skip2), (got_gate,) = _s5_backward(
        dypre, proj, x_re, x_im, b_blk_r, b_blk_i, c_blk_r, c_blk_i, lam_r3, lam_i3, d_skip2, n_blk, t_chunk,
        riders=[_Rider(gw_gate, False)])
    (dqh, dkh, dv), (got_up, got_out, got_glu) = _attention_backward(
        qh, kh, proj, v_col, y_attn, dy_attn, n_pair, tq, tk,
        riders=[_Rider(gw_up, False), _Rider(gw_out.reshape(NDEV, w_out.shape[1], d), False),
                _Rider(gw_glu.reshape(NDEV, w_glu.shape[1], w_ssm), False)])
    (dq, dk, dqg_t, dkg_t) = _rowwise_bwd(
        "seg_qk_bwd", _seg_qk, qk_rows, qk_specs, [qg_t, kg_t], [vec_a] * 2, [dqh, dkh], [row_a] * 2, [[0], [1]],
        [0, 1], [sds((seq, w_attn), BF16)] * 2, [row_a] * 2, [0, 1], [sds((1, w_attn), F32)] * 2, [vec_a] * 2, n_rt)

    dbb_r = _from_b_blocks(db_blk_r, n_blk, p_state).reshape(gp, SSM_GROUP)
    dbb_i = _from_b_blocks(db_blk_i, n_blk, p_state).reshape(gp, SSM_GROUP)
    dcoef_r2, dcoef_i2, db_re2, db_im2 = _whole_vjp("s5_bbar_bwd", _s5_bbar, [coef_r2, coef_i2, b_re2, b_im2], [dbb_r, dbb_i],
                                                    [(gp, 1), (gp, 1), (gp, SSM_GROUP), (gp, SSM_GROUP)])
    lam_cts = [dlam_r3.reshape(n_groups, p_state), dlam_i3.reshape(n_groups, p_state),
               dcoef_r2.reshape(n_groups, p_state), dcoef_i2.reshape(n_groups, p_state)]
    da_re2, da_im2, dldt2 = _whole_vjp("s5_lam_bwd", _s5_lam, [a_re2, a_im2, ldt2], lam_cts,
                                       [(n_groups, p_state), (n_groups, p_state), (n_groups, 1)])
    dc_re2, dc_im2 = _from_c_blocks(dc_blk_r, n_blk, p_state), _from_c_blocks(dc_blk_i, n_blk, p_state)

    small_part = {
        "b_ada_b": jnp.concatenate([d_gate_m, d_shift_f, d_scale_f, d_gate_f], axis=-1),
        "a_re": da_re2, "a_im": da_im2, "log_dt": dldt2, "b_re": db_re2, "b_im": db_im2,
        "c_re": dc_re2, "c_im": dc_im2, "d_skip": dd_skip2,
        "q_gain": dqg_t.reshape(n_heads, HEAD_DIM).sum(0), "k_gain": dkg_t.reshape(n_heads, HEAD_DIM).sum(0),
        "g_ssm_out": d_g_ssm, "g_attn_out": d_g_attn, "g_ffn": d_g_ffn,
    }
    dproj = jnp.concatenate([du.astype(BF16), dq, dk, dv.astype(BF16)], axis=-1)
    bk_in = _tile(ns_in, BK)
    per_k = ns_in // bk_in
    gw_in, early_parts = _mm(
        "mm_dw_in", xm, dproj, TN, (d // bmd, NDEV * per, seq // bl),
        pl.BlockSpec((bl, bmd), lambda i, j, k: (k, i)), pl.BlockSpec((bl, bn_in), lambda i, j, k: (k, j)),
        pl.BlockSpec((None, bmd, bn_in), lambda i, j, k: (j // per, i, j % per)), (NDEV, d, ns_in), BF16, (bmd, bn_in),
        riders=[_Rider(_pack([small_part[n] for n in SMALL_EARLY]), True)])
    dxm, got_in = _mm(
        "mm_dxm", dproj, w_in_g, NT, (seq // bm, d // bn_d, NDEV * per_k),
        pl.BlockSpec((bm, bk_in), lambda i, j, k: (i, k)),
        pl.BlockSpec((None, bn_d, bk_in), lambda i, j, k: (k // per_k, j, k % per_k)),
        pl.BlockSpec((bm, bn_d), lambda i, j, k: (i, j)), (seq, d), F32, (bm, bn_d), riders=[_Rider(gw_in, False)])
    (grad_x, d_shift_m, d_scale_m, d_g_mix) = _rowwise_bwd(
        "seg_in_bwd", _seg_in, [xs], [row_d], [shift_m, scale_m, g_mix], [vec_d] * 3, [dxm, dx_a], [row_d] * 2, [[0], [1]],
        [0], [sds((seq, d), F32)], [row_d], [0, 1, 2], [sds((1, d), F32)] * 3, [vec_d] * 3, n_rt)
    small_part["b_ada_a"] = jnp.concatenate([d_shift_m, d_scale_m], axis=-1)
    small_part["g_mix"] = d_g_mix
    late_parts = _exchange(_pack([small_part[n] for n in SMALL_LATE]), True, "comm_ag_small_late")
    packed_parts = jnp.concatenate([late_parts, early_parts], axis=1)

    big = {}

    def sharded(nm, got, width, tr):
        big[nm] = _adam_sum("adam_" + nm, got, pl.BlockSpec((NDEV, tr, width), lambda i: (0, i, 0)), given[nm][0],
                            given["m_" + nm][0], given["v_" + nm][0], tr)

    sharded("w_down", got_down, d, _tile(w_down.shape[1], 64))
    sharded("w_gate", got_gate, ns_ff, _tile(d, 256))
    sharded("w_up", got_up, ns_ff, _tile(d, 256))
    sharded("w_out", got_out, d, _tile(w_out.shape[1], 128))
    sharded("w_glu", got_glu, w_ssm, _tile(w_glu.shape[1], 128))
    sharded("w_in", got_in, ns_in, _tile(d, 256))

    split = dict(given)
    for pre in ("", "m_", "v_"):
        split[pre + "b_ada_a"], split[pre + "b_ada_b"] = given[pre + "b_ada"][:, :2 * d], given[pre + "b_ada"][:, 2 * d:]
    packs = [jnp.concatenate([_pack([split[pre + n] for n in SMALL_LATE]), _pack([split[pre + n] for n in SMALL_EARLY])])
             for pre in ("", "m_", "v_")]
    rows_p = packed_parts.shape[1]
    tr_p = _tile(rows_p, 64)
    sm = _adam_sum("adam_small", packed_parts, pl.BlockSpec((NDEV, tr_p, PACK_COLS), lambda i: (0, i, 0)), *packs, tr_p)
    rows_late = late_parts.shape[1]
    small_out = []
    for t in sm:
        out = dict(zip(SMALL_LATE, _unpack(t[:rows_late], [split[n] for n in SMALL_LATE])))
        out.update(zip(SMALL_EARLY, _unpack(t[rows_late:], [split[n] for n in SMALL_EARLY])))
        out["b_ada"] = jnp.concatenate([out["b_ada_a"], out["b_ada_b"]], axis=1)
        small_out.append(out)

    rows_a, rows_b = (2 * d) // PACK_COLS, (4 * d) // PACK_COLS
    assert rows_a * PACK_COLS == 2 * d
    dmod_all = jnp.concatenate([late_parts[:, :rows_a].reshape(NDEV, 2 * d), early_parts[:, :rows_b].reshape(NDEV, 4 * d)], axis=1)
    dmod_cols = lax.dynamic_slice(dmod_all, (0, me * n_ada), (NDEV, n_ada))
    big["w_ada"] = _adam_ada(c_all, dmod_cols, w_ada[0], m_w_ada[0], v_w_ada[0])

    order = ("w_ada", "b_ada", "g_mix", "w_in", "a_re", "a_im", "log_dt", "b_re", "b_im", "c_re", "c_im", "d_skip", "w_glu",
             "q_gain", "k_gain", "g_ssm_out", "g_attn_out", "w_out", "g_ffn", "w_gate", "w_up", "w_down")
    outs = [loss, grad_x[None]]
    for kind in range(4):
        for n in order:
            outs.append(big[n][kind][None] if n in big else small_out[kind][n])
    return tuple(outs)
```

```python
import functools
import math

import jax
import jax.numpy as jnp
from jax import lax
from jax.experimental import pallas as pl
from jax.experimental.pallas import tpu as pltpu

F32 = jnp.float32
BF16 = jnp.bfloat16
NDEV = 8
MESH_AXES = ("x", "y", "c")
MESH_ID = pl.DeviceIdType.MESH
EPS = 1e-6
LANES = 128
SUBLANES = 8
HEAD_DIM = 64
SSM_GROUP = 16
ADAM_LR, ADAM_B1, ADAM_B2, ADAM_EPS, ADAM_WD, ADAM_STEP = 0.001, 0.9, 0.999, 1e-08, 0.01, 10

NN = (((1,), (0,)), ((), ()))
NT = (((1,), (1,)), ((), ()))
TN = (((0,), (0,)), ((), ()))


def _dot(a, b, dn=NN):
    return lax.dot_general(a, b, dn, preferred_element_type=F32)


def _tile(dim, pref):
    t = min(dim, pref)
    while dim % t:
        t //= 2
    return t


def _params(n):
    return pltpu.CompilerParams(dimension_semantics=("arbitrary",) * n)


def _me():
    mx, my, mc = lax.axis_index("x"), lax.axis_index("y"), lax.axis_index("c")
    return mx, my, mc


def _peer(mx, my, mc, k):
    px = 1 - mx if (k >> 2) & 1 else mx
    py = 1 - my if (k >> 1) & 1 else my
    pc = 1 - mc if k & 1 else mc
    return (px, py, pc), 4 * px + 2 * py + pc


def _exchange_copies(x_ref, land_ref, send_sems, recv_sems, gather):
    mx, my, mc = _me()
    me = 4 * mx + 2 * my + mc
    pairs = []
    for k in range(1, NDEV):
        peer, pidx = _peer(mx, my, mc, k)
        src = x_ref if gather else x_ref.at[pidx]
        mk = lambda dst, src=src, k=k, peer=peer: pltpu.make_async_remote_copy(
            src_ref=src, dst_ref=dst, send_sem=send_sems.at[k - 1], recv_sem=recv_sems.at[k - 1],
            device_id=peer, device_id_type=MESH_ID)
        pairs.append((mk(land_ref.at[me]), mk(land_ref.at[pidx])))
    return me, pairs


def _exchange(x, gather, name, relayed=False):
    def body(x_ref, o_ref, send_sems, recv_sems, local_sem):
        if relayed:
            phases = _relayed_gather_phases(x_ref, o_ref, send_sems, recv_sems, local_sem)
        else:
            phases = _direct_phases(x_ref, o_ref, send_sems, recv_sems, local_sem, gather)
        for phase in phases:
            if phase is not None:
                phase()

    return pl.pallas_call(
        body, name=name, out_shape=jax.ShapeDtypeStruct(((NDEV,) + x.shape) if gather else x.shape, x.dtype),
        in_specs=[pl.BlockSpec(memory_space=pl.ANY)], out_specs=pl.BlockSpec(memory_space=pl.ANY),
        scratch_shapes=[pltpu.SemaphoreType.DMA((NDEV - 1,)), pltpu.SemaphoreType.DMA((NDEV - 1,)), pltpu.SemaphoreType.DMA],
    )(x)


def _direct_phases(x_ref, zone, send_sems, recv_sems, local_sem, gather):
    me, pairs = _exchange_copies(x_ref, zone, send_sems, recv_sems, gather)
    local = pltpu.make_async_copy(x_ref if gather else x_ref.at[me], zone.at[me], local_sem)

    def start():
        for send, _ in pairs:
            send.start()
        local.start()

    def finish():
        for send, arrival in pairs:
            send.wait_send()
            arrival.wait_recv()
        local.wait()

    return start, None, finish


def _relayed_gather_phases(x_ref, zone, send_sems, recv_sems, local_sem):
    mx, my, mc = _me()
    me, sibling = (mx, my, mc), (mx, my, 1 - mc)
    chips = [(1 - mx, my), (mx, 1 - my), (1 - mx, 1 - my)]
    rows = lambda dev: zone.at[4 * dev[0] + 2 * dev[1] + dev[2]]

    def copy(k, block, to, src=None):
        return pltpu.make_async_remote_copy(src_ref=rows(block) if src is None else src, dst_ref=rows(block),
                                            send_sem=send_sems.at[k], recv_sem=recv_sems.at[k], device_id=to,
                                            device_id_type=MESH_ID)

    local = pltpu.make_async_copy(x_ref, rows(me), local_sem)
    first = [copy(0, me, sibling, x_ref)] + [copy(1 + j, me, (*chip, mc), x_ref) for j, chip in enumerate(chips)]
    passed = [copy(4 + j, (*chip, mc), sibling) for j, chip in enumerate(chips)]
    over_links = [copy(1 + j, (*chip, mc), me) for j, chip in enumerate(chips)]
    from_sibling = [copy(0, sibling, me)] + [copy(4 + j, (*chip, 1 - mc), me) for j, chip in enumerate(chips)]

    def start():
        local.start()
        for cp in first:
            cp.start()

    def relay():
        for arrival, onward in zip(over_links, passed):
            arrival.wait_recv()
            onward.start()

    def finish():
        for arrival in from_sibling:
            arrival.wait_recv()
        for cp in first + passed:
            cp.wait_send()
        local.wait()

    return start, relay, finish


class _Rider:
    def __init__(self, x, gather, land=None, slot=None, n_slots=None, relayed=False):
        self.x, self.gather, self.land, self.slot, self.relayed = x, gather, land, slot, relayed
        own = ((NDEV,) + x.shape) if gather else x.shape
        self.land_shape = land.shape if land is not None else (own if n_slots is None else (n_slots,) + own)

    def phases(self, x_ref, land_ref, send_sems, recv_sems, local_sem):
        zone = land_ref if self.slot is None else land_ref.at[self.slot]
        if self.relayed:
            return _relayed_gather_phases(x_ref, zone, send_sems, recv_sems, local_sem)
        return _direct_phases(x_ref, zone, send_sems, recv_sems, local_sem, self.gather)


def _ride(call_name, grid, riders, inner, in_specs, out_specs, out_shape, scratch_shapes, compiler_params, operands):
    n_in, n_out, n_scr = len(in_specs), len(out_specs), len(scratch_shapes)
    any_spec = pl.BlockSpec(memory_space=pl.ANY)
    extra_in, aliases = [], {}
    for r_idx, r in enumerate(riders):
        extra_in.append(r.x)
        if r.land is not None:
            aliases[n_in + len(extra_in)] = n_out + r_idx
            extra_in.append(r.land)
    sems = []
    for _ in riders:
        sems += [pltpu.SemaphoreType.DMA((NDEV - 1,)), pltpu.SemaphoreType.DMA((NDEV - 1,)), pltpu.SemaphoreType.DMA]

    def body(*refs):
        base_in, rest = refs[:n_in], refs[n_in:]
        rider_in, rest = rest[:len(extra_in)], rest[len(extra_in):]
        base_out, rest = rest[:n_out], rest[n_out:]
        lands, rest = rest[:len(riders)], rest[len(riders):]
        base_scr, rider_sems = rest[:n_scr], rest[n_scr:]
        step = 0
        for a, g in enumerate(grid):
            step = step * g + pl.program_id(a)
        n_steps = math.prod(grid)
        sets, pos = [], 0
        for r_idx, r in enumerate(riders):
            x_ref = rider_in[pos]
            pos += 2 if r.land is not None else 1
            sets.append(r.phases(x_ref, lands[r_idx], *rider_sems[3 * r_idx:3 * r_idx + 3]))

        if sets:
            @pl.when(step == 0)
            def _():
                for start, _, _ in sets:
                    start()

        inner(*base_in, *base_out, *base_scr)

        if any(relay is not None for _, relay, _ in sets):
            @pl.when(step == (3 * n_steps) // 5)
            def _():
                for _, relay, _ in sets:
                    if relay is not None:
                        relay()

        if sets:
            @pl.when(step == n_steps - 1)
            def _():
                for _, _, finish in sets:
                    finish()

    outs = pl.pallas_call(
        body, name=call_name, grid=grid, in_specs=list(in_specs) + [any_spec] * len(extra_in),
        out_specs=list(out_specs) + [any_spec] * len(riders),
        out_shape=list(out_shape) + [jax.ShapeDtypeStruct(r.land_shape, r.x.dtype) for r in riders],
        scratch_shapes=list(scratch_shapes) + sems, input_output_aliases=aliases, compiler_params=compiler_params,
    )(*operands, *extra_in)
    return outs[:n_out], outs[n_out:]


def _mm(name, a, b, dn, grid, a_spec, b_spec, o_spec, out_shape, out_dtype, acc_shape, riders=()):
    nk = grid[2]

    def body(a_ref, b_ref, o_ref, *scratch):
        part = _dot(a_ref[...].astype(BF16), b_ref[...].astype(BF16), dn)
        if nk == 1:
            o_ref[...] = part.astype(o_ref.dtype)
            return
        acc_ref = scratch[0]
        k = pl.program_id(2)

        @pl.when(k == 0)
        def _():
            acc_ref[...] = part

        @pl.when(k > 0)
        def _():
            acc_ref[...] += part

        @pl.when(k == nk - 1)
        def _():
            o_ref[...] = acc_ref[...].astype(o_ref.dtype)

    (out,), lands = _ride(name, grid, riders, body, [a_spec, b_spec], [o_spec], [jax.ShapeDtypeStruct(out_shape, out_dtype)],
                          [] if nk == 1 else [pltpu.VMEM(acc_shape, F32)], _params(3), [a, b])
    return (out, *lands) if riders else out


BM, BN, BK = 1024, 1024, 4096


def _mm_plain(name, a, b, dn, out_dtype):
    if dn == NN:
        (m, kk), n = a.shape, b.shape[1]
    elif dn == NT:
        (m, kk), n = a.shape, b.shape[0]
    else:
        (kk, m), n = a.shape, b.shape[1]
    half = 2 if dn == TN else 1
    bm, bn, bk = _tile(m, BM // half), _tile(n, BN // half), _tile(kk, BK)
    a_spec = pl.BlockSpec((bk, bm), lambda i, j, k: (k, i)) if dn == TN else pl.BlockSpec((bm, bk), lambda i, j, k: (i, k))
    b_spec = pl.BlockSpec((bn, bk), lambda i, j, k: (j, k)) if dn == NT else pl.BlockSpec((bk, bn), lambda i, j, k: (k, j))
    return _mm(name, a, b, dn, (m // bm, n // bn, kk // bk), a_spec, b_spec,
               pl.BlockSpec((bm, bn), lambda i, j, k: (i, j)), (m, n), out_dtype, (bm, bn))


def _row_spec(tile, width, col=0):
    return pl.BlockSpec((tile, width), lambda i: (i, col))


def _vec_spec(width, col=0):
    return pl.BlockSpec((1, width), lambda i: (0, col))


def _rowwise_fwd(name, fn, rows, row_specs, vecs, vec_specs, out_shapes, out_specs, n_tiles):
    nr, nv = len(rows), len(vecs)

    def body(*refs):
        ins = [r[...].astype(F32) for r in refs[:nr + nv]]
        outs = fn(*ins)
        for o_ref, o in zip(refs[nr + nv:], outs):
            o_ref[...] = o.astype(o_ref.dtype)

    return pl.pallas_call(body, name=name, grid=(n_tiles,), in_specs=list(row_specs) + list(vec_specs),
                          out_specs=list(out_specs), out_shape=list(out_shapes), compiler_params=_params(1))(*rows, *vecs)


def _rowwise_bwd(name, fn, rows, row_specs, vecs, vec_specs, cts, ct_specs, ct_groups,
                 drow_idx, drow_shapes, drow_specs, dvec_idx, dvec_shapes, dvec_specs, n_tiles):
    nr, nv, nc = len(rows), len(vecs), len(cts)

    def body(*refs):
        ins = [r[...].astype(F32) for r in refs[:nr + nv]]
        ct_vals = [r[...].astype(F32) for r in refs[nr + nv:nr + nv + nc]]
        out_refs = refs[nr + nv + nc:]
        _, vjp = jax.vjp(fn, *ins)
        grads = vjp(tuple(functools.reduce(lambda p, q: p + q, [ct_vals[j] for j in grp]) for grp in ct_groups))
        for o_ref, idx in zip(out_refs[:len(drow_idx)], drow_idx):
            o_ref[...] = grads[idx].astype(o_ref.dtype)
        step = pl.program_id(0)
        for o_ref, idx in zip(out_refs[len(drow_idx):], dvec_idx):
            @pl.when(step == 0)
            def _(o_ref=o_ref):
                o_ref[...] = jnp.zeros_like(o_ref)
            o_ref[...] += grads[nr + idx]

    return pl.pallas_call(body, name=name, grid=(n_tiles,),
                          in_specs=list(row_specs) + list(vec_specs) + list(ct_specs),
                          out_specs=list(drow_specs) + list(dvec_specs),
                          out_shape=list(drow_shapes) + list(dvec_shapes), compiler_params=_params(1))(*rows, *vecs, *cts)


def _rms(x):
    return x * lax.rsqrt(jnp.mean(x * x, axis=-1, keepdims=True) + EPS)


def _seg_in(x, shift, scale, gain):
    return _rms(x) * gain * (1.0 + scale) + shift, x


def _seg_qk(q, k, qg, kg):
    def norm(t, g, mult):
        blocks = []
        lane = lax.broadcasted_iota(jnp.int32, (1, LANES), 1)
        for p in range(t.shape[1] // LANES):
            tb = t[:, p * LANES:(p + 1) * LANES]
            sq = tb * tb
            lo = jnp.sum(jnp.where(lane < HEAD_DIM, sq, 0.0), axis=-1, keepdims=True)
            hi = jnp.sum(jnp.where(lane < HEAD_DIM, 0.0, sq), axis=-1, keepdims=True)
            ms = jnp.where(lane < HEAD_DIM, lo, hi) * (1.0 / HEAD_DIM)
            blocks.append(tb * lax.rsqrt(ms + EPS) * (g[:, p * LANES:(p + 1) * LANES] * mult))
        return jnp.concatenate(blocks, axis=-1) if len(blocks) > 1 else blocks[0]
    return norm(q, qg, 1.0 / math.sqrt(HEAD_DIM)), norm(k, kg, 1.0)


def _seg_gelu(ypre):
    return (jax.nn.gelu(ypre),)


def _seg_mix(y1, z, yattn, g_ssm, g_attn):
    ys = y1 * jax.nn.sigmoid(z)
    return (jnp.concatenate([_rms(ys) * g_ssm, _rms(yattn) * g_attn], axis=-1),)


def _seg_mid(x, o, gate_m, g_ffn, scale_f, shift_f):
    h1 = x + gate_m * o
    return h1, _rms(h1) * g_ffn * (1.0 + scale_f) + shift_f


def _seg_act(gate, up):
    return (jax.nn.silu(gate) * up,)


def _s5_lam(a_re, a_im, log_dt):
    dt = jnp.exp(log_dt)
    mag = jnp.exp(a_re * dt)
    lr, li = mag * jnp.cos(a_im * dt), mag * jnp.sin(a_im * dt)
    den = a_re * a_re + a_im * a_im
    nr, ni = lr - 1.0, li
    return lr, li, (nr * a_re + ni * a_im) / den, (ni * a_re - nr * a_im) / den


def _s5_bbar(coef_re, coef_im, b_re, b_im):
    return coef_re * b_re - coef_im * b_im, coef_re * b_im + coef_im * b_re


def _whole(name, fn, ins, out_shapes):
    n = len(ins)

    def body(*refs):
        outs = fn(*[r[...] for r in refs[:n]])
        for o_ref, o in zip(refs[n:], outs):
            o_ref[...] = o

    return pl.pallas_call(body, name=name, out_shape=[jax.ShapeDtypeStruct(s, F32) for s in out_shapes])(*ins)


def _whole_vjp(name, fn, ins, cts, out_shapes):
    n, nc = len(ins), len(cts)

    def body(*refs):
        _, vjp = jax.vjp(fn, *[r[...] for r in refs[:n]])
        grads = vjp(tuple(r[...] for r in refs[n:n + nc]))
        for o_ref, g in zip(refs[n + nc:], grads):
            o_ref[...] = g

    return pl.pallas_call(body, name=name, out_shape=[jax.ShapeDtypeStruct(s, F32) for s in out_shapes])(*ins, *cts)


SCAN_SHIFTS = (1, 2, 4)


def _cmul(ar, ai, br, bi):
    return ar * br - ai * bi, ar * bi + ai * br


def _scan_coefs(lr, li, reverse):
    s = lr.shape[1]
    row = lax.broadcasted_iota(jnp.int32, (SUBLANES, s), 0)
    p1 = (lr, li)
    p2 = _cmul(*p1, *p1)
    p4 = _cmul(*p2, *p2)
    p8 = _cmul(*p4, *p4)
    p3, p5, p6 = _cmul(*p1, *p2), _cmul(*p4, *p1), _cmul(*p4, *p2)
    p7 = _cmul(*p6, *p1)
    pows = (p1, p2, p3, p4, p5, p6, p7, p8)
    bc = lambda t: jnp.broadcast_to(t, (SUBLANES, s))
    steps = []
    for sh, pw in zip(SCAN_SHIFTS, (p1, p2, p4)):
        keep = (row + sh <= SUBLANES - 1) if reverse else (row >= sh)
        steps.append((jnp.where(keep, bc(pw[0]), 0.0), jnp.where(keep, bc(pw[1]), 0.0)))
    cr, ci = jnp.zeros((SUBLANES, s), F32), jnp.zeros((SUBLANES, s), F32)
    for r in range(SUBLANES):
        pw = pows[SUBLANES - 1 - r] if reverse else pows[r]
        cr = jnp.where(row == r, bc(pw[0]), cr)
        ci = jnp.where(row == r, bc(pw[1]), ci)
    return steps, (cr, ci)


def _scan_tile(xr, xi, steps, carry_pow, cr, ci, reverse):
    for sh, (ar, ai) in zip(SCAN_SHIFTS, steps):
        rs = SUBLANES - sh if reverse else sh
        sr, si = pltpu.roll(xr, rs, 0), pltpu.roll(xi, rs, 0)
        xr, xi = xr + ar * sr - ai * si, xi + ar * si + ai * sr
    pr, pi = carry_pow
    return xr + pr * cr - pi * ci, xi + pr * ci + pi * cr


def _s5_forward(proj, b_blk_re, b_blk_im, c_blk_re, c_blk_im, lam_re, lam_im, d_skip, n_blk, t_chunk, riders=()):
    seq = proj.shape[0]
    n_chunks = seq // t_chunk
    n_tiles = t_chunk // SUBLANES
    s = b_blk_re.shape[2]

    def body(u_ref, bre_ref, bim_ref, cre_ref, cim_ref, lr_ref, li_ref, d_ref, y_ref, xr_ref, xi_ref, wr, wi, carry):
        t = pl.program_id(1)

        @pl.when(t == 0)
        def _():
            carry[...] = jnp.zeros_like(carry)

        u = u_ref[...]
        ub = u.astype(BF16)
        wr[...] = _dot(ub, bre_ref[...])
        wi[...] = _dot(ub, bim_ref[...])
        steps, cpow = _scan_coefs(lr_ref[...], li_ref[...], False)

        def tile(i, c):
            r0 = pl.multiple_of(i * SUBLANES, SUBLANES)
            xr, xi = _scan_tile(wr[pl.ds(r0, SUBLANES), :], wi[pl.ds(r0, SUBLANES), :], steps, cpow, c[0], c[1], False)
            xr_ref[pl.ds(r0, SUBLANES), :] = xr
            xi_ref[pl.ds(r0, SUBLANES), :] = xi
            last = SUBLANES - 1
            return (jnp.broadcast_to(xr[last:, :], xr.shape), jnp.broadcast_to(xi[last:, :], xi.shape))

        cr, ci = lax.fori_loop(0, n_tiles, tile, (carry[0], carry[1]))
        carry[0] = cr
        carry[1] = ci
        y = _dot(xr_ref[...].astype(BF16), cre_ref[...]) - _dot(xi_ref[...].astype(BF16), cim_ref[...])
        y_ref[...] = y + d_ref[...] * u

    blk = lambda shape: pl.BlockSpec((None,) + shape, lambda j, t: (j, 0, 0))
    return _ride(
        "s5_fwd", (n_blk, n_chunks), riders, body,
        [pl.BlockSpec((t_chunk, LANES), lambda j, t: (t, j)), blk((LANES, s)), blk((LANES, s)),
         blk((s, LANES)), blk((s, LANES)), blk((1, s)), blk((1, s)), pl.BlockSpec((1, LANES), lambda j, t: (0, j))],
        [pl.BlockSpec((t_chunk, LANES), lambda j, t: (t, j)), pl.BlockSpec((t_chunk, s), lambda j, t: (t, j)),
         pl.BlockSpec((t_chunk, s), lambda j, t: (t, j))],
        [jax.ShapeDtypeStruct((seq, n_blk * LANES), F32), jax.ShapeDtypeStruct((seq, n_blk * s), F32),
         jax.ShapeDtypeStruct((seq, n_blk * s), F32)],
        [pltpu.VMEM((t_chunk, s), F32), pltpu.VMEM((t_chunk, s), F32), pltpu.VMEM((2, SUBLANES, s), F32)],
        _params(2), [proj, b_blk_re, b_blk_im, c_blk_re, c_blk_im, lam_re, lam_im, d_skip])


def _s5_backward(dypre, proj, x_re, x_im, b_blk_re, b_blk_im, c_blk_re, c_blk_im, lam_re, lam_im, d_skip, n_blk, t_chunk,
                 riders=()):
    seq = proj.shape[0]
    n_chunks = seq // t_chunk
    n_tiles = t_chunk // SUBLANES
    s = b_blk_re.shape[2]

    def body(dy_ref, u_ref, xr_ref, xi_ref, pr_ref, pi_ref, bre_ref, bim_ref, cre_ref, cim_ref, lr_ref, li_ref, d_ref,
             du_ref, dbre_ref, dbim_ref, dcre_ref, dcim_ref, dlr_ref, dli_ref, dd_ref, gr, gi, carry):
        t = pl.program_id(1)

        @pl.when(t == 0)
        def _():
            carry[...] = jnp.zeros_like(carry)
            for r in (dbre_ref, dbim_ref, dcre_ref, dcim_ref, dlr_ref, dli_ref, dd_ref):
                r[...] = jnp.zeros_like(r)

        dy = dy_ref[...]
        dyb = dy.astype(BF16)
        u = u_ref[...]
        gr[...] = _dot(dyb, cre_ref[...], NT)
        gi[...] = -_dot(dyb, cim_ref[...], NT)
        steps, cpow = _scan_coefs(lr_ref[...], -li_ref[...], True)
        row = lax.broadcasted_iota(jnp.int32, (SUBLANES, s), 0)
        last = SUBLANES - 1
        first_chunk = t == n_chunks - 1

        def tile_at(r0, prev_r, prev_i, c):
            cr, ci, ar, ai = c
            lr_, li_ = _scan_tile(gr[pl.ds(r0, SUBLANES), :], gi[pl.ds(r0, SUBLANES), :], steps, cpow, cr, ci, True)
            gr[pl.ds(r0, SUBLANES), :] = lr_
            gi[pl.ds(r0, SUBLANES), :] = li_
            xr, xi = xr_ref[pl.ds(r0, SUBLANES), :], xi_ref[pl.ds(r0, SUBLANES), :]
            xpr = jnp.where(row == 0, jnp.broadcast_to(prev_r[last:, :], xr.shape), pltpu.roll(xr, 1, 0))
            xpi = jnp.where(row == 0, jnp.broadcast_to(prev_i[last:, :], xi.shape), pltpu.roll(xi, 1, 0))
            ar = ar + lr_ * xpr + li_ * xpi
            ai = ai + li_ * xpr - lr_ * xpi
            return (jnp.broadcast_to(lr_[:1, :], lr_.shape), jnp.broadcast_to(li_[:1, :], li_.shape), ar, ai)

        def tile(ii, c):
            i = n_tiles - 1 - ii
            r0 = pl.multiple_of(i * SUBLANES, SUBLANES)
            rp = pl.multiple_of(r0 - SUBLANES, SUBLANES)
            return tile_at(r0, xr_ref[pl.ds(rp, SUBLANES), :], xi_ref[pl.ds(rp, SUBLANES), :], c)

        zero = jnp.zeros((SUBLANES, s), F32)
        c = lax.fori_loop(0, n_tiles - 1, tile, (carry[0], carry[1], zero, zero))
        keep = jnp.where(first_chunk, 0.0, 1.0)
        c = tile_at(0, pr_ref[...] * keep, pi_ref[...] * keep, c)
        carry[0] = c[0]
        carry[1] = c[1]
        dlr_ref[...] += jnp.sum(c[2], axis=0, keepdims=True)
        dli_ref[...] += jnp.sum(c[3], axis=0, keepdims=True)

        lam_r, lam_i = gr[...].astype(BF16), gi[...].astype(BF16)
        du_ref[...] = _dot(lam_r, bre_ref[...], NT) + _dot(lam_i, bim_ref[...], NT) + d_ref[...] * dy
        ub = u.astype(BF16)
        dbre_ref[...] += _dot(ub, lam_r, TN)
        dbim_ref[...] += _dot(ub, lam_i, TN)
        dcre_ref[...] += _dot(xr_ref[...].astype(BF16), dyb, TN)
        dcim_ref[...] -= _dot(xi_ref[...].astype(BF16), dyb, TN)
        dd_ref[...] += jnp.sum(dy * u, axis=0, keepdims=True)

    rev = lambda t: n_chunks - 1 - t
    blk = lambda shape: pl.BlockSpec((None,) + shape, lambda j, t: (j, 0, 0))
    tpc = t_chunk // SUBLANES
    prev_spec = pl.BlockSpec((SUBLANES, s), lambda j, t: (jnp.maximum(rev(t) * tpc - 1, 0), j))
    chunk = lambda w: pl.BlockSpec((t_chunk, w), lambda j, t: (rev(t), j))
    return _ride(
        "s5_bwd", (n_blk, n_chunks), riders, body,
        [chunk(LANES), chunk(LANES), chunk(s), chunk(s), prev_spec, prev_spec, blk((LANES, s)), blk((LANES, s)),
         blk((s, LANES)), blk((s, LANES)), blk((1, s)), blk((1, s)), pl.BlockSpec((1, LANES), lambda j, t: (0, j))],
        [chunk(LANES), blk((LANES, s)), blk((LANES, s)), blk((s, LANES)), blk((s, LANES)), blk((1, s)), blk((1, s)),
         pl.BlockSpec((1, LANES), lambda j, t: (0, j))],
        [jax.ShapeDtypeStruct((seq, n_blk * LANES), F32),
         jax.ShapeDtypeStruct((n_blk, LANES, s), F32), jax.ShapeDtypeStruct((n_blk, LANES, s), F32),
         jax.ShapeDtypeStruct((n_blk, s, LANES), F32), jax.ShapeDtypeStruct((n_blk, s, LANES), F32),
         jax.ShapeDtypeStruct((n_blk, 1, s), F32), jax.ShapeDtypeStruct((n_blk, 1, s), F32),
         jax.ShapeDtypeStruct((1, n_blk * LANES), F32)],
        [pltpu.VMEM((t_chunk, s), F32), pltpu.VMEM((t_chunk, s), F32), pltpu.VMEM((2, SUBLANES, s), F32)],
        _params(2), [dypre, proj, x_re, x_im, x_re, x_im, b_blk_re, b_blk_im, c_blk_re, c_blk_im, lam_re, lam_im, d_skip])


TQ, TK = 256, 128


def _split_bf16(x):
    hi = x.astype(BF16)
    return hi, (x - hi.astype(F32)).astype(BF16)


def _sb_weights(z, past, carry, tri):
    ls = jnp.minimum(z, 0.0) - jnp.log(1.0 + jnp.exp(-jnp.abs(z)))
    lk = ls - z
    if past is not None:
        lk = jnp.where(past, lk, 0.0)
    hi, lo = _split_bf16(lk)
    w = jnp.exp(ls + (_dot(hi, tri) + _dot(lo, tri)) + carry)
    if past is not None:
        w = jnp.where(past, w, 0.0)
    return ls, lk, w


LOG_KEEP_DEAD = -104.0


def _walk_key_blocks(i, ratio, prologue, block, epilogue, log_keep):
    n_kb = (i + 1) * ratio
    prologue(n_kb - 1)
    for n in range(ratio):
        block(n_kb - 1 - n, n % 2, True)
    n_pairs = (i * ratio) // 2

    def more(state):
        t, alive = state
        return jnp.logical_and(t < n_pairs, alive)

    def pair(state):
        t, _ = state
        j = n_kb - 1 - ratio - 2 * t
        block(j, ratio % 2, False)
        block(j - 1, (ratio + 1) % 2, False)
        return t + 1, log_keep() >= LOG_KEEP_DEAD

    done, _ = lax.while_loop(more, pair, (jnp.int32(0), log_keep() >= LOG_KEEP_DEAD))
    epilogue(n_kb - ratio - 2 * done)


def _attention_forward(qh, kh, proj, v_col, n_pair, tq, tk, riders=()):
    seq = qh.shape[0]
    ratio = tq // tk
    assert ratio % 2 == 0

    def body(q_ref, k_ref, v_ref, o_ref, q_scr, z_scr, w_scr, acc_scr, c_scr):
        i = pl.program_id(1)
        lane = lax.broadcasted_iota(jnp.int32, (1, LANES), 1)
        q2 = q_ref[...]
        q_scr[0] = jnp.where(lane < HEAD_DIM, q2, 0.0).astype(BF16)
        q_scr[1] = jnp.where(lane < HEAD_DIM, 0.0, q2).astype(BF16)
        tri = (lax.broadcasted_iota(jnp.int32, (tk, tk), 0) > lax.broadcasted_iota(jnp.int32, (tk, tk), 1)).astype(BF16)
        qpos = i * tq + lax.broadcasted_iota(jnp.int32, (tq, tk), 0)
        kidx = lax.broadcasted_iota(jnp.int32, (tq, tk), 1)

        def rows(ref, j):
            j = jnp.clip(j, 0, seq // tk - 1)
            return ref[pl.ds(pl.multiple_of(j * tk, tk), tk), :].astype(BF16)

        def scores(j, slot):
            kb = rows(k_ref, j)
            for h in range(2):
                z_scr[slot, h] = _dot(q_scr[h], kb, NT)

        def finish(j):
            vb = rows(v_ref, j)
            for h in range(2):
                acc_scr[h] += _dot(w_scr[h], vb)

        def prologue(j):
            w_scr[...] = jnp.zeros_like(w_scr)
            acc_scr[...] = jnp.zeros_like(acc_scr)
            c_scr[...] = jnp.zeros_like(c_scr)
            scores(j, 0)

        def block(j, slot, masked):
            scores(j - 1, 1 - slot)
            finish(j + 1)
            past = ((kidx + j * tk) < qpos) if masked else None
            for h in range(2):
                _, lk, w = _sb_weights(z_scr[slot, h], past, c_scr[h], tri)
                w_scr[h] = w.astype(BF16)
                c_scr[h] += jnp.sum(lk, axis=-1, keepdims=True)

        _walk_key_blocks(i, ratio, prologue, block, finish, lambda: jnp.max(c_scr[...]))
        o_ref[...] = jnp.where(lane < HEAD_DIM, acc_scr[0], acc_scr[1])

    (out,), lands = _ride(
        "attn_fwd", (n_pair, seq // tq), riders, body,
        [pl.BlockSpec((tq, LANES), lambda p, i: (i, p)), pl.BlockSpec((seq, LANES), lambda p, i: (0, p)),
         pl.BlockSpec((seq, LANES), lambda p, i: (0, v_col + p))],
        [pl.BlockSpec((tq, LANES), lambda p, i: (i, p))], [jax.ShapeDtypeStruct(qh.shape, F32)],
        [pltpu.VMEM((2, tq, LANES), BF16), pltpu.VMEM((2, 2, tq, tk), F32), pltpu.VMEM((2, tq, tk), BF16),
         pltpu.VMEM((2, tq, LANES), F32), pltpu.VMEM((2, tq, 1), F32)],
        _params(2), [qh, kh, proj])
    return out, lands


def _attention_backward(qh, kh, proj, v_col, y, dy, n_pair, tq, tk, riders=()):
    seq = qh.shape[0]

    ratio = tq // tk
    assert ratio % 2 == 0

    def body(q_ref, k_ref, v_ref, y_ref, dy_ref, dq_ref, dk_ref, dv_ref,
             q_scr, do_scr, z_scr, dw_scr, w_scr, dz_scr, dq_scr, c_scr, c2_scr, tot_scr):
        i = pl.program_id(1)

        @pl.when(i == 0)
        def _():
            dk_ref[...] = jnp.zeros_like(dk_ref)
            dv_ref[...] = jnp.zeros_like(dv_ref)

        lane = lax.broadcasted_iota(jnp.int32, (1, LANES), 1)
        sel = (lane < HEAD_DIM, lane >= HEAD_DIM)
        q2, do2 = q_ref[...], dy_ref[...].astype(BF16)
        dot_oy = do2.astype(F32) * y_ref[...]
        for h in range(2):
            q_scr[h] = jnp.where(sel[h], q2, 0.0).astype(BF16)
            do_scr[h] = jnp.where(sel[h], do2, jnp.zeros_like(do2))
            tot_scr[h] = jnp.sum(jnp.where(sel[h], dot_oy, 0.0), axis=-1, keepdims=True)
        r_i, c_i = lax.broadcasted_iota(jnp.int32, (tk, tk), 0), lax.broadcasted_iota(jnp.int32, (tk, tk), 1)
        tri = (r_i > c_i).astype(BF16)
        tri_ge = (r_i >= c_i).astype(BF16)
        qpos = i * tq + lax.broadcasted_iota(jnp.int32, (tq, tk), 0)
        kidx = lax.broadcasted_iota(jnp.int32, (tq, tk), 1)

        def start(j):
            return pl.multiple_of(jnp.clip(j, 0, seq // tk - 1) * tk, tk)

        def scores(j, slot):
            c0 = start(j)
            kb, vb = k_ref[pl.ds(c0, tk), :].astype(BF16), v_ref[pl.ds(c0, tk), :].astype(BF16)
            for h in range(2):
                z_scr[slot, h] = _dot(q_scr[h], kb, NT)
                dw_scr[slot, h] = _dot(do_scr[h], vb, NT)

        def finish(j):
            c0 = start(j)
            kb = k_ref[pl.ds(c0, tk), :].astype(BF16)
            dk_add, dv_add = jnp.zeros((tk, LANES), F32), jnp.zeros((tk, LANES), F32)
            for h in range(2):
                dz = dz_scr[h]
                dq_scr[h] += _dot(dz, kb)
                dk_add = dk_add + _dot(dz, q_scr[h], TN)
                dv_add = dv_add + _dot(w_scr[h], do_scr[h], TN)
            dk_ref[pl.ds(c0, tk), :] += dk_add
            dv_ref[pl.ds(c0, tk), :] += dv_add

        def prologue(j):
            for r in (w_scr, dz_scr, dq_scr, c_scr, c2_scr):
                r[...] = jnp.zeros_like(r)
            scores(j, 0)

        def block(j, slot, masked):
            scores(j - 1, 1 - slot)
            finish(j + 1)
            past = ((kidx + j * tk) < qpos) if masked else None
            for h in range(2):
                ls, lk, w = _sb_weights(z_scr[slot, h], past, c_scr[h], tri)
                wb = w.astype(BF16)
                dlw = dw_scr[slot, h] * wb.astype(F32)
                hi, lo = _split_bf16(dlw)
                dlk = tot_scr[h] - c2_scr[h] - (_dot(hi, tri_ge) + _dot(lo, tri_ge))
                if masked:
                    dlk = jnp.where(past, dlk, 0.0)
                sig = jnp.exp(ls)
                w_scr[h] = wb
                dz_scr[h] = (dlw * (1.0 - sig) - dlk * sig).astype(BF16)
                c_scr[h] += jnp.sum(lk, axis=-1, keepdims=True)
                c2_scr[h] += jnp.sum(dlw, axis=-1, keepdims=True)

        _walk_key_blocks(i, ratio, prologue, block, finish, lambda: jnp.max(c_scr[...]))
        dq_ref[...] = jnp.where(sel[0], dq_scr[0], dq_scr[1])

    blk = pl.BlockSpec((tq, LANES), lambda p, i: (i, p))
    full = pl.BlockSpec((seq, LANES), lambda p, i: (0, p))
    shape = jax.ShapeDtypeStruct(qh.shape, F32)
    return _ride(
        "attn_bwd", (n_pair, seq // tq), riders, body,
        [blk, full, pl.BlockSpec((seq, LANES), lambda p, i: (0, v_col + p)), blk, blk],
        [blk, full, full], [shape, shape, shape],
        [pltpu.VMEM((2, tq, LANES), BF16), pltpu.VMEM((2, tq, LANES), BF16),
         pltpu.VMEM((2, 2, tq, tk), F32), pltpu.VMEM((2, 2, tq, tk), F32),
         pltpu.VMEM((2, tq, tk), BF16), pltpu.VMEM((2, tq, tk), BF16), pltpu.VMEM((2, tq, LANES), F32),
         pltpu.VMEM((2, tq, 1), F32), pltpu.VMEM((2, tq, 1), F32), pltpu.VMEM((2, tq, 1), F32)],
        _params(2), [qh, kh, proj, y, dy])


def _loss_head(h1, ffn, target, gate_f, tile):
    seq, d = h1.shape

    def body(h_ref, f_ref, t_ref, g_ref, dy_ref, df_ref, dg_ref, loss_ref):
        @pl.when(pl.program_id(0) == 0)
        def _():
            dg_ref[...] = jnp.zeros_like(dg_ref)
            loss_ref[...] = jnp.zeros_like(loss_ref)

        f, g = f_ref[...], g_ref[...]
        err = h_ref[...] + g * f - t_ref[...]
        dy = err * (1.0 / d)
        dy_ref[...] = dy
        df_ref[...] = (dy * g).astype(df_ref.dtype)
        dg_ref[...] += jnp.sum(dy * f, axis=0, keepdims=True)
        loss_ref[...] += jnp.sum(jnp.sum(err * err, axis=-1, keepdims=True), axis=0, keepdims=True) * (0.5 / d)

    row = _row_spec(tile, d)
    return pl.pallas_call(
        body, name="loss_head", grid=(seq // tile,), in_specs=[row, row, row, _vec_spec(d)],
        out_specs=[row, row, _vec_spec(d), pl.BlockSpec((1, 1), lambda i: (0, 0))],
        out_shape=[jax.ShapeDtypeStruct((seq, d), F32), jax.ShapeDtypeStruct((seq, d), BF16),
                   jax.ShapeDtypeStruct((1, d), F32), jax.ShapeDtypeStruct((1, 1), F32)],
        compiler_params=_params(1),
    )(h1, ffn, target, gate_f)


def _dot3(a, b, dn):
    ah, al = _split_bf16(a)
    bh, bl = _split_bf16(b)
    return _dot(ah, bh, dn) + (_dot(ah, bl, dn) + _dot(al, bh, dn))


def _ada_forward(c_all, w_shard, b_cols):
    d, n = w_shard.shape
    bk = _tile(d, 512)

    def body(c_ref, w_ref, b_ref, o_ref):
        @pl.when(pl.program_id(0) == 0)
        def _():
            o_ref[...] = jnp.broadcast_to(b_ref[...], o_ref.shape)

        o_ref[...] += _dot3(jax.nn.silu(c_ref[...]), w_ref[...], NN)

    return pl.pallas_call(
        body, name="ada_fwd", grid=(d // bk,),
        in_specs=[pl.BlockSpec((NDEV, bk), lambda k: (0, k)), pl.BlockSpec((bk, n), lambda k: (k, 0)), _vec_spec(n)],
        out_specs=pl.BlockSpec((NDEV, n), lambda k: (0, 0)), out_shape=jax.ShapeDtypeStruct((NDEV, n), F32),
        compiler_params=_params(1),
    )(c_all, w_shard, b_cols)


def _adam(w, g, m, v):
    m = ADAM_B1 * m + (1.0 - ADAM_B1) * g
    v = ADAM_B2 * v + (1.0 - ADAM_B2) * (g * g)
    m_hat = m / (1.0 - ADAM_B1 ** ADAM_STEP)
    v_hat = v / (1.0 - ADAM_B2 ** ADAM_STEP)
    return -ADAM_LR * (m_hat / (jnp.sqrt(v_hat) + ADAM_EPS) + ADAM_WD * w), m, v


def _adam_ada(c_all, dmod_cols, w, m, v):
    d, n = w.shape
    tr = _tile(d, 256)

    def body(c_ref, dm_ref, w_ref, m_ref, v_ref, g_ref, dl_ref, nm_ref, nv_ref):
        g = _dot3(jax.nn.silu(c_ref[...]), dm_ref[...], TN)
        delta, nm, nv = _adam(w_ref[...], g, m_ref[...], v_ref[...])
        g_ref[...] = g
        dl_ref[...] = delta
        nm_ref[...] = nm
        nv_ref[...] = nv

    row = _row_spec(tr, n)
    return pl.pallas_call(
        body, name="adam_ada", grid=(d // tr,),
        in_specs=[pl.BlockSpec((NDEV, tr), lambda i: (0, i)), pl.BlockSpec((NDEV, n), lambda i: (0, 0)), row, row, row],
        out_specs=[row] * 4, out_shape=[jax.ShapeDtypeStruct((d, n), F32)] * 4, compiler_params=_params(1),
    )(c_all, dmod_cols, w, m, v)


def _adam_sum(name, parts, part_spec, w, m, v, tr):
    r, c = w.shape

    def body(p_ref, w_ref, m_ref, v_ref, g_ref, dl_ref, nm_ref, nv_ref):
        g = p_ref[0].astype(F32)
        for k in range(1, NDEV):
            g = g + p_ref[k].astype(F32)
        delta, nm, nv = _adam(w_ref[...], g, m_ref[...], v_ref[...])
        g_ref[...] = g
        dl_ref[...] = delta
        nm_ref[...] = nm
        nv_ref[...] = nv

    row = _row_spec(tr, c)
    return pl.pallas_call(
        body, name=name, grid=(r // tr,), in_specs=[part_spec, row, row, row],
        out_specs=[row] * 4, out_shape=[jax.ShapeDtypeStruct((r, c), F32)] * 4, compiler_params=_params(1),
    )(parts, w, m, v)


GROUPS_PER_BLOCK = LANES // SSM_GROUP


def _to_b_blocks(bb, n_blk, p):
    t = bb.reshape(n_blk, GROUPS_PER_BLOCK, p, SSM_GROUP)
    eye = jnp.eye(GROUPS_PER_BLOCK, dtype=bb.dtype)
    return jnp.einsum("jgph,gk->jghkp", t, eye).reshape(n_blk, LANES, GROUPS_PER_BLOCK * p)


def _from_b_blocks(blk, n_blk, p):
    t = blk.reshape(n_blk, GROUPS_PER_BLOCK, SSM_GROUP, GROUPS_PER_BLOCK, p)
    eye = jnp.eye(GROUPS_PER_BLOCK, dtype=blk.dtype)
    return jnp.einsum("jghkp,gk->jgph", t, eye).reshape(n_blk * GROUPS_PER_BLOCK, p, SSM_GROUP)


def _to_c_blocks(cc, n_blk, p):
    t = cc.reshape(n_blk, GROUPS_PER_BLOCK, SSM_GROUP, p)
    eye = jnp.eye(GROUPS_PER_BLOCK, dtype=cc.dtype)
    return jnp.einsum("jghp,gk->jgpkh", t, eye).reshape(n_blk, GROUPS_PER_BLOCK * p, LANES)


def _from_c_blocks(blk, n_blk, p):
    t = blk.reshape(n_blk, GROUPS_PER_BLOCK, p, GROUPS_PER_BLOCK, SSM_GROUP)
    eye = jnp.eye(GROUPS_PER_BLOCK, dtype=blk.dtype)
    return jnp.einsum("jgpkh,gk->jghp", t, eye).reshape(n_blk * GROUPS_PER_BLOCK, SSM_GROUP, p)


SMALL_LATE = ("b_ada_a", "g_mix")
SMALL_EARLY = ("b_ada_b", "a_re", "a_im", "log_dt", "b_re", "b_im", "c_re", "c_im", "d_skip",
               "q_gain", "k_gain", "g_ssm_out", "g_attn_out", "g_ffn")
PACK_COLS = 1024


def _pack(arrs):
    flat = jnp.concatenate([a.reshape(-1) for a in arrs])
    n = flat.shape[0]
    quantum = SUBLANES * PACK_COLS
    padded = -(-n // quantum) * quantum
    return jnp.pad(flat, (0, padded - n)).reshape(padded // PACK_COLS, PACK_COLS)


def _unpack(packed, like):
    flat, out, off = packed.reshape(-1), [], 0
    for a in like:
        out.append(flat[off:off + a.size].reshape(a.shape))
        off += a.size
    return out


def kernel(x, c, w_ada, b_ada, g_mix, w_in, a_re, a_im, log_dt, b_re, b_im, c_re, c_im, d_skip, w_glu, q_gain, k_gain, g_ssm_out, g_attn_out, w_out, g_ffn, w_gate, w_up, w_down, loss_target, m_w_ada, m_b_ada, m_g_mix, m_w_in, m_a_re, m_a_im, m_log_dt, m_b_re, m_b_im, m_c_re, m_c_im, m_d_skip, m_w_glu, m_q_gain, m_k_gain, m_g_ssm_out, m_g_attn_out, m_w_out, m_g_ffn, m_w_gate, m_w_up, m_w_down, v_w_ada, v_b_ada, v_g_mix, v_w_in, v_a_re, v_a_im, v_log_dt, v_b_re, v_b_im, v_c_re, v_c_im, v_d_skip, v_w_glu, v_q_gain, v_k_gain, v_g_ssm_out, v_g_attn_out, v_w_out, v_g_ffn, v_w_gate, v_w_up, v_w_down):
    given = dict(locals())
    seq, d = x.shape[1], x.shape[2]
    xs, tgt = x[0], loss_target[0]
    n_groups, p_state = a_re.shape[1], a_re.shape[2]
    w_ssm = n_groups * SSM_GROUP
    w_attn = w_in.shape[2] * NDEV - w_ssm
    w_attn //= 3
    n_blk, n_pair = w_ssm // LANES, w_attn // LANES
    n_heads = w_attn // HEAD_DIM
    ns_in, ns_ff = w_in.shape[2], w_gate.shape[2]
    d_mix = w_ssm + w_attn
    mx, my, mc = _me()
    me = 4 * mx + 2 * my + mc
    rt = _tile(seq, 256)
    n_rt = seq // rt
    sds = jax.ShapeDtypeStruct

    c_all = _exchange(c, True, "comm_ag_c").reshape(NDEV, d)
    n_ada = w_ada.shape[2]
    b_cols = lax.dynamic_slice(b_ada, (0, me * n_ada), (1, n_ada))
    mod_cols = _ada_forward(c_all, w_ada[0], b_cols)
    mod_all = _exchange(mod_cols, True, "comm_ag_mod")
    mod = lax.dynamic_slice(mod_all, (0, me, 0), (NDEV, 1, n_ada)).reshape(1, NDEV * n_ada)
    shift_m, scale_m, gate_m, shift_f, scale_f, gate_f = [mod[:, i * d:(i + 1) * d] for i in range(6)]
    w_in_g = _exchange(w_in[0].astype(BF16), True, "comm_ag_w_in", relayed=True)

    gp = n_groups * p_state
    a_re2, a_im2, ldt2 = a_re[0], a_im[0], log_dt[0].reshape(n_groups, 1)
    b_re2, b_im2 = b_re[0].reshape(gp, SSM_GROUP), b_im[0].reshape(gp, SSM_GROUP)
    lam_r, lam_i, coef_r, coef_i = _whole("s5_lam", _s5_lam, [a_re2, a_im2, ldt2], [(n_groups, p_state)] * 4)
    coef_r2, coef_i2 = coef_r.reshape(gp, 1), coef_i.reshape(gp, 1)
    bb_r, bb_i = _whole("s5_bbar", _s5_bbar, [coef_r2, coef_i2, b_re2, b_im2], [(gp, SSM_GROUP)] * 2)
    s_blk = GROUPS_PER_BLOCK * p_state
    b_blk_r = _to_b_blocks(bb_r.reshape(n_groups, p_state, SSM_GROUP), n_blk, p_state).astype(BF16)
    b_blk_i = _to_b_blocks(bb_i.reshape(n_groups, p_state, SSM_GROUP), n_blk, p_state).astype(BF16)
    c_blk_r = _to_c_blocks(c_re[0], n_blk, p_state).astype(BF16)
    c_blk_i = _to_c_blocks(c_im[0], n_blk, p_state).astype(BF16)
    lam_r3, lam_i3 = lam_r.reshape(n_blk, 1, s_blk), lam_i.reshape(n_blk, 1, s_blk)
    d_skip2 = d_skip[0].reshape(1, w_ssm)

    row_d, vec_d = _row_spec(rt, d), _vec_spec(d)
    xm, = _rowwise_fwd("seg_in", lambda *a: _seg_in(*a)[:1], [xs], [row_d], [shift_m, scale_m, g_mix], [vec_d] * 3,
                       [sds((seq, d), BF16)], [row_d], n_rt)
    bn_in = _tile(ns_in, 512)
    per = ns_in // bn_in
    bm, bk = _tile(seq, BM), _tile(d, BK)
    proj, w_glu_g, w_out_g = _mm(
        "mm_in", xm, w_in_g, NN, (seq // bm, NDEV * per, d // bk),
        pl.BlockSpec((bm, bk), lambda i, j, k: (i, k)),
        pl.BlockSpec((None, bk, bn_in), lambda i, j, k: (j // per, k, j % per)),
        pl.BlockSpec((bm, bn_in), lambda i, j, k: (i, j)), (seq, NDEV * ns_in), F32, (bm, bn_in),
        riders=[_Rider(w_glu[0].astype(BF16), True, relayed=True), _Rider(w_out[0].astype(BF16), True, relayed=True)])
    w_glu_g, w_out_g = w_glu_g.reshape(w_ssm, w_ssm), w_out_g.reshape(d_mix, d)
    q_col, k_col, v_col = w_ssm // w_attn, w_ssm // w_attn + 1, (w_ssm + 2 * w_attn) // LANES
    qg_t, kg_t = jnp.tile(q_gain, (1, n_heads)), jnp.tile(k_gain, (1, n_heads))
    row_a, vec_a = _row_spec(rt, w_attn), _vec_spec(w_attn)
    qk_rows, qk_specs = [proj, proj], [_row_spec(rt, w_attn, q_col), _row_spec(rt, w_attn, k_col)]
    qh, kh = _rowwise_fwd("seg_qk", _seg_qk, qk_rows, qk_specs, [qg_t, kg_t], [vec_a] * 2,
                          [sds((seq, w_attn), F32)] * 2, [row_a] * 2, n_rt)
    t_chunk = _tile(seq, 256)
    (ypre, x_re, x_im), (w_gu_land,) = _s5_forward(
        proj, b_blk_r, b_blk_i, c_blk_r, c_blk_i, lam_r3, lam_i3, d_skip2, n_blk, t_chunk,
        riders=[_Rider(w_up[0].astype(BF16), True, slot=1, n_slots=2, relayed=True)])
    tq, tk = _tile(seq, TQ), _tile(seq, TK)
    y_attn, (w_gu_land,) = _attention_forward(qh, kh, proj, v_col, n_pair, tq, tk,
                                              riders=[_Rider(w_gate[0].astype(BF16), True, land=w_gu_land, slot=0,
                                                             relayed=True)])
    w_gu_g = w_gu_land.reshape(2 * NDEV, d, ns_ff)
    row_s, vec_s = _row_spec(rt, w_ssm), _vec_spec(w_ssm)
    y1, = _rowwise_fwd("seg_gelu", _seg_gelu, [ypre], [row_s], [], [], [sds((seq, w_ssm), F32)], [row_s], n_rt)
    z = _mm_plain("mm_glu", y1, w_glu_g, NN, F32)
    row_m = _row_spec(rt, d_mix)
    mixed, = _rowwise_fwd("seg_mix", _seg_mix, [y1, z, y_attn], [row_s, row_s, row_a], [g_ssm_out, g_attn_out], [vec_s, vec_a],
                          [sds((seq, d_mix), BF16)], [row_m], n_rt)
    o = _mm_plain("mm_out", mixed, w_out_g, NN, F32)
    h1, xf = _rowwise_fwd("seg_mid", _seg_mid, [xs, o], [row_d] * 2, [gate_m, g_ffn, scale_f, shift_f], [vec_d] * 4,
                          [sds((seq, d), F32), sds((seq, d), BF16)], [row_d] * 2, n_rt)
    gu, w_down_g = _mm(
        "mm_gu", xf, w_gu_g, NN, (seq // bm, 2 * NDEV, d // bk),
        pl.BlockSpec((bm, bk), lambda i, j, k: (i, k)), pl.BlockSpec((None, bk, ns_ff), lambda i, j, k: (j, k, 0)),
        pl.BlockSpec((None, bm, ns_ff), lambda i, j, k: (j, i, 0)), (2 * NDEV, seq, ns_ff), BF16, (bm, ns_ff),
        riders=[_Rider(w_down[0].astype(BF16), True, relayed=True)])
    gu4 = gu.reshape(2, NDEV, seq, ns_ff)
    ft = _tile(seq, 512)
    pair_spec = pl.BlockSpec((2, None, ft, ns_ff), lambda s, i: (0, s, i, 0))
    one_spec = pl.BlockSpec((None, ft, ns_ff), lambda s, i: (s, i, 0))

    def act_body(gu_ref, a_ref):
        a_ref[...] = _seg_act(gu_ref[0].astype(F32), gu_ref[1].astype(F32))[0].astype(a_ref.dtype)

    act = pl.pallas_call(act_body, name="seg_act", grid=(NDEV, seq // ft), in_specs=[pair_spec], out_specs=one_spec,
                         out_shape=sds((NDEV, seq, ns_ff), BF16), compiler_params=_params(2))(gu4)
    bn_d = _tile(d, BN)
    ffn = _mm("mm_down", act, w_down_g, NN, (seq // bm, d // bn_d, NDEV),
              pl.BlockSpec((None, bm, ns_ff), lambda i, j, k: (k, i, 0)), pl.BlockSpec((None, ns_ff, bn_d), lambda i, j, k: (k, 0, j)),
              pl.BlockSpec((bm, bn_d), lambda i, j, k: (i, j)), (seq, d), F32, (bm, bn_d))
    dy, dffn, d_gate_f, loss_part = _loss_head(h1, ffn, tgt, gate_f, rt)
    loss = lax.psum(loss_part[0, 0], MESH_AXES)

    bl = _tile(seq, BK)
    gw_down = _mm("mm_dw_down", act, dffn, TN, (NDEV, d // bn_d, seq // bl),
                  pl.BlockSpec((None, bl, ns_ff), lambda i, j, k: (i, k, 0)), pl.BlockSpec((bl, bn_d), lambda i, j, k: (k, j)),
                  pl.BlockSpec((None, ns_ff, bn_d), lambda i, j, k: (i, 0, j)), (NDEV, ns_ff, d), BF16, (ns_ff, bn_d))
    dact = _mm(
        "mm_dact", dffn, w_down_g, NT, (seq // bm, NDEV, d // bk),
        pl.BlockSpec((bm, bk), lambda i, j, k: (i, k)), pl.BlockSpec((None, ns_ff, bk), lambda i, j, k: (j, 0, k)),
        pl.BlockSpec((None, bm, ns_ff), lambda i, j, k: (j, i, 0)), (NDEV, seq, ns_ff), BF16, (bm, ns_ff))

    def dact_body(gu_ref, da_ref, dgu_ref):
        _, vjp = jax.vjp(_seg_act, gu_ref[0].astype(F32), gu_ref[1].astype(F32))
        dg, du_ = vjp((da_ref[...].astype(F32),))
        dgu_ref[0] = dg.astype(dgu_ref.dtype)
        dgu_ref[1] = du_.astype(dgu_ref.dtype)

    dgu4 = pl.pallas_call(dact_body, name="seg_act_bwd", grid=(NDEV, seq // ft), in_specs=[pair_spec, one_spec],
                          out_specs=pair_spec, out_shape=sds((2, NDEV, seq, ns_ff), BF16), compiler_params=_params(2))(gu4, dact)
    dgu = dgu4.reshape(2 * NDEV, seq, ns_ff)
    bmd = _tile(d, BM)

    def dw_half(name, which, riders):
        return _mm(name, xf, dgu, TN, (d // bmd, NDEV, seq // bl), pl.BlockSpec((bl, bmd), lambda i, j, k: (k, i)),
                   pl.BlockSpec((None, bl, ns_ff), lambda i, j, k: (which * NDEV + j, k, 0)),
                   pl.BlockSpec((None, bmd, ns_ff), lambda i, j, k: (j, i, 0)), (NDEV, d, ns_ff), BF16, (bmd, ns_ff), riders=riders)

    gw_gate = dw_half("mm_dw_gate", 0, ())
    gw_up = dw_half("mm_dw_up", 1, ())
    dxf, got_down = _mm(
        "mm_dxf", dgu, w_gu_g, NT, (seq // bm, d // bn_d, 2 * NDEV),
        pl.BlockSpec((None, bm, ns_ff), lambda i, j, k: (k, i, 0)), pl.BlockSpec((None, bn_d, ns_ff), lambda i, j, k: (k, j, 0)),
        pl.BlockSpec((bm, bn_d), lambda i, j, k: (i, j)), (seq, d), F32, (bm, bn_d), riders=[_Rider(gw_down, False)])
    (do, dx_a, d_gate_m, d_g_ffn, d_scale_f, d_shift_f) = _rowwise_bwd(
        "seg_mid_bwd", _seg_mid, [xs, o], [row_d] * 2, [gate_m, g_ffn, scale_f, shift_f], [vec_d] * 4,
        [dy, dxf], [row_d] * 2, [[0], [1]], [1, 0], [sds((seq, d), BF16), sds((seq, d), F32)], [row_d] * 2,
        [0, 1, 2, 3], [sds((1, d), F32)] * 4, [vec_d] * 4, n_rt)

    dmixed = _mm_plain("mm_dmixed", do, w_out_g, NT, F32)
    gw_out = _mm_plain("mm_dw_out", mixed, do, TN, BF16)
    (dz, dy1_a, dy_attn, d_g_ssm, d_g_attn) = _rowwise_bwd(
        "seg_mix_bwd", _seg_mix, [y1, z, y_attn], [row_s, row_s, row_a], [g_ssm_out, g_attn_out], [vec_s, vec_a],
        [dmixed], [row_m], [[0]], [1, 0, 2], [sds((seq, w_ssm), BF16), sds((seq, w_ssm), F32), sds((seq, w_attn), F32)],
        [row_s, row_s, row_a], [0, 1], [sds((1, w_ssm), F32), sds((1, w_attn), F32)], [vec_s, vec_a], n_rt)
    dy1_b = _mm_plain("mm_dy1", dz, w_glu_g, NT, F32)
    gw_glu = _mm_plain("mm_dw_glu", y1, dz, TN, BF16)
    (dypre,) = _rowwise_bwd("seg_gelu_bwd", _seg_gelu, [ypre], [row_s], [], [], [dy1_a, dy1_b], [row_s] * 2, [[0, 1]],
                            [0], [sds((seq, w_ssm), F32)], [row_s], [], [], [], n_rt)
    (du, db_blk_r, db_blk_i, dc_blk_r, dc_blk_i, dlam_r3, dlam_i3, dd_skip2), (got_gate,) = _s5_backward(
        dypre, proj, x_re, x_im, b_blk_r, b_blk_i, c_blk_r, c_blk_i, lam_r3, lam_i3, d_skip2, n_blk, t_chunk,
        riders=[_Rider(gw_gate, False)])
    (dqh, dkh, dv), (got_up, got_out, got_glu) = _attention_backward(
        qh, kh, proj, v_col, y_attn, dy_attn, n_pair, tq, tk,
        riders=[_Rider(gw_up, False), _Rider(gw_out.reshape(NDEV, w_out.shape[1], d), False),
                _Rider(gw_glu.reshape(NDEV, w_glu.shape[1], w_ssm), False)])
    (dq, dk, dqg_t, dkg_t) = _rowwise_bwd(
        "seg_qk_bwd", _seg_qk, qk_rows, qk_specs, [qg_t, kg_t], [vec_a] * 2, [dqh, dkh], [row_a] * 2, [[0], [1]],
        [0, 1], [sds((seq, w_attn), BF16)] * 2, [row_a] * 2, [0, 1], [sds((1, w_attn), F32)] * 2, [vec_a] * 2, n_rt)

    dbb_r = _from_b_blocks(db_blk_r, n_blk, p_state).reshape(gp, SSM_GROUP)
    dbb_i = _from_b_blocks(db_blk_i, n_blk, p_state).reshape(gp, SSM_GROUP)
    dcoef_r2, dcoef_i2, db_re2, db_im2 = _whole_vjp("s5_bbar_bwd", _s5_bbar, [coef_r2, coef_i2, b_re2, b_im2], [dbb_r, dbb_i],
                                                    [(gp, 1), (gp, 1), (gp, SSM_GROUP), (gp, SSM_GROUP)])
    lam_cts = [dlam_r3.reshape(n_groups, p_state), dlam_i3.reshape(n_groups, p_state),
               dcoef_r2.reshape(n_groups, p_state), dcoef_i2.reshape(n_groups, p_state)]
    da_re2, da_im2, dldt2 = _whole_vjp("s5_lam_bwd", _s5_lam, [a_re2, a_im2, ldt2], lam_cts,
                                       [(n_groups, p_state), (n_groups, p_state), (n_groups, 1)])
    dc_re2, dc_im2 = _from_c_blocks(dc_blk_r, n_blk, p_state), _from_c_blocks(dc_blk_i, n_blk, p_state)

    small_part = {
        "b_ada_b": jnp.concatenate([d_gate_m, d_shift_f, d_scale_f, d_gate_f], axis=-1),
        "a_re": da_re2, "a_im": da_im2, "log_dt": dldt2, "b_re": db_re2, "b_im": db_im2,
        "c_re": dc_re2, "c_im": dc_im2, "d_skip": dd_skip2,
        "q_gain": dqg_t.reshape(n_heads, HEAD_DIM).sum(0), "k_gain": dkg_t.reshape(n_heads, HEAD_DIM).sum(0),
        "g_ssm_out": d_g_ssm, "g_attn_out": d_g_attn, "g_ffn": d_g_ffn,
    }
    dproj = jnp.concatenate([du.astype(BF16), dq, dk, dv.astype(BF16)], axis=-1)
    bk_in = _tile(ns_in, BK)
    per_k = ns_in // bk_in
    gw_in, early_parts = _mm(
        "mm_dw_in", xm, dproj, TN, (d // bmd, NDEV * per, seq // bl),
        pl.BlockSpec((bl, bmd), lambda i, j, k: (k, i)), pl.BlockSpec((bl, bn_in), lambda i, j, k: (k, j)),
        pl.BlockSpec((None, bmd, bn_in), lambda i, j, k: (j // per, i, j % per)), (NDEV, d, ns_in), BF16, (bmd, bn_in),
        riders=[_Rider(_pack([small_part[n] for n in SMALL_EARLY]), True, relayed=True)])
    dxm, got_in = _mm(
        "mm_dxm", dproj, w_in_g, NT, (seq // bm, d // bn_d, NDEV * per_k),
        pl.BlockSpec((bm, bk_in), lambda i, j, k: (i, k)),
        pl.BlockSpec((None, bn_d, bk_in), lambda i, j, k: (k // per_k, j, k % per_k)),
        pl.BlockSpec((bm, bn_d), lambda i, j, k: (i, j)), (seq, d), F32, (bm, bn_d), riders=[_Rider(gw_in, False)])
    (grad_x, d_shift_m, d_scale_m, d_g_mix) = _rowwise_bwd(
        "seg_in_bwd", _seg_in, [xs], [row_d], [shift_m, scale_m, g_mix], [vec_d] * 3, [dxm, dx_a], [row_d] * 2, [[0], [1]],
        [0], [sds((seq, d), F32)], [row_d], [0, 1, 2], [sds((1, d), F32)] * 3, [vec_d] * 3, n_rt)
    small_part["b_ada_a"] = jnp.concatenate([d_shift_m, d_scale_m], axis=-1)
    small_part["g_mix"] = d_g_mix
    late_parts = _exchange(_pack([small_part[n] for n in SMALL_LATE]), True, "comm_ag_small_late")
    packed_parts = jnp.concatenate([late_parts, early_parts], axis=1)

    big = {}

    def sharded(nm, got, width, tr):
        big[nm] = _adam_sum("adam_" + nm, got, pl.BlockSpec((NDEV, tr, width), lambda i: (0, i, 0)), given[nm][0],
                            given["m_" + nm][0], given["v_" + nm][0], tr)

    sharded("w_down", got_down, d, _tile(w_down.shape[1], 64))
    sharded("w_gate", got_gate, ns_ff, _tile(d, 256))
    sharded("w_up", got_up, ns_ff, _tile(d, 256))
    sharded("w_out", got_out, d, _tile(w_out.shape[1], 128))
    sharded("w_glu", got_glu, w_ssm, _tile(w_glu.shape[1], 128))
    sharded("w_in", got_in, ns_in, _tile(d, 256))

    split = dict(given)
    for pre in ("", "m_", "v_"):
        split[pre + "b_ada_a"], split[pre + "b_ada_b"] = given[pre + "b_ada"][:, :2 * d], given[pre + "b_ada"][:, 2 * d:]
    packs = [jnp.concatenate([_pack([split[pre + n] for n in SMALL_LATE]), _pack([split[pre + n] for n in SMALL_EARLY])])
             for pre in ("", "m_", "v_")]
    rows_p = packed_parts.shape[1]
    tr_p = _tile(rows_p, 64)
    sm = _adam_sum("adam_small", packed_parts, pl.BlockSpec((NDEV, tr_p, PACK_COLS), lambda i: (0, i, 0)), *packs, tr_p)
    rows_late = late_parts.shape[1]
    small_out = []
    for t in sm:
        out = dict(zip(SMALL_LATE, _unpack(t[:rows_late], [split[n] for n in SMALL_LATE])))
        out.update(zip(SMALL_EARLY, _unpack(t[rows_late:], [split[n] for n in SMALL_EARLY])))
        out["b_ada"] = jnp.concatenate([out["b_ada_a"], out["b_ada_b"]], axis=1)
        small_out.append(out)

    rows_a, rows_b = (2 * d) // PACK_COLS, (4 * d) // PACK_COLS
    assert rows_a * PACK_COLS == 2 * d
    dmod_all = jnp.concatenate([late_parts[:, :rows_a].reshape(NDEV, 2 * d), early_parts[:, :rows_b].reshape(NDEV, 4 * d)], axis=1)
    dmod_cols = lax.dynamic_slice(dmod_all, (0, me * n_ada), (NDEV, n_ada))
    big["w_ada"] = _adam_ada(c_all, dmod_cols, w_ada[0], m_w_ada[0], v_w_ada[0])

    order = ("w_ada", "b_ada", "g_mix", "w_in", "a_re", "a_im", "log_dt", "b_re", "b_im", "c_re", "c_im", "d_skip", "w_glu",
             "q_gain", "k_gain", "g_ssm_out", "g_attn_out", "w_out", "g_ffn", "w_gate", "w_up", "w_down")
    outs = [loss, grad_x[None]]
    for kind in range(4):
        for n in order:
            outs.append(big[n][kind][None] if n in big else small_out[kind][n])
    return tuple(outs)
```

```python
import functools
import math

import jax
import jax.numpy as jnp
from jax import lax
from jax.experimental import pallas as pl
from jax.experimental.pallas import tpu as pltpu

F32 = jnp.float32
BF16 = jnp.bfloat16
NDEV = 8
MESH_AXES = ("x", "y", "c")
MESH_ID = pl.DeviceIdType.MESH
EPS = 1e-6
LANES = 128
SUBLANES = 8
HEAD_DIM = 64
SSM_GROUP = 16
ADAM_LR, ADAM_B1, ADAM_B2, ADAM_EPS, ADAM_WD, ADAM_STEP = 0.001, 0.9, 0.999, 1e-08, 0.01, 10

NN = (((1,), (0,)), ((), ()))
NT = (((1,), (1,)), ((), ()))
TN = (((0,), (0,)), ((), ()))


def _dot(a, b, dn=NN):
    return lax.dot_general(a, b, dn, preferred_element_type=F32)


def _tile(dim, pref):
    t = min(dim, pref)
    while dim % t:
        t //= 2
    return t


def _params(n):
    return pltpu.CompilerParams(dimension_semantics=("arbitrary",) * n)


def _me():
    mx, my, mc = lax.axis_index("x"), lax.axis_index("y"), lax.axis_index("c")
    return mx, my, mc


def _peer(mx, my, mc, k):
    px = 1 - mx if (k >> 2) & 1 else mx
    py = 1 - my if (k >> 1) & 1 else my
    pc = 1 - mc if k & 1 else mc
    return (px, py, pc), 4 * px + 2 * py + pc


def _exchange_copies(x_ref, land_ref, send_sems, recv_sems, gather):
    mx, my, mc = _me()
    me = 4 * mx + 2 * my + mc
    pairs = []
    for k in range(1, NDEV):
        peer, pidx = _peer(mx, my, mc, k)
        src = x_ref if gather else x_ref.at[pidx]
        mk = lambda dst, src=src, k=k, peer=peer: pltpu.make_async_remote_copy(
            src_ref=src, dst_ref=dst, send_sem=send_sems.at[k - 1], recv_sem=recv_sems.at[k - 1],
            device_id=peer, device_id_type=MESH_ID)
        pairs.append((mk(land_ref.at[me]), mk(land_ref.at[pidx])))
    return me, pairs


def _exchange(x, gather, name, relayed=False):
    def body(x_ref, o_ref, send_sems, recv_sems, local_sem):
        if relayed:
            phases = _relayed_gather_phases(x_ref, o_ref, send_sems, recv_sems, local_sem)
        else:
            phases = _direct_phases(x_ref, o_ref, send_sems, recv_sems, local_sem, gather)
        for phase in phases:
            if phase is not None:
                phase()

    return pl.pallas_call(
        body, name=name, out_shape=jax.ShapeDtypeStruct(((NDEV,) + x.shape) if gather else x.shape, x.dtype),
        in_specs=[pl.BlockSpec(memory_space=pl.ANY)], out_specs=pl.BlockSpec(memory_space=pl.ANY),
        scratch_shapes=[pltpu.SemaphoreType.DMA((NDEV - 1,)), pltpu.SemaphoreType.DMA((NDEV - 1,)), pltpu.SemaphoreType.DMA],
    )(x)


def _direct_phases(x_ref, zone, send_sems, recv_sems, local_sem, gather):
    me, pairs = _exchange_copies(x_ref, zone, send_sems, recv_sems, gather)
    local = pltpu.make_async_copy(x_ref if gather else x_ref.at[me], zone.at[me], local_sem)

    def start():
        for send, _ in pairs:
            send.start()
        local.start()

    def finish():
        for send, arrival in pairs:
            send.wait_send()
            arrival.wait_recv()
        local.wait()

    return start, None, finish


def _relayed_gather_phases(x_ref, zone, send_sems, recv_sems, local_sem):
    mx, my, mc = _me()
    me, sibling = (mx, my, mc), (mx, my, 1 - mc)
    chips = [(1 - mx, my), (mx, 1 - my), (1 - mx, 1 - my)]
    rows = lambda dev: zone.at[4 * dev[0] + 2 * dev[1] + dev[2]]

    def copy(k, block, to, src=None):
        return pltpu.make_async_remote_copy(src_ref=rows(block) if src is None else src, dst_ref=rows(block),
                                            send_sem=send_sems.at[k], recv_sem=recv_sems.at[k], device_id=to,
                                            device_id_type=MESH_ID)

    local = pltpu.make_async_copy(x_ref, rows(me), local_sem)
    first = [copy(0, me, sibling, x_ref)] + [copy(1 + j, me, (*chip, mc), x_ref) for j, chip in enumerate(chips)]
    passed = [copy(4 + j, (*chip, mc), sibling) for j, chip in enumerate(chips)]
    over_links = [copy(1 + j, (*chip, mc), me) for j, chip in enumerate(chips)]
    from_sibling = [copy(0, sibling, me)] + [copy(4 + j, (*chip, 1 - mc), me) for j, chip in enumerate(chips)]

    def start():
        local.start()
        for cp in first:
            cp.start()

    def relay():
        for arrival, onward in zip(over_links, passed):
            arrival.wait_recv()
            onward.start()

    def finish():
        for arrival in from_sibling:
            arrival.wait_recv()
        for cp in first + passed:
            cp.wait_send()
        local.wait()

    return start, relay, finish


class _Rider:
    def __init__(self, x, gather, land=None, slot=None, n_slots=None, relayed=False):
        self.x, self.gather, self.land, self.slot, self.relayed = x, gather, land, slot, relayed
        own = ((NDEV,) + x.shape) if gather else x.shape
        self.land_shape = land.shape if land is not None else (own if n_slots is None else (n_slots,) + own)

    def phases(self, x_ref, land_ref, send_sems, recv_sems, local_sem):
        zone = land_ref if self.slot is None else land_ref.at[self.slot]
        if self.relayed:
            return _relayed_gather_phases(x_ref, zone, send_sems, recv_sems, local_sem)
        return _direct_phases(x_ref, zone, send_sems, recv_sems, local_sem, self.gather)


def _ride(call_name, grid, riders, inner, in_specs, out_specs, out_shape, scratch_shapes, compiler_params, operands):
    n_in, n_out, n_scr = len(in_specs), len(out_specs), len(scratch_shapes)
    any_spec = pl.BlockSpec(memory_space=pl.ANY)
    extra_in, aliases = [], {}
    for r_idx, r in enumerate(riders):
        extra_in.append(r.x)
        if r.land is not None:
            aliases[n_in + len(extra_in)] = n_out + r_idx
            extra_in.append(r.land)
    sems = []
    for _ in riders:
        sems += [pltpu.SemaphoreType.DMA((NDEV - 1,)), pltpu.SemaphoreType.DMA((NDEV - 1,)), pltpu.SemaphoreType.DMA]

    def body(*refs):
        base_in, rest = refs[:n_in], refs[n_in:]
        rider_in, rest = rest[:len(extra_in)], rest[len(extra_in):]
        base_out, rest = rest[:n_out], rest[n_out:]
        lands, rest = rest[:len(riders)], rest[len(riders):]
        base_scr, rider_sems = rest[:n_scr], rest[n_scr:]
        step = 0
        for a, g in enumerate(grid):
            step = step * g + pl.program_id(a)
        n_steps = math.prod(grid)
        sets, pos = [], 0
        for r_idx, r in enumerate(riders):
            x_ref = rider_in[pos]
            pos += 2 if r.land is not None else 1
            sets.append(r.phases(x_ref, lands[r_idx], *rider_sems[3 * r_idx:3 * r_idx + 3]))

        if sets:
            @pl.when(step == 0)
            def _():
                for start, _, _ in sets:
                    start()

        inner(*base_in, *base_out, *base_scr)

        if any(relay is not None for _, relay, _ in sets):
            @pl.when(step == (3 * n_steps) // 5)
            def _():
                for _, relay, _ in sets:
                    if relay is not None:
                        relay()

        if sets:
            @pl.when(step == n_steps - 1)
            def _():
                for _, _, finish in sets:
                    finish()

    outs = pl.pallas_call(
        body, name=call_name, grid=grid, in_specs=list(in_specs) + [any_spec] * len(extra_in),
        out_specs=list(out_specs) + [any_spec] * len(riders),
        out_shape=list(out_shape) + [jax.ShapeDtypeStruct(r.land_shape, r.x.dtype) for r in riders],
        scratch_shapes=list(scratch_shapes) + sems, input_output_aliases=aliases, compiler_params=compiler_params,
    )(*operands, *extra_in)
    return outs[:n_out], outs[n_out:]


def _mm(name, a, b, dn, grid, a_spec, b_spec, o_spec, out_shape, out_dtype, acc_shape, riders=(), pieces=None):
    nk = grid[2]

    def body(a_ref, b_ref, o_ref, *scratch):
        if pieces is None:
            part = _dot(a_ref[...].astype(BF16), b_ref[...].astype(BF16), dn)
        else:
            part = functools.reduce(lambda p, q: p + q, [_dot(a_g.astype(BF16), b_g.astype(BF16), dn)
                                                         for a_g, b_g in pieces(a_ref, b_ref)])
        if nk == 1:
            o_ref[...] = part.astype(o_ref.dtype)
            return
        acc_ref = scratch[0]
        k = pl.program_id(2)

        @pl.when(k == 0)
        def _():
            acc_ref[...] = part

        @pl.when(k > 0)
        def _():
            acc_ref[...] += part

        @pl.when(k == nk - 1)
        def _():
            o_ref[...] = acc_ref[...].astype(o_ref.dtype)

    (out,), lands = _ride(name, grid, riders, body, [a_spec, b_spec], [o_spec], [jax.ShapeDtypeStruct(out_shape, out_dtype)],
                          [] if nk == 1 else [pltpu.VMEM(acc_shape, F32)], _params(3), [a, b])
    return (out, *lands) if riders else out


BM, BN, BK = 1024, 1024, 4096


def _mm_plain(name, a, b, dn, out_dtype):
    if dn == NN:
        (m, kk), n = a.shape, b.shape[1]
    elif dn == NT:
        (m, kk), n = a.shape, b.shape[0]
    else:
        (kk, m), n = a.shape, b.shape[1]
    half = 2 if dn == TN else 1
    bm, bn, bk = _tile(m, BM // half), _tile(n, BN // half), _tile(kk, BK)
    a_spec = pl.BlockSpec((bk, bm), lambda i, j, k: (k, i)) if dn == TN else pl.BlockSpec((bm, bk), lambda i, j, k: (i, k))
    b_spec = pl.BlockSpec((bn, bk), lambda i, j, k: (j, k)) if dn == NT else pl.BlockSpec((bk, bn), lambda i, j, k: (k, j))
    return _mm(name, a, b, dn, (m // bm, n // bn, kk // bk), a_spec, b_spec,
               pl.BlockSpec((bm, bn), lambda i, j, k: (i, j)), (m, n), out_dtype, (bm, bn))


def _row_spec(tile, width, col=0):
    return pl.BlockSpec((tile, width), lambda i: (i, col))


def _vec_spec(width, col=0):
    return pl.BlockSpec((1, width), lambda i: (0, col))


def _rowwise_fwd(name, fn, rows, row_specs, vecs, vec_specs, out_shapes, out_specs, n_tiles):
    nr, nv = len(rows), len(vecs)

    def body(*refs):
        ins = [r[...].astype(F32) for r in refs[:nr + nv]]
        outs = fn(*ins)
        for o_ref, o in zip(refs[nr + nv:], outs):
            o_ref[...] = o.astype(o_ref.dtype)

    return pl.pallas_call(body, name=name, grid=(n_tiles,), in_specs=list(row_specs) + list(vec_specs),
                          out_specs=list(out_specs), out_shape=list(out_shapes), compiler_params=_params(1))(*rows, *vecs)


def _rowwise_bwd(name, fn, rows, row_specs, vecs, vec_specs, cts, ct_specs, ct_groups,
                 drow_idx, drow_shapes, drow_specs, dvec_idx, dvec_shapes, dvec_specs, n_tiles):
    nr, nv, nc = len(rows), len(vecs), len(cts)

    def body(*refs):
        ins = [r[...].astype(F32) for r in refs[:nr + nv]]
        ct_vals = [r[...].astype(F32) for r in refs[nr + nv:nr + nv + nc]]
        out_refs = refs[nr + nv + nc:]
        _, vjp = jax.vjp(fn, *ins)
        grads = vjp(tuple(functools.reduce(lambda p, q: p + q, [ct_vals[j] for j in grp]) for grp in ct_groups))
        for o_ref, idx in zip(out_refs[:len(drow_idx)], drow_idx):
            o_ref[...] = grads[idx].astype(o_ref.dtype)
        step = pl.program_id(0)
        for o_ref, idx in zip(out_refs[len(drow_idx):], dvec_idx):
            @pl.when(step == 0)
            def _(o_ref=o_ref):
                o_ref[...] = jnp.zeros_like(o_ref)
            o_ref[...] += grads[nr + idx]

    return pl.pallas_call(body, name=name, grid=(n_tiles,),
                          in_specs=list(row_specs) + list(vec_specs) + list(ct_specs),
                          out_specs=list(drow_specs) + list(dvec_specs),
                          out_shape=list(drow_shapes) + list(dvec_shapes), compiler_params=_params(1))(*rows, *vecs, *cts)


def _rms(x):
    return x * lax.rsqrt(jnp.mean(x * x, axis=-1, keepdims=True) + EPS)


def _seg_in(x, shift, scale, gain):
    return _rms(x) * gain * (1.0 + scale) + shift, x


def _seg_qk(q, k, qg, kg):
    def norm(t, g, mult):
        blocks = []
        lane = lax.broadcasted_iota(jnp.int32, (1, LANES), 1)
        for p in range(t.shape[1] // LANES):
            tb = t[:, p * LANES:(p + 1) * LANES]
            sq = tb * tb
            lo = jnp.sum(jnp.where(lane < HEAD_DIM, sq, 0.0), axis=-1, keepdims=True)
            hi = jnp.sum(jnp.where(lane < HEAD_DIM, 0.0, sq), axis=-1, keepdims=True)
            ms = jnp.where(lane < HEAD_DIM, lo, hi) * (1.0 / HEAD_DIM)
            blocks.append(tb * lax.rsqrt(ms + EPS) * (g[:, p * LANES:(p + 1) * LANES] * mult))
        return jnp.concatenate(blocks, axis=-1) if len(blocks) > 1 else blocks[0]
    return norm(q, qg, 1.0 / math.sqrt(HEAD_DIM)), norm(k, kg, 1.0)


def _seg_gelu(ypre):
    return (jax.nn.gelu(ypre),)


def _seg_mix(y1, z, yattn, g_ssm, g_attn):
    ys = y1 * jax.nn.sigmoid(z)
    return (jnp.concatenate([_rms(ys) * g_ssm, _rms(yattn) * g_attn], axis=-1),)


def _seg_mid(x, o, gate_m, g_ffn, scale_f, shift_f):
    h1 = x + gate_m * o
    return h1, _rms(h1) * g_ffn * (1.0 + scale_f) + shift_f


def _seg_act(gate, up):
    return (jax.nn.silu(gate) * up,)


def _s5_lam(a_re, a_im, log_dt):
    dt = jnp.exp(log_dt)
    mag = jnp.exp(a_re * dt)
    lr, li = mag * jnp.cos(a_im * dt), mag * jnp.sin(a_im * dt)
    den = a_re * a_re + a_im * a_im
    nr, ni = lr - 1.0, li
    return lr, li, (nr * a_re + ni * a_im) / den, (ni * a_re - nr * a_im) / den


def _s5_bbar(coef_re, coef_im, b_re, b_im):
    return coef_re * b_re - coef_im * b_im, coef_re * b_im + coef_im * b_re


def _whole(name, fn, ins, out_shapes):
    n = len(ins)

    def body(*refs):
        outs = fn(*[r[...] for r in refs[:n]])
        for o_ref, o in zip(refs[n:], outs):
            o_ref[...] = o

    return pl.pallas_call(body, name=name, out_shape=[jax.ShapeDtypeStruct(s, F32) for s in out_shapes])(*ins)


def _whole_vjp(name, fn, ins, cts, out_shapes):
    n, nc = len(ins), len(cts)

    def body(*refs):
        _, vjp = jax.vjp(fn, *[r[...] for r in refs[:n]])
        grads = vjp(tuple(r[...] for r in refs[n:n + nc]))
        for o_ref, g in zip(refs[n + nc:], grads):
            o_ref[...] = g

    return pl.pallas_call(body, name=name, out_shape=[jax.ShapeDtypeStruct(s, F32) for s in out_shapes])(*ins, *cts)


SCAN_SHIFTS = (1, 2, 4)


def _cmul(ar, ai, br, bi):
    return ar * br - ai * bi, ar * bi + ai * br


def _scan_coefs(lr, li, reverse):
    s = lr.shape[1]
    row = lax.broadcasted_iota(jnp.int32, (SUBLANES, s), 0)
    p1 = (lr, li)
    p2 = _cmul(*p1, *p1)
    p4 = _cmul(*p2, *p2)
    p8 = _cmul(*p4, *p4)
    p3, p5, p6 = _cmul(*p1, *p2), _cmul(*p4, *p1), _cmul(*p4, *p2)
    p7 = _cmul(*p6, *p1)
    pows = (p1, p2, p3, p4, p5, p6, p7, p8)
    bc = lambda t: jnp.broadcast_to(t, (SUBLANES, s))
    steps = []
    for sh, pw in zip(SCAN_SHIFTS, (p1, p2, p4)):
        keep = (row + sh <= SUBLANES - 1) if reverse else (row >= sh)
        steps.append((jnp.where(keep, bc(pw[0]), 0.0), jnp.where(keep, bc(pw[1]), 0.0)))
    cr, ci = jnp.zeros((SUBLANES, s), F32), jnp.zeros((SUBLANES, s), F32)
    for r in range(SUBLANES):
        pw = pows[SUBLANES - 1 - r] if reverse else pows[r]
        cr = jnp.where(row == r, bc(pw[0]), cr)
        ci = jnp.where(row == r, bc(pw[1]), ci)
    return steps, (cr, ci)


def _scan_tile(xr, xi, steps, carry_pow, cr, ci, reverse):
    for sh, (ar, ai) in zip(SCAN_SHIFTS, steps):
        rs = SUBLANES - sh if reverse else sh
        sr, si = pltpu.roll(xr, rs, 0), pltpu.roll(xi, rs, 0)
        xr, xi = xr + ar * sr - ai * si, xi + ar * si + ai * sr
    pr, pi = carry_pow
    return xr + pr * cr - pi * ci, xi + pr * ci + pi * cr


def _s5_forward(proj, b_blk_re, b_blk_im, c_blk_re, c_blk_im, lam_re, lam_im, d_skip, n_blk, t_chunk, riders=()):
    seq = proj.shape[0]
    n_chunks = seq // t_chunk
    n_tiles = t_chunk // SUBLANES
    s = b_blk_re.shape[2]

    def body(u_ref, bre_ref, bim_ref, cre_ref, cim_ref, lr_ref, li_ref, d_ref, y_ref, xr_ref, xi_ref, wr, wi, carry):
        t = pl.program_id(1)

        @pl.when(t == 0)
        def _():
            carry[...] = jnp.zeros_like(carry)

        u = u_ref[...]
        ub = u.astype(BF16)
        wr[...] = _dot(ub, bre_ref[...])
        wi[...] = _dot(ub, bim_ref[...])
        steps, cpow = _scan_coefs(lr_ref[...], li_ref[...], False)

        def tile(i, c):
            r0 = pl.multiple_of(i * SUBLANES, SUBLANES)
            xr, xi = _scan_tile(wr[pl.ds(r0, SUBLANES), :], wi[pl.ds(r0, SUBLANES), :], steps, cpow, c[0], c[1], False)
            xr_ref[pl.ds(r0, SUBLANES), :] = xr
            xi_ref[pl.ds(r0, SUBLANES), :] = xi
            last = SUBLANES - 1
            return (jnp.broadcast_to(xr[last:, :], xr.shape), jnp.broadcast_to(xi[last:, :], xi.shape))

        cr, ci = lax.fori_loop(0, n_tiles, tile, (carry[0], carry[1]))
        carry[0] = cr
        carry[1] = ci
        y = _dot(xr_ref[...].astype(BF16), cre_ref[...]) - _dot(xi_ref[...].astype(BF16), cim_ref[...])
        y_ref[...] = y + d_ref[...] * u

    blk = lambda shape: pl.BlockSpec((None,) + shape, lambda j, t: (j, 0, 0))
    return _ride(
        "s5_fwd", (n_blk, n_chunks), riders, body,
        [pl.BlockSpec((t_chunk, LANES), lambda j, t: (t, j)), blk((LANES, s)), blk((LANES, s)),
         blk((s, LANES)), blk((s, LANES)), blk((1, s)), blk((1, s)), pl.BlockSpec((1, LANES), lambda j, t: (0, j))],
        [pl.BlockSpec((t_chunk, LANES), lambda j, t: (t, j)), pl.BlockSpec((t_chunk, s), lambda j, t: (t, j)),
         pl.BlockSpec((t_chunk, s), lambda j, t: (t, j))],
        [jax.ShapeDtypeStruct((seq, n_blk * LANES), F32), jax.ShapeDtypeStruct((seq, n_blk * s), F32),
         jax.ShapeDtypeStruct((seq, n_blk * s), F32)],
        [pltpu.VMEM((t_chunk, s), F32), pltpu.VMEM((t_chunk, s), F32), pltpu.VMEM((2, SUBLANES, s), F32)],
        _params(2), [proj, b_blk_re, b_blk_im, c_blk_re, c_blk_im, lam_re, lam_im, d_skip])


def _s5_backward(dypre, proj, x_re, x_im, b_blk_re, b_blk_im, c_blk_re, c_blk_im, lam_re, lam_im, d_skip, n_blk, t_chunk,
                 riders=()):
    seq = proj.shape[0]
    n_chunks = seq // t_chunk
    n_tiles = t_chunk // SUBLANES
    s = b_blk_re.shape[2]

    def body(dy_ref, u_ref, xr_ref, xi_ref, pr_ref, pi_ref, bre_ref, bim_ref, cre_ref, cim_ref, lr_ref, li_ref, d_ref,
             du_ref, dbre_ref, dbim_ref, dcre_ref, dcim_ref, dlr_ref, dli_ref, dd_ref, gr, gi, carry):
        t = pl.program_id(1)

        @pl.when(t == 0)
        def _():
            carry[...] = jnp.zeros_like(carry)
            for r in (dbre_ref, dbim_ref, dcre_ref, dcim_ref, dlr_ref, dli_ref, dd_ref):
                r[...] = jnp.zeros_like(r)

        dy = dy_ref[...]
        dyb = dy.astype(BF16)
        u = u_ref[...]
        gr[...] = _dot(dyb, cre_ref[...], NT)
        gi[...] = -_dot(dyb, cim_ref[...], NT)
        steps, cpow = _scan_coefs(lr_ref[...], -li_ref[...], True)
        row = lax.broadcasted_iota(jnp.int32, (SUBLANES, s), 0)
        last = SUBLANES - 1
        first_chunk = t == n_chunks - 1

        def tile_at(r0, prev_r, prev_i, c):
            cr, ci, ar, ai = c
            lr_, li_ = _scan_tile(gr[pl.ds(r0, SUBLANES), :], gi[pl.ds(r0, SUBLANES), :], steps, cpow, cr, ci, True)
            gr[pl.ds(r0, SUBLANES), :] = lr_
            gi[pl.ds(r0, SUBLANES), :] = li_
            xr, xi = xr_ref[pl.ds(r0, SUBLANES), :], xi_ref[pl.ds(r0, SUBLANES), :]
            xpr = jnp.where(row == 0, jnp.broadcast_to(prev_r[last:, :], xr.shape), pltpu.roll(xr, 1, 0))
            xpi = jnp.where(row == 0, jnp.broadcast_to(prev_i[last:, :], xi.shape), pltpu.roll(xi, 1, 0))
            ar = ar + lr_ * xpr + li_ * xpi
            ai = ai + li_ * xpr - lr_ * xpi
            return (jnp.broadcast_to(lr_[:1, :], lr_.shape), jnp.broadcast_to(li_[:1, :], li_.shape), ar, ai)

        def tile(ii, c):
            i = n_tiles - 1 - ii
            r0 = pl.multiple_of(i * SUBLANES, SUBLANES)
            rp = pl.multiple_of(r0 - SUBLANES, SUBLANES)
            return tile_at(r0, xr_ref[pl.ds(rp, SUBLANES), :], xi_ref[pl.ds(rp, SUBLANES), :], c)

        zero = jnp.zeros((SUBLANES, s), F32)
        c = lax.fori_loop(0, n_tiles - 1, tile, (carry[0], carry[1], zero, zero))
        keep = jnp.where(first_chunk, 0.0, 1.0)
        c = tile_at(0, pr_ref[...] * keep, pi_ref[...] * keep, c)
        carry[0] = c[0]
        carry[1] = c[1]
        dlr_ref[...] += jnp.sum(c[2], axis=0, keepdims=True)
        dli_ref[...] += jnp.sum(c[3], axis=0, keepdims=True)

        lam_r, lam_i = gr[...].astype(BF16), gi[...].astype(BF16)
        du_ref[...] = _dot(lam_r, bre_ref[...], NT) + _dot(lam_i, bim_ref[...], NT) + d_ref[...] * dy
        ub = u.astype(BF16)
        dbre_ref[...] += _dot(ub, lam_r, TN)
        dbim_ref[...] += _dot(ub, lam_i, TN)
        dcre_ref[...] += _dot(xr_ref[...].astype(BF16), dyb, TN)
        dcim_ref[...] -= _dot(xi_ref[...].astype(BF16), dyb, TN)
        dd_ref[...] += jnp.sum(dy * u, axis=0, keepdims=True)

    rev = lambda t: n_chunks - 1 - t
    blk = lambda shape: pl.BlockSpec((None,) + shape, lambda j, t: (j, 0, 0))
    tpc = t_chunk // SUBLANES
    prev_spec = pl.BlockSpec((SUBLANES, s), lambda j, t: (jnp.maximum(rev(t) * tpc - 1, 0), j))
    chunk = lambda w: pl.BlockSpec((t_chunk, w), lambda j, t: (rev(t), j))
    return _ride(
        "s5_bwd", (n_blk, n_chunks), riders, body,
        [chunk(LANES), chunk(LANES), chunk(s), chunk(s), prev_spec, prev_spec, blk((LANES, s)), blk((LANES, s)),
         blk((s, LANES)), blk((s, LANES)), blk((1, s)), blk((1, s)), pl.BlockSpec((1, LANES), lambda j, t: (0, j))],
        [chunk(LANES), blk((LANES, s)), blk((LANES, s)), blk((s, LANES)), blk((s, LANES)), blk((1, s)), blk((1, s)),
         pl.BlockSpec((1, LANES), lambda j, t: (0, j))],
        [jax.ShapeDtypeStruct((seq, n_blk * LANES), F32),
         jax.ShapeDtypeStruct((n_blk, LANES, s), F32), jax.ShapeDtypeStruct((n_blk, LANES, s), F32),
         jax.ShapeDtypeStruct((n_blk, s, LANES), F32), jax.ShapeDtypeStruct((n_blk, s, LANES), F32),
         jax.ShapeDtypeStruct((n_blk, 1, s), F32), jax.ShapeDtypeStruct((n_blk, 1, s), F32),
         jax.ShapeDtypeStruct((1, n_blk * LANES), F32)],
        [pltpu.VMEM((t_chunk, s), F32), pltpu.VMEM((t_chunk, s), F32), pltpu.VMEM((2, SUBLANES, s), F32)],
        _params(2), [dypre, proj, x_re, x_im, x_re, x_im, b_blk_re, b_blk_im, c_blk_re, c_blk_im, lam_re, lam_im, d_skip])


TQ, TK = 256, 128


def _split_bf16(x):
    hi = x.astype(BF16)
    return hi, (x - hi.astype(F32)).astype(BF16)


def _sb_weights(z, past, carry, tri):
    ls = jnp.minimum(z, 0.0) - jnp.log(1.0 + jnp.exp(-jnp.abs(z)))
    lk = ls - z
    if past is not None:
        lk = jnp.where(past, lk, 0.0)
    hi, lo = _split_bf16(lk)
    w = jnp.exp(ls + (_dot(hi, tri) + _dot(lo, tri)) + carry)
    if past is not None:
        w = jnp.where(past, w, 0.0)
    return ls, lk, w


LOG_KEEP_DEAD = -104.0


def _walk_key_blocks(i, ratio, prologue, block, epilogue, log_keep):
    n_kb = (i + 1) * ratio
    prologue(n_kb - 1)
    for n in range(ratio):
        block(n_kb - 1 - n, n % 2, True)
    n_pairs = (i * ratio) // 2

    def more(state):
        t, alive = state
        return jnp.logical_and(t < n_pairs, alive)

    def pair(state):
        t, _ = state
        j = n_kb - 1 - ratio - 2 * t
        block(j, ratio % 2, False)
        block(j - 1, (ratio + 1) % 2, False)
        return t + 1, log_keep() >= LOG_KEEP_DEAD

    done, _ = lax.while_loop(more, pair, (jnp.int32(0), log_keep() >= LOG_KEEP_DEAD))
    epilogue(n_kb - ratio - 2 * done)


def _attention_forward(qh, kh, proj, v_col, n_pair, tq, tk, riders=()):
    seq = qh.shape[0]
    ratio = tq // tk
    assert ratio % 2 == 0

    def body(q_ref, k_ref, v_ref, o_ref, q_scr, z_scr, w_scr, acc_scr, c_scr):
        i = pl.program_id(1)
        lane = lax.broadcasted_iota(jnp.int32, (1, LANES), 1)
        q2 = q_ref[...]
        q_scr[0] = jnp.where(lane < HEAD_DIM, q2, 0.0).astype(BF16)
        q_scr[1] = jnp.where(lane < HEAD_DIM, 0.0, q2).astype(BF16)
        tri = (lax.broadcasted_iota(jnp.int32, (tk, tk), 0) > lax.broadcasted_iota(jnp.int32, (tk, tk), 1)).astype(BF16)
        qpos = i * tq + lax.broadcasted_iota(jnp.int32, (tq, tk), 0)
        kidx = lax.broadcasted_iota(jnp.int32, (tq, tk), 1)

        def rows(ref, j):
            j = jnp.clip(j, 0, seq // tk - 1)
            return ref[pl.ds(pl.multiple_of(j * tk, tk), tk), :].astype(BF16)

        def scores(j, slot):
            kb = rows(k_ref, j)
            for h in range(2):
                z_scr[slot, h] = _dot(q_scr[h], kb, NT)

        def finish(j):
            vb = rows(v_ref, j)
            for h in range(2):
                acc_scr[h] += _dot(w_scr[h], vb)

        def prologue(j):
            w_scr[...] = jnp.zeros_like(w_scr)
            acc_scr[...] = jnp.zeros_like(acc_scr)
            c_scr[...] = jnp.zeros_like(c_scr)
            scores(j, 0)

        def block(j, slot, masked):
            scores(j - 1, 1 - slot)
            finish(j + 1)
            past = ((kidx + j * tk) < qpos) if masked else None
            for h in range(2):
                _, lk, w = _sb_weights(z_scr[slot, h], past, c_scr[h], tri)
                w_scr[h] = w.astype(BF16)
                c_scr[h] += jnp.sum(lk, axis=-1, keepdims=True)

        _walk_key_blocks(i, ratio, prologue, block, finish, lambda: jnp.max(c_scr[...]))
        o_ref[...] = jnp.where(lane < HEAD_DIM, acc_scr[0], acc_scr[1])

    (out,), lands = _ride(
        "attn_fwd", (n_pair, seq // tq), riders, body,
        [pl.BlockSpec((tq, LANES), lambda p, i: (i, p)), pl.BlockSpec((seq, LANES), lambda p, i: (0, p)),
         pl.BlockSpec((seq, LANES), lambda p, i: (0, v_col + p))],
        [pl.BlockSpec((tq, LANES), lambda p, i: (i, p))], [jax.ShapeDtypeStruct(qh.shape, F32)],
        [pltpu.VMEM((2, tq, LANES), BF16), pltpu.VMEM((2, 2, tq, tk), F32), pltpu.VMEM((2, tq, tk), BF16),
         pltpu.VMEM((2, tq, LANES), F32), pltpu.VMEM((2, tq, 1), F32)],
        _params(2), [qh, kh, proj])
    return out, lands


def _attention_backward(qh, kh, proj, v_col, y, dy, n_pair, tq, tk, riders=()):
    seq = qh.shape[0]

    ratio = tq // tk
    assert ratio % 2 == 0

    n_kblk = seq // tk

    def body(q_ref, k_ref, v_ref, y_ref, dy_ref, dq_ref, dk_ref, dv_ref,
             q_scr, do_scr, qt_scr, dot_scr, dkt_scr, dvt_scr, z_scr, dw_scr, w_scr, dz_scr, dq_scr, c_scr, c2_scr, tot_scr):
        i = pl.program_id(1)

        @pl.when(i == 0)
        def _():
            dkt_scr[...] = jnp.zeros_like(dkt_scr)
            dvt_scr[...] = jnp.zeros_like(dvt_scr)

        lane = lax.broadcasted_iota(jnp.int32, (1, LANES), 1)
        sel = (lane < HEAD_DIM, lane >= HEAD_DIM)
        q2, do2 = q_ref[...], dy_ref[...].astype(BF16)
        do2f = do2.astype(F32)
        dot_oy = do2f * y_ref[...]
        for h in range(2):
            qm, dm = jnp.where(sel[h], q2, 0.0), jnp.where(sel[h], do2f, 0.0)
            q_scr[h] = qm.astype(BF16)
            do_scr[h] = dm.astype(BF16)
            qt_scr[h] = qm.T.astype(BF16)
            dot_scr[h] = dm.T.astype(BF16)
            tot_scr[h] = jnp.sum(jnp.where(sel[h], dot_oy, 0.0), axis=-1, keepdims=True)
        r_i, c_i = lax.broadcasted_iota(jnp.int32, (tk, tk), 0), lax.broadcasted_iota(jnp.int32, (tk, tk), 1)
        tri = (r_i > c_i).astype(BF16)
        tri_ge = (r_i >= c_i).astype(BF16)
        qpos = i * tq + lax.broadcasted_iota(jnp.int32, (tq, tk), 0)
        kidx = lax.broadcasted_iota(jnp.int32, (tq, tk), 1)

        def start(j):
            return pl.multiple_of(jnp.clip(j, 0, seq // tk - 1) * tk, tk)

        def scores(j, slot):
            c0 = start(j)
            kb, vb = k_ref[pl.ds(c0, tk), :].astype(BF16), v_ref[pl.ds(c0, tk), :].astype(BF16)
            for h in range(2):
                z_scr[slot, h] = _dot(q_scr[h], kb, NT)
                dw_scr[slot, h] = _dot(do_scr[h], vb, NT)

        def finish(j):
            jc = jnp.clip(j, 0, n_kblk - 1)
            kb = k_ref[pl.ds(pl.multiple_of(jc * tk, tk), tk), :].astype(BF16)
            dkt_add, dvt_add = jnp.zeros((LANES, tk), F32), jnp.zeros((LANES, tk), F32)
            for h in range(2):
                dz = dz_scr[h]
                dq_scr[h] += _dot(dz, kb)
                dkt_add = dkt_add + _dot(qt_scr[h], dz)
                dvt_add = dvt_add + _dot(dot_scr[h], w_scr[h])
            dkt_scr[jc] += dkt_add
            dvt_scr[jc] += dvt_add

        def prologue(j):
            for r in (w_scr, dz_scr, dq_scr, c_scr, c2_scr):
                r[...] = jnp.zeros_like(r)
            scores(j, 0)

        def block(j, slot, masked):
            scores(j - 1, 1 - slot)
            finish(j + 1)
            past = ((kidx + j * tk) < qpos) if masked else None
            for h in range(2):
                ls, lk, w = _sb_weights(z_scr[slot, h], past, c_scr[h], tri)
                wb = w.astype(BF16)
                dlw = dw_scr[slot, h] * wb.astype(F32)
                hi, lo = _split_bf16(dlw)
                dlk = tot_scr[h] - c2_scr[h] - (_dot(hi, tri_ge) + _dot(lo, tri_ge))
                if masked:
                    dlk = jnp.where(past, dlk, 0.0)
                sig = jnp.exp(ls)
                w_scr[h] = wb
                dz_scr[h] = (dlw * (1.0 - sig) - dlk * sig).astype(BF16)
                c_scr[h] += jnp.sum(lk, axis=-1, keepdims=True)
                c2_scr[h] += jnp.sum(dlw, axis=-1, keepdims=True)

        _walk_key_blocks(i, ratio, prologue, block, finish, lambda: jnp.max(c_scr[...]))
        dq_ref[...] = jnp.where(sel[0], dq_scr[0], dq_scr[1])

        @pl.when(i == seq // tq - 1)
        def _():
            for jb in range(n_kblk):
                dk_ref[jb * tk:(jb + 1) * tk, :] = dkt_scr[jb].T
                dv_ref[jb * tk:(jb + 1) * tk, :] = dvt_scr[jb].T

    blk = pl.BlockSpec((tq, LANES), lambda p, i: (i, p))
    full = pl.BlockSpec((seq, LANES), lambda p, i: (0, p))
    shape = jax.ShapeDtypeStruct(qh.shape, F32)
    return _ride(
        "attn_bwd", (n_pair, seq // tq), riders, body,
        [blk, full, pl.BlockSpec((seq, LANES), lambda p, i: (0, v_col + p)), blk, blk],
        [blk, full, full], [shape, shape, shape],
        [pltpu.VMEM((2, tq, LANES), BF16), pltpu.VMEM((2, tq, LANES), BF16),
         pltpu.VMEM((2, LANES, tq), BF16), pltpu.VMEM((2, LANES, tq), BF16),
         pltpu.VMEM((n_kblk, LANES, tk), F32), pltpu.VMEM((n_kblk, LANES, tk), F32),
         pltpu.VMEM((2, 2, tq, tk), F32), pltpu.VMEM((2, 2, tq, tk), F32),
         pltpu.VMEM((2, tq, tk), BF16), pltpu.VMEM((2, tq, tk), BF16), pltpu.VMEM((2, tq, LANES), F32),
         pltpu.VMEM((2, tq, 1), F32), pltpu.VMEM((2, tq, 1), F32), pltpu.VMEM((2, tq, 1), F32)],
        _params(2), [qh, kh, proj, y, dy])


def _loss_head(h1, ffn, target, gate_f, tile):
    seq, d = h1.shape

    def body(h_ref, f_ref, t_ref, g_ref, dy_ref, df_ref, dg_ref, loss_ref):
        @pl.when(pl.program_id(0) == 0)
        def _():
            dg_ref[...] = jnp.zeros_like(dg_ref)
            loss_ref[...] = jnp.zeros_like(loss_ref)

        f, g = f_ref[...], g_ref[...]
        err = h_ref[...] + g * f - t_ref[...]
        dy = err * (1.0 / d)
        dy_ref[...] = dy
        df_ref[...] = (dy * g).astype(df_ref.dtype)
        dg_ref[...] += jnp.sum(dy * f, axis=0, keepdims=True)
        loss_ref[...] += jnp.sum(jnp.sum(err * err, axis=-1, keepdims=True), axis=0, keepdims=True) * (0.5 / d)

    row = _row_spec(tile, d)
    return pl.pallas_call(
        body, name="loss_head", grid=(seq // tile,), in_specs=[row, row, row, _vec_spec(d)],
        out_specs=[row, row, _vec_spec(d), pl.BlockSpec((1, 1), lambda i: (0, 0))],
        out_shape=[jax.ShapeDtypeStruct((seq, d), F32), jax.ShapeDtypeStruct((seq, d), BF16),
                   jax.ShapeDtypeStruct((1, d), F32), jax.ShapeDtypeStruct((1, 1), F32)],
        compiler_params=_params(1),
    )(h1, ffn, target, gate_f)


def _dot3(a, b, dn):
    ah, al = _split_bf16(a)
    bh, bl = _split_bf16(b)
    return _dot(ah, bh, dn) + (_dot(ah, bl, dn) + _dot(al, bh, dn))


def _ada_forward(c_all, w_shard, b_cols):
    d, n = w_shard.shape
    bk = _tile(d, 512)

    def body(c_ref, w_ref, b_ref, o_ref):
        @pl.when(pl.program_id(0) == 0)
        def _():
            o_ref[...] = jnp.broadcast_to(b_ref[...], o_ref.shape)

        o_ref[...] += _dot3(jax.nn.silu(c_ref[...]), w_ref[...], NN)

    return pl.pallas_call(
        body, name="ada_fwd", grid=(d // bk,),
        in_specs=[pl.BlockSpec((NDEV, bk), lambda k: (0, k)), pl.BlockSpec((bk, n), lambda k: (k, 0)), _vec_spec(n)],
        out_specs=pl.BlockSpec((NDEV, n), lambda k: (0, 0)), out_shape=jax.ShapeDtypeStruct((NDEV, n), F32),
        compiler_params=_params(1),
    )(c_all, w_shard, b_cols)


def _adam(w, g, m, v):
    m = ADAM_B1 * m + (1.0 - ADAM_B1) * g
    v = ADAM_B2 * v + (1.0 - ADAM_B2) * (g * g)
    m_hat = m / (1.0 - ADAM_B1 ** ADAM_STEP)
    v_hat = v / (1.0 - ADAM_B2 ** ADAM_STEP)
    return -ADAM_LR * (m_hat / (jnp.sqrt(v_hat) + ADAM_EPS) + ADAM_WD * w), m, v


def _adam_ada(c_all, dmod_cols, w, m, v):
    d, n = w.shape
    tr = _tile(d, 256)

    def body(c_ref, dm_ref, w_ref, m_ref, v_ref, g_ref, dl_ref, nm_ref, nv_ref):
        g = _dot3(jax.nn.silu(c_ref[...]), dm_ref[...], TN)
        delta, nm, nv = _adam(w_ref[...], g, m_ref[...], v_ref[...])
        g_ref[...] = g
        dl_ref[...] = delta
        nm_ref[...] = nm
        nv_ref[...] = nv

    row = _row_spec(tr, n)
    return pl.pallas_call(
        body, name="adam_ada", grid=(d // tr,),
        in_specs=[pl.BlockSpec((NDEV, tr), lambda i: (0, i)), pl.BlockSpec((NDEV, n), lambda i: (0, 0)), row, row, row],
        out_specs=[row] * 4, out_shape=[jax.ShapeDtypeStruct((d, n), F32)] * 4, compiler_params=_params(1),
    )(c_all, dmod_cols, w, m, v)


def _adam_sum(name, parts, part_spec, w, m, v, tr):
    r, c = w.shape

    def body(p_ref, w_ref, m_ref, v_ref, g_ref, dl_ref, nm_ref, nv_ref):
        g = p_ref[0].astype(F32)
        for k in range(1, NDEV):
            g = g + p_ref[k].astype(F32)
        delta, nm, nv = _adam(w_ref[...], g, m_ref[...], v_ref[...])
        g_ref[...] = g
        dl_ref[...] = delta
        nm_ref[...] = nm
        nv_ref[...] = nv

    row = _row_spec(tr, c)
    return pl.pallas_call(
        body, name=name, grid=(r // tr,), in_specs=[part_spec, row, row, row],
        out_specs=[row] * 4, out_shape=[jax.ShapeDtypeStruct((r, c), F32)] * 4, compiler_params=_params(1),
    )(parts, w, m, v)


GROUPS_PER_BLOCK = LANES // SSM_GROUP


def _to_b_blocks(bb, n_blk, p):
    t = bb.reshape(n_blk, GROUPS_PER_BLOCK, p, SSM_GROUP)
    eye = jnp.eye(GROUPS_PER_BLOCK, dtype=bb.dtype)
    return jnp.einsum("jgph,gk->jghkp", t, eye).reshape(n_blk, LANES, GROUPS_PER_BLOCK * p)


def _from_b_blocks(blk, n_blk, p):
    t = blk.reshape(n_blk, GROUPS_PER_BLOCK, SSM_GROUP, GROUPS_PER_BLOCK, p)
    eye = jnp.eye(GROUPS_PER_BLOCK, dtype=blk.dtype)
    return jnp.einsum("jghkp,gk->jgph", t, eye).reshape(n_blk * GROUPS_PER_BLOCK, p, SSM_GROUP)


def _to_c_blocks(cc, n_blk, p):
    t = cc.reshape(n_blk, GROUPS_PER_BLOCK, SSM_GROUP, p)
    eye = jnp.eye(GROUPS_PER_BLOCK, dtype=cc.dtype)
    return jnp.einsum("jghp,gk->jgpkh", t, eye).reshape(n_blk, GROUPS_PER_BLOCK * p, LANES)


def _from_c_blocks(blk, n_blk, p):
    t = blk.reshape(n_blk, GROUPS_PER_BLOCK, p, GROUPS_PER_BLOCK, SSM_GROUP)
    eye = jnp.eye(GROUPS_PER_BLOCK, dtype=blk.dtype)
    return jnp.einsum("jgpkh,gk->jghp", t, eye).reshape(n_blk * GROUPS_PER_BLOCK, SSM_GROUP, p)


SMALL_LATE = ("b_ada_a", "g_mix")
SMALL_EARLY = ("b_ada_b", "a_re", "a_im", "log_dt", "b_re", "b_im", "c_re", "c_im", "d_skip",
               "q_gain", "k_gain", "g_ssm_out", "g_attn_out", "g_ffn")
PACK_COLS = 1024


def _pack(arrs):
    flat = jnp.concatenate([a.reshape(-1) for a in arrs])
    n = flat.shape[0]
    quantum = SUBLANES * PACK_COLS
    padded = -(-n // quantum) * quantum
    return jnp.pad(flat, (0, padded - n)).reshape(padded // PACK_COLS, PACK_COLS)


def _unpack(packed, like):
    flat, out, off = packed.reshape(-1), [], 0
    for a in like:
        out.append(flat[off:off + a.size].reshape(a.shape))
        off += a.size
    return out


def kernel(x, c, w_ada, b_ada, g_mix, w_in, a_re, a_im, log_dt, b_re, b_im, c_re, c_im, d_skip, w_glu, q_gain, k_gain, g_ssm_out, g_attn_out, w_out, g_ffn, w_gate, w_up, w_down, loss_target, m_w_ada, m_b_ada, m_g_mix, m_w_in, m_a_re, m_a_im, m_log_dt, m_b_re, m_b_im, m_c_re, m_c_im, m_d_skip, m_w_glu, m_q_gain, m_k_gain, m_g_ssm_out, m_g_attn_out, m_w_out, m_g_ffn, m_w_gate, m_w_up, m_w_down, v_w_ada, v_b_ada, v_g_mix, v_w_in, v_a_re, v_a_im, v_log_dt, v_b_re, v_b_im, v_c_re, v_c_im, v_d_skip, v_w_glu, v_q_gain, v_k_gain, v_g_ssm_out, v_g_attn_out, v_w_out, v_g_ffn, v_w_gate, v_w_up, v_w_down):
    given = dict(locals())
    seq, d = x.shape[1], x.shape[2]
    xs, tgt = x[0], loss_target[0]
    n_groups, p_state = a_re.shape[1], a_re.shape[2]
    w_ssm = n_groups * SSM_GROUP
    w_attn = w_in.shape[2] * NDEV - w_ssm
    w_attn //= 3
    n_blk, n_pair = w_ssm // LANES, w_attn // LANES
    n_heads = w_attn // HEAD_DIM
    ns_in, ns_ff = w_in.shape[2], w_gate.shape[2]
    d_mix = w_ssm + w_attn
    mx, my, mc = _me()
    me = 4 * mx + 2 * my + mc
    rt = _tile(seq, 256)
    n_rt = seq // rt
    sds = jax.ShapeDtypeStruct

    c_all = _exchange(c, True, "comm_ag_c").reshape(NDEV, d)
    n_ada = w_ada.shape[2]
    b_cols = lax.dynamic_slice(b_ada, (0, me * n_ada), (1, n_ada))
    mod_cols = _ada_forward(c_all, w_ada[0], b_cols)
    mod_all = _exchange(mod_cols, True, "comm_ag_mod")
    mod = lax.dynamic_slice(mod_all, (0, me, 0), (NDEV, 1, n_ada)).reshape(1, NDEV * n_ada)
    shift_m, scale_m, gate_m, shift_f, scale_f, gate_f = [mod[:, i * d:(i + 1) * d] for i in range(6)]
    w_in_g = _exchange(w_in[0].astype(BF16), True, "comm_ag_w_in", relayed=True)

    gp = n_groups * p_state
    a_re2, a_im2, ldt2 = a_re[0], a_im[0], log_dt[0].reshape(n_groups, 1)
    b_re2, b_im2 = b_re[0].reshape(gp, SSM_GROUP), b_im[0].reshape(gp, SSM_GROUP)
    lam_r, lam_i, coef_r, coef_i = _whole("s5_lam", _s5_lam, [a_re2, a_im2, ldt2], [(n_groups, p_state)] * 4)
    coef_r2, coef_i2 = coef_r.reshape(gp, 1), coef_i.reshape(gp, 1)
    bb_r, bb_i = _whole("s5_bbar", _s5_bbar, [coef_r2, coef_i2, b_re2, b_im2], [(gp, SSM_GROUP)] * 2)
    s_blk = GROUPS_PER_BLOCK * p_state
    b_blk_r = _to_b_blocks(bb_r.reshape(n_groups, p_state, SSM_GROUP), n_blk, p_state).astype(BF16)
    b_blk_i = _to_b_blocks(bb_i.reshape(n_groups, p_state, SSM_GROUP), n_blk, p_state).astype(BF16)
    c_blk_r = _to_c_blocks(c_re[0], n_blk, p_state).astype(BF16)
    c_blk_i = _to_c_blocks(c_im[0], n_blk, p_state).astype(BF16)
    lam_r3, lam_i3 = lam_r.reshape(n_blk, 1, s_blk), lam_i.reshape(n_blk, 1, s_blk)
    d_skip2 = d_skip[0].reshape(1, w_ssm)

    row_d, vec_d = _row_spec(rt, d), _vec_spec(d)
    xm, = _rowwise_fwd("seg_in", lambda *a: _seg_in(*a)[:1], [xs], [row_d], [shift_m, scale_m, g_mix], [vec_d] * 3,
                       [sds((seq, d), BF16)], [row_d], n_rt)
    bn_in = _tile(ns_in, 512)
    per = ns_in // bn_in
    bm, bk = _tile(seq, BM), _tile(d, BK)
    proj, w_glu_g, w_out_g = _mm(
        "mm_in", xm, w_in_g, NN, (seq // bm, NDEV * per, d // bk),
        pl.BlockSpec((bm, bk), lambda i, j, k: (i, k)),
        pl.BlockSpec((None, bk, bn_in), lambda i, j, k: (j // per, k, j % per)),
        pl.BlockSpec((bm, bn_in), lambda i, j, k: (i, j)), (seq, NDEV * ns_in), F32, (bm, bn_in),
        riders=[_Rider(w_glu[0].astype(BF16), True, relayed=True), _Rider(w_out[0].astype(BF16), True, relayed=True)])
    w_glu_g, w_out_g = w_glu_g.reshape(w_ssm, w_ssm), w_out_g.reshape(d_mix, d)
    q_col, k_col, v_col = w_ssm // w_attn, w_ssm // w_attn + 1, (w_ssm + 2 * w_attn) // LANES
    qg_t, kg_t = jnp.tile(q_gain, (1, n_heads)), jnp.tile(k_gain, (1, n_heads))
    row_a, vec_a = _row_spec(rt, w_attn), _vec_spec(w_attn)
    qk_rows, qk_specs = [proj, proj], [_row_spec(rt, w_attn, q_col), _row_spec(rt, w_attn, k_col)]
    qh, kh = _rowwise_fwd("seg_qk", _seg_qk, qk_rows, qk_specs, [qg_t, kg_t], [vec_a] * 2,
                          [sds((seq, w_attn), F32)] * 2, [row_a] * 2, n_rt)
    t_chunk = _tile(seq, 512)
    (ypre, x_re, x_im), (w_gu_land,) = _s5_forward(
        proj, b_blk_r, b_blk_i, c_blk_r, c_blk_i, lam_r3, lam_i3, d_skip2, n_blk, t_chunk,
        riders=[_Rider(w_up[0].astype(BF16), True, slot=1, n_slots=2, relayed=True)])
    tq, tk = _tile(seq, TQ), _tile(seq, TK)
    y_attn, (w_gu_land,) = _attention_forward(qh, kh, proj, v_col, n_pair, tq, tk,
                                              riders=[_Rider(w_gate[0].astype(BF16), True, land=w_gu_land, slot=0,
                                                             relayed=True)])
    w_gu_g = w_gu_land.reshape(2 * NDEV, d, ns_ff)
    row_s, vec_s = _row_spec(rt, w_ssm), _vec_spec(w_ssm)
    y1, = _rowwise_fwd("seg_gelu", _seg_gelu, [ypre], [row_s], [], [], [sds((seq, w_ssm), F32)], [row_s], n_rt)
    z = _mm_plain("mm_glu", y1, w_glu_g, NN, F32)
    row_m = _row_spec(rt, d_mix)
    mixed, = _rowwise_fwd("seg_mix", _seg_mix, [y1, z, y_attn], [row_s, row_s, row_a], [g_ssm_out, g_attn_out], [vec_s, vec_a],
                          [sds((seq, d_mix), BF16)], [row_m], n_rt)
    o = _mm_plain("mm_out", mixed, w_out_g, NN, F32)
    h1, xf = _rowwise_fwd("seg_mid", _seg_mid, [xs, o], [row_d] * 2, [gate_m, g_ffn, scale_f, shift_f], [vec_d] * 4,
                          [sds((seq, d), F32), sds((seq, d), BF16)], [row_d] * 2, n_rt)
    gu, w_down_g = _mm(
        "mm_gu", xf, w_gu_g, NN, (seq // bm, 2 * NDEV, d // bk),
        pl.BlockSpec((bm, bk), lambda i, j, k: (i, k)), pl.BlockSpec((None, bk, ns_ff), lambda i, j, k: (j, k, 0)),
        pl.BlockSpec((None, bm, ns_ff), lambda i, j, k: (j, i, 0)), (2 * NDEV, seq, ns_ff), BF16, (bm, ns_ff),
        riders=[_Rider(w_down[0].astype(BF16), True, relayed=True)])
    gu4 = gu.reshape(2, NDEV, seq, ns_ff)
    ft = _tile(seq, 512)
    pair_spec = pl.BlockSpec((2, None, ft, ns_ff), lambda s, i: (0, s, i, 0))
    one_spec = pl.BlockSpec((None, ft, ns_ff), lambda s, i: (s, i, 0))

    def act_body(gu_ref, a_ref):
        a_ref[...] = _seg_act(gu_ref[0].astype(F32), gu_ref[1].astype(F32))[0].astype(a_ref.dtype)

    act = pl.pallas_call(act_body, name="seg_act", grid=(NDEV, seq // ft), in_specs=[pair_spec], out_specs=one_spec,
                         out_shape=sds((NDEV, seq, ns_ff), BF16), compiler_params=_params(2))(gu4)
    bn_d = _tile(d, BN)
    bm_h = _tile(seq, BM // 2)
    shard_pieces = lambda n: (lambda a_ref, b_ref: [(a_ref[g], b_ref[g]) for g in range(n)])
    ffn = _mm("mm_down", act, w_down_g, NN, (seq // bm_h, d // bn_d, 1),
              pl.BlockSpec((NDEV, bm_h, ns_ff), lambda i, j, k: (0, i, 0)), pl.BlockSpec((NDEV, ns_ff, bn_d), lambda i, j, k: (0, 0, j)),
              pl.BlockSpec((bm_h, bn_d), lambda i, j, k: (i, j)), (seq, d), F32, (bm_h, bn_d), pieces=shard_pieces(NDEV))
    dy, dffn, d_gate_f, loss_part = _loss_head(h1, ffn, tgt, gate_f, rt)
    loss = lax.psum(loss_part[0, 0], MESH_AXES)

    bl = _tile(seq, BK)
    gw_down = _mm("mm_dw_down", act, dffn, TN, (NDEV, d // bn_d, seq // bl),
                  pl.BlockSpec((None, bl, ns_ff), lambda i, j, k: (i, k, 0)), pl.BlockSpec((bl, bn_d), lambda i, j, k: (k, j)),
                  pl.BlockSpec((None, ns_ff, bn_d), lambda i, j, k: (i, 0, j)), (NDEV, ns_ff, d), BF16, (ns_ff, bn_d))
    dact = _mm(
        "mm_dact", dffn, w_down_g, NT, (seq // bm, NDEV, d // bk),
        pl.BlockSpec((bm, bk), lambda i, j, k: (i, k)), pl.BlockSpec((None, ns_ff, bk), lambda i, j, k: (j, 0, k)),
        pl.BlockSpec((None, bm, ns_ff), lambda i, j, k: (j, i, 0)), (NDEV, seq, ns_ff), BF16, (bm, ns_ff))

    def dact_body(gu_ref, da_ref, dgu_ref):
        _, vjp = jax.vjp(_seg_act, gu_ref[0].astype(F32), gu_ref[1].astype(F32))
        dg, du_ = vjp((da_ref[...].astype(F32),))
        dgu_ref[0] = dg.astype(dgu_ref.dtype)
        dgu_ref[1] = du_.astype(dgu_ref.dtype)

    dgu4 = pl.pallas_call(dact_body, name="seg_act_bwd", grid=(NDEV, seq // ft), in_specs=[pair_spec, one_spec],
                          out_specs=pair_spec, out_shape=sds((2, NDEV, seq, ns_ff), BF16), compiler_params=_params(2))(gu4, dact)
    dgu = dgu4.reshape(2 * NDEV, seq, ns_ff)
    bmd = _tile(d, BM)

    def dw_half(name, which, riders):
        return _mm(name, xf, dgu, TN, (d // bmd, NDEV, seq // bl), pl.BlockSpec((bl, bmd), lambda i, j, k: (k, i)),
                   pl.BlockSpec((None, bl, ns_ff), lambda i, j, k: (which * NDEV + j, k, 0)),
                   pl.BlockSpec((None, bmd, ns_ff), lambda i, j, k: (j, i, 0)), (NDEV, d, ns_ff), BF16, (bmd, ns_ff), riders=riders)

    gw_gate = dw_half("mm_dw_gate", 0, ())
    gw_up = dw_half("mm_dw_up", 1, ())
    dxf, got_down = _mm(
        "mm_dxf", dgu, w_gu_g, NT, (seq // bm, d // bn_d, 4),
        pl.BlockSpec((4, bm, ns_ff), lambda i, j, k: (k, i, 0)), pl.BlockSpec((4, bn_d, ns_ff), lambda i, j, k: (k, j, 0)),
        pl.BlockSpec((bm, bn_d), lambda i, j, k: (i, j)), (seq, d), F32, (bm, bn_d), riders=[_Rider(gw_down, False)],
        pieces=shard_pieces(4))
    (do, dx_a, d_gate_m, d_g_ffn, d_scale_f, d_shift_f) = _rowwise_bwd(
        "seg_mid_bwd", _seg_mid, [xs, o], [row_d] * 2, [gate_m, g_ffn, scale_f, shift_f], [vec_d] * 4,
        [dy, dxf], [row_d] * 2, [[0], [1]], [1, 0], [sds((seq, d), BF16), sds((seq, d), F32)], [row_d] * 2,
        [0, 1, 2, 3], [sds((1, d), F32)] * 4, [vec_d] * 4, n_rt)

    dmixed = _mm_plain("mm_dmixed", do, w_out_g, NT, F32)
    gw_out = _mm_plain("mm_dw_out", mixed, do, TN, BF16)
    (dz, dy1_a, dy_attn, d_g_ssm, d_g_attn) = _rowwise_bwd(
        "seg_mix_bwd", _seg_mix, [y1, z, y_attn], [row_s, row_s, row_a], [g_ssm_out, g_attn_out], [vec_s, vec_a],
        [dmixed], [row_m], [[0]], [1, 0, 2], [sds((seq, w_ssm), BF16), sds((seq, w_ssm), F32), sds((seq, w_attn), F32)],
        [row_s, row_s, row_a], [0, 1], [sds((1, w_ssm), F32), sds((1, w_attn), F32)], [vec_s, vec_a], n_rt)
    dy1_b = _mm_plain("mm_dy1", dz, w_glu_g, NT, F32)
    gw_glu = _mm_plain("mm_dw_glu", y1, dz, TN, BF16)
    (dypre,) = _rowwise_bwd("seg_gelu_bwd", _seg_gelu, [ypre], [row_s], [], [], [dy1_a, dy1_b], [row_s] * 2, [[0, 1]],
                            [0], [sds((seq, w_ssm), F32)], [row_s], [], [], [], n_rt)
    (du, db_blk_r, db_blk_i, dc_blk_r, dc_blk_i, dlam_r3, dlam_i3, dd_skip2), (got_gate,) = _s5_backward(
        dypre, proj, x_re, x_im, b_blk_r, b_blk_i, c_blk_r, c_blk_i, lam_r3, lam_i3, d_skip2, n_blk, t_chunk,
        riders=[_Rider(gw_gate, False)])
    (dqh, dkh, dv), (got_up, got_out, got_glu) = _attention_backward(
        qh, kh, proj, v_col, y_attn, dy_attn, n_pair, tq, tk,
        riders=[_Rider(gw_up, False), _Rider(gw_out.reshape(NDEV, w_out.shape[1], d), False),
                _Rider(gw_glu.reshape(NDEV, w_glu.shape[1], w_ssm), False)])
    (dq, dk, dqg_t, dkg_t) = _rowwise_bwd(
        "seg_qk_bwd", _seg_qk, qk_rows, qk_specs, [qg_t, kg_t], [vec_a] * 2, [dqh, dkh], [row_a] * 2, [[0], [1]],
        [0, 1], [sds((seq, w_attn), BF16)] * 2, [row_a] * 2, [0, 1], [sds((1, w_attn), F32)] * 2, [vec_a] * 2, n_rt)

    dbb_r = _from_b_blocks(db_blk_r, n_blk, p_state).reshape(gp, SSM_GROUP)
    dbb_i = _from_b_blocks(db_blk_i, n_blk, p_state).reshape(gp, SSM_GROUP)
    dcoef_r2, dcoef_i2, db_re2, db_im2 = _whole_vjp("s5_bbar_bwd", _s5_bbar, [coef_r2, coef_i2, b_re2, b_im2], [dbb_r, dbb_i],
                                                    [(gp, 1), (gp, 1), (gp, SSM_GROUP), (gp, SSM_GROUP)])
    lam_cts = [dlam_r3.reshape(n_groups, p_state), dlam_i3.reshape(n_groups, p_state),
               dcoef_r2.reshape(n_groups, p_state), dcoef_i2.reshape(n_groups, p_state)]
    da_re2, da_im2, dldt2 = _whole_vjp("s5_lam_bwd", _s5_lam, [a_re2, a_im2, ldt2], lam_cts,
                                       [(n_groups, p_state), (n_groups, p_state), (n_groups, 1)])
    dc_re2, dc_im2 = _from_c_blocks(dc_blk_r, n_blk, p_state), _from_c_blocks(dc_blk_i, n_blk, p_state)

    small_part = {
        "b_ada_b": jnp.concatenate([d_gate_m, d_shift_f, d_scale_f, d_gate_f], axis=-1),
        "a_re": da_re2, "a_im": da_im2, "log_dt": dldt2, "b_re": db_re2, "b_im": db_im2,
        "c_re": dc_re2, "c_im": dc_im2, "d_skip": dd_skip2,
        "q_gain": dqg_t.reshape(n_heads, HEAD_DIM).sum(0), "k_gain": dkg_t.reshape(n_heads, HEAD_DIM).sum(0),
        "g_ssm_out": d_g_ssm, "g_attn_out": d_g_attn, "g_ffn": d_g_ffn,
    }
    dproj = jnp.concatenate([du.astype(BF16), dq, dk, dv.astype(BF16)], axis=-1)
    gw_in, early_parts = _mm(
        "mm_dw_in", xm, dproj, TN, (d // bmd, NDEV * per, seq // bl),
        pl.BlockSpec((bl, bmd), lambda i, j, k: (k, i)), pl.BlockSpec((bl, bn_in), lambda i, j, k: (k, j)),
        pl.BlockSpec((None, bmd, bn_in), lambda i, j, k: (j // per, i, j % per)), (NDEV, d, ns_in), BF16, (bmd, bn_in),
        riders=[_Rider(_pack([small_part[n] for n in SMALL_EARLY]), True, relayed=True)])
    dxm, got_in = _mm(
        "mm_dxm", dproj, w_in_g, NT, (seq // bm, d // bn_d, 1),
        pl.BlockSpec((bm, NDEV * ns_in), lambda i, j, k: (i, 0)), pl.BlockSpec((NDEV, bn_d, ns_in), lambda i, j, k: (0, j, 0)),
        pl.BlockSpec((bm, bn_d), lambda i, j, k: (i, j)), (seq, d), F32, (bm, bn_d), riders=[_Rider(gw_in, False)],
        pieces=lambda a_ref, b_ref: [(a_ref[:, g * ns_in:(g + 1) * ns_in], b_ref[g]) for g in range(NDEV)])
    (grad_x, d_shift_m, d_scale_m, d_g_mix) = _rowwise_bwd(
        "seg_in_bwd", _seg_in, [xs], [row_d], [shift_m, scale_m, g_mix], [vec_d] * 3, [dxm, dx_a], [row_d] * 2, [[0], [1]],
        [0], [sds((seq, d), F32)], [row_d], [0, 1, 2], [sds((1, d), F32)] * 3, [vec_d] * 3, n_rt)
    small_part["b_ada_a"] = jnp.concatenate([d_shift_m, d_scale_m], axis=-1)
    small_part["g_mix"] = d_g_mix
    late_parts = _exchange(_pack([small_part[n] for n in SMALL_LATE]), True, "comm_ag_small_late")
    packed_parts = jnp.concatenate([late_parts, early_parts], axis=1)

    big = {}

    def sharded(nm, got, width, tr):
        big[nm] = _adam_sum("adam_" + nm, got, pl.BlockSpec((NDEV, tr, width), lambda i: (0, i, 0)), given[nm][0],
                            given["m_" + nm][0], given["v_" + nm][0], tr)

    sharded("w_down", got_down, d, _tile(w_down.shape[1], 64))
    sharded("w_gate", got_gate, ns_ff, _tile(d, 256))
    sharded("w_up", got_up, ns_ff, _tile(d, 256))
    sharded("w_out", got_out, d, _tile(w_out.shape[1], 128))
    sharded("w_glu", got_glu, w_ssm, _tile(w_glu.shape[1], 128))
    sharded("w_in", got_in, ns_in, _tile(d, 256))

    split = dict(given)
    for pre in ("", "m_", "v_"):
        split[pre + "b_ada_a"], split[pre + "b_ada_b"] = given[pre + "b_ada"][:, :2 * d], given[pre + "b_ada"][:, 2 * d:]
    packs = [jnp.concatenate([_pack([split[pre + n] for n in SMALL_LATE]), _pack([split[pre + n] for n in SMALL_EARLY])])
             for pre in ("", "m_", "v_")]
    rows_p = packed_parts.shape[1]
    tr_p = _tile(rows_p, 64)
    sm = _adam_sum("adam_small", packed_parts, pl.BlockSpec((NDEV, tr_p, PACK_COLS), lambda i: (0, i, 0)), *packs, tr_p)
    rows_late = late_parts.shape[1]
    small_out = []
    for t in sm:
        out = dict(zip(SMALL_LATE, _unpack(t[:rows_late], [split[n] for n in SMALL_LATE])))
        out.update(zip(SMALL_EARLY, _unpack(t[rows_late:], [split[n] for n in SMALL_EARLY])))
        out["b_ada"] = jnp.concatenate([out["b_ada_a"], out["b_ada_b"]], axis=1)
        small_out.append(out)

    rows_a, rows_b = (2 * d) // PACK_COLS, (4 * d) // PACK_COLS
    assert rows_a * PACK_COLS == 2 * d
    dmod_all = jnp.concatenate([late_parts[:, :rows_a].reshape(NDEV, 2 * d), early_parts[:, :rows_b].reshape(NDEV, 4 * d)], axis=1)
    dmod_cols = lax.dynamic_slice(dmod_all, (0, me * n_ada), (NDEV, n_ada))
    big["w_ada"] = _adam_ada(c_all, dmod_cols, w_ada[0], m_w_ada[0], v_w_ada[0])

    order = ("w_ada", "b_ada", "g_mix", "w_in", "a_re", "a_im", "log_dt", "b_re", "b_im", "c_re", "c_im", "d_skip", "w_glu",
             "q_gain", "k_gain", "g_ssm_out", "g_attn_out", "w_out", "g_ffn", "w_gate", "w_up", "w_down")
    outs = [loss, grad_x[None]]
    for kind in range(4):
        for n in order:
            outs.append(big[n][kind][None] if n in big else small_out[kind][n])
    return tuple(outs)
```

```python
import functools
import math

import jax
import jax.numpy as jnp
from jax import lax
from jax.experimental import pallas as pl
from jax.experimental.pallas import tpu as pltpu

F32 = jnp.float32
BF16 = jnp.bfloat16
NDEV = 8
MESH_AXES = ("x", "y", "c")
MESH_ID = pl.DeviceIdType.MESH
EPS = 1e-6
LANES = 128
SUBLANES = 8
HEAD_DIM = 64
SSM_GROUP = 16
ADAM_LR, ADAM_B1, ADAM_B2, ADAM_EPS, ADAM_WD, ADAM_STEP = 0.001, 0.9, 0.999, 1e-08, 0.01, 10

NN = (((1,), (0,)), ((), ()))
NT = (((1,), (1,)), ((), ()))
TN = (((0,), (0,)), ((), ()))


def _dot(a, b, dn=NN):
    return lax.dot_general(a, b, dn, preferred_element_type=F32)


def _tile(dim, pref):
    t = min(dim, pref)
    while dim % t:
        t //= 2
    return t


def _params(n):
    return pltpu.CompilerParams(dimension_semantics=("arbitrary",) * n)


def _me():
    mx, my, mc = lax.axis_index("x"), lax.axis_index("y"), lax.axis_index("c")
    return mx, my, mc


def _peer(mx, my, mc, k):
    px = 1 - mx if (k >> 2) & 1 else mx
    py = 1 - my if (k >> 1) & 1 else my
    pc = 1 - mc if k & 1 else mc
    return (px, py, pc), 4 * px + 2 * py + pc


def _exchange_copies(x_ref, land_ref, send_sems, recv_sems, gather):
    mx, my, mc = _me()
    me = 4 * mx + 2 * my + mc
    pairs = []
    for k in range(1, NDEV):
        peer, pidx = _peer(mx, my, mc, k)
        src = x_ref if gather else x_ref.at[pidx]
        mk = lambda dst, src=src, k=k, peer=peer: pltpu.make_async_remote_copy(
            src_ref=src, dst_ref=dst, send_sem=send_sems.at[k - 1], recv_sem=recv_sems.at[k - 1],
            device_id=peer, device_id_type=MESH_ID)
        pairs.append((mk(land_ref.at[me]), mk(land_ref.at[pidx])))
    return me, pairs


def _exchange(x, gather, name, relayed=False):
    def body(x_ref, o_ref, send_sems, recv_sems, local_sem):
        if relayed:
            phases = _relayed_gather_phases(x_ref, o_ref, send_sems, recv_sems, local_sem)
        else:
            phases = _direct_phases(x_ref, o_ref, send_sems, recv_sems, local_sem, gather)
        for phase in phases:
            if phase is not None:
                phase()

    return pl.pallas_call(
        body, name=name, out_shape=jax.ShapeDtypeStruct(((NDEV,) + x.shape) if gather else x.shape, x.dtype),
        in_specs=[pl.BlockSpec(memory_space=pl.ANY)], out_specs=pl.BlockSpec(memory_space=pl.ANY),
        scratch_shapes=[pltpu.SemaphoreType.DMA((NDEV - 1,)), pltpu.SemaphoreType.DMA((NDEV - 1,)), pltpu.SemaphoreType.DMA],
    )(x)


def _direct_phases(x_ref, zone, send_sems, recv_sems, local_sem, gather):
    me, pairs = _exchange_copies(x_ref, zone, send_sems, recv_sems, gather)
    local = pltpu.make_async_copy(x_ref if gather else x_ref.at[me], zone.at[me], local_sem)

    def start():
        for send, _ in pairs:
            send.start()
        local.start()

    def finish():
        for send, arrival in pairs:
            send.wait_send()
            arrival.wait_recv()
        local.wait()

    return start, None, finish


def _relayed_gather_phases(x_ref, zone, send_sems, recv_sems, local_sem):
    mx, my, mc = _me()
    me, sibling = (mx, my, mc), (mx, my, 1 - mc)
    chips = [(1 - mx, my), (mx, 1 - my), (1 - mx, 1 - my)]
    rows = lambda dev: zone.at[4 * dev[0] + 2 * dev[1] + dev[2]]

    def copy(k, block, to, src=None):
        return pltpu.make_async_remote_copy(src_ref=rows(block) if src is None else src, dst_ref=rows(block),
                                            send_sem=send_sems.at[k], recv_sem=recv_sems.at[k], device_id=to,
                                            device_id_type=MESH_ID)

    local = pltpu.make_async_copy(x_ref, rows(me), local_sem)
    first = [copy(0, me, sibling, x_ref)] + [copy(1 + j, me, (*chip, mc), x_ref) for j, chip in enumerate(chips)]
    passed = [copy(4 + j, (*chip, mc), sibling) for j, chip in enumerate(chips)]
    over_links = [copy(1 + j, (*chip, mc), me) for j, chip in enumerate(chips)]
    from_sibling = [copy(0, sibling, me)] + [copy(4 + j, (*chip, 1 - mc), me) for j, chip in enumerate(chips)]

    def start():
        local.start()
        for cp in first:
            cp.start()

    def relay():
        for arrival, onward in zip(over_links, passed):
            arrival.wait_recv()
            onward.start()

    def finish():
        for arrival in from_sibling:
            arrival.wait_recv()
        for cp in first + passed:
            cp.wait_send()
        local.wait()

    return start, relay, finish


class _Rider:
    def __init__(self, x, gather, land=None, slot=None, n_slots=None, relayed=False):
        self.x, self.gather, self.land, self.slot, self.relayed = x, gather, land, slot, relayed
        own = ((NDEV,) + x.shape) if gather else x.shape
        self.land_shape = land.shape if land is not None else (own if n_slots is None else (n_slots,) + own)

    def phases(self, x_ref, land_ref, send_sems, recv_sems, local_sem):
        zone = land_ref if self.slot is None else land_ref.at[self.slot]
        if self.relayed:
            return _relayed_gather_phases(x_ref, zone, send_sems, recv_sems, local_sem)
        return _direct_phases(x_ref, zone, send_sems, recv_sems, local_sem, self.gather)


def _ride(call_name, grid, riders, inner, in_specs, out_specs, out_shape, scratch_shapes, compiler_params, operands):
    n_in, n_out, n_scr = len(in_specs), len(out_specs), len(scratch_shapes)
    any_spec = pl.BlockSpec(memory_space=pl.ANY)
    extra_in, aliases = [], {}
    for r_idx, r in enumerate(riders):
        extra_in.append(r.x)
        if r.land is not None:
            aliases[n_in + len(extra_in)] = n_out + r_idx
            extra_in.append(r.land)
    sems = []
    for _ in riders:
        sems += [pltpu.SemaphoreType.DMA((NDEV - 1,)), pltpu.SemaphoreType.DMA((NDEV - 1,)), pltpu.SemaphoreType.DMA]

    def body(*refs):
        base_in, rest = refs[:n_in], refs[n_in:]
        rider_in, rest = rest[:len(extra_in)], rest[len(extra_in):]
        base_out, rest = rest[:n_out], rest[n_out:]
        lands, rest = rest[:len(riders)], rest[len(riders):]
        base_scr, rider_sems = rest[:n_scr], rest[n_scr:]
        step = 0
        for a, g in enumerate(grid):
            step = step * g + pl.program_id(a)
        n_steps = math.prod(grid)
        sets, pos = [], 0
        for r_idx, r in enumerate(riders):
            x_ref = rider_in[pos]
            pos += 2 if r.land is not None else 1
            sets.append(r.phases(x_ref, lands[r_idx], *rider_sems[3 * r_idx:3 * r_idx + 3]))

        if sets:
            @pl.when(step == 0)
            def _():
                for start, _, _ in sets:
                    start()

        inner(*base_in, *base_out, *base_scr)

        if any(relay is not None for _, relay, _ in sets):
            @pl.when(step == (3 * n_steps) // 5)
            def _():
                for _, relay, _ in sets:
                    if relay is not None:
                        relay()

        if sets:
            @pl.when(step == n_steps - 1)
            def _():
                for _, _, finish in sets:
                    finish()

    outs = pl.pallas_call(
        body, name=call_name, grid=grid, in_specs=list(in_specs) + [any_spec] * len(extra_in),
        out_specs=list(out_specs) + [any_spec] * len(riders),
        out_shape=list(out_shape) + [jax.ShapeDtypeStruct(r.land_shape, r.x.dtype) for r in riders],
        scratch_shapes=list(scratch_shapes) + sems, input_output_aliases=aliases, compiler_params=compiler_params,
    )(*operands, *extra_in)
    return outs[:n_out], outs[n_out:]


def _mm(name, a, b, dn, grid, a_spec, b_spec, o_spec, out_shape, out_dtype, acc_shape, riders=(), pieces=None):
    nk = grid[2]

    def body(a_ref, b_ref, o_ref, *scratch):
        if pieces is None:
            part = _dot(a_ref[...].astype(BF16), b_ref[...].astype(BF16), dn)
        else:
            part = functools.reduce(lambda p, q: p + q, [_dot(a_g.astype(BF16), b_g.astype(BF16), dn)
                                                         for a_g, b_g in pieces(a_ref, b_ref)])
        if nk == 1:
            o_ref[...] = part.astype(o_ref.dtype)
            return
        acc_ref = scratch[0]
        k = pl.program_id(2)

        @pl.when(k == 0)
        def _():
            acc_ref[...] = part

        @pl.when(k > 0)
        def _():
            acc_ref[...] += part

        @pl.when(k == nk - 1)
        def _():
            o_ref[...] = acc_ref[...].astype(o_ref.dtype)

    (out,), lands = _ride(name, grid, riders, body, [a_spec, b_spec], [o_spec], [jax.ShapeDtypeStruct(out_shape, out_dtype)],
                          [] if nk == 1 else [pltpu.VMEM(acc_shape, F32)], _params(3), [a, b])
    return (out, *lands) if riders else out


BM, BN, BK = 1024, 1024, 4096


def _mm_plain(name, a, b, dn, out_dtype):
    if dn == NN:
        (m, kk), n = a.shape, b.shape[1]
    elif dn == NT:
        (m, kk), n = a.shape, b.shape[0]
    else:
        (kk, m), n = a.shape, b.shape[1]
    half = 2 if dn == TN else 1
    bm, bn, bk = _tile(m, BM // half), _tile(n, BN // half), _tile(kk, BK)
    a_spec = pl.BlockSpec((bk, bm), lambda i, j, k: (k, i)) if dn == TN else pl.BlockSpec((bm, bk), lambda i, j, k: (i, k))
    b_spec = pl.BlockSpec((bn, bk), lambda i, j, k: (j, k)) if dn == NT else pl.BlockSpec((bk, bn), lambda i, j, k: (k, j))
    return _mm(name, a, b, dn, (m // bm, n // bn, kk // bk), a_spec, b_spec,
               pl.BlockSpec((bm, bn), lambda i, j, k: (i, j)), (m, n), out_dtype, (bm, bn))


def _row_spec(tile, width, col=0):
    return pl.BlockSpec((tile, width), lambda i: (i, col))


def _vec_spec(width, col=0):
    return pl.BlockSpec((1, width), lambda i: (0, col))


def _rowwise_fwd(name, fn, rows, row_specs, vecs, vec_specs, out_shapes, out_specs, n_tiles):
    nr, nv = len(rows), len(vecs)

    def body(*refs):
        ins = [r[...].astype(F32) for r in refs[:nr + nv]]
        outs = fn(*ins)
        for o_ref, o in zip(refs[nr + nv:], outs):
            o_ref[...] = o.astype(o_ref.dtype)

    return pl.pallas_call(body, name=name, grid=(n_tiles,), in_specs=list(row_specs) + list(vec_specs),
                          out_specs=list(out_specs), out_shape=list(out_shapes), compiler_params=_params(1))(*rows, *vecs)


def _rowwise_bwd(name, fn, rows, row_specs, vecs, vec_specs, cts, ct_specs, ct_groups,
                 drow_idx, drow_shapes, drow_specs, dvec_idx, dvec_shapes, dvec_specs, n_tiles):
    nr, nv, nc = len(rows), len(vecs), len(cts)

    def body(*refs):
        ins = [r[...].astype(F32) for r in refs[:nr + nv]]
        ct_vals = [r[...].astype(F32) for r in refs[nr + nv:nr + nv + nc]]
        out_refs = refs[nr + nv + nc:]
        _, vjp = jax.vjp(fn, *ins)
        grads = vjp(tuple(functools.reduce(lambda p, q: p + q, [ct_vals[j] for j in grp]) for grp in ct_groups))
        for o_ref, idx in zip(out_refs[:len(drow_idx)], drow_idx):
            o_ref[...] = grads[idx].astype(o_ref.dtype)
        step = pl.program_id(0)
        for o_ref, idx in zip(out_refs[len(drow_idx):], dvec_idx):
            @pl.when(step == 0)
            def _(o_ref=o_ref):
                o_ref[...] = jnp.zeros_like(o_ref)
            o_ref[...] += grads[nr + idx]

    return pl.pallas_call(body, name=name, grid=(n_tiles,),
                          in_specs=list(row_specs) + list(vec_specs) + list(ct_specs),
                          out_specs=list(drow_specs) + list(dvec_specs),
                          out_shape=list(drow_shapes) + list(dvec_shapes), compiler_params=_params(1))(*rows, *vecs, *cts)


def _rms(x):
    return x * lax.rsqrt(jnp.mean(x * x, axis=-1, keepdims=True) + EPS)


def _seg_in(x, shift, scale, gain):
    return _rms(x) * gain * (1.0 + scale) + shift, x


def _seg_qk(q, k, qg, kg):
    def norm(t, g, mult):
        blocks = []
        lane = lax.broadcasted_iota(jnp.int32, (1, LANES), 1)
        for p in range(t.shape[1] // LANES):
            tb = t[:, p * LANES:(p + 1) * LANES]
            sq = tb * tb
            lo = jnp.sum(jnp.where(lane < HEAD_DIM, sq, 0.0), axis=-1, keepdims=True)
            hi = jnp.sum(jnp.where(lane < HEAD_DIM, 0.0, sq), axis=-1, keepdims=True)
            ms = jnp.where(lane < HEAD_DIM, lo, hi) * (1.0 / HEAD_DIM)
            blocks.append(tb * lax.rsqrt(ms + EPS) * (g[:, p * LANES:(p + 1) * LANES] * mult))
        return jnp.concatenate(blocks, axis=-1) if len(blocks) > 1 else blocks[0]
    return norm(q, qg, 1.0 / math.sqrt(HEAD_DIM)), norm(k, kg, 1.0)


def _seg_gelu(ypre):
    return (jax.nn.gelu(ypre),)


def _seg_mix(y1, z, yattn, g_ssm, g_attn):
    ys = y1 * jax.nn.sigmoid(z)
    return (jnp.concatenate([_rms(ys) * g_ssm, _rms(yattn) * g_attn], axis=-1),)


def _seg_mid(x, o, gate_m, g_ffn, scale_f, shift_f):
    h1 = x + gate_m * o
    return h1, _rms(h1) * g_ffn * (1.0 + scale_f) + shift_f


def _seg_act(gate, up):
    return (jax.nn.silu(gate) * up,)


def _s5_lam(a_re, a_im, log_dt):
    dt = jnp.exp(log_dt)
    mag = jnp.exp(a_re * dt)
    lr, li = mag * jnp.cos(a_im * dt), mag * jnp.sin(a_im * dt)
    den = a_re * a_re + a_im * a_im
    nr, ni = lr - 1.0, li
    return lr, li, (nr * a_re + ni * a_im) / den, (ni * a_re - nr * a_im) / den


def _s5_bbar(coef_re, coef_im, b_re, b_im):
    return coef_re * b_re - coef_im * b_im, coef_re * b_im + coef_im * b_re


def _whole(name, fn, ins, out_shapes):
    n = len(ins)

    def body(*refs):
        outs = fn(*[r[...] for r in refs[:n]])
        for o_ref, o in zip(refs[n:], outs):
            o_ref[...] = o

    return pl.pallas_call(body, name=name, out_shape=[jax.ShapeDtypeStruct(s, F32) for s in out_shapes])(*ins)


def _whole_vjp(name, fn, ins, cts, out_shapes):
    n, nc = len(ins), len(cts)

    def body(*refs):
        _, vjp = jax.vjp(fn, *[r[...] for r in refs[:n]])
        grads = vjp(tuple(r[...] for r in refs[n:n + nc]))
        for o_ref, g in zip(refs[n + nc:], grads):
            o_ref[...] = g

    return pl.pallas_call(body, name=name, out_shape=[jax.ShapeDtypeStruct(s, F32) for s in out_shapes])(*ins, *cts)


SCAN_SHIFTS = (1, 2, 4)


def _cmul(ar, ai, br, bi):
    return ar * br - ai * bi, ar * bi + ai * br


def _scan_coefs(lr, li, reverse):
    s = lr.shape[1]
    row = lax.broadcasted_iota(jnp.int32, (SUBLANES, s), 0)
    p1 = (lr, li)
    p2 = _cmul(*p1, *p1)
    p4 = _cmul(*p2, *p2)
    p8 = _cmul(*p4, *p4)
    p3, p5, p6 = _cmul(*p1, *p2), _cmul(*p4, *p1), _cmul(*p4, *p2)
    p7 = _cmul(*p6, *p1)
    pows = (p1, p2, p3, p4, p5, p6, p7, p8)
    bc = lambda t: jnp.broadcast_to(t, (SUBLANES, s))
    steps = []
    for sh, pw in zip(SCAN_SHIFTS, (p1, p2, p4)):
        keep = (row + sh <= SUBLANES - 1) if reverse else (row >= sh)
        steps.append((jnp.where(keep, bc(pw[0]), 0.0), jnp.where(keep, bc(pw[1]), 0.0)))
    cr, ci = jnp.zeros((SUBLANES, s), F32), jnp.zeros((SUBLANES, s), F32)
    for r in range(SUBLANES):
        pw = pows[SUBLANES - 1 - r] if reverse else pows[r]
        cr = jnp.where(row == r, bc(pw[0]), cr)
        ci = jnp.where(row == r, bc(pw[1]), ci)
    return steps, (cr, ci)


def _scan_tile(xr, xi, steps, carry_pow, cr, ci, reverse):
    for sh, (ar, ai) in zip(SCAN_SHIFTS, steps):
        rs = SUBLANES - sh if reverse else sh
        sr, si = pltpu.roll(xr, rs, 0), pltpu.roll(xi, rs, 0)
        xr, xi = xr + ar * sr - ai * si, xi + ar * si + ai * sr
    pr, pi = carry_pow
    return xr + pr * cr - pi * ci, xi + pr * ci + pi * cr


def _s5_forward(proj, b_blk_re, b_blk_im, c_blk_re, c_blk_im, lam_re, lam_im, d_skip, n_blk, t_chunk, riders=()):
    seq = proj.shape[0]
    n_chunks = seq // t_chunk
    n_tiles = t_chunk // SUBLANES
    s = b_blk_re.shape[2]

    def body(u_ref, bre_ref, bim_ref, cre_ref, cim_ref, lr_ref, li_ref, d_ref, y_ref, xr_ref, xi_ref, wr, wi, carry):
        t = pl.program_id(1)

        @pl.when(t == 0)
        def _():
            carry[...] = jnp.zeros_like(carry)

        u = u_ref[...]
        ub = u.astype(BF16)
        wr[...] = _dot(ub, bre_ref[...])
        wi[...] = _dot(ub, bim_ref[...])
        steps, cpow = _scan_coefs(lr_ref[...], li_ref[...], False)

        def tile(i, c):
            r0 = pl.multiple_of(i * SUBLANES, SUBLANES)
            xr, xi = _scan_tile(wr[pl.ds(r0, SUBLANES), :], wi[pl.ds(r0, SUBLANES), :], steps, cpow, c[0], c[1], False)
            xr_ref[pl.ds(r0, SUBLANES), :] = xr
            xi_ref[pl.ds(r0, SUBLANES), :] = xi
            last = SUBLANES - 1
            return (jnp.broadcast_to(xr[last:, :], xr.shape), jnp.broadcast_to(xi[last:, :], xi.shape))

        cr, ci = lax.fori_loop(0, n_tiles, tile, (carry[0], carry[1]))
        carry[0] = cr
        carry[1] = ci
        y = _dot(xr_ref[...].astype(BF16), cre_ref[...]) - _dot(xi_ref[...].astype(BF16), cim_ref[...])
        y_ref[...] = y + d_ref[...] * u

    blk = lambda shape: pl.BlockSpec((None,) + shape, lambda j, t: (j, 0, 0))
    return _ride(
        "s5_fwd", (n_blk, n_chunks), riders, body,
        [pl.BlockSpec((t_chunk, LANES), lambda j, t: (t, j)), blk((LANES, s)), blk((LANES, s)),
         blk((s, LANES)), blk((s, LANES)), blk((1, s)), blk((1, s)), pl.BlockSpec((1, LANES), lambda j, t: (0, j))],
        [pl.BlockSpec((t_chunk, LANES), lambda j, t: (t, j)), pl.BlockSpec((t_chunk, s), lambda j, t: (t, j)),
         pl.BlockSpec((t_chunk, s), lambda j, t: (t, j))],
        [jax.ShapeDtypeStruct((seq, n_blk * LANES), F32), jax.ShapeDtypeStruct((seq, n_blk * s), F32),
         jax.ShapeDtypeStruct((seq, n_blk * s), F32)],
        [pltpu.VMEM((t_chunk, s), F32), pltpu.VMEM((t_chunk, s), F32), pltpu.VMEM((2, SUBLANES, s), F32)],
        _params(2), [proj, b_blk_re, b_blk_im, c_blk_re, c_blk_im, lam_re, lam_im, d_skip])


def _s5_backward(dypre, proj, x_re, x_im, b_blk_re, b_blk_im, c_blk_re, c_blk_im, lam_re, lam_im, d_skip, n_blk, t_chunk,
                 riders=()):
    seq = proj.shape[0]
    n_chunks = seq // t_chunk
    n_tiles = t_chunk // SUBLANES
    s = b_blk_re.shape[2]

    def body(dy_ref, u_ref, xr_ref, xi_ref, pr_ref, pi_ref, bre_ref, bim_ref, cre_ref, cim_ref, lr_ref, li_ref, d_ref,
             du_ref, dbre_ref, dbim_ref, dcre_ref, dcim_ref, dlr_ref, dli_ref, dd_ref, gr, gi, carry):
        t = pl.program_id(1)

        @pl.when(t == 0)
        def _():
            carry[...] = jnp.zeros_like(carry)
            for r in (dbre_ref, dbim_ref, dcre_ref, dcim_ref, dlr_ref, dli_ref, dd_ref):
                r[...] = jnp.zeros_like(r)

        dy = dy_ref[...]
        dyb = dy.astype(BF16)
        u = u_ref[...]
        gr[...] = _dot(dyb, cre_ref[...], NT)
        gi[...] = -_dot(dyb, cim_ref[...], NT)
        steps, cpow = _scan_coefs(lr_ref[...], -li_ref[...], True)
        row = lax.broadcasted_iota(jnp.int32, (SUBLANES, s), 0)
        last = SUBLANES - 1
        first_chunk = t == n_chunks - 1

        def tile_at(r0, prev_r, prev_i, c):
            cr, ci, ar, ai = c
            lr_, li_ = _scan_tile(gr[pl.ds(r0, SUBLANES), :], gi[pl.ds(r0, SUBLANES), :], steps, cpow, cr, ci, True)
            gr[pl.ds(r0, SUBLANES), :] = lr_
            gi[pl.ds(r0, SUBLANES), :] = li_
            xr, xi = xr_ref[pl.ds(r0, SUBLANES), :], xi_ref[pl.ds(r0, SUBLANES), :]
            xpr = jnp.where(row == 0, jnp.broadcast_to(prev_r[last:, :], xr.shape), pltpu.roll(xr, 1, 0))
            xpi = jnp.where(row == 0, jnp.broadcast_to(prev_i[last:, :], xi.shape), pltpu.roll(xi, 1, 0))
            ar = ar + lr_ * xpr + li_ * xpi
            ai = ai + li_ * xpr - lr_ * xpi
            return (jnp.broadcast_to(lr_[:1, :], lr_.shape), jnp.broadcast_to(li_[:1, :], li_.shape), ar, ai)

        def tile(ii, c):
            i = n_tiles - 1 - ii
            r0 = pl.multiple_of(i * SUBLANES, SUBLANES)
            rp = pl.multiple_of(r0 - SUBLANES, SUBLANES)
            return tile_at(r0, xr_ref[pl.ds(rp, SUBLANES), :], xi_ref[pl.ds(rp, SUBLANES), :], c)

        zero = jnp.zeros((SUBLANES, s), F32)
        c = lax.fori_loop(0, n_tiles - 1, tile, (carry[0], carry[1], zero, zero))
        keep = jnp.where(first_chunk, 0.0, 1.0)
        c = tile_at(0, pr_ref[...] * keep, pi_ref[...] * keep, c)
        carry[0] = c[0]
        carry[1] = c[1]
        dlr_ref[...] += jnp.sum(c[2], axis=0, keepdims=True)
        dli_ref[...] += jnp.sum(c[3], axis=0, keepdims=True)

        lam_r, lam_i = gr[...].astype(BF16), gi[...].astype(BF16)
        du_ref[...] = (_dot(lam_r, bre_ref[...], NT) + _dot(lam_i, bim_ref[...], NT) + d_ref[...] * dy).astype(du_ref.dtype)
        ub = u.astype(BF16)
        dbre_ref[...] += _dot(ub, lam_r, TN)
        dbim_ref[...] += _dot(ub, lam_i, TN)
        dcre_ref[...] += _dot(xr_ref[...].astype(BF16), dyb, TN)
        dcim_ref[...] -= _dot(xi_ref[...].astype(BF16), dyb, TN)
        dd_ref[...] += jnp.sum(dy * u, axis=0, keepdims=True)

    rev = lambda t: n_chunks - 1 - t
    blk = lambda shape: pl.BlockSpec((None,) + shape, lambda j, t: (j, 0, 0))
    tpc = t_chunk // SUBLANES
    prev_spec = pl.BlockSpec((SUBLANES, s), lambda j, t: (jnp.maximum(rev(t) * tpc - 1, 0), j))
    chunk = lambda w: pl.BlockSpec((t_chunk, w), lambda j, t: (rev(t), j))
    return _ride(
        "s5_bwd", (n_blk, n_chunks), riders, body,
        [chunk(LANES), chunk(LANES), chunk(s), chunk(s), prev_spec, prev_spec, blk((LANES, s)), blk((LANES, s)),
         blk((s, LANES)), blk((s, LANES)), blk((1, s)), blk((1, s)), pl.BlockSpec((1, LANES), lambda j, t: (0, j))],
        [chunk(LANES), blk((LANES, s)), blk((LANES, s)), blk((s, LANES)), blk((s, LANES)), blk((1, s)), blk((1, s)),
         pl.BlockSpec((1, LANES), lambda j, t: (0, j))],
        [jax.ShapeDtypeStruct((seq, n_blk * LANES), BF16),
         jax.ShapeDtypeStruct((n_blk, LANES, s), F32), jax.ShapeDtypeStruct((n_blk, LANES, s), F32),
         jax.ShapeDtypeStruct((n_blk, s, LANES), F32), jax.ShapeDtypeStruct((n_blk, s, LANES), F32),
         jax.ShapeDtypeStruct((n_blk, 1, s), F32), jax.ShapeDtypeStruct((n_blk, 1, s), F32),
         jax.ShapeDtypeStruct((1, n_blk * LANES), F32)],
        [pltpu.VMEM((t_chunk, s), F32), pltpu.VMEM((t_chunk, s), F32), pltpu.VMEM((2, SUBLANES, s), F32)],
        _params(2), [dypre, proj, x_re, x_im, x_re, x_im, b_blk_re, b_blk_im, c_blk_re, c_blk_im, lam_re, lam_im, d_skip])


TQ, TK = 256, 128


def _split_bf16(x):
    hi = x.astype(BF16)
    return hi, (x - hi.astype(F32)).astype(BF16)


def _sb_weights(z, past, carry, tri):
    ls = jnp.minimum(z, 0.0) - jnp.log(1.0 + jnp.exp(-jnp.abs(z)))
    lk = ls - z
    if past is not None:
        lk = jnp.where(past, lk, 0.0)
    w = jnp.exp(ls + _dot(lk.astype(BF16), tri) + carry)
    if past is not None:
        w = jnp.where(past, w, 0.0)
    return ls, lk, w


LOG_KEEP_DEAD = -104.0


def _walk_key_blocks(i, ratio, prologue, block, epilogue, log_keep):
    n_kb = (i + 1) * ratio
    prologue(n_kb - 1)
    for n in range(ratio):
        block(n_kb - 1 - n, n % 2, True)
    n_pairs = (i * ratio) // 2

    def more(state):
        t, alive = state
        return jnp.logical_and(t < n_pairs, alive)

    def pair(state):
        t, _ = state
        j = n_kb - 1 - ratio - 2 * t
        block(j, ratio % 2, False)
        block(j - 1, (ratio + 1) % 2, False)
        return t + 1, log_keep() >= LOG_KEEP_DEAD

    done, _ = lax.while_loop(more, pair, (jnp.int32(0), log_keep() >= LOG_KEEP_DEAD))
    epilogue(n_kb - ratio - 2 * done)


def _attention_forward(qh, kh, proj, v_col, n_pair, tq, tk, riders=()):
    seq = qh.shape[0]
    ratio = tq // tk
    assert ratio % 2 == 0

    def body(q_ref, k_ref, v_ref, o_ref, q_scr, z_scr, w_scr, acc_scr, c_scr):
        i = pl.program_id(1)
        lane = lax.broadcasted_iota(jnp.int32, (1, LANES), 1)
        q2 = q_ref[...]
        q_scr[0] = jnp.where(lane < HEAD_DIM, q2, 0.0).astype(BF16)
        q_scr[1] = jnp.where(lane < HEAD_DIM, 0.0, q2).astype(BF16)
        tri = (lax.broadcasted_iota(jnp.int32, (tk, tk), 0) > lax.broadcasted_iota(jnp.int32, (tk, tk), 1)).astype(BF16)
        qpos = i * tq + lax.broadcasted_iota(jnp.int32, (tq, tk), 0)
        kidx = lax.broadcasted_iota(jnp.int32, (tq, tk), 1)

        def rows(ref, j):
            j = jnp.clip(j, 0, seq // tk - 1)
            return ref[pl.ds(pl.multiple_of(j * tk, tk), tk), :].astype(BF16)

        def scores(j, slot):
            kb = rows(k_ref, j)
            for h in range(2):
                z_scr[slot, h] = _dot(q_scr[h], kb, NT)

        def finish(j):
            vb = rows(v_ref, j)
            for h in range(2):
                acc_scr[h] += _dot(w_scr[h], vb)

        def prologue(j):
            w_scr[...] = jnp.zeros_like(w_scr)
            acc_scr[...] = jnp.zeros_like(acc_scr)
            c_scr[...] = jnp.zeros_like(c_scr)
            scores(j, 0)

        def block(j, slot, masked):
            scores(j - 1, 1 - slot)
            finish(j + 1)
            past = ((kidx + j * tk) < qpos) if masked else None
            for h in range(2):
                _, lk, w = _sb_weights(z_scr[slot, h], past, c_scr[h], tri)
                w_scr[h] = w.astype(BF16)
                c_scr[h] += jnp.sum(lk, axis=-1, keepdims=True)

        _walk_key_blocks(i, ratio, prologue, block, finish, lambda: jnp.max(c_scr[...]))
        o_ref[...] = jnp.where(lane < HEAD_DIM, acc_scr[0], acc_scr[1])

    (out,), lands = _ride(
        "attn_fwd", (n_pair, seq // tq), riders, body,
        [pl.BlockSpec((tq, LANES), lambda p, i: (i, p)), pl.BlockSpec((seq, LANES), lambda p, i: (0, p)),
         pl.BlockSpec((seq, LANES), lambda p, i: (0, v_col + p))],
        [pl.BlockSpec((tq, LANES), lambda p, i: (i, p))], [jax.ShapeDtypeStruct(qh.shape, F32)],
        [pltpu.VMEM((2, tq, LANES), BF16), pltpu.VMEM((2, 2, tq, tk), F32), pltpu.VMEM((2, tq, tk), BF16),
         pltpu.VMEM((2, tq, LANES), F32), pltpu.VMEM((2, tq, 1), F32)],
        _params(2), [qh, kh, proj])
    return out, lands


def _attention_backward(qh, kh, proj, v_col, y, dy, n_pair, tq, tk, riders=()):
    seq = qh.shape[0]

    ratio = tq // tk
    assert ratio % 2 == 0

    n_kblk = seq // tk

    def body(q_ref, k_ref, v_ref, y_ref, dy_ref, dq_ref, dk_ref, dv_ref,
             q_scr, do_scr, qt_scr, dot_scr, dkt_scr, dvt_scr, z_scr, dw_scr, w_scr, dz_scr, dq_scr, c_scr, c2_scr, tot_scr):
        i = pl.program_id(1)

        @pl.when(i == 0)
        def _():
            dkt_scr[...] = jnp.zeros_like(dkt_scr)
            dvt_scr[...] = jnp.zeros_like(dvt_scr)

        lane = lax.broadcasted_iota(jnp.int32, (1, LANES), 1)
        sel = (lane < HEAD_DIM, lane >= HEAD_DIM)
        q2, do2 = q_ref[...], dy_ref[...].astype(BF16)
        do2f = do2.astype(F32)
        dot_oy = do2f * y_ref[...]
        for h in range(2):
            qm, dm = jnp.where(sel[h], q2, 0.0), jnp.where(sel[h], do2f, 0.0)
            q_scr[h] = qm.astype(BF16)
            do_scr[h] = dm.astype(BF16)
            qt_scr[h] = qm.T.astype(BF16)
            dot_scr[h] = dm.T.astype(BF16)
            tot_scr[h] = jnp.sum(jnp.where(sel[h], dot_oy, 0.0), axis=-1, keepdims=True)
        r_i, c_i = lax.broadcasted_iota(jnp.int32, (tk, tk), 0), lax.broadcasted_iota(jnp.int32, (tk, tk), 1)
        tri = (r_i > c_i).astype(BF16)
        tri_ge = (r_i >= c_i).astype(BF16)
        qpos = i * tq + lax.broadcasted_iota(jnp.int32, (tq, tk), 0)
        kidx = lax.broadcasted_iota(jnp.int32, (tq, tk), 1)

        def start(j):
            return pl.multiple_of(jnp.clip(j, 0, seq // tk - 1) * tk, tk)

        def scores(j, slot):
            c0 = start(j)
            kb, vb = k_ref[pl.ds(c0, tk), :].astype(BF16), v_ref[pl.ds(c0, tk), :].astype(BF16)
            for h in range(2):
                z_scr[slot, h] = _dot(q_scr[h], kb, NT)
                dw_scr[slot, h] = _dot(do_scr[h], vb, NT)

        def finish(j):
            jc = jnp.clip(j, 0, n_kblk - 1)
            kb = k_ref[pl.ds(pl.multiple_of(jc * tk, tk), tk), :].astype(BF16)
            dkt_add, dvt_add = jnp.zeros((LANES, tk), F32), jnp.zeros((LANES, tk), F32)
            for h in range(2):
                dz = dz_scr[h]
                dq_scr[h] += _dot(dz, kb)
                dkt_add = dkt_add + _dot(qt_scr[h], dz)
                dvt_add = dvt_add + _dot(dot_scr[h], w_scr[h])
            dkt_scr[jc] += dkt_add
            dvt_scr[jc] += dvt_add

        def prologue(j):
            for r in (w_scr, dz_scr, dq_scr, c_scr, c2_scr):
                r[...] = jnp.zeros_like(r)
            scores(j, 0)

        def block(j, slot, masked):
            scores(j - 1, 1 - slot)
            finish(j + 1)
            past = ((kidx + j * tk) < qpos) if masked else None
            for h in range(2):
                ls, lk, w = _sb_weights(z_scr[slot, h], past, c_scr[h], tri)
                wb = w.astype(BF16)
                dlw = dw_scr[slot, h] * wb.astype(F32)
                hi, lo = _split_bf16(dlw)
                dlk = tot_scr[h] - c2_scr[h] - (_dot(hi, tri_ge) + _dot(lo, tri_ge))
                if masked:
                    dlk = jnp.where(past, dlk, 0.0)
                sig = jnp.exp(ls)
                w_scr[h] = wb
                dz_scr[h] = (dlw * (1.0 - sig) - dlk * sig).astype(BF16)
                c_scr[h] += jnp.sum(lk, axis=-1, keepdims=True)
                c2_scr[h] += jnp.sum(dlw, axis=-1, keepdims=True)

        _walk_key_blocks(i, ratio, prologue, block, finish, lambda: jnp.max(c_scr[...]))
        dq_ref[...] = jnp.where(sel[0], dq_scr[0], dq_scr[1])

        @pl.when(i == seq // tq - 1)
        def _():
            for jb in range(n_kblk):
                dk_ref[jb * tk:(jb + 1) * tk, :] = dkt_scr[jb].T
                dv_ref[jb * tk:(jb + 1) * tk, :] = dvt_scr[jb].T.astype(dv_ref.dtype)

    blk = pl.BlockSpec((tq, LANES), lambda p, i: (i, p))
    full = pl.BlockSpec((seq, LANES), lambda p, i: (0, p))
    shape = jax.ShapeDtypeStruct(qh.shape, F32)
    return _ride(
        "attn_bwd", (n_pair, seq // tq), riders, body,
        [blk, full, pl.BlockSpec((seq, LANES), lambda p, i: (0, v_col + p)), blk, blk],
        [blk, full, full], [shape, shape, jax.ShapeDtypeStruct(qh.shape, BF16)],
        [pltpu.VMEM((2, tq, LANES), BF16), pltpu.VMEM((2, tq, LANES), BF16),
         pltpu.VMEM((2, LANES, tq), BF16), pltpu.VMEM((2, LANES, tq), BF16),
         pltpu.VMEM((n_kblk, LANES, tk), F32), pltpu.VMEM((n_kblk, LANES, tk), F32),
         pltpu.VMEM((2, 2, tq, tk), F32), pltpu.VMEM((2, 2, tq, tk), F32),
         pltpu.VMEM((2, tq, tk), BF16), pltpu.VMEM((2, tq, tk), BF16), pltpu.VMEM((2, tq, LANES), F32),
         pltpu.VMEM((2, tq, 1), F32), pltpu.VMEM((2, tq, 1), F32), pltpu.VMEM((2, tq, 1), F32)],
        _params(2), [qh, kh, proj, y, dy])


def _loss_head(h1, ffn, target, gate_f, tile):
    seq, d = h1.shape

    def body(h_ref, f_ref, t_ref, g_ref, dy_ref, df_ref, dg_ref, loss_ref):
        @pl.when(pl.program_id(0) == 0)
        def _():
            dg_ref[...] = jnp.zeros_like(dg_ref)
            loss_ref[...] = jnp.zeros_like(loss_ref)

        f, g = f_ref[...], g_ref[...]
        err = h_ref[...] + g * f - t_ref[...]
        dy = err * (1.0 / d)
        dy_ref[...] = dy
        df_ref[...] = (dy * g).astype(df_ref.dtype)
        dg_ref[...] += jnp.sum(dy * f, axis=0, keepdims=True)
        loss_ref[...] += jnp.sum(jnp.sum(err * err, axis=-1, keepdims=True), axis=0, keepdims=True) * (0.5 / d)

    row = _row_spec(tile, d)
    return pl.pallas_call(
        body, name="loss_head", grid=(seq // tile,), in_specs=[row, row, row, _vec_spec(d)],
        out_specs=[row, row, _vec_spec(d), pl.BlockSpec((1, 1), lambda i: (0, 0))],
        out_shape=[jax.ShapeDtypeStruct((seq, d), F32), jax.ShapeDtypeStruct((seq, d), BF16),
                   jax.ShapeDtypeStruct((1, d), F32), jax.ShapeDtypeStruct((1, 1), F32)],
        compiler_params=_params(1),
    )(h1, ffn, target, gate_f)


def _dot3(a, b, dn):
    ah, al = _split_bf16(a)
    bh, bl = _split_bf16(b)
    return _dot(ah, bh, dn) + (_dot(ah, bl, dn) + _dot(al, bh, dn))


def _ada_forward(c_all, w_shard, b_cols):
    d, n = w_shard.shape
    bk = _tile(d, 512)

    def body(c_ref, w_ref, b_ref, o_ref):
        @pl.when(pl.program_id(0) == 0)
        def _():
            o_ref[...] = jnp.broadcast_to(b_ref[...], o_ref.shape)

        o_ref[...] += _dot3(jax.nn.silu(c_ref[...]), w_ref[...], NN)

    return pl.pallas_call(
        body, name="ada_fwd", grid=(d // bk,),
        in_specs=[pl.BlockSpec((NDEV, bk), lambda k: (0, k)), pl.BlockSpec((bk, n), lambda k: (k, 0)), _vec_spec(n)],
        out_specs=pl.BlockSpec((NDEV, n), lambda k: (0, 0)), out_shape=jax.ShapeDtypeStruct((NDEV, n), F32),
        compiler_params=_params(1),
    )(c_all, w_shard, b_cols)


def _adam(w, g, m, v):
    m = ADAM_B1 * m + (1.0 - ADAM_B1) * g
    v = ADAM_B2 * v + (1.0 - ADAM_B2) * (g * g)
    m_hat = m / (1.0 - ADAM_B1 ** ADAM_STEP)
    v_hat = v / (1.0 - ADAM_B2 ** ADAM_STEP)
    return -ADAM_LR * (m_hat / (jnp.sqrt(v_hat) + ADAM_EPS) + ADAM_WD * w), m, v


def _adam_ada(c_all, dmod_cols, w, m, v):
    d, n = w.shape
    tr = _tile(d, 256)

    def body(c_ref, dm_ref, w_ref, m_ref, v_ref, g_ref, dl_ref, nm_ref, nv_ref):
        g = _dot3(jax.nn.silu(c_ref[...]), dm_ref[...], TN)
        delta, nm, nv = _adam(w_ref[...], g, m_ref[...], v_ref[...])
        g_ref[...] = g
        dl_ref[...] = delta
        nm_ref[...] = nm
        nv_ref[...] = nv

    row = _row_spec(tr, n)
    return pl.pallas_call(
        body, name="adam_ada", grid=(d // tr,),
        in_specs=[pl.BlockSpec((NDEV, tr), lambda i: (0, i)), pl.BlockSpec((NDEV, n), lambda i: (0, 0)), row, row, row],
        out_specs=[row] * 4, out_shape=[jax.ShapeDtypeStruct((d, n), F32)] * 4, compiler_params=_params(1),
    )(c_all, dmod_cols, w, m, v)


def _adam_sum(name, parts, part_spec, w, m, v, tr):
    r, c = w.shape

    def body(p_ref, w_ref, m_ref, v_ref, g_ref, dl_ref, nm_ref, nv_ref):
        g = p_ref[0].astype(F32)
        for k in range(1, NDEV):
            g = g + p_ref[k].astype(F32)
        delta, nm, nv = _adam(w_ref[...], g, m_ref[...], v_ref[...])
        g_ref[...] = g
        dl_ref[...] = delta
        nm_ref[...] = nm
        nv_ref[...] = nv

    row = _row_spec(tr, c)
    return pl.pallas_call(
        body, name=name, grid=(r // tr,), in_specs=[part_spec, row, row, row],
        out_specs=[row] * 4, out_shape=[jax.ShapeDtypeStruct((r, c), F32)] * 4, compiler_params=_params(1),
    )(parts, w, m, v)


GROUPS_PER_BLOCK = LANES // SSM_GROUP


def _to_b_blocks(bb, n_blk, p):
    t = bb.reshape(n_blk, GROUPS_PER_BLOCK, p, SSM_GROUP)
    eye = jnp.eye(GROUPS_PER_BLOCK, dtype=bb.dtype)
    return jnp.einsum("jgph,gk->jghkp", t, eye).reshape(n_blk, LANES, GROUPS_PER_BLOCK * p)


def _from_b_blocks(blk, n_blk, p):
    t = blk.reshape(n_blk, GROUPS_PER_BLOCK, SSM_GROUP, GROUPS_PER_BLOCK, p)
    eye = jnp.eye(GROUPS_PER_BLOCK, dtype=blk.dtype)
    return jnp.einsum("jghkp,gk->jgph", t, eye).reshape(n_blk * GROUPS_PER_BLOCK, p, SSM_GROUP)


def _to_c_blocks(cc, n_blk, p):
    t = cc.reshape(n_blk, GROUPS_PER_BLOCK, SSM_GROUP, p)
    eye = jnp.eye(GROUPS_PER_BLOCK, dtype=cc.dtype)
    return jnp.einsum("jghp,gk->jgpkh", t, eye).reshape(n_blk, GROUPS_PER_BLOCK * p, LANES)


def _from_c_blocks(blk, n_blk, p):
    t = blk.reshape(n_blk, GROUPS_PER_BLOCK, p, GROUPS_PER_BLOCK, SSM_GROUP)
    eye = jnp.eye(GROUPS_PER_BLOCK, dtype=blk.dtype)
    return jnp.einsum("jgpkh,gk->jghp", t, eye).reshape(n_blk * GROUPS_PER_BLOCK, SSM_GROUP, p)


SMALL_LATE = ("b_ada_a", "g_mix")
SMALL_EARLY = ("b_ada_b", "a_re", "a_im", "log_dt", "b_re", "b_im", "c_re", "c_im", "d_skip",
               "q_gain", "k_gain", "g_ssm_out", "g_attn_out", "g_ffn")
PACK_COLS = 1024


def _pack(arrs):
    flat = jnp.concatenate([a.reshape(-1) for a in arrs])
    n = flat.shape[0]
    quantum = SUBLANES * PACK_COLS
    padded = -(-n // quantum) * quantum
    return jnp.pad(flat, (0, padded - n)).reshape(padded // PACK_COLS, PACK_COLS)


def _unpack(packed, like):
    flat, out, off = packed.reshape(-1), [], 0
    for a in like:
        out.append(flat[off:off + a.size].reshape(a.shape))
        off += a.size
    return out


def kernel(x, c, w_ada, b_ada, g_mix, w_in, a_re, a_im, log_dt, b_re, b_im, c_re, c_im, d_skip, w_glu, q_gain, k_gain, g_ssm_out, g_attn_out, w_out, g_ffn, w_gate, w_up, w_down, loss_target, m_w_ada, m_b_ada, m_g_mix, m_w_in, m_a_re, m_a_im, m_log_dt, m_b_re, m_b_im, m_c_re, m_c_im, m_d_skip, m_w_glu, m_q_gain, m_k_gain, m_g_ssm_out, m_g_attn_out, m_w_out, m_g_ffn, m_w_gate, m_w_up, m_w_down, v_w_ada, v_b_ada, v_g_mix, v_w_in, v_a_re, v_a_im, v_log_dt, v_b_re, v_b_im, v_c_re, v_c_im, v_d_skip, v_w_glu, v_q_gain, v_k_gain, v_g_ssm_out, v_g_attn_out, v_w_out, v_g_ffn, v_w_gate, v_w_up, v_w_down):
    given = dict(locals())
    seq, d = x.shape[1], x.shape[2]
    xs, tgt = x[0], loss_target[0]
    n_groups, p_state = a_re.shape[1], a_re.shape[2]
    w_ssm = n_groups * SSM_GROUP
    w_attn = w_in.shape[2] * NDEV - w_ssm
    w_attn //= 3
    n_blk, n_pair = w_ssm // LANES, w_attn // LANES
    n_heads = w_attn // HEAD_DIM
    ns_in, ns_ff = w_in.shape[2], w_gate.shape[2]
    d_mix = w_ssm + w_attn
    mx, my, mc = _me()
    me = 4 * mx + 2 * my + mc
    rt = _tile(seq, 256)
    n_rt = seq // rt
    sds = jax.ShapeDtypeStruct

    c_all = _exchange(c, True, "comm_ag_c").reshape(NDEV, d)
    n_ada = w_ada.shape[2]
    b_cols = lax.dynamic_slice(b_ada, (0, me * n_ada), (1, n_ada))
    mod_cols = _ada_forward(c_all, w_ada[0], b_cols)
    mod_all = _exchange(mod_cols, True, "comm_ag_mod")
    mod = lax.dynamic_slice(mod_all, (0, me, 0), (NDEV, 1, n_ada)).reshape(1, NDEV * n_ada)
    shift_m, scale_m, gate_m, shift_f, scale_f, gate_f = [mod[:, i * d:(i + 1) * d] for i in range(6)]
    w_in_g = _exchange(w_in[0].astype(BF16), True, "comm_ag_w_in", relayed=True)

    gp = n_groups * p_state
    a_re2, a_im2, ldt2 = a_re[0], a_im[0], log_dt[0].reshape(n_groups, 1)
    b_re2, b_im2 = b_re[0].reshape(gp, SSM_GROUP), b_im[0].reshape(gp, SSM_GROUP)
    lam_r, lam_i, coef_r, coef_i = _whole("s5_lam", _s5_lam, [a_re2, a_im2, ldt2], [(n_groups, p_state)] * 4)
    coef_r2, coef_i2 = coef_r.reshape(gp, 1), coef_i.reshape(gp, 1)
    bb_r, bb_i = _whole("s5_bbar", _s5_bbar, [coef_r2, coef_i2, b_re2, b_im2], [(gp, SSM_GROUP)] * 2)
    s_blk = GROUPS_PER_BLOCK * p_state
    b_blk_r = _to_b_blocks(bb_r.reshape(n_groups, p_state, SSM_GROUP), n_blk, p_state).astype(BF16)
    b_blk_i = _to_b_blocks(bb_i.reshape(n_groups, p_state, SSM_GROUP), n_blk, p_state).astype(BF16)
    c_blk_r = _to_c_blocks(c_re[0], n_blk, p_state).astype(BF16)
    c_blk_i = _to_c_blocks(c_im[0], n_blk, p_state).astype(BF16)
    lam_r3, lam_i3 = lam_r.reshape(n_blk, 1, s_blk), lam_i.reshape(n_blk, 1, s_blk)
    d_skip2 = d_skip[0].reshape(1, w_ssm)

    row_d, vec_d = _row_spec(rt, d), _vec_spec(d)
    xm, = _rowwise_fwd("seg_in", lambda *a: _seg_in(*a)[:1], [xs], [row_d], [shift_m, scale_m, g_mix], [vec_d] * 3,
                       [sds((seq, d), BF16)], [row_d], n_rt)
    bn_in = _tile(ns_in, 512)
    per = ns_in // bn_in
    bm, bk = _tile(seq, BM), _tile(d, BK)
    proj, w_glu_g, w_out_g = _mm(
        "mm_in", xm, w_in_g, NN, (seq // bm, NDEV * per, d // bk),
        pl.BlockSpec((bm, bk), lambda i, j, k: (i, k)),
        pl.BlockSpec((None, bk, bn_in), lambda i, j, k: (j // per, k, j % per)),
        pl.BlockSpec((bm, bn_in), lambda i, j, k: (i, j)), (seq, NDEV * ns_in), F32, (bm, bn_in),
        riders=[_Rider(w_glu[0].astype(BF16), True, relayed=True), _Rider(w_out[0].astype(BF16), True, relayed=True)])
    w_glu_g, w_out_g = w_glu_g.reshape(w_ssm, w_ssm), w_out_g.reshape(d_mix, d)
    q_col, k_col, v_col = w_ssm // w_attn, w_ssm // w_attn + 1, (w_ssm + 2 * w_attn) // LANES
    qg_t, kg_t = jnp.tile(q_gain, (1, n_heads)), jnp.tile(k_gain, (1, n_heads))
    row_a, vec_a = _row_spec(rt, w_attn), _vec_spec(w_attn)
    qk_rows, qk_specs = [proj, proj], [_row_spec(rt, w_attn, q_col), _row_spec(rt, w_attn, k_col)]
    qh, kh = _rowwise_fwd("seg_qk", _seg_qk, qk_rows, qk_specs, [qg_t, kg_t], [vec_a] * 2,
                          [sds((seq, w_attn), F32)] * 2, [row_a] * 2, n_rt)
    t_chunk = _tile(seq, 512)
    (ypre, x_re, x_im), (w_gu_land,) = _s5_forward(
        proj, b_blk_r, b_blk_i, c_blk_r, c_blk_i, lam_r3, lam_i3, d_skip2, n_blk, t_chunk,
        riders=[_Rider(w_up[0].astype(BF16), True, slot=1, n_slots=2, relayed=True)])
    tq, tk = _tile(seq, TQ), _tile(seq, TK)
    y_attn, (w_gu_land,) = _attention_forward(qh, kh, proj, v_col, n_pair, tq, tk,
                                              riders=[_Rider(w_gate[0].astype(BF16), True, land=w_gu_land, slot=0,
                                                             relayed=True)])
    w_gu_g = w_gu_land.reshape(2 * NDEV, d, ns_ff)
    row_s, vec_s = _row_spec(rt, w_ssm), _vec_spec(w_ssm)
    y1, = _rowwise_fwd("seg_gelu", _seg_gelu, [ypre], [row_s], [], [], [sds((seq, w_ssm), F32)], [row_s], n_rt)
    z = _mm_plain("mm_glu", y1, w_glu_g, NN, F32)
    row_m = _row_spec(rt, d_mix)
    mixed, = _rowwise_fwd("seg_mix", _seg_mix, [y1, z, y_attn], [row_s, row_s, row_a], [g_ssm_out, g_attn_out], [vec_s, vec_a],
                          [sds((seq, d_mix), BF16)], [row_m], n_rt)
    o = _mm_plain("mm_out", mixed, w_out_g, NN, F32)
    h1, xf = _rowwise_fwd("seg_mid", _seg_mid, [xs, o], [row_d] * 2, [gate_m, g_ffn, scale_f, shift_f], [vec_d] * 4,
                          [sds((seq, d), F32), sds((seq, d), BF16)], [row_d] * 2, n_rt)
    gu, w_down_g = _mm(
        "mm_gu", xf, w_gu_g, NN, (seq // bm, 2 * NDEV, d // bk),
        pl.BlockSpec((bm, bk), lambda i, j, k: (i, k)), pl.BlockSpec((None, bk, ns_ff), lambda i, j, k: (j, k, 0)),
        pl.BlockSpec((None, bm, ns_ff), lambda i, j, k: (j, i, 0)), (2 * NDEV, seq, ns_ff), BF16, (bm, ns_ff),
        riders=[_Rider(w_down[0].astype(BF16), True, relayed=True)])
    gu4 = gu.reshape(2, NDEV, seq, ns_ff)
    ft = _tile(seq, 512)
    pair_spec = pl.BlockSpec((2, None, ft, ns_ff), lambda s, i: (0, s, i, 0))
    one_spec = pl.BlockSpec((None, ft, ns_ff), lambda s, i: (s, i, 0))

    def act_body(gu_ref, a_ref):
        a_ref[...] = _seg_act(gu_ref[0].astype(F32), gu_ref[1].astype(F32))[0].astype(a_ref.dtype)

    act = pl.pallas_call(act_body, name="seg_act", grid=(NDEV, seq // ft), in_specs=[pair_spec], out_specs=one_spec,
                         out_shape=sds((NDEV, seq, ns_ff), BF16), compiler_params=_params(2))(gu4)
    bn_d = _tile(d, BN)
    bm_h = _tile(seq, BM // 2)
    shard_pieces = lambda n: (lambda a_ref, b_ref: [(a_ref[g], b_ref[g]) for g in range(n)])
    ffn = _mm("mm_down", act, w_down_g, NN, (seq // bm_h, d // bn_d, 1),
              pl.BlockSpec((NDEV, bm_h, ns_ff), lambda i, j, k: (0, i, 0)), pl.BlockSpec((NDEV, ns_ff, bn_d), lambda i, j, k: (0, 0, j)),
              pl.BlockSpec((bm_h, bn_d), lambda i, j, k: (i, j)), (seq, d), F32, (bm_h, bn_d), pieces=shard_pieces(NDEV))
    dy, dffn, d_gate_f, loss_part = _loss_head(h1, ffn, tgt, gate_f, rt)
    loss = lax.psum(loss_part[0, 0], MESH_AXES)

    bl = _tile(seq, BK)
    gw_down = _mm("mm_dw_down", act, dffn, TN, (NDEV, d // bn_d, seq // bl),
                  pl.BlockSpec((None, bl, ns_ff), lambda i, j, k: (i, k, 0)), pl.BlockSpec((bl, bn_d), lambda i, j, k: (k, j)),
                  pl.BlockSpec((None, ns_ff, bn_d), lambda i, j, k: (i, 0, j)), (NDEV, ns_ff, d), BF16, (ns_ff, bn_d))
    dact = _mm(
        "mm_dact", dffn, w_down_g, NT, (seq // bm, NDEV, d // bk),
        pl.BlockSpec((bm, bk), lambda i, j, k: (i, k)), pl.BlockSpec((None, ns_ff, bk), lambda i, j, k: (j, 0, k)),
        pl.BlockSpec((None, bm, ns_ff), lambda i, j, k: (j, i, 0)), (NDEV, seq, ns_ff), BF16, (bm, ns_ff))

    def dact_body(gu_ref, da_ref, dgu_ref):
        _, vjp = jax.vjp(_seg_act, gu_ref[0].astype(F32), gu_ref[1].astype(F32))
        dg, du_ = vjp((da_ref[...].astype(F32),))
        dgu_ref[0] = dg.astype(dgu_ref.dtype)
        dgu_ref[1] = du_.astype(dgu_ref.dtype)

    dgu4 = pl.pallas_call(dact_body, name="seg_act_bwd", grid=(NDEV, seq // ft), in_specs=[pair_spec, one_spec],
                          out_specs=pair_spec, out_shape=sds((2, NDEV, seq, ns_ff), BF16), compiler_params=_params(2))(gu4, dact)
    dgu = dgu4.reshape(2 * NDEV, seq, ns_ff)
    bmd = _tile(d, BM)

    def dw_half(name, which, riders):
        return _mm(name, xf, dgu, TN, (d // bmd, NDEV, seq // bl), pl.BlockSpec((bl, bmd), lambda i, j, k: (k, i)),
                   pl.BlockSpec((None, bl, ns_ff), lambda i, j, k: (which * NDEV + j, k, 0)),
                   pl.BlockSpec((None, bmd, ns_ff), lambda i, j, k: (j, i, 0)), (NDEV, d, ns_ff), BF16, (bmd, ns_ff), riders=riders)

    gw_gate = dw_half("mm_dw_gate", 0, ())
    gw_up = dw_half("mm_dw_up", 1, ())
    dxf, got_down = _mm(
        "mm_dxf", dgu, w_gu_g, NT, (seq // bm, d // bn_d, 4),
        pl.BlockSpec((4, bm, ns_ff), lambda i, j, k: (k, i, 0)), pl.BlockSpec((4, bn_d, ns_ff), lambda i, j, k: (k, j, 0)),
        pl.BlockSpec((bm, bn_d), lambda i, j, k: (i, j)), (seq, d), F32, (bm, bn_d), riders=[_Rider(gw_down, False)],
        pieces=shard_pieces(4))
    (do, dx_a, d_gate_m, d_g_ffn, d_scale_f, d_shift_f) = _rowwise_bwd(
        "seg_mid_bwd", _seg_mid, [xs, o], [row_d] * 2, [gate_m, g_ffn, scale_f, shift_f], [vec_d] * 4,
        [dy, dxf], [row_d] * 2, [[0], [1]], [1, 0], [sds((seq, d), BF16), sds((seq, d), F32)], [row_d] * 2,
        [0, 1, 2, 3], [sds((1, d), F32)] * 4, [vec_d] * 4, n_rt)

    dmixed = _mm_plain("mm_dmixed", do, w_out_g, NT, F32)
    gw_out = _mm_plain("mm_dw_out", mixed, do, TN, BF16)
    (dz, dy1_a, dy_attn, d_g_ssm, d_g_attn) = _rowwise_bwd(
        "seg_mix_bwd", _seg_mix, [y1, z, y_attn], [row_s, row_s, row_a], [g_ssm_out, g_attn_out], [vec_s, vec_a],
        [dmixed], [row_m], [[0]], [1, 0, 2], [sds((seq, w_ssm), BF16), sds((seq, w_ssm), F32), sds((seq, w_attn), F32)],
        [row_s, row_s, row_a], [0, 1], [sds((1, w_ssm), F32), sds((1, w_attn), F32)], [vec_s, vec_a], n_rt)
    dy1_b = _mm_plain("mm_dy1", dz, w_glu_g, NT, F32)
    gw_glu = _mm_plain("mm_dw_glu", y1, dz, TN, BF16)
    (dypre,) = _rowwise_bwd("seg_gelu_bwd", _seg_gelu, [ypre], [row_s], [], [], [dy1_a, dy1_b], [row_s] * 2, [[0, 1]],
                            [0], [sds((seq, w_ssm), F32)], [row_s], [], [], [], n_rt)
    (du, db_blk_r, db_blk_i, dc_blk_r, dc_blk_i, dlam_r3, dlam_i3, dd_skip2), (got_gate,) = _s5_backward(
        dypre, proj, x_re, x_im, b_blk_r, b_blk_i, c_blk_r, c_blk_i, lam_r3, lam_i3, d_skip2, n_blk, t_chunk,
        riders=[_Rider(gw_gate, False)])
    (dqh, dkh, dv), (got_up, got_out, got_glu) = _attention_backward(
        qh, kh, proj, v_col, y_attn, dy_attn, n_pair, tq, tk,
        riders=[_Rider(gw_up, False), _Rider(gw_out.reshape(NDEV, w_out.shape[1], d), False),
                _Rider(gw_glu.reshape(NDEV, w_glu.shape[1], w_ssm), False)])
    (dq, dk, dqg_t, dkg_t) = _rowwise_bwd(
        "seg_qk_bwd", _seg_qk, qk_rows, qk_specs, [qg_t, kg_t], [vec_a] * 2, [dqh, dkh], [row_a] * 2, [[0], [1]],
        [0, 1], [sds((seq, w_attn), BF16)] * 2, [row_a] * 2, [0, 1], [sds((1, w_attn), F32)] * 2, [vec_a] * 2, n_rt)

    dbb_r = _from_b_blocks(db_blk_r, n_blk, p_state).reshape(gp, SSM_GROUP)
    dbb_i = _from_b_blocks(db_blk_i, n_blk, p_state).reshape(gp, SSM_GROUP)
    dcoef_r2, dcoef_i2, db_re2, db_im2 = _whole_vjp("s5_bbar_bwd", _s5_bbar, [coef_r2, coef_i2, b_re2, b_im2], [dbb_r, dbb_i],
                                                    [(gp, 1), (gp, 1), (gp, SSM_GROUP), (gp, SSM_GROUP)])
    lam_cts = [dlam_r3.reshape(n_groups, p_state), dlam_i3.reshape(n_groups, p_state),
               dcoef_r2.reshape(n_groups, p_state), dcoef_i2.reshape(n_groups, p_state)]
    da_re2, da_im2, dldt2 = _whole_vjp("s5_lam_bwd", _s5_lam, [a_re2, a_im2, ldt2], lam_cts,
                                       [(n_groups, p_state), (n_groups, p_state), (n_groups, 1)])
    dc_re2, dc_im2 = _from_c_blocks(dc_blk_r, n_blk, p_state), _from_c_blocks(dc_blk_i, n_blk, p_state)

    small_part = {
        "b_ada_b": jnp.concatenate([d_gate_m, d_shift_f, d_scale_f, d_gate_f], axis=-1),
        "a_re": da_re2, "a_im": da_im2, "log_dt": dldt2, "b_re": db_re2, "b_im": db_im2,
        "c_re": dc_re2, "c_im": dc_im2, "d_skip": dd_skip2,
        "q_gain": dqg_t.reshape(n_heads, HEAD_DIM).sum(0), "k_gain": dkg_t.reshape(n_heads, HEAD_DIM).sum(0),
        "g_ssm_out": d_g_ssm, "g_attn_out": d_g_attn, "g_ffn": d_g_ffn,
    }
    dproj = jnp.concatenate([du, dq, dk, dv], axis=-1)
    gw_in, early_parts = _mm(
        "mm_dw_in", xm, dproj, TN, (d // bmd, NDEV * per, seq // bl),
        pl.BlockSpec((bl, bmd), lambda i, j, k: (k, i)), pl.BlockSpec((bl, bn_in), lambda i, j, k: (k, j)),
        pl.BlockSpec((None, bmd, bn_in), lambda i, j, k: (j // per, i, j % per)), (NDEV, d, ns_in), BF16, (bmd, bn_in),
        riders=[_Rider(_pack([small_part[n] for n in SMALL_EARLY]), True, relayed=True)])
    dxm, got_in = _mm(
        "mm_dxm", dproj, w_in_g, NT, (seq // bm, d // bn_d, 1),
        pl.BlockSpec((bm, NDEV * ns_in), lambda i, j, k: (i, 0)), pl.BlockSpec((NDEV, bn_d, ns_in), lambda i, j, k: (0, j, 0)),
        pl.BlockSpec((bm, bn_d), lambda i, j, k: (i, j)), (seq, d), F32, (bm, bn_d), riders=[_Rider(gw_in, False)],
        pieces=lambda a_ref, b_ref: [(a_ref[:, g * ns_in:(g + 1) * ns_in], b_ref[g]) for g in range(NDEV)])
    (grad_x, d_shift_m, d_scale_m, d_g_mix) = _rowwise_bwd(
        "seg_in_bwd", _seg_in, [xs], [row_d], [shift_m, scale_m, g_mix], [vec_d] * 3, [dxm, dx_a], [row_d] * 2, [[0], [1]],
        [0], [sds((seq, d), F32)], [row_d], [0, 1, 2], [sds((1, d), F32)] * 3, [vec_d] * 3, n_rt)
    small_part["b_ada_a"] = jnp.concatenate([d_shift_m, d_scale_m], axis=-1)
    small_part["g_mix"] = d_g_mix
    late_parts = _exchange(_pack([small_part[n] for n in SMALL_LATE]), True, "comm_ag_small_late")
    packed_parts = jnp.concatenate([late_parts, early_parts], axis=1)

    big = {}

    def sharded(nm, got, width, tr):
        big[nm] = _adam_sum("adam_" + nm, got, pl.BlockSpec((NDEV, tr, width), lambda i: (0, i, 0)), given[nm][0],
                            given["m_" + nm][0], given["v_" + nm][0], tr)

    sharded("w_down", got_down, d, _tile(w_down.shape[1], 64))
    sharded("w_gate", got_gate, ns_ff, _tile(d, 256))
    sharded("w_up", got_up, ns_ff, _tile(d, 256))
    sharded("w_out", got_out, d, _tile(w_out.shape[1], 128))
    sharded("w_glu", got_glu, w_ssm, _tile(w_glu.shape[1], 128))
    sharded("w_in", got_in, ns_in, _tile(d, 256))

    split = dict(given)
    for pre in ("", "m_", "v_"):
        split[pre + "b_ada_a"], split[pre + "b_ada_b"] = given[pre + "b_ada"][:, :2 * d], given[pre + "b_ada"][:, 2 * d:]
    packs = [jnp.concatenate([_pack([split[pre + n] for n in SMALL_LATE]), _pack([split[pre + n] for n in SMALL_EARLY])])
             for pre in ("", "m_", "v_")]
    rows_p = packed_parts.shape[1]
    tr_p = _tile(rows_p, 64)
    sm = _adam_sum("adam_small", packed_parts, pl.BlockSpec((NDEV, tr_p, PACK_COLS), lambda i: (0, i, 0)), *packs, tr_p)
    rows_late = late_parts.shape[1]
    small_out = []
    for t in sm:
        out = dict(zip(SMALL_LATE, _unpack(t[:rows_late], [split[n] for n in SMALL_LATE])))
        out.update(zip(SMALL_EARLY, _unpack(t[rows_late:], [split[n] for n in SMALL_EARLY])))
        out["b_ada"] = jnp.concatenate([out["b_ada_a"], out["b_ada_b"]], axis=1)
        small_out.append(out)

    rows_a, rows_b = (2 * d) // PACK_COLS, (4 * d) // PACK_COLS
    assert rows_a * PACK_COLS == 2 * d
    dmod_all = jnp.concatenate([late_parts[:, :rows_a].reshape(NDEV, 2 * d), early_parts[:, :rows_b].reshape(NDEV, 4 * d)], axis=1)
    dmod_cols = lax.dynamic_slice(dmod_all, (0, me * n_ada), (NDEV, n_ada))
    big["w_ada"] = _adam_ada(c_all, dmod_cols, w_ada[0], m_w_ada[0], v_w_ada[0])

    order = ("w_ada", "b_ada", "g_mix", "w_in", "a_re", "a_im", "log_dt", "b_re", "b_im", "c_re", "c_im", "d_skip", "w_glu",
             "q_gain", "k_gain", "g_ssm_out", "g_attn_out", "w_out", "g_ffn", "w_gate", "w_up", "w_down")
    outs = [loss, grad_x[None]]
    for kind in range(4):
        for n in order:
            outs.append(big[n][kind][None] if n in big else small_out[kind][n])
    return tuple(outs)
```

```python
import functools
import math

import jax
import jax.numpy as jnp
from jax import lax
from jax.experimental import pallas as pl
from jax.experimental.pallas import tpu as pltpu

F32 = jnp.float32
BF16 = jnp.bfloat16
NDEV = 8
MESH_AXES = ("x", "y", "c")
MESH_ID = pl.DeviceIdType.MESH
EPS = 1e-6
LANES = 128
SUBLANES = 8
HEAD_DIM = 64
SSM_GROUP = 16
ADAM_LR, ADAM_B1, ADAM_B2, ADAM_EPS, ADAM_WD, ADAM_STEP = 0.001, 0.9, 0.999, 1e-08, 0.01, 10

NN = (((1,), (0,)), ((), ()))
NT = (((1,), (1,)), ((), ()))
TN = (((0,), (0,)), ((), ()))


def _dot(a, b, dn=NN):
    return lax.dot_general(a, b, dn, preferred_element_type=F32)


def _tile(dim, pref):
    t = min(dim, pref)
    while dim % t:
        t //= 2
    return t


def _params(n):
    return pltpu.CompilerParams(dimension_semantics=("arbitrary",) * n)


def _me():
    mx, my, mc = lax.axis_index("x"), lax.axis_index("y"), lax.axis_index("c")
    return mx, my, mc


def _peer(mx, my, mc, k):
    px = 1 - mx if (k >> 2) & 1 else mx
    py = 1 - my if (k >> 1) & 1 else my
    pc = 1 - mc if k & 1 else mc
    return (px, py, pc), 4 * px + 2 * py + pc


def _slot(ref, idx, part):
    return ref.at[idx] if part is None else ref.at[idx, pl.ds(*part)]


def _exchange_copies(x_ref, land_ref, send_sems, recv_sems, gather, part=None):
    mx, my, mc = _me()
    me = 4 * mx + 2 * my + mc
    pairs = []
    for k in range(1, NDEV):
        peer, pidx = _peer(mx, my, mc, k)
        src = x_ref if gather else _slot(x_ref, pidx, part)
        mk = lambda dst, src=src, k=k, peer=peer: pltpu.make_async_remote_copy(
            src_ref=src, dst_ref=dst, send_sem=send_sems.at[k - 1], recv_sem=recv_sems.at[k - 1],
            device_id=peer, device_id_type=MESH_ID)
        pairs.append((mk(_slot(land_ref, me, part)), mk(_slot(land_ref, pidx, part))))
    return me, pairs


def _exchange(x, gather, name, relayed=False):
    def body(x_ref, o_ref, send_sems, recv_sems, local_sem):
        if relayed:
            phases = _relayed_gather_phases(x_ref, o_ref, send_sems, recv_sems, local_sem)
        else:
            phases = _direct_phases(x_ref, o_ref, send_sems, recv_sems, local_sem, gather)
        for phase in phases:
            if phase is not None:
                phase()

    return pl.pallas_call(
        body, name=name, out_shape=jax.ShapeDtypeStruct(((NDEV,) + x.shape) if gather else x.shape, x.dtype),
        in_specs=[pl.BlockSpec(memory_space=pl.ANY)], out_specs=pl.BlockSpec(memory_space=pl.ANY),
        scratch_shapes=[pltpu.SemaphoreType.DMA((NDEV - 1,)), pltpu.SemaphoreType.DMA((NDEV - 1,)), pltpu.SemaphoreType.DMA],
    )(x)


def _direct_phases(x_ref, zone, send_sems, recv_sems, local_sem, gather, part=None):
    me, pairs = _exchange_copies(x_ref, zone, send_sems, recv_sems, gather, part)
    local = pltpu.make_async_copy(x_ref if gather else _slot(x_ref, me, part), _slot(zone, me, part), local_sem)

    def start():
        for send, _ in pairs:
            send.start()
        local.start()

    def finish():
        for send, arrival in pairs:
            send.wait_send()
            arrival.wait_recv()
        local.wait()

    return start, None, finish


def _relayed_gather_phases(x_ref, zone, send_sems, recv_sems, local_sem):
    mx, my, mc = _me()
    me, sibling = (mx, my, mc), (mx, my, 1 - mc)
    chips = [(1 - mx, my), (mx, 1 - my), (1 - mx, 1 - my)]
    rows = lambda dev: zone.at[4 * dev[0] + 2 * dev[1] + dev[2]]

    def copy(k, block, to, src=None):
        return pltpu.make_async_remote_copy(src_ref=rows(block) if src is None else src, dst_ref=rows(block),
                                            send_sem=send_sems.at[k], recv_sem=recv_sems.at[k], device_id=to,
                                            device_id_type=MESH_ID)

    local = pltpu.make_async_copy(x_ref, rows(me), local_sem)
    first = [copy(0, me, sibling, x_ref)] + [copy(1 + j, me, (*chip, mc), x_ref) for j, chip in enumerate(chips)]
    passed = [copy(4 + j, (*chip, mc), sibling) for j, chip in enumerate(chips)]
    over_links = [copy(1 + j, (*chip, mc), me) for j, chip in enumerate(chips)]
    from_sibling = [copy(0, sibling, me)] + [copy(4 + j, (*chip, 1 - mc), me) for j, chip in enumerate(chips)]

    def start():
        local.start()
        for cp in first:
            cp.start()

    def relay():
        for arrival, onward in zip(over_links, passed):
            arrival.wait_recv()
            onward.start()

    def finish():
        for arrival in from_sibling:
            arrival.wait_recv()
        for cp in first + passed:
            cp.wait_send()
        local.wait()

    return start, relay, finish


class _Rider:
    def __init__(self, x, gather, land=None, slot=None, n_slots=None, relayed=False, part=None):
        self.x, self.gather, self.land, self.slot, self.relayed, self.part = x, gather, land, slot, relayed, part
        own = ((NDEV,) + x.shape) if gather else x.shape
        self.land_shape = land.shape if land is not None else (own if n_slots is None else (n_slots,) + own)

    def phases(self, x_ref, land_ref, send_sems, recv_sems, local_sem):
        zone = land_ref if self.slot is None else land_ref.at[self.slot]
        if self.relayed:
            return _relayed_gather_phases(x_ref, zone, send_sems, recv_sems, local_sem)
        return _direct_phases(x_ref, zone, send_sems, recv_sems, local_sem, self.gather, self.part)


def _ride(call_name, grid, riders, inner, in_specs, out_specs, out_shape, scratch_shapes, compiler_params, operands):
    n_in, n_out, n_scr = len(in_specs), len(out_specs), len(scratch_shapes)
    any_spec = pl.BlockSpec(memory_space=pl.ANY)
    extra_in, aliases = [], {}
    for r_idx, r in enumerate(riders):
        extra_in.append(r.x)
        if r.land is not None:
            aliases[n_in + len(extra_in)] = n_out + r_idx
            extra_in.append(r.land)
    sems = []
    for _ in riders:
        sems += [pltpu.SemaphoreType.DMA((NDEV - 1,)), pltpu.SemaphoreType.DMA((NDEV - 1,)), pltpu.SemaphoreType.DMA]

    def body(*refs):
        base_in, rest = refs[:n_in], refs[n_in:]
        rider_in, rest = rest[:len(extra_in)], rest[len(extra_in):]
        base_out, rest = rest[:n_out], rest[n_out:]
        lands, rest = rest[:len(riders)], rest[len(riders):]
        base_scr, rider_sems = rest[:n_scr], rest[n_scr:]
        step = 0
        for a, g in enumerate(grid):
            step = step * g + pl.program_id(a)
        n_steps = math.prod(grid)
        sets, pos = [], 0
        for r_idx, r in enumerate(riders):
            x_ref = rider_in[pos]
            pos += 2 if r.land is not None else 1
            sets.append(r.phases(x_ref, lands[r_idx], *rider_sems[3 * r_idx:3 * r_idx + 3]))

        if sets:
            @pl.when(step == 0)
            def _():
                for start, _, _ in sets:
                    start()

        inner(*base_in, *base_out, *base_scr)

        if any(relay is not None for _, relay, _ in sets):
            @pl.when(step == (3 * n_steps) // 5)
            def _():
                for _, relay, _ in sets:
                    if relay is not None:
                        relay()

        if sets:
            @pl.when(step == n_steps - 1)
            def _():
                for _, _, finish in sets:
                    finish()

    outs = pl.pallas_call(
        body, name=call_name, grid=grid, in_specs=list(in_specs) + [any_spec] * len(extra_in),
        out_specs=list(out_specs) + [any_spec] * len(riders),
        out_shape=list(out_shape) + [jax.ShapeDtypeStruct(r.land_shape, r.x.dtype) for r in riders],
        scratch_shapes=list(scratch_shapes) + sems, input_output_aliases=aliases, compiler_params=compiler_params,
    )(*operands, *extra_in)
    return outs[:n_out], outs[n_out:]


def _mm(name, a, b, dn, grid, a_spec, b_spec, o_spec, out_shape, out_dtype, acc_shape, riders=(), pieces=None):
    nk = grid[2]

    def body(a_ref, b_ref, o_ref, *scratch):
        if pieces is None:
            part = _dot(a_ref[...].astype(BF16), b_ref[...].astype(BF16), dn)
        else:
            part = functools.reduce(lambda p, q: p + q, [_dot(a_g.astype(BF16), b_g.astype(BF16), dn)
                                                         for a_g, b_g in pieces(a_ref, b_ref)])
        if nk == 1:
            o_ref[...] = part.astype(o_ref.dtype)
            return
        acc_ref = scratch[0]
        k = pl.program_id(2)

        @pl.when(k == 0)
        def _():
            acc_ref[...] = part

        @pl.when(k > 0)
        def _():
            acc_ref[...] += part

        @pl.when(k == nk - 1)
        def _():
            o_ref[...] = acc_ref[...].astype(o_ref.dtype)

    (out,), lands = _ride(name, grid, riders, body, [a_spec, b_spec], [o_spec], [jax.ShapeDtypeStruct(out_shape, out_dtype)],
                          [] if nk == 1 else [pltpu.VMEM(acc_shape, F32)], _params(3), [a, b])
    return (out, *lands) if riders else out


BM, BN, BK = 1024, 1024, 4096


def _mm_plain(name, a, b, dn, out_dtype):
    if dn == NN:
        (m, kk), n = a.shape, b.shape[1]
    elif dn == NT:
        (m, kk), n = a.shape, b.shape[0]
    else:
        (kk, m), n = a.shape, b.shape[1]
    half = 2 if dn == TN else 1
    bm, bn, bk = _tile(m, BM // half), _tile(n, BN // half), _tile(kk, BK)
    a_spec = pl.BlockSpec((bk, bm), lambda i, j, k: (k, i)) if dn == TN else pl.BlockSpec((bm, bk), lambda i, j, k: (i, k))
    b_spec = pl.BlockSpec((bn, bk), lambda i, j, k: (j, k)) if dn == NT else pl.BlockSpec((bk, bn), lambda i, j, k: (k, j))
    return _mm(name, a, b, dn, (m // bm, n // bn, kk // bk), a_spec, b_spec,
               pl.BlockSpec((bm, bn), lambda i, j, k: (i, j)), (m, n), out_dtype, (bm, bn))


def _row_spec(tile, width, col=0):
    return pl.BlockSpec((tile, width), lambda i: (i, col))


def _vec_spec(width, col=0):
    return pl.BlockSpec((1, width), lambda i: (0, col))


def _rowwise_fwd(name, fn, rows, row_specs, vecs, vec_specs, out_shapes, out_specs, n_tiles):
    nr, nv = len(rows), len(vecs)

    def body(*refs):
        ins = [r[...].astype(F32) for r in refs[:nr + nv]]
        outs = fn(*ins)
        for o_ref, o in zip(refs[nr + nv:], outs):
            o_ref[...] = o.astype(o_ref.dtype)

    return pl.pallas_call(body, name=name, grid=(n_tiles,), in_specs=list(row_specs) + list(vec_specs),
                          out_specs=list(out_specs), out_shape=list(out_shapes), compiler_params=_params(1))(*rows, *vecs)


def _rowwise_bwd(name, fn, rows, row_specs, vecs, vec_specs, cts, ct_specs, ct_groups,
                 drow_idx, drow_shapes, drow_specs, dvec_idx, dvec_shapes, dvec_specs, n_tiles, riders=()):
    nr, nv, nc = len(rows), len(vecs), len(cts)

    def body(*refs):
        ins = [r[...].astype(F32) for r in refs[:nr + nv]]
        ct_vals = [r[...].astype(F32) for r in refs[nr + nv:nr + nv + nc]]
        out_refs = refs[nr + nv + nc:]
        _, vjp = jax.vjp(fn, *ins)
        grads = vjp(tuple(functools.reduce(lambda p, q: p + q, [ct_vals[j] for j in grp]) for grp in ct_groups))
        for o_ref, idx in zip(out_refs[:len(drow_idx)], drow_idx):
            o_ref[...] = grads[idx].astype(o_ref.dtype)
        step = pl.program_id(0)
        for o_ref, idx in zip(out_refs[len(drow_idx):], dvec_idx):
            @pl.when(step == 0)
            def _(o_ref=o_ref):
                o_ref[...] = jnp.zeros_like(o_ref)
            o_ref[...] += grads[nr + idx]

    outs, lands = _ride(name, (n_tiles,), riders, body, list(row_specs) + list(vec_specs) + list(ct_specs),
                        list(drow_specs) + list(dvec_specs), list(drow_shapes) + list(dvec_shapes), [], _params(1),
                        [*rows, *vecs, *cts])
    return (outs, lands) if riders else outs


def _rms(x):
    return x * lax.rsqrt(jnp.mean(x * x, axis=-1, keepdims=True) + EPS)


def _seg_in(x, shift, scale, gain):
    return _rms(x) * gain * (1.0 + scale) + shift, x


def _seg_qk(q, k, qg, kg):
    def norm(t, g, mult):
        blocks = []
        lane = lax.broadcasted_iota(jnp.int32, (1, LANES), 1)
        for p in range(t.shape[1] // LANES):
            tb = t[:, p * LANES:(p + 1) * LANES]
            sq = tb * tb
            lo = jnp.sum(jnp.where(lane < HEAD_DIM, sq, 0.0), axis=-1, keepdims=True)
            hi = jnp.sum(jnp.where(lane < HEAD_DIM, 0.0, sq), axis=-1, keepdims=True)
            ms = jnp.where(lane < HEAD_DIM, lo, hi) * (1.0 / HEAD_DIM)
            blocks.append(tb * lax.rsqrt(ms + EPS) * (g[:, p * LANES:(p + 1) * LANES] * mult))
        return jnp.concatenate(blocks, axis=-1) if len(blocks) > 1 else blocks[0]
    return norm(q, qg, 1.0 / math.sqrt(HEAD_DIM)), norm(k, kg, 1.0)


def _seg_gelu(ypre):
    return (jax.nn.gelu(ypre),)


def _seg_mix(y1, z, yattn, g_ssm, g_attn):
    ys = y1 * jax.nn.sigmoid(z)
    return (jnp.concatenate([_rms(ys) * g_ssm, _rms(yattn) * g_attn], axis=-1),)


def _seg_mid(x, o, gate_m, g_ffn, scale_f, shift_f):
    h1 = x + gate_m * o
    return h1, _rms(h1) * g_ffn * (1.0 + scale_f) + shift_f


def _seg_act(gate, up):
    return (jax.nn.silu(gate) * up,)


def _s5_lam(a_re, a_im, log_dt):
    dt = jnp.exp(log_dt)
    mag = jnp.exp(a_re * dt)
    lr, li = mag * jnp.cos(a_im * dt), mag * jnp.sin(a_im * dt)
    den = a_re * a_re + a_im * a_im
    nr, ni = lr - 1.0, li
    return lr, li, (nr * a_re + ni * a_im) / den, (ni * a_re - nr * a_im) / den


def _s5_bbar(coef_re, coef_im, b_re, b_im):
    return coef_re * b_re - coef_im * b_im, coef_re * b_im + coef_im * b_re


def _whole(name, fn, ins, out_shapes):
    n = len(ins)

    def body(*refs):
        outs = fn(*[r[...] for r in refs[:n]])
        for o_ref, o in zip(refs[n:], outs):
            o_ref[...] = o

    return pl.pallas_call(body, name=name, out_shape=[jax.ShapeDtypeStruct(s, F32) for s in out_shapes])(*ins)


def _whole_vjp(name, fn, ins, cts, out_shapes):
    n, nc = len(ins), len(cts)

    def body(*refs):
        _, vjp = jax.vjp(fn, *[r[...] for r in refs[:n]])
        grads = vjp(tuple(r[...] for r in refs[n:n + nc]))
        for o_ref, g in zip(refs[n + nc:], grads):
            o_ref[...] = g

    return pl.pallas_call(body, name=name, out_shape=[jax.ShapeDtypeStruct(s, F32) for s in out_shapes])(*ins, *cts)


SCAN_SHIFTS = (1, 2, 4)


def _cmul(ar, ai, br, bi):
    return ar * br - ai * bi, ar * bi + ai * br


def _scan_coefs(lr, li, reverse):
    s = lr.shape[1]
    row = lax.broadcasted_iota(jnp.int32, (SUBLANES, s), 0)
    p1 = (lr, li)
    p2 = _cmul(*p1, *p1)
    p4 = _cmul(*p2, *p2)
    p8 = _cmul(*p4, *p4)
    p3, p5, p6 = _cmul(*p1, *p2), _cmul(*p4, *p1), _cmul(*p4, *p2)
    p7 = _cmul(*p6, *p1)
    pows = (p1, p2, p3, p4, p5, p6, p7, p8)
    bc = lambda t: jnp.broadcast_to(t, (SUBLANES, s))
    steps = []
    for sh, pw in zip(SCAN_SHIFTS, (p1, p2, p4)):
        keep = (row + sh <= SUBLANES - 1) if reverse else (row >= sh)
        steps.append((jnp.where(keep, bc(pw[0]), 0.0), jnp.where(keep, bc(pw[1]), 0.0)))
    cr, ci = jnp.zeros((SUBLANES, s), F32), jnp.zeros((SUBLANES, s), F32)
    for r in range(SUBLANES):
        pw = pows[SUBLANES - 1 - r] if reverse else pows[r]
        cr = jnp.where(row == r, bc(pw[0]), cr)
        ci = jnp.where(row == r, bc(pw[1]), ci)
    return steps, (cr, ci)


def _scan_tile(xr, xi, steps, carry_pow, cr, ci, reverse):
    for sh, (ar, ai) in zip(SCAN_SHIFTS, steps):
        rs = SUBLANES - sh if reverse else sh
        sr, si = pltpu.roll(xr, rs, 0), pltpu.roll(xi, rs, 0)
        xr, xi = xr + ar * sr - ai * si, xi + ar * si + ai * sr
    pr, pi = carry_pow
    return xr + pr * cr - pi * ci, xi + pr * ci + pi * cr


def _s5_forward(proj, b_blk_re, b_blk_im, c_blk_re, c_blk_im, lam_re, lam_im, d_skip, n_blk, t_chunk, riders=()):
    seq = proj.shape[0]
    n_chunks = seq // t_chunk
    n_tiles = t_chunk // SUBLANES
    s = b_blk_re.shape[2]

    def body(u_ref, bre_ref, bim_ref, cre_ref, cim_ref, lr_ref, li_ref, d_ref, y_ref, xr_ref, xi_ref, wr, wi, carry):
        t = pl.program_id(1)

        @pl.when(t == 0)
        def _():
            carry[...] = jnp.zeros_like(carry)

        u = u_ref[...]
        wr[...] = _dot3(u, bre_ref[...], NN)
        wi[...] = _dot3(u, bim_ref[...], NN)
        steps, cpow = _scan_coefs(lr_ref[...], li_ref[...], False)

        def tile(i, c):
            r0 = pl.multiple_of(i * SUBLANES, SUBLANES)
            xr, xi = _scan_tile(wr[pl.ds(r0, SUBLANES), :], wi[pl.ds(r0, SUBLANES), :], steps, cpow, c[0], c[1], False)
            xr_ref[pl.ds(r0, SUBLANES), :] = xr
            xi_ref[pl.ds(r0, SUBLANES), :] = xi
            last = SUBLANES - 1
            return (jnp.broadcast_to(xr[last:, :], xr.shape), jnp.broadcast_to(xi[last:, :], xi.shape))

        cr, ci = lax.fori_loop(0, n_tiles, tile, (carry[0], carry[1]))
        carry[0] = cr
        carry[1] = ci
        y = (_dot(xr_ref[...].astype(BF16), cre_ref[...].astype(BF16))
             - _dot(xi_ref[...].astype(BF16), cim_ref[...].astype(BF16)))
        y_ref[...] = y + d_ref[...] * u

    blk = lambda shape: pl.BlockSpec((None,) + shape, lambda j, t: (j, 0, 0))
    return _ride(
        "s5_fwd", (n_blk, n_chunks), riders, body,
        [pl.BlockSpec((t_chunk, LANES), lambda j, t: (t, j)), blk((LANES, s)), blk((LANES, s)),
         blk((s, LANES)), blk((s, LANES)), blk((1, s)), blk((1, s)), pl.BlockSpec((1, LANES), lambda j, t: (0, j))],
        [pl.BlockSpec((t_chunk, LANES), lambda j, t: (t, j)), pl.BlockSpec((t_chunk, s), lambda j, t: (t, j)),
         pl.BlockSpec((t_chunk, s), lambda j, t: (t, j))],
        [jax.ShapeDtypeStruct((seq, n_blk * LANES), F32), jax.ShapeDtypeStruct((seq, n_blk * s), F32),
         jax.ShapeDtypeStruct((seq, n_blk * s), F32)],
        [pltpu.VMEM((t_chunk, s), F32), pltpu.VMEM((t_chunk, s), F32), pltpu.VMEM((2, SUBLANES, s), F32)],
        _params(2), [proj, b_blk_re, b_blk_im, c_blk_re, c_blk_im, lam_re, lam_im, d_skip])


def _s5_backward(dypre, proj, x_re, x_im, b_blk_re, b_blk_im, c_blk_re, c_blk_im, lam_re, lam_im, d_skip, n_blk, t_chunk,
                 riders=()):
    seq = proj.shape[0]
    n_chunks = seq // t_chunk
    n_tiles = t_chunk // SUBLANES
    s = b_blk_re.shape[2]

    def body(dy_ref, u_ref, xr_ref, xi_ref, pr_ref, pi_ref, bre_ref, bim_ref, cre_ref, cim_ref, lr_ref, li_ref, d_ref,
             du_ref, dbre_ref, dbim_ref, dcre_ref, dcim_ref, dlr_ref, dli_ref, dd_ref, gr, gi, carry):
        t = pl.program_id(1)

        @pl.when(t == 0)
        def _():
            carry[...] = jnp.zeros_like(carry)
            for r in (dbre_ref, dbim_ref, dcre_ref, dcim_ref, dlr_ref, dli_ref, dd_ref):
                r[...] = jnp.zeros_like(r)

        dy = dy_ref[...]
        dyb = dy.astype(BF16)
        u = u_ref[...]
        gr[...] = _dot3(dy, cre_ref[...], NT)
        gi[...] = -_dot3(dy, cim_ref[...], NT)
        steps, cpow = _scan_coefs(lr_ref[...], -li_ref[...], True)
        row = lax.broadcasted_iota(jnp.int32, (SUBLANES, s), 0)
        last = SUBLANES - 1
        first_chunk = t == n_chunks - 1

        def tile_at(r0, prev_r, prev_i, c):
            cr, ci, ar, ai = c
            lr_, li_ = _scan_tile(gr[pl.ds(r0, SUBLANES), :], gi[pl.ds(r0, SUBLANES), :], steps, cpow, cr, ci, True)
            gr[pl.ds(r0, SUBLANES), :] = lr_
            gi[pl.ds(r0, SUBLANES), :] = li_
            xr, xi = xr_ref[pl.ds(r0, SUBLANES), :], xi_ref[pl.ds(r0, SUBLANES), :]
            xpr = jnp.where(row == 0, jnp.broadcast_to(prev_r[last:, :], xr.shape), pltpu.roll(xr, 1, 0))
            xpi = jnp.where(row == 0, jnp.broadcast_to(prev_i[last:, :], xi.shape), pltpu.roll(xi, 1, 0))
            ar = ar + lr_ * xpr + li_ * xpi
            ai = ai + li_ * xpr - lr_ * xpi
            return (jnp.broadcast_to(lr_[:1, :], lr_.shape), jnp.broadcast_to(li_[:1, :], li_.shape), ar, ai)

        def tile(ii, c):
            i = n_tiles - 1 - ii
            r0 = pl.multiple_of(i * SUBLANES, SUBLANES)
            rp = pl.multiple_of(r0 - SUBLANES, SUBLANES)
            return tile_at(r0, xr_ref[pl.ds(rp, SUBLANES), :], xi_ref[pl.ds(rp, SUBLANES), :], c)

        zero = jnp.zeros((SUBLANES, s), F32)
        c = lax.fori_loop(0, n_tiles - 1, tile, (carry[0], carry[1], zero, zero))
        keep = jnp.where(first_chunk, 0.0, 1.0)
        c = tile_at(0, pr_ref[...] * keep, pi_ref[...] * keep, c)
        carry[0] = c[0]
        carry[1] = c[1]
        dlr_ref[...] += jnp.sum(c[2], axis=0, keepdims=True)
        dli_ref[...] += jnp.sum(c[3], axis=0, keepdims=True)

        lam_r, lam_i = gr[...].astype(BF16), gi[...].astype(BF16)
        du_ref[...] = (_dot(lam_r, bre_ref[...].astype(BF16), NT) + _dot(lam_i, bim_ref[...].astype(BF16), NT)
                       + d_ref[...] * dy).astype(du_ref.dtype)
        ub = u.astype(BF16)
        dbre_ref[...] += _dot(ub, lam_r, TN)
        dbim_ref[...] += _dot(ub, lam_i, TN)
        dcre_ref[...] += _dot(xr_ref[...].astype(BF16), dyb, TN)
        dcim_ref[...] -= _dot(xi_ref[...].astype(BF16), dyb, TN)
        dd_ref[...] += jnp.sum(dy * u, axis=0, keepdims=True)

    rev = lambda t: n_chunks - 1 - t
    blk = lambda shape: pl.BlockSpec((None,) + shape, lambda j, t: (j, 0, 0))
    tpc = t_chunk // SUBLANES
    prev_spec = pl.BlockSpec((SUBLANES, s), lambda j, t: (jnp.maximum(rev(t) * tpc - 1, 0), j))
    chunk = lambda w: pl.BlockSpec((t_chunk, w), lambda j, t: (rev(t), j))
    return _ride(
        "s5_bwd", (n_blk, n_chunks), riders, body,
        [chunk(LANES), chunk(LANES), chunk(s), chunk(s), prev_spec, prev_spec, blk((LANES, s)), blk((LANES, s)),
         blk((s, LANES)), blk((s, LANES)), blk((1, s)), blk((1, s)), pl.BlockSpec((1, LANES), lambda j, t: (0, j))],
        [chunk(LANES), blk((LANES, s)), blk((LANES, s)), blk((s, LANES)), blk((s, LANES)), blk((1, s)), blk((1, s)),
         pl.BlockSpec((1, LANES), lambda j, t: (0, j))],
        [jax.ShapeDtypeStruct((seq, n_blk * LANES), BF16),
         jax.ShapeDtypeStruct((n_blk, LANES, s), F32), jax.ShapeDtypeStruct((n_blk, LANES, s), F32),
         jax.ShapeDtypeStruct((n_blk, s, LANES), F32), jax.ShapeDtypeStruct((n_blk, s, LANES), F32),
         jax.ShapeDtypeStruct((n_blk, 1, s), F32), jax.ShapeDtypeStruct((n_blk, 1, s), F32),
         jax.ShapeDtypeStruct((1, n_blk * LANES), F32)],
        [pltpu.VMEM((t_chunk, s), F32), pltpu.VMEM((t_chunk, s), F32), pltpu.VMEM((2, SUBLANES, s), F32)],
        _params(2), [dypre, proj, x_re, x_im, x_re, x_im, b_blk_re, b_blk_im, c_blk_re, c_blk_im, lam_re, lam_im, d_skip])


TQ, TK = 256, 128


def _split_bf16(x):
    hi = x.astype(BF16)
    return hi, (x - hi.astype(F32)).astype(BF16)


def _sb_weights(z, past, carry, tri):
    ls = jnp.minimum(z, 0.0) - jnp.log(1.0 + jnp.exp(-jnp.abs(z)))
    lk = ls - z
    if past is not None:
        lk = jnp.where(past, lk, 0.0)
    w = jnp.exp(ls + _dot(lk.astype(BF16), tri) + carry)
    if past is not None:
        w = jnp.where(past, w, 0.0)
    return ls, lk, w


LOG_KEEP_DEAD = -104.0


def _walk_key_blocks(i, ratio, prologue, block, epilogue, log_keep):
    n_kb = (i + 1) * ratio
    prologue(n_kb - 1)
    for n in range(ratio):
        block(n_kb - 1 - n, n % 2, True)
    assert ratio % 2 == 0
    n_pairs = (i * ratio) // 2

    def more(state):
        t, alive = state
        return jnp.logical_and(t < n_pairs, alive)

    def pair(state):
        t, _ = state
        j = n_kb - 1 - ratio - 2 * t
        block(j, ratio % 2, False)
        block(j - 1, (ratio + 1) % 2, False)
        return t + 1, log_keep() >= LOG_KEEP_DEAD

    done, _ = lax.while_loop(more, pair, (jnp.int32(0), log_keep() >= LOG_KEEP_DEAD))
    epilogue(n_kb - ratio - 2 * done)


def _attention_forward(qh, kh, proj, v_col, n_pair, tq, tk, riders=()):
    seq = qh.shape[0]
    ratio = tq // tk

    def body(q_ref, k_ref, v_ref, o_ref, q_scr, z_scr, w_scr, acc_scr, c_scr):
        i = pl.program_id(1)
        lane = lax.broadcasted_iota(jnp.int32, (1, LANES), 1)
        q2 = q_ref[...]
        q_scr[0] = jnp.where(lane < HEAD_DIM, q2, 0.0).astype(BF16)
        q_scr[1] = jnp.where(lane < HEAD_DIM, 0.0, q2).astype(BF16)
        tri = (lax.broadcasted_iota(jnp.int32, (tk, tk), 0) > lax.broadcasted_iota(jnp.int32, (tk, tk), 1)).astype(BF16)
        qpos = i * tq + lax.broadcasted_iota(jnp.int32, (tq, tk), 0)
        kidx = lax.broadcasted_iota(jnp.int32, (tq, tk), 1)

        def rows(ref, j):
            j = jnp.clip(j, 0, seq // tk - 1)
            return ref[pl.ds(pl.multiple_of(j * tk, tk), tk), :].astype(BF16)

        def scores(j, slot):
            kb = rows(k_ref, j)
            for h in range(2):
                z_scr[slot, h] = _dot(q_scr[h], kb, NT)

        def finish(j):
            vb = rows(v_ref, j)
            for h in range(2):
                acc_scr[h] += _dot(w_scr[h], vb)

        def prologue(j):
            w_scr[...] = jnp.zeros_like(w_scr)
            acc_scr[...] = jnp.zeros_like(acc_scr)
            c_scr[...] = jnp.zeros_like(c_scr)
            scores(j, 0)

        def block(j, slot, masked):
            scores(j - 1, 1 - slot)
            finish(j + 1)
            past = ((kidx + j * tk) < qpos) if masked else None
            for h in range(2):
                _, lk, w = _sb_weights(z_scr[slot, h], past, c_scr[h], tri)
                w_scr[h] = w.astype(BF16)
                c_scr[h] += jnp.sum(lk, axis=-1, keepdims=True)

        _walk_key_blocks(i, ratio, prologue, block, finish, lambda: jnp.max(c_scr[...]))
        o_ref[...] = jnp.where(lane < HEAD_DIM, acc_scr[0], acc_scr[1])

    (out,), lands = _ride(
        "attn_fwd", (n_pair, seq // tq), riders, body,
        [pl.BlockSpec((tq, LANES), lambda p, i: (i, p)), pl.BlockSpec((seq, LANES), lambda p, i: (0, p)),
         pl.BlockSpec((seq, LANES), lambda p, i: (0, v_col + p))],
        [pl.BlockSpec((tq, LANES), lambda p, i: (i, p))], [jax.ShapeDtypeStruct(qh.shape, F32)],
        [pltpu.VMEM((2, tq, LANES), BF16), pltpu.VMEM((2, 2, tq, tk), F32), pltpu.VMEM((2, tq, tk), BF16),
         pltpu.VMEM((2, tq, LANES), F32), pltpu.VMEM((2, tq, 1), F32)],
        _params(2), [qh, kh, proj])
    return out, lands


def _attention_backward(qh, kh, proj, v_col, y, dy, n_pair, tq, tk, riders=()):
    seq = qh.shape[0]

    ratio = tq // tk

    n_kblk = seq // tk

    def body(q_ref, k_ref, v_ref, y_ref, dy_ref, dq_ref, dk_ref, dv_ref,
             q_scr, do_scr, qt_scr, dot_scr, dkt_scr, dvt_scr, z_scr, dw_scr, w_scr, dz_scr, dq_scr, c_scr, c2_scr, tot_scr):
        i = pl.program_id(1)

        @pl.when(i == 0)
        def _():
            dkt_scr[...] = jnp.zeros_like(dkt_scr)
            dvt_scr[...] = jnp.zeros_like(dvt_scr)

        lane = lax.broadcasted_iota(jnp.int32, (1, LANES), 1)
        sel = (lane < HEAD_DIM, lane >= HEAD_DIM)
        q2, do2 = q_ref[...], dy_ref[...].astype(BF16)
        do2f = do2.astype(F32)
        dot_oy = do2f * y_ref[...]
        for h in range(2):
            qm, dm = jnp.where(sel[h], q2, 0.0), jnp.where(sel[h], do2f, 0.0)
            q_scr[h] = qm.astype(BF16)
            do_scr[h] = dm.astype(BF16)
            qt_scr[h] = qm.T.astype(BF16)
            dot_scr[h] = dm.T.astype(BF16)
            tot_scr[h] = jnp.sum(jnp.where(sel[h], dot_oy, 0.0), axis=-1, keepdims=True)
        r_i, c_i = lax.broadcasted_iota(jnp.int32, (tk, tk), 0), lax.broadcasted_iota(jnp.int32, (tk, tk), 1)
        tri = (r_i > c_i).astype(BF16)
        tri_ge = (r_i >= c_i).astype(BF16)
        qpos = i * tq + lax.broadcasted_iota(jnp.int32, (tq, tk), 0)
        kidx = lax.broadcasted_iota(jnp.int32, (tq, tk), 1)

        def start(j):
            return pl.multiple_of(jnp.clip(j, 0, seq // tk - 1) * tk, tk)

        def scores(j, slot):
            c0 = start(j)
            kb, vb = k_ref[pl.ds(c0, tk), :].astype(BF16), v_ref[pl.ds(c0, tk), :].astype(BF16)
            for h in range(2):
                z_scr[slot, h] = _dot(q_scr[h], kb, NT)
                dw_scr[slot, h] = _dot(do_scr[h], vb, NT)

        def finish(j):
            jc = jnp.clip(j, 0, n_kblk - 1)
            kb = k_ref[pl.ds(pl.multiple_of(jc * tk, tk), tk), :].astype(BF16)
            dkt_add, dvt_add = jnp.zeros((LANES, tk), F32), jnp.zeros((LANES, tk), F32)
            for h in range(2):
                dz = dz_scr[h]
                dq_scr[h] += _dot(dz, kb)
                dkt_add = dkt_add + _dot(qt_scr[h], dz)
                dvt_add = dvt_add + _dot(dot_scr[h], w_scr[h])
            dkt_scr[jc] += dkt_add
            dvt_scr[jc] += dvt_add

        def prologue(j):
            for r in (w_scr, dz_scr, dq_scr, c_scr, c2_scr):
                r[...] = jnp.zeros_like(r)
            scores(j, 0)

        def block(j, slot, masked):
            scores(j - 1, 1 - slot)
            finish(j + 1)
            past = ((kidx + j * tk) < qpos) if masked else None
            for h in range(2):
                ls, lk, w = _sb_weights(z_scr[slot, h], past, c_scr[h], tri)
                wb = w.astype(BF16)
                dlw = dw_scr[slot, h] * wb.astype(F32)
                hi, lo = _split_bf16(dlw)
                dlk = tot_scr[h] - c2_scr[h] - (_dot(hi, tri_ge) + _dot(lo, tri_ge))
                if masked:
                    dlk = jnp.where(past, dlk, 0.0)
                sig = jnp.exp(ls)
                w_scr[h] = wb
                dz_scr[h] = (dlw * (1.0 - sig) - dlk * sig).astype(BF16)
                c_scr[h] += jnp.sum(lk, axis=-1, keepdims=True)
                c2_scr[h] += jnp.sum(dlw, axis=-1, keepdims=True)

        _walk_key_blocks(i, ratio, prologue, block, finish, lambda: jnp.max(c_scr[...]))
        dq_ref[...] = jnp.where(sel[0], dq_scr[0], dq_scr[1])

        @pl.when(i == seq // tq - 1)
        def _():
            for jb in range(n_kblk):
                dk_ref[jb * tk:(jb + 1) * tk, :] = dkt_scr[jb].T
                dv_ref[jb * tk:(jb + 1) * tk, :] = dvt_scr[jb].T.astype(dv_ref.dtype)

    blk = pl.BlockSpec((tq, LANES), lambda p, i: (i, p))
    full = pl.BlockSpec((seq, LANES), lambda p, i: (0, p))
    shape = jax.ShapeDtypeStruct(qh.shape, F32)
    return _ride(
        "attn_bwd", (n_pair, seq // tq), riders, body,
        [blk, full, pl.BlockSpec((seq, LANES), lambda p, i: (0, v_col + p)), blk, blk],
        [blk, full, full], [shape, shape, jax.ShapeDtypeStruct(qh.shape, BF16)],
        [pltpu.VMEM((2, tq, LANES), BF16), pltpu.VMEM((2, tq, LANES), BF16),
         pltpu.VMEM((2, LANES, tq), BF16), pltpu.VMEM((2, LANES, tq), BF16),
         pltpu.VMEM((n_kblk, LANES, tk), F32), pltpu.VMEM((n_kblk, LANES, tk), F32),
         pltpu.VMEM((2, 2, tq, tk), F32), pltpu.VMEM((2, 2, tq, tk), F32),
         pltpu.VMEM((2, tq, tk), BF16), pltpu.VMEM((2, tq, tk), BF16), pltpu.VMEM((2, tq, LANES), F32),
         pltpu.VMEM((2, tq, 1), F32), pltpu.VMEM((2, tq, 1), F32), pltpu.VMEM((2, tq, 1), F32)],
        _params(2), [qh, kh, proj, y, dy])


def _loss_head(h1, ffn, target, gate_f, tile):
    seq, d = h1.shape

    def body(h_ref, f_ref, t_ref, g_ref, dy_ref, df_ref, dg_ref, loss_ref):
        @pl.when(pl.program_id(0) == 0)
        def _():
            dg_ref[...] = jnp.zeros_like(dg_ref)
            loss_ref[...] = jnp.zeros_like(loss_ref)

        f, g = f_ref[...], g_ref[...]
        err = h_ref[...] + g * f - t_ref[...]
        dy = err * (1.0 / d)
        dy_ref[...] = dy
        df_ref[...] = (dy * g).astype(df_ref.dtype)
        dg_ref[...] += jnp.sum(dy * f, axis=0, keepdims=True)
        loss_ref[...] += jnp.sum(jnp.sum(err * err, axis=-1, keepdims=True), axis=0, keepdims=True) * (0.5 / d)

    row = _row_spec(tile, d)
    return pl.pallas_call(
        body, name="loss_head", grid=(seq // tile,), in_specs=[row, row, row, _vec_spec(d)],
        out_specs=[row, row, _vec_spec(d), pl.BlockSpec((1, 1), lambda i: (0, 0))],
        out_shape=[jax.ShapeDtypeStruct((seq, d), F32), jax.ShapeDtypeStruct((seq, d), BF16),
                   jax.ShapeDtypeStruct((1, d), F32), jax.ShapeDtypeStruct((1, 1), F32)],
        compiler_params=_params(1),
    )(h1, ffn, target, gate_f)


def _dot3(a, b, dn):
    ah, al = _split_bf16(a)
    bh, bl = _split_bf16(b)
    return _dot(ah, bh, dn) + (_dot(ah, bl, dn) + _dot(al, bh, dn))


def _ada_forward(c_all, w_shard, b_cols):
    d, n = w_shard.shape
    bk = _tile(d, 512)

    def body(c_ref, w_ref, b_ref, o_ref):
        @pl.when(pl.program_id(0) == 0)
        def _():
            o_ref[...] = jnp.broadcast_to(b_ref[...], o_ref.shape)

        o_ref[...] += _dot3(jax.nn.silu(c_ref[...]), w_ref[...], NN)

    return pl.pallas_call(
        body, name="ada_fwd", grid=(d // bk,),
        in_specs=[pl.BlockSpec((NDEV, bk), lambda k: (0, k)), pl.BlockSpec((bk, n), lambda k: (k, 0)), _vec_spec(n)],
        out_specs=pl.BlockSpec((NDEV, n), lambda k: (0, 0)), out_shape=jax.ShapeDtypeStruct((NDEV, n), F32),
        compiler_params=_params(1),
    )(c_all, w_shard, b_cols)


def _adam(w, g, m, v):
    m = ADAM_B1 * m + (1.0 - ADAM_B1) * g
    v = ADAM_B2 * v + (1.0 - ADAM_B2) * (g * g)
    m_hat = m / (1.0 - ADAM_B1 ** ADAM_STEP)
    v_hat = v / (1.0 - ADAM_B2 ** ADAM_STEP)
    return -ADAM_LR * (m_hat / (jnp.sqrt(v_hat) + ADAM_EPS) + ADAM_WD * w), m, v


def _adam_ada(c_all, dmod_cols, w, m, v):
    d, n = w.shape
    tr = _tile(d, 256)

    def body(c_ref, dm_ref, w_ref, m_ref, v_ref, g_ref, dl_ref, nm_ref, nv_ref):
        g = _dot3(jax.nn.silu(c_ref[...]), dm_ref[...], TN)
        delta, nm, nv = _adam(w_ref[...], g, m_ref[...], v_ref[...])
        g_ref[...] = g
        dl_ref[...] = delta
        nm_ref[...] = nm
        nv_ref[...] = nv

    row = _row_spec(tr, n)
    return pl.pallas_call(
        body, name="adam_ada", grid=(d // tr,),
        in_specs=[pl.BlockSpec((NDEV, tr), lambda i: (0, i)), pl.BlockSpec((NDEV, n), lambda i: (0, 0)), row, row, row],
        out_specs=[row] * 4, out_shape=[jax.ShapeDtypeStruct((d, n), F32)] * 4, compiler_params=_params(1),
    )(c_all, dmod_cols, w, m, v)


def _adam_sum(name, parts, part_spec, w, m, v, tr):
    r, c = w.shape

    def body(p_ref, w_ref, m_ref, v_ref, g_ref, dl_ref, nm_ref, nv_ref):
        g = p_ref[0].astype(F32)
        for k in range(1, NDEV):
            g = g + p_ref[k].astype(F32)
        delta, nm, nv = _adam(w_ref[...], g, m_ref[...], v_ref[...])
        g_ref[...] = g
        dl_ref[...] = delta
        nm_ref[...] = nm
        nv_ref[...] = nv

    row = _row_spec(tr, c)
    return pl.pallas_call(
        body, name=name, grid=(r // tr,), in_specs=[part_spec, row, row, row],
        out_specs=[row] * 4, out_shape=[jax.ShapeDtypeStruct((r, c), F32)] * 4, compiler_params=_params(1),
    )(parts, w, m, v)


GROUPS_PER_BLOCK = LANES // SSM_GROUP


def _to_b_blocks(bb, n_blk, p):
    t = bb.reshape(n_blk, GROUPS_PER_BLOCK, p, SSM_GROUP)
    eye = jnp.eye(GROUPS_PER_BLOCK, dtype=bb.dtype)
    return jnp.einsum("jgph,gk->jghkp", t, eye).reshape(n_blk, LANES, GROUPS_PER_BLOCK * p)


def _from_b_blocks(blk, n_blk, p):
    t = blk.reshape(n_blk, GROUPS_PER_BLOCK, SSM_GROUP, GROUPS_PER_BLOCK, p)
    eye = jnp.eye(GROUPS_PER_BLOCK, dtype=blk.dtype)
    return jnp.einsum("jghkp,gk->jgph", t, eye).reshape(n_blk * GROUPS_PER_BLOCK, p, SSM_GROUP)


def _to_c_blocks(cc, n_blk, p):
    t = cc.reshape(n_blk, GROUPS_PER_BLOCK, SSM_GROUP, p)
    eye = jnp.eye(GROUPS_PER_BLOCK, dtype=cc.dtype)
    return jnp.einsum("jghp,gk->jgpkh", t, eye).reshape(n_blk, GROUPS_PER_BLOCK * p, LANES)


def _from_c_blocks(blk, n_blk, p):
    t = blk.reshape(n_blk, GROUPS_PER_BLOCK, p, GROUPS_PER_BLOCK, SSM_GROUP)
    eye = jnp.eye(GROUPS_PER_BLOCK, dtype=blk.dtype)
    return jnp.einsum("jgpkh,gk->jghp", t, eye).reshape(n_blk * GROUPS_PER_BLOCK, SSM_GROUP, p)


SMALL_LATE = ("b_ada_a", "g_mix")
SMALL_EARLY = ("b_ada_b", "a_re", "a_im", "log_dt", "b_re", "b_im", "c_re", "c_im", "d_skip",
               "q_gain", "k_gain", "g_ssm_out", "g_attn_out", "g_ffn")
PACK_COLS = 1024


def _pack(arrs):
    flat = jnp.concatenate([a.reshape(-1) for a in arrs])
    n = flat.shape[0]
    quantum = SUBLANES * PACK_COLS
    padded = -(-n // quantum) * quantum
    return jnp.pad(flat, (0, padded - n)).reshape(padded // PACK_COLS, PACK_COLS)


def _unpack(packed, like):
    flat, out, off = packed.reshape(-1), [], 0
    for a in like:
        out.append(flat[off:off + a.size].reshape(a.shape))
        off += a.size
    return out


def kernel(x, c, w_ada, b_ada, g_mix, w_in, a_re, a_im, log_dt, b_re, b_im, c_re, c_im, d_skip, w_glu, q_gain, k_gain, g_ssm_out, g_attn_out, w_out, g_ffn, w_gate, w_up, w_down, loss_target, m_w_ada, m_b_ada, m_g_mix, m_w_in, m_a_re, m_a_im, m_log_dt, m_b_re, m_b_im, m_c_re, m_c_im, m_d_skip, m_w_glu, m_q_gain, m_k_gain, m_g_ssm_out, m_g_attn_out, m_w_out, m_g_ffn, m_w_gate, m_w_up, m_w_down, v_w_ada, v_b_ada, v_g_mix, v_w_in, v_a_re, v_a_im, v_log_dt, v_b_re, v_b_im, v_c_re, v_c_im, v_d_skip, v_w_glu, v_q_gain, v_k_gain, v_g_ssm_out, v_g_attn_out, v_w_out, v_g_ffn, v_w_gate, v_w_up, v_w_down):
    given = dict(locals())
    seq, d = x.shape[1], x.shape[2]
    xs, tgt = x[0], loss_target[0]
    n_groups, p_state = a_re.shape[1], a_re.shape[2]
    w_ssm = n_groups * SSM_GROUP
    w_attn = w_in.shape[2] * NDEV - w_ssm
    w_attn //= 3
    n_blk, n_pair = w_ssm // LANES, w_attn // LANES
    n_heads = w_attn // HEAD_DIM
    ns_in, ns_ff = w_in.shape[2], w_gate.shape[2]
    d_mix = w_ssm + w_attn
    mx, my, mc = _me()
    me = 4 * mx + 2 * my + mc
    rt = _tile(seq, 256)
    n_rt = seq // rt
    sds = jax.ShapeDtypeStruct

    c_all = _exchange(c, True, "comm_ag_c").reshape(NDEV, d)
    n_ada = w_ada.shape[2]
    b_cols = lax.dynamic_slice(b_ada, (0, me * n_ada), (1, n_ada))
    mod_cols = _ada_forward(c_all, w_ada[0], b_cols)
    mod_all = _exchange(mod_cols, True, "comm_ag_mod")
    mod = lax.dynamic_slice(mod_all, (0, me, 0), (NDEV, 1, n_ada)).reshape(1, NDEV * n_ada)
    shift_m, scale_m, gate_m, shift_f, scale_f, gate_f = [mod[:, i * d:(i + 1) * d] for i in range(6)]
    w_in_g = _exchange(w_in[0].astype(BF16), True, "comm_ag_w_in", relayed=True)

    gp = n_groups * p_state
    a_re2, a_im2, ldt2 = a_re[0], a_im[0], log_dt[0].reshape(n_groups, 1)
    b_re2, b_im2 = b_re[0].reshape(gp, SSM_GROUP), b_im[0].reshape(gp, SSM_GROUP)
    lam_r, lam_i, coef_r, coef_i = _whole("s5_lam", _s5_lam, [a_re2, a_im2, ldt2], [(n_groups, p_state)] * 4)
    coef_r2, coef_i2 = coef_r.reshape(gp, 1), coef_i.reshape(gp, 1)
    bb_r, bb_i = _whole("s5_bbar", _s5_bbar, [coef_r2, coef_i2, b_re2, b_im2], [(gp, SSM_GROUP)] * 2)
    s_blk = GROUPS_PER_BLOCK * p_state
    b_blk_r = _to_b_blocks(bb_r.reshape(n_groups, p_state, SSM_GROUP), n_blk, p_state)
    b_blk_i = _to_b_blocks(bb_i.reshape(n_groups, p_state, SSM_GROUP), n_blk, p_state)
    c_blk_r = _to_c_blocks(c_re[0], n_blk, p_state)
    c_blk_i = _to_c_blocks(c_im[0], n_blk, p_state)
    lam_r3, lam_i3 = lam_r.reshape(n_blk, 1, s_blk), lam_i.reshape(n_blk, 1, s_blk)
    d_skip2 = d_skip[0].reshape(1, w_ssm)

    row_d, vec_d = _row_spec(rt, d), _vec_spec(d)
    xm, = _rowwise_fwd("seg_in", lambda *a: _seg_in(*a)[:1], [xs], [row_d], [shift_m, scale_m, g_mix], [vec_d] * 3,
                       [sds((seq, d), BF16)], [row_d], n_rt)
    bn_in = _tile(ns_in, 512)
    per = ns_in // bn_in
    bm, bk = _tile(seq, BM), _tile(d, BK)
    proj, w_glu_g, w_out_g = _mm(
        "mm_in", xm, w_in_g, NN, (seq // bm, NDEV * per, d // bk),
        pl.BlockSpec((bm, bk), lambda i, j, k: (i, k)),
        pl.BlockSpec((None, bk, bn_in), lambda i, j, k: (j // per, k, j % per)),
        pl.BlockSpec((bm, bn_in), lambda i, j, k: (i, j)), (seq, NDEV * ns_in), F32, (bm, bn_in),
        riders=[_Rider(w_glu[0].astype(BF16), True, relayed=True), _Rider(w_out[0].astype(BF16), True, relayed=True)])
    w_glu_g, w_out_g = w_glu_g.reshape(w_ssm, w_ssm), w_out_g.reshape(d_mix, d)
    q_col, k_col, v_col = w_ssm // w_attn, w_ssm // w_attn + 1, (w_ssm + 2 * w_attn) // LANES
    qg_t, kg_t = jnp.tile(q_gain, (1, n_heads)), jnp.tile(k_gain, (1, n_heads))
    row_a, vec_a = _row_spec(rt, w_attn), _vec_spec(w_attn)
    qk_rows, qk_specs = [proj, proj], [_row_spec(rt, w_attn, q_col), _row_spec(rt, w_attn, k_col)]
    qh, kh = _rowwise_fwd("seg_qk", _seg_qk, qk_rows, qk_specs, [qg_t, kg_t], [vec_a] * 2,
                          [sds((seq, w_attn), F32)] * 2, [row_a] * 2, n_rt)
    t_chunk = _tile(seq, 512)
    (ypre, x_re, x_im), (w_gu_land,) = _s5_forward(
        proj, b_blk_r, b_blk_i, c_blk_r, c_blk_i, lam_r3, lam_i3, d_skip2, n_blk, t_chunk,
        riders=[_Rider(w_up[0].astype(BF16), True, slot=1, n_slots=2, relayed=True)])
    tq, tk = _tile(seq, TQ), _tile(seq, TK)
    y_attn, (w_gu_land,) = _attention_forward(qh, kh, proj, v_col, n_pair, tq, tk,
                                              riders=[_Rider(w_gate[0].astype(BF16), True, land=w_gu_land, slot=0,
                                                             relayed=True)])
    w_gu_g = w_gu_land.reshape(2 * NDEV, d, ns_ff)
    row_s, vec_s = _row_spec(rt, w_ssm), _vec_spec(w_ssm)
    y1, = _rowwise_fwd("seg_gelu", _seg_gelu, [ypre], [row_s], [], [], [sds((seq, w_ssm), F32)], [row_s], n_rt)
    z = _mm_plain("mm_glu", y1, w_glu_g, NN, F32)
    row_m = _row_spec(rt, d_mix)
    mixed, = _rowwise_fwd("seg_mix", _seg_mix, [y1, z, y_attn], [row_s, row_s, row_a], [g_ssm_out, g_attn_out], [vec_s, vec_a],
                          [sds((seq, d_mix), BF16)], [row_m], n_rt)
    o = _mm_plain("mm_out", mixed, w_out_g, NN, F32)
    h1, xf = _rowwise_fwd("seg_mid", _seg_mid, [xs, o], [row_d] * 2, [gate_m, g_ffn, scale_f, shift_f], [vec_d] * 4,
                          [sds((seq, d), F32), sds((seq, d), BF16)], [row_d] * 2, n_rt)
    gu, w_down_g = _mm(
        "mm_gu", xf, w_gu_g, NN, (seq // bm, 2 * NDEV, d // bk),
        pl.BlockSpec((bm, bk), lambda i, j, k: (i, k)), pl.BlockSpec((None, bk, ns_ff), lambda i, j, k: (j, k, 0)),
        pl.BlockSpec((None, bm, ns_ff), lambda i, j, k: (j, i, 0)), (2 * NDEV, seq, ns_ff), BF16, (bm, ns_ff),
        riders=[_Rider(w_down[0].astype(BF16), True, relayed=True)])
    gu4 = gu.reshape(2, NDEV, seq, ns_ff)
    ft = _tile(seq, 512)
    pair_spec = pl.BlockSpec((2, None, ft, ns_ff), lambda s, i: (0, s, i, 0))
    one_spec = pl.BlockSpec((None, ft, ns_ff), lambda s, i: (s, i, 0))

    def act_body(gu_ref, a_ref):
        a_ref[...] = _seg_act(gu_ref[0].astype(F32), gu_ref[1].astype(F32))[0].astype(a_ref.dtype)

    act = pl.pallas_call(act_body, name="seg_act", grid=(NDEV, seq // ft), in_specs=[pair_spec], out_specs=one_spec,
                         out_shape=sds((NDEV, seq, ns_ff), BF16), compiler_params=_params(2))(gu4)
    bn_d = _tile(d, BN)
    bm_h = _tile(seq, BM // 2)
    shard_pieces = lambda n: (lambda a_ref, b_ref: [(a_ref[g], b_ref[g]) for g in range(n)])
    ffn = _mm("mm_down", act, w_down_g, NN, (seq // bm_h, d // bn_d, 1),
              pl.BlockSpec((NDEV, bm_h, ns_ff), lambda i, j, k: (0, i, 0)), pl.BlockSpec((NDEV, ns_ff, bn_d), lambda i, j, k: (0, 0, j)),
              pl.BlockSpec((bm_h, bn_d), lambda i, j, k: (i, j)), (seq, d), F32, (bm_h, bn_d), pieces=shard_pieces(NDEV))
    dy, dffn, d_gate_f, loss_part = _loss_head(h1, ffn, tgt, gate_f, rt)
    loss = lax.psum(loss_part[0, 0], MESH_AXES)

    bl = _tile(seq, BK)
    gw_down = _mm("mm_dw_down", act, dffn, TN, (NDEV, d // bn_d, seq // bl),
                  pl.BlockSpec((None, bl, ns_ff), lambda i, j, k: (i, k, 0)), pl.BlockSpec((bl, bn_d), lambda i, j, k: (k, j)),
                  pl.BlockSpec((None, ns_ff, bn_d), lambda i, j, k: (i, 0, j)), (NDEV, ns_ff, d), BF16, (ns_ff, bn_d))
    dact = _mm(
        "mm_dact", dffn, w_down_g, NT, (seq // bm, NDEV, d // bk),
        pl.BlockSpec((bm, bk), lambda i, j, k: (i, k)), pl.BlockSpec((None, ns_ff, bk), lambda i, j, k: (j, 0, k)),
        pl.BlockSpec((None, bm, ns_ff), lambda i, j, k: (j, i, 0)), (NDEV, seq, ns_ff), BF16, (bm, ns_ff))

    def dact_body(gu_ref, da_ref, dgu_ref):
        _, vjp = jax.vjp(_seg_act, gu_ref[0].astype(F32), gu_ref[1].astype(F32))
        dg, du_ = vjp((da_ref[...].astype(F32),))
        dgu_ref[0] = dg.astype(dgu_ref.dtype)
        dgu_ref[1] = du_.astype(dgu_ref.dtype)

    dgu4 = pl.pallas_call(dact_body, name="seg_act_bwd", grid=(NDEV, seq // ft), in_specs=[pair_spec, one_spec],
                          out_specs=pair_spec, out_shape=sds((2, NDEV, seq, ns_ff), BF16), compiler_params=_params(2))(gu4, dact)
    dgu = dgu4.reshape(2 * NDEV, seq, ns_ff)
    bmd = _tile(d, BM)

    def dw_half(name, which, riders):
        return _mm(name, xf, dgu, TN, (d // bmd, NDEV, seq // bl), pl.BlockSpec((bl, bmd), lambda i, j, k: (k, i)),
                   pl.BlockSpec((None, bl, ns_ff), lambda i, j, k: (which * NDEV + j, k, 0)),
                   pl.BlockSpec((None, bmd, ns_ff), lambda i, j, k: (j, i, 0)), (NDEV, d, ns_ff), BF16, (bmd, ns_ff), riders=riders)

    gw_gate = dw_half("mm_dw_gate", 0, ())
    gw_up = dw_half("mm_dw_up", 1, ())
    dxf, got_down = _mm(
        "mm_dxf", dgu, w_gu_g, NT, (seq // bm, d // bn_d, 4),
        pl.BlockSpec((4, bm, ns_ff), lambda i, j, k: (k, i, 0)), pl.BlockSpec((4, bn_d, ns_ff), lambda i, j, k: (k, j, 0)),
        pl.BlockSpec((bm, bn_d), lambda i, j, k: (i, j)), (seq, d), F32, (bm, bn_d), riders=[_Rider(gw_down, False)],
        pieces=shard_pieces(4))
    (do, dx_a, d_gate_m, d_g_ffn, d_scale_f, d_shift_f) = _rowwise_bwd(
        "seg_mid_bwd", _seg_mid, [xs, o], [row_d] * 2, [gate_m, g_ffn, scale_f, shift_f], [vec_d] * 4,
        [dy, dxf], [row_d] * 2, [[0], [1]], [1, 0], [sds((seq, d), BF16), sds((seq, d), F32)], [row_d] * 2,
        [0, 1, 2, 3], [sds((1, d), F32)] * 4, [vec_d] * 4, n_rt)

    dmixed = _mm_plain("mm_dmixed", do, w_out_g, NT, F32)
    gw_out = _mm_plain("mm_dw_out", mixed, do, TN, BF16)
    (dz, dy1_a, dy_attn, d_g_ssm, d_g_attn) = _rowwise_bwd(
        "seg_mix_bwd", _seg_mix, [y1, z, y_attn], [row_s, row_s, row_a], [g_ssm_out, g_attn_out], [vec_s, vec_a],
        [dmixed], [row_m], [[0]], [1, 0, 2], [sds((seq, w_ssm), BF16), sds((seq, w_ssm), F32), sds((seq, w_attn), F32)],
        [row_s, row_s, row_a], [0, 1], [sds((1, w_ssm), F32), sds((1, w_attn), F32)], [vec_s, vec_a], n_rt)
    dy1_b = _mm_plain("mm_dy1", dz, w_glu_g, NT, F32)
    gw_glu = _mm_plain("mm_dw_glu", y1, dz, TN, BF16)
    (dypre,) = _rowwise_bwd("seg_gelu_bwd", _seg_gelu, [ypre], [row_s], [], [], [dy1_a, dy1_b], [row_s] * 2, [[0, 1]],
                            [0], [sds((seq, w_ssm), F32)], [row_s], [], [], [], n_rt)
    (du, db_blk_r, db_blk_i, dc_blk_r, dc_blk_i, dlam_r3, dlam_i3, dd_skip2), (got_gate,) = _s5_backward(
        dypre, proj, x_re, x_im, b_blk_r, b_blk_i, c_blk_r, c_blk_i, lam_r3, lam_i3, d_skip2, n_blk, t_chunk,
        riders=[_Rider(gw_gate, False)])
    (dqh, dkh, dv), (got_up, got_out, got_glu) = _attention_backward(
        qh, kh, proj, v_col, y_attn, dy_attn, n_pair, tq, tk,
        riders=[_Rider(gw_up, False), _Rider(gw_out.reshape(NDEV, w_out.shape[1], d), False),
                _Rider(gw_glu.reshape(NDEV, w_glu.shape[1], w_ssm), False)])
    (dq, dk, dqg_t, dkg_t) = _rowwise_bwd(
        "seg_qk_bwd", _seg_qk, qk_rows, qk_specs, [qg_t, kg_t], [vec_a] * 2, [dqh, dkh], [row_a] * 2, [[0], [1]],
        [0, 1], [sds((seq, w_attn), BF16)] * 2, [row_a] * 2, [0, 1], [sds((1, w_attn), F32)] * 2, [vec_a] * 2, n_rt)

    dbb_r = _from_b_blocks(db_blk_r, n_blk, p_state).reshape(gp, SSM_GROUP)
    dbb_i = _from_b_blocks(db_blk_i, n_blk, p_state).reshape(gp, SSM_GROUP)
    dcoef_r2, dcoef_i2, db_re2, db_im2 = _whole_vjp("s5_bbar_bwd", _s5_bbar, [coef_r2, coef_i2, b_re2, b_im2], [dbb_r, dbb_i],
                                                    [(gp, 1), (gp, 1), (gp, SSM_GROUP), (gp, SSM_GROUP)])
    lam_cts = [dlam_r3.reshape(n_groups, p_state), dlam_i3.reshape(n_groups, p_state),
               dcoef_r2.reshape(n_groups, p_state), dcoef_i2.reshape(n_groups, p_state)]
    da_re2, da_im2, dldt2 = _whole_vjp("s5_lam_bwd", _s5_lam, [a_re2, a_im2, ldt2], lam_cts,
                                       [(n_groups, p_state), (n_groups, p_state), (n_groups, 1)])
    dc_re2, dc_im2 = _from_c_blocks(dc_blk_r, n_blk, p_state), _from_c_blocks(dc_blk_i, n_blk, p_state)

    small_part = {
        "b_ada_b": jnp.concatenate([d_gate_m, d_shift_f, d_scale_f, d_gate_f], axis=-1),
        "a_re": da_re2, "a_im": da_im2, "log_dt": dldt2, "b_re": db_re2, "b_im": db_im2,
        "c_re": dc_re2, "c_im": dc_im2, "d_skip": dd_skip2,
        "q_gain": dqg_t.reshape(n_heads, HEAD_DIM).sum(0), "k_gain": dkg_t.reshape(n_heads, HEAD_DIM).sum(0),
        "g_ssm_out": d_g_ssm, "g_attn_out": d_g_attn, "g_ffn": d_g_ffn,
    }
    dproj = jnp.concatenate([du, dq, dk, dv], axis=-1)
    gw_in, early_parts = _mm(
        "mm_dw_in", xm, dproj, TN, (d // bmd, NDEV * per, seq // bl),
        pl.BlockSpec((bl, bmd), lambda i, j, k: (k, i)), pl.BlockSpec((bl, bn_in), lambda i, j, k: (k, j)),
        pl.BlockSpec((None, bmd, bn_in), lambda i, j, k: (j // per, i, j % per)), (NDEV, d, ns_in), BF16, (bmd, bn_in),
        riders=[_Rider(_pack([small_part[n] for n in SMALL_EARLY]), True, relayed=True)])
    dxm, got_in = _mm(
        "mm_dxm", dproj, w_in_g, NT, (seq // bm, d // bn_d, 1),
        pl.BlockSpec((bm, NDEV * ns_in), lambda i, j, k: (i, 0)), pl.BlockSpec((NDEV, bn_d, ns_in), lambda i, j, k: (0, j, 0)),
        pl.BlockSpec((bm, bn_d), lambda i, j, k: (i, j)), (seq, d), F32, (bm, bn_d),
        riders=[_Rider(gw_in, False, part=(0, d // 2))],
        pieces=lambda a_ref, b_ref: [(a_ref[:, g * ns_in:(g + 1) * ns_in], b_ref[g]) for g in range(NDEV)])
    (grad_x, d_shift_m, d_scale_m, d_g_mix), (got_in,) = _rowwise_bwd(
        "seg_in_bwd", _seg_in, [xs], [row_d], [shift_m, scale_m, g_mix], [vec_d] * 3, [dxm, dx_a], [row_d] * 2, [[0], [1]],
        [0], [sds((seq, d), F32)], [row_d], [0, 1, 2], [sds((1, d), F32)] * 3, [vec_d] * 3, n_rt,
        riders=[_Rider(gw_in, False, land=got_in, part=(d // 2, d // 2))])
    small_part["b_ada_a"] = jnp.concatenate([d_shift_m, d_scale_m], axis=-1)
    small_part["g_mix"] = d_g_mix
    late_parts = _exchange(_pack([small_part[n] for n in SMALL_LATE]), True, "comm_ag_small_late")
    packed_parts = jnp.concatenate([late_parts, early_parts], axis=1)

    big = {}

    def sharded(nm, got, width, tr):
        big[nm] = _adam_sum("adam_" + nm, got, pl.BlockSpec((NDEV, tr, width), lambda i: (0, i, 0)), given[nm][0],
                            given["m_" + nm][0], given["v_" + nm][0], tr)

    sharded("w_down", got_down, d, _tile(w_down.shape[1], 64))
    sharded("w_gate", got_gate, ns_ff, _tile(d, 256))
    sharded("w_up", got_up, ns_ff, _tile(d, 256))
    sharded("w_out", got_out, d, _tile(w_out.shape[1], 128))
    sharded("w_glu", got_glu, w_ssm, _tile(w_glu.shape[1], 128))
    sharded("w_in", got_in, ns_in, _tile(d, 256))

    split = dict(given)
    for pre in ("", "m_", "v_"):
        split[pre + "b_ada_a"], split[pre + "b_ada_b"] = given[pre + "b_ada"][:, :2 * d], given[pre + "b_ada"][:, 2 * d:]
    packs = [jnp.concatenate([_pack([split[pre + n] for n in SMALL_LATE]), _pack([split[pre + n] for n in SMALL_EARLY])])
             for pre in ("", "m_", "v_")]
    rows_p = packed_parts.shape[1]
    tr_p = _tile(rows_p, 64)
    sm = _adam_sum("adam_small", packed_parts, pl.BlockSpec((NDEV, tr_p, PACK_COLS), lambda i: (0, i, 0)), *packs, tr_p)
    rows_late = late_parts.shape[1]
    small_out = []
    for t in sm:
        out = dict(zip(SMALL_LATE, _unpack(t[:rows_late], [split[n] for n in SMALL_LATE])))
        out.update(zip(SMALL_EARLY, _unpack(t[rows_late:], [split[n] for n in SMALL_EARLY])))
        out["b_ada"] = jnp.concatenate([out["b_ada_a"], out["b_ada_b"]], axis=1)
        small_out.append(out)

    rows_a, rows_b = (2 * d) // PACK_COLS, (4 * d) // PACK_COLS
    assert rows_a * PACK_COLS == 2 * d
    dmod_all = jnp.concatenate([late_parts[:, :rows_a].reshape(NDEV, 2 * d), early_parts[:, :rows_b].reshape(NDEV, 4 * d)], axis=1)
    dmod_cols = lax.dynamic_slice(dmod_all, (0, me * n_ada), (NDEV, n_ada))
    big["w_ada"] = _adam_ada(c_all, dmod_cols, w_ada[0], m_w_ada[0], v_w_ada[0])

    order = ("w_ada", "b_ada", "g_mix", "w_in", "a_re", "a_im", "log_dt", "b_re", "b_im", "c_re", "c_im", "d_skip", "w_glu",
             "q_gain", "k_gain", "g_ssm_out", "g_attn_out", "w_out", "g_ffn", "w_gate", "w_up", "w_down")
    outs = [loss, grad_x[None]]
    for kind in range(4):
        for n in order:
            outs.append(big[n][kind][None] if n in big else small_out[kind][n])
    return tuple(outs)
```

```python
import functools
import math

import jax
import jax.numpy as jnp
from jax import lax
from jax.experimental import pallas as pl
from jax.experimental.pallas import tpu as pltpu

F32 = jnp.float32
BF16 = jnp.bfloat16
NDEV = 8
MESH_AXES = ("x", "y", "c")
MESH_ID = pl.DeviceIdType.MESH
EPS = 1e-6
LANES = 128
SUBLANES = 8
HEAD_DIM = 64
SSM_GROUP = 16
ADAM_LR, ADAM_B1, ADAM_B2, ADAM_EPS, ADAM_WD, ADAM_STEP = 0.001, 0.9, 0.999, 1e-08, 0.01, 10

NN = (((1,), (0,)), ((), ()))
NT = (((1,), (1,)), ((), ()))
TN = (((0,), (0,)), ((), ()))


def _dot(a, b, dn=NN):
    return lax.dot_general(a, b, dn, preferred_element_type=F32)


def _tile(dim, pref):
    t = min(dim, pref)
    while dim % t:
        t //= 2
    return t


def _params(n):
    return pltpu.CompilerParams(dimension_semantics=("arbitrary",) * n)


def _me():
    mx, my, mc = lax.axis_index("x"), lax.axis_index("y"), lax.axis_index("c")
    return mx, my, mc


def _peer(mx, my, mc, k):
    px = 1 - mx if (k >> 2) & 1 else mx
    py = 1 - my if (k >> 1) & 1 else my
    pc = 1 - mc if k & 1 else mc
    return (px, py, pc), 4 * px + 2 * py + pc


def _slot(ref, idx, part):
    return ref.at[idx] if part is None else ref.at[idx, pl.ds(*part)]


def _exchange_copies(x_ref, land_ref, send_sems, recv_sems, gather, part=None):
    mx, my, mc = _me()
    me = 4 * mx + 2 * my + mc
    pairs = []
    for k in range(1, NDEV):
        peer, pidx = _peer(mx, my, mc, k)
        src = x_ref if gather else _slot(x_ref, pidx, part)
        mk = lambda dst, src=src, k=k, peer=peer: pltpu.make_async_remote_copy(
            src_ref=src, dst_ref=dst, send_sem=send_sems.at[k - 1], recv_sem=recv_sems.at[k - 1],
            device_id=peer, device_id_type=MESH_ID)
        pairs.append((mk(_slot(land_ref, me, part)), mk(_slot(land_ref, pidx, part))))
    return me, pairs


def _exchange(x, gather, name, relayed=False):
    def body(x_ref, o_ref, send_sems, recv_sems, local_sem):
        if relayed:
            phases = _relayed_gather_phases(x_ref, o_ref, send_sems, recv_sems, local_sem)
        else:
            phases = _direct_phases(x_ref, o_ref, send_sems, recv_sems, local_sem, gather)
        for phase in phases:
            if phase is not None:
                phase()

    return pl.pallas_call(
        body, name=name, out_shape=jax.ShapeDtypeStruct(((NDEV,) + x.shape) if gather else x.shape, x.dtype),
        in_specs=[pl.BlockSpec(memory_space=pl.ANY)], out_specs=pl.BlockSpec(memory_space=pl.ANY),
        scratch_shapes=[pltpu.SemaphoreType.DMA((NDEV - 1,)), pltpu.SemaphoreType.DMA((NDEV - 1,)), pltpu.SemaphoreType.DMA],
    )(x)


def _direct_phases(x_ref, zone, send_sems, recv_sems, local_sem, gather, part=None):
    me, pairs = _exchange_copies(x_ref, zone, send_sems, recv_sems, gather, part)
    local = pltpu.make_async_copy(x_ref if gather else _slot(x_ref, me, part), _slot(zone, me, part), local_sem)

    def start():
        for send, _ in pairs:
            send.start()
        local.start()

    def finish():
        for send, arrival in pairs:
            send.wait_send()
            arrival.wait_recv()
        local.wait()

    return start, None, finish


def _relayed_gather_phases(x_ref, zone, send_sems, recv_sems, local_sem):
    mx, my, mc = _me()
    me, sibling = (mx, my, mc), (mx, my, 1 - mc)
    chips = [(1 - mx, my), (mx, 1 - my), (1 - mx, 1 - my)]
    rows = lambda dev: zone.at[4 * dev[0] + 2 * dev[1] + dev[2]]

    def copy(k, block, to, src=None):
        return pltpu.make_async_remote_copy(src_ref=rows(block) if src is None else src, dst_ref=rows(block),
                                            send_sem=send_sems.at[k], recv_sem=recv_sems.at[k], device_id=to,
                                            device_id_type=MESH_ID)

    local = pltpu.make_async_copy(x_ref, rows(me), local_sem)
    first = [copy(0, me, sibling, x_ref)] + [copy(1 + j, me, (*chip, mc), x_ref) for j, chip in enumerate(chips)]
    passed = [copy(4 + j, (*chip, mc), sibling) for j, chip in enumerate(chips)]
    over_links = [copy(1 + j, (*chip, mc), me) for j, chip in enumerate(chips)]
    from_sibling = [copy(0, sibling, me)] + [copy(4 + j, (*chip, 1 - mc), me) for j, chip in enumerate(chips)]

    def start():
        local.start()
        for cp in first:
            cp.start()

    def relay():
        for arrival, onward in zip(over_links, passed):
            arrival.wait_recv()
            onward.start()

    def finish():
        for arrival in from_sibling:
            arrival.wait_recv()
        for cp in first + passed:
            cp.wait_send()
        local.wait()

    return start, relay, finish


class _Rider:
    def __init__(self, x, gather, land=None, slot=None, n_slots=None, relayed=False, part=None):
        self.x, self.gather, self.land, self.slot, self.relayed, self.part = x, gather, land, slot, relayed, part
        own = ((NDEV,) + x.shape) if gather else x.shape
        self.land_shape = land.shape if land is not None else (own if n_slots is None else (n_slots,) + own)

    def phases(self, x_ref, land_ref, send_sems, recv_sems, local_sem):
        zone = land_ref if self.slot is None else land_ref.at[self.slot]
        if self.relayed:
            return _relayed_gather_phases(x_ref, zone, send_sems, recv_sems, local_sem)
        return _direct_phases(x_ref, zone, send_sems, recv_sems, local_sem, self.gather, self.part)


def _ride(call_name, grid, riders, inner, in_specs, out_specs, out_shape, scratch_shapes, compiler_params, operands):
    n_in, n_out, n_scr = len(in_specs), len(out_specs), len(scratch_shapes)
    any_spec = pl.BlockSpec(memory_space=pl.ANY)
    extra_in, aliases = [], {}
    for r_idx, r in enumerate(riders):
        extra_in.append(r.x)
        if r.land is not None:
            aliases[n_in + len(extra_in)] = n_out + r_idx
            extra_in.append(r.land)
    sems = []
    for _ in riders:
        sems += [pltpu.SemaphoreType.DMA((NDEV - 1,)), pltpu.SemaphoreType.DMA((NDEV - 1,)), pltpu.SemaphoreType.DMA]

    def body(*refs):
        base_in, rest = refs[:n_in], refs[n_in:]
        rider_in, rest = rest[:len(extra_in)], rest[len(extra_in):]
        base_out, rest = rest[:n_out], rest[n_out:]
        lands, rest = rest[:len(riders)], rest[len(riders):]
        base_scr, rider_sems = rest[:n_scr], rest[n_scr:]
        step = 0
        for a, g in enumerate(grid):
            step = step * g + pl.program_id(a)
        n_steps = math.prod(grid)
        sets, pos = [], 0
        for r_idx, r in enumerate(riders):
            x_ref = rider_in[pos]
            pos += 2 if r.land is not None else 1
            sets.append(r.phases(x_ref, lands[r_idx], *rider_sems[3 * r_idx:3 * r_idx + 3]))

        if sets:
            @pl.when(step == 0)
            def _():
                for start, _, _ in sets:
                    start()

        inner(*base_in, *base_out, *base_scr)

        if any(relay is not None for _, relay, _ in sets):
            @pl.when(step == (3 * n_steps) // 5)
            def _():
                for _, relay, _ in sets:
                    if relay is not None:
                        relay()

        if sets:
            @pl.when(step == n_steps - 1)
            def _():
                for _, _, finish in sets:
                    finish()

    outs = pl.pallas_call(
        body, name=call_name, grid=grid, in_specs=list(in_specs) + [any_spec] * len(extra_in),
        out_specs=list(out_specs) + [any_spec] * len(riders),
        out_shape=list(out_shape) + [jax.ShapeDtypeStruct(r.land_shape, r.x.dtype) for r in riders],
        scratch_shapes=list(scratch_shapes) + sems, input_output_aliases=aliases, compiler_params=compiler_params,
    )(*operands, *extra_in)
    return outs[:n_out], outs[n_out:]


def _mm(name, a, b, dn, grid, a_spec, b_spec, o_spec, out_shape, out_dtype, acc_shape, riders=(), pieces=None):
    nk = grid[2]

    def body(a_ref, b_ref, o_ref, *scratch):
        if pieces is None:
            part = _dot(a_ref[...].astype(BF16), b_ref[...].astype(BF16), dn)
        else:
            part = functools.reduce(lambda p, q: p + q, [_dot(a_g.astype(BF16), b_g.astype(BF16), dn)
                                                         for a_g, b_g in pieces(a_ref, b_ref)])
        if nk == 1:
            o_ref[...] = part.astype(o_ref.dtype)
            return
        acc_ref = scratch[0]
        k = pl.program_id(2)

        @pl.when(k == 0)
        def _():
            acc_ref[...] = part

        @pl.when(k > 0)
        def _():
            acc_ref[...] += part

        @pl.when(k == nk - 1)
        def _():
            o_ref[...] = acc_ref[...].astype(o_ref.dtype)

    (out,), lands = _ride(name, grid, riders, body, [a_spec, b_spec], [o_spec], [jax.ShapeDtypeStruct(out_shape, out_dtype)],
                          [] if nk == 1 else [pltpu.VMEM(acc_shape, F32)], _params(3), [a, b])
    return (out, *lands) if riders else out


BM, BN, BK = 1024, 1024, 4096


def _mm_plain(name, a, b, dn, out_dtype):
    if dn == NN:
        (m, kk), n = a.shape, b.shape[1]
    elif dn == NT:
        (m, kk), n = a.shape, b.shape[0]
    else:
        (kk, m), n = a.shape, b.shape[1]
    half = 2 if dn == TN else 1
    bm, bn, bk = _tile(m, BM // half), _tile(n, BN // half), _tile(kk, BK)
    a_spec = pl.BlockSpec((bk, bm), lambda i, j, k: (k, i)) if dn == TN else pl.BlockSpec((bm, bk), lambda i, j, k: (i, k))
    b_spec = pl.BlockSpec((bn, bk), lambda i, j, k: (j, k)) if dn == NT else pl.BlockSpec((bk, bn), lambda i, j, k: (k, j))
    return _mm(name, a, b, dn, (m // bm, n // bn, kk // bk), a_spec, b_spec,
               pl.BlockSpec((bm, bn), lambda i, j, k: (i, j)), (m, n), out_dtype, (bm, bn))


def _row_spec(tile, width, col=0):
    return pl.BlockSpec((tile, width), lambda i: (i, col))


def _vec_spec(width, col=0):
    return pl.BlockSpec((1, width), lambda i: (0, col))


def _rowwise_fwd(name, fn, rows, row_specs, vecs, vec_specs, out_shapes, out_specs, n_tiles, riders=()):
    nr, nv = len(rows), len(vecs)

    def body(*refs):
        ins = [r[...].astype(F32) for r in refs[:nr + nv]]
        outs = fn(*ins)
        for o_ref, o in zip(refs[nr + nv:], outs):
            o_ref[...] = o.astype(o_ref.dtype)

    outs, lands = _ride(name, (n_tiles,), riders, body, list(row_specs) + list(vec_specs), list(out_specs),
                        list(out_shapes), [], _params(1), [*rows, *vecs])
    return (outs, lands) if riders else outs


def _rowwise_bwd(name, fn, rows, row_specs, vecs, vec_specs, cts, ct_specs, ct_groups,
                 drow_idx, drow_shapes, drow_specs, dvec_idx, dvec_shapes, dvec_specs, n_tiles, riders=()):
    nr, nv, nc = len(rows), len(vecs), len(cts)

    def body(*refs):
        ins = [r[...].astype(F32) for r in refs[:nr + nv]]
        ct_vals = [r[...].astype(F32) for r in refs[nr + nv:nr + nv + nc]]
        out_refs = refs[nr + nv + nc:]
        _, vjp = jax.vjp(fn, *ins)
        grads = vjp(tuple(functools.reduce(lambda p, q: p + q, [ct_vals[j] for j in grp]) for grp in ct_groups))
        for o_ref, idx in zip(out_refs[:len(drow_idx)], drow_idx):
            o_ref[...] = grads[idx].astype(o_ref.dtype)
        step = pl.program_id(0)
        for o_ref, idx in zip(out_refs[len(drow_idx):], dvec_idx):
            @pl.when(step == 0)
            def _(o_ref=o_ref):
                o_ref[...] = jnp.zeros_like(o_ref)
            o_ref[...] += grads[nr + idx]

    outs, lands = _ride(name, (n_tiles,), riders, body, list(row_specs) + list(vec_specs) + list(ct_specs),
                        list(drow_specs) + list(dvec_specs), list(drow_shapes) + list(dvec_shapes), [], _params(1),
                        [*rows, *vecs, *cts])
    return (outs, lands) if riders else outs


def _rms(x):
    return x * lax.rsqrt(jnp.mean(x * x, axis=-1, keepdims=True) + EPS)


def _seg_in(x, shift, scale, gain):
    return _rms(x) * gain * (1.0 + scale) + shift, x


def _seg_qk(q, k, qg, kg):
    def norm(t, g, mult):
        blocks = []
        lane = lax.broadcasted_iota(jnp.int32, (1, LANES), 1)
        for p in range(t.shape[1] // LANES):
            tb = t[:, p * LANES:(p + 1) * LANES]
            sq = tb * tb
            lo = jnp.sum(jnp.where(lane < HEAD_DIM, sq, 0.0), axis=-1, keepdims=True)
            hi = jnp.sum(jnp.where(lane < HEAD_DIM, 0.0, sq), axis=-1, keepdims=True)
            ms = jnp.where(lane < HEAD_DIM, lo, hi) * (1.0 / HEAD_DIM)
            blocks.append(tb * lax.rsqrt(ms + EPS) * (g[:, p * LANES:(p + 1) * LANES] * mult))
        return jnp.concatenate(blocks, axis=-1) if len(blocks) > 1 else blocks[0]
    return norm(q, qg, 1.0 / math.sqrt(HEAD_DIM)), norm(k, kg, 1.0)


def _seg_gelu(ypre):
    return (jax.nn.gelu(ypre),)


def _seg_mix(y1, z, yattn, g_ssm, g_attn):
    ys = y1 * jax.nn.sigmoid(z)
    return (jnp.concatenate([_rms(ys) * g_ssm, _rms(yattn) * g_attn], axis=-1),)


def _seg_mid(x, o, gate_m, g_ffn, scale_f, shift_f):
    h1 = x + gate_m * o
    return h1, _rms(h1) * g_ffn * (1.0 + scale_f) + shift_f


def _seg_act(gate, up):
    return (jax.nn.silu(gate) * up,)


def _s5_lam(a_re, a_im, log_dt):
    dt = jnp.exp(log_dt)
    mag = jnp.exp(a_re * dt)
    lr, li = mag * jnp.cos(a_im * dt), mag * jnp.sin(a_im * dt)
    den = a_re * a_re + a_im * a_im
    nr, ni = lr - 1.0, li
    return lr, li, (nr * a_re + ni * a_im) / den, (ni * a_re - nr * a_im) / den


def _s5_bbar(coef_re, coef_im, b_re, b_im):
    return coef_re * b_re - coef_im * b_im, coef_re * b_im + coef_im * b_re


def _whole(name, fn, ins, out_shapes):
    n = len(ins)

    def body(*refs):
        outs = fn(*[r[...] for r in refs[:n]])
        for o_ref, o in zip(refs[n:], outs):
            o_ref[...] = o

    return pl.pallas_call(body, name=name, out_shape=[jax.ShapeDtypeStruct(s, F32) for s in out_shapes])(*ins)


def _whole_vjp(name, fn, ins, cts, out_shapes):
    n, nc = len(ins), len(cts)

    def body(*refs):
        _, vjp = jax.vjp(fn, *[r[...] for r in refs[:n]])
        grads = vjp(tuple(r[...] for r in refs[n:n + nc]))
        for o_ref, g in zip(refs[n + nc:], grads):
            o_ref[...] = g

    return pl.pallas_call(body, name=name, out_shape=[jax.ShapeDtypeStruct(s, F32) for s in out_shapes])(*ins, *cts)


SCAN_SHIFTS = (1, 2, 4)


def _cmul(ar, ai, br, bi):
    return ar * br - ai * bi, ar * bi + ai * br


def _scan_coefs(lr, li, reverse):
    s = lr.shape[1]
    row = lax.broadcasted_iota(jnp.int32, (SUBLANES, s), 0)
    p1 = (lr, li)
    p2 = _cmul(*p1, *p1)
    p4 = _cmul(*p2, *p2)
    p8 = _cmul(*p4, *p4)
    p3, p5, p6 = _cmul(*p1, *p2), _cmul(*p4, *p1), _cmul(*p4, *p2)
    p7 = _cmul(*p6, *p1)
    pows = (p1, p2, p3, p4, p5, p6, p7, p8)
    bc = lambda t: jnp.broadcast_to(t, (SUBLANES, s))
    steps = []
    for sh, pw in zip(SCAN_SHIFTS, (p1, p2, p4)):
        keep = (row + sh <= SUBLANES - 1) if reverse else (row >= sh)
        steps.append((jnp.where(keep, bc(pw[0]), 0.0), jnp.where(keep, bc(pw[1]), 0.0)))
    cr, ci = jnp.zeros((SUBLANES, s), F32), jnp.zeros((SUBLANES, s), F32)
    for r in range(SUBLANES):
        pw = pows[SUBLANES - 1 - r] if reverse else pows[r]
        cr = jnp.where(row == r, bc(pw[0]), cr)
        ci = jnp.where(row == r, bc(pw[1]), ci)
    return steps, (cr, ci)


def _scan_tile(xr, xi, steps, carry_pow, cr, ci, reverse):
    for sh, (ar, ai) in zip(SCAN_SHIFTS, steps):
        rs = SUBLANES - sh if reverse else sh
        sr, si = pltpu.roll(xr, rs, 0), pltpu.roll(xi, rs, 0)
        xr, xi = xr + ar * sr - ai * si, xi + ar * si + ai * sr
    pr, pi = carry_pow
    return xr + pr * cr - pi * ci, xi + pr * ci + pi * cr


def _s5_forward(proj, b_blk_re, b_blk_im, c_blk_re, c_blk_im, lam_re, lam_im, d_skip, n_blk, t_chunk, riders=()):
    seq = proj.shape[0]
    n_chunks = seq // t_chunk
    n_tiles = t_chunk // SUBLANES
    s = b_blk_re.shape[2]

    def body(u_ref, bre_ref, bim_ref, cre_ref, cim_ref, lr_ref, li_ref, d_ref, y_ref, xr_ref, xi_ref, wr, wi, carry):
        t = pl.program_id(1)

        @pl.when(t == 0)
        def _():
            carry[...] = jnp.zeros_like(carry)

        u = u_ref[...]
        ub = u.astype(BF16)
        wr[...] = _dot(ub, bre_ref[...])
        wi[...] = _dot(ub, bim_ref[...])
        steps, cpow = _scan_coefs(lr_ref[...], li_ref[...], False)

        def tile(i, c):
            r0 = pl.multiple_of(i * SUBLANES, SUBLANES)
            xr, xi = _scan_tile(wr[pl.ds(r0, SUBLANES), :], wi[pl.ds(r0, SUBLANES), :], steps, cpow, c[0], c[1], False)
            xr_ref[pl.ds(r0, SUBLANES), :] = xr
            xi_ref[pl.ds(r0, SUBLANES), :] = xi
            last = SUBLANES - 1
            return (jnp.broadcast_to(xr[last:, :], xr.shape), jnp.broadcast_to(xi[last:, :], xi.shape))

        cr, ci = lax.fori_loop(0, n_tiles, tile, (carry[0], carry[1]))
        carry[0] = cr
        carry[1] = ci
        y = _dot(xr_ref[...].astype(BF16), cre_ref[...]) - _dot(xi_ref[...].astype(BF16), cim_ref[...])
        y_ref[...] = y + d_ref[...] * u

    blk = lambda shape: pl.BlockSpec((None,) + shape, lambda j, t: (j, 0, 0))
    return _ride(
        "s5_fwd", (n_blk, n_chunks), riders, body,
        [pl.BlockSpec((t_chunk, LANES), lambda j, t: (t, j)), blk((LANES, s)), blk((LANES, s)),
         blk((s, LANES)), blk((s, LANES)), blk((1, s)), blk((1, s)), pl.BlockSpec((1, LANES), lambda j, t: (0, j))],
        [pl.BlockSpec((t_chunk, LANES), lambda j, t: (t, j)), pl.BlockSpec((t_chunk, s), lambda j, t: (t, j)),
         pl.BlockSpec((t_chunk, s), lambda j, t: (t, j))],
        [jax.ShapeDtypeStruct((seq, n_blk * LANES), F32), jax.ShapeDtypeStruct((seq, n_blk * s), F32),
         jax.ShapeDtypeStruct((seq, n_blk * s), F32)],
        [pltpu.VMEM((t_chunk, s), F32), pltpu.VMEM((t_chunk, s), F32), pltpu.VMEM((2, SUBLANES, s), F32)],
        _params(2), [proj, b_blk_re, b_blk_im, c_blk_re, c_blk_im, lam_re, lam_im, d_skip])


def _s5_backward(dypre, proj, x_re, x_im, b_blk_re, b_blk_im, c_blk_re, c_blk_im, lam_re, lam_im, d_skip, n_blk, t_chunk,
                 riders=()):
    seq = proj.shape[0]
    n_chunks = seq // t_chunk
    n_tiles = t_chunk // SUBLANES
    s = b_blk_re.shape[2]

    def body(dy_ref, u_ref, xr_ref, xi_ref, pr_ref, pi_ref, bre_ref, bim_ref, cre_ref, cim_ref, lr_ref, li_ref, d_ref,
             du_ref, dbre_ref, dbim_ref, dcre_ref, dcim_ref, dlr_ref, dli_ref, dd_ref, gr, gi, carry):
        t = pl.program_id(1)

        @pl.when(t == 0)
        def _():
            carry[...] = jnp.zeros_like(carry)
            for r in (dbre_ref, dbim_ref, dcre_ref, dcim_ref, dlr_ref, dli_ref, dd_ref):
                r[...] = jnp.zeros_like(r)

        dy = dy_ref[...]
        dyb = dy.astype(BF16)
        u = u_ref[...]
        gr[...] = _dot(dyb, cre_ref[...], NT)
        gi[...] = -_dot(dyb, cim_ref[...], NT)
        steps, cpow = _scan_coefs(lr_ref[...], -li_ref[...], True)
        row = lax.broadcasted_iota(jnp.int32, (SUBLANES, s), 0)
        last = SUBLANES - 1
        first_chunk = t == n_chunks - 1

        def tile_at(r0, prev_r, prev_i, c):
            cr, ci, ar, ai = c
            lr_, li_ = _scan_tile(gr[pl.ds(r0, SUBLANES), :], gi[pl.ds(r0, SUBLANES), :], steps, cpow, cr, ci, True)
            gr[pl.ds(r0, SUBLANES), :] = lr_
            gi[pl.ds(r0, SUBLANES), :] = li_
            xr, xi = xr_ref[pl.ds(r0, SUBLANES), :], xi_ref[pl.ds(r0, SUBLANES), :]
            xpr = jnp.where(row == 0, jnp.broadcast_to(prev_r[last:, :], xr.shape), pltpu.roll(xr, 1, 0))
            xpi = jnp.where(row == 0, jnp.broadcast_to(prev_i[last:, :], xi.shape), pltpu.roll(xi, 1, 0))
            ar = ar + lr_ * xpr + li_ * xpi
            ai = ai + li_ * xpr - lr_ * xpi
            return (jnp.broadcast_to(lr_[:1, :], lr_.shape), jnp.broadcast_to(li_[:1, :], li_.shape), ar, ai)

        def tile(ii, c):
            i = n_tiles - 1 - ii
            r0 = pl.multiple_of(i * SUBLANES, SUBLANES)
            rp = pl.multiple_of(r0 - SUBLANES, SUBLANES)
            return tile_at(r0, xr_ref[pl.ds(rp, SUBLANES), :], xi_ref[pl.ds(rp, SUBLANES), :], c)

        zero = jnp.zeros((SUBLANES, s), F32)
        c = lax.fori_loop(0, n_tiles - 1, tile, (carry[0], carry[1], zero, zero))
        keep = jnp.where(first_chunk, 0.0, 1.0)
        c = tile_at(0, pr_ref[...] * keep, pi_ref[...] * keep, c)
        carry[0] = c[0]
        carry[1] = c[1]
        dlr_ref[...] += jnp.sum(c[2], axis=0, keepdims=True)
        dli_ref[...] += jnp.sum(c[3], axis=0, keepdims=True)

        lam_r, lam_i = gr[...].astype(BF16), gi[...].astype(BF16)
        du_ref[...] = (_dot(lam_r, bre_ref[...], NT) + _dot(lam_i, bim_ref[...], NT) + d_ref[...] * dy).astype(du_ref.dtype)
        ub = u.astype(BF16)
        dbre_ref[...] += _dot(ub, lam_r, TN)
        dbim_ref[...] += _dot(ub, lam_i, TN)
        dcre_ref[...] += _dot(xr_ref[...].astype(BF16), dyb, TN)
        dcim_ref[...] -= _dot(xi_ref[...].astype(BF16), dyb, TN)
        dd_ref[...] += jnp.sum(dy * u, axis=0, keepdims=True)

    rev = lambda t: n_chunks - 1 - t
    blk = lambda shape: pl.BlockSpec((None,) + shape, lambda j, t: (j, 0, 0))
    tpc = t_chunk // SUBLANES
    prev_spec = pl.BlockSpec((SUBLANES, s), lambda j, t: (jnp.maximum(rev(t) * tpc - 1, 0), j))
    chunk = lambda w: pl.BlockSpec((t_chunk, w), lambda j, t: (rev(t), j))
    return _ride(
        "s5_bwd", (n_blk, n_chunks), riders, body,
        [chunk(LANES), chunk(LANES), chunk(s), chunk(s), prev_spec, prev_spec, blk((LANES, s)), blk((LANES, s)),
         blk((s, LANES)), blk((s, LANES)), blk((1, s)), blk((1, s)), pl.BlockSpec((1, LANES), lambda j, t: (0, j))],
        [chunk(LANES), blk((LANES, s)), blk((LANES, s)), blk((s, LANES)), blk((s, LANES)), blk((1, s)), blk((1, s)),
         pl.BlockSpec((1, LANES), lambda j, t: (0, j))],
        [jax.ShapeDtypeStruct((seq, n_blk * LANES), BF16),
         jax.ShapeDtypeStruct((n_blk, LANES, s), F32), jax.ShapeDtypeStruct((n_blk, LANES, s), F32),
         jax.ShapeDtypeStruct((n_blk, s, LANES), F32), jax.ShapeDtypeStruct((n_blk, s, LANES), F32),
         jax.ShapeDtypeStruct((n_blk, 1, s), F32), jax.ShapeDtypeStruct((n_blk, 1, s), F32),
         jax.ShapeDtypeStruct((1, n_blk * LANES), F32)],
        [pltpu.VMEM((t_chunk, s), F32), pltpu.VMEM((t_chunk, s), F32), pltpu.VMEM((2, SUBLANES, s), F32)],
        _params(2), [dypre, proj, x_re, x_im, x_re, x_im, b_blk_re, b_blk_im, c_blk_re, c_blk_im, lam_re, lam_im, d_skip])


TQ, TK = 256, 128


def _split_bf16(x):
    hi = x.astype(BF16)
    return hi, (x - hi.astype(F32)).astype(BF16)


def _sb_weights(z, past, carry, tri):
    ls = jnp.minimum(z, 0.0) - jnp.log(1.0 + jnp.exp(-jnp.abs(z)))
    lk = ls - z
    if past is not None:
        lk = jnp.where(past, lk, 0.0)
    w = jnp.exp(ls + _dot(lk.astype(BF16), tri) + carry)
    if past is not None:
        w = jnp.where(past, w, 0.0)
    return ls, lk, w


LOG_KEEP_DEAD = -104.0


def _walk_key_blocks(i, ratio, prologue, block, epilogue, log_keep):
    n_kb = (i + 1) * ratio
    prologue(n_kb - 1)
    for n in range(ratio):
        block(n_kb - 1 - n, n % 2, True)
    assert ratio % 2 == 0
    n_pairs = (i * ratio) // 2

    def more(state):
        t, alive = state
        return jnp.logical_and(t < n_pairs, alive)

    def pair(state):
        t, _ = state
        j = n_kb - 1 - ratio - 2 * t
        block(j, ratio % 2, False)
        block(j - 1, (ratio + 1) % 2, False)
        return t + 1, log_keep() >= LOG_KEEP_DEAD

    done, _ = lax.while_loop(more, pair, (jnp.int32(0), log_keep() >= LOG_KEEP_DEAD))
    epilogue(n_kb - ratio - 2 * done)


def _attention_forward(qh, kh, proj, v_col, n_pair, tq, tk, riders=()):
    seq = qh.shape[0]
    ratio = tq // tk

    def body(q_ref, k_ref, v_ref, o_ref, q_scr, z_scr, w_scr, acc_scr, c_scr):
        i = pl.program_id(1)
        lane = lax.broadcasted_iota(jnp.int32, (1, LANES), 1)
        q2 = q_ref[...]
        q_scr[0] = jnp.where(lane < HEAD_DIM, q2, 0.0).astype(BF16)
        q_scr[1] = jnp.where(lane < HEAD_DIM, 0.0, q2).astype(BF16)
        tri = (lax.broadcasted_iota(jnp.int32, (tk, tk), 0) > lax.broadcasted_iota(jnp.int32, (tk, tk), 1)).astype(BF16)
        qpos = i * tq + lax.broadcasted_iota(jnp.int32, (tq, tk), 0)
        kidx = lax.broadcasted_iota(jnp.int32, (tq, tk), 1)

        def rows(ref, j):
            j = jnp.clip(j, 0, seq // tk - 1)
            return ref[pl.ds(pl.multiple_of(j * tk, tk), tk), :].astype(BF16)

        def scores(j, slot):
            kb = rows(k_ref, j)
            for h in range(2):
                z_scr[slot, h] = _dot(q_scr[h], kb, NT)

        def finish(j):
            vb = rows(v_ref, j)
            for h in range(2):
                acc_scr[h] += _dot(w_scr[h], vb)

        def prologue(j):
            w_scr[...] = jnp.zeros_like(w_scr)
            acc_scr[...] = jnp.zeros_like(acc_scr)
            c_scr[...] = jnp.zeros_like(c_scr)
            scores(j, 0)

        def block(j, slot, masked):
            scores(j - 1, 1 - slot)
            finish(j + 1)
            past = ((kidx + j * tk) < qpos) if masked else None
            for h in range(2):
                _, lk, w = _sb_weights(z_scr[slot, h], past, c_scr[h], tri)
                w_scr[h] = w.astype(BF16)
                c_scr[h] += jnp.sum(lk, axis=-1, keepdims=True)

        _walk_key_blocks(i, ratio, prologue, block, finish, lambda: jnp.max(c_scr[...]))
        o_ref[...] = jnp.where(lane < HEAD_DIM, acc_scr[0], acc_scr[1])

    (out,), lands = _ride(
        "attn_fwd", (n_pair, seq // tq), riders, body,
        [pl.BlockSpec((tq, LANES), lambda p, i: (i, p)), pl.BlockSpec((seq, LANES), lambda p, i: (0, p)),
         pl.BlockSpec((seq, LANES), lambda p, i: (0, v_col + p))],
        [pl.BlockSpec((tq, LANES), lambda p, i: (i, p))], [jax.ShapeDtypeStruct(qh.shape, F32)],
        [pltpu.VMEM((2, tq, LANES), BF16), pltpu.VMEM((2, 2, tq, tk), F32), pltpu.VMEM((2, tq, tk), BF16),
         pltpu.VMEM((2, tq, LANES), F32), pltpu.VMEM((2, tq, 1), F32)],
        _params(2), [qh, kh, proj])
    return out, lands


def _attention_backward(qh, kh, proj, v_col, y, dy, n_pair, tq, tk, riders=()):
    seq = qh.shape[0]

    ratio = tq // tk

    n_kblk = seq // tk

    def body(q_ref, k_ref, v_ref, y_ref, dy_ref, dq_ref, dk_ref, dv_ref,
             q_scr, do_scr, qt_scr, dot_scr, dkt_scr, dvt_scr, z_scr, dw_scr, w_scr, dz_scr, dq_scr, c_scr, c2_scr, tot_scr):
        i = pl.program_id(1)

        @pl.when(i == 0)
        def _():
            dkt_scr[...] = jnp.zeros_like(dkt_scr)
            dvt_scr[...] = jnp.zeros_like(dvt_scr)

        lane = lax.broadcasted_iota(jnp.int32, (1, LANES), 1)
        sel = (lane < HEAD_DIM, lane >= HEAD_DIM)
        q2, do2 = q_ref[...], dy_ref[...].astype(BF16)
        do2f = do2.astype(F32)
        dot_oy = do2f * y_ref[...]
        for h in range(2):
            qm, dm = jnp.where(sel[h], q2, 0.0), jnp.where(sel[h], do2f, 0.0)
            q_scr[h] = qm.astype(BF16)
            do_scr[h] = dm.astype(BF16)
            qt_scr[h] = qm.T.astype(BF16)
            dot_scr[h] = dm.T.astype(BF16)
            tot_scr[h] = jnp.sum(jnp.where(sel[h], dot_oy, 0.0), axis=-1, keepdims=True)
        r_i, c_i = lax.broadcasted_iota(jnp.int32, (tk, tk), 0), lax.broadcasted_iota(jnp.int32, (tk, tk), 1)
        tri = (r_i > c_i).astype(BF16)
        tri_ge = (r_i >= c_i).astype(BF16)
        qpos = i * tq + lax.broadcasted_iota(jnp.int32, (tq, tk), 0)
        kidx = lax.broadcasted_iota(jnp.int32, (tq, tk), 1)

        def start(j):
            return pl.multiple_of(jnp.clip(j, 0, seq // tk - 1) * tk, tk)

        def scores(j, slot):
            c0 = start(j)
            kb, vb = k_ref[pl.ds(c0, tk), :].astype(BF16), v_ref[pl.ds(c0, tk), :].astype(BF16)
            for h in range(2):
                z_scr[slot, h] = _dot(q_scr[h], kb, NT)
                dw_scr[slot, h] = _dot(do_scr[h], vb, NT)

        def finish(j):
            jc = jnp.clip(j, 0, n_kblk - 1)
            kb = k_ref[pl.ds(pl.multiple_of(jc * tk, tk), tk), :].astype(BF16)
            dkt_add, dvt_add = jnp.zeros((LANES, tk), F32), jnp.zeros((LANES, tk), F32)
            for h in range(2):
                dz = dz_scr[h]
                dq_scr[h] += _dot(dz, kb)
                dkt_add = dkt_add + _dot(qt_scr[h], dz)
                dvt_add = dvt_add + _dot(dot_scr[h], w_scr[h])
            dkt_scr[jc] += dkt_add
            dvt_scr[jc] += dvt_add

        def prologue(j):
            for r in (w_scr, dz_scr, dq_scr, c_scr, c2_scr):
                r[...] = jnp.zeros_like(r)
            scores(j, 0)

        def block(j, slot, masked):
            scores(j - 1, 1 - slot)
            finish(j + 1)
            past = ((kidx + j * tk) < qpos) if masked else None
            for h in range(2):
                ls, lk, w = _sb_weights(z_scr[slot, h], past, c_scr[h], tri)
                wb = w.astype(BF16)
                dlw = dw_scr[slot, h] * wb.astype(F32)
                hi, lo = _split_bf16(dlw)
                dlk = tot_scr[h] - c2_scr[h] - (_dot(hi, tri_ge) + _dot(lo, tri_ge))
                if masked:
                    dlk = jnp.where(past, dlk, 0.0)
                sig = jnp.exp(ls)
                w_scr[h] = wb
                dz_scr[h] = (dlw * (1.0 - sig) - dlk * sig).astype(BF16)
                c_scr[h] += jnp.sum(lk, axis=-1, keepdims=True)
                c2_scr[h] += jnp.sum(dlw, axis=-1, keepdims=True)

        _walk_key_blocks(i, ratio, prologue, block, finish, lambda: jnp.max(c_scr[...]))
        dq_ref[...] = jnp.where(sel[0], dq_scr[0], dq_scr[1])

        @pl.when(i == seq // tq - 1)
        def _():
            for jb in range(n_kblk):
                dk_ref[jb * tk:(jb + 1) * tk, :] = dkt_scr[jb].T
                dv_ref[jb * tk:(jb + 1) * tk, :] = dvt_scr[jb].T.astype(dv_ref.dtype)

    blk = pl.BlockSpec((tq, LANES), lambda p, i: (i, p))
    full = pl.BlockSpec((seq, LANES), lambda p, i: (0, p))
    shape = jax.ShapeDtypeStruct(qh.shape, F32)
    return _ride(
        "attn_bwd", (n_pair, seq // tq), riders, body,
        [blk, full, pl.BlockSpec((seq, LANES), lambda p, i: (0, v_col + p)), blk, blk],
        [blk, full, full], [shape, shape, jax.ShapeDtypeStruct(qh.shape, BF16)],
        [pltpu.VMEM((2, tq, LANES), BF16), pltpu.VMEM((2, tq, LANES), BF16),
         pltpu.VMEM((2, LANES, tq), BF16), pltpu.VMEM((2, LANES, tq), BF16),
         pltpu.VMEM((n_kblk, LANES, tk), F32), pltpu.VMEM((n_kblk, LANES, tk), F32),
         pltpu.VMEM((2, 2, tq, tk), F32), pltpu.VMEM((2, 2, tq, tk), F32),
         pltpu.VMEM((2, tq, tk), BF16), pltpu.VMEM((2, tq, tk), BF16), pltpu.VMEM((2, tq, LANES), F32),
         pltpu.VMEM((2, tq, 1), F32), pltpu.VMEM((2, tq, 1), F32), pltpu.VMEM((2, tq, 1), F32)],
        _params(2), [qh, kh, proj, y, dy])


def _loss_head(h1, ffn, target, gate_f, tile):
    seq, d = h1.shape

    def body(h_ref, f_ref, t_ref, g_ref, dy_ref, df_ref, dg_ref, loss_ref):
        @pl.when(pl.program_id(0) == 0)
        def _():
            dg_ref[...] = jnp.zeros_like(dg_ref)
            loss_ref[...] = jnp.zeros_like(loss_ref)

        f, g = f_ref[...], g_ref[...]
        err = h_ref[...] + g * f - t_ref[...]
        dy = err * (1.0 / d)
        dy_ref[...] = dy
        df_ref[...] = (dy * g).astype(df_ref.dtype)
        dg_ref[...] += jnp.sum(dy * f, axis=0, keepdims=True)
        loss_ref[...] += jnp.sum(jnp.sum(err * err, axis=-1, keepdims=True), axis=0, keepdims=True) * (0.5 / d)

    row = _row_spec(tile, d)
    return pl.pallas_call(
        body, name="loss_head", grid=(seq // tile,), in_specs=[row, row, row, _vec_spec(d)],
        out_specs=[row, row, _vec_spec(d), pl.BlockSpec((1, 1), lambda i: (0, 0))],
        out_shape=[jax.ShapeDtypeStruct((seq, d), F32), jax.ShapeDtypeStruct((seq, d), BF16),
                   jax.ShapeDtypeStruct((1, d), F32), jax.ShapeDtypeStruct((1, 1), F32)],
        compiler_params=_params(1),
    )(h1, ffn, target, gate_f)


def _dot3(a, b, dn):
    ah, al = _split_bf16(a)
    bh, bl = _split_bf16(b)
    return _dot(ah, bh, dn) + (_dot(ah, bl, dn) + _dot(al, bh, dn))


def _ada_forward(c_all, w_shard, b_cols):
    d, n = w_shard.shape
    bk = _tile(d, 512)

    def body(c_ref, w_ref, b_ref, o_ref):
        @pl.when(pl.program_id(0) == 0)
        def _():
            o_ref[...] = jnp.broadcast_to(b_ref[...], o_ref.shape)

        o_ref[...] += _dot3(jax.nn.silu(c_ref[...]), w_ref[...], NN)

    return pl.pallas_call(
        body, name="ada_fwd", grid=(d // bk,),
        in_specs=[pl.BlockSpec((NDEV, bk), lambda k: (0, k)), pl.BlockSpec((bk, n), lambda k: (k, 0)), _vec_spec(n)],
        out_specs=pl.BlockSpec((NDEV, n), lambda k: (0, 0)), out_shape=jax.ShapeDtypeStruct((NDEV, n), F32),
        compiler_params=_params(1),
    )(c_all, w_shard, b_cols)


def _adam(w, g, m, v):
    m = ADAM_B1 * m + (1.0 - ADAM_B1) * g
    v = ADAM_B2 * v + (1.0 - ADAM_B2) * (g * g)
    m_hat = m / (1.0 - ADAM_B1 ** ADAM_STEP)
    v_hat = v / (1.0 - ADAM_B2 ** ADAM_STEP)
    return -ADAM_LR * (m_hat / (jnp.sqrt(v_hat) + ADAM_EPS) + ADAM_WD * w), m, v


def _adam_ada(c_all, dmod_cols, w, m, v):
    d, n = w.shape
    tr = _tile(d, 256)

    def body(c_ref, dm_ref, w_ref, m_ref, v_ref, g_ref, dl_ref, nm_ref, nv_ref):
        g = _dot3(jax.nn.silu(c_ref[...]), dm_ref[...], TN)
        delta, nm, nv = _adam(w_ref[...], g, m_ref[...], v_ref[...])
        g_ref[...] = g
        dl_ref[...] = delta
        nm_ref[...] = nm
        nv_ref[...] = nv

    row = _row_spec(tr, n)
    return pl.pallas_call(
        body, name="adam_ada", grid=(d // tr,),
        in_specs=[pl.BlockSpec((NDEV, tr), lambda i: (0, i)), pl.BlockSpec((NDEV, n), lambda i: (0, 0)), row, row, row],
        out_specs=[row] * 4, out_shape=[jax.ShapeDtypeStruct((d, n), F32)] * 4, compiler_params=_params(1),
    )(c_all, dmod_cols, w, m, v)


def _adam_sum(name, parts, part_spec, w, m, v, tr):
    r, c = w.shape

    def body(p_ref, w_ref, m_ref, v_ref, g_ref, dl_ref, nm_ref, nv_ref):
        g = p_ref[0].astype(F32)
        for k in range(1, NDEV):
            g = g + p_ref[k].astype(F32)
        delta, nm, nv = _adam(w_ref[...], g, m_ref[...], v_ref[...])
        g_ref[...] = g
        dl_ref[...] = delta
        nm_ref[...] = nm
        nv_ref[...] = nv

    row = _row_spec(tr, c)
    return pl.pallas_call(
        body, name=name, grid=(r // tr,), in_specs=[part_spec, row, row, row],
        out_specs=[row] * 4, out_shape=[jax.ShapeDtypeStruct((r, c), F32)] * 4, compiler_params=_params(1),
    )(parts, w, m, v)


GROUPS_PER_BLOCK = LANES // SSM_GROUP


def _to_b_blocks(bb, n_blk, p):
    t = bb.reshape(n_blk, GROUPS_PER_BLOCK, p, SSM_GROUP)
    eye = jnp.eye(GROUPS_PER_BLOCK, dtype=bb.dtype)
    return jnp.einsum("jgph,gk->jghkp", t, eye).reshape(n_blk, LANES, GROUPS_PER_BLOCK * p)


def _from_b_blocks(blk, n_blk, p):
    t = blk.reshape(n_blk, GROUPS_PER_BLOCK, SSM_GROUP, GROUPS_PER_BLOCK, p)
    eye = jnp.eye(GROUPS_PER_BLOCK, dtype=blk.dtype)
    return jnp.einsum("jghkp,gk->jgph", t, eye).reshape(n_blk * GROUPS_PER_BLOCK, p, SSM_GROUP)


def _to_c_blocks(cc, n_blk, p):
    t = cc.reshape(n_blk, GROUPS_PER_BLOCK, SSM_GROUP, p)
    eye = jnp.eye(GROUPS_PER_BLOCK, dtype=cc.dtype)
    return jnp.einsum("jghp,gk->jgpkh", t, eye).reshape(n_blk, GROUPS_PER_BLOCK * p, LANES)


def _from_c_blocks(blk, n_blk, p):
    t = blk.reshape(n_blk, GROUPS_PER_BLOCK, p, GROUPS_PER_BLOCK, SSM_GROUP)
    eye = jnp.eye(GROUPS_PER_BLOCK, dtype=blk.dtype)
    return jnp.einsum("jgpkh,gk->jghp", t, eye).reshape(n_blk * GROUPS_PER_BLOCK, SSM_GROUP, p)


SMALL_LATE = ("b_ada_a", "g_mix")
SMALL_EARLY = ("b_ada_b", "a_re", "a_im", "log_dt", "b_re", "b_im", "c_re", "c_im", "d_skip",
               "q_gain", "k_gain", "g_ssm_out", "g_attn_out", "g_ffn")
PACK_COLS = 1024


def _pack(arrs):
    flat = jnp.concatenate([a.reshape(-1) for a in arrs])
    n = flat.shape[0]
    quantum = SUBLANES * PACK_COLS
    padded = -(-n // quantum) * quantum
    return jnp.pad(flat, (0, padded - n)).reshape(padded // PACK_COLS, PACK_COLS)


def _unpack(packed, like):
    flat, out, off = packed.reshape(-1), [], 0
    for a in like:
        out.append(flat[off:off + a.size].reshape(a.shape))
        off += a.size
    return out


def kernel(x, c, w_ada, b_ada, g_mix, w_in, a_re, a_im, log_dt, b_re, b_im, c_re, c_im, d_skip, w_glu, q_gain, k_gain, g_ssm_out, g_attn_out, w_out, g_ffn, w_gate, w_up, w_down, loss_target, m_w_ada, m_b_ada, m_g_mix, m_w_in, m_a_re, m_a_im, m_log_dt, m_b_re, m_b_im, m_c_re, m_c_im, m_d_skip, m_w_glu, m_q_gain, m_k_gain, m_g_ssm_out, m_g_attn_out, m_w_out, m_g_ffn, m_w_gate, m_w_up, m_w_down, v_w_ada, v_b_ada, v_g_mix, v_w_in, v_a_re, v_a_im, v_log_dt, v_b_re, v_b_im, v_c_re, v_c_im, v_d_skip, v_w_glu, v_q_gain, v_k_gain, v_g_ssm_out, v_g_attn_out, v_w_out, v_g_ffn, v_w_gate, v_w_up, v_w_down):
    given = dict(locals())
    seq, d = x.shape[1], x.shape[2]
    xs, tgt = x[0], loss_target[0]
    n_groups, p_state = a_re.shape[1], a_re.shape[2]
    w_ssm = n_groups * SSM_GROUP
    w_attn = w_in.shape[2] * NDEV - w_ssm
    w_attn //= 3
    n_blk, n_pair = w_ssm // LANES, w_attn // LANES
    n_heads = w_attn // HEAD_DIM
    ns_in, ns_ff = w_in.shape[2], w_gate.shape[2]
    d_mix = w_ssm + w_attn
    mx, my, mc = _me()
    me = 4 * mx + 2 * my + mc
    rt = _tile(seq, 256)
    n_rt = seq // rt
    sds = jax.ShapeDtypeStruct

    c_all = _exchange(c, True, "comm_ag_c").reshape(NDEV, d)
    n_ada = w_ada.shape[2]
    b_cols = lax.dynamic_slice(b_ada, (0, me * n_ada), (1, n_ada))
    mod_cols = _ada_forward(c_all, w_ada[0], b_cols)
    mod_all = _exchange(mod_cols, True, "comm_ag_mod")
    mod = lax.dynamic_slice(mod_all, (0, me, 0), (NDEV, 1, n_ada)).reshape(1, NDEV * n_ada)
    shift_m, scale_m, gate_m, shift_f, scale_f, gate_f = [mod[:, i * d:(i + 1) * d] for i in range(6)]

    gp = n_groups * p_state
    a_re2, a_im2, ldt2 = a_re[0], a_im[0], log_dt[0].reshape(n_groups, 1)
    b_re2, b_im2 = b_re[0].reshape(gp, SSM_GROUP), b_im[0].reshape(gp, SSM_GROUP)
    lam_r, lam_i, coef_r, coef_i = _whole("s5_lam", _s5_lam, [a_re2, a_im2, ldt2], [(n_groups, p_state)] * 4)
    coef_r2, coef_i2 = coef_r.reshape(gp, 1), coef_i.reshape(gp, 1)
    bb_r, bb_i = _whole("s5_bbar", _s5_bbar, [coef_r2, coef_i2, b_re2, b_im2], [(gp, SSM_GROUP)] * 2)
    s_blk = GROUPS_PER_BLOCK * p_state
    b_blk_r = _to_b_blocks(bb_r.reshape(n_groups, p_state, SSM_GROUP), n_blk, p_state).astype(BF16)
    b_blk_i = _to_b_blocks(bb_i.reshape(n_groups, p_state, SSM_GROUP), n_blk, p_state).astype(BF16)
    c_blk_r = _to_c_blocks(c_re[0], n_blk, p_state).astype(BF16)
    c_blk_i = _to_c_blocks(c_im[0], n_blk, p_state).astype(BF16)
    lam_r3, lam_i3 = lam_r.reshape(n_blk, 1, s_blk), lam_i.reshape(n_blk, 1, s_blk)
    d_skip2 = d_skip[0].reshape(1, w_ssm)

    row_d, vec_d = _row_spec(rt, d), _vec_spec(d)
    (xm,), (w_in_g,) = _rowwise_fwd("seg_in", lambda *a: _seg_in(*a)[:1], [xs], [row_d], [shift_m, scale_m, g_mix], [vec_d] * 3,
                                    [sds((seq, d), BF16)], [row_d], n_rt,
                                    riders=[_Rider(w_in[0].astype(BF16), True, relayed=True)])
    bn_in = _tile(ns_in, 512)
    per = ns_in // bn_in
    bm, bk = _tile(seq, BM), _tile(d, BK)
    proj, w_glu_g, w_out_g = _mm(
        "mm_in", xm, w_in_g, NN, (seq // bm, NDEV * per, d // bk),
        pl.BlockSpec((bm, bk), lambda i, j, k: (i, k)),
        pl.BlockSpec((None, bk, bn_in), lambda i, j, k: (j // per, k, j % per)),
        pl.BlockSpec((bm, bn_in), lambda i, j, k: (i, j)), (seq, NDEV * ns_in), F32, (bm, bn_in),
        riders=[_Rider(w_glu[0].astype(BF16), True, relayed=True), _Rider(w_out[0].astype(BF16), True, relayed=True)])
    w_glu_g, w_out_g = w_glu_g.reshape(w_ssm, w_ssm), w_out_g.reshape(d_mix, d)
    q_col, k_col, v_col = w_ssm // w_attn, w_ssm // w_attn + 1, (w_ssm + 2 * w_attn) // LANES
    qg_t, kg_t = jnp.tile(q_gain, (1, n_heads)), jnp.tile(k_gain, (1, n_heads))
    row_a, vec_a = _row_spec(rt, w_attn), _vec_spec(w_attn)
    qk_rows, qk_specs = [proj, proj], [_row_spec(rt, w_attn, q_col), _row_spec(rt, w_attn, k_col)]
    qh, kh = _rowwise_fwd("seg_qk", _seg_qk, qk_rows, qk_specs, [qg_t, kg_t], [vec_a] * 2,
                          [sds((seq, w_attn), F32)] * 2, [row_a] * 2, n_rt)
    t_chunk = _tile(seq, 512)
    (ypre, x_re, x_im), (w_gu_land,) = _s5_forward(
        proj, b_blk_r, b_blk_i, c_blk_r, c_blk_i, lam_r3, lam_i3, d_skip2, n_blk, t_chunk,
        riders=[_Rider(w_up[0].astype(BF16), True, slot=1, n_slots=2, relayed=True)])
    tq, tk = _tile(seq, TQ), _tile(seq, TK)
    y_attn, (w_gu_land,) = _attention_forward(qh, kh, proj, v_col, n_pair, tq, tk,
                                              riders=[_Rider(w_gate[0].astype(BF16), True, land=w_gu_land, slot=0,
                                                             relayed=True)])
    w_gu_g = w_gu_land.reshape(2 * NDEV, d, ns_ff)
    row_s, vec_s = _row_spec(rt, w_ssm), _vec_spec(w_ssm)
    y1, = _rowwise_fwd("seg_gelu", _seg_gelu, [ypre], [row_s], [], [], [sds((seq, w_ssm), F32)], [row_s], n_rt)
    z = _mm_plain("mm_glu", y1, w_glu_g, NN, F32)
    row_m = _row_spec(rt, d_mix)
    mixed, = _rowwise_fwd("seg_mix", _seg_mix, [y1, z, y_attn], [row_s, row_s, row_a], [g_ssm_out, g_attn_out], [vec_s, vec_a],
                          [sds((seq, d_mix), BF16)], [row_m], n_rt)
    o = _mm_plain("mm_out", mixed, w_out_g, NN, F32)
    h1, xf = _rowwise_fwd("seg_mid", _seg_mid, [xs, o], [row_d] * 2, [gate_m, g_ffn, scale_f, shift_f], [vec_d] * 4,
                          [sds((seq, d), F32), sds((seq, d), BF16)], [row_d] * 2, n_rt)
    gu, w_down_g = _mm(
        "mm_gu", xf, w_gu_g, NN, (seq // bm, 2 * NDEV, d // bk),
        pl.BlockSpec((bm, bk), lambda i, j, k: (i, k)), pl.BlockSpec((None, bk, ns_ff), lambda i, j, k: (j, k, 0)),
        pl.BlockSpec((None, bm, ns_ff), lambda i, j, k: (j, i, 0)), (2 * NDEV, seq, ns_ff), BF16, (bm, ns_ff),
        riders=[_Rider(w_down[0].astype(BF16), True, relayed=True)])
    gu4 = gu.reshape(2, NDEV, seq, ns_ff)
    ft = _tile(seq, 512)
    pair_spec = pl.BlockSpec((2, None, ft, ns_ff), lambda s, i: (0, s, i, 0))
    one_spec = pl.BlockSpec((None, ft, ns_ff), lambda s, i: (s, i, 0))

    def act_body(gu_ref, a_ref):
        a_ref[...] = _seg_act(gu_ref[0].astype(F32), gu_ref[1].astype(F32))[0].astype(a_ref.dtype)

    act = pl.pallas_call(act_body, name="seg_act", grid=(NDEV, seq // ft), in_specs=[pair_spec], out_specs=one_spec,
                         out_shape=sds((NDEV, seq, ns_ff), BF16), compiler_params=_params(2))(gu4)
    bn_d = _tile(d, BN)
    bm_h = _tile(seq, BM // 2)
    shard_pieces = lambda n: (lambda a_ref, b_ref: [(a_ref[g], b_ref[g]) for g in range(n)])
    ffn = _mm("mm_down", act, w_down_g, NN, (seq // bm_h, d // bn_d, 1),
              pl.BlockSpec((NDEV, bm_h, ns_ff), lambda i, j, k: (0, i, 0)), pl.BlockSpec((NDEV, ns_ff, bn_d), lambda i, j, k: (0, 0, j)),
              pl.BlockSpec((bm_h, bn_d), lambda i, j, k: (i, j)), (seq, d), F32, (bm_h, bn_d), pieces=shard_pieces(NDEV))
    dy, dffn, d_gate_f, loss_part = _loss_head(h1, ffn, tgt, gate_f, rt)
    loss = lax.psum(loss_part[0, 0], MESH_AXES)

    bl = _tile(seq, BK)
    gw_down = _mm("mm_dw_down", act, dffn, TN, (NDEV, d // bn_d, seq // bl),
                  pl.BlockSpec((None, bl, ns_ff), lambda i, j, k: (i, k, 0)), pl.BlockSpec((bl, bn_d), lambda i, j, k: (k, j)),
                  pl.BlockSpec((None, ns_ff, bn_d), lambda i, j, k: (i, 0, j)), (NDEV, ns_ff, d), BF16, (ns_ff, bn_d))
    dact = _mm(
        "mm_dact", dffn, w_down_g, NT, (seq // bm, NDEV, d // bk),
        pl.BlockSpec((bm, bk), lambda i, j, k: (i, k)), pl.BlockSpec((None, ns_ff, bk), lambda i, j, k: (j, 0, k)),
        pl.BlockSpec((None, bm, ns_ff), lambda i, j, k: (j, i, 0)), (NDEV, seq, ns_ff), BF16, (bm, ns_ff))

    def dact_body(gu_ref, da_ref, dgu_ref):
        _, vjp = jax.vjp(_seg_act, gu_ref[0].astype(F32), gu_ref[1].astype(F32))
        dg, du_ = vjp((da_ref[...].astype(F32),))
        dgu_ref[0] = dg.astype(dgu_ref.dtype)
        dgu_ref[1] = du_.astype(dgu_ref.dtype)

    dgu4 = pl.pallas_call(dact_body, name="seg_act_bwd", grid=(NDEV, seq // ft), in_specs=[pair_spec, one_spec],
                          out_specs=pair_spec, out_shape=sds((2, NDEV, seq, ns_ff), BF16), compiler_params=_params(2))(gu4, dact)
    dgu = dgu4.reshape(2 * NDEV, seq, ns_ff)
    bmd = _tile(d, BM)

    def dw_half(name, which, riders):
        return _mm(name, xf, dgu, TN, (d // bmd, NDEV, seq // bl), pl.BlockSpec((bl, bmd), lambda i, j, k: (k, i)),
                   pl.BlockSpec((None, bl, ns_ff), lambda i, j, k: (which * NDEV + j, k, 0)),
                   pl.BlockSpec((None, bmd, ns_ff), lambda i, j, k: (j, i, 0)), (NDEV, d, ns_ff), BF16, (bmd, ns_ff), riders=riders)

    gw_gate = dw_half("mm_dw_gate", 0, ())
    gw_up = dw_half("mm_dw_up", 1, ())
    dxf, got_down = _mm(
        "mm_dxf", dgu, w_gu_g, NT, (seq // bm, d // bn_d, 4),
        pl.BlockSpec((4, bm, ns_ff), lambda i, j, k: (k, i, 0)), pl.BlockSpec((4, bn_d, ns_ff), lambda i, j, k: (k, j, 0)),
        pl.BlockSpec((bm, bn_d), lambda i, j, k: (i, j)), (seq, d), F32, (bm, bn_d), riders=[_Rider(gw_down, False)],
        pieces=shard_pieces(4))
    (do, dx_a, d_gate_m, d_g_ffn, d_scale_f, d_shift_f) = _rowwise_bwd(
        "seg_mid_bwd", _seg_mid, [xs, o], [row_d] * 2, [gate_m, g_ffn, scale_f, shift_f], [vec_d] * 4,
        [dy, dxf], [row_d] * 2, [[0], [1]], [1, 0], [sds((seq, d), BF16), sds((seq, d), F32)], [row_d] * 2,
        [0, 1, 2, 3], [sds((1, d), F32)] * 4, [vec_d] * 4, n_rt)

    dmixed = _mm_plain("mm_dmixed", do, w_out_g, NT, F32)
    gw_out = _mm_plain("mm_dw_out", mixed, do, TN, BF16)
    (dz, dy1_a, dy_attn, d_g_ssm, d_g_attn) = _rowwise_bwd(
        "seg_mix_bwd", _seg_mix, [y1, z, y_attn], [row_s, row_s, row_a], [g_ssm_out, g_attn_out], [vec_s, vec_a],
        [dmixed], [row_m], [[0]], [1, 0, 2], [sds((seq, w_ssm), BF16), sds((seq, w_ssm), F32), sds((seq, w_attn), F32)],
        [row_s, row_s, row_a], [0, 1], [sds((1, w_ssm), F32), sds((1, w_attn), F32)], [vec_s, vec_a], n_rt)
    dy1_b = _mm_plain("mm_dy1", dz, w_glu_g, NT, F32)
    gw_glu = _mm_plain("mm_dw_glu", y1, dz, TN, BF16)
    (dypre,) = _rowwise_bwd("seg_gelu_bwd", _seg_gelu, [ypre], [row_s], [], [], [dy1_a, dy1_b], [row_s] * 2, [[0, 1]],
                            [0], [sds((seq, w_ssm), F32)], [row_s], [], [], [], n_rt)
    (du, db_blk_r, db_blk_i, dc_blk_r, dc_blk_i, dlam_r3, dlam_i3, dd_skip2), (got_gate,) = _s5_backward(
        dypre, proj, x_re, x_im, b_blk_r, b_blk_i, c_blk_r, c_blk_i, lam_r3, lam_i3, d_skip2, n_blk, t_chunk,
        riders=[_Rider(gw_gate, False)])
    (dqh, dkh, dv), (got_up, got_out, got_glu) = _attention_backward(
        qh, kh, proj, v_col, y_attn, dy_attn, n_pair, tq, tk,
        riders=[_Rider(gw_up, False), _Rider(gw_out.reshape(NDEV, w_out.shape[1], d), False),
                _Rider(gw_glu.reshape(NDEV, w_glu.shape[1], w_ssm), False)])
    (dq, dk, dqg_t, dkg_t) = _rowwise_bwd(
        "seg_qk_bwd", _seg_qk, qk_rows, qk_specs, [qg_t, kg_t], [vec_a] * 2, [dqh, dkh], [row_a] * 2, [[0], [1]],
        [0, 1], [sds((seq, w_attn), BF16)] * 2, [row_a] * 2, [0, 1], [sds((1, w_attn), F32)] * 2, [vec_a] * 2, n_rt)

    dbb_r = _from_b_blocks(db_blk_r, n_blk, p_state).reshape(gp, SSM_GROUP)
    dbb_i = _from_b_blocks(db_blk_i, n_blk, p_state).reshape(gp, SSM_GROUP)
    dcoef_r2, dcoef_i2, db_re2, db_im2 = _whole_vjp("s5_bbar_bwd", _s5_bbar, [coef_r2, coef_i2, b_re2, b_im2], [dbb_r, dbb_i],
                                                    [(gp, 1), (gp, 1), (gp, SSM_GROUP), (gp, SSM_GROUP)])
    lam_cts = [dlam_r3.reshape(n_groups, p_state), dlam_i3.reshape(n_groups, p_state),
               dcoef_r2.reshape(n_groups, p_state), dcoef_i2.reshape(n_groups, p_state)]
    da_re2, da_im2, dldt2 = _whole_vjp("s5_lam_bwd", _s5_lam, [a_re2, a_im2, ldt2], lam_cts,
                                       [(n_groups, p_state), (n_groups, p_state), (n_groups, 1)])
    dc_re2, dc_im2 = _from_c_blocks(dc_blk_r, n_blk, p_state), _from_c_blocks(dc_blk_i, n_blk, p_state)

    small_part = {
        "b_ada_b": jnp.concatenate([d_gate_m, d_shift_f, d_scale_f, d_gate_f], axis=-1),
        "a_re": da_re2, "a_im": da_im2, "log_dt": dldt2, "b_re": db_re2, "b_im": db_im2,
        "c_re": dc_re2, "c_im": dc_im2, "d_skip": dd_skip2,
        "q_gain": dqg_t.reshape(n_heads, HEAD_DIM).sum(0), "k_gain": dkg_t.reshape(n_heads, HEAD_DIM).sum(0),
        "g_ssm_out": d_g_ssm, "g_attn_out": d_g_attn, "g_ffn": d_g_ffn,
    }
    dproj = jnp.concatenate([du, dq, dk, dv], axis=-1)
    gw_in, early_parts = _mm(
        "mm_dw_in", xm, dproj, TN, (d // bmd, NDEV * per, seq // bl),
        pl.BlockSpec((bl, bmd), lambda i, j, k: (k, i)), pl.BlockSpec((bl, bn_in), lambda i, j, k: (k, j)),
        pl.BlockSpec((None, bmd, bn_in), lambda i, j, k: (j // per, i, j % per)), (NDEV, d, ns_in), BF16, (bmd, bn_in),
        riders=[_Rider(_pack([small_part[n] for n in SMALL_EARLY]), True, relayed=True)])
    rows_first = (9 * d) // 16
    dxm, got_in = _mm(
        "mm_dxm", dproj, w_in_g, NT, (seq // bm, d // bn_d, 1),
        pl.BlockSpec((bm, NDEV * ns_in), lambda i, j, k: (i, 0)), pl.BlockSpec((NDEV, bn_d, ns_in), lambda i, j, k: (0, j, 0)),
        pl.BlockSpec((bm, bn_d), lambda i, j, k: (i, j)), (seq, d), F32, (bm, bn_d),
        riders=[_Rider(gw_in, False, part=(0, rows_first))],
        pieces=lambda a_ref, b_ref: [(a_ref[:, g * ns_in:(g + 1) * ns_in], b_ref[g]) for g in range(NDEV)])
    (grad_x, d_shift_m, d_scale_m, d_g_mix), (got_in,) = _rowwise_bwd(
        "seg_in_bwd", _seg_in, [xs], [row_d], [shift_m, scale_m, g_mix], [vec_d] * 3, [dxm, dx_a], [row_d] * 2, [[0], [1]],
        [0], [sds((seq, d), F32)], [row_d], [0, 1, 2], [sds((1, d), F32)] * 3, [vec_d] * 3, n_rt,
        riders=[_Rider(gw_in, False, land=got_in, part=(rows_first, d - rows_first))])
    small_part["b_ada_a"] = jnp.concatenate([d_shift_m, d_scale_m], axis=-1)
    small_part["g_mix"] = d_g_mix
    late_parts = _exchange(_pack([small_part[n] for n in SMALL_LATE]), True, "comm_ag_small_late")
    packed_parts = jnp.concatenate([late_parts, early_parts], axis=1)

    big = {}

    def sharded(nm, got, width, tr):
        big[nm] = _adam_sum("adam_" + nm, got, pl.BlockSpec((NDEV, tr, width), lambda i: (0, i, 0)), given[nm][0],
                            given["m_" + nm][0], given["v_" + nm][0], tr)

    sharded("w_down", got_down, d, _tile(w_down.shape[1], 64))
    sharded("w_gate", got_gate, ns_ff, _tile(d, 256))
    sharded("w_up", got_up, ns_ff, _tile(d, 256))
    sharded("w_out", got_out, d, _tile(w_out.shape[1], 128))
    sharded("w_glu", got_glu, w_ssm, _tile(w_glu.shape[1], 128))
    sharded("w_in", got_in, ns_in, _tile(d, 256))

    split = dict(given)
    for pre in ("", "m_", "v_"):
        split[pre + "b_ada_a"], split[pre + "b_ada_b"] = given[pre + "b_ada"][:, :2 * d], given[pre + "b_ada"][:, 2 * d:]
    packs = [jnp.concatenate([_pack([split[pre + n] for n in SMALL_LATE]), _pack([split[pre + n] for n in SMALL_EARLY])])
             for pre in ("", "m_", "v_")]
    rows_p = packed_parts.shape[1]
    tr_p = _tile(rows_p, 64)
    sm = _adam_sum("adam_small", packed_parts, pl.BlockSpec((NDEV, tr_p, PACK_COLS), lambda i: (0, i, 0)), *packs, tr_p)
    rows_late = late_parts.shape[1]
    small_out = []
    for t in sm:
        out = dict(zip(SMALL_LATE, _unpack(t[:rows_late], [split[n] for n in SMALL_LATE])))
        out.update(zip(SMALL_EARLY, _unpack(t[rows_late:], [split[n] for n in SMALL_EARLY])))
        out["b_ada"] = jnp.concatenate([out["b_ada_a"], out["b_ada_b"]], axis=1)
        small_out.append(out)

    rows_a, rows_b = (2 * d) // PACK_COLS, (4 * d) // PACK_COLS
    assert rows_a * PACK_COLS == 2 * d
    dmod_all = jnp.concatenate([late_parts[:, :rows_a].reshape(NDEV, 2 * d), early_parts[:, :rows_b].reshape(NDEV, 4 * d)], axis=1)
    dmod_cols = lax.dynamic_slice(dmod_all, (0, me * n_ada), (NDEV, n_ada))
    big["w_ada"] = _adam_ada(c_all, dmod_cols, w_ada[0], m_w_ada[0], v_w_ada[0])

    order = ("w_ada", "b_ada", "g_mix", "w_in", "a_re", "a_im", "log_dt", "b_re", "b_im", "c_re", "c_im", "d_skip", "w_glu",
             "q_gain", "k_gain", "g_ssm_out", "g_attn_out", "w_out", "g_ffn", "w_gate", "w_up", "w_down")
    outs = [loss, grad_x[None]]
    for kind in range(4):
        for n in order:
            outs.append(big[n][kind][None] if n in big else small_out[kind][n])
    return tuple(outs)
```

```python
import functools
import math

import jax
import jax.numpy as jnp
from jax import lax
from jax.experimental import pallas as pl
from jax.experimental.pallas import tpu as pltpu

F32 = jnp.float32
BF16 = jnp.bfloat16
NDEV = 8
MESH_AXES = ("x", "y", "c")
MESH_ID = pl.DeviceIdType.MESH
EPS = 1e-6
LANES = 128
SUBLANES = 8
HEAD_DIM = 64
SSM_GROUP = 16
ADAM_LR, ADAM_B1, ADAM_B2, ADAM_EPS, ADAM_WD, ADAM_STEP = 0.001, 0.9, 0.999, 1e-08, 0.01, 10

NN = (((1,), (0,)), ((), ()))
NT = (((1,), (1,)), ((), ()))
TN = (((0,), (0,)), ((), ()))


def _dot(a, b, dn=NN):
    return lax.dot_general(a, b, dn, preferred_element_type=F32)


def _tile(dim, pref):
    t = min(dim, pref)
    while dim % t:
        t //= 2
    return t


def _params(n):
    return pltpu.CompilerParams(dimension_semantics=("arbitrary",) * n)


def _me():
    mx, my, mc = lax.axis_index("x"), lax.axis_index("y"), lax.axis_index("c")
    return mx, my, mc


def _peer(mx, my, mc, k):
    px = 1 - mx if (k >> 2) & 1 else mx
    py = 1 - my if (k >> 1) & 1 else my
    pc = 1 - mc if k & 1 else mc
    return (px, py, pc), 4 * px + 2 * py + pc


def _slot(ref, idx, part):
    return ref.at[idx] if part is None else ref.at[idx, pl.ds(*part)]


def _exchange_copies(x_ref, land_ref, send_sems, recv_sems, gather, part=None):
    mx, my, mc = _me()
    me = 4 * mx + 2 * my + mc
    pairs = []
    for k in range(1, NDEV):
        peer, pidx = _peer(mx, my, mc, k)
        src = x_ref if gather else _slot(x_ref, pidx, part)
        mk = lambda dst, src=src, k=k, peer=peer: pltpu.make_async_remote_copy(
            src_ref=src, dst_ref=dst, send_sem=send_sems.at[k - 1], recv_sem=recv_sems.at[k - 1],
            device_id=peer, device_id_type=MESH_ID)
        pairs.append((mk(_slot(land_ref, me, part)), mk(_slot(land_ref, pidx, part))))
    return me, pairs


def _exchange(x, gather, name, relayed=False):
    def body(x_ref, o_ref, send_sems, recv_sems, local_sem):
        if relayed:
            phases = _relayed_gather_phases(x_ref, o_ref, send_sems, recv_sems, local_sem)
        else:
            phases = _direct_phases(x_ref, o_ref, send_sems, recv_sems, local_sem, gather)
        for phase in phases:
            if phase is not None:
                phase()

    return pl.pallas_call(
        body, name=name, out_shape=jax.ShapeDtypeStruct(((NDEV,) + x.shape) if gather else x.shape, x.dtype),
        in_specs=[pl.BlockSpec(memory_space=pl.ANY)], out_specs=pl.BlockSpec(memory_space=pl.ANY),
        scratch_shapes=[pltpu.SemaphoreType.DMA((NDEV - 1,)), pltpu.SemaphoreType.DMA((NDEV - 1,)), pltpu.SemaphoreType.DMA],
    )(x)


def _direct_phases(x_ref, zone, send_sems, recv_sems, local_sem, gather, part=None):
    me, pairs = _exchange_copies(x_ref, zone, send_sems, recv_sems, gather, part)
    local = pltpu.make_async_copy(x_ref if gather else _slot(x_ref, me, part), _slot(zone, me, part), local_sem)

    def start():
        for send, _ in pairs:
            send.start()
        local.start()

    def finish():
        for send, arrival in pairs:
            send.wait_send()
            arrival.wait_recv()
        local.wait()

    return start, None, finish


def _relayed_gather_phases(x_ref, zone, send_sems, recv_sems, local_sem):
    mx, my, mc = _me()
    me, sibling = (mx, my, mc), (mx, my, 1 - mc)
    chips = [(1 - mx, my), (mx, 1 - my), (1 - mx, 1 - my)]
    rows = lambda dev: zone.at[4 * dev[0] + 2 * dev[1] + dev[2]]

    def copy(k, block, to, src=None):
        return pltpu.make_async_remote_copy(src_ref=rows(block) if src is None else src, dst_ref=rows(block),
                                            send_sem=send_sems.at[k], recv_sem=recv_sems.at[k], device_id=to,
                                            device_id_type=MESH_ID)

    local = pltpu.make_async_copy(x_ref, rows(me), local_sem)
    first = [copy(0, me, sibling, x_ref)] + [copy(1 + j, me, (*chip, mc), x_ref) for j, chip in enumerate(chips)]
    passed = [copy(4 + j, (*chip, mc), sibling) for j, chip in enumerate(chips)]
    over_links = [copy(1 + j, (*chip, mc), me) for j, chip in enumerate(chips)]
    from_sibling = [copy(0, sibling, me)] + [copy(4 + j, (*chip, 1 - mc), me) for j, chip in enumerate(chips)]

    def start():
        local.start()
        for cp in first:
            cp.start()

    def relay():
        for arrival, onward in zip(over_links, passed):
            arrival.wait_recv()
            onward.start()

    def finish():
        for arrival in from_sibling:
            arrival.wait_recv()
        for cp in first + passed:
            cp.wait_send()
        local.wait()

    return start, relay, finish


class _Rider:
    def __init__(self, x, gather, land=None, slot=None, n_slots=None, relayed=False, part=None):
        self.x, self.gather, self.land, self.slot, self.relayed, self.part = x, gather, land, slot, relayed, part
        own = ((NDEV,) + x.shape) if gather else x.shape
        self.land_shape = land.shape if land is not None else (own if n_slots is None else (n_slots,) + own)

    def phases(self, x_ref, land_ref, send_sems, recv_sems, local_sem):
        zone = land_ref if self.slot is None else land_ref.at[self.slot]
        if self.relayed:
            return _relayed_gather_phases(x_ref, zone, send_sems, recv_sems, local_sem)
        return _direct_phases(x_ref, zone, send_sems, recv_sems, local_sem, self.gather, self.part)


def _ride(call_name, grid, riders, inner, in_specs, out_specs, out_shape, scratch_shapes, compiler_params, operands):
    n_in, n_out, n_scr = len(in_specs), len(out_specs), len(scratch_shapes)
    any_spec = pl.BlockSpec(memory_space=pl.ANY)
    extra_in, aliases = [], {}
    for r_idx, r in enumerate(riders):
        extra_in.append(r.x)
        if r.land is not None:
            aliases[n_in + len(extra_in)] = n_out + r_idx
            extra_in.append(r.land)
    sems = []
    for _ in riders:
        sems += [pltpu.SemaphoreType.DMA((NDEV - 1,)), pltpu.SemaphoreType.DMA((NDEV - 1,)), pltpu.SemaphoreType.DMA]

    def body(*refs):
        base_in, rest = refs[:n_in], refs[n_in:]
        rider_in, rest = rest[:len(extra_in)], rest[len(extra_in):]
        base_out, rest = rest[:n_out], rest[n_out:]
        lands, rest = rest[:len(riders)], rest[len(riders):]
        base_scr, rider_sems = rest[:n_scr], rest[n_scr:]
        step = 0
        for a, g in enumerate(grid):
            step = step * g + pl.program_id(a)
        n_steps = math.prod(grid)
        sets, pos = [], 0
        for r_idx, r in enumerate(riders):
            x_ref = rider_in[pos]
            pos += 2 if r.land is not None else 1
            sets.append(r.phases(x_ref, lands[r_idx], *rider_sems[3 * r_idx:3 * r_idx + 3]))

        if sets:
            @pl.when(step == 0)
            def _():
                for start, _, _ in sets:
                    start()

        inner(*base_in, *base_out, *base_scr)

        if any(relay is not None for _, relay, _ in sets):
            @pl.when(step == (3 * n_steps) // 5)
            def _():
                for _, relay, _ in sets:
                    if relay is not None:
                        relay()

        if sets:
            @pl.when(step == n_steps - 1)
            def _():
                for _, _, finish in sets:
                    finish()

    outs = pl.pallas_call(
        body, name=call_name, grid=grid, in_specs=list(in_specs) + [any_spec] * len(extra_in),
        out_specs=list(out_specs) + [any_spec] * len(riders),
        out_shape=list(out_shape) + [jax.ShapeDtypeStruct(r.land_shape, r.x.dtype) for r in riders],
        scratch_shapes=list(scratch_shapes) + sems, input_output_aliases=aliases, compiler_params=compiler_params,
    )(*operands, *extra_in)
    return outs[:n_out], outs[n_out:]


def _mm(name, a, b, dn, grid, a_spec, b_spec, o_spec, out_shape, out_dtype, acc_shape, riders=(), pieces=None):
    nk = grid[2]

    def body(a_ref, b_ref, o_ref, *scratch):
        if pieces is None:
            part = _dot(a_ref[...].astype(BF16), b_ref[...].astype(BF16), dn)
        else:
            part = functools.reduce(lambda p, q: p + q, [_dot(a_g.astype(BF16), b_g.astype(BF16), dn)
                                                         for a_g, b_g in pieces(a_ref, b_ref)])
        if nk == 1:
            o_ref[...] = part.astype(o_ref.dtype)
            return
        acc_ref = scratch[0]
        k = pl.program_id(2)

        @pl.when(k == 0)
        def _():
            acc_ref[...] = part

        @pl.when(k > 0)
        def _():
            acc_ref[...] += part

        @pl.when(k == nk - 1)
        def _():
            o_ref[...] = acc_ref[...].astype(o_ref.dtype)

    (out,), lands = _ride(name, grid, riders, body, [a_spec, b_spec], [o_spec], [jax.ShapeDtypeStruct(out_shape, out_dtype)],
                          [] if nk == 1 else [pltpu.VMEM(acc_shape, F32)], _params(3), [a, b])
    return (out, *lands) if riders else out


BM, BN, BK = 1024, 1024, 4096


def _mm_plain(name, a, b, dn, out_dtype):
    if dn == NN:
        (m, kk), n = a.shape, b.shape[1]
    elif dn == NT:
        (m, kk), n = a.shape, b.shape[0]
    else:
        (kk, m), n = a.shape, b.shape[1]
    half = 2 if dn == TN else 1
    bm, bn, bk = _tile(m, BM // half), _tile(n, BN // half), _tile(kk, BK)
    a_spec = pl.BlockSpec((bk, bm), lambda i, j, k: (k, i)) if dn == TN else pl.BlockSpec((bm, bk), lambda i, j, k: (i, k))
    b_spec = pl.BlockSpec((bn, bk), lambda i, j, k: (j, k)) if dn == NT else pl.BlockSpec((bk, bn), lambda i, j, k: (k, j))
    return _mm(name, a, b, dn, (m // bm, n // bn, kk // bk), a_spec, b_spec,
               pl.BlockSpec((bm, bn), lambda i, j, k: (i, j)), (m, n), out_dtype, (bm, bn))


def _row_spec(tile, width, col=0):
    return pl.BlockSpec((tile, width), lambda i: (i, col))


def _vec_spec(width, col=0):
    return pl.BlockSpec((1, width), lambda i: (0, col))


def _rowwise_fwd(name, fn, rows, row_specs, vecs, vec_specs, out_shapes, out_specs, n_tiles, riders=()):
    nr, nv = len(rows), len(vecs)

    def body(*refs):
        ins = [r[...].astype(F32) for r in refs[:nr + nv]]
        outs = fn(*ins)
        for o_ref, o in zip(refs[nr + nv:], outs):
            o_ref[...] = o.astype(o_ref.dtype)

    outs, lands = _ride(name, (n_tiles,), riders, body, list(row_specs) + list(vec_specs), list(out_specs),
                        list(out_shapes), [], _params(1), [*rows, *vecs])
    return (outs, lands) if riders else outs


def _rowwise_bwd(name, fn, rows, row_specs, vecs, vec_specs, cts, ct_specs, ct_groups,
                 drow_idx, drow_shapes, drow_specs, dvec_idx, dvec_shapes, dvec_specs, n_tiles, riders=()):
    nr, nv, nc = len(rows), len(vecs), len(cts)

    def body(*refs):
        ins = [r[...].astype(F32) for r in refs[:nr + nv]]
        ct_vals = [r[...].astype(F32) for r in refs[nr + nv:nr + nv + nc]]
        out_refs = refs[nr + nv + nc:]
        _, vjp = jax.vjp(fn, *ins)
        grads = vjp(tuple(functools.reduce(lambda p, q: p + q, [ct_vals[j] for j in grp]) for grp in ct_groups))
        for o_ref, idx in zip(out_refs[:len(drow_idx)], drow_idx):
            o_ref[...] = grads[idx].astype(o_ref.dtype)
        step = pl.program_id(0)
        for o_ref, idx in zip(out_refs[len(drow_idx):], dvec_idx):
            @pl.when(step == 0)
            def _(o_ref=o_ref):
                o_ref[...] = jnp.zeros_like(o_ref)
            o_ref[...] += grads[nr + idx]

    outs, lands = _ride(name, (n_tiles,), riders, body, list(row_specs) + list(vec_specs) + list(ct_specs),
                        list(drow_specs) + list(dvec_specs), list(drow_shapes) + list(dvec_shapes), [], _params(1),
                        [*rows, *vecs, *cts])
    return (outs, lands) if riders else outs


def _rms(x):
    return x * lax.rsqrt(jnp.mean(x * x, axis=-1, keepdims=True) + EPS)


def _seg_in(x, shift, scale, gain):
    return _rms(x) * gain * (1.0 + scale) + shift, x


def _seg_qk(q, k, qg, kg):
    def norm(t, g, mult):
        blocks = []
        lane = lax.broadcasted_iota(jnp.int32, (1, LANES), 1)
        for p in range(t.shape[1] // LANES):
            tb = t[:, p * LANES:(p + 1) * LANES]
            sq = tb * tb
            lo = jnp.sum(jnp.where(lane < HEAD_DIM, sq, 0.0), axis=-1, keepdims=True)
            hi = jnp.sum(jnp.where(lane < HEAD_DIM, 0.0, sq), axis=-1, keepdims=True)
            ms = jnp.where(lane < HEAD_DIM, lo, hi) * (1.0 / HEAD_DIM)
            blocks.append(tb * lax.rsqrt(ms + EPS) * (g[:, p * LANES:(p + 1) * LANES] * mult))
        return jnp.concatenate(blocks, axis=-1) if len(blocks) > 1 else blocks[0]
    return norm(q, qg, 1.0 / math.sqrt(HEAD_DIM)), norm(k, kg, 1.0)


def _seg_gelu(ypre):
    return (jax.nn.gelu(ypre),)


def _seg_mix(y1, z, yattn, g_ssm, g_attn):
    ys = y1 * jax.nn.sigmoid(z)
    return (jnp.concatenate([_rms(ys) * g_ssm, _rms(yattn) * g_attn], axis=-1),)


def _seg_mid(x, o, gate_m, g_ffn, scale_f, shift_f):
    h1 = x + gate_m * o
    return h1, _rms(h1) * g_ffn * (1.0 + scale_f) + shift_f


def _seg_act(gate, up):
    return (jax.nn.silu(gate) * up,)


def _s5_lam(a_re, a_im, log_dt):
    dt = jnp.exp(log_dt)
    mag = jnp.exp(a_re * dt)
    lr, li = mag * jnp.cos(a_im * dt), mag * jnp.sin(a_im * dt)
    den = a_re * a_re + a_im * a_im
    nr, ni = lr - 1.0, li
    return lr, li, (nr * a_re + ni * a_im) / den, (ni * a_re - nr * a_im) / den


def _s5_bbar(coef_re, coef_im, b_re, b_im):
    return coef_re * b_re - coef_im * b_im, coef_re * b_im + coef_im * b_re


def _whole(name, fn, ins, out_shapes):
    n = len(ins)

    def body(*refs):
        outs = fn(*[r[...] for r in refs[:n]])
        for o_ref, o in zip(refs[n:], outs):
            o_ref[...] = o

    return pl.pallas_call(body, name=name, out_shape=[jax.ShapeDtypeStruct(s, F32) for s in out_shapes])(*ins)


def _whole_vjp(name, fn, ins, cts, out_shapes):
    n, nc = len(ins), len(cts)

    def body(*refs):
        _, vjp = jax.vjp(fn, *[r[...] for r in refs[:n]])
        grads = vjp(tuple(r[...] for r in refs[n:n + nc]))
        for o_ref, g in zip(refs[n + nc:], grads):
            o_ref[...] = g

    return pl.pallas_call(body, name=name, out_shape=[jax.ShapeDtypeStruct(s, F32) for s in out_shapes])(*ins, *cts)


SCAN_SHIFTS = (1, 2, 4)


def _cmul(ar, ai, br, bi):
    return ar * br - ai * bi, ar * bi + ai * br


def _scan_coefs(lr, li, reverse):
    s = lr.shape[1]
    row = lax.broadcasted_iota(jnp.int32, (SUBLANES, s), 0)
    p1 = (lr, li)
    p2 = _cmul(*p1, *p1)
    p4 = _cmul(*p2, *p2)
    p8 = _cmul(*p4, *p4)
    p3, p5, p6 = _cmul(*p1, *p2), _cmul(*p4, *p1), _cmul(*p4, *p2)
    p7 = _cmul(*p6, *p1)
    pows = (p1, p2, p3, p4, p5, p6, p7, p8)
    bc = lambda t: jnp.broadcast_to(t, (SUBLANES, s))
    steps = []
    for sh, pw in zip(SCAN_SHIFTS, (p1, p2, p4)):
        keep = (row + sh <= SUBLANES - 1) if reverse else (row >= sh)
        steps.append((jnp.where(keep, bc(pw[0]), 0.0), jnp.where(keep, bc(pw[1]), 0.0)))
    cr, ci = jnp.zeros((SUBLANES, s), F32), jnp.zeros((SUBLANES, s), F32)
    for r in range(SUBLANES):
        pw = pows[SUBLANES - 1 - r] if reverse else pows[r]
        cr = jnp.where(row == r, bc(pw[0]), cr)
        ci = jnp.where(row == r, bc(pw[1]), ci)
    return steps, (cr, ci)


def _scan_tile(xr, xi, steps, carry_pow, cr, ci, reverse):
    for sh, (ar, ai) in zip(SCAN_SHIFTS, steps):
        rs = SUBLANES - sh if reverse else sh
        sr, si = pltpu.roll(xr, rs, 0), pltpu.roll(xi, rs, 0)
        xr, xi = xr + ar * sr - ai * si, xi + ar * si + ai * sr
    pr, pi = carry_pow
    return xr + pr * cr - pi * ci, xi + pr * ci + pi * cr


def _s5_forward(proj, b_blk_re, b_blk_im, c_blk_re, c_blk_im, lam_re, lam_im, d_skip, n_blk, t_chunk, riders=()):
    seq = proj.shape[0]
    n_chunks = seq // t_chunk
    n_tiles = t_chunk // SUBLANES
    s = b_blk_re.shape[2]

    def body(u_ref, bre_ref, bim_ref, cre_ref, cim_ref, lr_ref, li_ref, d_ref, y_ref, xr_ref, xi_ref, wr, wi, carry):
        t = pl.program_id(1)

        @pl.when(t == 0)
        def _():
            carry[...] = jnp.zeros_like(carry)

        u = u_ref[...]
        ub = u.astype(BF16)
        wr[...] = _dot(ub, bre_ref[...])
        wi[...] = _dot(ub, bim_ref[...])
        steps, cpow = _scan_coefs(lr_ref[...], li_ref[...], False)

        def tile(i, c):
            r0 = pl.multiple_of(i * SUBLANES, SUBLANES)
            xr, xi = _scan_tile(wr[pl.ds(r0, SUBLANES), :], wi[pl.ds(r0, SUBLANES), :], steps, cpow, c[0], c[1], False)
            xr_ref[pl.ds(r0, SUBLANES), :] = xr
            xi_ref[pl.ds(r0, SUBLANES), :] = xi
            last = SUBLANES - 1
            return (jnp.broadcast_to(xr[last:, :], xr.shape), jnp.broadcast_to(xi[last:, :], xi.shape))

        cr, ci = lax.fori_loop(0, n_tiles, tile, (carry[0], carry[1]))
        carry[0] = cr
        carry[1] = ci
        y = _dot(xr_ref[...].astype(BF16), cre_ref[...]) - _dot(xi_ref[...].astype(BF16), cim_ref[...])
        y_ref[...] = y + d_ref[...] * u

    blk = lambda shape: pl.BlockSpec((None,) + shape, lambda j, t: (j, 0, 0))
    return _ride(
        "s5_fwd", (n_blk, n_chunks), riders, body,
        [pl.BlockSpec((t_chunk, LANES), lambda j, t: (t, j)), blk((LANES, s)), blk((LANES, s)),
         blk((s, LANES)), blk((s, LANES)), blk((1, s)), blk((1, s)), pl.BlockSpec((1, LANES), lambda j, t: (0, j))],
        [pl.BlockSpec((t_chunk, LANES), lambda j, t: (t, j)), pl.BlockSpec((t_chunk, s), lambda j, t: (t, j)),
         pl.BlockSpec((t_chunk, s), lambda j, t: (t, j))],
        [jax.ShapeDtypeStruct((seq, n_blk * LANES), F32), jax.ShapeDtypeStruct((seq, n_blk * s), F32),
         jax.ShapeDtypeStruct((seq, n_blk * s), F32)],
        [pltpu.VMEM((t_chunk, s), F32), pltpu.VMEM((t_chunk, s), F32), pltpu.VMEM((2, SUBLANES, s), F32)],
        _params(2), [proj, b_blk_re, b_blk_im, c_blk_re, c_blk_im, lam_re, lam_im, d_skip])


def _s5_backward(dypre, proj, x_re, x_im, b_blk_re, b_blk_im, c_blk_re, c_blk_im, lam_re, lam_im, d_skip, n_blk, t_chunk,
                 riders=()):
    seq = proj.shape[0]
    n_chunks = seq // t_chunk
    n_tiles = t_chunk // SUBLANES
    s = b_blk_re.shape[2]

    def body(dy_ref, u_ref, xr_ref, xi_ref, pr_ref, pi_ref, bre_ref, bim_ref, cre_ref, cim_ref, lr_ref, li_ref, d_ref,
             du_ref, dbre_ref, dbim_ref, dcre_ref, dcim_ref, dlr_ref, dli_ref, dd_ref, gr, gi, carry):
        t = pl.program_id(1)

        @pl.when(t == 0)
        def _():
            carry[...] = jnp.zeros_like(carry)
            for r in (dbre_ref, dbim_ref, dcre_ref, dcim_ref, dlr_ref, dli_ref, dd_ref):
                r[...] = jnp.zeros_like(r)

        dy = dy_ref[...]
        dyb = dy.astype(BF16)
        u = u_ref[...]
        gr[...] = _dot(dyb, cre_ref[...], NT)
        gi[...] = -_dot(dyb, cim_ref[...], NT)
        steps, cpow = _scan_coefs(lr_ref[...], -li_ref[...], True)
        row = lax.broadcasted_iota(jnp.int32, (SUBLANES, s), 0)
        last = SUBLANES - 1
        first_chunk = t == n_chunks - 1

        def tile_at(r0, prev_r, prev_i, c):
            cr, ci, ar, ai = c
            lr_, li_ = _scan_tile(gr[pl.ds(r0, SUBLANES), :], gi[pl.ds(r0, SUBLANES), :], steps, cpow, cr, ci, True)
            gr[pl.ds(r0, SUBLANES), :] = lr_
            gi[pl.ds(r0, SUBLANES), :] = li_
            xr, xi = xr_ref[pl.ds(r0, SUBLANES), :], xi_ref[pl.ds(r0, SUBLANES), :]
            xpr = jnp.where(row == 0, jnp.broadcast_to(prev_r[last:, :], xr.shape), pltpu.roll(xr, 1, 0))
            xpi = jnp.where(row == 0, jnp.broadcast_to(prev_i[last:, :], xi.shape), pltpu.roll(xi, 1, 0))
            ar = ar + lr_ * xpr + li_ * xpi
            ai = ai + li_ * xpr - lr_ * xpi
            return (jnp.broadcast_to(lr_[:1, :], lr_.shape), jnp.broadcast_to(li_[:1, :], li_.shape), ar, ai)

        def tile(ii, c):
            i = n_tiles - 1 - ii
            r0 = pl.multiple_of(i * SUBLANES, SUBLANES)
            rp = pl.multiple_of(r0 - SUBLANES, SUBLANES)
            return tile_at(r0, xr_ref[pl.ds(rp, SUBLANES), :], xi_ref[pl.ds(rp, SUBLANES), :], c)

        zero = jnp.zeros((SUBLANES, s), F32)
        c = lax.fori_loop(0, n_tiles - 1, tile, (carry[0], carry[1], zero, zero))
        keep = jnp.where(first_chunk, 0.0, 1.0)
        c = tile_at(0, pr_ref[...] * keep, pi_ref[...] * keep, c)
        carry[0] = c[0]
        carry[1] = c[1]
        dlr_ref[...] += jnp.sum(c[2], axis=0, keepdims=True)
        dli_ref[...] += jnp.sum(c[3], axis=0, keepdims=True)

        lam_r, lam_i = gr[...].astype(BF16), gi[...].astype(BF16)
        du_ref[...] = (_dot(lam_r, bre_ref[...], NT) + _dot(lam_i, bim_ref[...], NT) + d_ref[...] * dy).astype(du_ref.dtype)
        ub = u.astype(BF16)
        dbre_ref[...] += _dot(ub, lam_r, TN)
        dbim_ref[...] += _dot(ub, lam_i, TN)
        dcre_ref[...] += _dot(xr_ref[...].astype(BF16), dyb, TN)
        dcim_ref[...] -= _dot(xi_ref[...].astype(BF16), dyb, TN)
        dd_ref[...] += jnp.sum(dy * u, axis=0, keepdims=True)

    rev = lambda t: n_chunks - 1 - t
    blk = lambda shape: pl.BlockSpec((None,) + shape, lambda j, t: (j, 0, 0))
    tpc = t_chunk // SUBLANES
    prev_spec = pl.BlockSpec((SUBLANES, s), lambda j, t: (jnp.maximum(rev(t) * tpc - 1, 0), j))
    chunk = lambda w: pl.BlockSpec((t_chunk, w), lambda j, t: (rev(t), j))
    return _ride(
        "s5_bwd", (n_blk, n_chunks), riders, body,
        [chunk(LANES), chunk(LANES), chunk(s), chunk(s), prev_spec, prev_spec, blk((LANES, s)), blk((LANES, s)),
         blk((s, LANES)), blk((s, LANES)), blk((1, s)), blk((1, s)), pl.BlockSpec((1, LANES), lambda j, t: (0, j))],
        [chunk(LANES), blk((LANES, s)), blk((LANES, s)), blk((s, LANES)), blk((s, LANES)), blk((1, s)), blk((1, s)),
         pl.BlockSpec((1, LANES), lambda j, t: (0, j))],
        [jax.ShapeDtypeStruct((seq, n_blk * LANES), BF16),
         jax.ShapeDtypeStruct((n_blk, LANES, s), F32), jax.ShapeDtypeStruct((n_blk, LANES, s), F32),
         jax.ShapeDtypeStruct((n_blk, s, LANES), F32), jax.ShapeDtypeStruct((n_blk, s, LANES), F32),
         jax.ShapeDtypeStruct((n_blk, 1, s), F32), jax.ShapeDtypeStruct((n_blk, 1, s), F32),
         jax.ShapeDtypeStruct((1, n_blk * LANES), F32)],
        [pltpu.VMEM((t_chunk, s), F32), pltpu.VMEM((t_chunk, s), F32), pltpu.VMEM((2, SUBLANES, s), F32)],
        _params(2), [dypre, proj, x_re, x_im, x_re, x_im, b_blk_re, b_blk_im, c_blk_re, c_blk_im, lam_re, lam_im, d_skip])


TQ, TK = 256, 128


def _split_bf16(x):
    hi = x.astype(BF16)
    return hi, (x - hi.astype(F32)).astype(BF16)


def _sb_weights(z, past, carry, tri):
    ls = jnp.minimum(z, 0.0) - jnp.log(1.0 + jnp.exp(-jnp.abs(z)))
    lk = ls - z
    if past is not None:
        lk = jnp.where(past, lk, 0.0)
    w = jnp.exp(ls + _dot(lk.astype(BF16), tri) + carry)
    if past is not None:
        w = jnp.where(past, w, 0.0)
    return ls, lk, w


LOG_KEEP_DEAD = -104.0


def _walk_key_blocks(i, ratio, prologue, block, epilogue, log_keep):
    n_kb = (i + 1) * ratio
    prologue(n_kb - 1)
    for n in range(ratio):
        block(n_kb - 1 - n, n % 2, True)
    assert ratio % 2 == 0
    n_pairs = (i * ratio) // 2

    def more(state):
        t, alive = state
        return jnp.logical_and(t < n_pairs, alive)

    def pair(state):
        t, _ = state
        j = n_kb - 1 - ratio - 2 * t
        block(j, ratio % 2, False)
        block(j - 1, (ratio + 1) % 2, False)
        return t + 1, log_keep() >= LOG_KEEP_DEAD

    done, _ = lax.while_loop(more, pair, (jnp.int32(0), log_keep() >= LOG_KEEP_DEAD))
    epilogue(n_kb - ratio - 2 * done)


def _attention_forward(qh, kh, proj, v_col, n_pair, tq, tk, riders=()):
    seq = qh.shape[0]
    ratio = tq // tk

    def body(q_ref, k_ref, v_ref, o_ref, q_scr, z_scr, w_scr, acc_scr, c_scr):
        i = pl.program_id(1)
        lane = lax.broadcasted_iota(jnp.int32, (1, LANES), 1)
        q2 = q_ref[...]
        q_scr[0] = jnp.where(lane < HEAD_DIM, q2, 0.0).astype(BF16)
        q_scr[1] = jnp.where(lane < HEAD_DIM, 0.0, q2).astype(BF16)
        tri = (lax.broadcasted_iota(jnp.int32, (tk, tk), 0) > lax.broadcasted_iota(jnp.int32, (tk, tk), 1)).astype(BF16)
        qpos = i * tq + lax.broadcasted_iota(jnp.int32, (tq, tk), 0)
        kidx = lax.broadcasted_iota(jnp.int32, (tq, tk), 1)

        def rows(ref, j):
            j = jnp.clip(j, 0, seq // tk - 1)
            return ref[pl.ds(pl.multiple_of(j * tk, tk), tk), :].astype(BF16)

        def scores(j, slot):
            kb = rows(k_ref, j)
            for h in range(2):
                z_scr[slot, h] = _dot(q_scr[h], kb, NT)

        def finish(j):
            vb = rows(v_ref, j)
            for h in range(2):
                acc_scr[h] += _dot(w_scr[h], vb)

        def prologue(j):
            w_scr[...] = jnp.zeros_like(w_scr)
            acc_scr[...] = jnp.zeros_like(acc_scr)
            c_scr[...] = jnp.zeros_like(c_scr)
            scores(j, 0)

        def block(j, slot, masked):
            scores(j - 1, 1 - slot)
            finish(j + 1)
            past = ((kidx + j * tk) < qpos) if masked else None
            for h in range(2):
                _, lk, w = _sb_weights(z_scr[slot, h], past, c_scr[h], tri)
                w_scr[h] = w.astype(BF16)
                c_scr[h] += jnp.sum(lk, axis=-1, keepdims=True)

        _walk_key_blocks(i, ratio, prologue, block, finish, lambda: jnp.max(c_scr[...]))
        o_ref[...] = jnp.where(lane < HEAD_DIM, acc_scr[0], acc_scr[1])

    (out,), lands = _ride(
        "attn_fwd", (n_pair, seq // tq), riders, body,
        [pl.BlockSpec((tq, LANES), lambda p, i: (i, p)), pl.BlockSpec((seq, LANES), lambda p, i: (0, p)),
         pl.BlockSpec((seq, LANES), lambda p, i: (0, v_col + p))],
        [pl.BlockSpec((tq, LANES), lambda p, i: (i, p))], [jax.ShapeDtypeStruct(qh.shape, F32)],
        [pltpu.VMEM((2, tq, LANES), BF16), pltpu.VMEM((2, 2, tq, tk), F32), pltpu.VMEM((2, tq, tk), BF16),
         pltpu.VMEM((2, tq, LANES), F32), pltpu.VMEM((2, tq, 1), F32)],
        _params(2), [qh, kh, proj])
    return out, lands


def _attention_backward(qh, kh, proj, v_col, y, dy, n_pair, tq, tk, riders=()):
    seq = qh.shape[0]

    ratio = tq // tk

    n_kblk = seq // tk

    def body(q_ref, k_ref, v_ref, y_ref, dy_ref, dq_ref, dk_ref, dv_ref,
             q_scr, do_scr, qt_scr, dot_scr, dkt_scr, dvt_scr, z_scr, dw_scr, w_scr, dz_scr, dq_scr, c_scr, c2_scr, tot_scr):
        i = pl.program_id(1)

        @pl.when(i == 0)
        def _():
            dkt_scr[...] = jnp.zeros_like(dkt_scr)
            dvt_scr[...] = jnp.zeros_like(dvt_scr)

        lane = lax.broadcasted_iota(jnp.int32, (1, LANES), 1)
        sel = (lane < HEAD_DIM, lane >= HEAD_DIM)
        q2, do2 = q_ref[...], dy_ref[...].astype(BF16)
        do2f = do2.astype(F32)
        dot_oy = do2f * y_ref[...]
        for h in range(2):
            qm, dm = jnp.where(sel[h], q2, 0.0), jnp.where(sel[h], do2f, 0.0)
            q_scr[h] = qm.astype(BF16)
            do_scr[h] = dm.astype(BF16)
            qt_scr[h] = qm.T.astype(BF16)
            dot_scr[h] = dm.T.astype(BF16)
            tot_scr[h] = jnp.sum(jnp.where(sel[h], dot_oy, 0.0), axis=-1, keepdims=True)
        r_i, c_i = lax.broadcasted_iota(jnp.int32, (tk, tk), 0), lax.broadcasted_iota(jnp.int32, (tk, tk), 1)
        tri = (r_i > c_i).astype(BF16)
        tri_ge = (r_i >= c_i).astype(BF16)
        qpos = i * tq + lax.broadcasted_iota(jnp.int32, (tq, tk), 0)
        kidx = lax.broadcasted_iota(jnp.int32, (tq, tk), 1)

        def start(j):
            return pl.multiple_of(jnp.clip(j, 0, seq // tk - 1) * tk, tk)

        def scores(j, slot):
            c0 = start(j)
            kb, vb = k_ref[pl.ds(c0, tk), :].astype(BF16), v_ref[pl.ds(c0, tk), :].astype(BF16)
            for h in range(2):
                z_scr[slot, h] = _dot(q_scr[h], kb, NT)
                dw_scr[slot, h] = _dot(do_scr[h], vb, NT)

        def finish(j):
            jc = jnp.clip(j, 0, n_kblk - 1)
            kb = k_ref[pl.ds(pl.multiple_of(jc * tk, tk), tk), :].astype(BF16)
            dkt_add, dvt_add = jnp.zeros((LANES, tk), F32), jnp.zeros((LANES, tk), F32)
            for h in range(2):
                dz = dz_scr[h]
                dq_scr[h] += _dot(dz, kb)
                dkt_add = dkt_add + _dot(qt_scr[h], dz)
                dvt_add = dvt_add + _dot(dot_scr[h], w_scr[h])
            dkt_scr[jc] += dkt_add
            dvt_scr[jc] += dvt_add

        def prologue(j):
            for r in (w_scr, dz_scr, dq_scr, c_scr, c2_scr):
                r[...] = jnp.zeros_like(r)
            scores(j, 0)

        def block(j, slot, masked):
            scores(j - 1, 1 - slot)
            finish(j + 1)
            past = ((kidx + j * tk) < qpos) if masked else None
            for h in range(2):
                ls, lk, w = _sb_weights(z_scr[slot, h], past, c_scr[h], tri)
                wb = w.astype(BF16)
                dlw = dw_scr[slot, h] * wb.astype(F32)
                hi, lo = _split_bf16(dlw)
                dlk = tot_scr[h] - c2_scr[h] - (_dot(hi, tri_ge) + _dot(lo, tri_ge))
                if masked:
                    dlk = jnp.where(past, dlk, 0.0)
                sig = jnp.exp(ls)
                w_scr[h] = wb
                dz_scr[h] = (dlw * (1.0 - sig) - dlk * sig).astype(BF16)
                c_scr[h] += jnp.sum(lk, axis=-1, keepdims=True)
                c2_scr[h] += jnp.sum(dlw, axis=-1, keepdims=True)

        _walk_key_blocks(i, ratio, prologue, block, finish, lambda: jnp.max(c_scr[...]))
        dq_ref[...] = jnp.where(sel[0], dq_scr[0], dq_scr[1])

        @pl.when(i == seq // tq - 1)
        def _():
            for jb in range(n_kblk):
                dk_ref[jb * tk:(jb + 1) * tk, :] = dkt_scr[jb].T
                dv_ref[jb * tk:(jb + 1) * tk, :] = dvt_scr[jb].T.astype(dv_ref.dtype)

    blk = pl.BlockSpec((tq, LANES), lambda p, i: (i, p))
    full = pl.BlockSpec((seq, LANES), lambda p, i: (0, p))
    shape = jax.ShapeDtypeStruct(qh.shape, F32)
    return _ride(
        "attn_bwd", (n_pair, seq // tq), riders, body,
        [blk, full, pl.BlockSpec((seq, LANES), lambda p, i: (0, v_col + p)), blk, blk],
        [blk, full, full], [shape, shape, jax.ShapeDtypeStruct(qh.shape, BF16)],
        [pltpu.VMEM((2, tq, LANES), BF16), pltpu.VMEM((2, tq, LANES), BF16),
         pltpu.VMEM((2, LANES, tq), BF16), pltpu.VMEM((2, LANES, tq), BF16),
         pltpu.VMEM((n_kblk, LANES, tk), F32), pltpu.VMEM((n_kblk, LANES, tk), F32),
         pltpu.VMEM((2, 2, tq, tk), F32), pltpu.VMEM((2, 2, tq, tk), F32),
         pltpu.VMEM((2, tq, tk), BF16), pltpu.VMEM((2, tq, tk), BF16), pltpu.VMEM((2, tq, LANES), F32),
         pltpu.VMEM((2, tq, 1), F32), pltpu.VMEM((2, tq, 1), F32), pltpu.VMEM((2, tq, 1), F32)],
        _params(2), [qh, kh, proj, y, dy])


def _loss_head(h1, ffn, target, gate_f, tile):
    seq, d = h1.shape

    def body(h_ref, f_ref, t_ref, g_ref, dy_ref, df_ref, dg_ref, loss_ref):
        @pl.when(pl.program_id(0) == 0)
        def _():
            dg_ref[...] = jnp.zeros_like(dg_ref)
            loss_ref[...] = jnp.zeros_like(loss_ref)

        f, g = f_ref[...], g_ref[...]
        err = h_ref[...] + g * f - t_ref[...]
        dy = err * (1.0 / d)
        dy_ref[...] = dy
        df_ref[...] = (dy * g).astype(df_ref.dtype)
        dg_ref[...] += jnp.sum(dy * f, axis=0, keepdims=True)
        loss_ref[...] += jnp.sum(jnp.sum(err * err, axis=-1, keepdims=True), axis=0, keepdims=True) * (0.5 / d)

    row = _row_spec(tile, d)
    return pl.pallas_call(
        body, name="loss_head", grid=(seq // tile,), in_specs=[row, row, row, _vec_spec(d)],
        out_specs=[row, row, _vec_spec(d), pl.BlockSpec((1, 1), lambda i: (0, 0))],
        out_shape=[jax.ShapeDtypeStruct((seq, d), F32), jax.ShapeDtypeStruct((seq, d), BF16),
                   jax.ShapeDtypeStruct((1, d), F32), jax.ShapeDtypeStruct((1, 1), F32)],
        compiler_params=_params(1),
    )(h1, ffn, target, gate_f)


def _dot3(a, b, dn):
    ah, al = _split_bf16(a)
    bh, bl = _split_bf16(b)
    return _dot(ah, bh, dn) + (_dot(ah, bl, dn) + _dot(al, bh, dn))


def _ada_forward(c_all, w_shard, b_cols):
    d, n = w_shard.shape
    bk = _tile(d, 512)

    def body(c_ref, w_ref, b_ref, o_ref):
        @pl.when(pl.program_id(0) == 0)
        def _():
            o_ref[...] = jnp.broadcast_to(b_ref[...], o_ref.shape)

        o_ref[...] += _dot3(jax.nn.silu(c_ref[...]), w_ref[...], NN)

    return pl.pallas_call(
        body, name="ada_fwd", grid=(d // bk,),
        in_specs=[pl.BlockSpec((NDEV, bk), lambda k: (0, k)), pl.BlockSpec((bk, n), lambda k: (k, 0)), _vec_spec(n)],
        out_specs=pl.BlockSpec((NDEV, n), lambda k: (0, 0)), out_shape=jax.ShapeDtypeStruct((NDEV, n), F32),
        compiler_params=_params(1),
    )(c_all, w_shard, b_cols)


def _adam(w, g, m, v):
    m = ADAM_B1 * m + (1.0 - ADAM_B1) * g
    v = ADAM_B2 * v + (1.0 - ADAM_B2) * (g * g)
    m_hat = m / (1.0 - ADAM_B1 ** ADAM_STEP)
    v_hat = v / (1.0 - ADAM_B2 ** ADAM_STEP)
    return -ADAM_LR * (m_hat / (jnp.sqrt(v_hat) + ADAM_EPS) + ADAM_WD * w), m, v


def _adam_ada(c_all, dmod_cols, w, m, v):
    d, n = w.shape
    tr = _tile(d, 256)

    def body(c_ref, dm_ref, w_ref, m_ref, v_ref, g_ref, dl_ref, nm_ref, nv_ref):
        g = _dot3(jax.nn.silu(c_ref[...]), dm_ref[...], TN)
        delta, nm, nv = _adam(w_ref[...], g, m_ref[...], v_ref[...])
        g_ref[...] = g
        dl_ref[...] = delta
        nm_ref[...] = nm
        nv_ref[...] = nv

    row = _row_spec(tr, n)
    return pl.pallas_call(
        body, name="adam_ada", grid=(d // tr,),
        in_specs=[pl.BlockSpec((NDEV, tr), lambda i: (0, i)), pl.BlockSpec((NDEV, n), lambda i: (0, 0)), row, row, row],
        out_specs=[row] * 4, out_shape=[jax.ShapeDtypeStruct((d, n), F32)] * 4, compiler_params=_params(1),
    )(c_all, dmod_cols, w, m, v)


def _adam_sum(name, parts, part_spec, w, m, v, tr):
    r, c = w.shape

    def body(p_ref, w_ref, m_ref, v_ref, g_ref, dl_ref, nm_ref, nv_ref):
        g = p_ref[0].astype(F32)
        for k in range(1, NDEV):
            g = g + p_ref[k].astype(F32)
        delta, nm, nv = _adam(w_ref[...], g, m_ref[...], v_ref[...])
        g_ref[...] = g
        dl_ref[...] = delta
        nm_ref[...] = nm
        nv_ref[...] = nv

    row = _row_spec(tr, c)
    return pl.pallas_call(
        body, name=name, grid=(r // tr,), in_specs=[part_spec, row, row, row],
        out_specs=[row] * 4, out_shape=[jax.ShapeDtypeStruct((r, c), F32)] * 4, compiler_params=_params(1),
    )(parts, w, m, v)


GROUPS_PER_BLOCK = LANES // SSM_GROUP


def _to_b_blocks(bb, n_blk, p):
    t = bb.reshape(n_blk, GROUPS_PER_BLOCK, p, SSM_GROUP)
    eye = jnp.eye(GROUPS_PER_BLOCK, dtype=bb.dtype)
    return jnp.einsum("jgph,gk->jghkp", t, eye).reshape(n_blk, LANES, GROUPS_PER_BLOCK * p)


def _from_b_blocks(blk, n_blk, p):
    t = blk.reshape(n_blk, GROUPS_PER_BLOCK, SSM_GROUP, GROUPS_PER_BLOCK, p)
    eye = jnp.eye(GROUPS_PER_BLOCK, dtype=blk.dtype)
    return jnp.einsum("jghkp,gk->jgph", t, eye).reshape(n_blk * GROUPS_PER_BLOCK, p, SSM_GROUP)


def _to_c_blocks(cc, n_blk, p):
    t = cc.reshape(n_blk, GROUPS_PER_BLOCK, SSM_GROUP, p)
    eye = jnp.eye(GROUPS_PER_BLOCK, dtype=cc.dtype)
    return jnp.einsum("jghp,gk->jgpkh", t, eye).reshape(n_blk, GROUPS_PER_BLOCK * p, LANES)


def _from_c_blocks(blk, n_blk, p):
    t = blk.reshape(n_blk, GROUPS_PER_BLOCK, p, GROUPS_PER_BLOCK, SSM_GROUP)
    eye = jnp.eye(GROUPS_PER_BLOCK, dtype=blk.dtype)
    return jnp.einsum("jgpkh,gk->jghp", t, eye).reshape(n_blk * GROUPS_PER_BLOCK, SSM_GROUP, p)


SMALL_LATE = ("b_ada_a", "g_mix")
SMALL_EARLY = ("b_ada_b", "a_re", "a_im", "log_dt", "b_re", "b_im", "c_re", "c_im", "d_skip",
               "q_gain", "k_gain", "g_ssm_out", "g_attn_out", "g_ffn")
PACK_COLS = 1024


def _pack(arrs):
    flat = jnp.concatenate([a.reshape(-1) for a in arrs])
    n = flat.shape[0]
    quantum = SUBLANES * PACK_COLS
    padded = -(-n // quantum) * quantum
    return jnp.pad(flat, (0, padded - n)).reshape(padded // PACK_COLS, PACK_COLS)


def _unpack(packed, like):
    flat, out, off = packed.reshape(-1), [], 0
    for a in like:
        out.append(flat[off:off + a.size].reshape(a.shape))
        off += a.size
    return out


def kernel(x, c, w_ada, b_ada, g_mix, w_in, a_re, a_im, log_dt, b_re, b_im, c_re, c_im, d_skip, w_glu, q_gain, k_gain, g_ssm_out, g_attn_out, w_out, g_ffn, w_gate, w_up, w_down, loss_target, m_w_ada, m_b_ada, m_g_mix, m_w_in, m_a_re, m_a_im, m_log_dt, m_b_re, m_b_im, m_c_re, m_c_im, m_d_skip, m_w_glu, m_q_gain, m_k_gain, m_g_ssm_out, m_g_attn_out, m_w_out, m_g_ffn, m_w_gate, m_w_up, m_w_down, v_w_ada, v_b_ada, v_g_mix, v_w_in, v_a_re, v_a_im, v_log_dt, v_b_re, v_b_im, v_c_re, v_c_im, v_d_skip, v_w_glu, v_q_gain, v_k_gain, v_g_ssm_out, v_g_attn_out, v_w_out, v_g_ffn, v_w_gate, v_w_up, v_w_down):
    given = dict(locals())
    seq, d = x.shape[1], x.shape[2]
    xs, tgt = x[0], loss_target[0]
    n_groups, p_state = a_re.shape[1], a_re.shape[2]
    w_ssm = n_groups * SSM_GROUP
    w_attn = w_in.shape[2] * NDEV - w_ssm
    w_attn //= 3
    n_blk, n_pair = w_ssm // LANES, w_attn // LANES
    n_heads = w_attn // HEAD_DIM
    ns_in, ns_ff = w_in.shape[2], w_gate.shape[2]
    d_mix = w_ssm + w_attn
    mx, my, mc = _me()
    me = 4 * mx + 2 * my + mc
    rt = _tile(seq, 256)
    n_rt = seq // rt
    sds = jax.ShapeDtypeStruct

    c_all = _exchange(c, True, "comm_ag_c").reshape(NDEV, d)
    n_ada = w_ada.shape[2]
    b_cols = lax.dynamic_slice(b_ada, (0, me * n_ada), (1, n_ada))
    mod_cols = _ada_forward(c_all, w_ada[0], b_cols)
    mod_all = _exchange(mod_cols, True, "comm_ag_mod")
    mod = lax.dynamic_slice(mod_all, (0, me, 0), (NDEV, 1, n_ada)).reshape(1, NDEV * n_ada)
    shift_m, scale_m, gate_m, shift_f, scale_f, gate_f = [mod[:, i * d:(i + 1) * d] for i in range(6)]

    gp = n_groups * p_state
    a_re2, a_im2, ldt2 = a_re[0], a_im[0], log_dt[0].reshape(n_groups, 1)
    b_re2, b_im2 = b_re[0].reshape(gp, SSM_GROUP), b_im[0].reshape(gp, SSM_GROUP)
    lam_r, lam_i, coef_r, coef_i = _whole("s5_lam", _s5_lam, [a_re2, a_im2, ldt2], [(n_groups, p_state)] * 4)
    coef_r2, coef_i2 = coef_r.reshape(gp, 1), coef_i.reshape(gp, 1)
    bb_r, bb_i = _whole("s5_bbar", _s5_bbar, [coef_r2, coef_i2, b_re2, b_im2], [(gp, SSM_GROUP)] * 2)
    s_blk = GROUPS_PER_BLOCK * p_state
    b_blk_r = _to_b_blocks(bb_r.reshape(n_groups, p_state, SSM_GROUP), n_blk, p_state).astype(BF16)
    b_blk_i = _to_b_blocks(bb_i.reshape(n_groups, p_state, SSM_GROUP), n_blk, p_state).astype(BF16)
    c_blk_r = _to_c_blocks(c_re[0], n_blk, p_state).astype(BF16)
    c_blk_i = _to_c_blocks(c_im[0], n_blk, p_state).astype(BF16)
    lam_r3, lam_i3 = lam_r.reshape(n_blk, 1, s_blk), lam_i.reshape(n_blk, 1, s_blk)
    d_skip2 = d_skip[0].reshape(1, w_ssm)

    row_d, vec_d = _row_spec(rt, d), _vec_spec(d)
    (xm,), (w_in_g,) = _rowwise_fwd("seg_in", lambda *a: _seg_in(*a)[:1], [xs], [row_d], [shift_m, scale_m, g_mix], [vec_d] * 3,
                                    [sds((seq, d), BF16)], [row_d], n_rt,
                                    riders=[_Rider(w_in[0].astype(BF16), True, relayed=True)])
    bn_in = _tile(ns_in, 512)
    per = ns_in // bn_in
    bm, bk = _tile(seq, BM), _tile(d, BK)
    proj, w_glu_g, w_out_g = _mm(
        "mm_in", xm, w_in_g, NN, (seq // bm, NDEV * per, d // bk),
        pl.BlockSpec((bm, bk), lambda i, j, k: (i, k)),
        pl.BlockSpec((None, bk, bn_in), lambda i, j, k: (j // per, k, j % per)),
        pl.BlockSpec((bm, bn_in), lambda i, j, k: (i, j)), (seq, NDEV * ns_in), F32, (bm, bn_in),
        riders=[_Rider(w_glu[0].astype(BF16), True, relayed=True), _Rider(w_out[0].astype(BF16), True, relayed=True)])
    w_glu_g, w_out_g = w_glu_g.reshape(w_ssm, w_ssm), w_out_g.reshape(d_mix, d)
    q_col, k_col, v_col = w_ssm // w_attn, w_ssm // w_attn + 1, (w_ssm + 2 * w_attn) // LANES
    qg_t, kg_t = jnp.tile(q_gain, (1, n_heads)), jnp.tile(k_gain, (1, n_heads))
    row_a, vec_a = _row_spec(rt, w_attn), _vec_spec(w_attn)
    qk_rows, qk_specs = [proj, proj], [_row_spec(rt, w_attn, q_col), _row_spec(rt, w_attn, k_col)]
    qh, kh = _rowwise_fwd("seg_qk", _seg_qk, qk_rows, qk_specs, [qg_t, kg_t], [vec_a] * 2,
                          [sds((seq, w_attn), F32)] * 2, [row_a] * 2, n_rt)
    t_chunk = _tile(seq, 512)
    (ypre, x_re, x_im), (w_gu_land,) = _s5_forward(
        proj, b_blk_r, b_blk_i, c_blk_r, c_blk_i, lam_r3, lam_i3, d_skip2, n_blk, t_chunk,
        riders=[_Rider(w_up[0].astype(BF16), True, slot=1, n_slots=2, relayed=True)])
    tq, tk = _tile(seq, TQ), _tile(seq, TK)
    y_attn, (w_gu_land,) = _attention_forward(qh, kh, proj, v_col, n_pair, tq, tk,
                                              riders=[_Rider(w_gate[0].astype(BF16), True, land=w_gu_land, slot=0,
                                                             relayed=True)])
    w_gu_g = w_gu_land.reshape(2 * NDEV, d, ns_ff)
    row_s, vec_s = _row_spec(rt, w_ssm), _vec_spec(w_ssm)
    y1, = _rowwise_fwd("seg_gelu", _seg_gelu, [ypre], [row_s], [], [], [sds((seq, w_ssm), F32)], [row_s], n_rt)
    z = _mm_plain("mm_glu", y1, w_glu_g, NN, F32)
    row_m = _row_spec(rt, d_mix)
    mixed, = _rowwise_fwd("seg_mix", _seg_mix, [y1, z, y_attn], [row_s, row_s, row_a], [g_ssm_out, g_attn_out], [vec_s, vec_a],
                          [sds((seq, d_mix), BF16)], [row_m], n_rt)
    o = _mm_plain("mm_out", mixed, w_out_g, NN, F32)
    h1, xf = _rowwise_fwd("seg_mid", _seg_mid, [xs, o], [row_d] * 2, [gate_m, g_ffn, scale_f, shift_f], [vec_d] * 4,
                          [sds((seq, d), F32), sds((seq, d), BF16)], [row_d] * 2, n_rt)
    gu, w_down_g = _mm(
        "mm_gu", xf, w_gu_g, NN, (seq // bm, 2 * NDEV, d // bk),
        pl.BlockSpec((bm, bk), lambda i, j, k: (i, k)), pl.BlockSpec((None, bk, ns_ff), lambda i, j, k: (j, k, 0)),
        pl.BlockSpec((None, bm, ns_ff), lambda i, j, k: (j, i, 0)), (2 * NDEV, seq, ns_ff), BF16, (bm, ns_ff),
        riders=[_Rider(w_down[0].astype(BF16), True, relayed=True)])
    gu4 = gu.reshape(2, NDEV, seq, ns_ff)
    ft = _tile(seq, 512)
    pair_spec = pl.BlockSpec((2, None, ft, ns_ff), lambda s, i: (0, s, i, 0))
    one_spec = pl.BlockSpec((None, ft, ns_ff), lambda s, i: (s, i, 0))

    def act_body(gu_ref, a_ref):
        a_ref[...] = _seg_act(gu_ref[0].astype(F32), gu_ref[1].astype(F32))[0].astype(a_ref.dtype)

    act = pl.pallas_call(act_body, name="seg_act", grid=(NDEV, seq // ft), in_specs=[pair_spec], out_specs=one_spec,
                         out_shape=sds((NDEV, seq, ns_ff), BF16), compiler_params=_params(2))(gu4)
    bn_d = _tile(d, BN)
    bm_h = _tile(seq, BM // 2)
    shard_pieces = lambda n: (lambda a_ref, b_ref: [(a_ref[g], b_ref[g]) for g in range(n)])
    ffn = _mm("mm_down", act, w_down_g, NN, (seq // bm_h, d // bn_d, 1),
              pl.BlockSpec((NDEV, bm_h, ns_ff), lambda i, j, k: (0, i, 0)), pl.BlockSpec((NDEV, ns_ff, bn_d), lambda i, j, k: (0, 0, j)),
              pl.BlockSpec((bm_h, bn_d), lambda i, j, k: (i, j)), (seq, d), F32, (bm_h, bn_d), pieces=shard_pieces(NDEV))
    dy, dffn, d_gate_f, loss_part = _loss_head(h1, ffn, tgt, gate_f, rt)
    loss = lax.psum(loss_part[0, 0], MESH_AXES)

    bl = _tile(seq, BK)
    gw_down = _mm("mm_dw_down", act, dffn, TN, (NDEV, d // bn_d, seq // bl),
                  pl.BlockSpec((None, bl, ns_ff), lambda i, j, k: (i, k, 0)), pl.BlockSpec((bl, bn_d), lambda i, j, k: (k, j)),
                  pl.BlockSpec((None, ns_ff, bn_d), lambda i, j, k: (i, 0, j)), (NDEV, ns_ff, d), BF16, (ns_ff, bn_d))
    rows_down = (ns_ff // 32) * 16
    dact, got_down = _mm(
        "mm_dact", dffn, w_down_g, NT, (seq // bm, NDEV, d // bk),
        pl.BlockSpec((bm, bk), lambda i, j, k: (i, k)), pl.BlockSpec((None, ns_ff, bk), lambda i, j, k: (j, 0, k)),
        pl.BlockSpec((None, bm, ns_ff), lambda i, j, k: (j, i, 0)), (NDEV, seq, ns_ff), BF16, (bm, ns_ff),
        riders=[_Rider(gw_down, False, part=(0, rows_down))])

    def dact_body(gu_ref, da_ref, dgu_ref):
        _, vjp = jax.vjp(_seg_act, gu_ref[0].astype(F32), gu_ref[1].astype(F32))
        dg, du_ = vjp((da_ref[...].astype(F32),))
        dgu_ref[0] = dg.astype(dgu_ref.dtype)
        dgu_ref[1] = du_.astype(dgu_ref.dtype)

    dgu4 = pl.pallas_call(dact_body, name="seg_act_bwd", grid=(NDEV, seq // ft), in_specs=[pair_spec, one_spec],
                          out_specs=pair_spec, out_shape=sds((2, NDEV, seq, ns_ff), BF16), compiler_params=_params(2))(gu4, dact)
    dgu = dgu4.reshape(2 * NDEV, seq, ns_ff)
    bmd = _tile(d, BM)

    def dw_half(name, which, riders):
        return _mm(name, xf, dgu, TN, (d // bmd, NDEV, seq // bl), pl.BlockSpec((bl, bmd), lambda i, j, k: (k, i)),
                   pl.BlockSpec((None, bl, ns_ff), lambda i, j, k: (which * NDEV + j, k, 0)),
                   pl.BlockSpec((None, bmd, ns_ff), lambda i, j, k: (j, i, 0)), (NDEV, d, ns_ff), BF16, (bmd, ns_ff), riders=riders)

    gw_gate, got_down = dw_half("mm_dw_gate", 0, [_Rider(gw_down, False, land=got_down, part=(rows_down, ns_ff - rows_down))])
    gw_up = dw_half("mm_dw_up", 1, ())
    dxf = _mm(
        "mm_dxf", dgu, w_gu_g, NT, (seq // bm, d // bn_d, 4),
        pl.BlockSpec((4, bm, ns_ff), lambda i, j, k: (k, i, 0)), pl.BlockSpec((4, bn_d, ns_ff), lambda i, j, k: (k, j, 0)),
        pl.BlockSpec((bm, bn_d), lambda i, j, k: (i, j)), (seq, d), F32, (bm, bn_d), pieces=shard_pieces(4))
    (do, dx_a, d_gate_m, d_g_ffn, d_scale_f, d_shift_f) = _rowwise_bwd(
        "seg_mid_bwd", _seg_mid, [xs, o], [row_d] * 2, [gate_m, g_ffn, scale_f, shift_f], [vec_d] * 4,
        [dy, dxf], [row_d] * 2, [[0], [1]], [1, 0], [sds((seq, d), BF16), sds((seq, d), F32)], [row_d] * 2,
        [0, 1, 2, 3], [sds((1, d), F32)] * 4, [vec_d] * 4, n_rt)

    dmixed = _mm_plain("mm_dmixed", do, w_out_g, NT, F32)
    gw_out = _mm_plain("mm_dw_out", mixed, do, TN, BF16)
    (dz, dy1_a, dy_attn, d_g_ssm, d_g_attn) = _rowwise_bwd(
        "seg_mix_bwd", _seg_mix, [y1, z, y_attn], [row_s, row_s, row_a], [g_ssm_out, g_attn_out], [vec_s, vec_a],
        [dmixed], [row_m], [[0]], [1, 0, 2], [sds((seq, w_ssm), BF16), sds((seq, w_ssm), F32), sds((seq, w_attn), F32)],
        [row_s, row_s, row_a], [0, 1], [sds((1, w_ssm), F32), sds((1, w_attn), F32)], [vec_s, vec_a], n_rt)
    dy1_b = _mm_plain("mm_dy1", dz, w_glu_g, NT, F32)
    gw_glu = _mm_plain("mm_dw_glu", y1, dz, TN, BF16)
    (dypre,) = _rowwise_bwd("seg_gelu_bwd", _seg_gelu, [ypre], [row_s], [], [], [dy1_a, dy1_b], [row_s] * 2, [[0, 1]],
                            [0], [sds((seq, w_ssm), F32)], [row_s], [], [], [], n_rt)
    (du, db_blk_r, db_blk_i, dc_blk_r, dc_blk_i, dlam_r3, dlam_i3, dd_skip2), (got_gate,) = _s5_backward(
        dypre, proj, x_re, x_im, b_blk_r, b_blk_i, c_blk_r, c_blk_i, lam_r3, lam_i3, d_skip2, n_blk, t_chunk,
        riders=[_Rider(gw_gate, False)])
    (dqh, dkh, dv), (got_up, got_out, got_glu) = _attention_backward(
        qh, kh, proj, v_col, y_attn, dy_attn, n_pair, tq, tk,
        riders=[_Rider(gw_up, False), _Rider(gw_out.reshape(NDEV, w_out.shape[1], d), False),
                _Rider(gw_glu.reshape(NDEV, w_glu.shape[1], w_ssm), False)])
    (dq, dk, dqg_t, dkg_t) = _rowwise_bwd(
        "seg_qk_bwd", _seg_qk, qk_rows, qk_specs, [qg_t, kg_t], [vec_a] * 2, [dqh, dkh], [row_a] * 2, [[0], [1]],
        [0, 1], [sds((seq, w_attn), BF16)] * 2, [row_a] * 2, [0, 1], [sds((1, w_attn), F32)] * 2, [vec_a] * 2, n_rt)

    dbb_r = _from_b_blocks(db_blk_r, n_blk, p_state).reshape(gp, SSM_GROUP)
    dbb_i = _from_b_blocks(db_blk_i, n_blk, p_state).reshape(gp, SSM_GROUP)
    dcoef_r2, dcoef_i2, db_re2, db_im2 = _whole_vjp("s5_bbar_bwd", _s5_bbar, [coef_r2, coef_i2, b_re2, b_im2], [dbb_r, dbb_i],
                                                    [(gp, 1), (gp, 1), (gp, SSM_GROUP), (gp, SSM_GROUP)])
    lam_cts = [dlam_r3.reshape(n_groups, p_state), dlam_i3.reshape(n_groups, p_state),
               dcoef_r2.reshape(n_groups, p_state), dcoef_i2.reshape(n_groups, p_state)]
    da_re2, da_im2, dldt2 = _whole_vjp("s5_lam_bwd", _s5_lam, [a_re2, a_im2, ldt2], lam_cts,
                                       [(n_groups, p_state), (n_groups, p_state), (n_groups, 1)])
    dc_re2, dc_im2 = _from_c_blocks(dc_blk_r, n_blk, p_state), _from_c_blocks(dc_blk_i, n_blk, p_state)

    small_part = {
        "b_ada_b": jnp.concatenate([d_gate_m, d_shift_f, d_scale_f, d_gate_f], axis=-1),
        "a_re": da_re2, "a_im": da_im2, "log_dt": dldt2, "b_re": db_re2, "b_im": db_im2,
        "c_re": dc_re2, "c_im": dc_im2, "d_skip": dd_skip2,
        "q_gain": dqg_t.reshape(n_heads, HEAD_DIM).sum(0), "k_gain": dkg_t.reshape(n_heads, HEAD_DIM).sum(0),
        "g_ssm_out": d_g_ssm, "g_attn_out": d_g_attn, "g_ffn": d_g_ffn,
    }
    dproj = jnp.concatenate([du, dq, dk, dv], axis=-1)
    gw_in, early_parts = _mm(
        "mm_dw_in", xm, dproj, TN, (d // bmd, NDEV * per, seq // bl),
        pl.BlockSpec((bl, bmd), lambda i, j, k: (k, i)), pl.BlockSpec((bl, bn_in), lambda i, j, k: (k, j)),
        pl.BlockSpec((None, bmd, bn_in), lambda i, j, k: (j // per, i, j % per)), (NDEV, d, ns_in), BF16, (bmd, bn_in),
        riders=[_Rider(_pack([small_part[n] for n in SMALL_EARLY]), True, relayed=True)])
    rows_first = (9 * d) // 16
    dxm, got_in = _mm(
        "mm_dxm", dproj, w_in_g, NT, (seq // bm, d // bn_d, 1),
        pl.BlockSpec((bm, NDEV * ns_in), lambda i, j, k: (i, 0)), pl.BlockSpec((NDEV, bn_d, ns_in), lambda i, j, k: (0, j, 0)),
        pl.BlockSpec((bm, bn_d), lambda i, j, k: (i, j)), (seq, d), F32, (bm, bn_d),
        riders=[_Rider(gw_in, False, part=(0, rows_first))],
        pieces=lambda a_ref, b_ref: [(a_ref[:, g * ns_in:(g + 1) * ns_in], b_ref[g]) for g in range(NDEV)])
    (grad_x, d_shift_m, d_scale_m, d_g_mix), (got_in,) = _rowwise_bwd(
        "seg_in_bwd", _seg_in, [xs], [row_d], [shift_m, scale_m, g_mix], [vec_d] * 3, [dxm, dx_a], [row_d] * 2, [[0], [1]],
        [0], [sds((seq, d), F32)], [row_d], [0, 1, 2], [sds((1, d), F32)] * 3, [vec_d] * 3, n_rt,
        riders=[_Rider(gw_in, False, land=got_in, part=(rows_first, d - rows_first))])
    small_part["b_ada_a"] = jnp.concatenate([d_shift_m, d_scale_m], axis=-1)
    small_part["g_mix"] = d_g_mix
    late_parts = _exchange(_pack([small_part[n] for n in SMALL_LATE]), True, "comm_ag_small_late")
    packed_parts = jnp.concatenate([late_parts, early_parts], axis=1)

    big = {}

    def sharded(nm, got, width, tr):
        big[nm] = _adam_sum("adam_" + nm, got, pl.BlockSpec((NDEV, tr, width), lambda i: (0, i, 0)), given[nm][0],
                            given["m_" + nm][0], given["v_" + nm][0], tr)

    sharded("w_down", got_down, d, _tile(w_down.shape[1], 64))
    sharded("w_gate", got_gate, ns_ff, _tile(d, 256))
    sharded("w_up", got_up, ns_ff, _tile(d, 256))
    sharded("w_out", got_out, d, _tile(w_out.shape[1], 128))
    sharded("w_glu", got_glu, w_ssm, _tile(w_glu.shape[1], 128))
    sharded("w_in", got_in, ns_in, _tile(d, 256))

    split = dict(given)
    for pre in ("", "m_", "v_"):
        split[pre + "b_ada_a"], split[pre + "b_ada_b"] = given[pre + "b_ada"][:, :2 * d], given[pre + "b_ada"][:, 2 * d:]
    packs = [jnp.concatenate([_pack([split[pre + n] for n in SMALL_LATE]), _pack([split[pre + n] for n in SMALL_EARLY])])
             for pre in ("", "m_", "v_")]
    rows_p = packed_parts.shape[1]
    tr_p = _tile(rows_p, 64)
    sm = _adam_sum("adam_small", packed_parts, pl.BlockSpec((NDEV, tr_p, PACK_COLS), lambda i: (0, i, 0)), *packs, tr_p)
    rows_late = late_parts.shape[1]
    small_out = []
    for t in sm:
        out = dict(zip(SMALL_LATE, _unpack(t[:rows_late], [split[n] for n in SMALL_LATE])))
        out.update(zip(SMALL_EARLY, _unpack(t[rows_late:], [split[n] for n in SMALL_EARLY])))
        out["b_ada"] = jnp.concatenate([out["b_ada_a"], out["b_ada_b"]], axis=1)
        small_out.append(out)

    rows_a, rows_b = (2 * d) // PACK_COLS, (4 * d) // PACK_COLS
    assert rows_a * PACK_COLS == 2 * d
    dmod_all = jnp.concatenate([late_parts[:, :rows_a].reshape(NDEV, 2 * d), early_parts[:, :rows_b].reshape(NDEV, 4 * d)], axis=1)
    dmod_cols = lax.dynamic_slice(dmod_all, (0, me * n_ada), (NDEV, n_ada))
    big["w_ada"] = _adam_ada(c_all, dmod_cols, w_ada[0], m_w_ada[0], v_w_ada[0])

    order = ("w_ada", "b_ada", "g_mix", "w_in", "a_re", "a_im", "log_dt", "b_re", "b_im", "c_re", "c_im", "d_skip", "w_glu",
             "q_gain", "k_gain", "g_ssm_out", "g_attn_out", "w_out", "g_ffn", "w_gate", "w_up", "w_down")
    outs = [loss, grad_x[None]]
    for kind in range(4):
        for n in order:
            outs.append(big[n][kind][None] if n in big else small_out[kind][n])
    return tuple(outs)
```

```python
import functools
import math

import jax
import jax.numpy as jnp
from jax import lax
from jax.experimental import pallas as pl
from jax.experimental.pallas import tpu as pltpu

F32 = jnp.float32
BF16 = jnp.bfloat16
NDEV = 8
MESH_AXES = ("x", "y", "c")
MESH_ID = pl.DeviceIdType.MESH
EPS = 1e-6
LANES = 128
SUBLANES = 8
HEAD_DIM = 64
SSM_GROUP = 16
ADAM_LR, ADAM_B1, ADAM_B2, ADAM_EPS, ADAM_WD, ADAM_STEP = 0.001, 0.9, 0.999, 1e-08, 0.01, 10

NN = (((1,), (0,)), ((), ()))
NT = (((1,), (1,)), ((), ()))
TN = (((0,), (0,)), ((), ()))


def _dot(a, b, dn=NN):
    return lax.dot_general(a, b, dn, preferred_element_type=F32)


def _tile(dim, pref):
    t = min(dim, pref)
    while dim % t:
        t //= 2
    return t


def _params(n):
    return pltpu.CompilerParams(dimension_semantics=("arbitrary",) * n)


def _me():
    mx, my, mc = lax.axis_index("x"), lax.axis_index("y"), lax.axis_index("c")
    return mx, my, mc


def _peer(mx, my, mc, k):
    px = 1 - mx if (k >> 2) & 1 else mx
    py = 1 - my if (k >> 1) & 1 else my
    pc = 1 - mc if k & 1 else mc
    return (px, py, pc), 4 * px + 2 * py + pc


def _slot(ref, idx, part):
    return ref.at[idx] if part is None else ref.at[idx, pl.ds(*part)]


def _exchange_copies(x_ref, land_ref, send_sems, recv_sems, gather, part=None):
    mx, my, mc = _me()
    me = 4 * mx + 2 * my + mc
    pairs = []
    for k in range(1, NDEV):
        peer, pidx = _peer(mx, my, mc, k)
        src = x_ref if gather else _slot(x_ref, pidx, part)
        mk = lambda dst, src=src, k=k, peer=peer: pltpu.make_async_remote_copy(
            src_ref=src, dst_ref=dst, send_sem=send_sems.at[k - 1], recv_sem=recv_sems.at[k - 1],
            device_id=peer, device_id_type=MESH_ID)
        pairs.append((mk(_slot(land_ref, me, part)), mk(_slot(land_ref, pidx, part))))
    return me, pairs


def _exchange(x, gather, name, relayed=False):
    def body(x_ref, o_ref, send_sems, recv_sems, local_sem):
        if relayed:
            phases = _relayed_gather_phases(x_ref, o_ref, send_sems, recv_sems, local_sem)
        else:
            phases = _direct_phases(x_ref, o_ref, send_sems, recv_sems, local_sem, gather)
        for phase in phases:
            if phase is not None:
                phase()

    return pl.pallas_call(
        body, name=name, out_shape=jax.ShapeDtypeStruct(((NDEV,) + x.shape) if gather else x.shape, x.dtype),
        in_specs=[pl.BlockSpec(memory_space=pl.ANY)], out_specs=pl.BlockSpec(memory_space=pl.ANY),
        scratch_shapes=[pltpu.SemaphoreType.DMA((NDEV - 1,)), pltpu.SemaphoreType.DMA((NDEV - 1,)), pltpu.SemaphoreType.DMA],
    )(x)


def _direct_phases(x_ref, zone, send_sems, recv_sems, local_sem, gather, part=None):
    me, pairs = _exchange_copies(x_ref, zone, send_sems, recv_sems, gather, part)
    local = pltpu.make_async_copy(x_ref if gather else _slot(x_ref, me, part), _slot(zone, me, part), local_sem)

    def start():
        for send, _ in pairs:
            send.start()
        local.start()

    def finish():
        for send, arrival in pairs:
            send.wait_send()
            arrival.wait_recv()
        local.wait()

    return start, None, finish


def _relayed_gather_phases(x_ref, zone, send_sems, recv_sems, local_sem):
    mx, my, mc = _me()
    me, sibling = (mx, my, mc), (mx, my, 1 - mc)
    chips = [(1 - mx, my), (mx, 1 - my), (1 - mx, 1 - my)]
    rows = lambda dev: zone.at[4 * dev[0] + 2 * dev[1] + dev[2]]

    def copy(k, block, to, src=None):
        return pltpu.make_async_remote_copy(src_ref=rows(block) if src is None else src, dst_ref=rows(block),
                                            send_sem=send_sems.at[k], recv_sem=recv_sems.at[k], device_id=to,
                                            device_id_type=MESH_ID)

    local = pltpu.make_async_copy(x_ref, rows(me), local_sem)
    first = [copy(0, me, sibling, x_ref)] + [copy(1 + j, me, (*chip, mc), x_ref) for j, chip in enumerate(chips)]
    passed = [copy(4 + j, (*chip, mc), sibling) for j, chip in enumerate(chips)]
    over_links = [copy(1 + j, (*chip, mc), me) for j, chip in enumerate(chips)]
    from_sibling = [copy(0, sibling, me)] + [copy(4 + j, (*chip, 1 - mc), me) for j, chip in enumerate(chips)]

    def start():
        local.start()
        for cp in first:
            cp.start()

    def relay():
        for arrival, onward in zip(over_links, passed):
            arrival.wait_recv()
            onward.start()

    def finish():
        for arrival in from_sibling:
            arrival.wait_recv()
        for cp in first + passed:
            cp.wait_send()
        local.wait()

    return start, relay, finish


class _Rider:
    def __init__(self, x, gather, land=None, slot=None, n_slots=None, relayed=False, part=None):
        self.x, self.gather, self.land, self.slot, self.relayed, self.part = x, gather, land, slot, relayed, part
        own = ((NDEV,) + x.shape) if gather else x.shape
        self.land_shape = land.shape if land is not None else (own if n_slots is None else (n_slots,) + own)

    def phases(self, x_ref, land_ref, send_sems, recv_sems, local_sem):
        zone = land_ref if self.slot is None else land_ref.at[self.slot]
        if self.relayed:
            return _relayed_gather_phases(x_ref, zone, send_sems, recv_sems, local_sem)
        return _direct_phases(x_ref, zone, send_sems, recv_sems, local_sem, self.gather, self.part)


def _ride(call_name, grid, riders, inner, in_specs, out_specs, out_shape, scratch_shapes, compiler_params, operands):
    n_in, n_out, n_scr = len(in_specs), len(out_specs), len(scratch_shapes)
    any_spec = pl.BlockSpec(memory_space=pl.ANY)
    extra_in, aliases = [], {}
    for r_idx, r in enumerate(riders):
        extra_in.append(r.x)
        if r.land is not None:
            aliases[n_in + len(extra_in)] = n_out + r_idx
            extra_in.append(r.land)
    sems = []
    for _ in riders:
        sems += [pltpu.SemaphoreType.DMA((NDEV - 1,)), pltpu.SemaphoreType.DMA((NDEV - 1,)), pltpu.SemaphoreType.DMA]

    def body(*refs):
        base_in, rest = refs[:n_in], refs[n_in:]
        rider_in, rest = rest[:len(extra_in)], rest[len(extra_in):]
        base_out, rest = rest[:n_out], rest[n_out:]
        lands, rest = rest[:len(riders)], rest[len(riders):]
        base_scr, rider_sems = rest[:n_scr], rest[n_scr:]
        step = 0
        for a, g in enumerate(grid):
            step = step * g + pl.program_id(a)
        n_steps = math.prod(grid)
        sets, pos = [], 0
        for r_idx, r in enumerate(riders):
            x_ref = rider_in[pos]
            pos += 2 if r.land is not None else 1
            sets.append(r.phases(x_ref, lands[r_idx], *rider_sems[3 * r_idx:3 * r_idx + 3]))

        if sets:
            @pl.when(step == 0)
            def _():
                for start, _, _ in sets:
                    start()

        inner(*base_in, *base_out, *base_scr)

        if any(relay is not None for _, relay, _ in sets):
            @pl.when(step == (3 * n_steps) // 5)
            def _():
                for _, relay, _ in sets:
                    if relay is not None:
                        relay()

        if sets:
            @pl.when(step == n_steps - 1)
            def _():
                for _, _, finish in sets:
                    finish()

    outs = pl.pallas_call(
        body, name=call_name, grid=grid, in_specs=list(in_specs) + [any_spec] * len(extra_in),
        out_specs=list(out_specs) + [any_spec] * len(riders),
        out_shape=list(out_shape) + [jax.ShapeDtypeStruct(r.land_shape, r.x.dtype) for r in riders],
        scratch_shapes=list(scratch_shapes) + sems, input_output_aliases=aliases, compiler_params=compiler_params,
    )(*operands, *extra_in)
    return outs[:n_out], outs[n_out:]


def _mm(name, a, b, dn, grid, a_spec, b_spec, o_spec, out_shape, out_dtype, acc_shape, riders=(), pieces=None):
    nk = grid[2]

    def body(a_ref, b_ref, o_ref, *scratch):
        if pieces is None:
            part = _dot(a_ref[...].astype(BF16), b_ref[...].astype(BF16), dn)
        else:
            part = functools.reduce(lambda p, q: p + q, [_dot(a_g.astype(BF16), b_g.astype(BF16), dn)
                                                         for a_g, b_g in pieces(a_ref, b_ref)])
        if nk == 1:
            o_ref[...] = part.astype(o_ref.dtype)
            return
        acc_ref = scratch[0]
        k = pl.program_id(2)

        @pl.when(k == 0)
        def _():
            acc_ref[...] = part

        @pl.when(k > 0)
        def _():
            acc_ref[...] += part

        @pl.when(k == nk - 1)
        def _():
            o_ref[...] = acc_ref[...].astype(o_ref.dtype)

    (out,), lands = _ride(name, grid, riders, body, [a_spec, b_spec], [o_spec], [jax.ShapeDtypeStruct(out_shape, out_dtype)],
                          [] if nk == 1 else [pltpu.VMEM(acc_shape, F32)], _params(3), [a, b])
    return (out, *lands) if riders else out


BM, BN, BK = 1024, 1024, 4096


def _mm_plain(name, a, b, dn, out_dtype):
    if dn == NN:
        (m, kk), n = a.shape, b.shape[1]
    elif dn == NT:
        (m, kk), n = a.shape, b.shape[0]
    else:
        (kk, m), n = a.shape, b.shape[1]
    half = 2 if dn == TN else 1
    bm, bn, bk = _tile(m, BM // half), _tile(n, BN // half), _tile(kk, BK)
    a_spec = pl.BlockSpec((bk, bm), lambda i, j, k: (k, i)) if dn == TN else pl.BlockSpec((bm, bk), lambda i, j, k: (i, k))
    b_spec = pl.BlockSpec((bn, bk), lambda i, j, k: (j, k)) if dn == NT else pl.BlockSpec((bk, bn), lambda i, j, k: (k, j))
    return _mm(name, a, b, dn, (m // bm, n // bn, kk // bk), a_spec, b_spec,
               pl.BlockSpec((bm, bn), lambda i, j, k: (i, j)), (m, n), out_dtype, (bm, bn))


def _row_spec(tile, width, col=0):
    return pl.BlockSpec((tile, width), lambda i: (i, col))


def _vec_spec(width, col=0):
    return pl.BlockSpec((1, width), lambda i: (0, col))


def _rowwise_fwd(name, fn, rows, row_specs, vecs, vec_specs, out_shapes, out_specs, n_tiles, riders=()):
    nr, nv = len(rows), len(vecs)

    def body(*refs):
        ins = [r[...].astype(F32) for r in refs[:nr + nv]]
        outs = fn(*ins)
        for o_ref, o in zip(refs[nr + nv:], outs):
            o_ref[...] = o.astype(o_ref.dtype)

    outs, lands = _ride(name, (n_tiles,), riders, body, list(row_specs) + list(vec_specs), list(out_specs),
                        list(out_shapes), [], _params(1), [*rows, *vecs])
    return (outs, lands) if riders else outs


def _rowwise_bwd(name, fn, rows, row_specs, vecs, vec_specs, cts, ct_specs, ct_groups,
                 drow_idx, drow_shapes, drow_specs, dvec_idx, dvec_shapes, dvec_specs, n_tiles, riders=()):
    nr, nv, nc = len(rows), len(vecs), len(cts)

    def body(*refs):
        ins = [r[...].astype(F32) for r in refs[:nr + nv]]
        ct_vals = [r[...].astype(F32) for r in refs[nr + nv:nr + nv + nc]]
        out_refs = refs[nr + nv + nc:]
        _, vjp = jax.vjp(fn, *ins)
        grads = vjp(tuple(functools.reduce(lambda p, q: p + q, [ct_vals[j] for j in grp]) for grp in ct_groups))
        for o_ref, idx in zip(out_refs[:len(drow_idx)], drow_idx):
            o_ref[...] = grads[idx].astype(o_ref.dtype)
        step = pl.program_id(0)
        for o_ref, idx in zip(out_refs[len(drow_idx):], dvec_idx):
            @pl.when(step == 0)
            def _(o_ref=o_ref):
                o_ref[...] = jnp.zeros_like(o_ref)
            o_ref[...] += grads[nr + idx]

    outs, lands = _ride(name, (n_tiles,), riders, body, list(row_specs) + list(vec_specs) + list(ct_specs),
                        list(drow_specs) + list(dvec_specs), list(drow_shapes) + list(dvec_shapes), [], _params(1),
                        [*rows, *vecs, *cts])
    return (outs, lands) if riders else outs


def _rms(x):
    return x * lax.rsqrt(jnp.mean(x * x, axis=-1, keepdims=True) + EPS)


def _seg_in(x, shift, scale, gain):
    return _rms(x) * gain * (1.0 + scale) + shift, x


def _seg_qk(q, k, qg, kg):
    def norm(t, g, mult):
        blocks = []
        lane = lax.broadcasted_iota(jnp.int32, (1, LANES), 1)
        for p in range(t.shape[1] // LANES):
            tb = t[:, p * LANES:(p + 1) * LANES]
            sq = tb * tb
            lo = jnp.sum(jnp.where(lane < HEAD_DIM, sq, 0.0), axis=-1, keepdims=True)
            hi = jnp.sum(jnp.where(lane < HEAD_DIM, 0.0, sq), axis=-1, keepdims=True)
            ms = jnp.where(lane < HEAD_DIM, lo, hi) * (1.0 / HEAD_DIM)
            blocks.append(tb * lax.rsqrt(ms + EPS) * (g[:, p * LANES:(p + 1) * LANES] * mult))
        return jnp.concatenate(blocks, axis=-1) if len(blocks) > 1 else blocks[0]
    return norm(q, qg, 1.0 / math.sqrt(HEAD_DIM)), norm(k, kg, 1.0)


def _seg_gelu(ypre):
    return (jax.nn.gelu(ypre),)


def _seg_mix(y1, z, yattn, g_ssm, g_attn):
    ys = y1 * jax.nn.sigmoid(z)
    return (jnp.concatenate([_rms(ys) * g_ssm, _rms(yattn) * g_attn], axis=-1),)


def _seg_mid(x, o, gate_m, g_ffn, scale_f, shift_f):
    h1 = x + gate_m * o
    return h1, _rms(h1) * g_ffn * (1.0 + scale_f) + shift_f


def _seg_act(gate, up):
    return (jax.nn.silu(gate) * up,)


def _s5_lam(a_re, a_im, log_dt):
    dt = jnp.exp(log_dt)
    mag = jnp.exp(a_re * dt)
    lr, li = mag * jnp.cos(a_im * dt), mag * jnp.sin(a_im * dt)
    den = a_re * a_re + a_im * a_im
    nr, ni = lr - 1.0, li
    return lr, li, (nr * a_re + ni * a_im) / den, (ni * a_re - nr * a_im) / den


def _s5_bbar(coef_re, coef_im, b_re, b_im):
    return coef_re * b_re - coef_im * b_im, coef_re * b_im + coef_im * b_re


def _whole(name, fn, ins, out_shapes):
    n = len(ins)

    def body(*refs):
        outs = fn(*[r[...] for r in refs[:n]])
        for o_ref, o in zip(refs[n:], outs):
            o_ref[...] = o

    return pl.pallas_call(body, name=name, out_shape=[jax.ShapeDtypeStruct(s, F32) for s in out_shapes])(*ins)


def _whole_vjp(name, fn, ins, cts, out_shapes):
    n, nc = len(ins), len(cts)

    def body(*refs):
        _, vjp = jax.vjp(fn, *[r[...] for r in refs[:n]])
        grads = vjp(tuple(r[...] for r in refs[n:n + nc]))
        for o_ref, g in zip(refs[n + nc:], grads):
            o_ref[...] = g

    return pl.pallas_call(body, name=name, out_shape=[jax.ShapeDtypeStruct(s, F32) for s in out_shapes])(*ins, *cts)


SCAN_SHIFTS = (1, 2, 4)


def _cmul(ar, ai, br, bi):
    return ar * br - ai * bi, ar * bi + ai * br


def _scan_coefs(lr, li, reverse):
    s = lr.shape[1]
    row = lax.broadcasted_iota(jnp.int32, (SUBLANES, s), 0)
    p1 = (lr, li)
    p2 = _cmul(*p1, *p1)
    p4 = _cmul(*p2, *p2)
    p8 = _cmul(*p4, *p4)
    p3, p5, p6 = _cmul(*p1, *p2), _cmul(*p4, *p1), _cmul(*p4, *p2)
    p7 = _cmul(*p6, *p1)
    pows = (p1, p2, p3, p4, p5, p6, p7, p8)
    bc = lambda t: jnp.broadcast_to(t, (SUBLANES, s))
    steps = []
    for sh, pw in zip(SCAN_SHIFTS, (p1, p2, p4)):
        keep = (row + sh <= SUBLANES - 1) if reverse else (row >= sh)
        steps.append((jnp.where(keep, bc(pw[0]), 0.0), jnp.where(keep, bc(pw[1]), 0.0)))
    cr, ci = jnp.zeros((SUBLANES, s), F32), jnp.zeros((SUBLANES, s), F32)
    for r in range(SUBLANES):
        pw = pows[SUBLANES - 1 - r] if reverse else pows[r]
        cr = jnp.where(row == r, bc(pw[0]), cr)
        ci = jnp.where(row == r, bc(pw[1]), ci)
    return steps, (cr, ci)


def _scan_tile(xr, xi, steps, carry_pow, cr, ci, reverse):
    for sh, (ar, ai) in zip(SCAN_SHIFTS, steps):
        rs = SUBLANES - sh if reverse else sh
        sr, si = pltpu.roll(xr, rs, 0), pltpu.roll(xi, rs, 0)
        xr, xi = xr + ar * sr - ai * si, xi + ar * si + ai * sr
    pr, pi = carry_pow
    return xr + pr * cr - pi * ci, xi + pr * ci + pi * cr


def _s5_forward(proj, b_blk_re, b_blk_im, c_blk_re, c_blk_im, lam_re, lam_im, d_skip, n_blk, t_chunk, riders=()):
    seq = proj.shape[0]
    n_chunks = seq // t_chunk
    n_tiles = t_chunk // SUBLANES
    s = b_blk_re.shape[2]

    def body(u_ref, bre_ref, bim_ref, cre_ref, cim_ref, lr_ref, li_ref, d_ref, y_ref, xr_ref, xi_ref, wr, wi, carry):
        t = pl.program_id(1)

        @pl.when(t == 0)
        def _():
            carry[...] = jnp.zeros_like(carry)

        u = u_ref[...]
        ub = u.astype(BF16)
        wr[...] = _dot(ub, bre_ref[...])
        wi[...] = _dot(ub, bim_ref[...])
        steps, cpow = _scan_coefs(lr_ref[...], li_ref[...], False)

        def tile(i, c):
            r0 = pl.multiple_of(i * SUBLANES, SUBLANES)
            xr, xi = _scan_tile(wr[pl.ds(r0, SUBLANES), :], wi[pl.ds(r0, SUBLANES), :], steps, cpow, c[0], c[1], False)
            xr_ref[pl.ds(r0, SUBLANES), :] = xr
            xi_ref[pl.ds(r0, SUBLANES), :] = xi
            last = SUBLANES - 1
            return (jnp.broadcast_to(xr[last:, :], xr.shape), jnp.broadcast_to(xi[last:, :], xi.shape))

        cr, ci = lax.fori_loop(0, n_tiles, tile, (carry[0], carry[1]))
        carry[0] = cr
        carry[1] = ci
        y = _dot(xr_ref[...].astype(BF16), cre_ref[...]) - _dot(xi_ref[...].astype(BF16), cim_ref[...])
        y_ref[...] = y + d_ref[...] * u

    blk = lambda shape: pl.BlockSpec((None,) + shape, lambda j, t: (j, 0, 0))
    return _ride(
        "s5_fwd", (n_blk, n_chunks), riders, body,
        [pl.BlockSpec((t_chunk, LANES), lambda j, t: (t, j)), blk((LANES, s)), blk((LANES, s)),
         blk((s, LANES)), blk((s, LANES)), blk((1, s)), blk((1, s)), pl.BlockSpec((1, LANES), lambda j, t: (0, j))],
        [pl.BlockSpec((t_chunk, LANES), lambda j, t: (t, j)), pl.BlockSpec((t_chunk, s), lambda j, t: (t, j)),
         pl.BlockSpec((t_chunk, s), lambda j, t: (t, j))],
        [jax.ShapeDtypeStruct((seq, n_blk * LANES), F32), jax.ShapeDtypeStruct((seq, n_blk * s), F32),
         jax.ShapeDtypeStruct((seq, n_blk * s), F32)],
        [pltpu.VMEM((t_chunk, s), F32), pltpu.VMEM((t_chunk, s), F32), pltpu.VMEM((2, SUBLANES, s), F32)],
        _params(2), [proj, b_blk_re, b_blk_im, c_blk_re, c_blk_im, lam_re, lam_im, d_skip])


def _s5_backward(dypre, proj, x_re, x_im, b_blk_re, b_blk_im, c_blk_re, c_blk_im, lam_re, lam_im, d_skip, n_blk, t_chunk,
                 riders=()):
    seq = proj.shape[0]
    n_chunks = seq // t_chunk
    n_tiles = t_chunk // SUBLANES
    s = b_blk_re.shape[2]

    def body(dy_ref, u_ref, xr_ref, xi_ref, pr_ref, pi_ref, bre_ref, bim_ref, cre_ref, cim_ref, lr_ref, li_ref, d_ref,
             du_ref, dbre_ref, dbim_ref, dcre_ref, dcim_ref, dlr_ref, dli_ref, dd_ref, gr, gi, carry):
        t = pl.program_id(1)

        @pl.when(t == 0)
        def _():
            carry[...] = jnp.zeros_like(carry)
            for r in (dbre_ref, dbim_ref, dcre_ref, dcim_ref, dlr_ref, dli_ref, dd_ref):
                r[...] = jnp.zeros_like(r)

        dy = dy_ref[...]
        dyb = dy.astype(BF16)
        u = u_ref[...]
        gr[...] = _dot(dyb, cre_ref[...], NT)
        gi[...] = -_dot(dyb, cim_ref[...], NT)
        steps, cpow = _scan_coefs(lr_ref[...], -li_ref[...], True)
        row = lax.broadcasted_iota(jnp.int32, (SUBLANES, s), 0)
        last = SUBLANES - 1
        first_chunk = t == n_chunks - 1

        def tile_at(r0, prev_r, prev_i, c):
            cr, ci, ar, ai = c
            lr_, li_ = _scan_tile(gr[pl.ds(r0, SUBLANES), :], gi[pl.ds(r0, SUBLANES), :], steps, cpow, cr, ci, True)
            gr[pl.ds(r0, SUBLANES), :] = lr_
            gi[pl.ds(r0, SUBLANES), :] = li_
            xr, xi = xr_ref[pl.ds(r0, SUBLANES), :], xi_ref[pl.ds(r0, SUBLANES), :]
            xpr = jnp.where(row == 0, jnp.broadcast_to(prev_r[last:, :], xr.shape), pltpu.roll(xr, 1, 0))
            xpi = jnp.where(row == 0, jnp.broadcast_to(prev_i[last:, :], xi.shape), pltpu.roll(xi, 1, 0))
            ar = ar + lr_ * xpr + li_ * xpi
            ai = ai + li_ * xpr - lr_ * xpi
            return (jnp.broadcast_to(lr_[:1, :], lr_.shape), jnp.broadcast_to(li_[:1, :], li_.shape), ar, ai)

        def tile(ii, c):
            i = n_tiles - 1 - ii
            r0 = pl.multiple_of(i * SUBLANES, SUBLANES)
            rp = pl.multiple_of(r0 - SUBLANES, SUBLANES)
            return tile_at(r0, xr_ref[pl.ds(rp, SUBLANES), :], xi_ref[pl.ds(rp, SUBLANES), :], c)

        zero = jnp.zeros((SUBLANES, s), F32)
        c = lax.fori_loop(0, n_tiles - 1, tile, (carry[0], carry[1], zero, zero))
        keep = jnp.where(first_chunk, 0.0, 1.0)
        c = tile_at(0, pr_ref[...] * keep, pi_ref[...] * keep, c)
        carry[0] = c[0]
        carry[1] = c[1]
        dlr_ref[...] += jnp.sum(c[2], axis=0, keepdims=True)
        dli_ref[...] += jnp.sum(c[3], axis=0, keepdims=True)

        lam_r, lam_i = gr[...].astype(BF16), gi[...].astype(BF16)
        du_ref[...] = (_dot(lam_r, bre_ref[...], NT) + _dot(lam_i, bim_ref[...], NT) + d_ref[...] * dy).astype(du_ref.dtype)
        ub = u.astype(BF16)
        dbre_ref[...] += _dot(ub, lam_r, TN)
        dbim_ref[...] += _dot(ub, lam_i, TN)
        dcre_ref[...] += _dot(xr_ref[...].astype(BF16), dyb, TN)
        dcim_ref[...] -= _dot(xi_ref[...].astype(BF16), dyb, TN)
        dd_ref[...] += jnp.sum(dy * u, axis=0, keepdims=True)

    rev = lambda t: n_chunks - 1 - t
    blk = lambda shape: pl.BlockSpec((None,) + shape, lambda j, t: (j, 0, 0))
    tpc = t_chunk // SUBLANES
    prev_spec = pl.BlockSpec((SUBLANES, s), lambda j, t: (jnp.maximum(rev(t) * tpc - 1, 0), j))
    chunk = lambda w: pl.BlockSpec((t_chunk, w), lambda j, t: (rev(t), j))
    return _ride(
        "s5_bwd", (n_blk, n_chunks), riders, body,
        [chunk(LANES), chunk(LANES), chunk(s), chunk(s), prev_spec, prev_spec, blk((LANES, s)), blk((LANES, s)),
         blk((s, LANES)), blk((s, LANES)), blk((1, s)), blk((1, s)), pl.BlockSpec((1, LANES), lambda j, t: (0, j))],
        [chunk(LANES), blk((LANES, s)), blk((LANES, s)), blk((s, LANES)), blk((s, LANES)), blk((1, s)), blk((1, s)),
         pl.BlockSpec((1, LANES), lambda j, t: (0, j))],
        [jax.ShapeDtypeStruct((seq, n_blk * LANES), BF16),
         jax.ShapeDtypeStruct((n_blk, LANES, s), F32), jax.ShapeDtypeStruct((n_blk, LANES, s), F32),
         jax.ShapeDtypeStruct((n_blk, s, LANES), F32), jax.ShapeDtypeStruct((n_blk, s, LANES), F32),
         jax.ShapeDtypeStruct((n_blk, 1, s), F32), jax.ShapeDtypeStruct((n_blk, 1, s), F32),
         jax.ShapeDtypeStruct((1, n_blk * LANES), F32)],
        [pltpu.VMEM((t_chunk, s), F32), pltpu.VMEM((t_chunk, s), F32), pltpu.VMEM((2, SUBLANES, s), F32)],
        _params(2), [dypre, proj, x_re, x_im, x_re, x_im, b_blk_re, b_blk_im, c_blk_re, c_blk_im, lam_re, lam_im, d_skip])


TQ, TK = 256, 128


def _split_bf16(x):
    hi = x.astype(BF16)
    return hi, (x - hi.astype(F32)).astype(BF16)


def _sb_weights(z, past, carry, tri):
    ls = jnp.minimum(z, 0.0) - jnp.log(1.0 + jnp.exp(-jnp.abs(z)))
    lk = ls - z
    if past is not None:
        lk = jnp.where(past, lk, 0.0)
    w = jnp.exp(ls + _dot(lk.astype(BF16), tri) + carry)
    if past is not None:
        w = jnp.where(past, w, 0.0)
    return ls, lk, w


LOG_KEEP_DEAD = -104.0


def _walk_key_blocks(i, ratio, prologue, block, epilogue, log_keep):
    n_kb = (i + 1) * ratio
    prologue(n_kb - 1)
    for n in range(ratio):
        block(n_kb - 1 - n, n % 2, True)
    assert ratio % 2 == 0
    n_pairs = (i * ratio) // 2

    def more(state):
        t, alive = state
        return jnp.logical_and(t < n_pairs, alive)

    def pair(state):
        t, _ = state
        j = n_kb - 1 - ratio - 2 * t
        block(j, ratio % 2, False)
        block(j - 1, (ratio + 1) % 2, False)
        return t + 1, log_keep() >= LOG_KEEP_DEAD

    done, _ = lax.while_loop(more, pair, (jnp.int32(0), log_keep() >= LOG_KEEP_DEAD))
    epilogue(n_kb - ratio - 2 * done)


def _attention_forward(qh, kh, proj, v_col, n_pair, tq, tk, riders=()):
    seq = qh.shape[0]
    ratio = tq // tk

    def body(q_ref, k_ref, v_ref, o_ref, q_scr, z_scr, w_scr, acc_scr, c_scr):
        i = pl.program_id(1)
        lane = lax.broadcasted_iota(jnp.int32, (1, LANES), 1)
        q2 = q_ref[...]
        q_scr[0] = jnp.where(lane < HEAD_DIM, q2, 0.0).astype(BF16)
        q_scr[1] = jnp.where(lane < HEAD_DIM, 0.0, q2).astype(BF16)
        tri = (lax.broadcasted_iota(jnp.int32, (tk, tk), 0) > lax.broadcasted_iota(jnp.int32, (tk, tk), 1)).astype(BF16)
        qpos = i * tq + lax.broadcasted_iota(jnp.int32, (tq, tk), 0)
        kidx = lax.broadcasted_iota(jnp.int32, (tq, tk), 1)

        def rows(ref, j):
            j = jnp.clip(j, 0, seq // tk - 1)
            return ref[pl.ds(pl.multiple_of(j * tk, tk), tk), :].astype(BF16)

        def scores(j, slot):
            kb = rows(k_ref, j)
            for h in range(2):
                z_scr[slot, h] = _dot(q_scr[h], kb, NT)

        def finish(j):
            vb = rows(v_ref, j)
            for h in range(2):
                acc_scr[h] += _dot(w_scr[h], vb)

        def prologue(j):
            w_scr[...] = jnp.zeros_like(w_scr)
            acc_scr[...] = jnp.zeros_like(acc_scr)
            c_scr[...] = jnp.zeros_like(c_scr)
            scores(j, 0)

        def block(j, slot, masked):
            scores(j - 1, 1 - slot)
            finish(j + 1)
            past = ((kidx + j * tk) < qpos) if masked else None
            for h in range(2):
                _, lk, w = _sb_weights(z_scr[slot, h], past, c_scr[h], tri)
                w_scr[h] = w.astype(BF16)
                c_scr[h] += jnp.sum(lk, axis=-1, keepdims=True)

        _walk_key_blocks(i, ratio, prologue, block, finish, lambda: jnp.max(c_scr[...]))
        o_ref[...] = jnp.where(lane < HEAD_DIM, acc_scr[0], acc_scr[1])

    (out,), lands = _ride(
        "attn_fwd", (n_pair, seq // tq), riders, body,
        [pl.BlockSpec((tq, LANES), lambda p, i: (i, p)), pl.BlockSpec((seq, LANES), lambda p, i: (0, p)),
         pl.BlockSpec((seq, LANES), lambda p, i: (0, v_col + p))],
        [pl.BlockSpec((tq, LANES), lambda p, i: (i, p))], [jax.ShapeDtypeStruct(qh.shape, F32)],
        [pltpu.VMEM((2, tq, LANES), BF16), pltpu.VMEM((2, 2, tq, tk), F32), pltpu.VMEM((2, tq, tk), BF16),
         pltpu.VMEM((2, tq, LANES), F32), pltpu.VMEM((2, tq, 1), F32)],
        _params(2), [qh, kh, proj])
    return out, lands


def _attention_backward(qh, kh, proj, v_col, y, dy, n_pair, tq, tk, riders=()):
    seq = qh.shape[0]

    ratio = tq // tk

    n_kblk = seq // tk

    def body(q_ref, k_ref, v_ref, y_ref, dy_ref, dq_ref, dk_ref, dv_ref,
             q_scr, do_scr, qt_scr, dot_scr, dkt_scr, dvt_scr, z_scr, dw_scr, w_scr, dz_scr, dq_scr, c_scr, c2_scr, tot_scr):
        i = pl.program_id(1)

        @pl.when(i == 0)
        def _():
            dkt_scr[...] = jnp.zeros_like(dkt_scr)
            dvt_scr[...] = jnp.zeros_like(dvt_scr)

        lane = lax.broadcasted_iota(jnp.int32, (1, LANES), 1)
        sel = (lane < HEAD_DIM, lane >= HEAD_DIM)
        q2, do2 = q_ref[...], dy_ref[...].astype(BF16)
        do2f = do2.astype(F32)
        dot_oy = do2f * y_ref[...]
        for h in range(2):
            qm, dm = jnp.where(sel[h], q2, 0.0), jnp.where(sel[h], do2f, 0.0)
            q_scr[h] = qm.astype(BF16)
            do_scr[h] = dm.astype(BF16)
            qt_scr[h] = qm.T.astype(BF16)
            dot_scr[h] = dm.T.astype(BF16)
            tot_scr[h] = jnp.sum(jnp.where(sel[h], dot_oy, 0.0), axis=-1, keepdims=True)
        r_i, c_i = lax.broadcasted_iota(jnp.int32, (tk, tk), 0), lax.broadcasted_iota(jnp.int32, (tk, tk), 1)
        tri = (r_i > c_i).astype(BF16)
        tri_ge = (r_i >= c_i).astype(BF16)
        qpos = i * tq + lax.broadcasted_iota(jnp.int32, (tq, tk), 0)
        kidx = lax.broadcasted_iota(jnp.int32, (tq, tk), 1)

        def start(j):
            return pl.multiple_of(jnp.clip(j, 0, seq // tk - 1) * tk, tk)

        def scores(j, slot):
            c0 = start(j)
            kb, vb = k_ref[pl.ds(c0, tk), :].astype(BF16), v_ref[pl.ds(c0, tk), :].astype(BF16)
            for h in range(2):
                z_scr[slot, h] = _dot(q_scr[h], kb, NT)
                dw_scr[slot, h] = _dot(do_scr[h], vb, NT)

        def finish(j):
            jc = jnp.clip(j, 0, n_kblk - 1)
            kb = k_ref[pl.ds(pl.multiple_of(jc * tk, tk), tk), :].astype(BF16)
            dkt_add, dvt_add = jnp.zeros((LANES, tk), F32), jnp.zeros((LANES, tk), F32)
            for h in range(2):
                dz = dz_scr[h]
                dq_scr[h] += _dot(dz, kb)
                dkt_add = dkt_add + _dot(qt_scr[h], dz)
                dvt_add = dvt_add + _dot(dot_scr[h], w_scr[h])
            dkt_scr[jc] += dkt_add
            dvt_scr[jc] += dvt_add

        def prologue(j):
            for r in (w_scr, dz_scr, dq_scr, c_scr, c2_scr):
                r[...] = jnp.zeros_like(r)
            scores(j, 0)

        def block(j, slot, masked):
            scores(j - 1, 1 - slot)
            finish(j + 1)
            past = ((kidx + j * tk) < qpos) if masked else None
            for h in range(2):
                ls, lk, w = _sb_weights(z_scr[slot, h], past, c_scr[h], tri)
                wb = w.astype(BF16)
                dlw = dw_scr[slot, h] * wb.astype(F32)
                hi, lo = _split_bf16(dlw)
                dlk = tot_scr[h] - c2_scr[h] - (_dot(hi, tri_ge) + _dot(lo, tri_ge))
                if masked:
                    dlk = jnp.where(past, dlk, 0.0)
                sig = jnp.exp(ls)
                w_scr[h] = wb
                dz_scr[h] = (dlw * (1.0 - sig) - dlk * sig).astype(BF16)
                c_scr[h] += jnp.sum(lk, axis=-1, keepdims=True)
                c2_scr[h] += jnp.sum(dlw, axis=-1, keepdims=True)

        _walk_key_blocks(i, ratio, prologue, block, finish, lambda: jnp.max(c_scr[...]))
        dq_ref[...] = jnp.where(sel[0], dq_scr[0], dq_scr[1])

        @pl.when(i == seq // tq - 1)
        def _():
            for jb in range(n_kblk):
                dk_ref[jb * tk:(jb + 1) * tk, :] = dkt_scr[jb].T
                dv_ref[jb * tk:(jb + 1) * tk, :] = dvt_scr[jb].T.astype(dv_ref.dtype)

    blk = pl.BlockSpec((tq, LANES), lambda p, i: (i, p))
    full = pl.BlockSpec((seq, LANES), lambda p, i: (0, p))
    shape = jax.ShapeDtypeStruct(qh.shape, F32)
    return _ride(
        "attn_bwd", (n_pair, seq // tq), riders, body,
        [blk, full, pl.BlockSpec((seq, LANES), lambda p, i: (0, v_col + p)), blk, blk],
        [blk, full, full], [shape, shape, jax.ShapeDtypeStruct(qh.shape, BF16)],
        [pltpu.VMEM((2, tq, LANES), BF16), pltpu.VMEM((2, tq, LANES), BF16),
         pltpu.VMEM((2, LANES, tq), BF16), pltpu.VMEM((2, LANES, tq), BF16),
         pltpu.VMEM((n_kblk, LANES, tk), F32), pltpu.VMEM((n_kblk, LANES, tk), F32),
         pltpu.VMEM((2, 2, tq, tk), F32), pltpu.VMEM((2, 2, tq, tk), F32),
         pltpu.VMEM((2, tq, tk), BF16), pltpu.VMEM((2, tq, tk), BF16), pltpu.VMEM((2, tq, LANES), F32),
         pltpu.VMEM((2, tq, 1), F32), pltpu.VMEM((2, tq, 1), F32), pltpu.VMEM((2, tq, 1), F32)],
        _params(2), [qh, kh, proj, y, dy])


def _loss_head(h1, ffn, target, gate_f, tile):
    seq, d = h1.shape

    def body(h_ref, f_ref, t_ref, g_ref, dy_ref, df_ref, dg_ref, loss_ref):
        @pl.when(pl.program_id(0) == 0)
        def _():
            dg_ref[...] = jnp.zeros_like(dg_ref)
            loss_ref[...] = jnp.zeros_like(loss_ref)

        f, g = f_ref[...], g_ref[...]
        err = h_ref[...] + g * f - t_ref[...]
        dy = err * (1.0 / d)
        dy_ref[...] = dy
        df_ref[...] = (dy * g).astype(df_ref.dtype)
        dg_ref[...] += jnp.sum(dy * f, axis=0, keepdims=True)
        loss_ref[...] += jnp.sum(jnp.sum(err * err, axis=-1, keepdims=True), axis=0, keepdims=True) * (0.5 / d)

    row = _row_spec(tile, d)
    return pl.pallas_call(
        body, name="loss_head", grid=(seq // tile,), in_specs=[row, row, row, _vec_spec(d)],
        out_specs=[row, row, _vec_spec(d), pl.BlockSpec((1, 1), lambda i: (0, 0))],
        out_shape=[jax.ShapeDtypeStruct((seq, d), F32), jax.ShapeDtypeStruct((seq, d), BF16),
                   jax.ShapeDtypeStruct((1, d), F32), jax.ShapeDtypeStruct((1, 1), F32)],
        compiler_params=_params(1),
    )(h1, ffn, target, gate_f)


def _dot3(a, b, dn):
    ah, al = _split_bf16(a)
    bh, bl = _split_bf16(b)
    return _dot(ah, bh, dn) + (_dot(ah, bl, dn) + _dot(al, bh, dn))


def _ada_forward(c_all, w_shard, b_cols):
    d, n = w_shard.shape
    bk = _tile(d, 512)

    def body(c_ref, w_ref, b_ref, o_ref):
        @pl.when(pl.program_id(0) == 0)
        def _():
            o_ref[...] = jnp.broadcast_to(b_ref[...], o_ref.shape)

        o_ref[...] += _dot3(jax.nn.silu(c_ref[...]), w_ref[...], NN)

    return pl.pallas_call(
        body, name="ada_fwd", grid=(d // bk,),
        in_specs=[pl.BlockSpec((NDEV, bk), lambda k: (0, k)), pl.BlockSpec((bk, n), lambda k: (k, 0)), _vec_spec(n)],
        out_specs=pl.BlockSpec((NDEV, n), lambda k: (0, 0)), out_shape=jax.ShapeDtypeStruct((NDEV, n), F32),
        compiler_params=_params(1),
    )(c_all, w_shard, b_cols)


def _adam(w, g, m, v):
    m = ADAM_B1 * m + (1.0 - ADAM_B1) * g
    v = ADAM_B2 * v + (1.0 - ADAM_B2) * (g * g)
    m_hat = m / (1.0 - ADAM_B1 ** ADAM_STEP)
    v_hat = v / (1.0 - ADAM_B2 ** ADAM_STEP)
    return -ADAM_LR * (m_hat / (jnp.sqrt(v_hat) + ADAM_EPS) + ADAM_WD * w), m, v


def _adam_ada(c_all, dmod_cols, w, m, v):
    d, n = w.shape
    tr = _tile(d, 256)

    def body(c_ref, dm_ref, w_ref, m_ref, v_ref, g_ref, dl_ref, nm_ref, nv_ref):
        g = _dot3(jax.nn.silu(c_ref[...]), dm_ref[...], TN)
        delta, nm, nv = _adam(w_ref[...], g, m_ref[...], v_ref[...])
        g_ref[...] = g
        dl_ref[...] = delta
        nm_ref[...] = nm
        nv_ref[...] = nv

    row = _row_spec(tr, n)
    return pl.pallas_call(
        body, name="adam_ada", grid=(d // tr,),
        in_specs=[pl.BlockSpec((NDEV, tr), lambda i: (0, i)), pl.BlockSpec((NDEV, n), lambda i: (0, 0)), row, row, row],
        out_specs=[row] * 4, out_shape=[jax.ShapeDtypeStruct((d, n), F32)] * 4, compiler_params=_params(1),
    )(c_all, dmod_cols, w, m, v)


def _adam_sum(name, parts, part_spec, w, m, v, tr):
    r, c = w.shape

    def body(p_ref, w_ref, m_ref, v_ref, g_ref, dl_ref, nm_ref, nv_ref):
        g = p_ref[0].astype(F32)
        for k in range(1, NDEV):
            g = g + p_ref[k].astype(F32)
        delta, nm, nv = _adam(w_ref[...], g, m_ref[...], v_ref[...])
        g_ref[...] = g
        dl_ref[...] = delta
        nm_ref[...] = nm
        nv_ref[...] = nv

    row = _row_spec(tr, c)
    return pl.pallas_call(
        body, name=name, grid=(r // tr,), in_specs=[part_spec, row, row, row],
        out_specs=[row] * 4, out_shape=[jax.ShapeDtypeStruct((r, c), F32)] * 4, compiler_params=_params(1),
    )(parts, w, m, v)


GROUPS_PER_BLOCK = LANES // SSM_GROUP


def _to_b_blocks(bb, n_blk, p):
    t = bb.reshape(n_blk, GROUPS_PER_BLOCK, p, SSM_GROUP)
    eye = jnp.eye(GROUPS_PER_BLOCK, dtype=bb.dtype)
    return jnp.einsum("jgph,gk->jghkp", t, eye).reshape(n_blk, LANES, GROUPS_PER_BLOCK * p)


def _from_b_blocks(blk, n_blk, p):
    t = blk.reshape(n_blk, GROUPS_PER_BLOCK, SSM_GROUP, GROUPS_PER_BLOCK, p)
    eye = jnp.eye(GROUPS_PER_BLOCK, dtype=blk.dtype)
    return jnp.einsum("jghkp,gk->jgph", t, eye).reshape(n_blk * GROUPS_PER_BLOCK, p, SSM_GROUP)


def _to_c_blocks(cc, n_blk, p):
    t = cc.reshape(n_blk, GROUPS_PER_BLOCK, SSM_GROUP, p)
    eye = jnp.eye(GROUPS_PER_BLOCK, dtype=cc.dtype)
    return jnp.einsum("jghp,gk->jgpkh", t, eye).reshape(n_blk, GROUPS_PER_BLOCK * p, LANES)


def _from_c_blocks(blk, n_blk, p):
    t = blk.reshape(n_blk, GROUPS_PER_BLOCK, p, GROUPS_PER_BLOCK, SSM_GROUP)
    eye = jnp.eye(GROUPS_PER_BLOCK, dtype=blk.dtype)
    return jnp.einsum("jgpkh,gk->jghp", t, eye).reshape(n_blk * GROUPS_PER_BLOCK, SSM_GROUP, p)


SMALL_LATE = ("b_ada_a", "g_mix")
SMALL_EARLY = ("b_ada_b", "a_re", "a_im", "log_dt", "b_re", "b_im", "c_re", "c_im", "d_skip",
               "q_gain", "k_gain", "g_ssm_out", "g_attn_out", "g_ffn")
PACK_COLS = 1024


def _pack(arrs):
    flat = jnp.concatenate([a.reshape(-1) for a in arrs])
    n = flat.shape[0]
    quantum = SUBLANES * PACK_COLS
    padded = -(-n // quantum) * quantum
    return jnp.pad(flat, (0, padded - n)).reshape(padded // PACK_COLS, PACK_COLS)


def _unpack(packed, like):
    flat, out, off = packed.reshape(-1), [], 0
    for a in like:
        out.append(flat[off:off + a.size].reshape(a.shape))
        off += a.size
    return out


def kernel(x, c, w_ada, b_ada, g_mix, w_in, a_re, a_im, log_dt, b_re, b_im, c_re, c_im, d_skip, w_glu, q_gain, k_gain, g_ssm_out, g_attn_out, w_out, g_ffn, w_gate, w_up, w_down, loss_target, m_w_ada, m_b_ada, m_g_mix, m_w_in, m_a_re, m_a_im, m_log_dt, m_b_re, m_b_im, m_c_re, m_c_im, m_d_skip, m_w_glu, m_q_gain, m_k_gain, m_g_ssm_out, m_g_attn_out, m_w_out, m_g_ffn, m_w_gate, m_w_up, m_w_down, v_w_ada, v_b_ada, v_g_mix, v_w_in, v_a_re, v_a_im, v_log_dt, v_b_re, v_b_im, v_c_re, v_c_im, v_d_skip, v_w_glu, v_q_gain, v_k_gain, v_g_ssm_out, v_g_attn_out, v_w_out, v_g_ffn, v_w_gate, v_w_up, v_w_down):
    given = dict(locals())
    seq, d = x.shape[1], x.shape[2]
    xs, tgt = x[0], loss_target[0]
    n_groups, p_state = a_re.shape[1], a_re.shape[2]
    w_ssm = n_groups * SSM_GROUP
    w_attn = w_in.shape[2] * NDEV - w_ssm
    w_attn //= 3
    n_blk, n_pair = w_ssm // LANES, w_attn // LANES
    n_heads = w_attn // HEAD_DIM
    ns_in, ns_ff = w_in.shape[2], w_gate.shape[2]
    d_mix = w_ssm + w_attn
    mx, my, mc = _me()
    me = 4 * mx + 2 * my + mc
    rt = _tile(seq, 256)
    n_rt = seq // rt
    sds = jax.ShapeDtypeStruct

    c_all = _exchange(c, True, "comm_ag_c").reshape(NDEV, d)
    n_ada = w_ada.shape[2]
    b_cols = lax.dynamic_slice(b_ada, (0, me * n_ada), (1, n_ada))
    mod_cols = _ada_forward(c_all, w_ada[0], b_cols)
    mod_all = _exchange(mod_cols, True, "comm_ag_mod")
    mod = lax.dynamic_slice(mod_all, (0, me, 0), (NDEV, 1, n_ada)).reshape(1, NDEV * n_ada)
    shift_m, scale_m, gate_m, shift_f, scale_f, gate_f = [mod[:, i * d:(i + 1) * d] for i in range(6)]

    gp = n_groups * p_state
    a_re2, a_im2, ldt2 = a_re[0], a_im[0], log_dt[0].reshape(n_groups, 1)
    b_re2, b_im2 = b_re[0].reshape(gp, SSM_GROUP), b_im[0].reshape(gp, SSM_GROUP)
    lam_r, lam_i, coef_r, coef_i = _whole("s5_lam", _s5_lam, [a_re2, a_im2, ldt2], [(n_groups, p_state)] * 4)
    coef_r2, coef_i2 = coef_r.reshape(gp, 1), coef_i.reshape(gp, 1)
    bb_r, bb_i = _whole("s5_bbar", _s5_bbar, [coef_r2, coef_i2, b_re2, b_im2], [(gp, SSM_GROUP)] * 2)
    s_blk = GROUPS_PER_BLOCK * p_state
    b_blk_r = _to_b_blocks(bb_r.reshape(n_groups, p_state, SSM_GROUP), n_blk, p_state).astype(BF16)
    b_blk_i = _to_b_blocks(bb_i.reshape(n_groups, p_state, SSM_GROUP), n_blk, p_state).astype(BF16)
    c_blk_r = _to_c_blocks(c_re[0], n_blk, p_state).astype(BF16)
    c_blk_i = _to_c_blocks(c_im[0], n_blk, p_state).astype(BF16)
    lam_r3, lam_i3 = lam_r.reshape(n_blk, 1, s_blk), lam_i.reshape(n_blk, 1, s_blk)
    d_skip2 = d_skip[0].reshape(1, w_ssm)

    row_d, vec_d = _row_spec(rt, d), _vec_spec(d)
    (xm,), (w_in_g,) = _rowwise_fwd("seg_in", lambda *a: _seg_in(*a)[:1], [xs], [row_d], [shift_m, scale_m, g_mix], [vec_d] * 3,
                                    [sds((seq, d), BF16)], [row_d], n_rt,
                                    riders=[_Rider(w_in[0].astype(BF16), True, relayed=True)])
    bn_in = _tile(ns_in, 512)
    per = ns_in // bn_in
    bm, bk = _tile(seq, BM), _tile(d, BK)
    proj, w_glu_g, w_out_g = _mm(
        "mm_in", xm, w_in_g, NN, (seq // bm, NDEV * per, d // bk),
        pl.BlockSpec((bm, bk), lambda i, j, k: (i, k)),
        pl.BlockSpec((None, bk, bn_in), lambda i, j, k: (j // per, k, j % per)),
        pl.BlockSpec((bm, bn_in), lambda i, j, k: (i, j)), (seq, NDEV * ns_in), F32, (bm, bn_in),
        riders=[_Rider(w_glu[0].astype(BF16), True, relayed=True), _Rider(w_out[0].astype(BF16), True, relayed=True)])
    w_glu_g, w_out_g = w_glu_g.reshape(w_ssm, w_ssm), w_out_g.reshape(d_mix, d)
    q_col, k_col, v_col = w_ssm // w_attn, w_ssm // w_attn + 1, (w_ssm + 2 * w_attn) // LANES
    qg_t, kg_t = jnp.tile(q_gain, (1, n_heads)), jnp.tile(k_gain, (1, n_heads))
    row_a, vec_a = _row_spec(rt, w_attn), _vec_spec(w_attn)
    qk_rows, qk_specs = [proj, proj], [_row_spec(rt, w_attn, q_col), _row_spec(rt, w_attn, k_col)]
    qh, kh = _rowwise_fwd("seg_qk", _seg_qk, qk_rows, qk_specs, [qg_t, kg_t], [vec_a] * 2,
                          [sds((seq, w_attn), F32)] * 2, [row_a] * 2, n_rt)
    t_chunk = _tile(seq, 1024)
    (ypre, x_re, x_im), (w_gu_land,) = _s5_forward(
        proj, b_blk_r, b_blk_i, c_blk_r, c_blk_i, lam_r3, lam_i3, d_skip2, n_blk, t_chunk,
        riders=[_Rider(w_up[0].astype(BF16), True, slot=1, n_slots=2, relayed=True)])
    tq, tk = _tile(seq, TQ), _tile(seq, TK)
    y_attn, (w_gu_land,) = _attention_forward(qh, kh, proj, v_col, n_pair, tq, tk,
                                              riders=[_Rider(w_gate[0].astype(BF16), True, land=w_gu_land, slot=0,
                                                             relayed=True)])
    w_gu_g = w_gu_land.reshape(2 * NDEV, d, ns_ff)
    row_s, vec_s = _row_spec(rt, w_ssm), _vec_spec(w_ssm)
    y1, = _rowwise_fwd("seg_gelu", _seg_gelu, [ypre], [row_s], [], [], [sds((seq, w_ssm), F32)], [row_s], n_rt)
    z = _mm_plain("mm_glu", y1, w_glu_g, NN, F32)
    row_m = _row_spec(rt, d_mix)
    mixed, = _rowwise_fwd("seg_mix", _seg_mix, [y1, z, y_attn], [row_s, row_s, row_a], [g_ssm_out, g_attn_out], [vec_s, vec_a],
                          [sds((seq, d_mix), BF16)], [row_m], n_rt)
    o = _mm_plain("mm_out", mixed, w_out_g, NN, F32)
    h1, xf = _rowwise_fwd("seg_mid", _seg_mid, [xs, o], [row_d] * 2, [gate_m, g_ffn, scale_f, shift_f], [vec_d] * 4,
                          [sds((seq, d), F32), sds((seq, d), BF16)], [row_d] * 2, n_rt)
    gu, w_down_g = _mm(
        "mm_gu", xf, w_gu_g, NN, (seq // bm, 2 * NDEV, d // bk),
        pl.BlockSpec((bm, bk), lambda i, j, k: (i, k)), pl.BlockSpec((None, bk, ns_ff), lambda i, j, k: (j, k, 0)),
        pl.BlockSpec((None, bm, ns_ff), lambda i, j, k: (j, i, 0)), (2 * NDEV, seq, ns_ff), BF16, (bm, ns_ff),
        riders=[_Rider(w_down[0].astype(BF16), True, relayed=True)])
    gu4 = gu.reshape(2, NDEV, seq, ns_ff)
    ft = _tile(seq, 512)
    pair_spec = pl.BlockSpec((2, None, ft, ns_ff), lambda s, i: (0, s, i, 0))
    one_spec = pl.BlockSpec((None, ft, ns_ff), lambda s, i: (s, i, 0))

    def act_body(gu_ref, a_ref):
        a_ref[...] = _seg_act(gu_ref[0].astype(F32), gu_ref[1].astype(F32))[0].astype(a_ref.dtype)

    act = pl.pallas_call(act_body, name="seg_act", grid=(NDEV, seq // ft), in_specs=[pair_spec], out_specs=one_spec,
                         out_shape=sds((NDEV, seq, ns_ff), BF16), compiler_params=_params(2))(gu4)
    bn_d = _tile(d, BN)
    bm_h = _tile(seq, BM // 2)
    shard_pieces = lambda n: (lambda a_ref, b_ref: [(a_ref[g], b_ref[g]) for g in range(n)])
    ffn = _mm("mm_down", act, w_down_g, NN, (seq // bm_h, d // bn_d, 1),
              pl.BlockSpec((NDEV, bm_h, ns_ff), lambda i, j, k: (0, i, 0)), pl.BlockSpec((NDEV, ns_ff, bn_d), lambda i, j, k: (0, 0, j)),
              pl.BlockSpec((bm_h, bn_d), lambda i, j, k: (i, j)), (seq, d), F32, (bm_h, bn_d), pieces=shard_pieces(NDEV))
    dy, dffn, d_gate_f, loss_part = _loss_head(h1, ffn, tgt, gate_f, rt)
    loss = lax.psum(loss_part[0, 0], MESH_AXES)

    bl = _tile(seq, BK)
    gw_down = _mm("mm_dw_down", act, dffn, TN, (NDEV, d // bn_d, seq // bl),
                  pl.BlockSpec((None, bl, ns_ff), lambda i, j, k: (i, k, 0)), pl.BlockSpec((bl, bn_d), lambda i, j, k: (k, j)),
                  pl.BlockSpec((None, ns_ff, bn_d), lambda i, j, k: (i, 0, j)), (NDEV, ns_ff, d), BF16, (ns_ff, bn_d))
    rows_down = (ns_ff // 32) * 16
    dact, got_down = _mm(
        "mm_dact", dffn, w_down_g, NT, (seq // bm, NDEV, d // bk),
        pl.BlockSpec((bm, bk), lambda i, j, k: (i, k)), pl.BlockSpec((None, ns_ff, bk), lambda i, j, k: (j, 0, k)),
        pl.BlockSpec((None, bm, ns_ff), lambda i, j, k: (j, i, 0)), (NDEV, seq, ns_ff), BF16, (bm, ns_ff),
        riders=[_Rider(gw_down, False, part=(0, rows_down))])

    def dact_body(gu_ref, da_ref, dgu_ref):
        _, vjp = jax.vjp(_seg_act, gu_ref[0].astype(F32), gu_ref[1].astype(F32))
        dg, du_ = vjp((da_ref[...].astype(F32),))
        dgu_ref[0] = dg.astype(dgu_ref.dtype)
        dgu_ref[1] = du_.astype(dgu_ref.dtype)

    dgu4 = pl.pallas_call(dact_body, name="seg_act_bwd", grid=(NDEV, seq // ft), in_specs=[pair_spec, one_spec],
                          out_specs=pair_spec, out_shape=sds((2, NDEV, seq, ns_ff), BF16), compiler_params=_params(2))(gu4, dact)
    dgu = dgu4.reshape(2 * NDEV, seq, ns_ff)
    bmd = _tile(d, BM)

    def dw_half(name, which, riders):
        return _mm(name, xf, dgu, TN, (d // bmd, NDEV, seq // bl), pl.BlockSpec((bl, bmd), lambda i, j, k: (k, i)),
                   pl.BlockSpec((None, bl, ns_ff), lambda i, j, k: (which * NDEV + j, k, 0)),
                   pl.BlockSpec((None, bmd, ns_ff), lambda i, j, k: (j, i, 0)), (NDEV, d, ns_ff), BF16, (bmd, ns_ff), riders=riders)

    gw_gate, got_down = dw_half("mm_dw_gate", 0, [_Rider(gw_down, False, land=got_down, part=(rows_down, ns_ff - rows_down))])
    gw_up = dw_half("mm_dw_up", 1, ())
    dxf = _mm(
        "mm_dxf", dgu, w_gu_g, NT, (seq // bm, d // bn_d, 4),
        pl.BlockSpec((4, bm, ns_ff), lambda i, j, k: (k, i, 0)), pl.BlockSpec((4, bn_d, ns_ff), lambda i, j, k: (k, j, 0)),
        pl.BlockSpec((bm, bn_d), lambda i, j, k: (i, j)), (seq, d), F32, (bm, bn_d), pieces=shard_pieces(4))
    (do, dx_a, d_gate_m, d_g_ffn, d_scale_f, d_shift_f) = _rowwise_bwd(
        "seg_mid_bwd", _seg_mid, [xs, o], [row_d] * 2, [gate_m, g_ffn, scale_f, shift_f], [vec_d] * 4,
        [dy, dxf], [row_d] * 2, [[0], [1]], [1, 0], [sds((seq, d), BF16), sds((seq, d), F32)], [row_d] * 2,
        [0, 1, 2, 3], [sds((1, d), F32)] * 4, [vec_d] * 4, n_rt)

    dmixed = _mm_plain("mm_dmixed", do, w_out_g, NT, F32)
    gw_out = _mm_plain("mm_dw_out", mixed, do, TN, BF16)
    (dz, dy1_a, dy_attn, d_g_ssm, d_g_attn) = _rowwise_bwd(
        "seg_mix_bwd", _seg_mix, [y1, z, y_attn], [row_s, row_s, row_a], [g_ssm_out, g_attn_out], [vec_s, vec_a],
        [dmixed], [row_m], [[0]], [1, 0, 2], [sds((seq, w_ssm), BF16), sds((seq, w_ssm), F32), sds((seq, w_attn), F32)],
        [row_s, row_s, row_a], [0, 1], [sds((1, w_ssm), F32), sds((1, w_attn), F32)], [vec_s, vec_a], n_rt)
    dy1_b = _mm_plain("mm_dy1", dz, w_glu_g, NT, F32)
    gw_glu = _mm_plain("mm_dw_glu", y1, dz, TN, BF16)
    (dypre,) = _rowwise_bwd("seg_gelu_bwd", _seg_gelu, [ypre], [row_s], [], [], [dy1_a, dy1_b], [row_s] * 2, [[0, 1]],
                            [0], [sds((seq, w_ssm), F32)], [row_s], [], [], [], n_rt)
    (du, db_blk_r, db_blk_i, dc_blk_r, dc_blk_i, dlam_r3, dlam_i3, dd_skip2), (got_gate,) = _s5_backward(
        dypre, proj, x_re, x_im, b_blk_r, b_blk_i, c_blk_r, c_blk_i, lam_r3, lam_i3, d_skip2, n_blk, t_chunk,
        riders=[_Rider(gw_gate, False)])
    (dqh, dkh, dv), (got_up, got_out, got_glu) = _attention_backward(
        qh, kh, proj, v_col, y_attn, dy_attn, n_pair, tq, tk,
        riders=[_Rider(gw_up, False), _Rider(gw_out.reshape(NDEV, w_out.shape[1], d), False),
                _Rider(gw_glu.reshape(NDEV, w_glu.shape[1], w_ssm), False)])
    (dq, dk, dqg_t, dkg_t) = _rowwise_bwd(
        "seg_qk_bwd", _seg_qk, qk_rows, qk_specs, [qg_t, kg_t], [vec_a] * 2, [dqh, dkh], [row_a] * 2, [[0], [1]],
        [0, 1], [sds((seq, w_attn), BF16)] * 2, [row_a] * 2, [0, 1], [sds((1, w_attn), F32)] * 2, [vec_a] * 2, n_rt)

    dbb_r = _from_b_blocks(db_blk_r, n_blk, p_state).reshape(gp, SSM_GROUP)
    dbb_i = _from_b_blocks(db_blk_i, n_blk, p_state).reshape(gp, SSM_GROUP)
    dcoef_r2, dcoef_i2, db_re2, db_im2 = _whole_vjp("s5_bbar_bwd", _s5_bbar, [coef_r2, coef_i2, b_re2, b_im2], [dbb_r, dbb_i],
                                                    [(gp, 1), (gp, 1), (gp, SSM_GROUP), (gp, SSM_GROUP)])
    lam_cts = [dlam_r3.reshape(n_groups, p_state), dlam_i3.reshape(n_groups, p_state),
               dcoef_r2.reshape(n_groups, p_state), dcoef_i2.reshape(n_groups, p_state)]
    da_re2, da_im2, dldt2 = _whole_vjp("s5_lam_bwd", _s5_lam, [a_re2, a_im2, ldt2], lam_cts,
                                       [(n_groups, p_state), (n_groups, p_state), (n_groups, 1)])
    dc_re2, dc_im2 = _from_c_blocks(dc_blk_r, n_blk, p_state), _from_c_blocks(dc_blk_i, n_blk, p_state)

    small_part = {
        "b_ada_b": jnp.concatenate([d_gate_m, d_shift_f, d_scale_f, d_gate_f], axis=-1),
        "a_re": da_re2, "a_im": da_im2, "log_dt": dldt2, "b_re": db_re2, "b_im": db_im2,
        "c_re": dc_re2, "c_im": dc_im2, "d_skip": dd_skip2,
        "q_gain": dqg_t.reshape(n_heads, HEAD_DIM).sum(0), "k_gain": dkg_t.reshape(n_heads, HEAD_DIM).sum(0),
        "g_ssm_out": d_g_ssm, "g_attn_out": d_g_attn, "g_ffn": d_g_ffn,
    }
    dproj = jnp.concatenate([du, dq, dk, dv], axis=-1)
    gw_in, early_parts = _mm(
        "mm_dw_in", xm, dproj, TN, (d // bmd, NDEV * per, seq // bl),
        pl.BlockSpec((bl, bmd), lambda i, j, k: (k, i)), pl.BlockSpec((bl, bn_in), lambda i, j, k: (k, j)),
        pl.BlockSpec((None, bmd, bn_in), lambda i, j, k: (j // per, i, j % per)), (NDEV, d, ns_in), BF16, (bmd, bn_in),
        riders=[_Rider(_pack([small_part[n] for n in SMALL_EARLY]), True, relayed=True)])
    rows_first = (9 * d) // 16
    dxm, got_in = _mm(
        "mm_dxm", dproj, w_in_g, NT, (seq // bm, d // bn_d, 1),
        pl.BlockSpec((bm, NDEV * ns_in), lambda i, j, k: (i, 0)), pl.BlockSpec((NDEV, bn_d, ns_in), lambda i, j, k: (0, j, 0)),
        pl.BlockSpec((bm, bn_d), lambda i, j, k: (i, j)), (seq, d), F32, (bm, bn_d),
        riders=[_Rider(gw_in, False, part=(0, rows_first))],
        pieces=lambda a_ref, b_ref: [(a_ref[:, g * ns_in:(g + 1) * ns_in], b_ref[g]) for g in range(NDEV)])
    (grad_x, d_shift_m, d_scale_m, d_g_mix), (got_in,) = _rowwise_bwd(
        "seg_in_bwd", _seg_in, [xs], [row_d], [shift_m, scale_m, g_mix], [vec_d] * 3, [dxm, dx_a], [row_d] * 2, [[0], [1]],
        [0], [sds((seq, d), F32)], [row_d], [0, 1, 2], [sds((1, d), F32)] * 3, [vec_d] * 3, n_rt,
        riders=[_Rider(gw_in, False, land=got_in, part=(rows_first, d - rows_first))])
    small_part["b_ada_a"] = jnp.concatenate([d_shift_m, d_scale_m], axis=-1)
    small_part["g_mix"] = d_g_mix
    late_parts = _exchange(_pack([small_part[n] for n in SMALL_LATE]), True, "comm_ag_small_late")
    packed_parts = jnp.concatenate([late_parts, early_parts], axis=1)

    big = {}

    def sharded(nm, got, width, tr):
        big[nm] = _adam_sum("adam_" + nm, got, pl.BlockSpec((NDEV, tr, width), lambda i: (0, i, 0)), given[nm][0],
                            given["m_" + nm][0], given["v_" + nm][0], tr)

    sharded("w_down", got_down, d, _tile(w_down.shape[1], 64))
    sharded("w_gate", got_gate, ns_ff, _tile(d, 256))
    sharded("w_up", got_up, ns_ff, _tile(d, 256))
    sharded("w_out", got_out, d, _tile(w_out.shape[1], 128))
    sharded("w_glu", got_glu, w_ssm, _tile(w_glu.shape[1], 128))
    sharded("w_in", got_in, ns_in, _tile(d, 256))

    split = dict(given)
    for pre in ("", "m_", "v_"):
        split[pre + "b_ada_a"], split[pre + "b_ada_b"] = given[pre + "b_ada"][:, :2 * d], given[pre + "b_ada"][:, 2 * d:]
    packs = [jnp.concatenate([_pack([split[pre + n] for n in SMALL_LATE]), _pack([split[pre + n] for n in SMALL_EARLY])])
             for pre in ("", "m_", "v_")]
    rows_p = packed_parts.shape[1]
    tr_p = _tile(rows_p, 64)
    sm = _adam_sum("adam_small", packed_parts, pl.BlockSpec((NDEV, tr_p, PACK_COLS), lambda i: (0, i, 0)), *packs, tr_p)
    rows_late = late_parts.shape[1]
    small_out = []
    for t in sm:
        out = dict(zip(SMALL_LATE, _unpack(t[:rows_late], [split[n] for n in SMALL_LATE])))
        out.update(zip(SMALL_EARLY, _unpack(t[rows_late:], [split[n] for n in SMALL_EARLY])))
        out["b_ada"] = jnp.concatenate([out["b_ada_a"], out["b_ada_b"]], axis=1)
        small_out.append(out)

    rows_a, rows_b = (2 * d) // PACK_COLS, (4 * d) // PACK_COLS
    assert rows_a * PACK_COLS == 2 * d
    dmod_all = jnp.concatenate([late_parts[:, :rows_a].reshape(NDEV, 2 * d), early_parts[:, :rows_b].reshape(NDEV, 4 * d)], axis=1)
    dmod_cols = lax.dynamic_slice(dmod_all, (0, me * n_ada), (NDEV, n_ada))
    big["w_ada"] = _adam_ada(c_all, dmod_cols, w_ada[0], m_w_ada[0], v_w_ada[0])

    order = ("w_ada", "b_ada", "g_mix", "w_in", "a_re", "a_im", "log_dt", "b_re", "b_im", "c_re", "c_im", "d_skip", "w_glu",
             "q_gain", "k_gain", "g_ssm_out", "g_attn_out", "w_out", "g_ffn", "w_gate", "w_up", "w_down")
    outs = [loss, grad_x[None]]
    for kind in range(4):
        for n in order:
            outs.append(big[n][kind][None] if n in big else small_out[kind][n])
    return tuple(outs)
```

```python
import functools
import math

import jax
import jax.numpy as jnp
from jax import lax
from jax.experimental import pallas as pl
from jax.experimental.pallas import tpu as pltpu

F32 = jnp.float32
BF16 = jnp.bfloat16
NDEV = 8
MESH_AXES = ("x", "y", "c")
MESH_ID = pl.DeviceIdType.MESH
EPS = 1e-6
LANES = 128
SUBLANES = 8
HEAD_DIM = 64
SSM_GROUP = 16
ADAM_LR, ADAM_B1, ADAM_B2, ADAM_EPS, ADAM_WD, ADAM_STEP = 0.001, 0.9, 0.999, 1e-08, 0.01, 10

NN = (((1,), (0,)), ((), ()))
NT = (((1,), (1,)), ((), ()))
TN = (((0,), (0,)), ((), ()))


def _dot(a, b, dn=NN):
    return lax.dot_general(a, b, dn, preferred_element_type=F32)


def _tile(dim, pref):
    t = min(dim, pref)
    while dim % t:
        t //= 2
    return t


def _params(n):
    return pltpu.CompilerParams(dimension_semantics=("arbitrary",) * n)


def _me():
    mx, my, mc = lax.axis_index("x"), lax.axis_index("y"), lax.axis_index("c")
    return mx, my, mc


def _peer(mx, my, mc, k):
    px = 1 - mx if (k >> 2) & 1 else mx
    py = 1 - my if (k >> 1) & 1 else my
    pc = 1 - mc if k & 1 else mc
    return (px, py, pc), 4 * px + 2 * py + pc


def _slot(ref, idx, part):
    return ref.at[idx] if part is None else ref.at[idx, pl.ds(*part)]


def _exchange_copies(x_ref, land_ref, send_sems, recv_sems, gather, part=None):
    mx, my, mc = _me()
    me = 4 * mx + 2 * my + mc
    pairs = []
    for k in range(1, NDEV):
        peer, pidx = _peer(mx, my, mc, k)
        src = x_ref if gather else _slot(x_ref, pidx, part)
        mk = lambda dst, src=src, k=k, peer=peer: pltpu.make_async_remote_copy(
            src_ref=src, dst_ref=dst, send_sem=send_sems.at[k - 1], recv_sem=recv_sems.at[k - 1],
            device_id=peer, device_id_type=MESH_ID)
        pairs.append((mk(_slot(land_ref, me, part)), mk(_slot(land_ref, pidx, part))))
    return me, pairs


def _exchange(x, gather, name, relayed=False):
    def body(x_ref, o_ref, send_sems, recv_sems, local_sem):
        if relayed:
            phases = _relayed_gather_phases(x_ref, o_ref, send_sems, recv_sems, local_sem)
        else:
            phases = _direct_phases(x_ref, o_ref, send_sems, recv_sems, local_sem, gather)
        for phase in phases:
            if phase is not None:
                phase()

    return pl.pallas_call(
        body, name=name, out_shape=jax.ShapeDtypeStruct(((NDEV,) + x.shape) if gather else x.shape, x.dtype),
        in_specs=[pl.BlockSpec(memory_space=pl.ANY)], out_specs=pl.BlockSpec(memory_space=pl.ANY),
        scratch_shapes=[pltpu.SemaphoreType.DMA((NDEV - 1,)), pltpu.SemaphoreType.DMA((NDEV - 1,)), pltpu.SemaphoreType.DMA],
    )(x)


def _direct_phases(x_ref, zone, send_sems, recv_sems, local_sem, gather, part=None):
    me, pairs = _exchange_copies(x_ref, zone, send_sems, recv_sems, gather, part)
    local = pltpu.make_async_copy(x_ref if gather else _slot(x_ref, me, part), _slot(zone, me, part), local_sem)

    def start():
        for send, _ in pairs:
            send.start()
        local.start()

    def finish():
        for send, arrival in pairs:
            send.wait_send()
            arrival.wait_recv()
        local.wait()

    return start, None, finish


def _relayed_gather_phases(x_ref, zone, send_sems, recv_sems, local_sem):
    mx, my, mc = _me()
    me, sibling = (mx, my, mc), (mx, my, 1 - mc)
    chips = [(1 - mx, my), (mx, 1 - my), (1 - mx, 1 - my)]
    rows = lambda dev: zone.at[4 * dev[0] + 2 * dev[1] + dev[2]]

    def copy(k, block, to, src=None):
        return pltpu.make_async_remote_copy(src_ref=rows(block) if src is None else src, dst_ref=rows(block),
                                            send_sem=send_sems.at[k], recv_sem=recv_sems.at[k], device_id=to,
                                            device_id_type=MESH_ID)

    local = pltpu.make_async_copy(x_ref, rows(me), local_sem)
    first = [copy(0, me, sibling, x_ref)] + [copy(1 + j, me, (*chip, mc), x_ref) for j, chip in enumerate(chips)]
    passed = [copy(4 + j, (*chip, mc), sibling) for j, chip in enumerate(chips)]
    over_links = [copy(1 + j, (*chip, mc), me) for j, chip in enumerate(chips)]
    from_sibling = [copy(0, sibling, me)] + [copy(4 + j, (*chip, 1 - mc), me) for j, chip in enumerate(chips)]

    def start():
        local.start()
        for cp in first:
            cp.start()

    def relay():
        for arrival, onward in zip(over_links, passed):
            arrival.wait_recv()
            onward.start()

    def finish():
        for arrival in from_sibling:
            arrival.wait_recv()
        for cp in first + passed:
            cp.wait_send()
        local.wait()

    return start, relay, finish


class _Rider:
    def __init__(self, x, gather, land=None, slot=None, n_slots=None, relayed=False, part=None):
        self.x, self.gather, self.land, self.slot, self.relayed, self.part = x, gather, land, slot, relayed, part
        own = ((NDEV,) + x.shape) if gather else x.shape
        self.land_shape = land.shape if land is not None else (own if n_slots is None else (n_slots,) + own)

    def phases(self, x_ref, land_ref, send_sems, recv_sems, local_sem):
        zone = land_ref if self.slot is None else land_ref.at[self.slot]
        if self.relayed:
            return _relayed_gather_phases(x_ref, zone, send_sems, recv_sems, local_sem)
        return _direct_phases(x_ref, zone, send_sems, recv_sems, local_sem, self.gather, self.part)


def _ride(call_name, grid, riders, inner, in_specs, out_specs, out_shape, scratch_shapes, compiler_params, operands):
    n_in, n_out, n_scr = len(in_specs), len(out_specs), len(scratch_shapes)
    any_spec = pl.BlockSpec(memory_space=pl.ANY)
    extra_in, aliases = [], {}
    for r_idx, r in enumerate(riders):
        extra_in.append(r.x)
        if r.land is not None:
            aliases[n_in + len(extra_in)] = n_out + r_idx
            extra_in.append(r.land)
    sems = []
    for _ in riders:
        sems += [pltpu.SemaphoreType.DMA((NDEV - 1,)), pltpu.SemaphoreType.DMA((NDEV - 1,)), pltpu.SemaphoreType.DMA]

    def body(*refs):
        base_in, rest = refs[:n_in], refs[n_in:]
        rider_in, rest = rest[:len(extra_in)], rest[len(extra_in):]
        base_out, rest = rest[:n_out], rest[n_out:]
        lands, rest = rest[:len(riders)], rest[len(riders):]
        base_scr, rider_sems = rest[:n_scr], rest[n_scr:]
        step = 0
        for a, g in enumerate(grid):
            step = step * g + pl.program_id(a)
        n_steps = math.prod(grid)
        sets, pos = [], 0
        for r_idx, r in enumerate(riders):
            x_ref = rider_in[pos]
            pos += 2 if r.land is not None else 1
            sets.append(r.phases(x_ref, lands[r_idx], *rider_sems[3 * r_idx:3 * r_idx + 3]))

        if sets:
            @pl.when(step == 0)
            def _():
                for start, _, _ in sets:
                    start()

        inner(*base_in, *base_out, *base_scr)

        if any(relay is not None for _, relay, _ in sets):
            @pl.when(step == (3 * n_steps) // 5)
            def _():
                for _, relay, _ in sets:
                    if relay is not None:
                        relay()

        if sets:
            @pl.when(step == n_steps - 1)
            def _():
                for _, _, finish in sets:
                    finish()

    outs = pl.pallas_call(
        body, name=call_name, grid=grid, in_specs=list(in_specs) + [any_spec] * len(extra_in),
        out_specs=list(out_specs) + [any_spec] * len(riders),
        out_shape=list(out_shape) + [jax.ShapeDtypeStruct(r.land_shape, r.x.dtype) for r in riders],
        scratch_shapes=list(scratch_shapes) + sems, input_output_aliases=aliases, compiler_params=compiler_params,
    )(*operands, *extra_in)
    return outs[:n_out], outs[n_out:]


def _mm(name, a, b, dn, grid, a_spec, b_spec, o_spec, out_shape, out_dtype, acc_shape, riders=(), pieces=None):
    nk = grid[2]

    def body(a_ref, b_ref, o_ref, *scratch):
        if pieces is None:
            part = _dot(a_ref[...].astype(BF16), b_ref[...].astype(BF16), dn)
        else:
            part = functools.reduce(lambda p, q: p + q, [_dot(a_g.astype(BF16), b_g.astype(BF16), dn)
                                                         for a_g, b_g in pieces(a_ref, b_ref)])
        if nk == 1:
            o_ref[...] = part.astype(o_ref.dtype)
            return
        acc_ref = scratch[0]
        k = pl.program_id(2)

        @pl.when(k == 0)
        def _():
            acc_ref[...] = part

        @pl.when(k > 0)
        def _():
            acc_ref[...] += part

        @pl.when(k == nk - 1)
        def _():
            o_ref[...] = acc_ref[...].astype(o_ref.dtype)

    (out,), lands = _ride(name, grid, riders, body, [a_spec, b_spec], [o_spec], [jax.ShapeDtypeStruct(out_shape, out_dtype)],
                          [] if nk == 1 else [pltpu.VMEM(acc_shape, F32)], _params(3), [a, b])
    return (out, *lands) if riders else out


BM, BN, BK = 1024, 1024, 4096


def _mm_plain(name, a, b, dn, out_dtype):
    if dn == NN:
        (m, kk), n = a.shape, b.shape[1]
    elif dn == NT:
        (m, kk), n = a.shape, b.shape[0]
    else:
        (kk, m), n = a.shape, b.shape[1]
    half = 2 if dn == TN else 1
    bm, bn, bk = _tile(m, BM // half), _tile(n, BN // half), _tile(kk, BK)
    a_spec = pl.BlockSpec((bk, bm), lambda i, j, k: (k, i)) if dn == TN else pl.BlockSpec((bm, bk), lambda i, j, k: (i, k))
    b_spec = pl.BlockSpec((bn, bk), lambda i, j, k: (j, k)) if dn == NT else pl.BlockSpec((bk, bn), lambda i, j, k: (k, j))
    return _mm(name, a, b, dn, (m // bm, n // bn, kk // bk), a_spec, b_spec,
               pl.BlockSpec((bm, bn), lambda i, j, k: (i, j)), (m, n), out_dtype, (bm, bn))


def _row_spec(tile, width, col=0):
    return pl.BlockSpec((tile, width), lambda i: (i, col))


def _vec_spec(width, col=0):
    return pl.BlockSpec((1, width), lambda i: (0, col))


def _rowwise_fwd(name, fn, rows, row_specs, vecs, vec_specs, out_shapes, out_specs, n_tiles, riders=()):
    nr, nv = len(rows), len(vecs)

    def body(*refs):
        ins = [r[...].astype(F32) for r in refs[:nr + nv]]
        outs = fn(*ins)
        for o_ref, o in zip(refs[nr + nv:], outs):
            o_ref[...] = o.astype(o_ref.dtype)

    outs, lands = _ride(name, (n_tiles,), riders, body, list(row_specs) + list(vec_specs), list(out_specs),
                        list(out_shapes), [], _params(1), [*rows, *vecs])
    return (outs, lands) if riders else outs


def _rowwise_bwd(name, fn, rows, row_specs, vecs, vec_specs, cts, ct_specs, ct_groups,
                 drow_idx, drow_shapes, drow_specs, dvec_idx, dvec_shapes, dvec_specs, n_tiles, riders=()):
    nr, nv, nc = len(rows), len(vecs), len(cts)

    def body(*refs):
        ins = [r[...].astype(F32) for r in refs[:nr + nv]]
        ct_vals = [r[...].astype(F32) for r in refs[nr + nv:nr + nv + nc]]
        out_refs = refs[nr + nv + nc:]
        _, vjp = jax.vjp(fn, *ins)
        grads = vjp(tuple(functools.reduce(lambda p, q: p + q, [ct_vals[j] for j in grp]) for grp in ct_groups))
        for o_ref, idx in zip(out_refs[:len(drow_idx)], drow_idx):
            o_ref[...] = grads[idx].astype(o_ref.dtype)
        step = pl.program_id(0)
        for o_ref, idx in zip(out_refs[len(drow_idx):], dvec_idx):
            @pl.when(step == 0)
            def _(o_ref=o_ref):
                o_ref[...] = jnp.zeros_like(o_ref)
            o_ref[...] += grads[nr + idx]

    outs, lands = _ride(name, (n_tiles,), riders, body, list(row_specs) + list(vec_specs) + list(ct_specs),
                        list(drow_specs) + list(dvec_specs), list(drow_shapes) + list(dvec_shapes), [], _params(1),
                        [*rows, *vecs, *cts])
    return (outs, lands) if riders else outs


def _rms(x):
    return x * lax.rsqrt(jnp.mean(x * x, axis=-1, keepdims=True) + EPS)


def _seg_in(x, shift, scale, gain):
    return _rms(x) * gain * (1.0 + scale) + shift, x


def _seg_qk(q, k, qg, kg):
    def norm(t, g, mult):
        blocks = []
        lane = lax.broadcasted_iota(jnp.int32, (1, LANES), 1)
        for p in range(t.shape[1] // LANES):
            tb = t[:, p * LANES:(p + 1) * LANES]
            sq = tb * tb
            lo = jnp.sum(jnp.where(lane < HEAD_DIM, sq, 0.0), axis=-1, keepdims=True)
            hi = jnp.sum(jnp.where(lane < HEAD_DIM, 0.0, sq), axis=-1, keepdims=True)
            ms = jnp.where(lane < HEAD_DIM, lo, hi) * (1.0 / HEAD_DIM)
            blocks.append(tb * lax.rsqrt(ms + EPS) * (g[:, p * LANES:(p + 1) * LANES] * mult))
        return jnp.concatenate(blocks, axis=-1) if len(blocks) > 1 else blocks[0]
    return norm(q, qg, 1.0 / math.sqrt(HEAD_DIM)), norm(k, kg, 1.0)


def _seg_gelu(ypre):
    return (jax.nn.gelu(ypre),)


def _seg_mix(y1, z, yattn, g_ssm, g_attn):
    ys = y1 * jax.nn.sigmoid(z)
    return (jnp.concatenate([_rms(ys) * g_ssm, _rms(yattn) * g_attn], axis=-1),)


def _seg_mid(x, o, gate_m, g_ffn, scale_f, shift_f):
    h1 = x + gate_m * o
    return h1, _rms(h1) * g_ffn * (1.0 + scale_f) + shift_f


def _seg_act(gate, up):
    return (jax.nn.silu(gate) * up,)


def _s5_lam(a_re, a_im, log_dt):
    dt = jnp.exp(log_dt)
    mag = jnp.exp(a_re * dt)
    lr, li = mag * jnp.cos(a_im * dt), mag * jnp.sin(a_im * dt)
    den = a_re * a_re + a_im * a_im
    nr, ni = lr - 1.0, li
    return lr, li, (nr * a_re + ni * a_im) / den, (ni * a_re - nr * a_im) / den


def _s5_bbar(coef_re, coef_im, b_re, b_im):
    return coef_re * b_re - coef_im * b_im, coef_re * b_im + coef_im * b_re


def _whole(name, fn, ins, out_shapes):
    n = len(ins)

    def body(*refs):
        outs = fn(*[r[...] for r in refs[:n]])
        for o_ref, o in zip(refs[n:], outs):
            o_ref[...] = o

    return pl.pallas_call(body, name=name, out_shape=[jax.ShapeDtypeStruct(s, F32) for s in out_shapes])(*ins)


def _whole_vjp(name, fn, ins, cts, out_shapes):
    n, nc = len(ins), len(cts)

    def body(*refs):
        _, vjp = jax.vjp(fn, *[r[...] for r in refs[:n]])
        grads = vjp(tuple(r[...] for r in refs[n:n + nc]))
        for o_ref, g in zip(refs[n + nc:], grads):
            o_ref[...] = g

    return pl.pallas_call(body, name=name, out_shape=[jax.ShapeDtypeStruct(s, F32) for s in out_shapes])(*ins, *cts)


SCAN_SHIFTS = (1, 2, 4)


def _cmul(ar, ai, br, bi):
    return ar * br - ai * bi, ar * bi + ai * br


def _scan_coefs(lr, li, reverse):
    s = lr.shape[1]
    row = lax.broadcasted_iota(jnp.int32, (SUBLANES, s), 0)
    p1 = (lr, li)
    p2 = _cmul(*p1, *p1)
    p4 = _cmul(*p2, *p2)
    p8 = _cmul(*p4, *p4)
    p3, p5, p6 = _cmul(*p1, *p2), _cmul(*p4, *p1), _cmul(*p4, *p2)
    p7 = _cmul(*p6, *p1)
    pows = (p1, p2, p3, p4, p5, p6, p7, p8)
    bc = lambda t: jnp.broadcast_to(t, (SUBLANES, s))
    steps = []
    for sh, pw in zip(SCAN_SHIFTS, (p1, p2, p4)):
        keep = (row + sh <= SUBLANES - 1) if reverse else (row >= sh)
        steps.append((jnp.where(keep, bc(pw[0]), 0.0), jnp.where(keep, bc(pw[1]), 0.0)))
    cr, ci = jnp.zeros((SUBLANES, s), F32), jnp.zeros((SUBLANES, s), F32)
    for r in range(SUBLANES):
        pw = pows[SUBLANES - 1 - r] if reverse else pows[r]
        cr = jnp.where(row == r, bc(pw[0]), cr)
        ci = jnp.where(row == r, bc(pw[1]), ci)
    return steps, (cr, ci)


def _scan_tile(xr, xi, steps, carry_pow, cr, ci, reverse):
    for sh, (ar, ai) in zip(SCAN_SHIFTS, steps):
        rs = SUBLANES - sh if reverse else sh
        sr, si = pltpu.roll(xr, rs, 0), pltpu.roll(xi, rs, 0)
        xr, xi = xr + ar * sr - ai * si, xi + ar * si + ai * sr
    pr, pi = carry_pow
    return xr + pr * cr - pi * ci, xi + pr * ci + pi * cr


def _s5_forward(proj, b_blk_re, b_blk_im, c_blk_re, c_blk_im, lam_re, lam_im, d_skip, n_blk, t_chunk, riders=()):
    seq = proj.shape[0]
    n_chunks = seq // t_chunk
    n_tiles = t_chunk // SUBLANES
    s = b_blk_re.shape[2]

    def body(u_ref, bre_ref, bim_ref, cre_ref, cim_ref, lr_ref, li_ref, d_ref, y_ref, xr_ref, xi_ref, wr, wi, carry):
        t = pl.program_id(1)

        @pl.when(t == 0)
        def _():
            carry[...] = jnp.zeros_like(carry)

        u = u_ref[...]
        ub = u.astype(BF16)
        wr[...] = _dot(ub, bre_ref[...])
        wi[...] = _dot(ub, bim_ref[...])
        steps, cpow = _scan_coefs(lr_ref[...], li_ref[...], False)

        def tile(i, c):
            r0 = pl.multiple_of(i * SUBLANES, SUBLANES)
            xr, xi = _scan_tile(wr[pl.ds(r0, SUBLANES), :], wi[pl.ds(r0, SUBLANES), :], steps, cpow, c[0], c[1], False)
            xr_ref[pl.ds(r0, SUBLANES), :] = xr
            xi_ref[pl.ds(r0, SUBLANES), :] = xi
            last = SUBLANES - 1
            return (jnp.broadcast_to(xr[last:, :], xr.shape), jnp.broadcast_to(xi[last:, :], xi.shape))

        cr, ci = lax.fori_loop(0, n_tiles, tile, (carry[0], carry[1]))
        carry[0] = cr
        carry[1] = ci
        y = _dot(xr_ref[...].astype(BF16), cre_ref[...]) - _dot(xi_ref[...].astype(BF16), cim_ref[...])
        y_ref[...] = y + d_ref[...] * u

    blk = lambda shape: pl.BlockSpec((None,) + shape, lambda j, t: (j, 0, 0))
    return _ride(
        "s5_fwd", (n_blk, n_chunks), riders, body,
        [pl.BlockSpec((t_chunk, LANES), lambda j, t: (t, j)), blk((LANES, s)), blk((LANES, s)),
         blk((s, LANES)), blk((s, LANES)), blk((1, s)), blk((1, s)), pl.BlockSpec((1, LANES), lambda j, t: (0, j))],
        [pl.BlockSpec((t_chunk, LANES), lambda j, t: (t, j)), pl.BlockSpec((t_chunk, s), lambda j, t: (t, j)),
         pl.BlockSpec((t_chunk, s), lambda j, t: (t, j))],
        [jax.ShapeDtypeStruct((seq, n_blk * LANES), F32), jax.ShapeDtypeStruct((seq, n_blk * s), F32),
         jax.ShapeDtypeStruct((seq, n_blk * s), F32)],
        [pltpu.VMEM((t_chunk, s), F32), pltpu.VMEM((t_chunk, s), F32), pltpu.VMEM((2, SUBLANES, s), F32)],
        _params(2), [proj, b_blk_re, b_blk_im, c_blk_re, c_blk_im, lam_re, lam_im, d_skip])


def _s5_backward(dypre, proj, x_re, x_im, b_blk_re, b_blk_im, c_blk_re, c_blk_im, lam_re, lam_im, d_skip, n_blk, t_chunk,
                 riders=()):
    seq = proj.shape[0]
    n_chunks = seq // t_chunk
    n_tiles = t_chunk // SUBLANES
    s = b_blk_re.shape[2]

    def body(dy_ref, u_ref, xr_ref, xi_ref, pr_ref, pi_ref, bre_ref, bim_ref, cre_ref, cim_ref, lr_ref, li_ref, d_ref,
             du_ref, dbre_ref, dbim_ref, dcre_ref, dcim_ref, dlr_ref, dli_ref, dd_ref, gr, gi, carry):
        t = pl.program_id(1)

        @pl.when(t == 0)
        def _():
            carry[...] = jnp.zeros_like(carry)
            for r in (dbre_ref, dbim_ref, dcre_ref, dcim_ref, dlr_ref, dli_ref, dd_ref):
                r[...] = jnp.zeros_like(r)

        dy = dy_ref[...]
        dyb = dy.astype(BF16)
        u = u_ref[...]
        gr[...] = _dot(dyb, cre_ref[...], NT)
        gi[...] = -_dot(dyb, cim_ref[...], NT)
        steps, cpow = _scan_coefs(lr_ref[...], -li_ref[...], True)
        row = lax.broadcasted_iota(jnp.int32, (SUBLANES, s), 0)
        last = SUBLANES - 1
        first_chunk = t == n_chunks - 1

        def tile_at(r0, prev_r, prev_i, c):
            cr, ci, ar, ai = c
            lr_, li_ = _scan_tile(gr[pl.ds(r0, SUBLANES), :], gi[pl.ds(r0, SUBLANES), :], steps, cpow, cr, ci, True)
            gr[pl.ds(r0, SUBLANES), :] = lr_
            gi[pl.ds(r0, SUBLANES), :] = li_
            xr, xi = xr_ref[pl.ds(r0, SUBLANES), :], xi_ref[pl.ds(r0, SUBLANES), :]
            xpr = jnp.where(row == 0, jnp.broadcast_to(prev_r[last:, :], xr.shape), pltpu.roll(xr, 1, 0))
            xpi = jnp.where(row == 0, jnp.broadcast_to(prev_i[last:, :], xi.shape), pltpu.roll(xi, 1, 0))
            ar = ar + lr_ * xpr + li_ * xpi
            ai = ai + li_ * xpr - lr_ * xpi
            return (jnp.broadcast_to(lr_[:1, :], lr_.shape), jnp.broadcast_to(li_[:1, :], li_.shape), ar, ai)

        def tile(ii, c):
            i = n_tiles - 1 - ii
            r0 = pl.multiple_of(i * SUBLANES, SUBLANES)
            rp = pl.multiple_of(r0 - SUBLANES, SUBLANES)
            return tile_at(r0, xr_ref[pl.ds(rp, SUBLANES), :], xi_ref[pl.ds(rp, SUBLANES), :], c)

        zero = jnp.zeros((SUBLANES, s), F32)
        c = lax.fori_loop(0, n_tiles - 1, tile, (carry[0], carry[1], zero, zero))
        keep = jnp.where(first_chunk, 0.0, 1.0)
        c = tile_at(0, pr_ref[...] * keep, pi_ref[...] * keep, c)
        carry[0] = c[0]
        carry[1] = c[1]
        dlr_ref[...] += jnp.sum(c[2], axis=0, keepdims=True)
        dli_ref[...] += jnp.sum(c[3], axis=0, keepdims=True)

        lam_r, lam_i = gr[...].astype(BF16), gi[...].astype(BF16)
        du_ref[...] = (_dot(lam_r, bre_ref[...], NT) + _dot(lam_i, bim_ref[...], NT) + d_ref[...] * dy).astype(du_ref.dtype)
        ub = u.astype(BF16)
        dbre_ref[...] += _dot(ub, lam_r, TN)
        dbim_ref[...] += _dot(ub, lam_i, TN)
        dcre_ref[...] += _dot(xr_ref[...].astype(BF16), dyb, TN)
        dcim_ref[...] -= _dot(xi_ref[...].astype(BF16), dyb, TN)
        dd_ref[...] += jnp.sum(dy * u, axis=0, keepdims=True)

    rev = lambda t: n_chunks - 1 - t
    blk = lambda shape: pl.BlockSpec((None,) + shape, lambda j, t: (j, 0, 0))
    tpc = t_chunk // SUBLANES
    prev_spec = pl.BlockSpec((SUBLANES, s), lambda j, t: (jnp.maximum(rev(t) * tpc - 1, 0), j))
    chunk = lambda w: pl.BlockSpec((t_chunk, w), lambda j, t: (rev(t), j))
    return _ride(
        "s5_bwd", (n_blk, n_chunks), riders, body,
        [chunk(LANES), chunk(LANES), chunk(s), chunk(s), prev_spec, prev_spec, blk((LANES, s)), blk((LANES, s)),
         blk((s, LANES)), blk((s, LANES)), blk((1, s)), blk((1, s)), pl.BlockSpec((1, LANES), lambda j, t: (0, j))],
        [chunk(LANES), blk((LANES, s)), blk((LANES, s)), blk((s, LANES)), blk((s, LANES)), blk((1, s)), blk((1, s)),
         pl.BlockSpec((1, LANES), lambda j, t: (0, j))],
        [jax.ShapeDtypeStruct((seq, n_blk * LANES), BF16),
         jax.ShapeDtypeStruct((n_blk, LANES, s), F32), jax.ShapeDtypeStruct((n_blk, LANES, s), F32),
         jax.ShapeDtypeStruct((n_blk, s, LANES), F32), jax.ShapeDtypeStruct((n_blk, s, LANES), F32),
         jax.ShapeDtypeStruct((n_blk, 1, s), F32), jax.ShapeDtypeStruct((n_blk, 1, s), F32),
         jax.ShapeDtypeStruct((1, n_blk * LANES), F32)],
        [pltpu.VMEM((t_chunk, s), F32), pltpu.VMEM((t_chunk, s), F32), pltpu.VMEM((2, SUBLANES, s), F32)],
        _params(2), [dypre, proj, x_re, x_im, x_re, x_im, b_blk_re, b_blk_im, c_blk_re, c_blk_im, lam_re, lam_im, d_skip])


TQ, TK = 256, 128


def _split_bf16(x):
    hi = x.astype(BF16)
    return hi, (x - hi.astype(F32)).astype(BF16)


def _sb_weights(z, past, carry, tri):
    ls = jnp.minimum(z, 0.0) - jnp.log(1.0 + jnp.exp(-jnp.abs(z)))
    lk = ls - z
    if past is not None:
        lk = jnp.where(past, lk, 0.0)
    w = jnp.exp(ls + _dot(lk.astype(BF16), tri) + carry)
    if past is not None:
        w = jnp.where(past, w, 0.0)
    return ls, lk, w


LOG_KEEP_DEAD = -104.0


def _walk_key_blocks(i, ratio, prologue, block, epilogue, log_keep):
    n_kb = (i + 1) * ratio
    prologue(n_kb - 1)
    for n in range(ratio):
        block(n_kb - 1 - n, n % 2, True)
    assert ratio % 2 == 0
    n_pairs = (i * ratio) // 2

    def more(state):
        t, alive = state
        return jnp.logical_and(t < n_pairs, alive)

    def pair(state):
        t, _ = state
        j = n_kb - 1 - ratio - 2 * t
        block(j, ratio % 2, False)
        block(j - 1, (ratio + 1) % 2, False)
        return t + 1, log_keep() >= LOG_KEEP_DEAD

    done, _ = lax.while_loop(more, pair, (jnp.int32(0), log_keep() >= LOG_KEEP_DEAD))
    epilogue(n_kb - ratio - 2 * done)


def _attention_forward(qh, kh, proj, v_col, n_pair, tq, tk, riders=()):
    seq = qh.shape[0]
    ratio = tq // tk

    def body(q_ref, k_ref, v_ref, o_ref, q_scr, z_scr, w_scr, acc_scr, c_scr):
        i = pl.program_id(1)
        lane = lax.broadcasted_iota(jnp.int32, (1, LANES), 1)
        q2 = q_ref[...]
        q_scr[0] = jnp.where(lane < HEAD_DIM, q2, 0.0).astype(BF16)
        q_scr[1] = jnp.where(lane < HEAD_DIM, 0.0, q2).astype(BF16)
        tri = (lax.broadcasted_iota(jnp.int32, (tk, tk), 0) > lax.broadcasted_iota(jnp.int32, (tk, tk), 1)).astype(BF16)
        qpos = i * tq + lax.broadcasted_iota(jnp.int32, (tq, tk), 0)
        kidx = lax.broadcasted_iota(jnp.int32, (tq, tk), 1)

        def rows(ref, j):
            j = jnp.clip(j, 0, seq // tk - 1)
            return ref[pl.ds(pl.multiple_of(j * tk, tk), tk), :].astype(BF16)

        def scores(j, slot):
            kb = rows(k_ref, j)
            for h in range(2):
                z_scr[slot, h] = _dot(q_scr[h], kb, NT)

        def finish(j):
            vb = rows(v_ref, j)
            for h in range(2):
                acc_scr[h] += _dot(w_scr[h], vb)

        def prologue(j):
            w_scr[...] = jnp.zeros_like(w_scr)
            acc_scr[...] = jnp.zeros_like(acc_scr)
            c_scr[...] = jnp.zeros_like(c_scr)
            scores(j, 0)

        def block(j, slot, masked):
            scores(j - 1, 1 - slot)
            finish(j + 1)
            past = ((kidx + j * tk) < qpos) if masked else None
            for h in range(2):
                _, lk, w = _sb_weights(z_scr[slot, h], past, c_scr[h], tri)
                w_scr[h] = w.astype(BF16)
                c_scr[h] += jnp.sum(lk, axis=-1, keepdims=True)

        _walk_key_blocks(i, ratio, prologue, block, finish, lambda: jnp.max(c_scr[...]))
        o_ref[...] = jnp.where(lane < HEAD_DIM, acc_scr[0], acc_scr[1])

    (out,), lands = _ride(
        "attn_fwd", (n_pair, seq // tq), riders, body,
        [pl.BlockSpec((tq, LANES), lambda p, i: (i, p)), pl.BlockSpec((seq, LANES), lambda p, i: (0, p)),
         pl.BlockSpec((seq, LANES), lambda p, i: (0, v_col + p))],
        [pl.BlockSpec((tq, LANES), lambda p, i: (i, p))], [jax.ShapeDtypeStruct(qh.shape, F32)],
        [pltpu.VMEM((2, tq, LANES), BF16), pltpu.VMEM((2, 2, tq, tk), F32), pltpu.VMEM((2, tq, tk), BF16),
         pltpu.VMEM((2, tq, LANES), F32), pltpu.VMEM((2, tq, 1), F32)],
        _params(2), [qh, kh, proj])
    return out, lands


def _attention_backward(qh, kh, proj, v_col, y, dy, n_pair, tq, tk, riders=()):
    seq = qh.shape[0]

    ratio = tq // tk

    n_kblk = seq // tk

    def body(q_ref, k_ref, v_ref, y_ref, dy_ref, dq_ref, dk_ref, dv_ref,
             q_scr, do_scr, qt_scr, dot_scr, dkt_scr, dvt_scr, z_scr, dw_scr, w_scr, dz_scr, dq_scr, c_scr, c2_scr, tot_scr):
        i = pl.program_id(1)

        @pl.when(i == 0)
        def _():
            dkt_scr[...] = jnp.zeros_like(dkt_scr)
            dvt_scr[...] = jnp.zeros_like(dvt_scr)

        lane = lax.broadcasted_iota(jnp.int32, (1, LANES), 1)
        sel = (lane < HEAD_DIM, lane >= HEAD_DIM)
        q2, do2 = q_ref[...], dy_ref[...].astype(BF16)
        do2f = do2.astype(F32)
        dot_oy = do2f * y_ref[...]
        for h in range(2):
            qm, dm = jnp.where(sel[h], q2, 0.0), jnp.where(sel[h], do2f, 0.0)
            q_scr[h] = qm.astype(BF16)
            do_scr[h] = dm.astype(BF16)
            qt_scr[h] = qm.T.astype(BF16)
            dot_scr[h] = dm.T.astype(BF16)
            tot_scr[h] = jnp.sum(jnp.where(sel[h], dot_oy, 0.0), axis=-1, keepdims=True)
        r_i, c_i = lax.broadcasted_iota(jnp.int32, (tk, tk), 0), lax.broadcasted_iota(jnp.int32, (tk, tk), 1)
        tri = (r_i > c_i).astype(BF16)
        tri_ge = (r_i >= c_i).astype(BF16)
        qpos = i * tq + lax.broadcasted_iota(jnp.int32, (tq, tk), 0)
        kidx = lax.broadcasted_iota(jnp.int32, (tq, tk), 1)

        def start(j):
            return pl.multiple_of(jnp.clip(j, 0, seq // tk - 1) * tk, tk)

        def scores(j, slot):
            c0 = start(j)
            kb, vb = k_ref[pl.ds(c0, tk), :].astype(BF16), v_ref[pl.ds(c0, tk), :].astype(BF16)
            for h in range(2):
                z_scr[slot, h] = _dot(q_scr[h], kb, NT)
                dw_scr[slot, h] = _dot(do_scr[h], vb, NT)

        def finish(j):
            jc = jnp.clip(j, 0, n_kblk - 1)
            kb = k_ref[pl.ds(pl.multiple_of(jc * tk, tk), tk), :].astype(BF16)
            dkt_add, dvt_add = jnp.zeros((LANES, tk), F32), jnp.zeros((LANES, tk), F32)
            for h in range(2):
                dz = dz_scr[h]
                dq_scr[h] += _dot(dz, kb)
                dkt_add = dkt_add + _dot(qt_scr[h], dz)
                dvt_add = dvt_add + _dot(dot_scr[h], w_scr[h])
            dkt_scr[jc] += dkt_add
            dvt_scr[jc] += dvt_add

        def prologue(j):
            for r in (w_scr, dz_scr, dq_scr, c_scr, c2_scr):
                r[...] = jnp.zeros_like(r)
            scores(j, 0)

        def block(j, slot, masked):
            scores(j - 1, 1 - slot)
            finish(j + 1)
            past = ((kidx + j * tk) < qpos) if masked else None
            for h in range(2):
                ls, lk, w = _sb_weights(z_scr[slot, h], past, c_scr[h], tri)
                wb = w.astype(BF16)
                dlw = dw_scr[slot, h] * wb.astype(F32)
                hi, lo = _split_bf16(dlw)
                dlk = tot_scr[h] - c2_scr[h] - (_dot(hi, tri_ge) + _dot(lo, tri_ge))
                if masked:
                    dlk = jnp.where(past, dlk, 0.0)
                sig = jnp.exp(ls)
                w_scr[h] = wb
                dz_scr[h] = (dlw * (1.0 - sig) - dlk * sig).astype(BF16)
                c_scr[h] += jnp.sum(lk, axis=-1, keepdims=True)
                c2_scr[h] += jnp.sum(dlw, axis=-1, keepdims=True)

        _walk_key_blocks(i, ratio, prologue, block, finish, lambda: jnp.max(c_scr[...]))
        dq_ref[...] = jnp.where(sel[0], dq_scr[0], dq_scr[1])

        @pl.when(i == seq // tq - 1)
        def _():
            for jb in range(n_kblk):
                dk_ref[jb * tk:(jb + 1) * tk, :] = dkt_scr[jb].T
                dv_ref[jb * tk:(jb + 1) * tk, :] = dvt_scr[jb].T.astype(dv_ref.dtype)

    blk = pl.BlockSpec((tq, LANES), lambda p, i: (i, p))
    full = pl.BlockSpec((seq, LANES), lambda p, i: (0, p))
    shape = jax.ShapeDtypeStruct(qh.shape, F32)
    return _ride(
        "attn_bwd", (n_pair, seq // tq), riders, body,
        [blk, full, pl.BlockSpec((seq, LANES), lambda p, i: (0, v_col + p)), blk, blk],
        [blk, full, full], [shape, shape, jax.ShapeDtypeStruct(qh.shape, BF16)],
        [pltpu.VMEM((2, tq, LANES), BF16), pltpu.VMEM((2, tq, LANES), BF16),
         pltpu.VMEM((2, LANES, tq), BF16), pltpu.VMEM((2, LANES, tq), BF16),
         pltpu.VMEM((n_kblk, LANES, tk), F32), pltpu.VMEM((n_kblk, LANES, tk), F32),
         pltpu.VMEM((2, 2, tq, tk), F32), pltpu.VMEM((2, 2, tq, tk), F32),
         pltpu.VMEM((2, tq, tk), BF16), pltpu.VMEM((2, tq, tk), BF16), pltpu.VMEM((2, tq, LANES), F32),
         pltpu.VMEM((2, tq, 1), F32), pltpu.VMEM((2, tq, 1), F32), pltpu.VMEM((2, tq, 1), F32)],
        _params(2), [qh, kh, proj, y, dy])


def _loss_head(h1, ffn, target, gate_f, tile):
    seq, d = h1.shape

    def body(h_ref, f_ref, t_ref, g_ref, dy_ref, df_ref, dg_ref, loss_ref):
        @pl.when(pl.program_id(0) == 0)
        def _():
            dg_ref[...] = jnp.zeros_like(dg_ref)
            loss_ref[...] = jnp.zeros_like(loss_ref)

        f, g = f_ref[...], g_ref[...]
        err = h_ref[...] + g * f - t_ref[...]
        dy = err * (1.0 / d)
        dy_ref[...] = dy
        df_ref[...] = (dy * g).astype(df_ref.dtype)
        dg_ref[...] += jnp.sum(dy * f, axis=0, keepdims=True)
        loss_ref[...] += jnp.sum(jnp.sum(err * err, axis=-1, keepdims=True), axis=0, keepdims=True) * (0.5 / d)

    row = _row_spec(tile, d)
    return pl.pallas_call(
        body, name="loss_head", grid=(seq // tile,), in_specs=[row, row, row, _vec_spec(d)],
        out_specs=[row, row, _vec_spec(d), pl.BlockSpec((1, 1), lambda i: (0, 0))],
        out_shape=[jax.ShapeDtypeStruct((seq, d), F32), jax.ShapeDtypeStruct((seq, d), BF16),
                   jax.ShapeDtypeStruct((1, d), F32), jax.ShapeDtypeStruct((1, 1), F32)],
        compiler_params=_params(1),
    )(h1, ffn, target, gate_f)


def _dot3(a, b, dn):
    ah, al = _split_bf16(a)
    bh, bl = _split_bf16(b)
    return _dot(ah, bh, dn) + (_dot(ah, bl, dn) + _dot(al, bh, dn))


def _ada_forward(c_all, w_shard, b_cols):
    d, n = w_shard.shape
    bk = _tile(d, 512)

    def body(c_ref, w_ref, b_ref, o_ref):
        @pl.when(pl.program_id(0) == 0)
        def _():
            o_ref[...] = jnp.broadcast_to(b_ref[...], o_ref.shape)

        o_ref[...] += _dot3(jax.nn.silu(c_ref[...]), w_ref[...], NN)

    return pl.pallas_call(
        body, name="ada_fwd", grid=(d // bk,),
        in_specs=[pl.BlockSpec((NDEV, bk), lambda k: (0, k)), pl.BlockSpec((bk, n), lambda k: (k, 0)), _vec_spec(n)],
        out_specs=pl.BlockSpec((NDEV, n), lambda k: (0, 0)), out_shape=jax.ShapeDtypeStruct((NDEV, n), F32),
        compiler_params=_params(1),
    )(c_all, w_shard, b_cols)


def _adam(w, g, m, v):
    m = ADAM_B1 * m + (1.0 - ADAM_B1) * g
    v = ADAM_B2 * v + (1.0 - ADAM_B2) * (g * g)
    m_hat = m / (1.0 - ADAM_B1 ** ADAM_STEP)
    v_hat = v / (1.0 - ADAM_B2 ** ADAM_STEP)
    return -ADAM_LR * (m_hat / (jnp.sqrt(v_hat) + ADAM_EPS) + ADAM_WD * w), m, v


def _adam_ada(c_all, dmod_cols, w, m, v):
    d, n = w.shape
    tr = _tile(d, 256)

    def body(c_ref, dm_ref, w_ref, m_ref, v_ref, g_ref, dl_ref, nm_ref, nv_ref):
        g = _dot3(jax.nn.silu(c_ref[...]), dm_ref[...], TN)
        delta, nm, nv = _adam(w_ref[...], g, m_ref[...], v_ref[...])
        g_ref[...] = g
        dl_ref[...] = delta
        nm_ref[...] = nm
        nv_ref[...] = nv

    row = _row_spec(tr, n)
    return pl.pallas_call(
        body, name="adam_ada", grid=(d // tr,),
        in_specs=[pl.BlockSpec((NDEV, tr), lambda i: (0, i)), pl.BlockSpec((NDEV, n), lambda i: (0, 0)), row, row, row],
        out_specs=[row] * 4, out_shape=[jax.ShapeDtypeStruct((d, n), F32)] * 4, compiler_params=_params(1),
    )(c_all, dmod_cols, w, m, v)


def _adam_sum(name, parts, part_spec, w, m, v, tr):
    r, c = w.shape

    def body(p_ref, w_ref, m_ref, v_ref, g_ref, dl_ref, nm_ref, nv_ref):
        g = p_ref[0].astype(F32)
        for k in range(1, NDEV):
            g = g + p_ref[k].astype(F32)
        delta, nm, nv = _adam(w_ref[...], g, m_ref[...], v_ref[...])
        g_ref[...] = g
        dl_ref[...] = delta
        nm_ref[...] = nm
        nv_ref[...] = nv

    row = _row_spec(tr, c)
    return pl.pallas_call(
        body, name=name, grid=(r // tr,), in_specs=[part_spec, row, row, row],
        out_specs=[row] * 4, out_shape=[jax.ShapeDtypeStruct((r, c), F32)] * 4, compiler_params=_params(1),
    )(parts, w, m, v)


GROUPS_PER_BLOCK = LANES // SSM_GROUP


def _to_b_blocks(bb, n_blk, p):
    t = bb.reshape(n_blk, GROUPS_PER_BLOCK, p, SSM_GROUP)
    eye = jnp.eye(GROUPS_PER_BLOCK, dtype=bb.dtype)
    return jnp.einsum("jgph,gk->jghkp", t, eye).reshape(n_blk, LANES, GROUPS_PER_BLOCK * p)


def _from_b_blocks(blk, n_blk, p):
    t = blk.reshape(n_blk, GROUPS_PER_BLOCK, SSM_GROUP, GROUPS_PER_BLOCK, p)
    eye = jnp.eye(GROUPS_PER_BLOCK, dtype=blk.dtype)
    return jnp.einsum("jghkp,gk->jgph", t, eye).reshape(n_blk * GROUPS_PER_BLOCK, p, SSM_GROUP)


def _to_c_blocks(cc, n_blk, p):
    t = cc.reshape(n_blk, GROUPS_PER_BLOCK, SSM_GROUP, p)
    eye = jnp.eye(GROUPS_PER_BLOCK, dtype=cc.dtype)
    return jnp.einsum("jghp,gk->jgpkh", t, eye).reshape(n_blk, GROUPS_PER_BLOCK * p, LANES)


def _from_c_blocks(blk, n_blk, p):
    t = blk.reshape(n_blk, GROUPS_PER_BLOCK, p, GROUPS_PER_BLOCK, SSM_GROUP)
    eye = jnp.eye(GROUPS_PER_BLOCK, dtype=blk.dtype)
    return jnp.einsum("jgpkh,gk->jghp", t, eye).reshape(n_blk * GROUPS_PER_BLOCK, SSM_GROUP, p)


SMALL_LATE = ("b_ada_a", "g_mix")
SMALL_EARLY = ("b_ada_b", "a_re", "a_im", "log_dt", "b_re", "b_im", "c_re", "c_im", "d_skip",
               "q_gain", "k_gain", "g_ssm_out", "g_attn_out", "g_ffn")
PACK_COLS = 1024


def _pack(arrs):
    flat = jnp.concatenate([a.reshape(-1) for a in arrs])
    n = flat.shape[0]
    quantum = SUBLANES * PACK_COLS
    padded = -(-n // quantum) * quantum
    return jnp.pad(flat, (0, padded - n)).reshape(padded // PACK_COLS, PACK_COLS)


def _unpack(packed, like):
    flat, out, off = packed.reshape(-1), [], 0
    for a in like:
        out.append(flat[off:off + a.size].reshape(a.shape))
        off += a.size
    return out


def kernel(x, c, w_ada, b_ada, g_mix, w_in, a_re, a_im, log_dt, b_re, b_im, c_re, c_im, d_skip, w_glu, q_gain, k_gain, g_ssm_out, g_attn_out, w_out, g_ffn, w_gate, w_up, w_down, loss_target, m_w_ada, m_b_ada, m_g_mix, m_w_in, m_a_re, m_a_im, m_log_dt, m_b_re, m_b_im, m_c_re, m_c_im, m_d_skip, m_w_glu, m_q_gain, m_k_gain, m_g_ssm_out, m_g_attn_out, m_w_out, m_g_ffn, m_w_gate, m_w_up, m_w_down, v_w_ada, v_b_ada, v_g_mix, v_w_in, v_a_re, v_a_im, v_log_dt, v_b_re, v_b_im, v_c_re, v_c_im, v_d_skip, v_w_glu, v_q_gain, v_k_gain, v_g_ssm_out, v_g_attn_out, v_w_out, v_g_ffn, v_w_gate, v_w_up, v_w_down):
    given = dict(locals())
    seq, d = x.shape[1], x.shape[2]
    xs, tgt = x[0], loss_target[0]
    n_groups, p_state = a_re.shape[1], a_re.shape[2]
    w_ssm = n_groups * SSM_GROUP
    w_attn = w_in.shape[2] * NDEV - w_ssm
    w_attn //= 3
    n_blk, n_pair = w_ssm // LANES, w_attn // LANES
    n_heads = w_attn // HEAD_DIM
    ns_in, ns_ff = w_in.shape[2], w_gate.shape[2]
    d_mix = w_ssm + w_attn
    mx, my, mc = _me()
    me = 4 * mx + 2 * my + mc
    rt = _tile(seq, 256)
    n_rt = seq // rt
    sds = jax.ShapeDtypeStruct

    c_all = _exchange(c, True, "comm_ag_c").reshape(NDEV, d)
    n_ada = w_ada.shape[2]
    b_cols = lax.dynamic_slice(b_ada, (0, me * n_ada), (1, n_ada))
    mod_cols = _ada_forward(c_all, w_ada[0], b_cols)
    mod_all = _exchange(mod_cols, True, "comm_ag_mod")
    mod = lax.dynamic_slice(mod_all, (0, me, 0), (NDEV, 1, n_ada)).reshape(1, NDEV * n_ada)
    shift_m, scale_m, gate_m, shift_f, scale_f, gate_f = [mod[:, i * d:(i + 1) * d] for i in range(6)]

    gp = n_groups * p_state
    a_re2, a_im2, ldt2 = a_re[0], a_im[0], log_dt[0].reshape(n_groups, 1)
    b_re2, b_im2 = b_re[0].reshape(gp, SSM_GROUP), b_im[0].reshape(gp, SSM_GROUP)
    lam_r, lam_i, coef_r, coef_i = _whole("s5_lam", _s5_lam, [a_re2, a_im2, ldt2], [(n_groups, p_state)] * 4)
    coef_r2, coef_i2 = coef_r.reshape(gp, 1), coef_i.reshape(gp, 1)
    bb_r, bb_i = _whole("s5_bbar", _s5_bbar, [coef_r2, coef_i2, b_re2, b_im2], [(gp, SSM_GROUP)] * 2)
    s_blk = GROUPS_PER_BLOCK * p_state
    b_blk_r = _to_b_blocks(bb_r.reshape(n_groups, p_state, SSM_GROUP), n_blk, p_state).astype(BF16)
    b_blk_i = _to_b_blocks(bb_i.reshape(n_groups, p_state, SSM_GROUP), n_blk, p_state).astype(BF16)
    c_blk_r = _to_c_blocks(c_re[0], n_blk, p_state).astype(BF16)
    c_blk_i = _to_c_blocks(c_im[0], n_blk, p_state).astype(BF16)
    lam_r3, lam_i3 = lam_r.reshape(n_blk, 1, s_blk), lam_i.reshape(n_blk, 1, s_blk)
    d_skip2 = d_skip[0].reshape(1, w_ssm)

    row_d, vec_d = _row_spec(rt, d), _vec_spec(d)
    (xm,), (w_in_g,) = _rowwise_fwd("seg_in", lambda *a: _seg_in(*a)[:1], [xs], [row_d], [shift_m, scale_m, g_mix], [vec_d] * 3,
                                    [sds((seq, d), BF16)], [row_d], n_rt,
                                    riders=[_Rider(w_in[0].astype(BF16), True, relayed=True)])
    bn_in = _tile(ns_in, 512)
    per = ns_in // bn_in
    bm, bk = _tile(seq, BM), _tile(d, BK)
    proj, w_glu_g, w_out_g = _mm(
        "mm_in", xm, w_in_g, NN, (seq // bm, NDEV * per, d // bk),
        pl.BlockSpec((bm, bk), lambda i, j, k: (i, k)),
        pl.BlockSpec((None, bk, bn_in), lambda i, j, k: (j // per, k, j % per)),
        pl.BlockSpec((bm, bn_in), lambda i, j, k: (i, j)), (seq, NDEV * ns_in), F32, (bm, bn_in),
        riders=[_Rider(w_glu[0].astype(BF16), True, relayed=True), _Rider(w_out[0].astype(BF16), True, relayed=True)])
    w_glu_g, w_out_g = w_glu_g.reshape(w_ssm, w_ssm), w_out_g.reshape(d_mix, d)
    q_col, k_col, v_col = w_ssm // w_attn, w_ssm // w_attn + 1, (w_ssm + 2 * w_attn) // LANES
    qg_t, kg_t = jnp.tile(q_gain, (1, n_heads)), jnp.tile(k_gain, (1, n_heads))
    row_a, vec_a = _row_spec(rt, w_attn), _vec_spec(w_attn)
    qk_rows, qk_specs = [proj, proj], [_row_spec(rt, w_attn, q_col), _row_spec(rt, w_attn, k_col)]
    qh, kh = _rowwise_fwd("seg_qk", _seg_qk, qk_rows, qk_specs, [qg_t, kg_t], [vec_a] * 2,
                          [sds((seq, w_attn), F32)] * 2, [row_a] * 2, n_rt)
    t_chunk = _tile(seq, 1024)
    (ypre, x_re, x_im), (w_gu_land,) = _s5_forward(
        proj, b_blk_r, b_blk_i, c_blk_r, c_blk_i, lam_r3, lam_i3, d_skip2, n_blk, t_chunk,
        riders=[_Rider(w_up[0].astype(BF16), True, slot=1, n_slots=2, relayed=True)])
    tq, tk = _tile(seq, TQ), _tile(seq, TK)
    y_attn, (w_gu_land,) = _attention_forward(qh, kh, proj, v_col, n_pair, tq, tk,
                                              riders=[_Rider(w_gate[0].astype(BF16), True, land=w_gu_land, slot=0,
                                                             relayed=True)])
    w_gu_g = w_gu_land.reshape(2 * NDEV, d, ns_ff)
    row_s, vec_s = _row_spec(rt, w_ssm), _vec_spec(w_ssm)
    y1, = _rowwise_fwd("seg_gelu", _seg_gelu, [ypre], [row_s], [], [], [sds((seq, w_ssm), F32)], [row_s], n_rt)
    z = _mm_plain("mm_glu", y1, w_glu_g, NN, F32)
    row_m = _row_spec(rt, d_mix)
    mixed, = _rowwise_fwd("seg_mix", _seg_mix, [y1, z, y_attn], [row_s, row_s, row_a], [g_ssm_out, g_attn_out], [vec_s, vec_a],
                          [sds((seq, d_mix), BF16)], [row_m], n_rt)
    o = _mm_plain("mm_out", mixed, w_out_g, NN, F32)
    h1, xf = _rowwise_fwd("seg_mid", _seg_mid, [xs, o], [row_d] * 2, [gate_m, g_ffn, scale_f, shift_f], [vec_d] * 4,
                          [sds((seq, d), F32), sds((seq, d), BF16)], [row_d] * 2, n_rt)
    gu, w_down_g = _mm(
        "mm_gu", xf, w_gu_g, NN, (seq // bm, 2 * NDEV, d // bk),
        pl.BlockSpec((bm, bk), lambda i, j, k: (i, k)), pl.BlockSpec((None, bk, ns_ff), lambda i, j, k: (j, k, 0)),
        pl.BlockSpec((None, bm, ns_ff), lambda i, j, k: (j, i, 0)), (2 * NDEV, seq, ns_ff), BF16, (bm, ns_ff),
        riders=[_Rider(w_down[0].astype(BF16), True, relayed=True)])
    gu4 = gu.reshape(2, NDEV, seq, ns_ff)
    ft = _tile(seq, 512)
    pair_spec = pl.BlockSpec((2, None, ft, ns_ff), lambda s, i: (0, s, i, 0))
    one_spec = pl.BlockSpec((None, ft, ns_ff), lambda s, i: (s, i, 0))

    def act_body(gu_ref, a_ref):
        a_ref[...] = _seg_act(gu_ref[0].astype(F32), gu_ref[1].astype(F32))[0].astype(a_ref.dtype)

    act = pl.pallas_call(act_body, name="seg_act", grid=(NDEV, seq // ft), in_specs=[pair_spec], out_specs=one_spec,
                         out_shape=sds((NDEV, seq, ns_ff), BF16), compiler_params=_params(2))(gu4)
    bn_d = _tile(d, BN)
    bm_h = _tile(seq, BM // 2)
    shard_pieces = lambda n: (lambda a_ref, b_ref: [(a_ref[g], b_ref[g]) for g in range(n)])
    ffn = _mm("mm_down", act, w_down_g, NN, (seq // bm_h, d // bn_d, 1),
              pl.BlockSpec((NDEV, bm_h, ns_ff), lambda i, j, k: (0, i, 0)), pl.BlockSpec((NDEV, ns_ff, bn_d), lambda i, j, k: (0, 0, j)),
              pl.BlockSpec((bm_h, bn_d), lambda i, j, k: (i, j)), (seq, d), F32, (bm_h, bn_d), pieces=shard_pieces(NDEV))
    dy, dffn, d_gate_f, loss_part = _loss_head(h1, ffn, tgt, gate_f, rt)
    loss = lax.psum(loss_part[0, 0], MESH_AXES)

    bl = _tile(seq, BK)
    gw_down = _mm("mm_dw_down", act, dffn, TN, (NDEV, d // bn_d, seq // bl),
                  pl.BlockSpec((None, bl, ns_ff), lambda i, j, k: (i, k, 0)), pl.BlockSpec((bl, bn_d), lambda i, j, k: (k, j)),
                  pl.BlockSpec((None, ns_ff, bn_d), lambda i, j, k: (i, 0, j)), (NDEV, ns_ff, d), BF16, (ns_ff, bn_d))
    rows_down = (ns_ff // 48) * 16
    dact, got_down = _mm(
        "mm_dact", dffn, w_down_g, NT, (seq // bm, NDEV, d // bk),
        pl.BlockSpec((bm, bk), lambda i, j, k: (i, k)), pl.BlockSpec((None, ns_ff, bk), lambda i, j, k: (j, 0, k)),
        pl.BlockSpec((None, bm, ns_ff), lambda i, j, k: (j, i, 0)), (NDEV, seq, ns_ff), BF16, (bm, ns_ff),
        riders=[_Rider(gw_down, False, part=(0, rows_down))])

    def dact_body(gu_ref, da_ref, dgu_ref):
        _, vjp = jax.vjp(_seg_act, gu_ref[0].astype(F32), gu_ref[1].astype(F32))
        dg, du_ = vjp((da_ref[...].astype(F32),))
        dgu_ref[0] = dg.astype(dgu_ref.dtype)
        dgu_ref[1] = du_.astype(dgu_ref.dtype)

    dgu4 = pl.pallas_call(dact_body, name="seg_act_bwd", grid=(NDEV, seq // ft), in_specs=[pair_spec, one_spec],
                          out_specs=pair_spec, out_shape=sds((2, NDEV, seq, ns_ff), BF16), compiler_params=_params(2))(gu4, dact)
    dgu = dgu4.reshape(2 * NDEV, seq, ns_ff)
    bmd = _tile(d, BM)

    def dw_half(name, which, riders):
        return _mm(name, xf, dgu, TN, (d // bmd, NDEV, seq // bl), pl.BlockSpec((bl, bmd), lambda i, j, k: (k, i)),
                   pl.BlockSpec((None, bl, ns_ff), lambda i, j, k: (which * NDEV + j, k, 0)),
                   pl.BlockSpec((None, bmd, ns_ff), lambda i, j, k: (j, i, 0)), (NDEV, d, ns_ff), BF16, (bmd, ns_ff), riders=riders)

    gw_gate, got_down = dw_half("mm_dw_gate", 0, [_Rider(gw_down, False, land=got_down, part=(rows_down, rows_down))])
    gw_up, got_down = dw_half("mm_dw_up", 1, [_Rider(gw_down, False, land=got_down,
                                                     part=(2 * rows_down, ns_ff - 2 * rows_down))])
    dxf = _mm(
        "mm_dxf", dgu, w_gu_g, NT, (seq // bm, d // bn_d, 4),
        pl.BlockSpec((4, bm, ns_ff), lambda i, j, k: (k, i, 0)), pl.BlockSpec((4, bn_d, ns_ff), lambda i, j, k: (k, j, 0)),
        pl.BlockSpec((bm, bn_d), lambda i, j, k: (i, j)), (seq, d), F32, (bm, bn_d), pieces=shard_pieces(4))
    (do, dx_a, d_gate_m, d_g_ffn, d_scale_f, d_shift_f) = _rowwise_bwd(
        "seg_mid_bwd", _seg_mid, [xs, o], [row_d] * 2, [gate_m, g_ffn, scale_f, shift_f], [vec_d] * 4,
        [dy, dxf], [row_d] * 2, [[0], [1]], [1, 0], [sds((seq, d), BF16), sds((seq, d), F32)], [row_d] * 2,
        [0, 1, 2, 3], [sds((1, d), F32)] * 4, [vec_d] * 4, n_rt)

    dmixed = _mm_plain("mm_dmixed", do, w_out_g, NT, F32)
    gw_out = _mm_plain("mm_dw_out", mixed, do, TN, BF16)
    (dz, dy1_a, dy_attn, d_g_ssm, d_g_attn) = _rowwise_bwd(
        "seg_mix_bwd", _seg_mix, [y1, z, y_attn], [row_s, row_s, row_a], [g_ssm_out, g_attn_out], [vec_s, vec_a],
        [dmixed], [row_m], [[0]], [1, 0, 2], [sds((seq, w_ssm), BF16), sds((seq, w_ssm), F32), sds((seq, w_attn), F32)],
        [row_s, row_s, row_a], [0, 1], [sds((1, w_ssm), F32), sds((1, w_attn), F32)], [vec_s, vec_a], n_rt)
    dy1_b = _mm_plain("mm_dy1", dz, w_glu_g, NT, F32)
    gw_glu = _mm_plain("mm_dw_glu", y1, dz, TN, BF16)
    (dypre,) = _rowwise_bwd("seg_gelu_bwd", _seg_gelu, [ypre], [row_s], [], [], [dy1_a, dy1_b], [row_s] * 2, [[0, 1]],
                            [0], [sds((seq, w_ssm), F32)], [row_s], [], [], [], n_rt)
    (du, db_blk_r, db_blk_i, dc_blk_r, dc_blk_i, dlam_r3, dlam_i3, dd_skip2), (got_gate,) = _s5_backward(
        dypre, proj, x_re, x_im, b_blk_r, b_blk_i, c_blk_r, c_blk_i, lam_r3, lam_i3, d_skip2, n_blk, t_chunk,
        riders=[_Rider(gw_gate, False)])
    (dqh, dkh, dv), (got_up, got_out, got_glu) = _attention_backward(
        qh, kh, proj, v_col, y_attn, dy_attn, n_pair, tq, tk,
        riders=[_Rider(gw_up, False), _Rider(gw_out.reshape(NDEV, w_out.shape[1], d), False),
                _Rider(gw_glu.reshape(NDEV, w_glu.shape[1], w_ssm), False)])
    (dq, dk, dqg_t, dkg_t) = _rowwise_bwd(
        "seg_qk_bwd", _seg_qk, qk_rows, qk_specs, [qg_t, kg_t], [vec_a] * 2, [dqh, dkh], [row_a] * 2, [[0], [1]],
        [0, 1], [sds((seq, w_attn), BF16)] * 2, [row_a] * 2, [0, 1], [sds((1, w_attn), F32)] * 2, [vec_a] * 2, n_rt)

    dbb_r = _from_b_blocks(db_blk_r, n_blk, p_state).reshape(gp, SSM_GROUP)
    dbb_i = _from_b_blocks(db_blk_i, n_blk, p_state).reshape(gp, SSM_GROUP)
    dcoef_r2, dcoef_i2, db_re2, db_im2 = _whole_vjp("s5_bbar_bwd", _s5_bbar, [coef_r2, coef_i2, b_re2, b_im2], [dbb_r, dbb_i],
                                                    [(gp, 1), (gp, 1), (gp, SSM_GROUP), (gp, SSM_GROUP)])
    lam_cts = [dlam_r3.reshape(n_groups, p_state), dlam_i3.reshape(n_groups, p_state),
               dcoef_r2.reshape(n_groups, p_state), dcoef_i2.reshape(n_groups, p_state)]
    da_re2, da_im2, dldt2 = _whole_vjp("s5_lam_bwd", _s5_lam, [a_re2, a_im2, ldt2], lam_cts,
                                       [(n_groups, p_state), (n_groups, p_state), (n_groups, 1)])
    dc_re2, dc_im2 = _from_c_blocks(dc_blk_r, n_blk, p_state), _from_c_blocks(dc_blk_i, n_blk, p_state)

    small_part = {
        "b_ada_b": jnp.concatenate([d_gate_m, d_shift_f, d_scale_f, d_gate_f], axis=-1),
        "a_re": da_re2, "a_im": da_im2, "log_dt": dldt2, "b_re": db_re2, "b_im": db_im2,
        "c_re": dc_re2, "c_im": dc_im2, "d_skip": dd_skip2,
        "q_gain": dqg_t.reshape(n_heads, HEAD_DIM).sum(0), "k_gain": dkg_t.reshape(n_heads, HEAD_DIM).sum(0),
        "g_ssm_out": d_g_ssm, "g_attn_out": d_g_attn, "g_ffn": d_g_ffn,
    }
    dproj = jnp.concatenate([du, dq, dk, dv], axis=-1)
    gw_in, early_parts = _mm(
        "mm_dw_in", xm, dproj, TN, (d // bmd, NDEV * per, seq // bl),
        pl.BlockSpec((bl, bmd), lambda i, j, k: (k, i)), pl.BlockSpec((bl, bn_in), lambda i, j, k: (k, j)),
        pl.BlockSpec((None, bmd, bn_in), lambda i, j, k: (j // per, i, j % per)), (NDEV, d, ns_in), BF16, (bmd, bn_in),
        riders=[_Rider(_pack([small_part[n] for n in SMALL_EARLY]), True, relayed=True)])
    rows_first = (9 * d) // 16
    dxm, got_in = _mm(
        "mm_dxm", dproj, w_in_g, NT, (seq // bm, d // bn_d, 1),
        pl.BlockSpec((bm, NDEV * ns_in), lambda i, j, k: (i, 0)), pl.BlockSpec((NDEV, bn_d, ns_in), lambda i, j, k: (0, j, 0)),
        pl.BlockSpec((bm, bn_d), lambda i, j, k: (i, j)), (seq, d), F32, (bm, bn_d),
        riders=[_Rider(gw_in, False, part=(0, rows_first))],
        pieces=lambda a_ref, b_ref: [(a_ref[:, g * ns_in:(g + 1) * ns_in], b_ref[g]) for g in range(NDEV)])
    (grad_x, d_shift_m, d_scale_m, d_g_mix), (got_in,) = _rowwise_bwd(
        "seg_in_bwd", _seg_in, [xs], [row_d], [shift_m, scale_m, g_mix], [vec_d] * 3, [dxm, dx_a], [row_d] * 2, [[0], [1]],
        [0], [sds((seq, d), F32)], [row_d], [0, 1, 2], [sds((1, d), F32)] * 3, [vec_d] * 3, n_rt,
        riders=[_Rider(gw_in, False, land=got_in, part=(rows_first, d - rows_first))])
    small_part["b_ada_a"] = jnp.concatenate([d_shift_m, d_scale_m], axis=-1)
    small_part["g_mix"] = d_g_mix
    late_parts = _exchange(_pack([small_part[n] for n in SMALL_LATE]), True, "comm_ag_small_late")
    packed_parts = jnp.concatenate([late_parts, early_parts], axis=1)

    big = {}

    def sharded(nm, got, width, tr):
        big[nm] = _adam_sum("adam_" + nm, got, pl.BlockSpec((NDEV, tr, width), lambda i: (0, i, 0)), given[nm][0],
                            given["m_" + nm][0], given["v_" + nm][0], tr)

    sharded("w_down", got_down, d, _tile(w_down.shape[1], 64))
    sharded("w_gate", got_gate, ns_ff, _tile(d, 256))
    sharded("w_up", got_up, ns_ff, _tile(d, 256))
    sharded("w_out", got_out, d, _tile(w_out.shape[1], 128))
    sharded("w_glu", got_glu, w_ssm, _tile(w_glu.shape[1], 128))
    sharded("w_in", got_in, ns_in, _tile(d, 256))

    split = dict(given)
    for pre in ("", "m_", "v_"):
        split[pre + "b_ada_a"], split[pre + "b_ada_b"] = given[pre + "b_ada"][:, :2 * d], given[pre + "b_ada"][:, 2 * d:]
    packs = [jnp.concatenate([_pack([split[pre + n] for n in SMALL_LATE]), _pack([split[pre + n] for n in SMALL_EARLY])])
             for pre in ("", "m_", "v_")]
    rows_p = packed_parts.shape[1]
    tr_p = _tile(rows_p, 64)
    sm = _adam_sum("adam_small", packed_parts, pl.BlockSpec((NDEV, tr_p, PACK_COLS), lambda i: (0, i, 0)), *packs, tr_p)
    rows_late = late_parts.shape[1]
    small_out = []
    for t in sm:
        out = dict(zip(SMALL_LATE, _unpack(t[:rows_late], [split[n] for n in SMALL_LATE])))
        out.update(zip(SMALL_EARLY, _unpack(t[rows_late:], [split[n] for n in SMALL_EARLY])))
        out["b_ada"] = jnp.concatenate([out["b_ada_a"], out["b_ada_b"]], axis=1)
        small_out.append(out)

    rows_a, rows_b = (2 * d) // PACK_COLS, (4 * d) // PACK_COLS
    assert rows_a * PACK_COLS == 2 * d
    dmod_all = jnp.concatenate([late_parts[:, :rows_a].reshape(NDEV, 2 * d), early_parts[:, :rows_b].reshape(NDEV, 4 * d)], axis=1)
    dmod_cols = lax.dynamic_slice(dmod_all, (0, me * n_ada), (NDEV, n_ada))
    big["w_ada"] = _adam_ada(c_all, dmod_cols, w_ada[0], m_w_ada[0], v_w_ada[0])

    order = ("w_ada", "b_ada", "g_mix", "w_in", "a_re", "a_im", "log_dt", "b_re", "b_im", "c_re", "c_im", "d_skip", "w_glu",
             "q_gain", "k_gain", "g_ssm_out", "g_attn_out", "w_out", "g_ffn", "w_gate", "w_up", "w_down")
    outs = [loss, grad_x[None]]
    for kind in range(4):
        for n in order:
            outs.append(big[n][kind][None] if n in big else small_out[kind][n])
    return tuple(outs)
```

```python
import functools
import math

import jax
import jax.numpy as jnp
from jax import lax
from jax.experimental import pallas as pl
from jax.experimental.pallas import tpu as pltpu

F32 = jnp.float32
BF16 = jnp.bfloat16
NDEV = 8
MESH_AXES = ("x", "y", "c")
MESH_ID = pl.DeviceIdType.MESH
EPS = 1e-6
LANES = 128
SUBLANES = 8
HEAD_DIM = 64
SSM_GROUP = 16
ADAM_LR, ADAM_B1, ADAM_B2, ADAM_EPS, ADAM_WD, ADAM_STEP = 0.001, 0.9, 0.999, 1e-08, 0.01, 10

NN = (((1,), (0,)), ((), ()))
NT = (((1,), (1,)), ((), ()))
TN = (((0,), (0,)), ((), ()))


def _dot(a, b, dn=NN):
    return lax.dot_general(a, b, dn, preferred_element_type=F32)


def _tile(dim, pref):
    t = min(dim, pref)
    while dim % t:
        t //= 2
    return t


def _params(n):
    return pltpu.CompilerParams(dimension_semantics=("arbitrary",) * n)


def _me():
    mx, my, mc = lax.axis_index("x"), lax.axis_index("y"), lax.axis_index("c")
    return mx, my, mc


def _peer(mx, my, mc, k):
    px = 1 - mx if (k >> 2) & 1 else mx
    py = 1 - my if (k >> 1) & 1 else my
    pc = 1 - mc if k & 1 else mc
    return (px, py, pc), 4 * px + 2 * py + pc


def _slot(ref, idx, part):
    return ref.at[idx] if part is None else ref.at[idx, pl.ds(*part)]


def _exchange_copies(x_ref, land_ref, send_sems, recv_sems, gather, part=None):
    mx, my, mc = _me()
    me = 4 * mx + 2 * my + mc
    pairs = []
    for k in range(1, NDEV):
        peer, pidx = _peer(mx, my, mc, k)
        src = x_ref if gather else _slot(x_ref, pidx, part)
        mk = lambda dst, src=src, k=k, peer=peer: pltpu.make_async_remote_copy(
            src_ref=src, dst_ref=dst, send_sem=send_sems.at[k - 1], recv_sem=recv_sems.at[k - 1],
            device_id=peer, device_id_type=MESH_ID)
        pairs.append((mk(_slot(land_ref, me, part)), mk(_slot(land_ref, pidx, part))))
    return me, pairs


def _exchange(x, gather, name, relayed=False):
    def body(x_ref, o_ref, send_sems, recv_sems, local_sem):
        if relayed:
            phases = _relayed_gather_phases(x_ref, o_ref, send_sems, recv_sems, local_sem)
        else:
            phases = _direct_phases(x_ref, o_ref, send_sems, recv_sems, local_sem, gather)
        for phase in phases:
            if phase is not None:
                phase()

    return pl.pallas_call(
        body, name=name, out_shape=jax.ShapeDtypeStruct(((NDEV,) + x.shape) if gather else x.shape, x.dtype),
        in_specs=[pl.BlockSpec(memory_space=pl.ANY)], out_specs=pl.BlockSpec(memory_space=pl.ANY),
        scratch_shapes=[pltpu.SemaphoreType.DMA((NDEV - 1,)), pltpu.SemaphoreType.DMA((NDEV - 1,)), pltpu.SemaphoreType.DMA],
    )(x)


def _direct_phases(x_ref, zone, send_sems, recv_sems, local_sem, gather, part=None):
    me, pairs = _exchange_copies(x_ref, zone, send_sems, recv_sems, gather, part)
    local = pltpu.make_async_copy(x_ref if gather else _slot(x_ref, me, part), _slot(zone, me, part), local_sem)

    def start():
        for send, _ in pairs:
            send.start()
        local.start()

    def finish():
        for send, arrival in pairs:
            send.wait_send()
            arrival.wait_recv()
        local.wait()

    return start, None, finish


def _relayed_gather_phases(x_ref, zone, send_sems, recv_sems, local_sem):
    mx, my, mc = _me()
    me, sibling = (mx, my, mc), (mx, my, 1 - mc)
    chips = [(1 - mx, my), (mx, 1 - my), (1 - mx, 1 - my)]
    rows = lambda dev: zone.at[4 * dev[0] + 2 * dev[1] + dev[2]]

    def copy(k, block, to, src=None):
        return pltpu.make_async_remote_copy(src_ref=rows(block) if src is None else src, dst_ref=rows(block),
                                            send_sem=send_sems.at[k], recv_sem=recv_sems.at[k], device_id=to,
                                            device_id_type=MESH_ID)

    local = pltpu.make_async_copy(x_ref, rows(me), local_sem)
    first = [copy(0, me, sibling, x_ref)] + [copy(1 + j, me, (*chip, mc), x_ref) for j, chip in enumerate(chips)]
    passed = [copy(4 + j, (*chip, mc), sibling) for j, chip in enumerate(chips)]
    over_links = [copy(1 + j, (*chip, mc), me) for j, chip in enumerate(chips)]
    from_sibling = [copy(0, sibling, me)] + [copy(4 + j, (*chip, 1 - mc), me) for j, chip in enumerate(chips)]

    def start():
        local.start()
        for cp in first:
            cp.start()

    def relay():
        for arrival, onward in zip(over_links, passed):
            arrival.wait_recv()
            onward.start()

    def finish():
        for arrival in from_sibling:
            arrival.wait_recv()
        for cp in first + passed:
            cp.wait_send()
        local.wait()

    return start, relay, finish


class _Rider:
    def __init__(self, x, gather, land=None, slot=None, n_slots=None, relayed=False, part=None):
        self.x, self.gather, self.land, self.slot, self.relayed, self.part = x, gather, land, slot, relayed, part
        own = ((NDEV,) + x.shape) if gather else x.shape
        self.land_shape = land.shape if land is not None else (own if n_slots is None else (n_slots,) + own)

    def phases(self, x_ref, land_ref, send_sems, recv_sems, local_sem):
        zone = land_ref if self.slot is None else land_ref.at[self.slot]
        if self.relayed:
            return _relayed_gather_phases(x_ref, zone, send_sems, recv_sems, local_sem)
        return _direct_phases(x_ref, zone, send_sems, recv_sems, local_sem, self.gather, self.part)


def _ride(call_name, grid, riders, inner, in_specs, out_specs, out_shape, scratch_shapes, compiler_params, operands):
    n_in, n_out, n_scr = len(in_specs), len(out_specs), len(scratch_shapes)
    any_spec = pl.BlockSpec(memory_space=pl.ANY)
    extra_in, aliases = [], {}
    for r_idx, r in enumerate(riders):
        extra_in.append(r.x)
        if r.land is not None:
            aliases[n_in + len(extra_in)] = n_out + r_idx
            extra_in.append(r.land)
    sems = []
    for _ in riders:
        sems += [pltpu.SemaphoreType.DMA((NDEV - 1,)), pltpu.SemaphoreType.DMA((NDEV - 1,)), pltpu.SemaphoreType.DMA]

    def body(*refs):
        base_in, rest = refs[:n_in], refs[n_in:]
        rider_in, rest = rest[:len(extra_in)], rest[len(extra_in):]
        base_out, rest = rest[:n_out], rest[n_out:]
        lands, rest = rest[:len(riders)], rest[len(riders):]
        base_scr, rider_sems = rest[:n_scr], rest[n_scr:]
        step = 0
        for a, g in enumerate(grid):
            step = step * g + pl.program_id(a)
        n_steps = math.prod(grid)
        sets, pos = [], 0
        for r_idx, r in enumerate(riders):
            x_ref = rider_in[pos]
            pos += 2 if r.land is not None else 1
            sets.append(r.phases(x_ref, lands[r_idx], *rider_sems[3 * r_idx:3 * r_idx + 3]))

        if sets:
            @pl.when(step == 0)
            def _():
                for start, _, _ in sets:
                    start()

        inner(*base_in, *base_out, *base_scr)

        if any(relay is not None for _, relay, _ in sets):
            @pl.when(step == (3 * n_steps) // 5)
            def _():
                for _, relay, _ in sets:
                    if relay is not None:
                        relay()

        if sets:
            @pl.when(step == n_steps - 1)
            def _():
                for _, _, finish in sets:
                    finish()

    outs = pl.pallas_call(
        body, name=call_name, grid=grid, in_specs=list(in_specs) + [any_spec] * len(extra_in),
        out_specs=list(out_specs) + [any_spec] * len(riders),
        out_shape=list(out_shape) + [jax.ShapeDtypeStruct(r.land_shape, r.x.dtype) for r in riders],
        scratch_shapes=list(scratch_shapes) + sems, input_output_aliases=aliases, compiler_params=compiler_params,
    )(*operands, *extra_in)
    return outs[:n_out], outs[n_out:]


def _mm(name, a, b, dn, grid, a_spec, b_spec, o_spec, out_shape, out_dtype, acc_shape, riders=(), pieces=None):
    nk = grid[2]

    def body(a_ref, b_ref, o_ref, *scratch):
        if pieces is None:
            part = _dot(a_ref[...].astype(BF16), b_ref[...].astype(BF16), dn)
        else:
            part = functools.reduce(lambda p, q: p + q, [_dot(a_g.astype(BF16), b_g.astype(BF16), dn)
                                                         for a_g, b_g in pieces(a_ref, b_ref)])
        if nk == 1:
            o_ref[...] = part.astype(o_ref.dtype)
            return
        acc_ref = scratch[0]
        k = pl.program_id(2)

        @pl.when(k == 0)
        def _():
            acc_ref[...] = part

        @pl.when(k > 0)
        def _():
            acc_ref[...] += part

        @pl.when(k == nk - 1)
        def _():
            o_ref[...] = acc_ref[...].astype(o_ref.dtype)

    (out,), lands = _ride(name, grid, riders, body, [a_spec, b_spec], [o_spec], [jax.ShapeDtypeStruct(out_shape, out_dtype)],
                          [] if nk == 1 else [pltpu.VMEM(acc_shape, F32)], _params(3), [a, b])
    return (out, *lands) if riders else out


BM, BN, BK = 1024, 1024, 4096


def _mm_plain(name, a, b, dn, out_dtype):
    if dn == NN:
        (m, kk), n = a.shape, b.shape[1]
    elif dn == NT:
        (m, kk), n = a.shape, b.shape[0]
    else:
        (kk, m), n = a.shape, b.shape[1]
    half = 2 if dn == TN else 1
    bm, bn, bk = _tile(m, BM // half), _tile(n, BN // half), _tile(kk, BK)
    a_spec = pl.BlockSpec((bk, bm), lambda i, j, k: (k, i)) if dn == TN else pl.BlockSpec((bm, bk), lambda i, j, k: (i, k))
    b_spec = pl.BlockSpec((bn, bk), lambda i, j, k: (j, k)) if dn == NT else pl.BlockSpec((bk, bn), lambda i, j, k: (k, j))
    return _mm(name, a, b, dn, (m // bm, n // bn, kk // bk), a_spec, b_spec,
               pl.BlockSpec((bm, bn), lambda i, j, k: (i, j)), (m, n), out_dtype, (bm, bn))


def _row_spec(tile, width, col=0):
    return pl.BlockSpec((tile, width), lambda i: (i, col))


def _vec_spec(width, col=0):
    return pl.BlockSpec((1, width), lambda i: (0, col))


def _rowwise_fwd(name, fn, rows, row_specs, vecs, vec_specs, out_shapes, out_specs, n_tiles, riders=()):
    nr, nv = len(rows), len(vecs)

    def body(*refs):
        ins = [r[...].astype(F32) for r in refs[:nr + nv]]
        outs = fn(*ins)
        for o_ref, o in zip(refs[nr + nv:], outs):
            o_ref[...] = o.astype(o_ref.dtype)

    outs, lands = _ride(name, (n_tiles,), riders, body, list(row_specs) + list(vec_specs), list(out_specs),
                        list(out_shapes), [], _params(1), [*rows, *vecs])
    return (outs, lands) if riders else outs


def _rowwise_bwd(name, fn, rows, row_specs, vecs, vec_specs, cts, ct_specs, ct_groups,
                 drow_idx, drow_shapes, drow_specs, dvec_idx, dvec_shapes, dvec_specs, n_tiles, riders=()):
    nr, nv, nc = len(rows), len(vecs), len(cts)

    def body(*refs):
        ins = [r[...].astype(F32) for r in refs[:nr + nv]]
        ct_vals = [r[...].astype(F32) for r in refs[nr + nv:nr + nv + nc]]
        out_refs = refs[nr + nv + nc:]
        _, vjp = jax.vjp(fn, *ins)
        grads = vjp(tuple(functools.reduce(lambda p, q: p + q, [ct_vals[j] for j in grp]) for grp in ct_groups))
        for o_ref, idx in zip(out_refs[:len(drow_idx)], drow_idx):
            o_ref[...] = grads[idx].astype(o_ref.dtype)
        step = pl.program_id(0)
        for o_ref, idx in zip(out_refs[len(drow_idx):], dvec_idx):
            @pl.when(step == 0)
            def _(o_ref=o_ref):
                o_ref[...] = jnp.zeros_like(o_ref)
            o_ref[...] += grads[nr + idx]

    outs, lands = _ride(name, (n_tiles,), riders, body, list(row_specs) + list(vec_specs) + list(ct_specs),
                        list(drow_specs) + list(dvec_specs), list(drow_shapes) + list(dvec_shapes), [], _params(1),
                        [*rows, *vecs, *cts])
    return (outs, lands) if riders else outs


def _rms(x):
    return x * lax.rsqrt(jnp.mean(x * x, axis=-1, keepdims=True) + EPS)


def _seg_in(x, shift, scale, gain):
    return _rms(x) * gain * (1.0 + scale) + shift, x


def _seg_qk(q, k, qg, kg):
    def norm(t, g, mult):
        blocks = []
        lane = lax.broadcasted_iota(jnp.int32, (1, LANES), 1)
        for p in range(t.shape[1] // LANES):
            tb = t[:, p * LANES:(p + 1) * LANES]
            sq = tb * tb
            lo = jnp.sum(jnp.where(lane < HEAD_DIM, sq, 0.0), axis=-1, keepdims=True)
            hi = jnp.sum(jnp.where(lane < HEAD_DIM, 0.0, sq), axis=-1, keepdims=True)
            ms = jnp.where(lane < HEAD_DIM, lo, hi) * (1.0 / HEAD_DIM)
            blocks.append(tb * lax.rsqrt(ms + EPS) * (g[:, p * LANES:(p + 1) * LANES] * mult))
        return jnp.concatenate(blocks, axis=-1) if len(blocks) > 1 else blocks[0]
    return norm(q, qg, 1.0 / math.sqrt(HEAD_DIM)), norm(k, kg, 1.0)


def _seg_gelu(ypre):
    return (jax.nn.gelu(ypre),)


def _seg_mix(y1, z, yattn, g_ssm, g_attn):
    ys = y1 * jax.nn.sigmoid(z)
    return (jnp.concatenate([_rms(ys) * g_ssm, _rms(yattn) * g_attn], axis=-1),)


def _seg_mid(x, o, gate_m, g_ffn, scale_f, shift_f):
    h1 = x + gate_m * o
    return h1, _rms(h1) * g_ffn * (1.0 + scale_f) + shift_f


def _seg_act(gate, up):
    return (jax.nn.silu(gate) * up,)


def _s5_lam(a_re, a_im, log_dt):
    dt = jnp.exp(log_dt)
    mag = jnp.exp(a_re * dt)
    lr, li = mag * jnp.cos(a_im * dt), mag * jnp.sin(a_im * dt)
    den = a_re * a_re + a_im * a_im
    nr, ni = lr - 1.0, li
    return lr, li, (nr * a_re + ni * a_im) / den, (ni * a_re - nr * a_im) / den


def _s5_bbar(coef_re, coef_im, b_re, b_im):
    return coef_re * b_re - coef_im * b_im, coef_re * b_im + coef_im * b_re


def _whole(name, fn, ins, out_shapes):
    n = len(ins)

    def body(*refs):
        outs = fn(*[r[...] for r in refs[:n]])
        for o_ref, o in zip(refs[n:], outs):
            o_ref[...] = o

    return pl.pallas_call(body, name=name, out_shape=[jax.ShapeDtypeStruct(s, F32) for s in out_shapes])(*ins)


def _whole_vjp(name, fn, ins, cts, out_shapes):
    n, nc = len(ins), len(cts)

    def body(*refs):
        _, vjp = jax.vjp(fn, *[r[...] for r in refs[:n]])
        grads = vjp(tuple(r[...] for r in refs[n:n + nc]))
        for o_ref, g in zip(refs[n + nc:], grads):
            o_ref[...] = g

    return pl.pallas_call(body, name=name, out_shape=[jax.ShapeDtypeStruct(s, F32) for s in out_shapes])(*ins, *cts)


SCAN_SHIFTS = (1, 2, 4)


def _cmul(ar, ai, br, bi):
    return ar * br - ai * bi, ar * bi + ai * br


def _scan_coefs(lr, li, reverse):
    s = lr.shape[1]
    row = lax.broadcasted_iota(jnp.int32, (SUBLANES, s), 0)
    p1 = (lr, li)
    p2 = _cmul(*p1, *p1)
    p4 = _cmul(*p2, *p2)
    p8 = _cmul(*p4, *p4)
    p3, p5, p6 = _cmul(*p1, *p2), _cmul(*p4, *p1), _cmul(*p4, *p2)
    p7 = _cmul(*p6, *p1)
    pows = (p1, p2, p3, p4, p5, p6, p7, p8)
    bc = lambda t: jnp.broadcast_to(t, (SUBLANES, s))
    steps = []
    for sh, pw in zip(SCAN_SHIFTS, (p1, p2, p4)):
        keep = (row + sh <= SUBLANES - 1) if reverse else (row >= sh)
        steps.append((jnp.where(keep, bc(pw[0]), 0.0), jnp.where(keep, bc(pw[1]), 0.0)))
    cr, ci = jnp.zeros((SUBLANES, s), F32), jnp.zeros((SUBLANES, s), F32)
    for r in range(SUBLANES):
        pw = pows[SUBLANES - 1 - r] if reverse else pows[r]
        cr = jnp.where(row == r, bc(pw[0]), cr)
        ci = jnp.where(row == r, bc(pw[1]), ci)
    return steps, (cr, ci)


def _scan_tile(xr, xi, steps, carry_pow, cr, ci, reverse):
    for sh, (ar, ai) in zip(SCAN_SHIFTS, steps):
        rs = SUBLANES - sh if reverse else sh
        sr, si = pltpu.roll(xr, rs, 0), pltpu.roll(xi, rs, 0)
        xr, xi = xr + ar * sr - ai * si, xi + ar * si + ai * sr
    pr, pi = carry_pow
    return xr + pr * cr - pi * ci, xi + pr * ci + pi * cr


def _s5_forward(proj, b_blk_re, b_blk_im, c_blk_re, c_blk_im, lam_re, lam_im, d_skip, n_blk, t_chunk, riders=()):
    seq = proj.shape[0]
    n_chunks = seq // t_chunk
    n_tiles = t_chunk // SUBLANES
    s = b_blk_re.shape[2]

    def body(u_ref, bre_ref, bim_ref, cre_ref, cim_ref, lr_ref, li_ref, d_ref, y_ref, xr_ref, xi_ref, wr, wi, carry):
        t = pl.program_id(1)

        @pl.when(t == 0)
        def _():
            carry[...] = jnp.zeros_like(carry)

        u = u_ref[...]
        ub = u.astype(BF16)
        wr[...] = _dot(ub, bre_ref[...])
        wi[...] = _dot(ub, bim_ref[...])
        steps, cpow = _scan_coefs(lr_ref[...], li_ref[...], False)

        def tile(i, c):
            r0 = pl.multiple_of(i * SUBLANES, SUBLANES)
            xr, xi = _scan_tile(wr[pl.ds(r0, SUBLANES), :], wi[pl.ds(r0, SUBLANES), :], steps, cpow, c[0], c[1], False)
            xr_ref[pl.ds(r0, SUBLANES), :] = xr
            xi_ref[pl.ds(r0, SUBLANES), :] = xi
            last = SUBLANES - 1
            return (jnp.broadcast_to(xr[last:, :], xr.shape), jnp.broadcast_to(xi[last:, :], xi.shape))

        cr, ci = lax.fori_loop(0, n_tiles, tile, (carry[0], carry[1]))
        carry[0] = cr
        carry[1] = ci
        y = _dot(xr_ref[...].astype(BF16), cre_ref[...]) - _dot(xi_ref[...].astype(BF16), cim_ref[...])
        y_ref[...] = y + d_ref[...] * u

    blk = lambda shape: pl.BlockSpec((None,) + shape, lambda j, t: (j, 0, 0))
    return _ride(
        "s5_fwd", (n_blk, n_chunks), riders, body,
        [pl.BlockSpec((t_chunk, LANES), lambda j, t: (t, j)), blk((LANES, s)), blk((LANES, s)),
         blk((s, LANES)), blk((s, LANES)), blk((1, s)), blk((1, s)), pl.BlockSpec((1, LANES), lambda j, t: (0, j))],
        [pl.BlockSpec((t_chunk, LANES), lambda j, t: (t, j)), pl.BlockSpec((t_chunk, s), lambda j, t: (t, j)),
         pl.BlockSpec((t_chunk, s), lambda j, t: (t, j))],
        [jax.ShapeDtypeStruct((seq, n_blk * LANES), F32), jax.ShapeDtypeStruct((seq, n_blk * s), F32),
         jax.ShapeDtypeStruct((seq, n_blk * s), F32)],
        [pltpu.VMEM((t_chunk, s), F32), pltpu.VMEM((t_chunk, s), F32), pltpu.VMEM((2, SUBLANES, s), F32)],
        _params(2), [proj, b_blk_re, b_blk_im, c_blk_re, c_blk_im, lam_re, lam_im, d_skip])


def _s5_backward(dypre, proj, x_re, x_im, b_blk_re, b_blk_im, c_blk_re, c_blk_im, lam_re, lam_im, d_skip, n_blk, t_chunk,
                 riders=()):
    seq = proj.shape[0]
    n_chunks = seq // t_chunk
    n_tiles = t_chunk // SUBLANES
    s = b_blk_re.shape[2]

    def body(dy_ref, u_ref, xr_ref, xi_ref, pr_ref, pi_ref, bre_ref, bim_ref, cre_ref, cim_ref, lr_ref, li_ref, d_ref,
             du_ref, dbre_ref, dbim_ref, dcre_ref, dcim_ref, dlr_ref, dli_ref, dd_ref, gr, gi, carry):
        t = pl.program_id(1)

        @pl.when(t == 0)
        def _():
            carry[...] = jnp.zeros_like(carry)
            for r in (dbre_ref, dbim_ref, dcre_ref, dcim_ref, dlr_ref, dli_ref, dd_ref):
                r[...] = jnp.zeros_like(r)

        dy = dy_ref[...]
        dyb = dy.astype(BF16)
        u = u_ref[...]
        gr[...] = _dot(dyb, cre_ref[...], NT)
        gi[...] = -_dot(dyb, cim_ref[...], NT)
        steps, cpow = _scan_coefs(lr_ref[...], -li_ref[...], True)
        row = lax.broadcasted_iota(jnp.int32, (SUBLANES, s), 0)
        last = SUBLANES - 1
        first_chunk = t == n_chunks - 1

        def tile_at(r0, prev_r, prev_i, c):
            cr, ci, ar, ai = c
            lr_, li_ = _scan_tile(gr[pl.ds(r0, SUBLANES), :], gi[pl.ds(r0, SUBLANES), :], steps, cpow, cr, ci, True)
            gr[pl.ds(r0, SUBLANES), :] = lr_
            gi[pl.ds(r0, SUBLANES), :] = li_
            xr, xi = xr_ref[pl.ds(r0, SUBLANES), :], xi_ref[pl.ds(r0, SUBLANES), :]
            xpr = jnp.where(row == 0, jnp.broadcast_to(prev_r[last:, :], xr.shape), pltpu.roll(xr, 1, 0))
            xpi = jnp.where(row == 0, jnp.broadcast_to(prev_i[last:, :], xi.shape), pltpu.roll(xi, 1, 0))
            ar = ar + lr_ * xpr + li_ * xpi
            ai = ai + li_ * xpr - lr_ * xpi
            return (jnp.broadcast_to(lr_[:1, :], lr_.shape), jnp.broadcast_to(li_[:1, :], li_.shape), ar, ai)

        def tile(ii, c):
            i = n_tiles - 1 - ii
            r0 = pl.multiple_of(i * SUBLANES, SUBLANES)
            rp = pl.multiple_of(r0 - SUBLANES, SUBLANES)
            return tile_at(r0, xr_ref[pl.ds(rp, SUBLANES), :], xi_ref[pl.ds(rp, SUBLANES), :], c)

        zero = jnp.zeros((SUBLANES, s), F32)
        c = lax.fori_loop(0, n_tiles - 1, tile, (carry[0], carry[1], zero, zero))
        keep = jnp.where(first_chunk, 0.0, 1.0)
        c = tile_at(0, pr_ref[...] * keep, pi_ref[...] * keep, c)
        carry[0] = c[0]
        carry[1] = c[1]
        dlr_ref[...] += jnp.sum(c[2], axis=0, keepdims=True)
        dli_ref[...] += jnp.sum(c[3], axis=0, keepdims=True)

        lam_r, lam_i = gr[...].astype(BF16), gi[...].astype(BF16)
        du_ref[...] = (_dot(lam_r, bre_ref[...], NT) + _dot(lam_i, bim_ref[...], NT) + d_ref[...] * dy).astype(du_ref.dtype)
        ub = u.astype(BF16)
        dbre_ref[...] += _dot(ub, lam_r, TN)
        dbim_ref[...] += _dot(ub, lam_i, TN)
        dcre_ref[...] += _dot(xr_ref[...].astype(BF16), dyb, TN)
        dcim_ref[...] -= _dot(xi_ref[...].astype(BF16), dyb, TN)
        dd_ref[...] += jnp.sum(dy * u, axis=0, keepdims=True)

    rev = lambda t: n_chunks - 1 - t
    blk = lambda shape: pl.BlockSpec((None,) + shape, lambda j, t: (j, 0, 0))
    tpc = t_chunk // SUBLANES
    prev_spec = pl.BlockSpec((SUBLANES, s), lambda j, t: (jnp.maximum(rev(t) * tpc - 1, 0), j))
    chunk = lambda w: pl.BlockSpec((t_chunk, w), lambda j, t: (rev(t), j))
    return _ride(
        "s5_bwd", (n_blk, n_chunks), riders, body,
        [chunk(LANES), chunk(LANES), chunk(s), chunk(s), prev_spec, prev_spec, blk((LANES, s)), blk((LANES, s)),
         blk((s, LANES)), blk((s, LANES)), blk((1, s)), blk((1, s)), pl.BlockSpec((1, LANES), lambda j, t: (0, j))],
        [chunk(LANES), blk((LANES, s)), blk((LANES, s)), blk((s, LANES)), blk((s, LANES)), blk((1, s)), blk((1, s)),
         pl.BlockSpec((1, LANES), lambda j, t: (0, j))],
        [jax.ShapeDtypeStruct((seq, n_blk * LANES), BF16),
         jax.ShapeDtypeStruct((n_blk, LANES, s), F32), jax.ShapeDtypeStruct((n_blk, LANES, s), F32),
         jax.ShapeDtypeStruct((n_blk, s, LANES), F32), jax.ShapeDtypeStruct((n_blk, s, LANES), F32),
         jax.ShapeDtypeStruct((n_blk, 1, s), F32), jax.ShapeDtypeStruct((n_blk, 1, s), F32),
         jax.ShapeDtypeStruct((1, n_blk * LANES), F32)],
        [pltpu.VMEM((t_chunk, s), F32), pltpu.VMEM((t_chunk, s), F32), pltpu.VMEM((2, SUBLANES, s), F32)],
        _params(2), [dypre, proj, x_re, x_im, x_re, x_im, b_blk_re, b_blk_im, c_blk_re, c_blk_im, lam_re, lam_im, d_skip])


TQ, TK = 256, 128


def _split_bf16(x):
    hi = x.astype(BF16)
    return hi, (x - hi.astype(F32)).astype(BF16)


def _sb_weights(z, past, carry, tri):
    ls = jnp.minimum(z, 0.0) - jnp.log(1.0 + jnp.exp(-jnp.abs(z)))
    lk = ls - z
    if past is not None:
        lk = jnp.where(past, lk, 0.0)
    w = jnp.exp(ls + _dot(lk.astype(BF16), tri) + carry)
    if past is not None:
        w = jnp.where(past, w, 0.0)
    return ls, lk, w


LOG_KEEP_DEAD = -104.0


def _walk_key_blocks(i, ratio, prologue, block, epilogue, log_keep):
    n_kb = (i + 1) * ratio
    prologue(n_kb - 1)
    for n in range(ratio):
        block(n_kb - 1 - n, n % 2, True)
    assert ratio % 2 == 0
    n_pairs = (i * ratio) // 2

    def more(state):
        t, alive = state
        return jnp.logical_and(t < n_pairs, alive)

    def pair(state):
        t, _ = state
        j = n_kb - 1 - ratio - 2 * t
        block(j, ratio % 2, False)
        block(j - 1, (ratio + 1) % 2, False)
        return t + 1, log_keep() >= LOG_KEEP_DEAD

    done, _ = lax.while_loop(more, pair, (jnp.int32(0), log_keep() >= LOG_KEEP_DEAD))
    epilogue(n_kb - ratio - 2 * done)


def _attention_forward(qh, kh, proj, v_col, n_pair, tq, tk, riders=()):
    seq = qh.shape[0]
    ratio = tq // tk

    def body(q_ref, k_ref, v_ref, o_ref, q_scr, z_scr, w_scr, acc_scr, c_scr):
        i = pl.program_id(1)
        lane = lax.broadcasted_iota(jnp.int32, (1, LANES), 1)
        q2 = q_ref[...]
        q_scr[0] = jnp.where(lane < HEAD_DIM, q2, 0.0).astype(BF16)
        q_scr[1] = jnp.where(lane < HEAD_DIM, 0.0, q2).astype(BF16)
        tri = (lax.broadcasted_iota(jnp.int32, (tk, tk), 0) > lax.broadcasted_iota(jnp.int32, (tk, tk), 1)).astype(BF16)
        qpos = i * tq + lax.broadcasted_iota(jnp.int32, (tq, tk), 0)
        kidx = lax.broadcasted_iota(jnp.int32, (tq, tk), 1)

        def rows(ref, j):
            j = jnp.clip(j, 0, seq // tk - 1)
            return ref[pl.ds(pl.multiple_of(j * tk, tk), tk), :].astype(BF16)

        def scores(j, slot):
            kb = rows(k_ref, j)
            for h in range(2):
                z_scr[slot, h] = _dot(q_scr[h], kb, NT)

        def finish(j):
            vb = rows(v_ref, j)
            for h in range(2):
                acc_scr[h] += _dot(w_scr[h], vb)

        def prologue(j):
            w_scr[...] = jnp.zeros_like(w_scr)
            acc_scr[...] = jnp.zeros_like(acc_scr)
            c_scr[...] = jnp.zeros_like(c_scr)
            scores(j, 0)

        def block(j, slot, masked):
            scores(j - 1, 1 - slot)
            finish(j + 1)
            past = ((kidx + j * tk) < qpos) if masked else None
            for h in range(2):
                _, lk, w = _sb_weights(z_scr[slot, h], past, c_scr[h], tri)
                w_scr[h] = w.astype(BF16)
                c_scr[h] += jnp.sum(lk, axis=-1, keepdims=True)

        _walk_key_blocks(i, ratio, prologue, block, finish, lambda: jnp.max(c_scr[...]))
        o_ref[...] = jnp.where(lane < HEAD_DIM, acc_scr[0], acc_scr[1])

    (out,), lands = _ride(
        "attn_fwd", (n_pair, seq // tq), riders, body,
        [pl.BlockSpec((tq, LANES), lambda p, i: (i, p)), pl.BlockSpec((seq, LANES), lambda p, i: (0, p)),
         pl.BlockSpec((seq, LANES), lambda p, i: (0, v_col + p))],
        [pl.BlockSpec((tq, LANES), lambda p, i: (i, p))], [jax.ShapeDtypeStruct(qh.shape, F32)],
        [pltpu.VMEM((2, tq, LANES), BF16), pltpu.VMEM((2, 2, tq, tk), F32), pltpu.VMEM((2, tq, tk), BF16),
         pltpu.VMEM((2, tq, LANES), F32), pltpu.VMEM((2, tq, 1), F32)],
        _params(2), [qh, kh, proj])
    return out, lands


def _attention_backward(qh, kh, proj, v_col, y, dy, n_pair, tq, tk, riders=()):
    seq = qh.shape[0]

    ratio = tq // tk

    n_kblk = seq // tk

    def body(q_ref, k_ref, v_ref, y_ref, dy_ref, dq_ref, dk_ref, dv_ref,
             q_scr, do_scr, qt_scr, dot_scr, dkt_scr, dvt_scr, z_scr, dw_scr, w_scr, dz_scr, dq_scr, c_scr, c2_scr, tot_scr):
        i = pl.program_id(1)

        @pl.when(i == 0)
        def _():
            dkt_scr[...] = jnp.zeros_like(dkt_scr)
            dvt_scr[...] = jnp.zeros_like(dvt_scr)

        lane = lax.broadcasted_iota(jnp.int32, (1, LANES), 1)
        sel = (lane < HEAD_DIM, lane >= HEAD_DIM)
        q2, do2 = q_ref[...], dy_ref[...].astype(BF16)
        do2f = do2.astype(F32)
        dot_oy = do2f * y_ref[...]
        for h in range(2):
            qm, dm = jnp.where(sel[h], q2, 0.0), jnp.where(sel[h], do2f, 0.0)
            q_scr[h] = qm.astype(BF16)
            do_scr[h] = dm.astype(BF16)
            qt_scr[h] = qm.T.astype(BF16)
            dot_scr[h] = dm.T.astype(BF16)
            tot_scr[h] = jnp.sum(jnp.where(sel[h], dot_oy, 0.0), axis=-1, keepdims=True)
        r_i, c_i = lax.broadcasted_iota(jnp.int32, (tk, tk), 0), lax.broadcasted_iota(jnp.int32, (tk, tk), 1)
        tri = (r_i > c_i).astype(BF16)
        tri_ge = (r_i >= c_i).astype(BF16)
        qpos = i * tq + lax.broadcasted_iota(jnp.int32, (tq, tk), 0)
        kidx = lax.broadcasted_iota(jnp.int32, (tq, tk), 1)

        def start(j):
            return pl.multiple_of(jnp.clip(j, 0, seq // tk - 1) * tk, tk)

        def scores(j, slot):
            c0 = start(j)
            kb, vb = k_ref[pl.ds(c0, tk), :].astype(BF16), v_ref[pl.ds(c0, tk), :].astype(BF16)
            for h in range(2):
                z_scr[slot, h] = _dot(q_scr[h], kb, NT)
                dw_scr[slot, h] = _dot(do_scr[h], vb, NT)

        def finish(j):
            jc = jnp.clip(j, 0, n_kblk - 1)
            kb = k_ref[pl.ds(pl.multiple_of(jc * tk, tk), tk), :].astype(BF16)
            dkt_add, dvt_add = jnp.zeros((LANES, tk), F32), jnp.zeros((LANES, tk), F32)
            for h in range(2):
                dz = dz_scr[h]
                dq_scr[h] += _dot(dz, kb)
                dkt_add = dkt_add + _dot(qt_scr[h], dz)
                dvt_add = dvt_add + _dot(dot_scr[h], w_scr[h])
            dkt_scr[jc] += dkt_add
            dvt_scr[jc] += dvt_add

        def prologue(j):
            for r in (w_scr, dz_scr, dq_scr, c_scr, c2_scr):
                r[...] = jnp.zeros_like(r)
            scores(j, 0)

        def block(j, slot, masked):
            scores(j - 1, 1 - slot)
            finish(j + 1)
            past = ((kidx + j * tk) < qpos) if masked else None
            for h in range(2):
                ls, lk, w = _sb_weights(z_scr[slot, h], past, c_scr[h], tri)
                wb = w.astype(BF16)
                dlw = dw_scr[slot, h] * wb.astype(F32)
                hi, lo = _split_bf16(dlw)
                dlk = tot_scr[h] - c2_scr[h] - (_dot(hi, tri_ge) + _dot(lo, tri_ge))
                if masked:
                    dlk = jnp.where(past, dlk, 0.0)
                sig = jnp.exp(ls)
                w_scr[h] = wb
                dz_scr[h] = (dlw * (1.0 - sig) - dlk * sig).astype(BF16)
                c_scr[h] += jnp.sum(lk, axis=-1, keepdims=True)
                c2_scr[h] += jnp.sum(dlw, axis=-1, keepdims=True)

        _walk_key_blocks(i, ratio, prologue, block, finish, lambda: jnp.max(c_scr[...]))
        dq_ref[...] = jnp.where(sel[0], dq_scr[0], dq_scr[1])

        @pl.when(i == seq // tq - 1)
        def _():
            for jb in range(n_kblk):
                dk_ref[jb * tk:(jb + 1) * tk, :] = dkt_scr[jb].T
                dv_ref[jb * tk:(jb + 1) * tk, :] = dvt_scr[jb].T.astype(dv_ref.dtype)

    blk = pl.BlockSpec((tq, LANES), lambda p, i: (i, p))
    full = pl.BlockSpec((seq, LANES), lambda p, i: (0, p))
    shape = jax.ShapeDtypeStruct(qh.shape, F32)
    return _ride(
        "attn_bwd", (n_pair, seq // tq), riders, body,
        [blk, full, pl.BlockSpec((seq, LANES), lambda p, i: (0, v_col + p)), blk, blk],
        [blk, full, full], [shape, shape, jax.ShapeDtypeStruct(qh.shape, BF16)],
        [pltpu.VMEM((2, tq, LANES), BF16), pltpu.VMEM((2, tq, LANES), BF16),
         pltpu.VMEM((2, LANES, tq), BF16), pltpu.VMEM((2, LANES, tq), BF16),
         pltpu.VMEM((n_kblk, LANES, tk), F32), pltpu.VMEM((n_kblk, LANES, tk), F32),
         pltpu.VMEM((2, 2, tq, tk), F32), pltpu.VMEM((2, 2, tq, tk), F32),
         pltpu.VMEM((2, tq, tk), BF16), pltpu.VMEM((2, tq, tk), BF16), pltpu.VMEM((2, tq, LANES), F32),
         pltpu.VMEM((2, tq, 1), F32), pltpu.VMEM((2, tq, 1), F32), pltpu.VMEM((2, tq, 1), F32)],
        _params(2), [qh, kh, proj, y, dy])


def _loss_head(h1, ffn, target, gate_f, tile):
    seq, d = h1.shape

    def body(h_ref, f_ref, t_ref, g_ref, dy_ref, df_ref, dg_ref, loss_ref):
        @pl.when(pl.program_id(0) == 0)
        def _():
            dg_ref[...] = jnp.zeros_like(dg_ref)
            loss_ref[...] = jnp.zeros_like(loss_ref)

        f, g = f_ref[...], g_ref[...]
        err = h_ref[...] + g * f - t_ref[...]
        dy = err * (1.0 / d)
        dy_ref[...] = dy
        df_ref[...] = (dy * g).astype(df_ref.dtype)
        dg_ref[...] += jnp.sum(dy * f, axis=0, keepdims=True)
        loss_ref[...] += jnp.sum(jnp.sum(err * err, axis=-1, keepdims=True), axis=0, keepdims=True) * (0.5 / d)

    row = _row_spec(tile, d)
    return pl.pallas_call(
        body, name="loss_head", grid=(seq // tile,), in_specs=[row, row, row, _vec_spec(d)],
        out_specs=[row, row, _vec_spec(d), pl.BlockSpec((1, 1), lambda i: (0, 0))],
        out_shape=[jax.ShapeDtypeStruct((seq, d), F32), jax.ShapeDtypeStruct((seq, d), BF16),
                   jax.ShapeDtypeStruct((1, d), F32), jax.ShapeDtypeStruct((1, 1), F32)],
        compiler_params=_params(1),
    )(h1, ffn, target, gate_f)


def _dot3(a, b, dn):
    ah, al = _split_bf16(a)
    bh, bl = _split_bf16(b)
    return _dot(ah, bh, dn) + (_dot(ah, bl, dn) + _dot(al, bh, dn))


def _ada_forward(c_all, w_shard, b_cols):
    d, n = w_shard.shape
    bk = _tile(d, 512)

    def body(c_ref, w_ref, b_ref, o_ref):
        @pl.when(pl.program_id(0) == 0)
        def _():
            o_ref[...] = jnp.broadcast_to(b_ref[...], o_ref.shape)

        o_ref[...] += _dot3(jax.nn.silu(c_ref[...]), w_ref[...], NN)

    return pl.pallas_call(
        body, name="ada_fwd", grid=(d // bk,),
        in_specs=[pl.BlockSpec((NDEV, bk), lambda k: (0, k)), pl.BlockSpec((bk, n), lambda k: (k, 0)), _vec_spec(n)],
        out_specs=pl.BlockSpec((NDEV, n), lambda k: (0, 0)), out_shape=jax.ShapeDtypeStruct((NDEV, n), F32),
        compiler_params=_params(1),
    )(c_all, w_shard, b_cols)


def _adam(w, g, m, v):
    m = ADAM_B1 * m + (1.0 - ADAM_B1) * g
    v = ADAM_B2 * v + (1.0 - ADAM_B2) * (g * g)
    m_hat = m / (1.0 - ADAM_B1 ** ADAM_STEP)
    v_hat = v / (1.0 - ADAM_B2 ** ADAM_STEP)
    return -ADAM_LR * (m_hat / (jnp.sqrt(v_hat) + ADAM_EPS) + ADAM_WD * w), m, v


def _adam_ada(c_all, dmod_cols, w, m, v):
    d, n = w.shape
    tr = _tile(d, 256)

    def body(c_ref, dm_ref, w_ref, m_ref, v_ref, g_ref, dl_ref, nm_ref, nv_ref):
        g = _dot3(jax.nn.silu(c_ref[...]), dm_ref[...], TN)
        delta, nm, nv = _adam(w_ref[...], g, m_ref[...], v_ref[...])
        g_ref[...] = g
        dl_ref[...] = delta
        nm_ref[...] = nm
        nv_ref[...] = nv

    row = _row_spec(tr, n)
    return pl.pallas_call(
        body, name="adam_ada", grid=(d // tr,),
        in_specs=[pl.BlockSpec((NDEV, tr), lambda i: (0, i)), pl.BlockSpec((NDEV, n), lambda i: (0, 0)), row, row, row],
        out_specs=[row] * 4, out_shape=[jax.ShapeDtypeStruct((d, n), F32)] * 4, compiler_params=_params(1),
    )(c_all, dmod_cols, w, m, v)


def _adam_sum(name, parts, part_spec, w, m, v, tr):
    r, c = w.shape

    def body(p_ref, w_ref, m_ref, v_ref, g_ref, dl_ref, nm_ref, nv_ref):
        g = p_ref[0].astype(F32)
        for k in range(1, NDEV):
            g = g + p_ref[k].astype(F32)
        delta, nm, nv = _adam(w_ref[...], g, m_ref[...], v_ref[...])
        g_ref[...] = g
        dl_ref[...] = delta
        nm_ref[...] = nm
        nv_ref[...] = nv

    row = _row_spec(tr, c)
    return pl.pallas_call(
        body, name=name, grid=(r // tr,), in_specs=[part_spec, row, row, row],
        out_specs=[row] * 4, out_shape=[jax.ShapeDtypeStruct((r, c), F32)] * 4, compiler_params=_params(1),
    )(parts, w, m, v)


GROUPS_PER_BLOCK = LANES // SSM_GROUP


def _to_b_blocks(bb, n_blk, p):
    t = bb.reshape(n_blk, GROUPS_PER_BLOCK, p, SSM_GROUP)
    eye = jnp.eye(GROUPS_PER_BLOCK, dtype=bb.dtype)
    return jnp.einsum("jgph,gk->jghkp", t, eye).reshape(n_blk, LANES, GROUPS_PER_BLOCK * p)


def _from_b_blocks(blk, n_blk, p):
    t = blk.reshape(n_blk, GROUPS_PER_BLOCK, SSM_GROUP, GROUPS_PER_BLOCK, p)
    eye = jnp.eye(GROUPS_PER_BLOCK, dtype=blk.dtype)
    return jnp.einsum("jghkp,gk->jgph", t, eye).reshape(n_blk * GROUPS_PER_BLOCK, p, SSM_GROUP)


def _to_c_blocks(cc, n_blk, p):
    t = cc.reshape(n_blk, GROUPS_PER_BLOCK, SSM_GROUP, p)
    eye = jnp.eye(GROUPS_PER_BLOCK, dtype=cc.dtype)
    return jnp.einsum("jghp,gk->jgpkh", t, eye).reshape(n_blk, GROUPS_PER_BLOCK * p, LANES)


def _from_c_blocks(blk, n_blk, p):
    t = blk.reshape(n_blk, GROUPS_PER_BLOCK, p, GROUPS_PER_BLOCK, SSM_GROUP)
    eye = jnp.eye(GROUPS_PER_BLOCK, dtype=blk.dtype)
    return jnp.einsum("jgpkh,gk->jghp", t, eye).reshape(n_blk * GROUPS_PER_BLOCK, SSM_GROUP, p)


SMALL_LATE = ("b_ada_a", "g_mix")
SMALL_EARLY = ("b_ada_b", "a_re", "a_im", "log_dt", "b_re", "b_im", "c_re", "c_im", "d_skip",
               "q_gain", "k_gain", "g_ssm_out", "g_attn_out", "g_ffn")
PACK_COLS = 1024


def _pack(arrs):
    flat = jnp.concatenate([a.reshape(-1) for a in arrs])
    n = flat.shape[0]
    quantum = SUBLANES * PACK_COLS
    padded = -(-n // quantum) * quantum
    return jnp.pad(flat, (0, padded - n)).reshape(padded // PACK_COLS, PACK_COLS)


def _unpack(packed, like):
    flat, out, off = packed.reshape(-1), [], 0
    for a in like:
        out.append(flat[off:off + a.size].reshape(a.shape))
        off += a.size
    return out


def kernel(x, c, w_ada, b_ada, g_mix, w_in, a_re, a_im, log_dt, b_re, b_im, c_re, c_im, d_skip, w_glu, q_gain, k_gain, g_ssm_out, g_attn_out, w_out, g_ffn, w_gate, w_up, w_down, loss_target, m_w_ada, m_b_ada, m_g_mix, m_w_in, m_a_re, m_a_im, m_log_dt, m_b_re, m_b_im, m_c_re, m_c_im, m_d_skip, m_w_glu, m_q_gain, m_k_gain, m_g_ssm_out, m_g_attn_out, m_w_out, m_g_ffn, m_w_gate, m_w_up, m_w_down, v_w_ada, v_b_ada, v_g_mix, v_w_in, v_a_re, v_a_im, v_log_dt, v_b_re, v_b_im, v_c_re, v_c_im, v_d_skip, v_w_glu, v_q_gain, v_k_gain, v_g_ssm_out, v_g_attn_out, v_w_out, v_g_ffn, v_w_gate, v_w_up, v_w_down):
    given = dict(locals())
    seq, d = x.shape[1], x.shape[2]
    xs, tgt = x[0], loss_target[0]
    n_groups, p_state = a_re.shape[1], a_re.shape[2]
    w_ssm = n_groups * SSM_GROUP
    w_attn = w_in.shape[2] * NDEV - w_ssm
    w_attn //= 3
    n_blk, n_pair = w_ssm // LANES, w_attn // LANES
    n_heads = w_attn // HEAD_DIM
    ns_in, ns_ff = w_in.shape[2], w_gate.shape[2]
    d_mix = w_ssm + w_attn
    mx, my, mc = _me()
    me = 4 * mx + 2 * my + mc
    rt = _tile(seq, 256)
    n_rt = seq // rt
    sds = jax.ShapeDtypeStruct

    c_all = _exchange(c, True, "comm_ag_c").reshape(NDEV, d)
    n_ada = w_ada.shape[2]
    b_cols = lax.dynamic_slice(b_ada, (0, me * n_ada), (1, n_ada))
    mod_cols = _ada_forward(c_all, w_ada[0], b_cols)
    mod_all = _exchange(mod_cols, True, "comm_ag_mod")
    mod = lax.dynamic_slice(mod_all, (0, me, 0), (NDEV, 1, n_ada)).reshape(1, NDEV * n_ada)
    shift_m, scale_m, gate_m, shift_f, scale_f, gate_f = [mod[:, i * d:(i + 1) * d] for i in range(6)]

    gp = n_groups * p_state
    a_re2, a_im2, ldt2 = a_re[0], a_im[0], log_dt[0].reshape(n_groups, 1)
    b_re2, b_im2 = b_re[0].reshape(gp, SSM_GROUP), b_im[0].reshape(gp, SSM_GROUP)
    lam_r, lam_i, coef_r, coef_i = _whole("s5_lam", _s5_lam, [a_re2, a_im2, ldt2], [(n_groups, p_state)] * 4)
    coef_r2, coef_i2 = coef_r.reshape(gp, 1), coef_i.reshape(gp, 1)
    bb_r, bb_i = _whole("s5_bbar", _s5_bbar, [coef_r2, coef_i2, b_re2, b_im2], [(gp, SSM_GROUP)] * 2)
    s_blk = GROUPS_PER_BLOCK * p_state
    b_blk_r = _to_b_blocks(bb_r.reshape(n_groups, p_state, SSM_GROUP), n_blk, p_state).astype(BF16)
    b_blk_i = _to_b_blocks(bb_i.reshape(n_groups, p_state, SSM_GROUP), n_blk, p_state).astype(BF16)
    c_blk_r = _to_c_blocks(c_re[0], n_blk, p_state).astype(BF16)
    c_blk_i = _to_c_blocks(c_im[0], n_blk, p_state).astype(BF16)
    lam_r3, lam_i3 = lam_r.reshape(n_blk, 1, s_blk), lam_i.reshape(n_blk, 1, s_blk)
    d_skip2 = d_skip[0].reshape(1, w_ssm)

    row_d, vec_d = _row_spec(rt, d), _vec_spec(d)
    (xm,), (w_in_g,) = _rowwise_fwd("seg_in", lambda *a: _seg_in(*a)[:1], [xs], [row_d], [shift_m, scale_m, g_mix], [vec_d] * 3,
                                    [sds((seq, d), BF16)], [row_d], n_rt,
                                    riders=[_Rider(w_in[0].astype(BF16), True, relayed=True)])
    bn_in = _tile(ns_in, 512)
    per = ns_in // bn_in
    bm, bk = _tile(seq, BM), _tile(d, BK)
    proj, w_glu_g, w_out_g = _mm(
        "mm_in", xm, w_in_g, NN, (seq // bm, NDEV * per, d // bk),
        pl.BlockSpec((bm, bk), lambda i, j, k: (i, k)),
        pl.BlockSpec((None, bk, bn_in), lambda i, j, k: (j // per, k, j % per)),
        pl.BlockSpec((bm, bn_in), lambda i, j, k: (i, j)), (seq, NDEV * ns_in), F32, (bm, bn_in),
        riders=[_Rider(w_glu[0].astype(BF16), True, relayed=True), _Rider(w_out[0].astype(BF16), True, relayed=True)])
    w_glu_g, w_out_g = w_glu_g.reshape(w_ssm, w_ssm), w_out_g.reshape(d_mix, d)
    q_col, k_col, v_col = w_ssm // w_attn, w_ssm // w_attn + 1, (w_ssm + 2 * w_attn) // LANES
    qg_t, kg_t = jnp.tile(q_gain, (1, n_heads)), jnp.tile(k_gain, (1, n_heads))
    row_a, vec_a = _row_spec(rt, w_attn), _vec_spec(w_attn)
    qk_rows, qk_specs = [proj, proj], [_row_spec(rt, w_attn, q_col), _row_spec(rt, w_attn, k_col)]
    qh, kh = _rowwise_fwd("seg_qk", _seg_qk, qk_rows, qk_specs, [qg_t, kg_t], [vec_a] * 2,
                          [sds((seq, w_attn), F32)] * 2, [row_a] * 2, n_rt)
    t_chunk = _tile(seq, 1024)
    (ypre, x_re, x_im), (w_gu_land,) = _s5_forward(
        proj, b_blk_r, b_blk_i, c_blk_r, c_blk_i, lam_r3, lam_i3, d_skip2, n_blk, t_chunk,
        riders=[_Rider(w_up[0].astype(BF16), True, slot=1, n_slots=2, relayed=True)])
    tq, tk = _tile(seq, TQ), _tile(seq, TK)
    y_attn, (w_gu_land,) = _attention_forward(qh, kh, proj, v_col, n_pair, tq, tk,
                                              riders=[_Rider(w_gate[0].astype(BF16), True, land=w_gu_land, slot=0,
                                                             relayed=True)])
    w_gu_g = w_gu_land.reshape(2 * NDEV, d, ns_ff)
    row_s, vec_s = _row_spec(rt, w_ssm), _vec_spec(w_ssm)
    y1, = _rowwise_fwd("seg_gelu", _seg_gelu, [ypre], [row_s], [], [], [sds((seq, w_ssm), F32)], [row_s], n_rt)
    z = _mm_plain("mm_glu", y1, w_glu_g, NN, F32)
    row_m = _row_spec(rt, d_mix)
    mixed, = _rowwise_fwd("seg_mix", _seg_mix, [y1, z, y_attn], [row_s, row_s, row_a], [g_ssm_out, g_attn_out], [vec_s, vec_a],
                          [sds((seq, d_mix), BF16)], [row_m], n_rt)
    o = _mm_plain("mm_out", mixed, w_out_g, NN, F32)
    h1, xf = _rowwise_fwd("seg_mid", _seg_mid, [xs, o], [row_d] * 2, [gate_m, g_ffn, scale_f, shift_f], [vec_d] * 4,
                          [sds((seq, d), F32), sds((seq, d), BF16)], [row_d] * 2, n_rt)
    gu, w_down_g = _mm(
        "mm_gu", xf, w_gu_g, NN, (seq // bm, 2 * NDEV, d // bk),
        pl.BlockSpec((bm, bk), lambda i, j, k: (i, k)), pl.BlockSpec((None, bk, ns_ff), lambda i, j, k: (j, k, 0)),
        pl.BlockSpec((None, bm, ns_ff), lambda i, j, k: (j, i, 0)), (2 * NDEV, seq, ns_ff), BF16, (bm, ns_ff),
        riders=[_Rider(w_down[0].astype(BF16), True, relayed=True)])
    gu4 = gu.reshape(2, NDEV, seq, ns_ff)
    ft = _tile(seq, 512)
    pair_spec = pl.BlockSpec((2, None, ft, ns_ff), lambda s, i: (0, s, i, 0))
    one_spec = pl.BlockSpec((None, ft, ns_ff), lambda s, i: (s, i, 0))

    def act_body(gu_ref, a_ref):
        a_ref[...] = _seg_act(gu_ref[0].astype(F32), gu_ref[1].astype(F32))[0].astype(a_ref.dtype)

    act = pl.pallas_call(act_body, name="seg_act", grid=(NDEV, seq // ft), in_specs=[pair_spec], out_specs=one_spec,
                         out_shape=sds((NDEV, seq, ns_ff), BF16), compiler_params=_params(2))(gu4)
    bn_d = _tile(d, BN)
    bm_h = _tile(seq, BM // 2)
    shard_pieces = lambda n: (lambda a_ref, b_ref: [(a_ref[g], b_ref[g]) for g in range(n)])
    ffn = _mm("mm_down", act, w_down_g, NN, (seq // bm_h, d // bn_d, 1),
              pl.BlockSpec((NDEV, bm_h, ns_ff), lambda i, j, k: (0, i, 0)), pl.BlockSpec((NDEV, ns_ff, bn_d), lambda i, j, k: (0, 0, j)),
              pl.BlockSpec((bm_h, bn_d), lambda i, j, k: (i, j)), (seq, d), F32, (bm_h, bn_d), pieces=shard_pieces(NDEV))
    dy, dffn, d_gate_f, loss_part = _loss_head(h1, ffn, tgt, gate_f, rt)
    loss = lax.psum(loss_part[0, 0], MESH_AXES)

    bl = _tile(seq, BK)
    gw_down = _mm("mm_dw_down", act, dffn, TN, (NDEV, d // bn_d, seq // bl),
                  pl.BlockSpec((None, bl, ns_ff), lambda i, j, k: (i, k, 0)), pl.BlockSpec((bl, bn_d), lambda i, j, k: (k, j)),
                  pl.BlockSpec((None, ns_ff, bn_d), lambda i, j, k: (i, 0, j)), (NDEV, ns_ff, d), BF16, (ns_ff, bn_d))
    rows_down = (ns_ff // 48) * 16
    dact, got_down = _mm(
        "mm_dact", dffn, w_down_g, NT, (seq // bm, NDEV, d // bk),
        pl.BlockSpec((bm, bk), lambda i, j, k: (i, k)), pl.BlockSpec((None, ns_ff, bk), lambda i, j, k: (j, 0, k)),
        pl.BlockSpec((None, bm, ns_ff), lambda i, j, k: (j, i, 0)), (NDEV, seq, ns_ff), BF16, (bm, ns_ff),
        riders=[_Rider(gw_down, False, part=(0, rows_down))])

    n_ft, ring = seq // ft, 3
    n_act_steps = NDEV * n_ft

    def dact_body(gu_hbm, da_hbm, dgu_ref, gu_buf, da_buf, sems):
        step = pl.program_id(0) * n_ft + pl.program_id(1)

        def reads(st):
            shard, r0, slot = st // n_ft, (st % n_ft) * ft, st % ring
            r0 = r0 if isinstance(r0, int) else pl.multiple_of(r0, ft)
            return (pltpu.make_async_copy(gu_hbm.at[:, shard, pl.ds(r0, ft), :], gu_buf.at[slot], sems.at[0, slot]),
                    pltpu.make_async_copy(da_hbm.at[shard, pl.ds(r0, ft), :], da_buf.at[slot], sems.at[1, slot]))

        @pl.when(step == 0)
        def _():
            for first in range(min(ring - 1, n_act_steps)):
                for cp in reads(first):
                    cp.start()

        @pl.when(step + ring - 1 < n_act_steps)
        def _():
            for cp in reads(step + ring - 1):
                cp.start()

        for cp in reads(step):
            cp.wait()
        slot = step % ring
        _, vjp = jax.vjp(_seg_act, gu_buf[slot, 0].astype(F32), gu_buf[slot, 1].astype(F32))
        dg, du_ = vjp((da_buf[slot].astype(F32),))
        dgu_ref[0] = dg.astype(dgu_ref.dtype)
        dgu_ref[1] = du_.astype(dgu_ref.dtype)

    any_spec = pl.BlockSpec(memory_space=pl.ANY)
    dgu4 = pl.pallas_call(dact_body, name="seg_act_bwd", grid=(NDEV, n_ft), in_specs=[any_spec, any_spec],
                          out_specs=pair_spec, out_shape=sds((2, NDEV, seq, ns_ff), BF16),
                          scratch_shapes=[pltpu.VMEM((ring, 2, ft, ns_ff), BF16), pltpu.VMEM((ring, ft, ns_ff), BF16),
                                          pltpu.SemaphoreType.DMA((2, ring))],
                          compiler_params=_params(2))(gu4, dact)
    dgu = dgu4.reshape(2 * NDEV, seq, ns_ff)
    bmd = _tile(d, BM)

    def dw_half(name, which, riders):
        return _mm(name, xf, dgu, TN, (d // bmd, NDEV, seq // bl), pl.BlockSpec((bl, bmd), lambda i, j, k: (k, i)),
                   pl.BlockSpec((None, bl, ns_ff), lambda i, j, k: (which * NDEV + j, k, 0)),
                   pl.BlockSpec((None, bmd, ns_ff), lambda i, j, k: (j, i, 0)), (NDEV, d, ns_ff), BF16, (bmd, ns_ff), riders=riders)

    gw_gate, got_down = dw_half("mm_dw_gate", 0, [_Rider(gw_down, False, land=got_down, part=(rows_down, rows_down))])
    gw_up, got_down = dw_half("mm_dw_up", 1, [_Rider(gw_down, False, land=got_down,
                                                     part=(2 * rows_down, ns_ff - 2 * rows_down))])
    dxf = _mm(
        "mm_dxf", dgu, w_gu_g, NT, (seq // bm, d // bn_d, 4),
        pl.BlockSpec((4, bm, ns_ff), lambda i, j, k: (k, i, 0)), pl.BlockSpec((4, bn_d, ns_ff), lambda i, j, k: (k, j, 0)),
        pl.BlockSpec((bm, bn_d), lambda i, j, k: (i, j)), (seq, d), F32, (bm, bn_d), pieces=shard_pieces(4))
    (do, dx_a, d_gate_m, d_g_ffn, d_scale_f, d_shift_f) = _rowwise_bwd(
        "seg_mid_bwd", _seg_mid, [xs, o], [row_d] * 2, [gate_m, g_ffn, scale_f, shift_f], [vec_d] * 4,
        [dy, dxf], [row_d] * 2, [[0], [1]], [1, 0], [sds((seq, d), BF16), sds((seq, d), F32)], [row_d] * 2,
        [0, 1, 2, 3], [sds((1, d), F32)] * 4, [vec_d] * 4, n_rt)

    dmixed = _mm_plain("mm_dmixed", do, w_out_g, NT, F32)
    gw_out = _mm_plain("mm_dw_out", mixed, do, TN, BF16)
    (dz, dy1_a, dy_attn, d_g_ssm, d_g_attn) = _rowwise_bwd(
        "seg_mix_bwd", _seg_mix, [y1, z, y_attn], [row_s, row_s, row_a], [g_ssm_out, g_attn_out], [vec_s, vec_a],
        [dmixed], [row_m], [[0]], [1, 0, 2], [sds((seq, w_ssm), BF16), sds((seq, w_ssm), F32), sds((seq, w_attn), F32)],
        [row_s, row_s, row_a], [0, 1], [sds((1, w_ssm), F32), sds((1, w_attn), F32)], [vec_s, vec_a], n_rt)
    dy1_b = _mm_plain("mm_dy1", dz, w_glu_g, NT, F32)
    gw_glu = _mm_plain("mm_dw_glu", y1, dz, TN, BF16)
    (dypre,) = _rowwise_bwd("seg_gelu_bwd", _seg_gelu, [ypre], [row_s], [], [], [dy1_a, dy1_b], [row_s] * 2, [[0, 1]],
                            [0], [sds((seq, w_ssm), F32)], [row_s], [], [], [], n_rt)
    (du, db_blk_r, db_blk_i, dc_blk_r, dc_blk_i, dlam_r3, dlam_i3, dd_skip2), (got_gate,) = _s5_backward(
        dypre, proj, x_re, x_im, b_blk_r, b_blk_i, c_blk_r, c_blk_i, lam_r3, lam_i3, d_skip2, n_blk, t_chunk,
        riders=[_Rider(gw_gate, False)])
    (dqh, dkh, dv), (got_up, got_out, got_glu) = _attention_backward(
        qh, kh, proj, v_col, y_attn, dy_attn, n_pair, tq, tk,
        riders=[_Rider(gw_up, False), _Rider(gw_out.reshape(NDEV, w_out.shape[1], d), False),
                _Rider(gw_glu.reshape(NDEV, w_glu.shape[1], w_ssm), False)])
    (dq, dk, dqg_t, dkg_t) = _rowwise_bwd(
        "seg_qk_bwd", _seg_qk, qk_rows, qk_specs, [qg_t, kg_t], [vec_a] * 2, [dqh, dkh], [row_a] * 2, [[0], [1]],
        [0, 1], [sds((seq, w_attn), BF16)] * 2, [row_a] * 2, [0, 1], [sds((1, w_attn), F32)] * 2, [vec_a] * 2, n_rt)

    dbb_r = _from_b_blocks(db_blk_r, n_blk, p_state).reshape(gp, SSM_GROUP)
    dbb_i = _from_b_blocks(db_blk_i, n_blk, p_state).reshape(gp, SSM_GROUP)
    dcoef_r2, dcoef_i2, db_re2, db_im2 = _whole_vjp("s5_bbar_bwd", _s5_bbar, [coef_r2, coef_i2, b_re2, b_im2], [dbb_r, dbb_i],
                                                    [(gp, 1), (gp, 1), (gp, SSM_GROUP), (gp, SSM_GROUP)])
    lam_cts = [dlam_r3.reshape(n_groups, p_state), dlam_i3.reshape(n_groups, p_state),
               dcoef_r2.reshape(n_groups, p_state), dcoef_i2.reshape(n_groups, p_state)]
    da_re2, da_im2, dldt2 = _whole_vjp("s5_lam_bwd", _s5_lam, [a_re2, a_im2, ldt2], lam_cts,
                                       [(n_groups, p_state), (n_groups, p_state), (n_groups, 1)])
    dc_re2, dc_im2 = _from_c_blocks(dc_blk_r, n_blk, p_state), _from_c_blocks(dc_blk_i, n_blk, p_state)

    small_part = {
        "b_ada_b": jnp.concatenate([d_gate_m, d_shift_f, d_scale_f, d_gate_f], axis=-1),
        "a_re": da_re2, "a_im": da_im2, "log_dt": dldt2, "b_re": db_re2, "b_im": db_im2,
        "c_re": dc_re2, "c_im": dc_im2, "d_skip": dd_skip2,
        "q_gain": dqg_t.reshape(n_heads, HEAD_DIM).sum(0), "k_gain": dkg_t.reshape(n_heads, HEAD_DIM).sum(0),
        "g_ssm_out": d_g_ssm, "g_attn_out": d_g_attn, "g_ffn": d_g_ffn,
    }
    dproj = jnp.concatenate([du, dq, dk, dv], axis=-1)
    gw_in, early_parts = _mm(
        "mm_dw_in", xm, dproj, TN, (d // bmd, NDEV * per, seq // bl),
        pl.BlockSpec((bl, bmd), lambda i, j, k: (k, i)), pl.BlockSpec((bl, bn_in), lambda i, j, k: (k, j)),
        pl.BlockSpec((None, bmd, bn_in), lambda i, j, k: (j // per, i, j % per)), (NDEV, d, ns_in), BF16, (bmd, bn_in),
        riders=[_Rider(_pack([small_part[n] for n in SMALL_EARLY]), True, relayed=True)])
    rows_first = (9 * d) // 16
    dxm, got_in = _mm(
        "mm_dxm", dproj, w_in_g, NT, (seq // bm, d // bn_d, 1),
        pl.BlockSpec((bm, NDEV * ns_in), lambda i, j, k: (i, 0)), pl.BlockSpec((NDEV, bn_d, ns_in), lambda i, j, k: (0, j, 0)),
        pl.BlockSpec((bm, bn_d), lambda i, j, k: (i, j)), (seq, d), F32, (bm, bn_d),
        riders=[_Rider(gw_in, False, part=(0, rows_first))],
        pieces=lambda a_ref, b_ref: [(a_ref[:, g * ns_in:(g + 1) * ns_in], b_ref[g]) for g in range(NDEV)])
    (grad_x, d_shift_m, d_scale_m, d_g_mix), (got_in,) = _rowwise_bwd(
        "seg_in_bwd", _seg_in, [xs], [row_d], [shift_m, scale_m, g_mix], [vec_d] * 3, [dxm, dx_a], [row_d] * 2, [[0], [1]],
        [0], [sds((seq, d), F32)], [row_d], [0, 1, 2], [sds((1, d), F32)] * 3, [vec_d] * 3, n_rt,
        riders=[_Rider(gw_in, False, land=got_in, part=(rows_first, d - rows_first))])
    small_part["b_ada_a"] = jnp.concatenate([d_shift_m, d_scale_m], axis=-1)
    small_part["g_mix"] = d_g_mix
    late_parts = _exchange(_pack([small_part[n] for n in SMALL_LATE]), True, "comm_ag_small_late")
    packed_parts = jnp.concatenate([late_parts, early_parts], axis=1)

    big = {}

    def sharded(nm, got, width, tr):
        big[nm] = _adam_sum("adam_" + nm, got, pl.BlockSpec((NDEV, tr, width), lambda i: (0, i, 0)), given[nm][0],
                            given["m_" + nm][0], given["v_" + nm][0], tr)

    sharded("w_down", got_down, d, _tile(w_down.shape[1], 64))
    sharded("w_gate", got_gate, ns_ff, _tile(d, 256))
    sharded("w_up", got_up, ns_ff, _tile(d, 256))
    sharded("w_out", got_out, d, _tile(w_out.shape[1], 128))
    sharded("w_glu", got_glu, w_ssm, _tile(w_glu.shape[1], 128))
    sharded("w_in", got_in, ns_in, _tile(d, 256))

    split = dict(given)
    for pre in ("", "m_", "v_"):
        split[pre + "b_ada_a"], split[pre + "b_ada_b"] = given[pre + "b_ada"][:, :2 * d], given[pre + "b_ada"][:, 2 * d:]
    packs = [jnp.concatenate([_pack([split[pre + n] for n in SMALL_LATE]), _pack([split[pre + n] for n in SMALL_EARLY])])
             for pre in ("", "m_", "v_")]
    rows_p = packed_parts.shape[1]
    tr_p = _tile(rows_p, 64)
    sm = _adam_sum("adam_small", packed_parts, pl.BlockSpec((NDEV, tr_p, PACK_COLS), lambda i: (0, i, 0)), *packs, tr_p)
    rows_late = late_parts.shape[1]
    small_out = []
    for t in sm:
        out = dict(zip(SMALL_LATE, _unpack(t[:rows_late], [split[n] for n in SMALL_LATE])))
        out.update(zip(SMALL_EARLY, _unpack(t[rows_late:], [split[n] for n in SMALL_EARLY])))
        out["b_ada"] = jnp.concatenate([out["b_ada_a"], out["b_ada_b"]], axis=1)
        small_out.append(out)

    rows_a, rows_b = (2 * d) // PACK_COLS, (4 * d) // PACK_COLS
    assert rows_a * PACK_COLS == 2 * d
    dmod_all = jnp.concatenate([late_parts[:, :rows_a].reshape(NDEV, 2 * d), early_parts[:, :rows_b].reshape(NDEV, 4 * d)], axis=1)
    dmod_cols = lax.dynamic_slice(dmod_all, (0, me * n_ada), (NDEV, n_ada))
    big["w_ada"] = _adam_ada(c_all, dmod_cols, w_ada[0], m_w_ada[0], v_w_ada[0])

    order = ("w_ada", "b_ada", "g_mix", "w_in", "a_re", "a_im", "log_dt", "b_re", "b_im", "c_re", "c_im", "d_skip", "w_glu",
             "q_gain", "k_gain", "g_ssm_out", "g_attn_out", "w_out", "g_ffn", "w_gate", "w_up", "w_down")
    outs = [loss, grad_x[None]]
    for kind in range(4):
        for n in order:
            outs.append(big[n][kind][None] if n in big else small_out[kind][n])
    return tuple(outs)
```
